```python
import jax, jax.numpy as jnp
from jax import lax
import numpy as np

D_MODEL = 1024
BATCH = 8
SEQ = 4096
DEPTH = 1

CHUNK = 64
Q_BLOCK = 128
HEAD_DIM = 64
N_SB_HEADS = 8
N_FOX_HEADS = 8
SB_WIDTH = N_SB_HEADS * HEAD_DIM
FOX_WIDTH = N_FOX_HEADS * HEAD_DIM
D_FF = -(-(8 * D_MODEL) // (3 * 256)) * 256
N_COND = 6
LN_EPS = 1e-5
DEEPNORM_ALPHA = (2 * DEPTH) ** 0.25
DEEPNORM_BETA = (8 * DEPTH) ** -0.25
OFF_SB = 0
OFF_FOX = OFF_SB + 3 * SB_WIDTH
OFF_FGATE = OFF_FOX + 3 * FOX_WIDTH
OFF_BGATE = OFF_FGATE + N_FOX_HEADS
IN_COLS = OFF_BGATE + 2 * D_MODEL

kernel_name = "hybrid_stickbreak_fox_gated_block"


def layer_norm(x, gain=None, bias=None):
    xf = x.astype(jnp.float32)
    mu = jnp.mean(xf, axis=-1, keepdims=True)
    var = jnp.mean(jnp.square(xf - mu), axis=-1, keepdims=True)
    y = (xf - mu) * lax.rsqrt(var + LN_EPS)
    if gain is not None:
        y = y * gain.astype(jnp.float32) + bias.astype(jnp.float32)
    return y.astype(x.dtype)


def modulate(x, shift, scale):
    return layer_norm(x) * (1.0 + scale[:, None, :]) + shift[:, None, :]


def split_heads(t, n_heads):
    b, s, _ = t.shape
    return t.reshape(b, s, n_heads, HEAD_DIM).transpose(0, 2, 1, 3)


def merge_heads(t):
    b, h, s, d = t.shape
    return t.transpose(0, 2, 1, 3).reshape(b, s, h * d)


def stick_breaking_attention(q, k, v):
    seq = q.shape[2]
    scale = HEAD_DIM ** -0.5
    outs = []
    for i in range(seq // Q_BLOCK):
        q0 = i * Q_BLOCK
        kv_len = q0 + Q_BLOCK
        qb = q[:, :, q0:kv_len]
        kb = k[:, :, :kv_len]
        vb = v[:, :, :kv_len]
        z = jnp.einsum('bhqd,bhkd->bhqk', qb, kb, preferred_element_type=jnp.float32) * scale
        qpos = q0 + jnp.arange(Q_BLOCK)[:, None]
        kpos = jnp.arange(kv_len)[None, :]
        mask = kpos < qpos
        log_1m = jnp.where(mask, jax.nn.log_sigmoid(-z), 0.0)
        suffix = lax.cumsum(log_1m, axis=3, reverse=True) - log_1m
        a = jnp.where(mask, jnp.exp(jax.nn.log_sigmoid(z) + suffix), 0.0)
        outs.append(jnp.einsum('bhqk,bhkd->bhqd', a.astype(v.dtype), vb))
    return jnp.concatenate(outs, axis=2)


def forgetting_attention(q, k, v, f_cum):
    seq = q.shape[2]
    scale = HEAD_DIM ** -0.5
    outs = []
    for i in range(seq // Q_BLOCK):
        q0 = i * Q_BLOCK
        kv_len = q0 + Q_BLOCK
        qb = q[:, :, q0:kv_len]
        kb = k[:, :, :kv_len]
        vb = v[:, :, :kv_len]
        z = (jnp.einsum('bhqd,bhkd->bhqk', qb, kb, preferred_element_type=jnp.float32) * scale
             + f_cum[:, :, q0:kv_len, None] - f_cum[:, :, None, :kv_len])
        qpos = q0 + jnp.arange(Q_BLOCK)[:, None]
        kpos = jnp.arange(kv_len)[None, :]
        p = jax.nn.softmax(jnp.where(kpos <= qpos, z, -jnp.inf), axis=-1)
        outs.append(jnp.einsum('bhqk,bhkd->bhqd', p.astype(v.dtype), vb))
    return jnp.concatenate(outs, axis=2)


def _fwd_setup_inputs(seed: int = 0) -> dict:
    key = jax.random.key(seed)
    ks = jax.random.split(key, 20)
    f32 = jnp.float32
    nrm = lambda k, shape, s: (jax.random.normal(k, shape, f32) * s).astype(f32)
    return {
        "x": nrm(ks[0], (BATCH, SEQ, D_MODEL), 1.0),
        "c": nrm(ks[1], (BATCH, D_MODEL), 1.0),
        "w_ada": nrm(ks[2], (DEPTH, D_MODEL, N_COND * D_MODEL), 0.5 * D_MODEL ** -0.5),
        "b_ada": nrm(ks[3], (DEPTH, N_COND * D_MODEL), 0.02),
        "w_in": nrm(ks[4], (DEPTH, D_MODEL, IN_COLS), D_MODEL ** -0.5),
        "b_gate": nrm(ks[5], (DEPTH, 2 * D_MODEL), 0.02),
        "b_forget": 3.0 + nrm(ks[6], (DEPTH, N_FOX_HEADS), 1.0),
        "w_sb_out": nrm(ks[7], (DEPTH, SB_WIDTH, D_MODEL), SB_WIDTH ** -0.5),
        "w_fox_out": nrm(ks[8], (DEPTH, FOX_WIDTH, D_MODEL), FOX_WIDTH ** -0.5),
        "w_o": nrm(ks[9], (DEPTH, D_MODEL, D_MODEL), DEEPNORM_BETA * D_MODEL ** -0.5),
        "ln1_g": 1.0 + nrm(ks[10], (DEPTH, D_MODEL), 0.02),
        "ln1_b": nrm(ks[11], (DEPTH, D_MODEL), 0.02),
        "w_ffn_gate": nrm(ks[12], (DEPTH, D_MODEL, D_FF), D_MODEL ** -0.5),
        "w_ffn_up": nrm(ks[13], (DEPTH, D_MODEL, D_FF), D_MODEL ** -0.5),
        "w_ffn_down": nrm(ks[14], (DEPTH, D_FF, D_MODEL), DEEPNORM_BETA * D_FF ** -0.5),
        "ln2_g": 1.0 + nrm(ks[15], (DEPTH, D_MODEL), 0.02),
        "ln2_b": nrm(ks[16], (DEPTH, D_MODEL), 0.02),
    }


def _fwd_reference(x, c, w_ada, b_ada, w_in, b_gate, b_forget, w_sb_out, w_fox_out, w_o,
              ln1_g, ln1_b, w_ffn_gate, w_ffn_up, w_ffn_down, ln2_g, ln2_b):
    c_act = jax.nn.silu(c)
    for l in range(DEPTH):
        ada = c_act @ w_ada[l] + b_ada[l]
        sh1, sc1, g1, sh2, sc2, g2 = jnp.split(ada, N_COND, axis=-1)

        u = modulate(x, sh1, sc1)
        proj = u @ w_in[l]
        q_sb = split_heads(proj[..., OFF_SB:OFF_SB + SB_WIDTH], N_SB_HEADS)
        k_sb = split_heads(proj[..., OFF_SB + SB_WIDTH:OFF_SB + 2 * SB_WIDTH], N_SB_HEADS)
        v_sb = split_heads(proj[..., OFF_SB + 2 * SB_WIDTH:OFF_FOX], N_SB_HEADS)
        q_fx = split_heads(proj[..., OFF_FOX:OFF_FOX + FOX_WIDTH], N_FOX_HEADS)
        k_fx = split_heads(proj[..., OFF_FOX + FOX_WIDTH:OFF_FOX + 2 * FOX_WIDTH], N_FOX_HEADS)
        v_fx = split_heads(proj[..., OFF_FOX + 2 * FOX_WIDTH:OFF_FGATE], N_FOX_HEADS)
        f_logit = proj[..., OFF_FGATE:OFF_BGATE].astype(jnp.float32) + b_forget[l].astype(jnp.float32)
        f_cum = jnp.cumsum(jax.nn.log_sigmoid(f_logit), axis=1).transpose(0, 2, 1)
        gate_logit = proj[..., OFF_BGATE:] + b_gate[l]
        g_sb = jax.nn.sigmoid(gate_logit[..., :D_MODEL])
        g_fx = jax.nn.sigmoid(gate_logit[..., D_MODEL:])

        y_sb = merge_heads(stick_breaking_attention(q_sb, k_sb, v_sb)) @ w_sb_out[l]
        y_fx = merge_heads(forgetting_attention(q_fx, k_fx, v_fx, f_cum)) @ w_fox_out[l]
        mix = (g_sb * y_sb + g_fx * y_fx) @ w_o[l]
        x = layer_norm(DEEPNORM_ALPHA * x + g1[:, None, :] * mix, ln1_g[l], ln1_b[l])

        u = modulate(x, sh2, sc2)
        h = (jax.nn.silu(u @ w_ffn_gate[l]) * (u @ w_ffn_up[l])) @ w_ffn_down[l]
        x = layer_norm(DEEPNORM_ALPHA * x + g2[:, None, :] * h, ln2_g[l], ln2_b[l])
    return x


import jax as _jax
import jax.numpy as _jnp

TWIN_FORMAT = 'train_step'
FWD_PARAMS = ['x', 'c', 'w_ada', 'b_ada', 'w_in', 'b_gate', 'b_forget', 'w_sb_out', 'w_fox_out', 'w_o', 'ln1_g', 'ln1_b', 'w_ffn_gate', 'w_ffn_up', 'w_ffn_down', 'ln2_g', 'ln2_b']
TWIN_WEIGHTS = ['w_ada', 'b_ada', 'w_in', 'b_gate', 'b_forget', 'w_sb_out', 'w_fox_out', 'w_o', 'ln1_g', 'ln1_b', 'w_ffn_gate', 'w_ffn_up', 'w_ffn_down', 'ln2_g', 'ln2_b']
TWIN_DIFF_INPUT = 'x'
TWIN_INPUTS = ['x', 'c', 'w_ada', 'b_ada', 'w_in', 'b_gate', 'b_forget', 'w_sb_out', 'w_fox_out', 'w_o', 'ln1_g', 'ln1_b', 'w_ffn_gate', 'w_ffn_up', 'w_ffn_down', 'ln2_g', 'ln2_b', 'loss_target', 'm_w_ada', 'm_b_ada', 'm_w_in', 'm_b_gate', 'm_b_forget', 'm_w_sb_out', 'm_w_fox_out', 'm_w_o', 'm_ln1_g', 'm_ln1_b', 'm_w_ffn_gate', 'm_w_ffn_up', 'm_w_ffn_down', 'm_ln2_g', 'm_ln2_b', 'v_w_ada', 'v_b_ada', 'v_w_in', 'v_b_gate', 'v_b_forget', 'v_w_sb_out', 'v_w_fox_out', 'v_w_o', 'v_ln1_g', 'v_ln1_b', 'v_w_ffn_gate', 'v_w_ffn_up', 'v_w_ffn_down', 'v_ln2_g', 'v_ln2_b']
TWIN_OUTPUTS = ['loss', 'grad_x', 'grad_w_ada', 'grad_b_ada', 'grad_w_in', 'grad_b_gate', 'grad_b_forget', 'grad_w_sb_out', 'grad_w_fox_out', 'grad_w_o', 'grad_ln1_g', 'grad_ln1_b', 'grad_w_ffn_gate', 'grad_w_ffn_up', 'grad_w_ffn_down', 'grad_ln2_g', 'grad_ln2_b', 'delta_w_ada', 'delta_b_ada', 'delta_w_in', 'delta_b_gate', 'delta_b_forget', 'delta_w_sb_out', 'delta_w_fox_out', 'delta_w_o', 'delta_ln1_g', 'delta_ln1_b', 'delta_w_ffn_gate', 'delta_w_ffn_up', 'delta_w_ffn_down', 'delta_ln2_g', 'delta_ln2_b', 'new_m_w_ada', 'new_m_b_ada', 'new_m_w_in', 'new_m_b_gate', 'new_m_b_forget', 'new_m_w_sb_out', 'new_m_w_fox_out', 'new_m_w_o', 'new_m_ln1_g', 'new_m_ln1_b', 'new_m_w_ffn_gate', 'new_m_w_ffn_up', 'new_m_w_ffn_down', 'new_m_ln2_g', 'new_m_ln2_b', 'new_v_w_ada', 'new_v_b_ada', 'new_v_w_in', 'new_v_b_gate', 'new_v_b_forget', 'new_v_w_sb_out', 'new_v_w_fox_out', 'new_v_w_o', 'new_v_ln1_g', 'new_v_ln1_b', 'new_v_w_ffn_gate', 'new_v_w_ffn_up', 'new_v_w_ffn_down', 'new_v_ln2_g', 'new_v_ln2_b']
TWIN_LEAF_KINDS = {'loss': 'loss', 'grad_x': 'grad_x', 'grad_w_ada': 'grad_w', 'grad_b_ada': 'grad_w', 'grad_w_in': 'grad_w', 'grad_b_gate': 'grad_w', 'grad_b_forget': 'grad_w', 'grad_w_sb_out': 'grad_w', 'grad_w_fox_out': 'grad_w', 'grad_w_o': 'grad_w', 'grad_ln1_g': 'grad_w', 'grad_ln1_b': 'grad_w', 'grad_w_ffn_gate': 'grad_w', 'grad_w_ffn_up': 'grad_w', 'grad_w_ffn_down': 'grad_w', 'grad_ln2_g': 'grad_w', 'grad_ln2_b': 'grad_w', 'delta_w_ada': 'delta_w', 'delta_b_ada': 'delta_w', 'delta_w_in': 'delta_w', 'delta_b_gate': 'delta_w', 'delta_b_forget': 'delta_w', 'delta_w_sb_out': 'delta_w', 'delta_w_fox_out': 'delta_w', 'delta_w_o': 'delta_w', 'delta_ln1_g': 'delta_w', 'delta_ln1_b': 'delta_w', 'delta_w_ffn_gate': 'delta_w', 'delta_w_ffn_up': 'delta_w', 'delta_w_ffn_down': 'delta_w', 'delta_ln2_g': 'delta_w', 'delta_ln2_b': 'delta_w', 'new_m_w_ada': 'new_m', 'new_m_b_ada': 'new_m', 'new_m_w_in': 'new_m', 'new_m_b_gate': 'new_m', 'new_m_b_forget': 'new_m', 'new_m_w_sb_out': 'new_m', 'new_m_w_fox_out': 'new_m', 'new_m_w_o': 'new_m', 'new_m_ln1_g': 'new_m', 'new_m_ln1_b': 'new_m', 'new_m_w_ffn_gate': 'new_m', 'new_m_w_ffn_up': 'new_m', 'new_m_w_ffn_down': 'new_m', 'new_m_ln2_g': 'new_m', 'new_m_ln2_b': 'new_m', 'new_v_w_ada': 'new_v', 'new_v_b_ada': 'new_v', 'new_v_w_in': 'new_v', 'new_v_b_gate': 'new_v', 'new_v_b_forget': 'new_v', 'new_v_w_sb_out': 'new_v', 'new_v_w_fox_out': 'new_v', 'new_v_w_o': 'new_v', 'new_v_ln1_g': 'new_v', 'new_v_ln1_b': 'new_v', 'new_v_w_ffn_gate': 'new_v', 'new_v_w_ffn_up': 'new_v', 'new_v_w_ffn_down': 'new_v', 'new_v_ln2_g': 'new_v', 'new_v_ln2_b': 'new_v'}


def _forward(args):
    return _fwd_reference(*[args[k] for k in FWD_PARAMS])


def _output_shape():
    def fwd():
        inp = _fwd_setup_inputs(0)
        return _fwd_reference(*[inp[k] for k in FWD_PARAMS])
    out = _jax.eval_shape(fwd)
    return out.shape, out.dtype

N_MICROBATCH = 1
ADAM_LR = 0.001
ADAM_B1 = 0.9
ADAM_B2 = 0.999
ADAM_EPS = 1e-08
ADAM_WD = 0.01
ADAM_STEP = 10
PER_EXAMPLE_BATCH_AXIS = {'x': 0, 'c': 0, 'loss_target': 0}
SHARED_INPUTS = []
_WEIGHT_DTYPES = {'w_ada': _jnp.float32, 'b_ada': _jnp.float32, 'w_in': _jnp.float32, 'b_gate': _jnp.float32, 'b_forget': _jnp.float32, 'w_sb_out': _jnp.float32, 'w_fox_out': _jnp.float32, 'w_o': _jnp.float32, 'ln1_g': _jnp.float32, 'ln1_b': _jnp.float32, 'w_ffn_gate': _jnp.float32, 'w_ffn_up': _jnp.float32, 'w_ffn_down': _jnp.float32, 'ln2_g': _jnp.float32, 'ln2_b': _jnp.float32}
MOMENT_SCALE = {'w_ada': 2.317231e-02, 'b_ada': 4.123911e-02, 'w_in': 7.967195e-03, 'b_gate': 3.658550e-03, 'b_forget': 2.741871e-02, 'w_sb_out': 1.174083e-02, 'w_fox_out': 7.300575e-03, 'w_o': 2.319884e-02, 'ln1_g': 1.196925e+00, 'ln1_b': 5.682200e-01, 'w_ffn_gate': 1.207636e-02, 'w_ffn_up': 1.175388e-02, 'w_ffn_down': 3.268323e-02, 'ln2_g': 3.200025e+01, 'ln2_b': 8.633620e-01}


def _to_microbatches(a, axis):
    t = _jnp.moveaxis(a, axis, 0)
    t = t.reshape((N_MICROBATCH, t.shape[0] // N_MICROBATCH) + t.shape[1:])
    return _jnp.moveaxis(t, 1, axis + 1)


def setup_inputs(seed: int = 0) -> dict:
    inp = _fwd_setup_inputs(seed)
    key = _jax.random.fold_in(_jax.random.key(seed), 7919)
    shape, _ = _output_shape()
    out = dict(inp)
    out["loss_target"] = _jax.random.normal(_jax.random.fold_in(key, 0), shape, _jnp.float32)
    for i, name in enumerate(TWIN_WEIGHTS):
        w = inp[name].astype(_jnp.float32)
        if MOMENT_SCALE is None:
            s = _jnp.sqrt(_jnp.mean(_jnp.square(w)) + 1e-30)
        else:
            s = MOMENT_SCALE[name]
        km, kv = _jax.random.split(_jax.random.fold_in(key, i + 1))
        out[name] = w
        out["m_" + name] = s * _jax.random.normal(km, w.shape, _jnp.float32)
        out["v_" + name] = (s * s) * _jax.random.uniform(kv, w.shape, _jnp.float32, 0.5, 1.5)
    if N_MICROBATCH > 1:
        for name, axis in PER_EXAMPLE_BATCH_AXIS.items():
            out[name] = _to_microbatches(out[name], axis)
    return {'x': out['x'], 'c': out['c'], 'w_ada': out['w_ada'], 'b_ada': out['b_ada'], 'w_in': out['w_in'], 'b_gate': out['b_gate'], 'b_forget': out['b_forget'], 'w_sb_out': out['w_sb_out'], 'w_fox_out': out['w_fox_out'], 'w_o': out['w_o'], 'ln1_g': out['ln1_g'], 'ln1_b': out['ln1_b'], 'w_ffn_gate': out['w_ffn_gate'], 'w_ffn_up': out['w_ffn_up'], 'w_ffn_down': out['w_ffn_down'], 'ln2_g': out['ln2_g'], 'ln2_b': out['ln2_b'], 'loss_target': out['loss_target'], 'm_w_ada': out['m_w_ada'], 'm_b_ada': out['m_b_ada'], 'm_w_in': out['m_w_in'], 'm_b_gate': out['m_b_gate'], 'm_b_forget': out['m_b_forget'], 'm_w_sb_out': out['m_w_sb_out'], 'm_w_fox_out': out['m_w_fox_out'], 'm_w_o': out['m_w_o'], 'm_ln1_g': out['m_ln1_g'], 'm_ln1_b': out['m_ln1_b'], 'm_w_ffn_gate': out['m_w_ffn_gate'], 'm_w_ffn_up': out['m_w_ffn_up'], 'm_w_ffn_down': out['m_w_ffn_down'], 'm_ln2_g': out['m_ln2_g'], 'm_ln2_b': out['m_ln2_b'], 'v_w_ada': out['v_w_ada'], 'v_b_ada': out['v_b_ada'], 'v_w_in': out['v_w_in'], 'v_b_gate': out['v_b_gate'], 'v_b_forget': out['v_b_forget'], 'v_w_sb_out': out['v_w_sb_out'], 'v_w_fox_out': out['v_w_fox_out'], 'v_w_o': out['v_w_o'], 'v_ln1_g': out['v_ln1_g'], 'v_ln1_b': out['v_ln1_b'], 'v_w_ffn_gate': out['v_w_ffn_gate'], 'v_w_ffn_up': out['v_w_ffn_up'], 'v_w_ffn_down': out['v_w_ffn_down'], 'v_ln2_g': out['v_ln2_g'], 'v_ln2_b': out['v_ln2_b']}


def _loss(weights, diff, rest, loss_target):
    with _jax.named_scope("forward"):
        args = {**rest, TWIN_DIFF_INPUT: diff, **{k: w.astype(_WEIGHT_DTYPES[k]) for k, w in weights.items()}}
        y = _forward(args)
    with _jax.named_scope("loss_head"):
        err = _jnp.square(y.astype(_jnp.float32) - loss_target)
        return 0.5 * _jnp.sum(_jnp.mean(err, axis=-1)) if err.ndim else 0.5 * err


def _adamw(w, g, m, v):
    m = ADAM_B1 * m + (1.0 - ADAM_B1) * g
    v = ADAM_B2 * v + (1.0 - ADAM_B2) * _jnp.square(g)
    m_hat = m / (1.0 - ADAM_B1 ** ADAM_STEP)
    v_hat = v / (1.0 - ADAM_B2 ** ADAM_STEP)
    delta = -ADAM_LR * (m_hat / (_jnp.sqrt(v_hat) + ADAM_EPS) + ADAM_WD * w)
    return delta, m, v


def reference(x, c, w_ada, b_ada, w_in, b_gate, b_forget, w_sb_out, w_fox_out, w_o, ln1_g, ln1_b, w_ffn_gate, w_ffn_up, w_ffn_down, ln2_g, ln2_b, loss_target, m_w_ada, m_b_ada, m_w_in, m_b_gate, m_b_forget, m_w_sb_out, m_w_fox_out, m_w_o, m_ln1_g, m_ln1_b, m_w_ffn_gate, m_w_ffn_up, m_w_ffn_down, m_ln2_g, m_ln2_b, v_w_ada, v_b_ada, v_w_in, v_b_gate, v_b_forget, v_w_sb_out, v_w_fox_out, v_w_o, v_ln1_g, v_ln1_b, v_w_ffn_gate, v_w_ffn_up, v_w_ffn_down, v_ln2_g, v_ln2_b):
    given = dict(x=x, c=c, w_ada=w_ada, b_ada=b_ada, w_in=w_in, b_gate=b_gate, b_forget=b_forget, w_sb_out=w_sb_out, w_fox_out=w_fox_out, w_o=w_o, ln1_g=ln1_g, ln1_b=ln1_b, w_ffn_gate=w_ffn_gate, w_ffn_up=w_ffn_up, w_ffn_down=w_ffn_down, ln2_g=ln2_g, ln2_b=ln2_b, loss_target=loss_target, m_w_ada=m_w_ada, m_b_ada=m_b_ada, m_w_in=m_w_in, m_b_gate=m_b_gate, m_b_forget=m_b_forget, m_w_sb_out=m_w_sb_out, m_w_fox_out=m_w_fox_out, m_w_o=m_w_o, m_ln1_g=m_ln1_g, m_ln1_b=m_ln1_b, m_w_ffn_gate=m_w_ffn_gate, m_w_ffn_up=m_w_ffn_up, m_w_ffn_down=m_w_ffn_down, m_ln2_g=m_ln2_g, m_ln2_b=m_ln2_b, v_w_ada=v_w_ada, v_b_ada=v_b_ada, v_w_in=v_w_in, v_b_gate=v_b_gate, v_b_forget=v_b_forget, v_w_sb_out=v_w_sb_out, v_w_fox_out=v_w_fox_out, v_w_o=v_w_o, v_ln1_g=v_ln1_g, v_ln1_b=v_ln1_b, v_w_ffn_gate=v_w_ffn_gate, v_w_ffn_up=v_w_ffn_up, v_w_ffn_down=v_w_ffn_down, v_ln2_g=v_ln2_g, v_ln2_b=v_ln2_b)
    weights = {n: given[n] for n in TWIN_WEIGHTS}
    shared = {n: given[n] for n in SHARED_INPUTS}
    per_example = {n: given[n] for n in ['x', 'c']}
    grad_fn = _jax.value_and_grad(_loss, argnums=(0, 1))

    def one_microbatch(ex, loss_target):
        ex = dict(ex)
        diff = ex.pop(TWIN_DIFF_INPUT)
        return grad_fn(weights, diff, {**shared, **ex}, loss_target)

    if N_MICROBATCH == 1:
        loss, (grad_w, grad_x) = one_microbatch(per_example, given["loss_target"])
    else:
        def body(carry, xs):
            loss_sum, grad_sum = carry
            l_k, (gw_k, gx_k) = one_microbatch(xs[0], xs[1])
            with _jax.named_scope("update"):
                return (loss_sum + l_k, _jax.tree.map(_jnp.add, grad_sum, gw_k)), gx_k

        init = (_jnp.zeros((), _jnp.float32), _jax.tree.map(_jnp.zeros_like, weights))
        (loss, grad_w), grad_x = _jax.lax.scan(body, init, (per_example, given["loss_target"]))
    with _jax.named_scope("update"):
        delta_w, new_m, new_v = {}, {}, {}
        for n in TWIN_WEIGHTS:
            delta_w[n], new_m[n], new_v[n] = _adamw(weights[n], grad_w[n], given["m_" + n], given["v_" + n])
    return (loss, grad_x, *[grad_w[n] for n in TWIN_WEIGHTS], *[delta_w[n] for n in TWIN_WEIGHTS],
            *[new_m[n] for n in TWIN_WEIGHTS], *[new_v[n] for n in TWIN_WEIGHTS])
```

```python
import functools

import jax
import jax.numpy as jnp
from jax import lax
from jax.experimental import pallas as pl
from jax.experimental.pallas import tpu as pltpu

F32 = jnp.float32
MXU = jnp.bfloat16

D = 1024
HEAD_DIM = 64
WIDTH = 512
D_FF = 2816
N_COND = 6
LN_EPS = 1e-5
ALPHA = 2.0 ** 0.25
QK_SCALE = HEAD_DIM ** -0.5
OFF_FGATE = 6 * WIDTH
N_FGATE = 8
IN_COLS = OFF_FGATE + N_FGATE + 2 * D
LANE = 128
W_ALL_COLS = OFF_FGATE + LANE + 2 * D
TQ = 128
ADAM_LR, ADAM_B1, ADAM_B2, ADAM_EPS, ADAM_WD, ADAM_STEP = 0.001, 0.9, 0.999, 1e-08, 0.01, 10
NEG = -1e30
MESH_AXES = ("x", "y", "c")
VMEM_BIG = 56 * 1024 * 1024


def _dot(a, b):
    return jnp.dot(a, b, preferred_element_type=F32)


def _dot_nt(a, b):
    return lax.dot_general(a, b, (((1,), (1,)), ((), ())), preferred_element_type=F32)


def _dot_tn(a, b):
    return lax.dot_general(a, b, (((0,), (0,)), ((), ())), preferred_element_type=F32)


def _ln(x):
    mu = jnp.mean(x, axis=-1, keepdims=True)
    xc = x - mu
    var = jnp.mean(xc * xc, axis=-1, keepdims=True)
    rstd = lax.rsqrt(var + LN_EPS)
    return xc * rstd, rstd


def _ln_bwd(dxhat, xhat, rstd):
    return rstd * (dxhat - jnp.mean(dxhat, axis=-1, keepdims=True) - xhat * jnp.mean(dxhat * xhat, axis=-1, keepdims=True))


def _sigmoid(x):
    return 1.0 / (1.0 + jnp.exp(-x))


def _colsum(x):
    return jnp.sum(x, axis=0, keepdims=True)


def _split(x):
    hi = x.astype(MXU)
    lo = (x - hi.astype(F32)).astype(MXU)
    return hi, lo


def _rows(tm, n):
    return pl.BlockSpec((tm, n), lambda i: (i, 0))


def _fixed(r, n):
    return pl.BlockSpec((r, n), lambda i: (0, 0))


def _res(a):
    return pl.BlockSpec(a.shape, lambda i: (0, 0), pipeline_mode=pl.Buffered(1))


def _params(limit=None, sem=None):
    return pltpu.CompilerParams(vmem_limit_bytes=limit, dimension_semantics=sem)


def _in_proj(x, sh1, sc1, w_all, b_gate):
    s = x.shape[0]
    tm = 256

    def body(x_ref, sh_ref, sc_ref, w_ref, bg_ref, u_ref, qkv_ref, fl_ref, gl_ref):
        xhat, _ = _ln(x_ref[...])
        u = (xhat * (1.0 + sc_ref[...]) + sh_ref[...]).astype(MXU)
        u_ref[...] = u
        for c0 in range(0, OFF_FGATE, WIDTH):
            p = _dot(u, w_ref[:, c0:c0 + WIDTH])
            if c0 in (0, 3 * WIDTH):
                p = p * QK_SCALE
            qkv_ref[:, c0:c0 + WIDTH] = p.astype(MXU)
        fl_ref[...] = _dot(u, w_ref[:, OFF_FGATE:OFF_FGATE + LANE])
        for c0 in range(0, 2 * D, D):
            gl_ref[:, c0:c0 + D] = _dot(u, w_ref[:, OFF_FGATE + LANE + c0:OFF_FGATE + LANE + c0 + D]) + bg_ref[:, c0:c0 + D]

    return pl.pallas_call(
        body, name="in_proj", grid=(s // tm,),
        in_specs=[_rows(tm, D), _fixed(1, D), _fixed(1, D), _res(w_all), _fixed(1, 2 * D)],
        out_specs=[_rows(tm, D), _rows(tm, OFF_FGATE), _rows(tm, LANE), _rows(tm, 2 * D)],
        out_shape=[jax.ShapeDtypeStruct((s, D), MXU), jax.ShapeDtypeStruct((s, OFF_FGATE), MXU),
                   jax.ShapeDtypeStruct((s, LANE), F32), jax.ShapeDtypeStruct((s, 2 * D), F32)],
        compiler_params=_params(VMEM_BIG),
    )(x, sh1, sc1, w_all, b_gate)


def _log_sigmoid_parts(z):
    e = jnp.exp(-jnp.abs(z))
    return -(jnp.maximum(z, 0.0) + jnp.log1p(e)), e


def _fcum_fwd(fl, bf):
    s = fl.shape[0]
    nb = s // LANE

    def body(fl_ref, bf_ref, fc_ref, fkt_ref):
        r = lax.broadcasted_iota(jnp.int32, (LANE, LANE), 0)
        c = lax.broadcasted_iota(jnp.int32, (LANE, LANE), 1)
        tri = (c <= r).astype(F32)

        def step(b, carry):
            r0 = pl.multiple_of(b * LANE, LANE)
            xb = fl_ref[pl.ds(r0, LANE), :] + bf_ref[...]
            ls = _log_sigmoid_parts(-xb)[0]
            cs = jnp.dot(tri, ls, precision=lax.Precision.HIGHEST, preferred_element_type=F32) + carry
            fc_ref[pl.ds(r0, LANE), :] = cs
            fkt_ref[b] = cs.T[:N_FGATE, :]
            return cs[LANE - 1:LANE, :]

        lax.fori_loop(0, nb, step, jnp.zeros((1, LANE), F32))

    return pl.pallas_call(
        body, name="fcum_fwd",
        out_shape=[jax.ShapeDtypeStruct((s, LANE), F32), jax.ShapeDtypeStruct((nb, N_FGATE, LANE), F32)],
    )(fl, bf)


def _attn_specs(s, col0):
    return [pl.BlockSpec((TQ, LANE), lambda hp, i: (i, col0 + hp)),
            pl.BlockSpec((s, LANE), lambda hp, i: (0, col0 + 4 + hp)),
            pl.BlockSpec((s, LANE), lambda hp, i: (0, col0 + 8 + hp))]


def _tile_iotas():
    lane = lax.broadcasted_iota(jnp.int32, (TQ, LANE), 1)
    row = lax.broadcasted_iota(jnp.int32, (TQ, TQ), 0)
    col = lax.broadcasted_iota(jnp.int32, (TQ, TQ), 1)
    return lane, row, col


def _sb_fwd(qkv):
    s = qkv.shape[0]
    nq = s // TQ
    assert nq <= LANE

    def body(q_ref, k_ref, v_ref, o_ref, rs_ref):
        i = pl.program_id(1)
        lane, row, col = _tile_iotas()
        u2 = jnp.concatenate([(row > col).astype(MXU), jnp.ones((TQ, TQ), MXU)], axis=1)
        diag = col < row
        q = q_ref[...]
        outs = []
        for hh in range(2):
            hm = (lane >= HEAD_DIM) if hh else (lane < HEAD_DIM)
            qm = jnp.where(hm, q, jnp.zeros_like(q))

            def step(kb, carry, masked):
                run, acc, rt = carry
                k0 = pl.multiple_of(kb * TQ, TQ)
                k = k_ref[pl.ds(k0, TQ), :]
                v = v_ref[pl.ds(k0, TQ), :]
                z = _dot_nt(qm, k)
                lneg, _ = _log_sigmoid_parts(z)
                lpos = z + lneg
                if masked:
                    lneg = jnp.where(diag, lneg, 0.0)
                hi, lo = _split(lneg)
                st = _dot(hi, u2) + _dot(lo, u2)
                a = jnp.exp(lpos + st[:, :TQ] + run)
                if masked:
                    a = jnp.where(diag, a, 0.0)
                acc = acc + _dot(a.astype(MXU), v)
                rt = jnp.where(lane == kb, run, rt)
                return run + st[:, TQ:], acc, rt

            zero = jnp.zeros((TQ, LANE), F32)
            carry = step(i, (zero, zero, zero), True)
            carry = lax.fori_loop(0, i, lambda j, cr: step(i - 1 - j, cr, False), carry)
            outs.append(carry[1])
            rs_ref[hh] = carry[2]
        o_ref[...] = jnp.where(lane < HEAD_DIM, outs[0], outs[1]).astype(o_ref.dtype)

    return pl.pallas_call(
        body, name="sb_fwd", grid=(4, nq),
        in_specs=_attn_specs(s, 0),
        out_specs=[pl.BlockSpec((TQ, LANE), lambda hp, i: (i, hp)), pl.BlockSpec((2, TQ, LANE), lambda hp, i: (hp, i, 0))],
        out_shape=[jax.ShapeDtypeStruct((s, WIDTH), MXU), jax.ShapeDtypeStruct((8, s, LANE), F32)],
    )(qkv, qkv, qkv)


def _sb_bwd(qkv, do, rs):
    s = qkv.shape[0]
    nq = s // TQ

    def body(q_ref, k_ref, v_ref, do_ref, rs_ref, dq_ref, dk_ref, dv_ref, dk_acc, dv_acc):
        i = pl.program_id(1)

        @pl.when(i == 0)
        def _():
            dk_acc[...] = jnp.zeros_like(dk_acc)
            dv_acc[...] = jnp.zeros_like(dv_acc)

        lane, row, col = _tile_iotas()
        upper = (row > col).astype(MXU)
        l2 = jnp.concatenate([(row < col).astype(MXU), jnp.ones((TQ, TQ), MXU)], axis=1)
        diag = col < row
        q = q_ref[...]
        do = do_ref[...]
        outs = []
        for hh in range(2):
            hm = (lane >= HEAD_DIM) if hh else (lane < HEAD_DIM)
            qm = jnp.where(hm, q, jnp.zeros_like(q))
            dom = jnp.where(hm, do, jnp.zeros_like(do))
            rblk = rs_ref[hh]

            def step(kb, carry, masked):
                gpre, dq = carry
                k0 = pl.multiple_of(kb * TQ, TQ)
                k = k_ref[pl.ds(k0, TQ), :]
                v = v_ref[pl.ds(k0, TQ), :]
                z = _dot_nt(qm, k)
                lneg, e = _log_sigmoid_parts(z)
                lpos = z + lneg
                if masked:
                    lneg = jnp.where(diag, lneg, 0.0)
                hi, lo = _split(lneg)
                sblk = _dot(hi, upper) + _dot(lo, upper)
                run = jnp.sum(jnp.where(lane == kb, rblk, 0.0), axis=1, keepdims=True)
                a = jnp.exp(lpos + sblk + run)
                if masked:
                    a = jnp.where(diag, a, 0.0)
                g = a * _dot_nt(dom, v)
                ghi, glo = _split(g)
                pt = _dot(ghi, l2) + _dot(glo, l2)
                sig = jnp.where(z >= 0.0, 1.0, e) / (1.0 + e)
                dz = g - (g + gpre + pt[:, :TQ]) * sig
                if masked:
                    dz = jnp.where(diag, dz, 0.0)
                dzb = dz.astype(MXU)
                dk_acc[pl.ds(k0, TQ), :] += _dot_tn(dzb, qm)
                dv_acc[pl.ds(k0, TQ), :] += _dot_tn(a.astype(MXU), dom)
                return gpre + pt[:, TQ:], dq + _dot(dzb, k)

            zero = jnp.zeros((TQ, LANE), F32)
            carry = lax.fori_loop(0, i, lambda kb, cr: step(kb, cr, False), (zero, zero))
            carry = step(i, carry, True)
            outs.append(carry[1])
        dq_ref[...] = (jnp.where(lane < HEAD_DIM, outs[0], outs[1]) * QK_SCALE).astype(dq_ref.dtype)

        @pl.when(i == nq - 1)
        def _():
            dk_ref[...] = dk_acc[...].astype(dk_ref.dtype)
            dv_ref[...] = dv_acc[...].astype(dv_ref.dtype)

    blk = pl.BlockSpec((TQ, LANE), lambda hp, i: (i, hp))
    whole = pl.BlockSpec((s, LANE), lambda hp, i: (0, hp))
    return pl.pallas_call(
        body, name="sb_bwd", grid=(4, nq),
        in_specs=_attn_specs(s, 0) + [blk, pl.BlockSpec((2, TQ, LANE), lambda hp, i: (hp, i, 0))],
        out_specs=[blk, whole, whole],
        out_shape=[jax.ShapeDtypeStruct((s, WIDTH), MXU)] * 3,
        scratch_shapes=[pltpu.VMEM((s, LANE), F32), pltpu.VMEM((s, LANE), F32)],
    )(qkv, qkv, qkv, do, rs)


def _fox_fwd(qkv, fc, fkt):
    s = qkv.shape[0]
    nq = s // TQ
    nb = fkt.shape[0]

    def body(q_ref, k_ref, v_ref, fq_ref, fkt_ref, o_ref, lse_ref):
        hp = pl.program_id(0)
        i = pl.program_id(1)
        lane, row, col = _tile_iotas()
        diag = col <= row
        q = q_ref[...]
        fqb = fq_ref[...]
        outs = []
        for hh in range(2):
            h = 2 * hp + hh
            hm = (lane >= HEAD_DIM) if hh else (lane < HEAD_DIM)
            qm = jnp.where(hm, q, jnp.zeros_like(q))
            fq = jnp.sum(jnp.where(lane == h, fqb, 0.0), axis=1, keepdims=True)

            def step(kb, carry, masked):
                m, l, acc = carry
                k0 = pl.multiple_of(kb * TQ, TQ)
                k = k_ref[pl.ds(k0, TQ), :]
                v = v_ref[pl.ds(k0, TQ), :]
                z = _dot_nt(qm, k) + fq - fkt_ref[kb, pl.ds(h, 1), :]
                if masked:
                    z = jnp.where(diag, z, NEG)
                mn = jnp.maximum(m, jnp.max(z, axis=1, keepdims=True))
                p = jnp.exp(z - mn)
                alpha = jnp.exp(m - mn)
                return mn, alpha * l + jnp.sum(p, axis=1, keepdims=True), alpha * acc + _dot(p.astype(MXU), v)

            init = (jnp.full((TQ, 1), NEG, F32), jnp.zeros((TQ, 1), F32), jnp.zeros((TQ, LANE), F32))
            carry = lax.fori_loop(0, i, lambda kb, cr: step(kb, cr, False), init)
            m, l, acc = step(i, carry, True)
            outs.append(acc / l)
            lse_ref[hh] = jnp.broadcast_to(m + jnp.log(l), (TQ, LANE))
        o_ref[...] = jnp.where(lane < HEAD_DIM, outs[0], outs[1]).astype(o_ref.dtype)

    return pl.pallas_call(
        body, name="fox_fwd", grid=(4, nq),
        in_specs=_attn_specs(s, 12) + [pl.BlockSpec((TQ, LANE), lambda hp, i: (i, 0)),
                                       pl.BlockSpec((nb, N_FGATE, LANE), lambda hp, i: (0, 0, 0))],
        out_specs=[pl.BlockSpec((TQ, LANE), lambda hp, i: (i, hp)), pl.BlockSpec((2, TQ, LANE), lambda hp, i: (hp, i, 0))],
        out_shape=[jax.ShapeDtypeStruct((s, WIDTH), MXU), jax.ShapeDtypeStruct((8, s, LANE), F32)],
    )(qkv, qkv, qkv, fc, fkt)


def _fox_bwd(qkv, fc, fkt, do, o, lse):
    s = qkv.shape[0]
    nq = s // TQ
    nb = fkt.shape[0]

    def body(q_ref, k_ref, v_ref, fq_ref, fkt_ref, do_ref, o_ref, lse_ref, dq_ref, dk_ref, dv_ref, dfk_ref, dfq_ref, dk_acc, dv_acc):
        hp = pl.program_id(0)
        i = pl.program_id(1)

        @pl.when(i == 0)
        def _():
            dk_acc[...] = jnp.zeros_like(dk_acc)
            dv_acc[...] = jnp.zeros_like(dv_acc)

        @pl.when((i == 0) & (hp == 0))
        def _():
            dfk_ref[...] = jnp.zeros_like(dfk_ref)

        lane, row, col = _tile_iotas()
        diag = col <= row
        q = q_ref[...]
        do = do_ref[...]
        dof = do.astype(F32) * o_ref[...].astype(F32)
        fqb = fq_ref[...]
        outs = []
        dfq = jnp.zeros((TQ, LANE), F32)
        for hh in range(2):
            h = 2 * hp + hh
            hm = (lane >= HEAD_DIM) if hh else (lane < HEAD_DIM)
            qm = jnp.where(hm, q, jnp.zeros_like(q))
            dom = jnp.where(hm, do, jnp.zeros_like(do))
            delta = jnp.sum(jnp.where(hm, dof, 0.0), axis=1, keepdims=True)
            fq = jnp.sum(jnp.where(lane == h, fqb, 0.0), axis=1, keepdims=True)
            lse_t = lse_ref[hh]

            def step(kb, carry, masked):
                dq, rsum = carry
                k0 = pl.multiple_of(kb * TQ, TQ)
                k = k_ref[pl.ds(k0, TQ), :]
                v = v_ref[pl.ds(k0, TQ), :]
                z = _dot_nt(qm, k) + fq - fkt_ref[kb, pl.ds(h, 1), :]
                if masked:
                    z = jnp.where(diag, z, NEG)
                p = jnp.exp(z - lse_t)
                ds = p * (_dot_nt(dom, v) - delta)
                dsb = ds.astype(MXU)
                dk_acc[pl.ds(k0, TQ), :] += _dot_tn(dsb, qm)
                dv_acc[pl.ds(k0, TQ), :] += _dot_tn(p.astype(MXU), dom)
                dfk_ref[kb, pl.ds(h, 1), :] += -_colsum(ds)
                return dq + _dot(dsb, k), rsum + jnp.sum(ds, axis=1, keepdims=True)

            carry = lax.fori_loop(0, i, lambda kb, cr: step(kb, cr, False), (jnp.zeros((TQ, LANE), F32), jnp.zeros((TQ, 1), F32)))
            dq, rsum = step(i, carry, True)
            outs.append(dq)
            dfq = jnp.where(lane == h, rsum, dfq)
        dq_ref[...] = (jnp.where(lane < HEAD_DIM, outs[0], outs[1]) * QK_SCALE).astype(dq_ref.dtype)
        dfq_ref[0] = dfq

        @pl.when(i == nq - 1)
        def _():
            dk_ref[...] = dk_acc[...].astype(dk_ref.dtype)
            dv_ref[...] = dv_acc[...].astype(dv_ref.dtype)

    blk = pl.BlockSpec((TQ, LANE), lambda hp, i: (i, hp))
    whole = pl.BlockSpec((s, LANE), lambda hp, i: (0, hp))
    pair = pl.BlockSpec((2, TQ, LANE), lambda hp, i: (hp, i, 0))
    fkt_spec = pl.BlockSpec((nb, N_FGATE, LANE), lambda hp, i: (0, 0, 0))
    return pl.pallas_call(
        body, name="fox_bwd", grid=(4, nq),
        in_specs=_attn_specs(s, 12) + [pl.BlockSpec((TQ, LANE), lambda hp, i: (i, 0)), fkt_spec, blk, blk, pair],
        out_specs=[blk, whole, whole, fkt_spec, pl.BlockSpec((1, TQ, LANE), lambda hp, i: (hp, i, 0))],
        out_shape=[jax.ShapeDtypeStruct((s, WIDTH), MXU)] * 3
        + [jax.ShapeDtypeStruct((nb, N_FGATE, LANE), F32), jax.ShapeDtypeStruct((4, s, LANE), F32)],
        scratch_shapes=[pltpu.VMEM((s, LANE), F32), pltpu.VMEM((s, LANE), F32)],
    )(qkv, qkv, qkv, fc, fkt, do, o, lse)


def _fcum_bwd(dfkt, dfq, fl, bf):
    s = fl.shape[0]
    nb = s // LANE

    def body(dfkt_ref, dfq_ref, fl_ref, bf_ref, df_ref, dbf_ref, tail_ref):
        @pl.when(pl.program_id(0) == 0)
        def _():
            tail_ref[...] = jnp.zeros_like(tail_ref)
            dbf_ref[...] = jnp.zeros_like(dbf_ref)

        r = lax.broadcasted_iota(jnp.int32, (LANE, LANE), 0)
        c = lax.broadcasted_iota(jnp.int32, (LANE, LANE), 1)
        tri = (c >= r).astype(F32)
        dfc = jnp.concatenate([dfkt_ref[0], jnp.zeros((LANE - N_FGATE, LANE), F32)], axis=0).T
        dfc = dfc + ((dfq_ref[0] + dfq_ref[1]) + (dfq_ref[2] + dfq_ref[3]))
        dls = jnp.dot(tri, dfc, precision=lax.Precision.HIGHEST, preferred_element_type=F32) + tail_ref[...]
        xb = fl_ref[...] + bf_ref[...]
        e = jnp.exp(-jnp.abs(xb))
        dfl = dls * (jnp.where(xb >= 0.0, e, 1.0) / (1.0 + e))
        df_ref[...] = dfl.astype(df_ref.dtype)
        tail_ref[...] = dls[0:1, :]
        dbf_ref[...] += _colsum(dfl)

    return pl.pallas_call(
        body, name="fcum_bwd", grid=(nb,),
        in_specs=[pl.BlockSpec((1, N_FGATE, LANE), lambda j: (nb - 1 - j, 0, 0)), pl.BlockSpec((4, LANE, LANE), lambda j: (0, nb - 1 - j, 0)),
                  pl.BlockSpec((LANE, LANE), lambda j: (nb - 1 - j, 0)), _fixed(1, LANE)],
        out_specs=[pl.BlockSpec((LANE, LANE), lambda j: (nb - 1 - j, 0)), _fixed(1, LANE)],
        out_shape=[jax.ShapeDtypeStruct((s, LANE), MXU), jax.ShapeDtypeStruct((1, LANE), F32)],
        scratch_shapes=[pltpu.VMEM((1, LANE), F32)],
    )(dfkt, dfq, fl, bf)


def _mix_fwd(x, o_sb, o_fx, gl, w_sb, w_fx, w_o, g1, ln1_g, ln1_b, sh2, sc2):
    s = x.shape[0]
    tm = 256

    def body(x_ref, osb_ref, ofx_ref, gl_ref, wsb_ref, wfx_ref, wo_ref, g1_ref, lg_ref, lb_ref, sh_ref, sc_ref, r1_ref, u2_ref):
        mixin = (_sigmoid(gl_ref[:, :D]) * _dot(osb_ref[...], wsb_ref[...])
                 + _sigmoid(gl_ref[:, D:]) * _dot(ofx_ref[...], wfx_ref[...]))
        r1 = ALPHA * x_ref[...] + g1_ref[...] * _dot(mixin.astype(MXU), wo_ref[...])
        r1_ref[...] = r1
        x1 = _ln(r1)[0] * lg_ref[...] + lb_ref[...]
        u2_ref[...] = (_ln(x1)[0] * (1.0 + sc_ref[...]) + sh_ref[...]).astype(MXU)

    vec = _fixed(1, D)
    return pl.pallas_call(
        body, name="mix_fwd", grid=(s // tm,),
        in_specs=[_rows(tm, D), _rows(tm, WIDTH), _rows(tm, WIDTH), _rows(tm, 2 * D), _res(w_sb), _res(w_fx), _res(w_o),
                  vec, vec, vec, vec, vec],
        out_specs=[_rows(tm, D), _rows(tm, D)],
        out_shape=[jax.ShapeDtypeStruct((s, D), F32), jax.ShapeDtypeStruct((s, D), MXU)],
        compiler_params=_params(VMEM_BIG),
    )(x, o_sb, o_fx, gl, w_sb, w_fx, w_o, g1, ln1_g, ln1_b, sh2, sc2)


def _ffn_fwd(r1, u2, tgt, w_g, w_u, w_d, g2, ln1_g, ln1_b, ln2_g, ln2_b):
    s = r1.shape[0]
    tm = 128

    def body(r1_ref, u2_ref, t_ref, wg_ref, wu_ref, wd_ref, g2_ref, l1g_ref, l1b_ref, l2g_ref, l2b_ref,
             hg_ref, hu_ref, dxa_ref, dh_ref, acc_ref):
        @pl.when(pl.program_id(0) == 0)
        def _():
            acc_ref[...] = jnp.zeros_like(acc_ref)

        u2 = u2_ref[...]
        hg = _dot(u2, wg_ref[...])
        hu = _dot(u2, wu_ref[...])
        hg_ref[...] = hg
        hu_ref[...] = hu
        h = _dot((hg * _sigmoid(hg) * hu).astype(MXU), wd_ref[...])
        x1 = _ln(r1_ref[...])[0] * l1g_ref[...] + l1b_ref[...]
        xh2, rstd2 = _ln(ALPHA * x1 + g2_ref[...] * h)
        err = xh2 * l2g_ref[...] + l2b_ref[...] - t_ref[...]
        dy = err * (1.0 / D)
        dr2 = _ln_bwd(dy * l2g_ref[...], xh2, rstd2)
        dxa_ref[...] = ALPHA * dr2
        dh_ref[...] = (g2_ref[...] * dr2).astype(MXU)
        acc_ref[0:1, :] += _colsum(dr2 * h)
        acc_ref[1:2, :] += _colsum(dy * xh2)
        acc_ref[2:3, :] += _colsum(dy)
        acc_ref[3:4, :] += _colsum(err * err) * (0.5 / D)

    vec = _fixed(1, D)
    return pl.pallas_call(
        body, name="ffn_fwd", grid=(s // tm,),
        in_specs=[_rows(tm, D), _rows(tm, D), _rows(tm, D), _res(w_g), _res(w_u), _res(w_d), vec, vec, vec, vec, vec],
        out_specs=[_rows(tm, D_FF), _rows(tm, D_FF), _rows(tm, D), _rows(tm, D), _fixed(8, D)],
        out_shape=[jax.ShapeDtypeStruct((s, D_FF), F32), jax.ShapeDtypeStruct((s, D_FF), F32),
                   jax.ShapeDtypeStruct((s, D), F32), jax.ShapeDtypeStruct((s, D), MXU), jax.ShapeDtypeStruct((8, D), F32)],
        compiler_params=_params(VMEM_BIG),
    )(r1, u2, tgt, w_g, w_u, w_d, g2, ln1_g, ln1_b, ln2_g, ln2_b)


def _ffn_bwd(dh, hg, hu, w_g, w_u, w_d):
    s = dh.shape[0]
    tm = 128

    def body(dh_ref, hg_ref, hu_ref, wg_ref, wu_ref, wd_ref, act_ref, dhg_ref, dhu_ref, du2_ref):
        dact = _dot_nt(dh_ref[...], wd_ref[...])
        hg = hg_ref[...]
        hu = hu_ref[...]
        sg = _sigmoid(hg)
        sl = hg * sg
        act_ref[...] = (sl * hu).astype(MXU)
        dhg = (dact * hu * (sg * (1.0 + hg * (1.0 - sg)))).astype(MXU)
        dhu = (dact * sl).astype(MXU)
        dhg_ref[...] = dhg
        dhu_ref[...] = dhu
        du2_ref[...] = _dot_nt(dhg, wg_ref[...]) + _dot_nt(dhu, wu_ref[...])

    return pl.pallas_call(
        body, name="ffn_bwd", grid=(s // tm,),
        in_specs=[_rows(tm, D), _rows(tm, D_FF), _rows(tm, D_FF), _res(w_g), _res(w_u), _res(w_d)],
        out_specs=[_rows(tm, D_FF), _rows(tm, D_FF), _rows(tm, D_FF), _rows(tm, D)],
        out_shape=[jax.ShapeDtypeStruct((s, D_FF), MXU)] * 3 + [jax.ShapeDtypeStruct((s, D), F32)],
        compiler_params=_params(VMEM_BIG),
    )(dh, hg, hu, w_g, w_u, w_d)


def _mix_bwd(du2, dxa, r1, o_sb, o_fx, gl, w_sb, w_fx, w_o, g1, ln1_g, ln1_b, sc2):
    s = r1.shape[0]
    tm = 256

    def body(du2_ref, dxa_ref, r1_ref, osb_ref, ofx_ref, gl_ref, wsb_ref, wfx_ref, wo_ref, g1_ref, lg_ref, lb_ref, sc_ref,
             dx_ref, mixin_ref, dmix_ref, dysb_ref, dyfx_ref, dosb_ref, dofx_ref, dgl_ref, dbg_ref, acc_ref):
        @pl.when(pl.program_id(0) == 0)
        def _():
            acc_ref[...] = jnp.zeros_like(acc_ref)
            dbg_ref[...] = jnp.zeros_like(dbg_ref)

        du2 = du2_ref[...]
        xh1, rstd1 = _ln(r1_ref[...])
        x1 = xh1 * lg_ref[...] + lb_ref[...]
        n1, rstdn = _ln(x1)
        dx1 = dxa_ref[...] + _ln_bwd(du2 * (1.0 + sc_ref[...]), n1, rstdn)
        dr1 = _ln_bwd(dx1 * lg_ref[...], xh1, rstd1)
        dx_ref[...] = ALPHA * dr1
        ysb = _dot(osb_ref[...], wsb_ref[...])
        yfx = _dot(ofx_ref[...], wfx_ref[...])
        gs = _sigmoid(gl_ref[:, :D])
        gf = _sigmoid(gl_ref[:, D:])
        mixin = (gs * ysb + gf * yfx).astype(MXU)
        mixin_ref[...] = mixin
        mix = _dot(mixin, wo_ref[...])
        dmix = (g1_ref[...] * dr1).astype(MXU)
        dmix_ref[...] = dmix
        dmixin = _dot_nt(dmix, wo_ref[...])
        dysb = (dmixin * gs).astype(MXU)
        dyfx = (dmixin * gf).astype(MXU)
        dysb_ref[...] = dysb
        dyfx_ref[...] = dyfx
        dosb_ref[...] = _dot_nt(dysb, wsb_ref[...]).astype(MXU)
        dofx_ref[...] = _dot_nt(dyfx, wfx_ref[...]).astype(MXU)
        dgs = dmixin * ysb * gs * (1.0 - gs)
        dgf = dmixin * yfx * gf * (1.0 - gf)
        dgl_ref[:, :D] = dgs.astype(MXU)
        dgl_ref[:, D:] = dgf.astype(MXU)
        dbg_ref[:, :D] += _colsum(dgs)
        dbg_ref[:, D:] += _colsum(dgf)
        acc_ref[0:1, :] += _colsum(du2)
        acc_ref[1:2, :] += _colsum(du2 * n1)
        acc_ref[2:3, :] += _colsum(dx1 * xh1)
        acc_ref[3:4, :] += _colsum(dx1)
        acc_ref[4:5, :] += _colsum(dr1 * mix)

    vec = _fixed(1, D)
    return pl.pallas_call(
        body, name="mix_bwd", grid=(s // tm,),
        in_specs=[_rows(tm, D), _rows(tm, D), _rows(tm, D), _rows(tm, WIDTH), _rows(tm, WIDTH), _rows(tm, 2 * D),
                  _res(w_sb), _res(w_fx), _res(w_o), vec, vec, vec, vec],
        out_specs=[_rows(tm, D), _rows(tm, D), _rows(tm, D), _rows(tm, D), _rows(tm, D), _rows(tm, WIDTH), _rows(tm, WIDTH),
                   _rows(tm, 2 * D), _fixed(1, 2 * D), _fixed(8, D)],
        out_shape=[jax.ShapeDtypeStruct((s, D), F32)] + [jax.ShapeDtypeStruct((s, D), MXU)] * 4
        + [jax.ShapeDtypeStruct((s, WIDTH), MXU)] * 2
        + [jax.ShapeDtypeStruct((s, 2 * D), MXU), jax.ShapeDtypeStruct((1, 2 * D), F32), jax.ShapeDtypeStruct((8, D), F32)],
        compiler_params=_params(VMEM_BIG),
    )(du2, dxa, r1, o_sb, o_fx, gl, w_sb, w_fx, w_o, g1, ln1_g, ln1_b, sc2)


def _in_bwd(pieces, x, dxa, w_all, sc1):
    s = x.shape[0]
    tm = 256
    n_p = len(pieces)

    def body(*refs):
        p_refs = refs[:n_p]
        x_ref, dxa_ref, w_ref, sc_ref, gx_ref, acc_ref = refs[n_p:]

        @pl.when(pl.program_id(0) == 0)
        def _():
            acc_ref[...] = jnp.zeros_like(acc_ref)

        du1 = jnp.zeros((tm, D), F32)
        for p_ref, (arr, c0) in zip(p_refs, pieces):
            du1 = du1 + _dot_nt(p_ref[...], w_ref[:, c0:c0 + arr.shape[1]])
        n0, rstd0 = _ln(x_ref[...])
        gx_ref[...] = dxa_ref[...] + _ln_bwd(du1 * (1.0 + sc_ref[...]), n0, rstd0)
        acc_ref[0:1, :] += _colsum(du1)
        acc_ref[1:2, :] += _colsum(du1 * n0)

    return pl.pallas_call(
        body, name="in_bwd", grid=(s // tm,),
        in_specs=[_rows(tm, a.shape[1]) for a, _ in pieces] + [_rows(tm, D), _rows(tm, D), _res(w_all), _fixed(1, D)],
        out_specs=[_rows(tm, D), _fixed(8, D)],
        out_shape=[jax.ShapeDtypeStruct((s, D), F32), jax.ShapeDtypeStruct((8, D), F32)],
        compiler_params=_params(VMEM_BIG),
    )(*[a for a, _ in pieces], x, dxa, w_all, sc1)


def _matmul_tn(a, b, name):
    s, m = a.shape
    n = b.shape[1]
    tm = 512 if m % 512 == 0 else m // 2
    tn = n // 2 if n > 2048 else n
    ts = 512
    assert m % tm == 0 and tm % LANE == 0 and n % tn == 0 and tn % LANE == 0 and s % ts == 0

    def body(a_ref, b_ref, o_ref):
        @pl.when(pl.program_id(2) == 0)
        def _():
            o_ref[...] = jnp.zeros_like(o_ref)

        o_ref[...] += _dot_tn(a_ref[...], b_ref[...])

    return pl.pallas_call(
        body, name=name, grid=(m // tm, n // tn, s // ts),
        in_specs=[pl.BlockSpec((ts, tm), lambda i, j, k: (k, i)), pl.BlockSpec((ts, tn), lambda i, j, k: (k, j))],
        out_specs=pl.BlockSpec((tm, tn), lambda i, j, k: (i, j)),
        out_shape=jax.ShapeDtypeStruct((m, n), F32),
        compiler_params=_params(VMEM_BIG),
    )(a, b)


def _local_step(x, tgt, ada, w_all, b_gate, bf_pad, w_sb, w_fx, w_o, ln1_g, ln1_b, w_g, w_u, w_d, ln2_g, ln2_b):
    sh1, sc1, g1, sh2, sc2, g2 = ada
    u1, qkv, fl, gl = _in_proj(x, sh1, sc1, w_all, b_gate)
    fc, fkt = _fcum_fwd(fl, bf_pad)
    o_sb, rs = _sb_fwd(qkv)
    o_fx, lse = _fox_fwd(qkv, fc, fkt)
    r1, u2 = _mix_fwd(x, o_sb, o_fx, gl, w_sb, w_fx, w_o, g1, ln1_g, ln1_b, sh2, sc2)
    hg, hu, dxa2, dh, acc_f = _ffn_fwd(r1, u2, tgt, w_g, w_u, w_d, g2, ln1_g, ln1_b, ln2_g, ln2_b)
    act, dhg, dhu, du2 = _ffn_bwd(dh, hg, hu, w_g, w_u, w_d)
    dxa1, mixin, dmix, dysb, dyfx, dosb, dofx, dgl, dbg, acc_m = _mix_bwd(
        du2, dxa2, r1, o_sb, o_fx, gl, w_sb, w_fx, w_o, g1, ln1_g, ln1_b, sc2)
    dq_sb, dk_sb, dv_sb = _sb_bwd(qkv, dosb, rs)
    dq_fx, dk_fx, dv_fx, dfkt, dfq = _fox_bwd(qkv, fc, fkt, dofx, o_fx, lse)
    df, dbf = _fcum_bwd(dfkt, dfq, fl, bf_pad)
    pieces = [(dq_sb, 0), (dk_sb, WIDTH), (dv_sb, 2 * WIDTH), (dq_fx, 3 * WIDTH), (dk_fx, 4 * WIDTH), (dv_fx, 5 * WIDTH),
              (df, OFF_FGATE), (dgl, OFF_FGATE + LANE)]
    grad_x, acc_i = _in_bwd(pieces, x, dxa1, w_all, sc1)
    dw_in = [_matmul_tn(u1, p, f"dw_in_{j}") for j, (p, _) in enumerate(pieces)]
    return dict(
        loss_lanes=acc_f[3:4], grad_x=grad_x, dw_in=dw_in,
        dw_sb=_matmul_tn(o_sb, dysb, "dw_sb_out"), dw_fx=_matmul_tn(o_fx, dyfx, "dw_fox_out"),
        dw_o=_matmul_tn(mixin, dmix, "dw_o"),
        dw_g=_matmul_tn(u2, dhg, "dw_ffn_gate"), dw_u=_matmul_tn(u2, dhu, "dw_ffn_up"), dw_d=_matmul_tn(act, dh, "dw_ffn_down"),
        d_ada=[acc_i[0:1], acc_i[1:2], acc_m[4:5], acc_m[0:1], acc_m[1:2], acc_f[0:1]],
        dln1_g=acc_m[2:3], dln1_b=acc_m[3:4], dln2_g=acc_f[1:2], dln2_b=acc_f[2:3], db_gate=dbg, db_forget=dbf)


_MESH_ID = pl.DeviceIdType.MESH
_ANY = pl.BlockSpec(memory_space=pl.ANY)
_VMEM = pl.BlockSpec(memory_space=pltpu.VMEM)


def _mesh_pos():
    return lax.axis_index("x"), lax.axis_index("y"), lax.axis_index("c")


def _other_chips(x, y):
    return [(1 - x, y), (x, 1 - y), (1 - x, 1 - y)]


def _allgather_rows(v, name):
    n = v.shape[1]

    def body(v_ref, out_ref, send_sems, recv_sems, local_sem):
        x, y, c = _mesh_pos()
        me = 4 * x + 2 * y + c
        mine = pltpu.make_async_copy(v_ref, out_ref.at[me], local_sem)
        mine.start()
        copies = []
        for d in range(1, 8):
            fx, fy, fc = (d >> 2) & 1, (d >> 1) & 1, d & 1
            to = (1 - x if fx else x, 1 - y if fy else y, 1 - c if fc else c)
            cp = pltpu.make_async_remote_copy(src_ref=v_ref, dst_ref=out_ref.at[me], send_sem=send_sems.at[d - 1],
                                              recv_sem=recv_sems.at[d - 1], device_id=to, device_id_type=_MESH_ID)
            cp.start()
            copies.append(cp)
        for cp in copies:
            cp.wait_recv()
        for cp in copies:
            cp.wait_send()
        mine.wait()

    return pl.pallas_call(
        body, name=name, in_specs=[_VMEM], out_specs=_VMEM,
        out_shape=jax.ShapeDtypeStruct((8, 1, n), v.dtype),
        scratch_shapes=[pltpu.SemaphoreType.DMA((7,)), pltpu.SemaphoreType.DMA((7,)), pltpu.SemaphoreType.DMA(())],
    )(v)


def _chip_exchange(arrays, name, gather):
    nt = len(arrays)

    def body(*refs):
        ins, outs = refs[:nt], refs[nt:2 * nt]
        send_sems, recv_sems, local_sems = refs[2 * nt:]
        x, y, c = _mesh_pos()
        me = 2 * x + y
        copies = []
        for t in range(nt):
            src = ins[t] if gather else ins[t].at[me]
            cp = pltpu.make_async_copy(src, outs[t].at[me], local_sems.at[t])
            cp.start()
            copies.append(cp)
        remote = []
        for t in range(nt):
            for j, (px, py) in enumerate(_other_chips(x, y)):
                src = ins[t] if gather else ins[t].at[2 * px + py]
                cp = pltpu.make_async_remote_copy(src_ref=src, dst_ref=outs[t].at[me], send_sem=send_sems.at[3 * t + j],
                                                  recv_sem=recv_sems.at[3 * t + j], device_id=(px, py, c), device_id_type=_MESH_ID)
                cp.start()
                remote.append(cp)
        for cp in remote:
            cp.wait_recv()
        for cp in remote:
            cp.wait_send()
        for cp in copies:
            cp.wait()

    out_shape = [jax.ShapeDtypeStruct((4,) + a.shape[-2:], a.dtype) for a in arrays]
    return pl.pallas_call(
        body, name=name, in_specs=[_ANY] * nt, out_specs=[_ANY] * nt, out_shape=out_shape,
        scratch_shapes=[pltpu.SemaphoreType.DMA((3 * nt,)), pltpu.SemaphoreType.DMA((3 * nt,)), pltpu.SemaphoreType.DMA((nt,))],
    )(*arrays)


def _sibling_exchange(arrays, name):
    nt = len(arrays)

    def body(*refs):
        ins, outs = refs[:nt], refs[nt:2 * nt]
        send_sems, recv_sems = refs[2 * nt:]
        x, y, c = _mesh_pos()
        copies = []
        for t in range(nt):
            cp = pltpu.make_async_remote_copy(src_ref=ins[t], dst_ref=outs[t], send_sem=send_sems.at[t], recv_sem=recv_sems.at[t],
                                              device_id=(x, y, 1 - c), device_id_type=_MESH_ID)
            cp.start()
            copies.append(cp)
        for cp in copies:
            cp.wait_recv()
        for cp in copies:
            cp.wait_send()

    return pl.pallas_call(
        body, name=name, in_specs=[_ANY] * nt, out_specs=[_ANY] * nt,
        out_shape=[jax.ShapeDtypeStruct(a.shape, a.dtype) for a in arrays],
        scratch_shapes=[pltpu.SemaphoreType.DMA((nt,)), pltpu.SemaphoreType.DMA((nt,))],
    )(*arrays)


def _row_tile(r):
    for tr in (256, 352, 128):
        if r % tr == 0:
            return tr
    return r


def _reduce_chips(own, recv, name):
    r, n = own.shape
    tr = _row_tile(r)

    def body(own_ref, recv_ref, out_ref):
        x, y, _ = _mesh_pos()
        me = 2 * x + y
        total = jnp.zeros((tr, n), F32)
        for k in range(4):
            total = total + jnp.where(me == k, own_ref[...], recv_ref[k].astype(F32))
        out_ref[...] = total

    return pl.pallas_call(
        body, name=name, grid=(r // tr,),
        in_specs=[_rows(tr, n), pl.BlockSpec((4, tr, n), lambda i: (0, i, 0))], out_specs=_rows(tr, n),
        out_shape=jax.ShapeDtypeStruct((r, n), F32),
    )(own, recv)


def _adamw_math(w, g, m, v):
    m = ADAM_B1 * m + (1.0 - ADAM_B1) * g
    v = ADAM_B2 * v + (1.0 - ADAM_B2) * (g * g)
    m_hat = m / (1.0 - ADAM_B1 ** ADAM_STEP)
    v_hat = v / (1.0 - ADAM_B2 ** ADAM_STEP)
    return -ADAM_LR * (m_hat / (jnp.sqrt(v_hat) + ADAM_EPS) + ADAM_WD * w), m, v


def _adamw(w, m, v, g_parts, name):
    r, n = w.shape
    tr = _row_tile(r)
    ng = len(g_parts)

    def body(*refs):
        w_ref, m_ref, v_ref = refs[:3]
        g_refs = refs[3:3 + ng]
        g_out, d_out, m_out, v_out = refs[3 + ng:]
        g = g_refs[0][...]
        for gr in g_refs[1:]:
            g = g + gr[...]
        g_out[...] = g
        d_out[...], m_out[...], v_out[...] = _adamw_math(w_ref[...], g, m_ref[...], v_ref[...])

    return pl.pallas_call(
        body, name=name, grid=(r // tr,),
        in_specs=[_rows(tr, n)] * (3 + ng), out_specs=[_rows(tr, n)] * 4,
        out_shape=[jax.ShapeDtypeStruct((r, n), F32)] * 4,
    )(w, m, v, *g_parts)


def _ada_fwd(c_all, w_shard, b_shard):
    n = w_shard.shape[1]
    tn = 512

    def body(c_ref, w_ref, b_ref, o_ref):
        cv = c_ref[...]
        ca = (cv * _sigmoid(cv)).astype(MXU)
        o_ref[...] = _dot(ca, w_ref[...].astype(MXU)) + b_ref[...]

    return pl.pallas_call(
        body, name="ada_fwd", grid=(n // tn,),
        in_specs=[_fixed(8, D), pl.BlockSpec((D, tn), lambda j: (0, j)), pl.BlockSpec((1, tn), lambda j: (0, j))],
        out_specs=pl.BlockSpec((8, tn), lambda j: (0, j)),
        out_shape=jax.ShapeDtypeStruct((8, n), F32),
    )(c_all, w_shard, b_shard)


def _ada_bwd(c_all, dada_shard):
    n = dada_shard.shape[1]
    tn = 512

    def body(c_ref, d_ref, o_ref):
        cv = c_ref[...]
        ca = (cv * _sigmoid(cv)).astype(MXU)
        o_ref[...] = _dot_tn(ca, d_ref[...].astype(MXU))

    return pl.pallas_call(
        body, name="ada_bwd", grid=(n // tn,),
        in_specs=[_fixed(8, D), pl.BlockSpec((8, tn), lambda j: (0, j))],
        out_specs=pl.BlockSpec((D, tn), lambda j: (0, j)),
        out_shape=jax.ShapeDtypeStruct((D, n), F32),
    )(c_all, dada_shard)


_SMALL = [("d_ada", N_COND * D), ("ln1_g", D), ("ln1_b", D), ("ln2_g", D), ("ln2_b", D), ("b_gate", 2 * D), ("b_forget", LANE),
          ("loss", D)]
_SMALL_OFF = {}
_o = 0
for _n, _w in _SMALL:
    _SMALL_OFF[_n] = (_o, _w)
    _o += _w
_SMALL_LEN = _o
_SMALL_PARAMS = [("b_ada", "d_ada", N_COND * D), ("b_gate", "b_gate", 2 * D), ("b_forget", "b_forget", N_FGATE),
                 ("ln1_g", "ln1_g", D), ("ln1_b", "ln1_b", D), ("ln2_g", "ln2_g", D), ("ln2_b", "ln2_b", D)]


def _small_update(rows, params):
    npar = len(_SMALL_PARAMS)

    def body(*refs):
        rows_ref = refs[0]
        p_refs = refs[1:1 + 3 * npar]
        loss_ref = refs[1 + 3 * npar]
        o_refs = refs[2 + 3 * npar:]
        total = rows_ref[0]
        for d in range(1, 8):
            total = total + rows_ref[d]
        lo, lw = _SMALL_OFF["loss"]
        loss_ref[...] = jnp.sum(total[:, lo:lo + lw], axis=1, keepdims=True)
        for j, (_, key, n) in enumerate(_SMALL_PARAMS):
            off = _SMALL_OFF[key][0]
            g = total[:, off:off + n]
            w_ref, m_ref, v_ref = p_refs[3 * j:3 * j + 3]
            o_refs[4 * j][...] = g
            o_refs[4 * j + 1][...], o_refs[4 * j + 2][...], o_refs[4 * j + 3][...] = _adamw_math(w_ref[...], g, m_ref[...], v_ref[...])

    flat = [a for p in params for a in p]
    out_shape = [jax.ShapeDtypeStruct((1, 1), F32)] + [jax.ShapeDtypeStruct((1, n), F32) for _, _, n in _SMALL_PARAMS for _ in range(4)]
    return pl.pallas_call(body, name="small_update", out_shape=out_shape)(rows, *flat)


_BIG = [("w_in", True), ("w_sb_out", True), ("w_fox_out", True), ("w_o", False),
        ("w_ffn_gate", True), ("w_ffn_up", True), ("w_ffn_down", False)]


def _unshard(g, by_cols):
    if by_cols:
        return g.transpose(1, 0, 2).reshape(g.shape[1], 4 * g.shape[2])
    return g.reshape(4 * g.shape[1], g.shape[2])


def _reshard(w, by_cols):
    if by_cols:
        return w.reshape(w.shape[0], 4, w.shape[1] // 4).transpose(1, 0, 2)
    return w.reshape(4, w.shape[0] // 4, w.shape[1])


def kernel(x, c, w_ada, b_ada, w_in, b_gate, b_forget, w_sb_out, w_fox_out, w_o, ln1_g, ln1_b, w_ffn_gate, w_ffn_up, w_ffn_down, ln2_g, ln2_b, loss_target, m_w_ada, m_b_ada, m_w_in, m_b_gate, m_b_forget, m_w_sb_out, m_w_fox_out, m_w_o, m_ln1_g, m_ln1_b, m_w_ffn_gate, m_w_ffn_up, m_w_ffn_down, m_ln2_g, m_ln2_b, v_w_ada, v_b_ada, v_w_in, v_b_gate, v_b_forget, v_w_sb_out, v_w_fox_out, v_w_o, v_ln1_g, v_ln1_b, v_w_ffn_gate, v_w_ffn_up, v_w_ffn_down, v_ln2_g, v_ln2_b):
    given = dict(locals())
    mx, my, mc = _mesh_pos()
    chip = 2 * mx + my
    seq = 4 * mx + 2 * my + mc

    c_all = _allgather_rows(c, "gather_c").reshape(8, D)
    n_ada = w_ada.shape[2]
    b_ada_shard = lax.dynamic_slice(b_ada, (0, chip * n_ada), (1, n_ada))
    ada_part = _ada_fwd(c_all, w_ada[0], b_ada_shard)
    ada_all = _allgather_rows(ada_part.reshape(1, 8 * n_ada), "gather_ada").reshape(4, 2, 8, n_ada)
    ada_row = lax.dynamic_slice(ada_all, (0, mc, seq, 0), (4, 1, 1, n_ada)).reshape(1, N_COND * D)
    ada = [ada_row[:, j * D:(j + 1) * D] for j in range(N_COND)]

    gathered = _chip_exchange([given[n][0].astype(MXU) for n, _ in _BIG], "gather_weights", gather=True)
    full = {n: _unshard(g, by_cols) for (n, by_cols), g in zip(_BIG, gathered)}
    wi = full["w_in"]
    w_all = jnp.concatenate([wi[:, :OFF_FGATE + N_FGATE], jnp.zeros((D, LANE - N_FGATE), MXU), wi[:, OFF_FGATE + N_FGATE:]], axis=1)
    bf_pad = jnp.concatenate([b_forget, jnp.zeros((1, LANE - N_FGATE), F32)], axis=1)

    out = _local_step(x[0], loss_target[0], ada, w_all, b_gate, bf_pad, full["w_sb_out"], full["w_fox_out"], full["w_o"],
                      ln1_g, ln1_b, full["w_ffn_gate"], full["w_ffn_up"], full["w_ffn_down"], ln2_g, ln2_b)

    row = jnp.concatenate(out["d_ada"] + [out["dln1_g"], out["dln1_b"], out["dln2_g"], out["dln2_b"], out["db_gate"],
                                          out["db_forget"], out["loss_lanes"]], axis=1)
    rows = _allgather_rows(row, "gather_small")
    small = _small_update(rows, [(given[p], given["m_" + p], given["v_" + p]) for p, _, _ in _SMALL_PARAMS])
    loss = small[0].reshape(())
    res = {}
    for j, (p, _, _) in enumerate(_SMALL_PARAMS):
        res[p] = small[1 + 4 * j:5 + 4 * j]

    dada_all = rows.reshape(8, _SMALL_LEN)[:, :N_COND * D]
    dada_shard = lax.dynamic_slice(dada_all, (0, chip * n_ada), (8, n_ada))
    g_ada = _ada_bwd(c_all, dada_shard)
    res["w_ada"] = [a[None] for a in _adamw(w_ada[0], m_w_ada[0], v_w_ada[0], [g_ada], "adamw_w_ada")]

    dwi = out["dw_in"]
    dw_full = {"w_in": jnp.concatenate(dwi[:6] + [dwi[6][:, :N_FGATE], dwi[7]], axis=1), "w_sb_out": out["dw_sb"],
               "w_fox_out": out["dw_fx"], "w_o": out["dw_o"], "w_ffn_gate": out["dw_g"], "w_ffn_up": out["dw_u"],
               "w_ffn_down": out["dw_d"]}
    pieces = [_reshard(dw_full[n], by_cols) for n, by_cols in _BIG]
    received = _chip_exchange([p.astype(MXU) for p in pieces], "scatter_grads", gather=False)
    partial = [_reduce_chips(lax.dynamic_index_in_dim(p, chip, 0, keepdims=False), r, "reduce_" + n)
               for (n, _), p, r in zip(_BIG, pieces, received)]
    theirs = _sibling_exchange(partial, "swap_cores")
    for (n, _), mine, other in zip(_BIG, partial, theirs):
        res[n] = [a[None] for a in _adamw(given[n][0], given["m_" + n][0], given["v_" + n][0], [mine, other], "adamw_" + n)]

    order = ["w_ada", "b_ada", "w_in", "b_gate", "b_forget", "w_sb_out", "w_fox_out", "w_o", "ln1_g", "ln1_b",
             "w_ffn_gate", "w_ffn_up", "w_ffn_down", "ln2_g", "ln2_b"]
    return (loss, out["grad_x"][None], *[res[n][0] for n in order], *[res[n][1] for n in order],
            *[res[n][2] for n in order], *[res[n][3] for n in order])
```

```python
import functools

import jax
import jax.numpy as jnp
from jax import lax
from jax.experimental import pallas as pl
from jax.experimental.pallas import tpu as pltpu

F32 = jnp.float32
MXU = jnp.bfloat16

D = 1024
HEAD_DIM = 64
WIDTH = 512
D_FF = 2816
N_COND = 6
LN_EPS = 1e-5
ALPHA = 2.0 ** 0.25
QK_SCALE = HEAD_DIM ** -0.5
OFF_FGATE = 6 * WIDTH
N_FGATE = 8
IN_COLS = OFF_FGATE + N_FGATE + 2 * D
LANE = 128
W_ALL_COLS = OFF_FGATE + LANE + 2 * D
TQ = 512
ADAM_LR, ADAM_B1, ADAM_B2, ADAM_EPS, ADAM_WD, ADAM_STEP = 0.001, 0.9, 0.999, 1e-08, 0.01, 10
NEG = -1e30
MESH_AXES = ("x", "y", "c")
VMEM_BIG = 56 * 1024 * 1024


def _dot(a, b):
    return jnp.dot(a, b, preferred_element_type=F32)


def _dot_nt(a, b):
    return lax.dot_general(a, b, (((1,), (1,)), ((), ())), preferred_element_type=F32)


def _dot_tn(a, b):
    return lax.dot_general(a, b, (((0,), (0,)), ((), ())), preferred_element_type=F32)


def _ln(x):
    mu = jnp.mean(x, axis=-1, keepdims=True)
    xc = x - mu
    var = jnp.mean(xc * xc, axis=-1, keepdims=True)
    rstd = lax.rsqrt(var + LN_EPS)
    return xc * rstd, rstd


def _ln_bwd(dxhat, xhat, rstd):
    return rstd * (dxhat - jnp.mean(dxhat, axis=-1, keepdims=True) - xhat * jnp.mean(dxhat * xhat, axis=-1, keepdims=True))


def _sigmoid(x):
    return 1.0 / (1.0 + jnp.exp(-x))


def _colsum(x):
    return jnp.sum(x, axis=0, keepdims=True)


def _split(x):
    hi = x.astype(MXU)
    lo = (x - hi.astype(F32)).astype(MXU)
    return hi, lo


def _rows(tm, n):
    return pl.BlockSpec((tm, n), lambda i: (i, 0))


def _fixed(r, n):
    return pl.BlockSpec((r, n), lambda i: (0, 0))


def _res(a):
    return pl.BlockSpec(a.shape, lambda i: (0, 0), pipeline_mode=pl.Buffered(1))


def _params(limit=None, sem=None):
    return pltpu.CompilerParams(vmem_limit_bytes=limit, dimension_semantics=sem)


def _in_proj(x, sh1, sc1, w_all, b_gate):
    s = x.shape[0]
    tm = 256

    def body(x_ref, sh_ref, sc_ref, w_ref, bg_ref, u_ref, qkv_ref, fl_ref, gl_ref):
        xhat, _ = _ln(x_ref[...])
        u = (xhat * (1.0 + sc_ref[...]) + sh_ref[...]).astype(MXU)
        u_ref[...] = u
        for c0 in range(0, OFF_FGATE, WIDTH):
            p = _dot(u, w_ref[:, c0:c0 + WIDTH])
            if c0 in (0, 3 * WIDTH):
                p = p * QK_SCALE
            qkv_ref[:, c0:c0 + WIDTH] = p.astype(MXU)
        fl_ref[...] = _dot(u, w_ref[:, OFF_FGATE:OFF_FGATE + LANE])
        for c0 in range(0, 2 * D, D):
            gl_ref[:, c0:c0 + D] = _dot(u, w_ref[:, OFF_FGATE + LANE + c0:OFF_FGATE + LANE + c0 + D]) + bg_ref[:, c0:c0 + D]

    return pl.pallas_call(
        body, name="in_proj", grid=(s // tm,),
        in_specs=[_rows(tm, D), _fixed(1, D), _fixed(1, D), _res(w_all), _fixed(1, 2 * D)],
        out_specs=[_rows(tm, D), _rows(tm, OFF_FGATE), _rows(tm, LANE), _rows(tm, 2 * D)],
        out_shape=[jax.ShapeDtypeStruct((s, D), MXU), jax.ShapeDtypeStruct((s, OFF_FGATE), MXU),
                   jax.ShapeDtypeStruct((s, LANE), F32), jax.ShapeDtypeStruct((s, 2 * D), F32)],
        compiler_params=_params(VMEM_BIG),
    )(x, sh1, sc1, w_all, b_gate)


def _log_sigmoid_parts(z):
    e = jnp.exp(-jnp.abs(z))
    return -(jnp.maximum(z, 0.0) + jnp.log1p(e)), e


def _fcum_fwd(fl, bf):
    s = fl.shape[0]
    nb = s // LANE

    def body(fl_ref, bf_ref, fc_ref, fkt_ref):
        r = lax.broadcasted_iota(jnp.int32, (LANE, LANE), 0)
        c = lax.broadcasted_iota(jnp.int32, (LANE, LANE), 1)
        tri = (c <= r).astype(F32)

        def step(b, carry):
            r0 = pl.multiple_of(b * LANE, LANE)
            xb = fl_ref[pl.ds(r0, LANE), :] + bf_ref[...]
            ls = _log_sigmoid_parts(-xb)[0]
            cs = jnp.dot(tri, ls, precision=lax.Precision.HIGHEST, preferred_element_type=F32) + carry
            fc_ref[pl.ds(r0, LANE), :] = cs
            fkt_ref[b] = cs.T[:N_FGATE, :]
            return cs[LANE - 1:LANE, :]

        lax.fori_loop(0, nb, step, jnp.zeros((1, LANE), F32))

    return pl.pallas_call(
        body, name="fcum_fwd",
        out_shape=[jax.ShapeDtypeStruct((s, LANE), F32), jax.ShapeDtypeStruct((nb, N_FGATE, LANE), F32)],
    )(fl, bf)


def _attn_specs(s, col0):
    return [pl.BlockSpec((TQ, LANE), lambda hp, i: (i, col0 + hp)),
            pl.BlockSpec((s, LANE), lambda hp, i: (0, col0 + 4 + hp)),
            pl.BlockSpec((s, LANE), lambda hp, i: (0, col0 + 8 + hp))]


def _tile_iotas():
    lane = lax.broadcasted_iota(jnp.int32, (TQ, LANE), 1)
    row = lax.broadcasted_iota(jnp.int32, (TQ, TQ), 0)
    col = lax.broadcasted_iota(jnp.int32, (TQ, TQ), 1)
    return lane, row, col


def _sub_blocks():
    return [slice(j * LANE, (j + 1) * LANE) for j in range(TQ // LANE)]


def _tri(below):
    r = lax.broadcasted_iota(jnp.int32, (LANE, LANE), 0)
    c = lax.broadcasted_iota(jnp.int32, (LANE, LANE), 1)
    return jnp.concatenate([((r > c) if below else (r < c)).astype(MXU), jnp.ones((LANE, LANE), MXU)], axis=1)


def _sb_fwd(qkv):
    s = qkv.shape[0]
    nq = s // TQ
    assert nq <= LANE

    def body(q_ref, k_ref, v_ref, o_ref, rs_ref):
        i = pl.program_id(1)
        lane, row, col = _tile_iotas()
        u2 = _tri(True)
        diag = col < row
        q = q_ref[...]
        outs = []
        for hh in range(2):
            hm = (lane >= HEAD_DIM) if hh else (lane < HEAD_DIM)
            qm = jnp.where(hm, q, jnp.zeros_like(q))

            def step(kb, carry, masked):
                run, acc, rt = carry
                k0 = pl.multiple_of(kb * TQ, TQ)
                k = k_ref[pl.ds(k0, TQ), :]
                v = v_ref[pl.ds(k0, TQ), :]
                z = _dot_nt(qm, k)
                lneg, _ = _log_sigmoid_parts(z)
                lpos = z + lneg
                if masked:
                    lneg = jnp.where(diag, lneg, 0.0)
                rt = jnp.where(lane == kb, run, rt)
                a = []
                for sl in reversed(_sub_blocks()):
                    hi, lo = _split(lneg[:, sl])
                    st = _dot(hi, u2) + _dot(lo, u2)
                    a.append(jnp.exp(lpos[:, sl] + st[:, :LANE] + run))
                    run = run + st[:, LANE:]
                a = jnp.concatenate(a[::-1], axis=1)
                if masked:
                    a = jnp.where(diag, a, 0.0)
                return run, acc + _dot(a.astype(MXU), v), rt

            zero = jnp.zeros((TQ, LANE), F32)
            carry = step(i, (zero, zero, zero), True)
            carry = lax.fori_loop(0, i, lambda j, cr: step(i - 1 - j, cr, False), carry)
            outs.append(carry[1])
            rs_ref[hh] = carry[2]
        o_ref[...] = jnp.where(lane < HEAD_DIM, outs[0], outs[1]).astype(o_ref.dtype)

    return pl.pallas_call(
        body, name="sb_fwd", grid=(4, nq),
        in_specs=_attn_specs(s, 0),
        out_specs=[pl.BlockSpec((TQ, LANE), lambda hp, i: (i, hp)), pl.BlockSpec((2, TQ, LANE), lambda hp, i: (hp, i, 0))],
        out_shape=[jax.ShapeDtypeStruct((s, WIDTH), MXU), jax.ShapeDtypeStruct((8, s, LANE), F32)],
    )(qkv, qkv, qkv)


def _sb_bwd(qkv, do, rs):
    s = qkv.shape[0]
    nq = s // TQ

    def body(q_ref, k_ref, v_ref, do_ref, rs_ref, dq_ref, dk_ref, dv_ref, dk_acc, dv_acc):
        i = pl.program_id(1)

        @pl.when(i == 0)
        def _():
            dk_acc[...] = jnp.zeros_like(dk_acc)
            dv_acc[...] = jnp.zeros_like(dv_acc)

        lane, row, col = _tile_iotas()
        u2 = _tri(True)
        l2 = _tri(False)
        diag = col < row
        q = q_ref[...]
        do = do_ref[...]
        outs = []
        for hh in range(2):
            hm = (lane >= HEAD_DIM) if hh else (lane < HEAD_DIM)
            qm = jnp.where(hm, q, jnp.zeros_like(q))
            dom = jnp.where(hm, do, jnp.zeros_like(do))
            rblk = rs_ref[hh]

            def step(kb, carry, masked):
                gpre, dq = carry
                k0 = pl.multiple_of(kb * TQ, TQ)
                k = k_ref[pl.ds(k0, TQ), :]
                v = v_ref[pl.ds(k0, TQ), :]
                z = _dot_nt(qm, k)
                lneg, e = _log_sigmoid_parts(z)
                lpos = z + lneg
                if masked:
                    lneg = jnp.where(diag, lneg, 0.0)
                run = jnp.sum(jnp.where(lane == kb, rblk, 0.0), axis=1, keepdims=True) + jnp.zeros((TQ, LANE), F32)
                a = []
                for sl in reversed(_sub_blocks()):
                    hi, lo = _split(lneg[:, sl])
                    st = _dot(hi, u2) + _dot(lo, u2)
                    a.append(jnp.exp(lpos[:, sl] + st[:, :LANE] + run))
                    run = run + st[:, LANE:]
                a = jnp.concatenate(a[::-1], axis=1)
                if masked:
                    a = jnp.where(diag, a, 0.0)
                g = a * _dot_nt(dom, v)
                pre = []
                for sl in _sub_blocks():
                    ghi, glo = _split(g[:, sl])
                    pt = _dot(ghi, l2) + _dot(glo, l2)
                    pre.append(gpre + pt[:, :LANE])
                    gpre = gpre + pt[:, LANE:]
                sig = jnp.where(z >= 0.0, 1.0, e) / (1.0 + e)
                dz = g - (g + jnp.concatenate(pre, axis=1)) * sig
                if masked:
                    dz = jnp.where(diag, dz, 0.0)
                dzb = dz.astype(MXU)
                dk_acc[pl.ds(k0, TQ), :] += _dot_tn(dzb, qm)
                dv_acc[pl.ds(k0, TQ), :] += _dot_tn(a.astype(MXU), dom)
                return gpre, dq + _dot(dzb, k)

            zero = jnp.zeros((TQ, LANE), F32)
            carry = lax.fori_loop(0, i, lambda kb, cr: step(kb, cr, False), (zero, zero))
            carry = step(i, carry, True)
            outs.append(carry[1])
        dq_ref[...] = (jnp.where(lane < HEAD_DIM, outs[0], outs[1]) * QK_SCALE).astype(dq_ref.dtype)

        @pl.when(i == nq - 1)
        def _():
            dk_ref[...] = dk_acc[...].astype(dk_ref.dtype)
            dv_ref[...] = dv_acc[...].astype(dv_ref.dtype)

    blk = pl.BlockSpec((TQ, LANE), lambda hp, i: (i, hp))
    whole = pl.BlockSpec((s, LANE), lambda hp, i: (0, hp))
    return pl.pallas_call(
        body, name="sb_bwd", grid=(4, nq),
        in_specs=_attn_specs(s, 0) + [blk, pl.BlockSpec((2, TQ, LANE), lambda hp, i: (hp, i, 0))],
        out_specs=[blk, whole, whole],
        out_shape=[jax.ShapeDtypeStruct((s, WIDTH), MXU)] * 3,
        scratch_shapes=[pltpu.VMEM((s, LANE), F32), pltpu.VMEM((s, LANE), F32)],
    )(qkv, qkv, qkv, do, rs)


def _key_bias(fkt_ref, kb, h):
    n_sub = TQ // LANE
    return jnp.concatenate([fkt_ref[kb * n_sub + j, pl.ds(h, 1), :] for j in range(n_sub)], axis=1)


def _fox_fwd(qkv, fc, fkt):
    s = qkv.shape[0]
    nq = s // TQ
    nb = fkt.shape[0]

    def body(q_ref, k_ref, v_ref, fq_ref, fkt_ref, o_ref, lse_ref):
        hp = pl.program_id(0)
        i = pl.program_id(1)
        lane, row, col = _tile_iotas()
        diag = col <= row
        q = q_ref[...]
        fqb = fq_ref[...]
        outs = []
        for hh in range(2):
            h = 2 * hp + hh
            hm = (lane >= HEAD_DIM) if hh else (lane < HEAD_DIM)
            qm = jnp.where(hm, q, jnp.zeros_like(q))
            fq = jnp.sum(jnp.where(lane == h, fqb, 0.0), axis=1, keepdims=True)

            def step(kb, carry, masked):
                m, l, acc = carry
                k0 = pl.multiple_of(kb * TQ, TQ)
                k = k_ref[pl.ds(k0, TQ), :]
                v = v_ref[pl.ds(k0, TQ), :]
                z = _dot_nt(qm, k) + fq - _key_bias(fkt_ref, kb, h)
                if masked:
                    z = jnp.where(diag, z, NEG)
                mn = jnp.maximum(m, jnp.max(z, axis=1, keepdims=True))
                p = jnp.exp(z - mn)
                alpha = jnp.exp(m - mn)
                return mn, alpha * l + jnp.sum(p, axis=1, keepdims=True), alpha * acc + _dot(p.astype(MXU), v)

            init = (jnp.full((TQ, 1), NEG, F32), jnp.zeros((TQ, 1), F32), jnp.zeros((TQ, LANE), F32))
            carry = lax.fori_loop(0, i, lambda kb, cr: step(kb, cr, False), init)
            m, l, acc = step(i, carry, True)
            outs.append(acc / l)
            lse_ref[hh] = jnp.broadcast_to(m + jnp.log(l), (TQ, LANE))
        o_ref[...] = jnp.where(lane < HEAD_DIM, outs[0], outs[1]).astype(o_ref.dtype)

    return pl.pallas_call(
        body, name="fox_fwd", grid=(4, nq),
        in_specs=_attn_specs(s, 12) + [pl.BlockSpec((TQ, LANE), lambda hp, i: (i, 0)),
                                       pl.BlockSpec((nb, N_FGATE, LANE), lambda hp, i: (0, 0, 0))],
        out_specs=[pl.BlockSpec((TQ, LANE), lambda hp, i: (i, hp)), pl.BlockSpec((2, TQ, LANE), lambda hp, i: (hp, i, 0))],
        out_shape=[jax.ShapeDtypeStruct((s, WIDTH), MXU), jax.ShapeDtypeStruct((8, s, LANE), F32)],
    )(qkv, qkv, qkv, fc, fkt)


def _fox_bwd(qkv, fc, fkt, do, o, lse):
    s = qkv.shape[0]
    nq = s // TQ
    nb = fkt.shape[0]

    def body(q_ref, k_ref, v_ref, fq_ref, fkt_ref, do_ref, o_ref, lse_ref, dq_ref, dk_ref, dv_ref, dfk_ref, dfq_ref, dk_acc, dv_acc):
        hp = pl.program_id(0)
        i = pl.program_id(1)

        @pl.when(i == 0)
        def _():
            dk_acc[...] = jnp.zeros_like(dk_acc)
            dv_acc[...] = jnp.zeros_like(dv_acc)

        @pl.when((i == 0) & (hp == 0))
        def _():
            dfk_ref[...] = jnp.zeros_like(dfk_ref)

        lane, row, col = _tile_iotas()
        diag = col <= row
        q = q_ref[...]
        do = do_ref[...]
        dof = do.astype(F32) * o_ref[...].astype(F32)
        fqb = fq_ref[...]
        outs = []
        dfq = jnp.zeros((TQ, LANE), F32)
        for hh in range(2):
            h = 2 * hp + hh
            hm = (lane >= HEAD_DIM) if hh else (lane < HEAD_DIM)
            qm = jnp.where(hm, q, jnp.zeros_like(q))
            dom = jnp.where(hm, do, jnp.zeros_like(do))
            delta = jnp.sum(jnp.where(hm, dof, 0.0), axis=1, keepdims=True)
            fq = jnp.sum(jnp.where(lane == h, fqb, 0.0), axis=1, keepdims=True)
            lse_t = lse_ref[hh][:, :1]

            def step(kb, carry, masked):
                dq, rsum = carry
                k0 = pl.multiple_of(kb * TQ, TQ)
                k = k_ref[pl.ds(k0, TQ), :]
                v = v_ref[pl.ds(k0, TQ), :]
                z = _dot_nt(qm, k) + fq - _key_bias(fkt_ref, kb, h)
                if masked:
                    z = jnp.where(diag, z, NEG)
                p = jnp.exp(z - lse_t)
                ds = p * (_dot_nt(dom, v) - delta)
                dsb = ds.astype(MXU)
                dk_acc[pl.ds(k0, TQ), :] += _dot_tn(dsb, qm)
                dv_acc[pl.ds(k0, TQ), :] += _dot_tn(p.astype(MXU), dom)
                csum = _colsum(ds)
                for j, sl in enumerate(_sub_blocks()):
                    dfk_ref[kb * len(_sub_blocks()) + j, pl.ds(h, 1), :] += -csum[:, sl]
                return dq + _dot(dsb, k), rsum + jnp.sum(ds, axis=1, keepdims=True)

            carry = lax.fori_loop(0, i, lambda kb, cr: step(kb, cr, False), (jnp.zeros((TQ, LANE), F32), jnp.zeros((TQ, 1), F32)))
            dq, rsum = step(i, carry, True)
            outs.append(dq)
            dfq = jnp.where(lane == h, rsum, dfq)
        dq_ref[...] = (jnp.where(lane < HEAD_DIM, outs[0], outs[1]) * QK_SCALE).astype(dq_ref.dtype)
        dfq_ref[0] = dfq

        @pl.when(i == nq - 1)
        def _():
            dk_ref[...] = dk_acc[...].astype(dk_ref.dtype)
            dv_ref[...] = dv_acc[...].astype(dv_ref.dtype)

    blk = pl.BlockSpec((TQ, LANE), lambda hp, i: (i, hp))
    whole = pl.BlockSpec((s, LANE), lambda hp, i: (0, hp))
    pair = pl.BlockSpec((2, TQ, LANE), lambda hp, i: (hp, i, 0))
    fkt_spec = pl.BlockSpec((nb, N_FGATE, LANE), lambda hp, i: (0, 0, 0))
    return pl.pallas_call(
        body, name="fox_bwd", grid=(4, nq),
        in_specs=_attn_specs(s, 12) + [pl.BlockSpec((TQ, LANE), lambda hp, i: (i, 0)), fkt_spec, blk, blk, pair],
        out_specs=[blk, whole, whole, fkt_spec, pl.BlockSpec((1, TQ, LANE), lambda hp, i: (hp, i, 0))],
        out_shape=[jax.ShapeDtypeStruct((s, WIDTH), MXU)] * 3
        + [jax.ShapeDtypeStruct((nb, N_FGATE, LANE), F32), jax.ShapeDtypeStruct((4, s, LANE), F32)],
        scratch_shapes=[pltpu.VMEM((s, LANE), F32), pltpu.VMEM((s, LANE), F32)],
    )(qkv, qkv, qkv, fc, fkt, do, o, lse)


def _fcum_bwd(dfkt, dfq, fl, bf):
    s = fl.shape[0]
    nb = s // LANE

    def body(dfkt_ref, dfq_ref, fl_ref, bf_ref, df_ref, dbf_ref, tail_ref):
        @pl.when(pl.program_id(0) == 0)
        def _():
            tail_ref[...] = jnp.zeros_like(tail_ref)
            dbf_ref[...] = jnp.zeros_like(dbf_ref)

        r = lax.broadcasted_iota(jnp.int32, (LANE, LANE), 0)
        c = lax.broadcasted_iota(jnp.int32, (LANE, LANE), 1)
        tri = (c >= r).astype(F32)
        dfc = jnp.concatenate([dfkt_ref[0], jnp.zeros((LANE - N_FGATE, LANE), F32)], axis=0).T
        dfc = dfc + ((dfq_ref[0] + dfq_ref[1]) + (dfq_ref[2] + dfq_ref[3]))
        dls = jnp.dot(tri, dfc, precision=lax.Precision.HIGHEST, preferred_element_type=F32) + tail_ref[...]
        xb = fl_ref[...] + bf_ref[...]
        e = jnp.exp(-jnp.abs(xb))
        dfl = dls * (jnp.where(xb >= 0.0, e, 1.0) / (1.0 + e))
        df_ref[...] = dfl.astype(df_ref.dtype)
        tail_ref[...] = dls[0:1, :]
        dbf_ref[...] += _colsum(dfl)

    return pl.pallas_call(
        body, name="fcum_bwd", grid=(nb,),
        in_specs=[pl.BlockSpec((1, N_FGATE, LANE), lambda j: (nb - 1 - j, 0, 0)), pl.BlockSpec((4, LANE, LANE), lambda j: (0, nb - 1 - j, 0)),
                  pl.BlockSpec((LANE, LANE), lambda j: (nb - 1 - j, 0)), _fixed(1, LANE)],
        out_specs=[pl.BlockSpec((LANE, LANE), lambda j: (nb - 1 - j, 0)), _fixed(1, LANE)],
        out_shape=[jax.ShapeDtypeStruct((s, LANE), MXU), jax.ShapeDtypeStruct((1, LANE), F32)],
        scratch_shapes=[pltpu.VMEM((1, LANE), F32)],
    )(dfkt, dfq, fl, bf)


def _mix_fwd(x, o_sb, o_fx, gl, w_sb, w_fx, w_o, g1, ln1_g, ln1_b, sh2, sc2):
    s = x.shape[0]
    tm = 256

    def body(x_ref, osb_ref, ofx_ref, gl_ref, wsb_ref, wfx_ref, wo_ref, g1_ref, lg_ref, lb_ref, sh_ref, sc_ref, r1_ref, u2_ref):
        mixin = (_sigmoid(gl_ref[:, :D]) * _dot(osb_ref[...], wsb_ref[...])
                 + _sigmoid(gl_ref[:, D:]) * _dot(ofx_ref[...], wfx_ref[...]))
        r1 = ALPHA * x_ref[...] + g1_ref[...] * _dot(mixin.astype(MXU), wo_ref[...])
        r1_ref[...] = r1
        x1 = _ln(r1)[0] * lg_ref[...] + lb_ref[...]
        u2_ref[...] = (_ln(x1)[0] * (1.0 + sc_ref[...]) + sh_ref[...]).astype(MXU)

    vec = _fixed(1, D)
    return pl.pallas_call(
        body, name="mix_fwd", grid=(s // tm,),
        in_specs=[_rows(tm, D), _rows(tm, WIDTH), _rows(tm, WIDTH), _rows(tm, 2 * D), _res(w_sb), _res(w_fx), _res(w_o),
                  vec, vec, vec, vec, vec],
        out_specs=[_rows(tm, D), _rows(tm, D)],
        out_shape=[jax.ShapeDtypeStruct((s, D), F32), jax.ShapeDtypeStruct((s, D), MXU)],
        compiler_params=_params(VMEM_BIG),
    )(x, o_sb, o_fx, gl, w_sb, w_fx, w_o, g1, ln1_g, ln1_b, sh2, sc2)


def _ffn_fwd(r1, u2, tgt, w_g, w_u, w_d, g2, ln1_g, ln1_b, ln2_g, ln2_b):
    s = r1.shape[0]
    tm = 128

    def body(r1_ref, u2_ref, t_ref, wg_ref, wu_ref, wd_ref, g2_ref, l1g_ref, l1b_ref, l2g_ref, l2b_ref,
             hg_ref, hu_ref, dxa_ref, dh_ref, acc_ref):
        @pl.when(pl.program_id(0) == 0)
        def _():
            acc_ref[...] = jnp.zeros_like(acc_ref)

        u2 = u2_ref[...]
        hg = _dot(u2, wg_ref[...])
        hu = _dot(u2, wu_ref[...])
        hg_ref[...] = hg
        hu_ref[...] = hu
        h = _dot((hg * _sigmoid(hg) * hu).astype(MXU), wd_ref[...])
        x1 = _ln(r1_ref[...])[0] * l1g_ref[...] + l1b_ref[...]
        xh2, rstd2 = _ln(ALPHA * x1 + g2_ref[...] * h)
        err = xh2 * l2g_ref[...] + l2b_ref[...] - t_ref[...]
        dy = err * (1.0 / D)
        dr2 = _ln_bwd(dy * l2g_ref[...], xh2, rstd2)
        dxa_ref[...] = ALPHA * dr2
        dh_ref[...] = (g2_ref[...] * dr2).astype(MXU)
        acc_ref[0:1, :] += _colsum(dr2 * h)
        acc_ref[1:2, :] += _colsum(dy * xh2)
        acc_ref[2:3, :] += _colsum(dy)
        acc_ref[3:4, :] += _colsum(err * err) * (0.5 / D)

    vec = _fixed(1, D)
    return pl.pallas_call(
        body, name="ffn_fwd", grid=(s // tm,),
        in_specs=[_rows(tm, D), _rows(tm, D), _rows(tm, D), _res(w_g), _res(w_u), _res(w_d), vec, vec, vec, vec, vec],
        out_specs=[_rows(tm, D_FF), _rows(tm, D_FF), _rows(tm, D), _rows(tm, D), _fixed(8, D)],
        out_shape=[jax.ShapeDtypeStruct((s, D_FF), F32), jax.ShapeDtypeStruct((s, D_FF), F32),
                   jax.ShapeDtypeStruct((s, D), F32), jax.ShapeDtypeStruct((s, D), MXU), jax.ShapeDtypeStruct((8, D), F32)],
        compiler_params=_params(VMEM_BIG),
    )(r1, u2, tgt, w_g, w_u, w_d, g2, ln1_g, ln1_b, ln2_g, ln2_b)


def _ffn_bwd(dh, hg, hu, w_g, w_u, w_d):
    s = dh.shape[0]
    tm = 128

    def body(dh_ref, hg_ref, hu_ref, wg_ref, wu_ref, wd_ref, act_ref, dhg_ref, dhu_ref, du2_ref):
        dact = _dot_nt(dh_ref[...], wd_ref[...])
        hg = hg_ref[...]
        hu = hu_ref[...]
        sg = _sigmoid(hg)
        sl = hg * sg
        act_ref[...] = (sl * hu).astype(MXU)
        dhg = (dact * hu * (sg * (1.0 + hg * (1.0 - sg)))).astype(MXU)
        dhu = (dact * sl).astype(MXU)
        dhg_ref[...] = dhg
        dhu_ref[...] = dhu
        du2_ref[...] = _dot_nt(dhg, wg_ref[...]) + _dot_nt(dhu, wu_ref[...])

    return pl.pallas_call(
        body, name="ffn_bwd", grid=(s // tm,),
        in_specs=[_rows(tm, D), _rows(tm, D_FF), _rows(tm, D_FF), _res(w_g), _res(w_u), _res(w_d)],
        out_specs=[_rows(tm, D_FF), _rows(tm, D_FF), _rows(tm, D_FF), _rows(tm, D)],
        out_shape=[jax.ShapeDtypeStruct((s, D_FF), MXU)] * 3 + [jax.ShapeDtypeStruct((s, D), F32)],
        compiler_params=_params(VMEM_BIG),
    )(dh, hg, hu, w_g, w_u, w_d)


def _mix_bwd(du2, dxa, r1, o_sb, o_fx, gl, w_sb, w_fx, w_o, g1, ln1_g, ln1_b, sc2):
    s = r1.shape[0]
    tm = 256

    def body(du2_ref, dxa_ref, r1_ref, osb_ref, ofx_ref, gl_ref, wsb_ref, wfx_ref, wo_ref, g1_ref, lg_ref, lb_ref, sc_ref,
             dx_ref, mixin_ref, dmix_ref, dysb_ref, dyfx_ref, dosb_ref, dofx_ref, dgl_ref, dbg_ref, acc_ref):
        @pl.when(pl.program_id(0) == 0)
        def _():
            acc_ref[...] = jnp.zeros_like(acc_ref)
            dbg_ref[...] = jnp.zeros_like(dbg_ref)

        du2 = du2_ref[...]
        xh1, rstd1 = _ln(r1_ref[...])
        x1 = xh1 * lg_ref[...] + lb_ref[...]
        n1, rstdn = _ln(x1)
        dx1 = dxa_ref[...] + _ln_bwd(du2 * (1.0 + sc_ref[...]), n1, rstdn)
        dr1 = _ln_bwd(dx1 * lg_ref[...], xh1, rstd1)
        dx_ref[...] = ALPHA * dr1
        ysb = _dot(osb_ref[...], wsb_ref[...])
        yfx = _dot(ofx_ref[...], wfx_ref[...])
        gs = _sigmoid(gl_ref[:, :D])
        gf = _sigmoid(gl_ref[:, D:])
        mixin = (gs * ysb + gf * yfx).astype(MXU)
        mixin_ref[...] = mixin
        mix = _dot(mixin, wo_ref[...])
        dmix = (g1_ref[...] * dr1).astype(MXU)
        dmix_ref[...] = dmix
        dmixin = _dot_nt(dmix, wo_ref[...])
        dysb = (dmixin * gs).astype(MXU)
        dyfx = (dmixin * gf).astype(MXU)
        dysb_ref[...] = dysb
        dyfx_ref[...] = dyfx
        dosb_ref[...] = _dot_nt(dysb, wsb_ref[...]).astype(MXU)
        dofx_ref[...] = _dot_nt(dyfx, wfx_ref[...]).astype(MXU)
        dgs = dmixin * ysb * gs * (1.0 - gs)
        dgf = dmixin * yfx * gf * (1.0 - gf)
        dgl_ref[:, :D] = dgs.astype(MXU)
        dgl_ref[:, D:] = dgf.astype(MXU)
        dbg_ref[:, :D] += _colsum(dgs)
        dbg_ref[:, D:] += _colsum(dgf)
        acc_ref[0:1, :] += _colsum(du2)
        acc_ref[1:2, :] += _colsum(du2 * n1)
        acc_ref[2:3, :] += _colsum(dx1 * xh1)
        acc_ref[3:4, :] += _colsum(dx1)
        acc_ref[4:5, :] += _colsum(dr1 * mix)

    vec = _fixed(1, D)
    return pl.pallas_call(
        body, name="mix_bwd", grid=(s // tm,),
        in_specs=[_rows(tm, D), _rows(tm, D), _rows(tm, D), _rows(tm, WIDTH), _rows(tm, WIDTH), _rows(tm, 2 * D),
                  _res(w_sb), _res(w_fx), _res(w_o), vec, vec, vec, vec],
        out_specs=[_rows(tm, D), _rows(tm, D), _rows(tm, D), _rows(tm, D), _rows(tm, D), _rows(tm, WIDTH), _rows(tm, WIDTH),
                   _rows(tm, 2 * D), _fixed(1, 2 * D), _fixed(8, D)],
        out_shape=[jax.ShapeDtypeStruct((s, D), F32)] + [jax.ShapeDtypeStruct((s, D), MXU)] * 4
        + [jax.ShapeDtypeStruct((s, WIDTH), MXU)] * 2
        + [jax.ShapeDtypeStruct((s, 2 * D), MXU), jax.ShapeDtypeStruct((1, 2 * D), F32), jax.ShapeDtypeStruct((8, D), F32)],
        compiler_params=_params(VMEM_BIG),
    )(du2, dxa, r1, o_sb, o_fx, gl, w_sb, w_fx, w_o, g1, ln1_g, ln1_b, sc2)


def _in_bwd(pieces, x, dxa, w_all, sc1):
    s = x.shape[0]
    tm = 256
    n_p = len(pieces)

    def body(*refs):
        p_refs = refs[:n_p]
        x_ref, dxa_ref, w_ref, sc_ref, gx_ref, acc_ref = refs[n_p:]

        @pl.when(pl.program_id(0) == 0)
        def _():
            acc_ref[...] = jnp.zeros_like(acc_ref)

        du1 = jnp.zeros((tm, D), F32)
        for p_ref, (arr, c0) in zip(p_refs, pieces):
            du1 = du1 + _dot_nt(p_ref[...], w_ref[:, c0:c0 + arr.shape[1]])
        n0, rstd0 = _ln(x_ref[...])
        gx_ref[...] = dxa_ref[...] + _ln_bwd(du1 * (1.0 + sc_ref[...]), n0, rstd0)
        acc_ref[0:1, :] += _colsum(du1)
        acc_ref[1:2, :] += _colsum(du1 * n0)

    return pl.pallas_call(
        body, name="in_bwd", grid=(s // tm,),
        in_specs=[_rows(tm, a.shape[1]) for a, _ in pieces] + [_rows(tm, D), _rows(tm, D), _res(w_all), _fixed(1, D)],
        out_specs=[_rows(tm, D), _fixed(8, D)],
        out_shape=[jax.ShapeDtypeStruct((s, D), F32), jax.ShapeDtypeStruct((8, D), F32)],
        compiler_params=_params(VMEM_BIG),
    )(*[a for a, _ in pieces], x, dxa, w_all, sc1)


def _matmul_tn(a, b, name):
    s, m = a.shape
    n = b.shape[1]
    tm = 512 if m % 512 == 0 else m // 2
    tn = n // 2 if n > 2048 else n
    ts = 512
    assert m % tm == 0 and tm % LANE == 0 and n % tn == 0 and tn % LANE == 0 and s % ts == 0

    def body(a_ref, b_ref, o_ref):
        @pl.when(pl.program_id(2) == 0)
        def _():
            o_ref[...] = jnp.zeros_like(o_ref)

        o_ref[...] += _dot_tn(a_ref[...], b_ref[...])

    return pl.pallas_call(
        body, name=name, grid=(m // tm, n // tn, s // ts),
        in_specs=[pl.BlockSpec((ts, tm), lambda i, j, k: (k, i)), pl.BlockSpec((ts, tn), lambda i, j, k: (k, j))],
        out_specs=pl.BlockSpec((tm, tn), lambda i, j, k: (i, j)),
        out_shape=jax.ShapeDtypeStruct((m, n), F32),
        compiler_params=_params(VMEM_BIG),
    )(a, b)


def _local_step(x, tgt, ada, w_all, b_gate, bf_pad, w_sb, w_fx, w_o, ln1_g, ln1_b, w_g, w_u, w_d, ln2_g, ln2_b):
    sh1, sc1, g1, sh2, sc2, g2 = ada
    u1, qkv, fl, gl = _in_proj(x, sh1, sc1, w_all, b_gate)
    fc, fkt = _fcum_fwd(fl, bf_pad)
    o_sb, rs = _sb_fwd(qkv)
    o_fx, lse = _fox_fwd(qkv, fc, fkt)
    r1, u2 = _mix_fwd(x, o_sb, o_fx, gl, w_sb, w_fx, w_o, g1, ln1_g, ln1_b, sh2, sc2)
    hg, hu, dxa2, dh, acc_f = _ffn_fwd(r1, u2, tgt, w_g, w_u, w_d, g2, ln1_g, ln1_b, ln2_g, ln2_b)
    act, dhg, dhu, du2 = _ffn_bwd(dh, hg, hu, w_g, w_u, w_d)
    dxa1, mixin, dmix, dysb, dyfx, dosb, dofx, dgl, dbg, acc_m = _mix_bwd(
        du2, dxa2, r1, o_sb, o_fx, gl, w_sb, w_fx, w_o, g1, ln1_g, ln1_b, sc2)
    dq_sb, dk_sb, dv_sb = _sb_bwd(qkv, dosb, rs)
    dq_fx, dk_fx, dv_fx, dfkt, dfq = _fox_bwd(qkv, fc, fkt, dofx, o_fx, lse)
    df, dbf = _fcum_bwd(dfkt, dfq, fl, bf_pad)
    pieces = [(dq_sb, 0), (dk_sb, WIDTH), (dv_sb, 2 * WIDTH), (dq_fx, 3 * WIDTH), (dk_fx, 4 * WIDTH), (dv_fx, 5 * WIDTH),
              (df, OFF_FGATE), (dgl, OFF_FGATE + LANE)]
    grad_x, acc_i = _in_bwd(pieces, x, dxa1, w_all, sc1)
    dw_in = [_matmul_tn(u1, p, f"dw_in_{j}") for j, (p, _) in enumerate(pieces)]
    return dict(
        loss_lanes=acc_f[3:4], grad_x=grad_x, dw_in=dw_in,
        dw_sb=_matmul_tn(o_sb, dysb, "dw_sb_out"), dw_fx=_matmul_tn(o_fx, dyfx, "dw_fox_out"),
        dw_o=_matmul_tn(mixin, dmix, "dw_o"),
        dw_g=_matmul_tn(u2, dhg, "dw_ffn_gate"), dw_u=_matmul_tn(u2, dhu, "dw_ffn_up"), dw_d=_matmul_tn(act, dh, "dw_ffn_down"),
        d_ada=[acc_i[0:1], acc_i[1:2], acc_m[4:5], acc_m[0:1], acc_m[1:2], acc_f[0:1]],
        dln1_g=acc_m[2:3], dln1_b=acc_m[3:4], dln2_g=acc_f[1:2], dln2_b=acc_f[2:3], db_gate=dbg, db_forget=dbf)


_MESH_ID = pl.DeviceIdType.MESH
_ANY = pl.BlockSpec(memory_space=pl.ANY)
_VMEM = pl.BlockSpec(memory_space=pltpu.VMEM)


def _mesh_pos():
    return lax.axis_index("x"), lax.axis_index("y"), lax.axis_index("c")


def _other_chips(x, y):
    return [(1 - x, y), (x, 1 - y), (1 - x, 1 - y)]


def _allgather_rows(v, name):
    n = v.shape[1]

    def body(v_ref, out_ref, send_sems, recv_sems, local_sem):
        x, y, c = _mesh_pos()
        me = 4 * x + 2 * y + c
        mine = pltpu.make_async_copy(v_ref, out_ref.at[me], local_sem)
        mine.start()
        copies = []
        for d in range(1, 8):
            fx, fy, fc = (d >> 2) & 1, (d >> 1) & 1, d & 1
            to = (1 - x if fx else x, 1 - y if fy else y, 1 - c if fc else c)
            cp = pltpu.make_async_remote_copy(src_ref=v_ref, dst_ref=out_ref.at[me], send_sem=send_sems.at[d - 1],
                                              recv_sem=recv_sems.at[d - 1], device_id=to, device_id_type=_MESH_ID)
            cp.start()
            copies.append(cp)
        for cp in copies:
            cp.wait_recv()
        for cp in copies:
            cp.wait_send()
        mine.wait()

    return pl.pallas_call(
        body, name=name, in_specs=[_VMEM], out_specs=_VMEM,
        out_shape=jax.ShapeDtypeStruct((8, 1, n), v.dtype),
        scratch_shapes=[pltpu.SemaphoreType.DMA((7,)), pltpu.SemaphoreType.DMA((7,)), pltpu.SemaphoreType.DMA(())],
    )(v)


def _chip_exchange(arrays, name, gather):
    nt = len(arrays)

    def body(*refs):
        ins, outs = refs[:nt], refs[nt:2 * nt]
        send_sems, recv_sems, local_sems = refs[2 * nt:]
        x, y, c = _mesh_pos()
        me = 2 * x + y
        copies = []
        for t in range(nt):
            src = ins[t] if gather else ins[t].at[me]
            cp = pltpu.make_async_copy(src, outs[t].at[me], local_sems.at[t])
            cp.start()
            copies.append(cp)
        remote = []
        for t in range(nt):
            for j, (px, py) in enumerate(_other_chips(x, y)):
                src = ins[t] if gather else ins[t].at[2 * px + py]
                cp = pltpu.make_async_remote_copy(src_ref=src, dst_ref=outs[t].at[me], send_sem=send_sems.at[3 * t + j],
                                                  recv_sem=recv_sems.at[3 * t + j], device_id=(px, py, c), device_id_type=_MESH_ID)
                cp.start()
                remote.append(cp)
        for cp in remote:
            cp.wait_recv()
        for cp in remote:
            cp.wait_send()
        for cp in copies:
            cp.wait()

    out_shape = [jax.ShapeDtypeStruct((4,) + a.shape[-2:], a.dtype) for a in arrays]
    return pl.pallas_call(
        body, name=name, in_specs=[_ANY] * nt, out_specs=[_ANY] * nt, out_shape=out_shape,
        scratch_shapes=[pltpu.SemaphoreType.DMA((3 * nt,)), pltpu.SemaphoreType.DMA((3 * nt,)), pltpu.SemaphoreType.DMA((nt,))],
    )(*arrays)


def _sibling_exchange(arrays, name):
    nt = len(arrays)

    def body(*refs):
        ins, outs = refs[:nt], refs[nt:2 * nt]
        send_sems, recv_sems = refs[2 * nt:]
        x, y, c = _mesh_pos()
        copies = []
        for t in range(nt):
            cp = pltpu.make_async_remote_copy(src_ref=ins[t], dst_ref=outs[t], send_sem=send_sems.at[t], recv_sem=recv_sems.at[t],
                                              device_id=(x, y, 1 - c), device_id_type=_MESH_ID)
            cp.start()
            copies.append(cp)
        for cp in copies:
            cp.wait_recv()
        for cp in copies:
            cp.wait_send()

    return pl.pallas_call(
        body, name=name, in_specs=[_ANY] * nt, out_specs=[_ANY] * nt,
        out_shape=[jax.ShapeDtypeStruct(a.shape, a.dtype) for a in arrays],
        scratch_shapes=[pltpu.SemaphoreType.DMA((nt,)), pltpu.SemaphoreType.DMA((nt,))],
    )(*arrays)


def _row_tile(r):
    for tr in (256, 352, 128):
        if r % tr == 0:
            return tr
    return r


def _reduce_chips(own, recv, name):
    r, n = own.shape
    tr = _row_tile(r)

    def body(own_ref, recv_ref, out_ref):
        x, y, _ = _mesh_pos()
        me = 2 * x + y
        total = jnp.zeros((tr, n), F32)
        for k in range(4):
            total = total + jnp.where(me == k, own_ref[...], recv_ref[k].astype(F32))
        out_ref[...] = total

    return pl.pallas_call(
        body, name=name, grid=(r // tr,),
        in_specs=[_rows(tr, n), pl.BlockSpec((4, tr, n), lambda i: (0, i, 0))], out_specs=_rows(tr, n),
        out_shape=jax.ShapeDtypeStruct((r, n), F32),
    )(own, recv)


def _adamw_math(w, g, m, v):
    m = ADAM_B1 * m + (1.0 - ADAM_B1) * g
    v = ADAM_B2 * v + (1.0 - ADAM_B2) * (g * g)
    m_hat = m / (1.0 - ADAM_B1 ** ADAM_STEP)
    v_hat = v / (1.0 - ADAM_B2 ** ADAM_STEP)
    return -ADAM_LR * (m_hat / (jnp.sqrt(v_hat) + ADAM_EPS) + ADAM_WD * w), m, v


def _adamw(w, m, v, g_parts, name):
    r, n = w.shape
    tr = _row_tile(r)
    ng = len(g_parts)

    def body(*refs):
        w_ref, m_ref, v_ref = refs[:3]
        g_refs = refs[3:3 + ng]
        g_out, d_out, m_out, v_out = refs[3 + ng:]
        g = g_refs[0][...]
        for gr in g_refs[1:]:
            g = g + gr[...]
        g_out[...] = g
        d_out[...], m_out[...], v_out[...] = _adamw_math(w_ref[...], g, m_ref[...], v_ref[...])

    return pl.pallas_call(
        body, name=name, grid=(r // tr,),
        in_specs=[_rows(tr, n)] * (3 + ng), out_specs=[_rows(tr, n)] * 4,
        out_shape=[jax.ShapeDtypeStruct((r, n), F32)] * 4,
    )(w, m, v, *g_parts)


def _ada_fwd(c_all, w_shard, b_shard):
    n = w_shard.shape[1]
    tn = 512

    def body(c_ref, w_ref, b_ref, o_ref):
        cv = c_ref[...]
        ca = (cv * _sigmoid(cv)).astype(MXU)
        o_ref[...] = _dot(ca, w_ref[...].astype(MXU)) + b_ref[...]

    return pl.pallas_call(
        body, name="ada_fwd", grid=(n // tn,),
        in_specs=[_fixed(8, D), pl.BlockSpec((D, tn), lambda j: (0, j)), pl.BlockSpec((1, tn), lambda j: (0, j))],
        out_specs=pl.BlockSpec((8, tn), lambda j: (0, j)),
        out_shape=jax.ShapeDtypeStruct((8, n), F32),
    )(c_all, w_shard, b_shard)


def _ada_bwd(c_all, dada_shard):
    n = dada_shard.shape[1]
    tn = 512

    def body(c_ref, d_ref, o_ref):
        cv = c_ref[...]
        ca = (cv * _sigmoid(cv)).astype(MXU)
        o_ref[...] = _dot_tn(ca, d_ref[...].astype(MXU))

    return pl.pallas_call(
        body, name="ada_bwd", grid=(n // tn,),
        in_specs=[_fixed(8, D), pl.BlockSpec((8, tn), lambda j: (0, j))],
        out_specs=pl.BlockSpec((D, tn), lambda j: (0, j)),
        out_shape=jax.ShapeDtypeStruct((D, n), F32),
    )(c_all, dada_shard)


_SMALL = [("d_ada", N_COND * D), ("ln1_g", D), ("ln1_b", D), ("ln2_g", D), ("ln2_b", D), ("b_gate", 2 * D), ("b_forget", LANE),
          ("loss", D)]
_SMALL_OFF = {}
_o = 0
for _n, _w in _SMALL:
    _SMALL_OFF[_n] = (_o, _w)
    _o += _w
_SMALL_LEN = _o
_SMALL_PARAMS = [("b_ada", "d_ada", N_COND * D), ("b_gate", "b_gate", 2 * D), ("b_forget", "b_forget", N_FGATE),
                 ("ln1_g", "ln1_g", D), ("ln1_b", "ln1_b", D), ("ln2_g", "ln2_g", D), ("ln2_b", "ln2_b", D)]


def _small_update(rows, params):
    npar = len(_SMALL_PARAMS)

    def body(*refs):
        rows_ref = refs[0]
        p_refs = refs[1:1 + 3 * npar]
        loss_ref = refs[1 + 3 * npar]
        o_refs = refs[2 + 3 * npar:]
        total = rows_ref[0]
        for d in range(1, 8):
            total = total + rows_ref[d]
        lo, lw = _SMALL_OFF["loss"]
        loss_ref[...] = jnp.sum(total[:, lo:lo + lw], axis=1, keepdims=True)
        for j, (_, key, n) in enumerate(_SMALL_PARAMS):
            off = _SMALL_OFF[key][0]
            g = total[:, off:off + n]
            w_ref, m_ref, v_ref = p_refs[3 * j:3 * j + 3]
            o_refs[4 * j][...] = g
            o_refs[4 * j + 1][...], o_refs[4 * j + 2][...], o_refs[4 * j + 3][...] = _adamw_math(w_ref[...], g, m_ref[...], v_ref[...])

    flat = [a for p in params for a in p]
    out_shape = [jax.ShapeDtypeStruct((1, 1), F32)] + [jax.ShapeDtypeStruct((1, n), F32) for _, _, n in _SMALL_PARAMS for _ in range(4)]
    return pl.pallas_call(body, name="small_update", out_shape=out_shape)(rows, *flat)


_BIG = [("w_in", True), ("w_sb_out", True), ("w_fox_out", True), ("w_o", False),
        ("w_ffn_gate", True), ("w_ffn_up", True), ("w_ffn_down", False)]


def _unshard(g, by_cols):
    if by_cols:
        return g.transpose(1, 0, 2).reshape(g.shape[1], 4 * g.shape[2])
    return g.reshape(4 * g.shape[1], g.shape[2])


def _reshard(w, by_cols):
    if by_cols:
        return w.reshape(w.shape[0], 4, w.shape[1] // 4).transpose(1, 0, 2)
    return w.reshape(4, w.shape[0] // 4, w.shape[1])


def kernel(x, c, w_ada, b_ada, w_in, b_gate, b_forget, w_sb_out, w_fox_out, w_o, ln1_g, ln1_b, w_ffn_gate, w_ffn_up, w_ffn_down, ln2_g, ln2_b, loss_target, m_w_ada, m_b_ada, m_w_in, m_b_gate, m_b_forget, m_w_sb_out, m_w_fox_out, m_w_o, m_ln1_g, m_ln1_b, m_w_ffn_gate, m_w_ffn_up, m_w_ffn_down, m_ln2_g, m_ln2_b, v_w_ada, v_b_ada, v_w_in, v_b_gate, v_b_forget, v_w_sb_out, v_w_fox_out, v_w_o, v_ln1_g, v_ln1_b, v_w_ffn_gate, v_w_ffn_up, v_w_ffn_down, v_ln2_g, v_ln2_b):
    given = dict(locals())
    mx, my, mc = _mesh_pos()
    chip = 2 * mx + my
    seq = 4 * mx + 2 * my + mc

    c_all = _allgather_rows(c, "gather_c").reshape(8, D)
    n_ada = w_ada.shape[2]
    b_ada_shard = lax.dynamic_slice(b_ada, (0, chip * n_ada), (1, n_ada))
    ada_part = _ada_fwd(c_all, w_ada[0], b_ada_shard)
    ada_all = _allgather_rows(ada_part.reshape(1, 8 * n_ada), "gather_ada").reshape(4, 2, 8, n_ada)
    ada_row = lax.dynamic_slice(ada_all, (0, mc, seq, 0), (4, 1, 1, n_ada)).reshape(1, N_COND * D)
    ada = [ada_row[:, j * D:(j + 1) * D] for j in range(N_COND)]

    gathered = _chip_exchange([given[n][0].astype(MXU) for n, _ in _BIG], "gather_weights", gather=True)
    full = {n: _unshard(g, by_cols) for (n, by_cols), g in zip(_BIG, gathered)}
    wi = full["w_in"]
    w_all = jnp.concatenate([wi[:, :OFF_FGATE + N_FGATE], jnp.zeros((D, LANE - N_FGATE), MXU), wi[:, OFF_FGATE + N_FGATE:]], axis=1)
    bf_pad = jnp.concatenate([b_forget, jnp.zeros((1, LANE - N_FGATE), F32)], axis=1)

    out = _local_step(x[0], loss_target[0], ada, w_all, b_gate, bf_pad, full["w_sb_out"], full["w_fox_out"], full["w_o"],
                      ln1_g, ln1_b, full["w_ffn_gate"], full["w_ffn_up"], full["w_ffn_down"], ln2_g, ln2_b)

    row = jnp.concatenate(out["d_ada"] + [out["dln1_g"], out["dln1_b"], out["dln2_g"], out["dln2_b"], out["db_gate"],
                                          out["db_forget"], out["loss_lanes"]], axis=1)
    rows = _allgather_rows(row, "gather_small")
    small = _small_update(rows, [(given[p], given["m_" + p], given["v_" + p]) for p, _, _ in _SMALL_PARAMS])
    loss = small[0].reshape(())
    res = {}
    for j, (p, _, _) in enumerate(_SMALL_PARAMS):
        res[p] = small[1 + 4 * j:5 + 4 * j]

    dada_all = rows.reshape(8, _SMALL_LEN)[:, :N_COND * D]
    dada_shard = lax.dynamic_slice(dada_all, (0, chip * n_ada), (8, n_ada))
    g_ada = _ada_bwd(c_all, dada_shard)
    res["w_ada"] = [a[None] for a in _adamw(w_ada[0], m_w_ada[0], v_w_ada[0], [g_ada], "adamw_w_ada")]

    dwi = out["dw_in"]
    dw_full = {"w_in": jnp.concatenate(dwi[:6] + [dwi[6][:, :N_FGATE], dwi[7]], axis=1), "w_sb_out": out["dw_sb"],
               "w_fox_out": out["dw_fx"], "w_o": out["dw_o"], "w_ffn_gate": out["dw_g"], "w_ffn_up": out["dw_u"],
               "w_ffn_down": out["dw_d"]}
    pieces = [_reshard(dw_full[n], by_cols) for n, by_cols in _BIG]
    received = _chip_exchange([p.astype(MXU) for p in pieces], "scatter_grads", gather=False)
    partial = [_reduce_chips(lax.dynamic_index_in_dim(p, chip, 0, keepdims=False), r, "reduce_" + n)
               for (n, _), p, r in zip(_BIG, pieces, received)]
    theirs = _sibling_exchange(partial, "swap_cores")
    for (n, _), mine, other in zip(_BIG, partial, theirs):
        res[n] = [a[None] for a in _adamw(given[n][0], given["m_" + n][0], given["v_" + n][0], [mine, other], "adamw_" + n)]

    order = ["w_ada", "b_ada", "w_in", "b_gate", "b_forget", "w_sb_out", "w_fox_out", "w_o", "ln1_g", "ln1_b",
             "w_ffn_gate", "w_ffn_up", "w_ffn_down", "ln2_g", "ln2_b"]
    return (loss, out["grad_x"][None], *[res[n][0] for n in order], *[res[n][1] for n in order],
            *[res[n][2] for n in order], *[res[n][3] for n in order])
```

```python
import functools

import jax
import jax.numpy as jnp
from jax import lax
from jax.experimental import pallas as pl
from jax.experimental.pallas import tpu as pltpu

F32 = jnp.float32
MXU = jnp.bfloat16

D = 1024
HEAD_DIM = 64
WIDTH = 512
D_FF = 2816
N_COND = 6
LN_EPS = 1e-5
ALPHA = 2.0 ** 0.25
QK_SCALE = HEAD_DIM ** -0.5
OFF_FGATE = 6 * WIDTH
N_FGATE = 8
IN_COLS = OFF_FGATE + N_FGATE + 2 * D
LANE = 128
W_ALL_COLS = OFF_FGATE + LANE + 2 * D
TQ = 512
ADAM_LR, ADAM_B1, ADAM_B2, ADAM_EPS, ADAM_WD, ADAM_STEP = 0.001, 0.9, 0.999, 1e-08, 0.01, 10
NEG = -1e30
MESH_AXES = ("x", "y", "c")
VMEM_BIG = 56 * 1024 * 1024


def _dot(a, b):
    return jnp.dot(a, b, preferred_element_type=F32)


def _dot_nt(a, b):
    return lax.dot_general(a, b, (((1,), (1,)), ((), ())), preferred_element_type=F32)


def _dot_tn(a, b):
    return lax.dot_general(a, b, (((0,), (0,)), ((), ())), preferred_element_type=F32)


def _ln(x):
    mu = jnp.mean(x, axis=-1, keepdims=True)
    xc = x - mu
    var = jnp.mean(xc * xc, axis=-1, keepdims=True)
    rstd = lax.rsqrt(var + LN_EPS)
    return xc * rstd, rstd


def _ln_bwd(dxhat, xhat, rstd):
    return rstd * (dxhat - jnp.mean(dxhat, axis=-1, keepdims=True) - xhat * jnp.mean(dxhat * xhat, axis=-1, keepdims=True))


def _sigmoid(x):
    return 1.0 / (1.0 + jnp.exp(-x))


def _colsum(x):
    return jnp.sum(x, axis=0, keepdims=True)


def _split(x):
    hi = x.astype(MXU)
    lo = (x - hi.astype(F32)).astype(MXU)
    return jnp.concatenate([hi, lo], axis=1)


def _rows(tm, n):
    return pl.BlockSpec((tm, n), lambda i: (i, 0))


def _fixed(r, n):
    return pl.BlockSpec((r, n), lambda i: (0, 0))


def _res(a):
    return pl.BlockSpec(a.shape, lambda i: (0, 0), pipeline_mode=pl.Buffered(1))


def _params(limit=None, sem=None):
    return pltpu.CompilerParams(vmem_limit_bytes=limit, dimension_semantics=sem)


def _in_proj(x, sh1, sc1, w_all, b_gate):
    s = x.shape[0]
    tm = 256

    def body(x_ref, sh_ref, sc_ref, w_ref, bg_ref, u_ref, qkv_ref, fl_ref, gl_ref):
        xhat, _ = _ln(x_ref[...])
        u = (xhat * (1.0 + sc_ref[...]) + sh_ref[...]).astype(MXU)
        u_ref[...] = u
        for c0 in range(0, OFF_FGATE, WIDTH):
            p = _dot_nt(u, w_ref[c0:c0 + WIDTH, :])
            if c0 in (0, 3 * WIDTH):
                p = p * QK_SCALE
            qkv_ref[:, c0:c0 + WIDTH] = p.astype(MXU)
        fl_ref[...] = _dot_nt(u, w_ref[OFF_FGATE:OFF_FGATE + LANE, :])
        for c0 in range(0, 2 * D, D):
            gl_ref[:, c0:c0 + D] = _dot_nt(u, w_ref[OFF_FGATE + LANE + c0:OFF_FGATE + LANE + c0 + D, :]) + bg_ref[:, c0:c0 + D]

    return pl.pallas_call(
        body, name="in_proj", grid=(s // tm,),
        in_specs=[_rows(tm, D), _fixed(1, D), _fixed(1, D), _res(w_all), _fixed(1, 2 * D)],
        out_specs=[_rows(tm, D), _rows(tm, OFF_FGATE), _rows(tm, LANE), _rows(tm, 2 * D)],
        out_shape=[jax.ShapeDtypeStruct((s, D), MXU), jax.ShapeDtypeStruct((s, OFF_FGATE), MXU),
                   jax.ShapeDtypeStruct((s, LANE), F32), jax.ShapeDtypeStruct((s, 2 * D), F32)],
        compiler_params=_params(VMEM_BIG),
    )(x, sh1, sc1, w_all, b_gate)


def _log_sigmoid_parts(z):
    e = jnp.exp(-jnp.abs(z))
    return -(jnp.maximum(z, 0.0) + jnp.log(1.0 + e)), e


def _fcum_fwd(fl, bf):
    s = fl.shape[0]
    nb = s // LANE

    def body(fl_ref, bf_ref, fc_ref, fkt_ref):
        r = lax.broadcasted_iota(jnp.int32, (LANE, LANE), 0)
        c = lax.broadcasted_iota(jnp.int32, (LANE, LANE), 1)
        tri = (c <= r).astype(F32)

        def step(b, carry):
            r0 = pl.multiple_of(b * LANE, LANE)
            xb = fl_ref[pl.ds(r0, LANE), :] + bf_ref[...]
            ls = _log_sigmoid_parts(-xb)[0]
            cs = jnp.dot(tri, ls, precision=lax.Precision.HIGHEST, preferred_element_type=F32) + carry
            fc_ref[pl.ds(r0, LANE), :] = cs
            fkt_ref[b] = cs.T[:N_FGATE, :]
            return cs[LANE - 1:LANE, :]

        lax.fori_loop(0, nb, step, jnp.zeros((1, LANE), F32))

    return pl.pallas_call(
        body, name="fcum_fwd",
        out_shape=[jax.ShapeDtypeStruct((s, LANE), F32), jax.ShapeDtypeStruct((nb, N_FGATE, LANE), F32)],
    )(fl, bf)


def _attn_specs(s, col0):
    return [pl.BlockSpec((TQ, LANE), lambda hp, i: (i, col0 + hp)),
            pl.BlockSpec((s, LANE), lambda hp, i: (0, col0 + 4 + hp)),
            pl.BlockSpec((s, LANE), lambda hp, i: (0, col0 + 8 + hp))]


def _tile_iotas():
    lane = lax.broadcasted_iota(jnp.int32, (TQ, LANE), 1)
    row = lax.broadcasted_iota(jnp.int32, (TQ, TQ), 0)
    col = lax.broadcasted_iota(jnp.int32, (TQ, TQ), 1)
    return lane, row, col


def _sub_blocks():
    return [slice(j * LANE, (j + 1) * LANE) for j in range(TQ // LANE)]


def _tri(below):
    r = lax.broadcasted_iota(jnp.int32, (LANE, LANE), 0)
    c = lax.broadcasted_iota(jnp.int32, (LANE, LANE), 1)
    t = jnp.concatenate([((r > c) if below else (r < c)).astype(MXU), jnp.ones((LANE, LANE), MXU)], axis=1)
    return jnp.concatenate([t, t], axis=0)


def _sb_fwd(qkv):
    s = qkv.shape[0]
    nq = s // TQ
    assert nq <= LANE

    def body(q_ref, k_ref, v_ref, o_ref, rs_ref):
        i = pl.program_id(1)
        lane, row, col = _tile_iotas()
        u2 = _tri(True)
        diag = col < row
        q = q_ref[...]
        outs = []
        for hh in range(2):
            hm = (lane >= HEAD_DIM) if hh else (lane < HEAD_DIM)
            qm = jnp.where(hm, q, jnp.zeros_like(q))

            def step(kb, carry, masked):
                run, acc, rt = carry
                k0 = pl.multiple_of(kb * TQ, TQ)
                k = k_ref[pl.ds(k0, TQ), :]
                v = v_ref[pl.ds(k0, TQ), :]
                z = _dot_nt(qm, k)
                lneg, _ = _log_sigmoid_parts(z)
                lpos = z + lneg
                if masked:
                    lneg = jnp.where(diag, lneg, 0.0)
                rt = jnp.where(lane == kb, run, rt)
                a = []
                for sl in reversed(_sub_blocks()):
                    st = _dot(_split(lneg[:, sl]), u2)
                    a.append(jnp.exp(lpos[:, sl] + st[:, :LANE] + run))
                    run = run + st[:, LANE:]
                a = jnp.concatenate(a[::-1], axis=1)
                if masked:
                    a = jnp.where(diag, a, 0.0)
                return run, acc + _dot(a.astype(MXU), v), rt

            zero = jnp.zeros((TQ, LANE), F32)
            carry = step(i, (zero, zero, zero), True)
            carry = lax.fori_loop(0, i, lambda j, cr: step(i - 1 - j, cr, False), carry)
            outs.append(carry[1])
            rs_ref[hh] = carry[2]
        o_ref[...] = jnp.where(lane < HEAD_DIM, outs[0], outs[1]).astype(o_ref.dtype)

    return pl.pallas_call(
        body, name="sb_fwd", grid=(4, nq),
        in_specs=_attn_specs(s, 0),
        out_specs=[pl.BlockSpec((TQ, LANE), lambda hp, i: (i, hp)), pl.BlockSpec((2, TQ, LANE), lambda hp, i: (hp, i, 0))],
        out_shape=[jax.ShapeDtypeStruct((s, WIDTH), MXU), jax.ShapeDtypeStruct((8, s, LANE), F32)],
    )(qkv, qkv, qkv)


def _sb_bwd(qkv, do, rs):
    s = qkv.shape[0]
    nq = s // TQ

    def body(q_ref, k_ref, v_ref, do_ref, rs_ref, dq_ref, dk_ref, dv_ref, dk_acc, dv_acc):
        i = pl.program_id(1)

        @pl.when(i == 0)
        def _():
            dk_acc[...] = jnp.zeros_like(dk_acc)
            dv_acc[...] = jnp.zeros_like(dv_acc)

        lane, row, col = _tile_iotas()
        u2 = _tri(True)
        l2 = _tri(False)
        diag = col < row
        q = q_ref[...]
        do = do_ref[...]
        outs = []
        for hh in range(2):
            hm = (lane >= HEAD_DIM) if hh else (lane < HEAD_DIM)
            qm = jnp.where(hm, q, jnp.zeros_like(q))
            dom = jnp.where(hm, do, jnp.zeros_like(do))
            rblk = rs_ref[hh]

            def step(kb, carry, masked):
                gpre, dq = carry
                k0 = pl.multiple_of(kb * TQ, TQ)
                k = k_ref[pl.ds(k0, TQ), :]
                v = v_ref[pl.ds(k0, TQ), :]
                z = _dot_nt(qm, k)
                lneg, e = _log_sigmoid_parts(z)
                lpos = z + lneg
                if masked:
                    lneg = jnp.where(diag, lneg, 0.0)
                run = jnp.sum(jnp.where(lane == kb, rblk, 0.0), axis=1, keepdims=True) + jnp.zeros((TQ, LANE), F32)
                a = []
                for sl in reversed(_sub_blocks()):
                    st = _dot(_split(lneg[:, sl]), u2)
                    a.append(jnp.exp(lpos[:, sl] + st[:, :LANE] + run))
                    run = run + st[:, LANE:]
                a = jnp.concatenate(a[::-1], axis=1)
                if masked:
                    a = jnp.where(diag, a, 0.0)
                g = a * _dot_nt(dom, v)
                pre = []
                for sl in _sub_blocks():
                    pt = _dot(_split(g[:, sl]), l2)
                    pre.append(gpre + pt[:, :LANE])
                    gpre = gpre + pt[:, LANE:]
                sig = jnp.where(z >= 0.0, 1.0, e) / (1.0 + e)
                dz = g - (g + jnp.concatenate(pre, axis=1)) * sig
                if masked:
                    dz = jnp.where(diag, dz, 0.0)
                dzb = dz.astype(MXU)
                dk_acc[pl.ds(k0, TQ), :] += _dot_tn(dzb, qm)
                dv_acc[pl.ds(k0, TQ), :] += _dot_tn(a.astype(MXU), dom)
                return gpre, dq + _dot(dzb, k)

            zero = jnp.zeros((TQ, LANE), F32)
            carry = lax.fori_loop(0, i, lambda kb, cr: step(kb, cr, False), (zero, zero))
            carry = step(i, carry, True)
            outs.append(carry[1])
        dq_ref[...] = (jnp.where(lane < HEAD_DIM, outs[0], outs[1]) * QK_SCALE).astype(dq_ref.dtype)

        @pl.when(i == nq - 1)
        def _():
            dk_ref[...] = dk_acc[...].astype(dk_ref.dtype)
            dv_ref[...] = dv_acc[...].astype(dv_ref.dtype)

    blk = pl.BlockSpec((TQ, LANE), lambda hp, i: (i, hp))
    whole = pl.BlockSpec((s, LANE), lambda hp, i: (0, hp))
    return pl.pallas_call(
        body, name="sb_bwd", grid=(4, nq),
        in_specs=_attn_specs(s, 0) + [blk, pl.BlockSpec((2, TQ, LANE), lambda hp, i: (hp, i, 0))],
        out_specs=[blk, whole, whole],
        out_shape=[jax.ShapeDtypeStruct((s, WIDTH), MXU)] * 3,
        scratch_shapes=[pltpu.VMEM((s, LANE), F32), pltpu.VMEM((s, LANE), F32)],
    )(qkv, qkv, qkv, do, rs)


def _key_bias(fkt_ref, kb, h):
    n_sub = TQ // LANE
    return jnp.concatenate([fkt_ref[kb * n_sub + j, pl.ds(h, 1), :] for j in range(n_sub)], axis=1)


def _fox_fwd(qkv, fc, fkt):
    s = qkv.shape[0]
    nq = s // TQ
    nb = fkt.shape[0]

    def body(q_ref, k_ref, v_ref, fq_ref, fkt_ref, o_ref, lse_ref):
        hp = pl.program_id(0)
        i = pl.program_id(1)
        lane, row, col = _tile_iotas()
        diag = col <= row
        q = q_ref[...]
        fqb = fq_ref[...]
        outs = []
        for hh in range(2):
            h = 2 * hp + hh
            hm = (lane >= HEAD_DIM) if hh else (lane < HEAD_DIM)
            qm = jnp.where(hm, q, jnp.zeros_like(q))
            fq = jnp.sum(jnp.where(lane == h, fqb, 0.0), axis=1, keepdims=True)

            def step(kb, carry, masked):
                m, l, acc = carry
                k0 = pl.multiple_of(kb * TQ, TQ)
                k = k_ref[pl.ds(k0, TQ), :]
                v = v_ref[pl.ds(k0, TQ), :]
                z = _dot_nt(qm, k) + fq - _key_bias(fkt_ref, kb, h)
                if masked:
                    z = jnp.where(diag, z, NEG)
                mn = jnp.maximum(m, jnp.max(z, axis=1, keepdims=True))
                p = jnp.exp(z - mn)
                alpha = jnp.exp(m - mn)
                return mn, alpha * l + jnp.sum(p, axis=1, keepdims=True), alpha * acc + _dot(p.astype(MXU), v)

            init = (jnp.full((TQ, 1), NEG, F32), jnp.zeros((TQ, 1), F32), jnp.zeros((TQ, LANE), F32))
            carry = lax.fori_loop(0, i, lambda kb, cr: step(kb, cr, False), init)
            m, l, acc = step(i, carry, True)
            outs.append(acc / l)
            lse_ref[hh] = jnp.broadcast_to(m + jnp.log(l), (TQ, LANE))
        o_ref[...] = jnp.where(lane < HEAD_DIM, outs[0], outs[1]).astype(o_ref.dtype)

    return pl.pallas_call(
        body, name="fox_fwd", grid=(4, nq),
        in_specs=_attn_specs(s, 12) + [pl.BlockSpec((TQ, LANE), lambda hp, i: (i, 0)),
                                       pl.BlockSpec((nb, N_FGATE, LANE), lambda hp, i: (0, 0, 0))],
        out_specs=[pl.BlockSpec((TQ, LANE), lambda hp, i: (i, hp)), pl.BlockSpec((2, TQ, LANE), lambda hp, i: (hp, i, 0))],
        out_shape=[jax.ShapeDtypeStruct((s, WIDTH), MXU), jax.ShapeDtypeStruct((8, s, LANE), F32)],
    )(qkv, qkv, qkv, fc, fkt)


def _fox_bwd(qkv, fc, fkt, do, o, lse):
    s = qkv.shape[0]
    nq = s // TQ
    nb = fkt.shape[0]

    def body(q_ref, k_ref, v_ref, fq_ref, fkt_ref, do_ref, o_ref, lse_ref, dq_ref, dk_ref, dv_ref, dfk_ref, dfq_ref, dk_acc, dv_acc):
        hp = pl.program_id(0)
        i = pl.program_id(1)

        @pl.when(i == 0)
        def _():
            dk_acc[...] = jnp.zeros_like(dk_acc)
            dv_acc[...] = jnp.zeros_like(dv_acc)

        @pl.when((i == 0) & (hp == 0))
        def _():
            dfk_ref[...] = jnp.zeros_like(dfk_ref)

        lane, row, col = _tile_iotas()
        diag = col <= row
        q = q_ref[...]
        do = do_ref[...]
        dof = do.astype(F32) * o_ref[...].astype(F32)
        fqb = fq_ref[...]
        outs = []
        dfq = jnp.zeros((TQ, LANE), F32)
        for hh in range(2):
            h = 2 * hp + hh
            hm = (lane >= HEAD_DIM) if hh else (lane < HEAD_DIM)
            qm = jnp.where(hm, q, jnp.zeros_like(q))
            dom = jnp.where(hm, do, jnp.zeros_like(do))
            delta = jnp.sum(jnp.where(hm, dof, 0.0), axis=1, keepdims=True)
            fq = jnp.sum(jnp.where(lane == h, fqb, 0.0), axis=1, keepdims=True)
            lse_t = lse_ref[hh][:, :1]

            def step(kb, carry, masked):
                dq, rsum = carry
                k0 = pl.multiple_of(kb * TQ, TQ)
                k = k_ref[pl.ds(k0, TQ), :]
                v = v_ref[pl.ds(k0, TQ), :]
                z = _dot_nt(qm, k) + fq - _key_bias(fkt_ref, kb, h)
                if masked:
                    z = jnp.where(diag, z, NEG)
                p = jnp.exp(z - lse_t)
                ds = p * (_dot_nt(dom, v) - delta)
                dsb = ds.astype(MXU)
                dk_acc[pl.ds(k0, TQ), :] += _dot_tn(dsb, qm)
                dv_acc[pl.ds(k0, TQ), :] += _dot_tn(p.astype(MXU), dom)
                csum = _colsum(ds)
                for j, sl in enumerate(_sub_blocks()):
                    dfk_ref[kb * len(_sub_blocks()) + j, pl.ds(h, 1), :] += -csum[:, sl]
                return dq + _dot(dsb, k), rsum + jnp.sum(ds, axis=1, keepdims=True)

            carry = lax.fori_loop(0, i, lambda kb, cr: step(kb, cr, False), (jnp.zeros((TQ, LANE), F32), jnp.zeros((TQ, 1), F32)))
            dq, rsum = step(i, carry, True)
            outs.append(dq)
            dfq = jnp.where(lane == h, rsum, dfq)
        dq_ref[...] = (jnp.where(lane < HEAD_DIM, outs[0], outs[1]) * QK_SCALE).astype(dq_ref.dtype)
        dfq_ref[0] = dfq

        @pl.when(i == nq - 1)
        def _():
            dk_ref[...] = dk_acc[...].astype(dk_ref.dtype)
            dv_ref[...] = dv_acc[...].astype(dv_ref.dtype)

    blk = pl.BlockSpec((TQ, LANE), lambda hp, i: (i, hp))
    whole = pl.BlockSpec((s, LANE), lambda hp, i: (0, hp))
    pair = pl.BlockSpec((2, TQ, LANE), lambda hp, i: (hp, i, 0))
    fkt_spec = pl.BlockSpec((nb, N_FGATE, LANE), lambda hp, i: (0, 0, 0))
    return pl.pallas_call(
        body, name="fox_bwd", grid=(4, nq),
        in_specs=_attn_specs(s, 12) + [pl.BlockSpec((TQ, LANE), lambda hp, i: (i, 0)), fkt_spec, blk, blk, pair],
        out_specs=[blk, whole, whole, fkt_spec, pl.BlockSpec((1, TQ, LANE), lambda hp, i: (hp, i, 0))],
        out_shape=[jax.ShapeDtypeStruct((s, WIDTH), MXU)] * 3
        + [jax.ShapeDtypeStruct((nb, N_FGATE, LANE), F32), jax.ShapeDtypeStruct((4, s, LANE), F32)],
        scratch_shapes=[pltpu.VMEM((s, LANE), F32), pltpu.VMEM((s, LANE), F32)],
    )(qkv, qkv, qkv, fc, fkt, do, o, lse)


def _fcum_bwd(dfkt, dfq, fl, bf):
    s = fl.shape[0]
    nb = s // LANE

    def body(dfkt_ref, dfq_ref, fl_ref, bf_ref, df_ref, dbf_ref, tail_ref):
        @pl.when(pl.program_id(0) == 0)
        def _():
            tail_ref[...] = jnp.zeros_like(tail_ref)
            dbf_ref[...] = jnp.zeros_like(dbf_ref)

        r = lax.broadcasted_iota(jnp.int32, (LANE, LANE), 0)
        c = lax.broadcasted_iota(jnp.int32, (LANE, LANE), 1)
        tri = (c >= r).astype(F32)
        dfc = jnp.concatenate([dfkt_ref[0], jnp.zeros((LANE - N_FGATE, LANE), F32)], axis=0).T
        dfc = dfc + ((dfq_ref[0] + dfq_ref[1]) + (dfq_ref[2] + dfq_ref[3]))
        dls = jnp.dot(tri, dfc, precision=lax.Precision.HIGHEST, preferred_element_type=F32) + tail_ref[...]
        xb = fl_ref[...] + bf_ref[...]
        e = jnp.exp(-jnp.abs(xb))
        dfl = dls * (jnp.where(xb >= 0.0, e, 1.0) / (1.0 + e))
        df_ref[...] = dfl.astype(df_ref.dtype)
        tail_ref[...] = dls[0:1, :]
        dbf_ref[...] += _colsum(dfl)

    return pl.pallas_call(
        body, name="fcum_bwd", grid=(nb,),
        in_specs=[pl.BlockSpec((1, N_FGATE, LANE), lambda j: (nb - 1 - j, 0, 0)), pl.BlockSpec((4, LANE, LANE), lambda j: (0, nb - 1 - j, 0)),
                  pl.BlockSpec((LANE, LANE), lambda j: (nb - 1 - j, 0)), _fixed(1, LANE)],
        out_specs=[pl.BlockSpec((LANE, LANE), lambda j: (nb - 1 - j, 0)), _fixed(1, LANE)],
        out_shape=[jax.ShapeDtypeStruct((s, LANE), MXU), jax.ShapeDtypeStruct((1, LANE), F32)],
        scratch_shapes=[pltpu.VMEM((1, LANE), F32)],
    )(dfkt, dfq, fl, bf)


def _mix_fwd(x, o_sb, o_fx, gl, w_sb, w_fx, w_o, g1, ln1_g, ln1_b, sh2, sc2):
    s = x.shape[0]
    tm = 256

    def body(x_ref, osb_ref, ofx_ref, gl_ref, wsb_ref, wfx_ref, wo_ref, g1_ref, lg_ref, lb_ref, sh_ref, sc_ref, r1_ref, u2_ref):
        mixin = (_sigmoid(gl_ref[:, :D]) * _dot(osb_ref[...], wsb_ref[...])
                 + _sigmoid(gl_ref[:, D:]) * _dot(ofx_ref[...], wfx_ref[...]))
        r1 = ALPHA * x_ref[...] + g1_ref[...] * _dot(mixin.astype(MXU), wo_ref[...])
        r1_ref[...] = r1
        x1 = _ln(r1)[0] * lg_ref[...] + lb_ref[...]
        u2_ref[...] = (_ln(x1)[0] * (1.0 + sc_ref[...]) + sh_ref[...]).astype(MXU)

    vec = _fixed(1, D)
    return pl.pallas_call(
        body, name="mix_fwd", grid=(s // tm,),
        in_specs=[_rows(tm, D), _rows(tm, WIDTH), _rows(tm, WIDTH), _rows(tm, 2 * D), _res(w_sb), _res(w_fx), _res(w_o),
                  vec, vec, vec, vec, vec],
        out_specs=[_rows(tm, D), _rows(tm, D)],
        out_shape=[jax.ShapeDtypeStruct((s, D), F32), jax.ShapeDtypeStruct((s, D), MXU)],
        compiler_params=_params(VMEM_BIG),
    )(x, o_sb, o_fx, gl, w_sb, w_fx, w_o, g1, ln1_g, ln1_b, sh2, sc2)


def _ffn_fwd(r1, u2, tgt, w_g, w_u, w_d, g2, ln1_g, ln1_b, ln2_g, ln2_b):
    s = r1.shape[0]
    tm = 256

    def body(r1_ref, u2_ref, t_ref, wg_ref, wu_ref, wd_ref, g2_ref, l1g_ref, l1b_ref, l2g_ref, l2b_ref,
             hg_ref, hu_ref, dxa_ref, dh_ref, acc_ref):
        @pl.when(pl.program_id(0) == 0)
        def _():
            acc_ref[...] = jnp.zeros_like(acc_ref)

        u2 = u2_ref[...]
        hg = _dot_nt(u2, wg_ref[...])
        hu = _dot_nt(u2, wu_ref[...])
        hg_ref[...] = hg
        hu_ref[...] = hu
        h = _dot((hg * _sigmoid(hg) * hu).astype(MXU), wd_ref[...])
        x1 = _ln(r1_ref[...])[0] * l1g_ref[...] + l1b_ref[...]
        xh2, rstd2 = _ln(ALPHA * x1 + g2_ref[...] * h)
        err = xh2 * l2g_ref[...] + l2b_ref[...] - t_ref[...]
        dy = err * (1.0 / D)
        dr2 = _ln_bwd(dy * l2g_ref[...], xh2, rstd2)
        dxa_ref[...] = ALPHA * dr2
        dh_ref[...] = (g2_ref[...] * dr2).astype(MXU)
        acc_ref[0:1, :] += _colsum(dr2 * h)
        acc_ref[1:2, :] += _colsum(dy * xh2)
        acc_ref[2:3, :] += _colsum(dy)
        acc_ref[3:4, :] += _colsum(err * err) * (0.5 / D)

    vec = _fixed(1, D)
    return pl.pallas_call(
        body, name="ffn_fwd", grid=(s // tm,),
        in_specs=[_rows(tm, D), _rows(tm, D), _rows(tm, D), _res(w_g), _res(w_u), _res(w_d), vec, vec, vec, vec, vec],
        out_specs=[_rows(tm, D_FF), _rows(tm, D_FF), _rows(tm, D), _rows(tm, D), _fixed(8, D)],
        out_shape=[jax.ShapeDtypeStruct((s, D_FF), F32), jax.ShapeDtypeStruct((s, D_FF), F32),
                   jax.ShapeDtypeStruct((s, D), F32), jax.ShapeDtypeStruct((s, D), MXU), jax.ShapeDtypeStruct((8, D), F32)],
        compiler_params=_params(VMEM_BIG),
    )(r1, u2, tgt, w_g, w_u, w_d, g2, ln1_g, ln1_b, ln2_g, ln2_b)


def _ffn_bwd(dh, hg, hu, w_g, w_u, w_d):
    s = dh.shape[0]
    tm = 256
    half = D_FF // 2

    def body(dh_ref, hg_ref, hu_ref, wg_ref, wu_ref, wd_ref, act_ref, dhg_ref, dhu_ref, du2_ref):
        dh = dh_ref[...]
        du2 = jnp.zeros((tm, D), F32)
        for c0 in (0, half):
            cols = slice(c0, c0 + half)
            dact = _dot_nt(dh, wd_ref[cols, :])
            hg = hg_ref[:, cols]
            hu = hu_ref[:, cols]
            sg = _sigmoid(hg)
            sl = hg * sg
            act_ref[:, cols] = (sl * hu).astype(MXU)
            dhg = (dact * hu * (sg * (1.0 + hg * (1.0 - sg)))).astype(MXU)
            dhu = (dact * sl).astype(MXU)
            dhg_ref[:, cols] = dhg
            dhu_ref[:, cols] = dhu
            du2 = du2 + _dot(dhg, wg_ref[cols, :]) + _dot(dhu, wu_ref[cols, :])
        du2_ref[...] = du2

    return pl.pallas_call(
        body, name="ffn_bwd", grid=(s // tm,),
        in_specs=[_rows(tm, D), _rows(tm, D_FF), _rows(tm, D_FF), _res(w_g), _res(w_u), _res(w_d)],
        out_specs=[_rows(tm, D_FF), _rows(tm, D_FF), _rows(tm, D_FF), _rows(tm, D)],
        out_shape=[jax.ShapeDtypeStruct((s, D_FF), MXU)] * 3 + [jax.ShapeDtypeStruct((s, D), F32)],
        compiler_params=_params(VMEM_BIG),
    )(dh, hg, hu, w_g, w_u, w_d)


def _mix_bwd(du2, dxa, r1, o_sb, o_fx, gl, w_sb, w_fx, w_o, g1, ln1_g, ln1_b, sc2):
    s = r1.shape[0]
    tm = 256

    def body(du2_ref, dxa_ref, r1_ref, osb_ref, ofx_ref, gl_ref, wsb_ref, wfx_ref, wo_ref, g1_ref, lg_ref, lb_ref, sc_ref,
             dx_ref, mixin_ref, dmix_ref, dysb_ref, dyfx_ref, dosb_ref, dofx_ref, dgl_ref, dbg_ref, acc_ref):
        @pl.when(pl.program_id(0) == 0)
        def _():
            acc_ref[...] = jnp.zeros_like(acc_ref)
            dbg_ref[...] = jnp.zeros_like(dbg_ref)

        du2 = du2_ref[...]
        xh1, rstd1 = _ln(r1_ref[...])
        x1 = xh1 * lg_ref[...] + lb_ref[...]
        n1, rstdn = _ln(x1)
        dx1 = dxa_ref[...] + _ln_bwd(du2 * (1.0 + sc_ref[...]), n1, rstdn)
        dr1 = _ln_bwd(dx1 * lg_ref[...], xh1, rstd1)
        dx_ref[...] = ALPHA * dr1
        ysb = _dot(osb_ref[...], wsb_ref[...])
        yfx = _dot(ofx_ref[...], wfx_ref[...])
        gs = _sigmoid(gl_ref[:, :D])
        gf = _sigmoid(gl_ref[:, D:])
        mixin = (gs * ysb + gf * yfx).astype(MXU)
        mixin_ref[...] = mixin
        mix = _dot(mixin, wo_ref[...])
        dmix = (g1_ref[...] * dr1).astype(MXU)
        dmix_ref[...] = dmix
        dmixin = _dot_nt(dmix, wo_ref[...])
        dysb = (dmixin * gs).astype(MXU)
        dyfx = (dmixin * gf).astype(MXU)
        dysb_ref[...] = dysb
        dyfx_ref[...] = dyfx
        dosb_ref[...] = _dot_nt(dysb, wsb_ref[...]).astype(MXU)
        dofx_ref[...] = _dot_nt(dyfx, wfx_ref[...]).astype(MXU)
        dgs = dmixin * ysb * gs * (1.0 - gs)
        dgf = dmixin * yfx * gf * (1.0 - gf)
        dgl_ref[:, :D] = dgs.astype(MXU)
        dgl_ref[:, D:] = dgf.astype(MXU)
        dbg_ref[:, :D] += _colsum(dgs)
        dbg_ref[:, D:] += _colsum(dgf)
        acc_ref[0:1, :] += _colsum(du2)
        acc_ref[1:2, :] += _colsum(du2 * n1)
        acc_ref[2:3, :] += _colsum(dx1 * xh1)
        acc_ref[3:4, :] += _colsum(dx1)
        acc_ref[4:5, :] += _colsum(dr1 * mix)

    vec = _fixed(1, D)
    return pl.pallas_call(
        body, name="mix_bwd", grid=(s // tm,),
        in_specs=[_rows(tm, D), _rows(tm, D), _rows(tm, D), _rows(tm, WIDTH), _rows(tm, WIDTH), _rows(tm, 2 * D),
                  _res(w_sb), _res(w_fx), _res(w_o), vec, vec, vec, vec],
        out_specs=[_rows(tm, D), _rows(tm, D), _rows(tm, D), _rows(tm, D), _rows(tm, D), _rows(tm, WIDTH), _rows(tm, WIDTH),
                   _rows(tm, 2 * D), _fixed(1, 2 * D), _fixed(8, D)],
        out_shape=[jax.ShapeDtypeStruct((s, D), F32)] + [jax.ShapeDtypeStruct((s, D), MXU)] * 4
        + [jax.ShapeDtypeStruct((s, WIDTH), MXU)] * 2
        + [jax.ShapeDtypeStruct((s, 2 * D), MXU), jax.ShapeDtypeStruct((1, 2 * D), F32), jax.ShapeDtypeStruct((8, D), F32)],
        compiler_params=_params(VMEM_BIG),
    )(du2, dxa, r1, o_sb, o_fx, gl, w_sb, w_fx, w_o, g1, ln1_g, ln1_b, sc2)


def _in_bwd(pieces, x, dxa, w_all, sc1):
    s = x.shape[0]
    tm = 256
    n_p = len(pieces)

    def body(*refs):
        p_refs = refs[:n_p]
        x_ref, dxa_ref, w_ref, sc_ref, gx_ref, acc_ref = refs[n_p:]

        @pl.when(pl.program_id(0) == 0)
        def _():
            acc_ref[...] = jnp.zeros_like(acc_ref)

        du1 = jnp.zeros((tm, D), F32)
        for p_ref, (arr, c0) in zip(p_refs, pieces):
            du1 = du1 + _dot(p_ref[...], w_ref[c0:c0 + arr.shape[1], :])
        n0, rstd0 = _ln(x_ref[...])
        gx_ref[...] = dxa_ref[...] + _ln_bwd(du1 * (1.0 + sc_ref[...]), n0, rstd0)
        acc_ref[0:1, :] += _colsum(du1)
        acc_ref[1:2, :] += _colsum(du1 * n0)

    return pl.pallas_call(
        body, name="in_bwd", grid=(s // tm,),
        in_specs=[_rows(tm, a.shape[1]) for a, _ in pieces] + [_rows(tm, D), _rows(tm, D), _res(w_all), _fixed(1, D)],
        out_specs=[_rows(tm, D), _fixed(8, D)],
        out_shape=[jax.ShapeDtypeStruct((s, D), F32), jax.ShapeDtypeStruct((8, D), F32)],
        compiler_params=_params(VMEM_BIG),
    )(*[a for a, _ in pieces], x, dxa, w_all, sc1)


def _matmul_tn(a, b, name):
    s, m = a.shape
    n = b.shape[1]
    tm = 512 if m % 512 == 0 else (m if m < 512 else m // 2)
    tn = n // 2 if n > 2048 else n
    ts = 512
    assert m % tm == 0 and tm % LANE == 0 and n % tn == 0 and tn % LANE == 0 and s % ts == 0

    def body(a_ref, b_ref, o_ref):
        @pl.when(pl.program_id(2) == 0)
        def _():
            o_ref[...] = jnp.zeros_like(o_ref)

        o_ref[...] += _dot_tn(a_ref[...], b_ref[...])

    return pl.pallas_call(
        body, name=name, grid=(m // tm, n // tn, s // ts),
        in_specs=[pl.BlockSpec((ts, tm), lambda i, j, k: (k, i)), pl.BlockSpec((ts, tn), lambda i, j, k: (k, j))],
        out_specs=pl.BlockSpec((tm, tn), lambda i, j, k: (i, j)),
        out_shape=jax.ShapeDtypeStruct((m, n), F32),
        compiler_params=_params(VMEM_BIG),
    )(a, b)


def _local_step(x, tgt, ada, w_all, b_gate, bf_pad, w_sb, w_fx, w_o, ln1_g, ln1_b, w_g, w_u, w_d, ln2_g, ln2_b):
    sh1, sc1, g1, sh2, sc2, g2 = ada
    u1, qkv, fl, gl = _in_proj(x, sh1, sc1, w_all, b_gate)
    fc, fkt = _fcum_fwd(fl, bf_pad)
    o_sb, rs = _sb_fwd(qkv)
    o_fx, lse = _fox_fwd(qkv, fc, fkt)
    r1, u2 = _mix_fwd(x, o_sb, o_fx, gl, w_sb, w_fx, w_o, g1, ln1_g, ln1_b, sh2, sc2)
    hg, hu, dxa2, dh, acc_f = _ffn_fwd(r1, u2, tgt, w_g, w_u, w_d, g2, ln1_g, ln1_b, ln2_g, ln2_b)
    act, dhg, dhu, du2 = _ffn_bwd(dh, hg, hu, w_g, w_u, w_d)
    dxa1, mixin, dmix, dysb, dyfx, dosb, dofx, dgl, dbg, acc_m = _mix_bwd(
        du2, dxa2, r1, o_sb, o_fx, gl, w_sb, w_fx, w_o, g1, ln1_g, ln1_b, sc2)
    dq_sb, dk_sb, dv_sb = _sb_bwd(qkv, dosb, rs)
    dq_fx, dk_fx, dv_fx, dfkt, dfq = _fox_bwd(qkv, fc, fkt, dofx, o_fx, lse)
    df, dbf = _fcum_bwd(dfkt, dfq, fl, bf_pad)
    pieces = [(dq_sb, 0), (dk_sb, WIDTH), (dv_sb, 2 * WIDTH), (dq_fx, 3 * WIDTH), (dk_fx, 4 * WIDTH), (dv_fx, 5 * WIDTH),
              (df, OFF_FGATE), (dgl, OFF_FGATE + LANE)]
    grad_x, acc_i = _in_bwd(pieces, x, dxa1, w_all, sc1)
    dw_in = [_matmul_tn(p, u1, f"dw_in_{j}") for j, (p, _) in enumerate(pieces)]
    return dict(
        loss_lanes=acc_f[3:4], grad_x=grad_x, dw_in=dw_in,
        dw_sb=_matmul_tn(o_sb, dysb, "dw_sb_out"), dw_fx=_matmul_tn(o_fx, dyfx, "dw_fox_out"),
        dw_o=_matmul_tn(mixin, dmix, "dw_o"),
        dw_g=_matmul_tn(dhg, u2, "dw_ffn_gate"), dw_u=_matmul_tn(dhu, u2, "dw_ffn_up"), dw_d=_matmul_tn(act, dh, "dw_ffn_down"),
        d_ada=[acc_i[0:1], acc_i[1:2], acc_m[4:5], acc_m[0:1], acc_m[1:2], acc_f[0:1]],
        dln1_g=acc_m[2:3], dln1_b=acc_m[3:4], dln2_g=acc_f[1:2], dln2_b=acc_f[2:3], db_gate=dbg, db_forget=dbf)


_MESH_ID = pl.DeviceIdType.MESH
_ANY = pl.BlockSpec(memory_space=pl.ANY)
_VMEM = pl.BlockSpec(memory_space=pltpu.VMEM)


def _mesh_pos():
    return lax.axis_index("x"), lax.axis_index("y"), lax.axis_index("c")


def _other_chips(x, y):
    return [(1 - x, y), (x, 1 - y), (1 - x, 1 - y)]


def _allgather_rows(v, name):
    n = v.shape[1]

    def body(v_ref, out_ref, send_sems, recv_sems, local_sem):
        x, y, c = _mesh_pos()
        me = 4 * x + 2 * y + c
        mine = pltpu.make_async_copy(v_ref, out_ref.at[me], local_sem)
        mine.start()
        copies = []
        for d in range(1, 8):
            fx, fy, fc = (d >> 2) & 1, (d >> 1) & 1, d & 1
            to = (1 - x if fx else x, 1 - y if fy else y, 1 - c if fc else c)
            cp = pltpu.make_async_remote_copy(src_ref=v_ref, dst_ref=out_ref.at[me], send_sem=send_sems.at[d - 1],
                                              recv_sem=recv_sems.at[d - 1], device_id=to, device_id_type=_MESH_ID)
            cp.start()
            copies.append(cp)
        for cp in copies:
            cp.wait_recv()
        for cp in copies:
            cp.wait_send()
        mine.wait()

    return pl.pallas_call(
        body, name=name, in_specs=[_VMEM], out_specs=_VMEM,
        out_shape=jax.ShapeDtypeStruct((8, 1, n), v.dtype),
        scratch_shapes=[pltpu.SemaphoreType.DMA((7,)), pltpu.SemaphoreType.DMA((7,)), pltpu.SemaphoreType.DMA(())],
    )(v)


def _chip_exchange(arrays, name, gather):
    nt = len(arrays)

    def body(*refs):
        ins, outs = refs[:nt], refs[nt:2 * nt]
        send_sems, recv_sems, local_sems = refs[2 * nt:]
        x, y, c = _mesh_pos()
        me = 2 * x + y
        copies = []
        for t in range(nt):
            src = ins[t] if gather else ins[t].at[me]
            cp = pltpu.make_async_copy(src, outs[t].at[me], local_sems.at[t])
            cp.start()
            copies.append(cp)
        remote = []
        for t in range(nt):
            for j, (px, py) in enumerate(_other_chips(x, y)):
                src = ins[t] if gather else ins[t].at[2 * px + py]
                cp = pltpu.make_async_remote_copy(src_ref=src, dst_ref=outs[t].at[me], send_sem=send_sems.at[3 * t + j],
                                                  recv_sem=recv_sems.at[3 * t + j], device_id=(px, py, c), device_id_type=_MESH_ID)
                cp.start()
                remote.append(cp)
        for cp in remote:
            cp.wait_recv()
        for cp in remote:
            cp.wait_send()
        for cp in copies:
            cp.wait()

    out_shape = [jax.ShapeDtypeStruct((4,) + a.shape[-2:], a.dtype) for a in arrays]
    return pl.pallas_call(
        body, name=name, in_specs=[_ANY] * nt, out_specs=[_ANY] * nt, out_shape=out_shape,
        scratch_shapes=[pltpu.SemaphoreType.DMA((3 * nt,)), pltpu.SemaphoreType.DMA((3 * nt,)), pltpu.SemaphoreType.DMA((nt,))],
    )(*arrays)


def _sibling_exchange(arrays, name):
    nt = len(arrays)

    def body(*refs):
        ins, outs = refs[:nt], refs[nt:2 * nt]
        send_sems, recv_sems = refs[2 * nt:]
        x, y, c = _mesh_pos()
        copies = []
        for t in range(nt):
            cp = pltpu.make_async_remote_copy(src_ref=ins[t], dst_ref=outs[t], send_sem=send_sems.at[t], recv_sem=recv_sems.at[t],
                                              device_id=(x, y, 1 - c), device_id_type=_MESH_ID)
            cp.start()
            copies.append(cp)
        for cp in copies:
            cp.wait_recv()
        for cp in copies:
            cp.wait_send()

    return pl.pallas_call(
        body, name=name, in_specs=[_ANY] * nt, out_specs=[_ANY] * nt,
        out_shape=[jax.ShapeDtypeStruct(a.shape, a.dtype) for a in arrays],
        scratch_shapes=[pltpu.SemaphoreType.DMA((nt,)), pltpu.SemaphoreType.DMA((nt,))],
    )(*arrays)


def _tiles(r, n):
    for tr in (256, 352, 128):
        if r % tr == 0:
            return tr, n, r // tr, lambda i: (i, 0)
    assert n % 256 == 0
    return r, 256, n // 256, lambda i: (0, i)


def _reduce_chips(own, recv, name):
    r, n = own.shape
    tr, tn, steps, at = _tiles(r, n)

    def body(own_ref, recv_ref, out_ref):
        x, y, _ = _mesh_pos()
        me = 2 * x + y
        total = jnp.zeros((tr, tn), F32)
        for k in range(4):
            total = total + jnp.where(me == k, own_ref[...], recv_ref[k].astype(F32))
        out_ref[...] = total

    blk = pl.BlockSpec((tr, tn), at)
    return pl.pallas_call(
        body, name=name, grid=(steps,),
        in_specs=[blk, pl.BlockSpec((4, tr, tn), lambda i: (0,) + at(i))], out_specs=blk,
        out_shape=jax.ShapeDtypeStruct((r, n), F32),
    )(own, recv)


def _adamw_math(w, g, m, v):
    m = ADAM_B1 * m + (1.0 - ADAM_B1) * g
    v = ADAM_B2 * v + (1.0 - ADAM_B2) * (g * g)
    m_hat = m / (1.0 - ADAM_B1 ** ADAM_STEP)
    v_hat = v / (1.0 - ADAM_B2 ** ADAM_STEP)
    return -ADAM_LR * (m_hat / (jnp.sqrt(v_hat) + ADAM_EPS) + ADAM_WD * w), m, v


def _adamw(w, m, v, g_parts, name):
    r, n = w.shape
    tr, tn, steps, at = _tiles(r, n)
    blk = pl.BlockSpec((tr, tn), at)
    ng = len(g_parts)

    def body(*refs):
        w_ref, m_ref, v_ref = refs[:3]
        g_refs = refs[3:3 + ng]
        g_out, d_out, m_out, v_out = refs[3 + ng:]
        g = g_refs[0][...]
        for gr in g_refs[1:]:
            g = g + gr[...]
        g_out[...] = g
        d_out[...], m_out[...], v_out[...] = _adamw_math(w_ref[...], g, m_ref[...], v_ref[...])

    return pl.pallas_call(
        body, name=name, grid=(steps,),
        in_specs=[blk] * (3 + ng), out_specs=[blk] * 4,
        out_shape=[jax.ShapeDtypeStruct((r, n), F32)] * 4,
    )(w, m, v, *g_parts)


def _ada_fwd(c_all, w_shard, b_shard):
    n = w_shard.shape[1]
    tn = 512

    def body(c_ref, w_ref, b_ref, o_ref):
        cv = c_ref[...]
        ca = (cv * _sigmoid(cv)).astype(MXU)
        o_ref[...] = _dot(ca, w_ref[...].astype(MXU)) + b_ref[...]

    return pl.pallas_call(
        body, name="ada_fwd", grid=(n // tn,),
        in_specs=[_fixed(8, D), pl.BlockSpec((D, tn), lambda j: (0, j)), pl.BlockSpec((1, tn), lambda j: (0, j))],
        out_specs=pl.BlockSpec((8, tn), lambda j: (0, j)),
        out_shape=jax.ShapeDtypeStruct((8, n), F32),
    )(c_all, w_shard, b_shard)


def _ada_bwd(c_all, dada_shard):
    n = dada_shard.shape[1]
    tn = 512

    def body(c_ref, d_ref, o_ref):
        cv = c_ref[...]
        ca = (cv * _sigmoid(cv)).astype(MXU)
        o_ref[...] = _dot_tn(ca, d_ref[...].astype(MXU))

    return pl.pallas_call(
        body, name="ada_bwd", grid=(n // tn,),
        in_specs=[_fixed(8, D), pl.BlockSpec((8, tn), lambda j: (0, j))],
        out_specs=pl.BlockSpec((D, tn), lambda j: (0, j)),
        out_shape=jax.ShapeDtypeStruct((D, n), F32),
    )(c_all, dada_shard)


_SMALL = [("d_ada", N_COND * D), ("ln1_g", D), ("ln1_b", D), ("ln2_g", D), ("ln2_b", D), ("b_gate", 2 * D), ("b_forget", LANE),
          ("loss", D)]
_SMALL_OFF = {}
_o = 0
for _n, _w in _SMALL:
    _SMALL_OFF[_n] = (_o, _w)
    _o += _w
_SMALL_LEN = _o
_SMALL_PARAMS = [("b_ada", "d_ada", N_COND * D), ("b_gate", "b_gate", 2 * D), ("b_forget", "b_forget", N_FGATE),
                 ("ln1_g", "ln1_g", D), ("ln1_b", "ln1_b", D), ("ln2_g", "ln2_g", D), ("ln2_b", "ln2_b", D)]


def _small_update(rows, params):
    npar = len(_SMALL_PARAMS)

    def body(*refs):
        rows_ref = refs[0]
        p_refs = refs[1:1 + 3 * npar]
        loss_ref = refs[1 + 3 * npar]
        o_refs = refs[2 + 3 * npar:]
        total = rows_ref[0]
        for d in range(1, 8):
            total = total + rows_ref[d]
        lo, lw = _SMALL_OFF["loss"]
        loss_ref[...] = jnp.sum(total[:, lo:lo + lw], axis=1, keepdims=True)
        for j, (_, key, n) in enumerate(_SMALL_PARAMS):
            off = _SMALL_OFF[key][0]
            g = total[:, off:off + n]
            w_ref, m_ref, v_ref = p_refs[3 * j:3 * j + 3]
            o_refs[4 * j][...] = g
            o_refs[4 * j + 1][...], o_refs[4 * j + 2][...], o_refs[4 * j + 3][...] = _adamw_math(w_ref[...], g, m_ref[...], v_ref[...])

    flat = [a for p in params for a in p]
    out_shape = [jax.ShapeDtypeStruct((1, 1), F32)] + [jax.ShapeDtypeStruct((1, n), F32) for _, _, n in _SMALL_PARAMS for _ in range(4)]
    return pl.pallas_call(body, name="small_update", out_shape=out_shape)(rows, *flat)


_BIG = [("w_in", "cols_t"), ("w_sb_out", "cols"), ("w_fox_out", "cols"), ("w_o", "rows"),
        ("w_ffn_gate", "cols_t"), ("w_ffn_up", "cols_t"), ("w_ffn_down", "rows")]


def _shard2d(a, how):
    return a[0].T if how == "cols_t" else a[0]


def _unshard(g, how):
    if how == "cols":
        return g.transpose(1, 0, 2).reshape(g.shape[1], 4 * g.shape[2])
    return g.reshape(4 * g.shape[1], g.shape[2])


def _reshard(w, how):
    if how == "cols":
        return w.reshape(w.shape[0], 4, w.shape[1] // 4).transpose(1, 0, 2)
    return w.reshape(4, w.shape[0] // 4, w.shape[1])


def kernel(x, c, w_ada, b_ada, w_in, b_gate, b_forget, w_sb_out, w_fox_out, w_o, ln1_g, ln1_b, w_ffn_gate, w_ffn_up, w_ffn_down, ln2_g, ln2_b, loss_target, m_w_ada, m_b_ada, m_w_in, m_b_gate, m_b_forget, m_w_sb_out, m_w_fox_out, m_w_o, m_ln1_g, m_ln1_b, m_w_ffn_gate, m_w_ffn_up, m_w_ffn_down, m_ln2_g, m_ln2_b, v_w_ada, v_b_ada, v_w_in, v_b_gate, v_b_forget, v_w_sb_out, v_w_fox_out, v_w_o, v_ln1_g, v_ln1_b, v_w_ffn_gate, v_w_ffn_up, v_w_ffn_down, v_ln2_g, v_ln2_b):
    given = dict(locals())
    mx, my, mc = _mesh_pos()
    chip = 2 * mx + my
    seq = 4 * mx + 2 * my + mc

    c_all = _allgather_rows(c, "gather_c").reshape(8, D)
    n_ada = w_ada.shape[2]
    b_ada_shard = lax.dynamic_slice(b_ada, (0, chip * n_ada), (1, n_ada))
    ada_part = _ada_fwd(c_all, w_ada[0], b_ada_shard)
    ada_all = _allgather_rows(ada_part.reshape(1, 8 * n_ada), "gather_ada").reshape(4, 2, 8, n_ada)
    ada_row = lax.dynamic_slice(ada_all, (0, mc, seq, 0), (4, 1, 1, n_ada)).reshape(1, N_COND * D)
    ada = [ada_row[:, j * D:(j + 1) * D] for j in range(N_COND)]

    gathered = _chip_exchange([_shard2d(given[n], how).astype(MXU) for n, how in _BIG], "gather_weights", gather=True)
    full = {n: _unshard(g, how) for (n, how), g in zip(_BIG, gathered)}
    wi = full["w_in"]
    w_all = jnp.concatenate([wi[:OFF_FGATE + N_FGATE], jnp.zeros((LANE - N_FGATE, D), MXU), wi[OFF_FGATE + N_FGATE:]], axis=0)
    bf_pad = jnp.concatenate([b_forget, jnp.zeros((1, LANE - N_FGATE), F32)], axis=1)

    out = _local_step(x[0], loss_target[0], ada, w_all, b_gate, bf_pad, full["w_sb_out"], full["w_fox_out"], full["w_o"],
                      ln1_g, ln1_b, full["w_ffn_gate"], full["w_ffn_up"], full["w_ffn_down"], ln2_g, ln2_b)

    row = jnp.concatenate(out["d_ada"] + [out["dln1_g"], out["dln1_b"], out["dln2_g"], out["dln2_b"], out["db_gate"],
                                          out["db_forget"], out["loss_lanes"]], axis=1)
    rows = _allgather_rows(row, "gather_small")
    small = _small_update(rows, [(given[p], given["m_" + p], given["v_" + p]) for p, _, _ in _SMALL_PARAMS])
    loss = small[0].reshape(())
    res = {}
    for j, (p, _, _) in enumerate(_SMALL_PARAMS):
        res[p] = small[1 + 4 * j:5 + 4 * j]

    dada_all = rows.reshape(8, _SMALL_LEN)[:, :N_COND * D]
    dada_shard = lax.dynamic_slice(dada_all, (0, chip * n_ada), (8, n_ada))
    g_ada = _ada_bwd(c_all, dada_shard)
    res["w_ada"] = [a[None] for a in _adamw(w_ada[0], m_w_ada[0], v_w_ada[0], [g_ada], "adamw_w_ada")]

    dwi = out["dw_in"]
    dw_full = {"w_in": jnp.concatenate(dwi[:6] + [dwi[6][:N_FGATE], dwi[7]], axis=0), "w_sb_out": out["dw_sb"],
               "w_fox_out": out["dw_fx"], "w_o": out["dw_o"], "w_ffn_gate": out["dw_g"], "w_ffn_up": out["dw_u"],
               "w_ffn_down": out["dw_d"]}
    pieces = [_reshard(dw_full[n], how) for n, how in _BIG]
    received = _chip_exchange([p.astype(MXU) for p in pieces], "scatter_grads", gather=False)
    partial = [_reduce_chips(lax.dynamic_index_in_dim(p, chip, 0, keepdims=False), r, "reduce_" + n)
               for (n, _), p, r in zip(_BIG, pieces, received)]
    theirs = _sibling_exchange(partial, "swap_cores")
    for (n, how), mine, other in zip(_BIG, partial, theirs):
        upd = _adamw(_shard2d(given[n], how), _shard2d(given["m_" + n], how), _shard2d(given["v_" + n], how), [mine, other], "adamw_" + n)
        res[n] = [(a.T if how == "cols_t" else a)[None] for a in upd]

    order = ["w_ada", "b_ada", "w_in", "b_gate", "b_forget", "w_sb_out", "w_fox_out", "w_o", "ln1_g", "ln1_b",
             "w_ffn_gate", "w_ffn_up", "w_ffn_down", "ln2_g", "ln2_b"]
    return (loss, out["grad_x"][None], *[res[n][0] for n in order], *[res[n][1] for n in order],
            *[res[n][2] for n in order], *[res[n][3] for n in order])
```

```python
import functools

import jax
import jax.numpy as jnp
from jax import lax
from jax.experimental import pallas as pl
from jax.experimental.pallas import tpu as pltpu

F32 = jnp.float32
MXU = jnp.bfloat16

D = 1024
HEAD_DIM = 64
WIDTH = 512
D_FF = 2816
N_COND = 6
LN_EPS = 1e-5
ALPHA = 2.0 ** 0.25
QK_SCALE = HEAD_DIM ** -0.5
OFF_FGATE = 6 * WIDTH
N_FGATE = 8
IN_COLS = OFF_FGATE + N_FGATE + 2 * D
LANE = 128
W_ALL_COLS = OFF_FGATE + LANE + 2 * D
TQ = 512
ADAM_LR, ADAM_B1, ADAM_B2, ADAM_EPS, ADAM_WD, ADAM_STEP = 0.001, 0.9, 0.999, 1e-08, 0.01, 10
NEG = -1e30
MESH_AXES = ("x", "y", "c")
VMEM_BIG = 56 * 1024 * 1024


def _dot(a, b):
    return jnp.dot(a, b, preferred_element_type=F32)


def _dot_nt(a, b):
    return lax.dot_general(a, b, (((1,), (1,)), ((), ())), preferred_element_type=F32)


def _dot_tn(a, b):
    return lax.dot_general(a, b, (((0,), (0,)), ((), ())), preferred_element_type=F32)


def _ln(x):
    mu = jnp.mean(x, axis=-1, keepdims=True)
    xc = x - mu
    var = jnp.mean(xc * xc, axis=-1, keepdims=True)
    rstd = lax.rsqrt(var + LN_EPS)
    return xc * rstd, rstd


def _ln_bwd(dxhat, xhat, rstd):
    return rstd * (dxhat - jnp.mean(dxhat, axis=-1, keepdims=True) - xhat * jnp.mean(dxhat * xhat, axis=-1, keepdims=True))


def _sigmoid(x):
    return 1.0 / (1.0 + jnp.exp(-x))


def _colsum(x):
    return jnp.sum(x, axis=0, keepdims=True)


def _split(x):
    hi = x.astype(MXU)
    lo = (x - hi.astype(F32)).astype(MXU)
    return jnp.concatenate([hi, lo], axis=1)


def _rows(tm, n):
    return pl.BlockSpec((tm, n), lambda i: (i, 0))


def _fixed(r, n):
    return pl.BlockSpec((r, n), lambda i: (0, 0))


def _res(a):
    return pl.BlockSpec(a.shape, lambda i: (0, 0), pipeline_mode=pl.Buffered(1))


def _params(limit=None, sem=None):
    return pltpu.CompilerParams(vmem_limit_bytes=limit, dimension_semantics=sem)


def _in_proj(x, sh1, sc1, w_all, b_gate):
    s = x.shape[0]
    tm = 256

    def body(x_ref, sh_ref, sc_ref, w_ref, bg_ref, u_ref, qkv_ref, fl_ref, gl_ref):
        xhat, _ = _ln(x_ref[...])
        u = (xhat * (1.0 + sc_ref[...]) + sh_ref[...]).astype(MXU)
        u_ref[...] = u
        for c0 in range(0, OFF_FGATE, WIDTH):
            p = _dot_nt(u, w_ref[c0:c0 + WIDTH, :])
            if c0 in (0, 3 * WIDTH):
                p = p * QK_SCALE
            qkv_ref[:, c0:c0 + WIDTH] = p.astype(MXU)
        fl_ref[...] = _dot_nt(u, w_ref[OFF_FGATE:OFF_FGATE + LANE, :])
        for c0 in range(0, 2 * D, D):
            gl_ref[:, c0:c0 + D] = _dot_nt(u, w_ref[OFF_FGATE + LANE + c0:OFF_FGATE + LANE + c0 + D, :]) + bg_ref[:, c0:c0 + D]

    return pl.pallas_call(
        body, name="in_proj", grid=(s // tm,),
        in_specs=[_rows(tm, D), _fixed(1, D), _fixed(1, D), _res(w_all), _fixed(1, 2 * D)],
        out_specs=[_rows(tm, D), _rows(tm, OFF_FGATE), _rows(tm, LANE), _rows(tm, 2 * D)],
        out_shape=[jax.ShapeDtypeStruct((s, D), MXU), jax.ShapeDtypeStruct((s, OFF_FGATE), MXU),
                   jax.ShapeDtypeStruct((s, LANE), F32), jax.ShapeDtypeStruct((s, 2 * D), F32)],
        compiler_params=_params(VMEM_BIG),
    )(x, sh1, sc1, w_all, b_gate)


def _log_sigmoid_parts(z):
    e = jnp.exp(-jnp.abs(z))
    return -(jnp.maximum(z, 0.0) + jnp.log(1.0 + e)), e


def _fcum_fwd(fl, bf):
    s = fl.shape[0]
    nb = s // LANE

    def body(fl_ref, bf_ref, fc_ref, fkt_ref):
        r = lax.broadcasted_iota(jnp.int32, (LANE, LANE), 0)
        c = lax.broadcasted_iota(jnp.int32, (LANE, LANE), 1)
        tri = (c <= r).astype(F32)

        def step(b, carry):
            r0 = pl.multiple_of(b * LANE, LANE)
            xb = fl_ref[pl.ds(r0, LANE), :] + bf_ref[...]
            ls = _log_sigmoid_parts(-xb)[0]
            cs = jnp.dot(tri, ls, precision=lax.Precision.HIGHEST, preferred_element_type=F32) + carry
            fc_ref[pl.ds(r0, LANE), :] = cs
            fkt_ref[b] = cs.T[:N_FGATE, :]
            return cs[LANE - 1:LANE, :]

        lax.fori_loop(0, nb, step, jnp.zeros((1, LANE), F32))

    return pl.pallas_call(
        body, name="fcum_fwd",
        out_shape=[jax.ShapeDtypeStruct((s, LANE), F32), jax.ShapeDtypeStruct((nb, N_FGATE, LANE), F32)],
    )(fl, bf)


def _attn_specs(s, col0):
    return [pl.BlockSpec((TQ, LANE), lambda hp, i: (i, col0 + hp)),
            pl.BlockSpec((s, LANE), lambda hp, i: (0, col0 + 4 + hp)),
            pl.BlockSpec((s, LANE), lambda hp, i: (0, col0 + 8 + hp))]


def _tile_iotas():
    lane = lax.broadcasted_iota(jnp.int32, (TQ, LANE), 1)
    row = lax.broadcasted_iota(jnp.int32, (TQ, TQ), 0)
    col = lax.broadcasted_iota(jnp.int32, (TQ, TQ), 1)
    return lane, row, col


def _sub_blocks():
    return [slice(j * LANE, (j + 1) * LANE) for j in range(TQ // LANE)]


def _tri(below):
    r = lax.broadcasted_iota(jnp.int32, (LANE, LANE), 0)
    c = lax.broadcasted_iota(jnp.int32, (LANE, LANE), 1)
    t = jnp.concatenate([((r > c) if below else (r < c)).astype(MXU), jnp.ones((LANE, LANE), MXU)], axis=1)
    return jnp.concatenate([t, t], axis=0)


def _call_with_riders(body, name, nq, in_specs, out_specs, out_shape, scratch, args, riders, gather):
    nr, n_in, n_out, n_sc = len(riders), len(in_specs), len(out_specs), len(scratch)

    def wrapped(*refs):
        ins, rin = refs[:n_in], refs[n_in:n_in + nr]
        outs, rout = refs[n_in + nr:n_in + nr + n_out], refs[n_in + nr + n_out:n_in + 2 * nr + n_out]
        own, sems = refs[n_in + 2 * nr + n_out:n_in + 2 * nr + n_out + n_sc], refs[n_in + 2 * nr + n_out + n_sc:]
        if nr:
            @pl.when((pl.program_id(0) == 0) & (pl.program_id(1) == 0))
            def _():
                _exchange_start(rin, rout, sems, gather)

        body(*ins, *outs, *own)
        if nr:
            @pl.when((pl.program_id(0) == 3) & (pl.program_id(1) == nq - 1))
            def _():
                _exchange_wait(rin, rout, sems, gather)

    res = pl.pallas_call(
        wrapped, name=name, grid=(4, nq),
        in_specs=list(in_specs) + [_ANY] * nr, out_specs=list(out_specs) + [_ANY] * nr,
        out_shape=list(out_shape) + _exchange_out_shape(riders),
        scratch_shapes=list(scratch) + (_exchange_sems(nr) if nr else []),
    )(*args, *riders)
    return res[:n_out], res[n_out:]


def _sb_fwd(qkv, riders=()):
    s = qkv.shape[0]
    nq = s // TQ
    assert nq <= LANE

    def body(q_ref, k_ref, v_ref, o_ref, rs_ref):
        i = pl.program_id(1)
        lane, row, col = _tile_iotas()
        u2 = _tri(True)
        diag = col < row
        q = q_ref[...]
        outs = []
        for hh in range(2):
            hm = (lane >= HEAD_DIM) if hh else (lane < HEAD_DIM)
            qm = jnp.where(hm, q, jnp.zeros_like(q))

            def step(kb, carry, masked):
                run, acc, rt = carry
                k0 = pl.multiple_of(kb * TQ, TQ)
                k = k_ref[pl.ds(k0, TQ), :]
                v = v_ref[pl.ds(k0, TQ), :]
                z = _dot_nt(qm, k)
                lneg, _ = _log_sigmoid_parts(z)
                lpos = z + lneg
                if masked:
                    lneg = jnp.where(diag, lneg, 0.0)
                rt = jnp.where(lane == kb, run, rt)
                a = []
                for sl in reversed(_sub_blocks()):
                    st = _dot(_split(lneg[:, sl]), u2)
                    a.append(jnp.exp(lpos[:, sl] + st[:, :LANE] + run))
                    run = run + st[:, LANE:]
                a = jnp.concatenate(a[::-1], axis=1)
                if masked:
                    a = jnp.where(diag, a, 0.0)
                return run, acc + _dot(a.astype(MXU), v), rt

            zero = jnp.zeros((TQ, LANE), F32)
            carry = step(i, (zero, zero, zero), True)
            carry = lax.fori_loop(0, i, lambda j, cr: step(i - 1 - j, cr, False), carry)
            outs.append(carry[1])
            rs_ref[hh] = carry[2]
        o_ref[...] = jnp.where(lane < HEAD_DIM, outs[0], outs[1]).astype(o_ref.dtype)

    return _call_with_riders(
        body, "sb_fwd", nq, _attn_specs(s, 0),
        [pl.BlockSpec((TQ, LANE), lambda hp, i: (i, hp)), pl.BlockSpec((2, TQ, LANE), lambda hp, i: (hp, i, 0))],
        [jax.ShapeDtypeStruct((s, WIDTH), MXU), jax.ShapeDtypeStruct((8, s, LANE), F32)], [], (qkv, qkv, qkv), riders, True)


def _sb_bwd(qkv, do, rs, riders=()):
    s = qkv.shape[0]
    nq = s // TQ

    def body(q_ref, k_ref, v_ref, do_ref, rs_ref, dq_ref, dk_ref, dv_ref, dk_acc, dv_acc):
        i = pl.program_id(1)

        @pl.when(i == 0)
        def _():
            dk_acc[...] = jnp.zeros_like(dk_acc)
            dv_acc[...] = jnp.zeros_like(dv_acc)

        lane, row, col = _tile_iotas()
        u2 = _tri(True)
        l2 = _tri(False)
        diag = col < row
        q = q_ref[...]
        do = do_ref[...]
        outs = []
        for hh in range(2):
            hm = (lane >= HEAD_DIM) if hh else (lane < HEAD_DIM)
            qm = jnp.where(hm, q, jnp.zeros_like(q))
            dom = jnp.where(hm, do, jnp.zeros_like(do))
            rblk = rs_ref[hh]

            def step(kb, carry, masked):
                gpre, dq = carry
                k0 = pl.multiple_of(kb * TQ, TQ)
                k = k_ref[pl.ds(k0, TQ), :]
                v = v_ref[pl.ds(k0, TQ), :]
                z = _dot_nt(qm, k)
                lneg, e = _log_sigmoid_parts(z)
                lpos = z + lneg
                if masked:
                    lneg = jnp.where(diag, lneg, 0.0)
                run = jnp.sum(jnp.where(lane == kb, rblk, 0.0), axis=1, keepdims=True) + jnp.zeros((TQ, LANE), F32)
                a = []
                for sl in reversed(_sub_blocks()):
                    st = _dot(_split(lneg[:, sl]), u2)
                    a.append(jnp.exp(lpos[:, sl] + st[:, :LANE] + run))
                    run = run + st[:, LANE:]
                a = jnp.concatenate(a[::-1], axis=1)
                if masked:
                    a = jnp.where(diag, a, 0.0)
                g = a * _dot_nt(dom, v)
                pre = []
                for sl in _sub_blocks():
                    pt = _dot(_split(g[:, sl]), l2)
                    pre.append(gpre + pt[:, :LANE])
                    gpre = gpre + pt[:, LANE:]
                sig = jnp.where(z >= 0.0, 1.0, e) / (1.0 + e)
                dz = g - (g + jnp.concatenate(pre, axis=1)) * sig
                if masked:
                    dz = jnp.where(diag, dz, 0.0)
                dzb = dz.astype(MXU)
                dk_acc[pl.ds(k0, TQ), :] += _dot_tn(dzb, qm)
                dv_acc[pl.ds(k0, TQ), :] += _dot_tn(a.astype(MXU), dom)
                return gpre, dq + _dot(dzb, k)

            zero = jnp.zeros((TQ, LANE), F32)
            carry = lax.fori_loop(0, i, lambda kb, cr: step(kb, cr, False), (zero, zero))
            carry = step(i, carry, True)
            outs.append(carry[1])
        dq_ref[...] = (jnp.where(lane < HEAD_DIM, outs[0], outs[1]) * QK_SCALE).astype(dq_ref.dtype)

        @pl.when(i == nq - 1)
        def _():
            dk_ref[...] = dk_acc[...].astype(dk_ref.dtype)
            dv_ref[...] = dv_acc[...].astype(dv_ref.dtype)

    blk = pl.BlockSpec((TQ, LANE), lambda hp, i: (i, hp))
    whole = pl.BlockSpec((s, LANE), lambda hp, i: (0, hp))
    return _call_with_riders(
        body, "sb_bwd", nq, _attn_specs(s, 0) + [blk, pl.BlockSpec((2, TQ, LANE), lambda hp, i: (hp, i, 0))],
        [blk, whole, whole], [jax.ShapeDtypeStruct((s, WIDTH), MXU)] * 3,
        [pltpu.VMEM((s, LANE), F32), pltpu.VMEM((s, LANE), F32)], (qkv, qkv, qkv, do, rs), riders, False)


def _key_bias(fkt_ref, kb, h):
    n_sub = TQ // LANE
    return jnp.concatenate([fkt_ref[kb * n_sub + j, pl.ds(h, 1), :] for j in range(n_sub)], axis=1)


def _fox_fwd(qkv, fc, fkt):
    s = qkv.shape[0]
    nq = s // TQ
    nb = fkt.shape[0]

    def body(q_ref, k_ref, v_ref, fq_ref, fkt_ref, o_ref, lse_ref):
        hp = pl.program_id(0)
        i = pl.program_id(1)
        lane, row, col = _tile_iotas()
        diag = col <= row
        q = q_ref[...]
        fqb = fq_ref[...]
        outs = []
        for hh in range(2):
            h = 2 * hp + hh
            hm = (lane >= HEAD_DIM) if hh else (lane < HEAD_DIM)
            qm = jnp.where(hm, q, jnp.zeros_like(q))
            fq = jnp.sum(jnp.where(lane == h, fqb, 0.0), axis=1, keepdims=True)

            def step(kb, carry, masked):
                m, l, acc = carry
                k0 = pl.multiple_of(kb * TQ, TQ)
                k = k_ref[pl.ds(k0, TQ), :]
                v = v_ref[pl.ds(k0, TQ), :]
                z = _dot_nt(qm, k) + fq - _key_bias(fkt_ref, kb, h)
                if masked:
                    z = jnp.where(diag, z, NEG)
                mn = jnp.maximum(m, jnp.max(z, axis=1, keepdims=True))
                p = jnp.exp(z - mn)
                alpha = jnp.exp(m - mn)
                return mn, alpha * l + jnp.sum(p, axis=1, keepdims=True), alpha * acc + _dot(p.astype(MXU), v)

            init = (jnp.full((TQ, 1), NEG, F32), jnp.zeros((TQ, 1), F32), jnp.zeros((TQ, LANE), F32))
            carry = lax.fori_loop(0, i, lambda kb, cr: step(kb, cr, False), init)
            m, l, acc = step(i, carry, True)
            outs.append(acc / l)
            lse_ref[hh] = jnp.broadcast_to(m + jnp.log(l), (TQ, LANE))
        o_ref[...] = jnp.where(lane < HEAD_DIM, outs[0], outs[1]).astype(o_ref.dtype)

    return pl.pallas_call(
        body, name="fox_fwd", grid=(4, nq),
        in_specs=_attn_specs(s, 12) + [pl.BlockSpec((TQ, LANE), lambda hp, i: (i, 0)),
                                       pl.BlockSpec((nb, N_FGATE, LANE), lambda hp, i: (0, 0, 0))],
        out_specs=[pl.BlockSpec((TQ, LANE), lambda hp, i: (i, hp)), pl.BlockSpec((2, TQ, LANE), lambda hp, i: (hp, i, 0))],
        out_shape=[jax.ShapeDtypeStruct((s, WIDTH), MXU), jax.ShapeDtypeStruct((8, s, LANE), F32)],
    )(qkv, qkv, qkv, fc, fkt)


def _fox_bwd(qkv, fc, fkt, do, o, lse):
    s = qkv.shape[0]
    nq = s // TQ
    nb = fkt.shape[0]

    def body(q_ref, k_ref, v_ref, fq_ref, fkt_ref, do_ref, o_ref, lse_ref, dq_ref, dk_ref, dv_ref, dfk_ref, dfq_ref, dk_acc, dv_acc):
        hp = pl.program_id(0)
        i = pl.program_id(1)

        @pl.when(i == 0)
        def _():
            dk_acc[...] = jnp.zeros_like(dk_acc)
            dv_acc[...] = jnp.zeros_like(dv_acc)

        @pl.when((i == 0) & (hp == 0))
        def _():
            dfk_ref[...] = jnp.zeros_like(dfk_ref)

        lane, row, col = _tile_iotas()
        diag = col <= row
        q = q_ref[...]
        do = do_ref[...]
        dof = do.astype(F32) * o_ref[...].astype(F32)
        fqb = fq_ref[...]
        outs = []
        dfq = jnp.zeros((TQ, LANE), F32)
        for hh in range(2):
            h = 2 * hp + hh
            hm = (lane >= HEAD_DIM) if hh else (lane < HEAD_DIM)
            qm = jnp.where(hm, q, jnp.zeros_like(q))
            dom = jnp.where(hm, do, jnp.zeros_like(do))
            delta = jnp.sum(jnp.where(hm, dof, 0.0), axis=1, keepdims=True)
            fq = jnp.sum(jnp.where(lane == h, fqb, 0.0), axis=1, keepdims=True)
            lse_t = lse_ref[hh][:, :1]

            def step(kb, carry, masked):
                dq, rsum = carry
                k0 = pl.multiple_of(kb * TQ, TQ)
                k = k_ref[pl.ds(k0, TQ), :]
                v = v_ref[pl.ds(k0, TQ), :]
                z = _dot_nt(qm, k) + fq - _key_bias(fkt_ref, kb, h)
                if masked:
                    z = jnp.where(diag, z, NEG)
                p = jnp.exp(z - lse_t)
                ds = p * (_dot_nt(dom, v) - delta)
                dsb = ds.astype(MXU)
                dk_acc[pl.ds(k0, TQ), :] += _dot_tn(dsb, qm)
                dv_acc[pl.ds(k0, TQ), :] += _dot_tn(p.astype(MXU), dom)
                csum = _colsum(ds)
                for j, sl in enumerate(_sub_blocks()):
                    dfk_ref[kb * len(_sub_blocks()) + j, pl.ds(h, 1), :] += -csum[:, sl]
                return dq + _dot(dsb, k), rsum + jnp.sum(ds, axis=1, keepdims=True)

            carry = lax.fori_loop(0, i, lambda kb, cr: step(kb, cr, False), (jnp.zeros((TQ, LANE), F32), jnp.zeros((TQ, 1), F32)))
            dq, rsum = step(i, carry, True)
            outs.append(dq)
            dfq = jnp.where(lane == h, rsum, dfq)
        dq_ref[...] = (jnp.where(lane < HEAD_DIM, outs[0], outs[1]) * QK_SCALE).astype(dq_ref.dtype)
        dfq_ref[0] = dfq

        @pl.when(i == nq - 1)
        def _():
            dk_ref[...] = dk_acc[...].astype(dk_ref.dtype)
            dv_ref[...] = dv_acc[...].astype(dv_ref.dtype)

    blk = pl.BlockSpec((TQ, LANE), lambda hp, i: (i, hp))
    whole = pl.BlockSpec((s, LANE), lambda hp, i: (0, hp))
    pair = pl.BlockSpec((2, TQ, LANE), lambda hp, i: (hp, i, 0))
    fkt_spec = pl.BlockSpec((nb, N_FGATE, LANE), lambda hp, i: (0, 0, 0))
    return pl.pallas_call(
        body, name="fox_bwd", grid=(4, nq),
        in_specs=_attn_specs(s, 12) + [pl.BlockSpec((TQ, LANE), lambda hp, i: (i, 0)), fkt_spec, blk, blk, pair],
        out_specs=[blk, whole, whole, fkt_spec, pl.BlockSpec((1, TQ, LANE), lambda hp, i: (hp, i, 0))],
        out_shape=[jax.ShapeDtypeStruct((s, WIDTH), MXU)] * 3
        + [jax.ShapeDtypeStruct((nb, N_FGATE, LANE), F32), jax.ShapeDtypeStruct((4, s, LANE), F32)],
        scratch_shapes=[pltpu.VMEM((s, LANE), F32), pltpu.VMEM((s, LANE), F32)],
    )(qkv, qkv, qkv, fc, fkt, do, o, lse)


def _fcum_bwd(dfkt, dfq, fl, bf):
    s = fl.shape[0]
    nb = s // LANE

    def body(dfkt_ref, dfq_ref, fl_ref, bf_ref, df_ref, dbf_ref, tail_ref):
        @pl.when(pl.program_id(0) == 0)
        def _():
            tail_ref[...] = jnp.zeros_like(tail_ref)
            dbf_ref[...] = jnp.zeros_like(dbf_ref)

        r = lax.broadcasted_iota(jnp.int32, (LANE, LANE), 0)
        c = lax.broadcasted_iota(jnp.int32, (LANE, LANE), 1)
        tri = (c >= r).astype(F32)
        dfc = jnp.concatenate([dfkt_ref[0], jnp.zeros((LANE - N_FGATE, LANE), F32)], axis=0).T
        dfc = dfc + ((dfq_ref[0] + dfq_ref[1]) + (dfq_ref[2] + dfq_ref[3]))
        dls = jnp.dot(tri, dfc, precision=lax.Precision.HIGHEST, preferred_element_type=F32) + tail_ref[...]
        xb = fl_ref[...] + bf_ref[...]
        e = jnp.exp(-jnp.abs(xb))
        dfl = dls * (jnp.where(xb >= 0.0, e, 1.0) / (1.0 + e))
        df_ref[...] = dfl.astype(df_ref.dtype)
        tail_ref[...] = dls[0:1, :]
        dbf_ref[...] += _colsum(dfl)

    return pl.pallas_call(
        body, name="fcum_bwd", grid=(nb,),
        in_specs=[pl.BlockSpec((1, N_FGATE, LANE), lambda j: (nb - 1 - j, 0, 0)), pl.BlockSpec((4, LANE, LANE), lambda j: (0, nb - 1 - j, 0)),
                  pl.BlockSpec((LANE, LANE), lambda j: (nb - 1 - j, 0)), _fixed(1, LANE)],
        out_specs=[pl.BlockSpec((LANE, LANE), lambda j: (nb - 1 - j, 0)), _fixed(1, LANE)],
        out_shape=[jax.ShapeDtypeStruct((s, LANE), MXU), jax.ShapeDtypeStruct((1, LANE), F32)],
        scratch_shapes=[pltpu.VMEM((1, LANE), F32)],
    )(dfkt, dfq, fl, bf)


def _mix_fwd(x, o_sb, o_fx, gl, w_sb, w_fx, w_o, g1, ln1_g, ln1_b, sh2, sc2):
    s = x.shape[0]
    tm = 256

    def body(x_ref, osb_ref, ofx_ref, gl_ref, wsb_ref, wfx_ref, wo_ref, g1_ref, lg_ref, lb_ref, sh_ref, sc_ref, r1_ref, u2_ref):
        mixin = (_sigmoid(gl_ref[:, :D]) * _dot(osb_ref[...], wsb_ref[...])
                 + _sigmoid(gl_ref[:, D:]) * _dot(ofx_ref[...], wfx_ref[...]))
        r1 = ALPHA * x_ref[...] + g1_ref[...] * _dot(mixin.astype(MXU), wo_ref[...])
        r1_ref[...] = r1
        x1 = _ln(r1)[0] * lg_ref[...] + lb_ref[...]
        u2_ref[...] = (_ln(x1)[0] * (1.0 + sc_ref[...]) + sh_ref[...]).astype(MXU)

    vec = _fixed(1, D)
    return pl.pallas_call(
        body, name="mix_fwd", grid=(s // tm,),
        in_specs=[_rows(tm, D), _rows(tm, WIDTH), _rows(tm, WIDTH), _rows(tm, 2 * D), _res(w_sb), _res(w_fx), _res(w_o),
                  vec, vec, vec, vec, vec],
        out_specs=[_rows(tm, D), _rows(tm, D)],
        out_shape=[jax.ShapeDtypeStruct((s, D), F32), jax.ShapeDtypeStruct((s, D), MXU)],
        compiler_params=_params(VMEM_BIG),
    )(x, o_sb, o_fx, gl, w_sb, w_fx, w_o, g1, ln1_g, ln1_b, sh2, sc2)


def _ffn_fwd(r1, u2, tgt, w_g, w_u, w_d, g2, ln1_g, ln1_b, ln2_g, ln2_b):
    s = r1.shape[0]
    tm = 256

    def body(r1_ref, u2_ref, t_ref, wg_ref, wu_ref, wd_ref, g2_ref, l1g_ref, l1b_ref, l2g_ref, l2b_ref,
             hg_ref, hu_ref, dxa_ref, dh_ref, acc_ref):
        @pl.when(pl.program_id(0) == 0)
        def _():
            acc_ref[...] = jnp.zeros_like(acc_ref)

        u2 = u2_ref[...]
        hg = _dot_nt(u2, wg_ref[...])
        hu = _dot_nt(u2, wu_ref[...])
        hg_ref[...] = hg
        hu_ref[...] = hu
        h = _dot((hg * _sigmoid(hg) * hu).astype(MXU), wd_ref[...])
        x1 = _ln(r1_ref[...])[0] * l1g_ref[...] + l1b_ref[...]
        xh2, rstd2 = _ln(ALPHA * x1 + g2_ref[...] * h)
        err = xh2 * l2g_ref[...] + l2b_ref[...] - t_ref[...]
        dy = err * (1.0 / D)
        dr2 = _ln_bwd(dy * l2g_ref[...], xh2, rstd2)
        dxa_ref[...] = ALPHA * dr2
        dh_ref[...] = (g2_ref[...] * dr2).astype(MXU)
        acc_ref[0:1, :] += _colsum(dr2 * h)
        acc_ref[1:2, :] += _colsum(dy * xh2)
        acc_ref[2:3, :] += _colsum(dy)
        acc_ref[3:4, :] += _colsum(err * err) * (0.5 / D)

    vec = _fixed(1, D)
    return pl.pallas_call(
        body, name="ffn_fwd", grid=(s // tm,),
        in_specs=[_rows(tm, D), _rows(tm, D), _rows(tm, D), _res(w_g), _res(w_u), _res(w_d), vec, vec, vec, vec, vec],
        out_specs=[_rows(tm, D_FF), _rows(tm, D_FF), _rows(tm, D), _rows(tm, D), _fixed(8, D)],
        out_shape=[jax.ShapeDtypeStruct((s, D_FF), F32), jax.ShapeDtypeStruct((s, D_FF), F32),
                   jax.ShapeDtypeStruct((s, D), F32), jax.ShapeDtypeStruct((s, D), MXU), jax.ShapeDtypeStruct((8, D), F32)],
        compiler_params=_params(VMEM_BIG),
    )(r1, u2, tgt, w_g, w_u, w_d, g2, ln1_g, ln1_b, ln2_g, ln2_b)


def _ffn_bwd(dh, hg, hu, w_g, w_u, w_d):
    s = dh.shape[0]
    tm = 256
    half = D_FF // 2

    def body(dh_ref, hg_ref, hu_ref, wg_ref, wu_ref, wd_ref, act_ref, dhg_ref, dhu_ref, du2_ref):
        dh = dh_ref[...]
        du2 = jnp.zeros((tm, D), F32)
        for c0 in (0, half):
            cols = slice(c0, c0 + half)
            dact = _dot_nt(dh, wd_ref[cols, :])
            hg = hg_ref[:, cols]
            hu = hu_ref[:, cols]
            sg = _sigmoid(hg)
            sl = hg * sg
            act_ref[:, cols] = (sl * hu).astype(MXU)
            dhg = (dact * hu * (sg * (1.0 + hg * (1.0 - sg)))).astype(MXU)
            dhu = (dact * sl).astype(MXU)
            dhg_ref[:, cols] = dhg
            dhu_ref[:, cols] = dhu
            du2 = du2 + _dot(dhg, wg_ref[cols, :]) + _dot(dhu, wu_ref[cols, :])
        du2_ref[...] = du2

    return pl.pallas_call(
        body, name="ffn_bwd", grid=(s // tm,),
        in_specs=[_rows(tm, D), _rows(tm, D_FF), _rows(tm, D_FF), _res(w_g), _res(w_u), _res(w_d)],
        out_specs=[_rows(tm, D_FF), _rows(tm, D_FF), _rows(tm, D_FF), _rows(tm, D)],
        out_shape=[jax.ShapeDtypeStruct((s, D_FF), MXU)] * 3 + [jax.ShapeDtypeStruct((s, D), F32)],
        compiler_params=_params(VMEM_BIG),
    )(dh, hg, hu, w_g, w_u, w_d)


def _mix_bwd(du2, dxa, r1, o_sb, o_fx, gl, w_sb, w_fx, w_o, g1, ln1_g, ln1_b, sc2):
    s = r1.shape[0]
    tm = 256

    def body(du2_ref, dxa_ref, r1_ref, osb_ref, ofx_ref, gl_ref, wsb_ref, wfx_ref, wo_ref, g1_ref, lg_ref, lb_ref, sc_ref,
             dx_ref, mixin_ref, dmix_ref, dysb_ref, dyfx_ref, dosb_ref, dofx_ref, dgl_ref, dbg_ref, acc_ref):
        @pl.when(pl.program_id(0) == 0)
        def _():
            acc_ref[...] = jnp.zeros_like(acc_ref)
            dbg_ref[...] = jnp.zeros_like(dbg_ref)

        du2 = du2_ref[...]
        xh1, rstd1 = _ln(r1_ref[...])
        x1 = xh1 * lg_ref[...] + lb_ref[...]
        n1, rstdn = _ln(x1)
        dx1 = dxa_ref[...] + _ln_bwd(du2 * (1.0 + sc_ref[...]), n1, rstdn)
        dr1 = _ln_bwd(dx1 * lg_ref[...], xh1, rstd1)
        dx_ref[...] = ALPHA * dr1
        ysb = _dot(osb_ref[...], wsb_ref[...])
        yfx = _dot(ofx_ref[...], wfx_ref[...])
        gs = _sigmoid(gl_ref[:, :D])
        gf = _sigmoid(gl_ref[:, D:])
        mixin = (gs * ysb + gf * yfx).astype(MXU)
        mixin_ref[...] = mixin
        mix = _dot(mixin, wo_ref[...])
        dmix = (g1_ref[...] * dr1).astype(MXU)
        dmix_ref[...] = dmix
        dmixin = _dot_nt(dmix, wo_ref[...])
        dysb = (dmixin * gs).astype(MXU)
        dyfx = (dmixin * gf).astype(MXU)
        dysb_ref[...] = dysb
        dyfx_ref[...] = dyfx
        dosb_ref[...] = _dot_nt(dysb, wsb_ref[...]).astype(MXU)
        dofx_ref[...] = _dot_nt(dyfx, wfx_ref[...]).astype(MXU)
        dgs = dmixin * ysb * gs * (1.0 - gs)
        dgf = dmixin * yfx * gf * (1.0 - gf)
        dgl_ref[:, :D] = dgs.astype(MXU)
        dgl_ref[:, D:] = dgf.astype(MXU)
        dbg_ref[:, :D] += _colsum(dgs)
        dbg_ref[:, D:] += _colsum(dgf)
        acc_ref[0:1, :] += _colsum(du2)
        acc_ref[1:2, :] += _colsum(du2 * n1)
        acc_ref[2:3, :] += _colsum(dx1 * xh1)
        acc_ref[3:4, :] += _colsum(dx1)
        acc_ref[4:5, :] += _colsum(dr1 * mix)

    vec = _fixed(1, D)
    return pl.pallas_call(
        body, name="mix_bwd", grid=(s // tm,),
        in_specs=[_rows(tm, D), _rows(tm, D), _rows(tm, D), _rows(tm, WIDTH), _rows(tm, WIDTH), _rows(tm, 2 * D),
                  _res(w_sb), _res(w_fx), _res(w_o), vec, vec, vec, vec],
        out_specs=[_rows(tm, D), _rows(tm, D), _rows(tm, D), _rows(tm, D), _rows(tm, D), _rows(tm, WIDTH), _rows(tm, WIDTH),
                   _rows(tm, 2 * D), _fixed(1, 2 * D), _fixed(8, D)],
        out_shape=[jax.ShapeDtypeStruct((s, D), F32)] + [jax.ShapeDtypeStruct((s, D), MXU)] * 4
        + [jax.ShapeDtypeStruct((s, WIDTH), MXU)] * 2
        + [jax.ShapeDtypeStruct((s, 2 * D), MXU), jax.ShapeDtypeStruct((1, 2 * D), F32), jax.ShapeDtypeStruct((8, D), F32)],
        compiler_params=_params(VMEM_BIG),
    )(du2, dxa, r1, o_sb, o_fx, gl, w_sb, w_fx, w_o, g1, ln1_g, ln1_b, sc2)


def _in_bwd(pieces, x, dxa, w_all, sc1):
    s = x.shape[0]
    tm = 256
    n_p = len(pieces)

    def body(*refs):
        p_refs = refs[:n_p]
        x_ref, dxa_ref, w_ref, sc_ref, gx_ref, acc_ref = refs[n_p:]

        @pl.when(pl.program_id(0) == 0)
        def _():
            acc_ref[...] = jnp.zeros_like(acc_ref)

        du1 = jnp.zeros((tm, D), F32)
        for p_ref, (arr, c0) in zip(p_refs, pieces):
            du1 = du1 + _dot(p_ref[...], w_ref[c0:c0 + arr.shape[1], :])
        n0, rstd0 = _ln(x_ref[...])
        gx_ref[...] = dxa_ref[...] + _ln_bwd(du1 * (1.0 + sc_ref[...]), n0, rstd0)
        acc_ref[0:1, :] += _colsum(du1)
        acc_ref[1:2, :] += _colsum(du1 * n0)

    return pl.pallas_call(
        body, name="in_bwd", grid=(s // tm,),
        in_specs=[_rows(tm, a.shape[1]) for a, _ in pieces] + [_rows(tm, D), _rows(tm, D), _res(w_all), _fixed(1, D)],
        out_specs=[_rows(tm, D), _fixed(8, D)],
        out_shape=[jax.ShapeDtypeStruct((s, D), F32), jax.ShapeDtypeStruct((8, D), F32)],
        compiler_params=_params(VMEM_BIG),
    )(*[a for a, _ in pieces], x, dxa, w_all, sc1)


def _matmul_tn(a, b, name):
    s, m = a.shape
    n = b.shape[1]
    tm = 512 if m % 512 == 0 else (m if m < 512 else m // 2)
    tn = n // 2 if n > 2048 else n
    ts = 512
    assert m % tm == 0 and tm % LANE == 0 and n % tn == 0 and tn % LANE == 0 and s % ts == 0

    def body(a_ref, b_ref, o_ref):
        @pl.when(pl.program_id(2) == 0)
        def _():
            o_ref[...] = jnp.zeros_like(o_ref)

        o_ref[...] += _dot_tn(a_ref[...], b_ref[...])

    return pl.pallas_call(
        body, name=name, grid=(m // tm, n // tn, s // ts),
        in_specs=[pl.BlockSpec((ts, tm), lambda i, j, k: (k, i)), pl.BlockSpec((ts, tn), lambda i, j, k: (k, j))],
        out_specs=pl.BlockSpec((tm, tn), lambda i, j, k: (i, j)),
        out_shape=jax.ShapeDtypeStruct((m, n), F32),
        compiler_params=_params(VMEM_BIG),
    )(a, b)


def _local_step(x, tgt, ada, w_all, b_gate, bf_pad, late_weights, early_grads, ln1_g, ln1_b, ln2_g, ln2_b):
    sh1, sc1, g1, sh2, sc2, g2 = ada
    u1, qkv, fl, gl = _in_proj(x, sh1, sc1, w_all, b_gate)
    fc, fkt = _fcum_fwd(fl, bf_pad)
    late_riders, late_full = late_weights
    (o_sb, rs), late_gathered = _sb_fwd(qkv, late_riders)
    w_sb, w_fx, w_o, w_g, w_u, w_d = late_full(late_gathered)
    o_fx, lse = _fox_fwd(qkv, fc, fkt)
    r1, u2 = _mix_fwd(x, o_sb, o_fx, gl, w_sb, w_fx, w_o, g1, ln1_g, ln1_b, sh2, sc2)
    hg, hu, dxa2, dh, acc_f = _ffn_fwd(r1, u2, tgt, w_g, w_u, w_d, g2, ln1_g, ln1_b, ln2_g, ln2_b)
    act, dhg, dhu, du2 = _ffn_bwd(dh, hg, hu, w_g, w_u, w_d)
    dxa1, mixin, dmix, dysb, dyfx, dosb, dofx, dgl, dbg, acc_m = _mix_bwd(
        du2, dxa2, r1, o_sb, o_fx, gl, w_sb, w_fx, w_o, g1, ln1_g, ln1_b, sc2)
    early = dict(w_sb_out=_matmul_tn(o_sb, dysb, "dw_sb_out"), w_fox_out=_matmul_tn(o_fx, dyfx, "dw_fox_out"),
                 w_o=_matmul_tn(mixin, dmix, "dw_o"), w_ffn_gate=_matmul_tn(dhg, u2, "dw_ffn_gate"),
                 w_ffn_up=_matmul_tn(dhu, u2, "dw_ffn_up"), w_ffn_down=_matmul_tn(act, dh, "dw_ffn_down"))
    (dq_sb, dk_sb, dv_sb), early_received = _sb_bwd(qkv, dosb, rs, early_grads(early))
    dq_fx, dk_fx, dv_fx, dfkt, dfq = _fox_bwd(qkv, fc, fkt, dofx, o_fx, lse)
    df, dbf = _fcum_bwd(dfkt, dfq, fl, bf_pad)
    pieces = [(dq_sb, 0), (dk_sb, WIDTH), (dv_sb, 2 * WIDTH), (dq_fx, 3 * WIDTH), (dk_fx, 4 * WIDTH), (dv_fx, 5 * WIDTH),
              (df, OFF_FGATE), (dgl, OFF_FGATE + LANE)]
    grad_x, acc_i = _in_bwd(pieces, x, dxa1, w_all, sc1)
    dw_in = [_matmul_tn(p, u1, f"dw_in_{j}") for j, (p, _) in enumerate(pieces)]
    return dict(
        loss_lanes=acc_f[3:4], grad_x=grad_x, dw_in=dw_in, early=early, early_received=early_received,
        d_ada=[acc_i[0:1], acc_i[1:2], acc_m[4:5], acc_m[0:1], acc_m[1:2], acc_f[0:1]],
        dln1_g=acc_m[2:3], dln1_b=acc_m[3:4], dln2_g=acc_f[1:2], dln2_b=acc_f[2:3], db_gate=dbg, db_forget=dbf)


_MESH_ID = pl.DeviceIdType.MESH
_ANY = pl.BlockSpec(memory_space=pl.ANY)
_VMEM = pl.BlockSpec(memory_space=pltpu.VMEM)


def _mesh_pos():
    return lax.axis_index("x"), lax.axis_index("y"), lax.axis_index("c")


def _other_chips(x, y):
    return [(1 - x, y), (x, 1 - y), (1 - x, 1 - y)]


def _allgather_rows(v, name):
    n = v.shape[1]

    def body(v_ref, out_ref, send_sems, recv_sems, local_sem):
        x, y, c = _mesh_pos()
        me = 4 * x + 2 * y + c
        mine = pltpu.make_async_copy(v_ref, out_ref.at[me], local_sem)
        mine.start()
        copies = []
        for d in range(1, 8):
            fx, fy, fc = (d >> 2) & 1, (d >> 1) & 1, d & 1
            to = (1 - x if fx else x, 1 - y if fy else y, 1 - c if fc else c)
            cp = pltpu.make_async_remote_copy(src_ref=v_ref, dst_ref=out_ref.at[me], send_sem=send_sems.at[d - 1],
                                              recv_sem=recv_sems.at[d - 1], device_id=to, device_id_type=_MESH_ID)
            cp.start()
            copies.append(cp)
        for cp in copies:
            cp.wait_recv()
        for cp in copies:
            cp.wait_send()
        mine.wait()

    return pl.pallas_call(
        body, name=name, in_specs=[_VMEM], out_specs=_VMEM,
        out_shape=jax.ShapeDtypeStruct((8, 1, n), v.dtype),
        scratch_shapes=[pltpu.SemaphoreType.DMA((7,)), pltpu.SemaphoreType.DMA((7,)), pltpu.SemaphoreType.DMA(())],
    )(v)


def _chip_exchange(arrays, name, gather):
    nt = len(arrays)

    def body(*refs):
        ins, outs = refs[:nt], refs[nt:2 * nt]
        _exchange_start(ins, outs, refs[2 * nt:], gather)
        _exchange_wait(ins, outs, refs[2 * nt:], gather)

    return pl.pallas_call(
        body, name=name, in_specs=[_ANY] * nt, out_specs=[_ANY] * nt, out_shape=_exchange_out_shape(arrays),
        scratch_shapes=_exchange_sems(nt),
    )(*arrays)


def _exchange_out_shape(arrays):
    return [jax.ShapeDtypeStruct((4,) + a.shape[-2:], a.dtype) for a in arrays]


def _exchange_sems(nt):
    return [pltpu.SemaphoreType.DMA((3 * nt,)), pltpu.SemaphoreType.DMA((3 * nt,)), pltpu.SemaphoreType.DMA((nt,))]


def _exchange_copies(ins, outs, sems, gather):
    send_sems, recv_sems, local_sems = sems
    x, y, c = _mesh_pos()
    me = 2 * x + y
    local, remote = [], []
    for t in range(len(ins)):
        local.append(pltpu.make_async_copy(ins[t] if gather else ins[t].at[me], outs[t].at[me], local_sems.at[t]))
        for j, (px, py) in enumerate(_other_chips(x, y)):
            remote.append(pltpu.make_async_remote_copy(
                src_ref=ins[t] if gather else ins[t].at[2 * px + py], dst_ref=outs[t].at[me], send_sem=send_sems.at[3 * t + j],
                recv_sem=recv_sems.at[3 * t + j], device_id=(px, py, c), device_id_type=_MESH_ID))
    return local, remote


def _exchange_start(ins, outs, sems, gather):
    local, remote = _exchange_copies(ins, outs, sems, gather)
    for cp in local + remote:
        cp.start()


def _exchange_wait(ins, outs, sems, gather):
    local, remote = _exchange_copies(ins, outs, sems, gather)
    for cp in remote:
        cp.wait_recv()
    for cp in remote:
        cp.wait_send()
    for cp in local:
        cp.wait()


def _sibling_exchange(arrays, name):
    nt = len(arrays)

    def body(*refs):
        ins, outs = refs[:nt], refs[nt:2 * nt]
        send_sems, recv_sems = refs[2 * nt:]
        x, y, c = _mesh_pos()
        copies = []
        for t in range(nt):
            cp = pltpu.make_async_remote_copy(src_ref=ins[t], dst_ref=outs[t], send_sem=send_sems.at[t], recv_sem=recv_sems.at[t],
                                              device_id=(x, y, 1 - c), device_id_type=_MESH_ID)
            cp.start()
            copies.append(cp)
        for cp in copies:
            cp.wait_recv()
        for cp in copies:
            cp.wait_send()

    return pl.pallas_call(
        body, name=name, in_specs=[_ANY] * nt, out_specs=[_ANY] * nt,
        out_shape=[jax.ShapeDtypeStruct(a.shape, a.dtype) for a in arrays],
        scratch_shapes=[pltpu.SemaphoreType.DMA((nt,)), pltpu.SemaphoreType.DMA((nt,))],
    )(*arrays)


def _tiles(r, n):
    for tr in (256, 352, 128):
        if r % tr == 0:
            return tr, n, r // tr, lambda i: (i, 0)
    assert n % 256 == 0
    return r, 256, n // 256, lambda i: (0, i)


def _reduce_chips(own, recv, name):
    r, n = own.shape
    tr, tn, steps, at = _tiles(r, n)

    def body(own_ref, recv_ref, out_ref):
        x, y, _ = _mesh_pos()
        me = 2 * x + y
        total = jnp.zeros((tr, tn), F32)
        for k in range(4):
            total = total + jnp.where(me == k, own_ref[...], recv_ref[k].astype(F32))
        out_ref[...] = total

    blk = pl.BlockSpec((tr, tn), at)
    return pl.pallas_call(
        body, name=name, grid=(steps,),
        in_specs=[blk, pl.BlockSpec((4, tr, tn), lambda i: (0,) + at(i))], out_specs=blk,
        out_shape=jax.ShapeDtypeStruct((r, n), F32),
    )(own, recv)


def _adamw_math(w, g, m, v):
    m = ADAM_B1 * m + (1.0 - ADAM_B1) * g
    v = ADAM_B2 * v + (1.0 - ADAM_B2) * (g * g)
    m_hat = m / (1.0 - ADAM_B1 ** ADAM_STEP)
    v_hat = v / (1.0 - ADAM_B2 ** ADAM_STEP)
    return -ADAM_LR * (m_hat / (jnp.sqrt(v_hat) + ADAM_EPS) + ADAM_WD * w), m, v


def _adamw(w, m, v, g_parts, name):
    r, n = w.shape
    tr, tn, steps, at = _tiles(r, n)
    blk = pl.BlockSpec((tr, tn), at)
    ng = len(g_parts)

    def body(*refs):
        w_ref, m_ref, v_ref = refs[:3]
        g_refs = refs[3:3 + ng]
        g_out, d_out, m_out, v_out = refs[3 + ng:]
        g = g_refs[0][...]
        for gr in g_refs[1:]:
            g = g + gr[...]
        g_out[...] = g
        d_out[...], m_out[...], v_out[...] = _adamw_math(w_ref[...], g, m_ref[...], v_ref[...])

    return pl.pallas_call(
        body, name=name, grid=(steps,),
        in_specs=[blk] * (3 + ng), out_specs=[blk] * 4,
        out_shape=[jax.ShapeDtypeStruct((r, n), F32)] * 4,
    )(w, m, v, *g_parts)


def _ada_fwd(c_all, w_shard, b_shard):
    n = w_shard.shape[1]
    tn = 512

    def body(c_ref, w_ref, b_ref, o_ref):
        cv = c_ref[...]
        ca = (cv * _sigmoid(cv)).astype(MXU)
        o_ref[...] = _dot(ca, w_ref[...].astype(MXU)) + b_ref[...]

    return pl.pallas_call(
        body, name="ada_fwd", grid=(n // tn,),
        in_specs=[_fixed(8, D), pl.BlockSpec((D, tn), lambda j: (0, j)), pl.BlockSpec((1, tn), lambda j: (0, j))],
        out_specs=pl.BlockSpec((8, tn), lambda j: (0, j)),
        out_shape=jax.ShapeDtypeStruct((8, n), F32),
    )(c_all, w_shard, b_shard)


def _ada_bwd(c_all, dada_shard):
    n = dada_shard.shape[1]
    tn = 512

    def body(c_ref, d_ref, o_ref):
        cv = c_ref[...]
        ca = (cv * _sigmoid(cv)).astype(MXU)
        o_ref[...] = _dot_tn(ca, d_ref[...].astype(MXU))

    return pl.pallas_call(
        body, name="ada_bwd", grid=(n // tn,),
        in_specs=[_fixed(8, D), pl.BlockSpec((8, tn), lambda j: (0, j))],
        out_specs=pl.BlockSpec((D, tn), lambda j: (0, j)),
        out_shape=jax.ShapeDtypeStruct((D, n), F32),
    )(c_all, dada_shard)


_SMALL = [("d_ada", N_COND * D), ("ln1_g", D), ("ln1_b", D), ("ln2_g", D), ("ln2_b", D), ("b_gate", 2 * D), ("b_forget", LANE),
          ("loss", D)]
_SMALL_OFF = {}
_o = 0
for _n, _w in _SMALL:
    _SMALL_OFF[_n] = (_o, _w)
    _o += _w
_SMALL_LEN = _o
_SMALL_PARAMS = [("b_ada", "d_ada", N_COND * D), ("b_gate", "b_gate", 2 * D), ("b_forget", "b_forget", N_FGATE),
                 ("ln1_g", "ln1_g", D), ("ln1_b", "ln1_b", D), ("ln2_g", "ln2_g", D), ("ln2_b", "ln2_b", D)]


def _small_update(rows, params):
    npar = len(_SMALL_PARAMS)

    def body(*refs):
        rows_ref = refs[0]
        p_refs = refs[1:1 + 3 * npar]
        loss_ref = refs[1 + 3 * npar]
        o_refs = refs[2 + 3 * npar:]
        total = rows_ref[0]
        for d in range(1, 8):
            total = total + rows_ref[d]
        lo, lw = _SMALL_OFF["loss"]
        loss_ref[...] = jnp.sum(total[:, lo:lo + lw], axis=1, keepdims=True)
        for j, (_, key, n) in enumerate(_SMALL_PARAMS):
            off = _SMALL_OFF[key][0]
            g = total[:, off:off + n]
            w_ref, m_ref, v_ref = p_refs[3 * j:3 * j + 3]
            o_refs[4 * j][...] = g
            o_refs[4 * j + 1][...], o_refs[4 * j + 2][...], o_refs[4 * j + 3][...] = _adamw_math(w_ref[...], g, m_ref[...], v_ref[...])

    flat = [a for p in params for a in p]
    out_shape = [jax.ShapeDtypeStruct((1, 1), F32)] + [jax.ShapeDtypeStruct((1, n), F32) for _, _, n in _SMALL_PARAMS for _ in range(4)]
    return pl.pallas_call(body, name="small_update", out_shape=out_shape)(rows, *flat)


_BIG = [("w_in", "cols_t"), ("w_sb_out", "cols"), ("w_fox_out", "cols"), ("w_o", "rows"),
        ("w_ffn_gate", "cols_t"), ("w_ffn_up", "cols_t"), ("w_ffn_down", "rows")]


def _shard2d(a, how):
    return a[0].T if how == "cols_t" else a[0]


def _unshard(g, how):
    if how == "cols":
        return g.transpose(1, 0, 2).reshape(g.shape[1], 4 * g.shape[2])
    return g.reshape(4 * g.shape[1], g.shape[2])


def _reshard(w, how):
    if how == "cols":
        return w.reshape(w.shape[0], 4, w.shape[1] // 4).transpose(1, 0, 2)
    return w.reshape(4, w.shape[0] // 4, w.shape[1])


def kernel(x, c, w_ada, b_ada, w_in, b_gate, b_forget, w_sb_out, w_fox_out, w_o, ln1_g, ln1_b, w_ffn_gate, w_ffn_up, w_ffn_down, ln2_g, ln2_b, loss_target, m_w_ada, m_b_ada, m_w_in, m_b_gate, m_b_forget, m_w_sb_out, m_w_fox_out, m_w_o, m_ln1_g, m_ln1_b, m_w_ffn_gate, m_w_ffn_up, m_w_ffn_down, m_ln2_g, m_ln2_b, v_w_ada, v_b_ada, v_w_in, v_b_gate, v_b_forget, v_w_sb_out, v_w_fox_out, v_w_o, v_ln1_g, v_ln1_b, v_w_ffn_gate, v_w_ffn_up, v_w_ffn_down, v_ln2_g, v_ln2_b):
    given = dict(locals())
    mx, my, mc = _mesh_pos()
    chip = 2 * mx + my
    seq = 4 * mx + 2 * my + mc

    c_all = _allgather_rows(c, "gather_c").reshape(8, D)
    n_ada = w_ada.shape[2]
    b_ada_shard = lax.dynamic_slice(b_ada, (0, chip * n_ada), (1, n_ada))
    ada_part = _ada_fwd(c_all, w_ada[0], b_ada_shard)
    ada_all = _allgather_rows(ada_part.reshape(1, 8 * n_ada), "gather_ada").reshape(4, 2, 8, n_ada)
    ada_row = lax.dynamic_slice(ada_all, (0, mc, seq, 0), (4, 1, 1, n_ada)).reshape(1, N_COND * D)
    ada = [ada_row[:, j * D:(j + 1) * D] for j in range(N_COND)]

    (w_in_g,) = _chip_exchange([_shard2d(w_in, "cols_t").astype(MXU)], "gather_w_in", gather=True)
    wi = _unshard(w_in_g, "cols_t")
    w_all = jnp.concatenate([wi[:OFF_FGATE + N_FGATE], jnp.zeros((LANE - N_FGATE, D), MXU), wi[OFF_FGATE + N_FGATE:]], axis=0)
    bf_pad = jnp.concatenate([b_forget, jnp.zeros((1, LANE - N_FGATE), F32)], axis=1)
    late = _BIG[1:]
    late_riders = [_shard2d(given[n], how).astype(MXU) for n, how in late]
    pieces = {}

    def late_full(gathered):
        return [_unshard(g, how) for (_, how), g in zip(late, gathered)]

    def early_grads(dw):
        for n, how in late:
            pieces[n] = _reshard(dw[n], how)
        return [pieces[n].astype(MXU) for n, _ in late]

    out = _local_step(x[0], loss_target[0], ada, w_all, b_gate, bf_pad, (late_riders, late_full), early_grads,
                      ln1_g, ln1_b, ln2_g, ln2_b)

    row = jnp.concatenate(out["d_ada"] + [out["dln1_g"], out["dln1_b"], out["dln2_g"], out["dln2_b"], out["db_gate"],
                                          out["db_forget"], out["loss_lanes"]], axis=1)
    rows = _allgather_rows(row, "gather_small")
    small = _small_update(rows, [(given[p], given["m_" + p], given["v_" + p]) for p, _, _ in _SMALL_PARAMS])
    loss = small[0].reshape(())
    res = {}
    for j, (p, _, _) in enumerate(_SMALL_PARAMS):
        res[p] = small[1 + 4 * j:5 + 4 * j]

    dada_all = rows.reshape(8, _SMALL_LEN)[:, :N_COND * D]
    dada_shard = lax.dynamic_slice(dada_all, (0, chip * n_ada), (8, n_ada))
    g_ada = _ada_bwd(c_all, dada_shard)
    res["w_ada"] = [a[None] for a in _adamw(w_ada[0], m_w_ada[0], v_w_ada[0], [g_ada], "adamw_w_ada")]

    dwi = out["dw_in"]
    pieces["w_in"] = _reshard(jnp.concatenate(dwi[:6] + [dwi[6][:N_FGATE], dwi[7]], axis=0), "cols_t")
    received = dict(zip([n for n, _ in late], out["early_received"]))
    (received["w_in"],) = _chip_exchange([pieces["w_in"].astype(MXU)], "scatter_w_in", gather=False)
    partial = [_reduce_chips(lax.dynamic_index_in_dim(pieces[n], chip, 0, keepdims=False), received[n], "reduce_" + n)
               for n, _ in _BIG]
    theirs = _sibling_exchange(partial, "swap_cores")
    for (n, how), mine, other in zip(_BIG, partial, theirs):
        upd = _adamw(_shard2d(given[n], how), _shard2d(given["m_" + n], how), _shard2d(given["v_" + n], how), [mine, other], "adamw_" + n)
        res[n] = [(a.T if how == "cols_t" else a)[None] for a in upd]

    order = ["w_ada", "b_ada", "w_in", "b_gate", "b_forget", "w_sb_out", "w_fox_out", "w_o", "ln1_g", "ln1_b",
             "w_ffn_gate", "w_ffn_up", "w_ffn_down", "ln2_g", "ln2_b"]
    return (loss, out["grad_x"][None], *[res[n][0] for n in order], *[res[n][1] for n in order],
            *[res[n][2] for n in order], *[res[n][3] for n in order])
```

```python
import functools

import jax
import jax.numpy as jnp
from jax import lax
from jax.experimental import pallas as pl
from jax.experimental.pallas import tpu as pltpu

F32 = jnp.float32
MXU = jnp.bfloat16

D = 1024
HEAD_DIM = 64
WIDTH = 512
D_FF = 2816
N_COND = 6
LN_EPS = 1e-5
ALPHA = 2.0 ** 0.25
QK_SCALE = HEAD_DIM ** -0.5
OFF_FGATE = 6 * WIDTH
N_FGATE = 8
IN_COLS = OFF_FGATE + N_FGATE + 2 * D
LANE = 128
W_ALL_COLS = OFF_FGATE + LANE + 2 * D
TQ = 512
ADAM_LR, ADAM_B1, ADAM_B2, ADAM_EPS, ADAM_WD, ADAM_STEP = 0.001, 0.9, 0.999, 1e-08, 0.01, 10
NEG = -1e30
MESH_AXES = ("x", "y", "c")
VMEM_BIG = 56 * 1024 * 1024


def _dot(a, b):
    return jnp.dot(a, b, preferred_element_type=F32)


def _dot_nt(a, b):
    return lax.dot_general(a, b, (((1,), (1,)), ((), ())), preferred_element_type=F32)


def _dot_tn(a, b):
    return lax.dot_general(a, b, (((0,), (0,)), ((), ())), preferred_element_type=F32)


def _ln(x):
    mu = jnp.mean(x, axis=-1, keepdims=True)
    xc = x - mu
    var = jnp.mean(xc * xc, axis=-1, keepdims=True)
    rstd = lax.rsqrt(var + LN_EPS)
    return xc * rstd, rstd


def _ln_bwd(dxhat, xhat, rstd):
    return rstd * (dxhat - jnp.mean(dxhat, axis=-1, keepdims=True) - xhat * jnp.mean(dxhat * xhat, axis=-1, keepdims=True))


def _sigmoid(x):
    return 1.0 / (1.0 + jnp.exp(-x))


def _colsum(x):
    return jnp.sum(x, axis=0, keepdims=True)


def _split(x):
    hi = x.astype(MXU)
    lo = (x - hi.astype(F32)).astype(MXU)
    return jnp.concatenate([hi, lo], axis=1)


def _rows(tm, n):
    return pl.BlockSpec((tm, n), lambda i: (i, 0))


def _fixed(r, n):
    return pl.BlockSpec((r, n), lambda i: (0, 0))


def _res(a):
    return pl.BlockSpec(a.shape, lambda i: (0, 0), pipeline_mode=pl.Buffered(1))


def _params(limit=None, sem=None):
    return pltpu.CompilerParams(vmem_limit_bytes=limit, dimension_semantics=sem)


def _in_proj(x, sh1, sc1, w_all, b_gate):
    s = x.shape[0]
    tm = 256

    def body(x_ref, sh_ref, sc_ref, w_ref, bg_ref, u_ref, qkv_ref, fl_ref, gl_ref):
        xhat, _ = _ln(x_ref[...])
        u = (xhat * (1.0 + sc_ref[...]) + sh_ref[...]).astype(MXU)
        u_ref[...] = u
        for c0 in range(0, OFF_FGATE, WIDTH):
            p = _dot_nt(u, w_ref[c0:c0 + WIDTH, :])
            if c0 in (0, 3 * WIDTH):
                p = p * QK_SCALE
            qkv_ref[:, c0:c0 + WIDTH] = p.astype(MXU)
        fl_ref[...] = _dot_nt(u, w_ref[OFF_FGATE:OFF_FGATE + LANE, :])
        for c0 in range(0, 2 * D, D):
            gl_ref[:, c0:c0 + D] = _dot_nt(u, w_ref[OFF_FGATE + LANE + c0:OFF_FGATE + LANE + c0 + D, :]) + bg_ref[:, c0:c0 + D]

    return pl.pallas_call(
        body, name="in_proj", grid=(s // tm,),
        in_specs=[_rows(tm, D), _fixed(1, D), _fixed(1, D), _res(w_all), _fixed(1, 2 * D)],
        out_specs=[_rows(tm, D), _rows(tm, OFF_FGATE), _rows(tm, LANE), _rows(tm, 2 * D)],
        out_shape=[jax.ShapeDtypeStruct((s, D), MXU), jax.ShapeDtypeStruct((s, OFF_FGATE), MXU),
                   jax.ShapeDtypeStruct((s, LANE), F32), jax.ShapeDtypeStruct((s, 2 * D), F32)],
        compiler_params=_params(VMEM_BIG),
    )(x, sh1, sc1, w_all, b_gate)


def _log_sigmoid_parts(z):
    e = jnp.exp(-jnp.abs(z))
    return -(jnp.maximum(z, 0.0) + jnp.log(1.0 + e)), e


def _fcum_fwd(fl, bf):
    s = fl.shape[0]
    nb = s // LANE

    def body(fl_ref, bf_ref, fc_ref, fkt_ref):
        r = lax.broadcasted_iota(jnp.int32, (LANE, LANE), 0)
        c = lax.broadcasted_iota(jnp.int32, (LANE, LANE), 1)
        tri = (c <= r).astype(F32)

        def step(b, carry):
            r0 = pl.multiple_of(b * LANE, LANE)
            xb = fl_ref[pl.ds(r0, LANE), :] + bf_ref[...]
            ls = _log_sigmoid_parts(-xb)[0]
            cs = jnp.dot(tri, ls, precision=lax.Precision.HIGHEST, preferred_element_type=F32) + carry
            fc_ref[pl.ds(r0, LANE), :] = cs
            fkt_ref[b] = cs.T[:N_FGATE, :]
            return cs[LANE - 1:LANE, :]

        lax.fori_loop(0, nb, step, jnp.zeros((1, LANE), F32))

    return pl.pallas_call(
        body, name="fcum_fwd",
        out_shape=[jax.ShapeDtypeStruct((s, LANE), F32), jax.ShapeDtypeStruct((nb, N_FGATE, LANE), F32)],
    )(fl, bf)


def _attn_specs(s, col0):
    return [pl.BlockSpec((TQ, LANE), lambda hp, i: (i, col0 + hp)),
            pl.BlockSpec((s, LANE), lambda hp, i: (0, col0 + 4 + hp)),
            pl.BlockSpec((s, LANE), lambda hp, i: (0, col0 + 8 + hp))]


def _tile_iotas():
    lane = lax.broadcasted_iota(jnp.int32, (TQ, LANE), 1)
    row = lax.broadcasted_iota(jnp.int32, (TQ, TQ), 0)
    col = lax.broadcasted_iota(jnp.int32, (TQ, TQ), 1)
    return lane, row, col


def _sub_blocks():
    return [slice(j * LANE, (j + 1) * LANE) for j in range(TQ // LANE)]


def _tri(below):
    r = lax.broadcasted_iota(jnp.int32, (LANE, LANE), 0)
    c = lax.broadcasted_iota(jnp.int32, (LANE, LANE), 1)
    t = jnp.concatenate([((r > c) if below else (r < c)).astype(MXU), jnp.ones((LANE, LANE), MXU)], axis=1)
    return jnp.concatenate([t, t], axis=0)


def _call_with_riders(body, name, grid, in_specs, out_specs, out_shape, scratch, args, riders, gather, limit=None):
    nr, n_in, n_out, n_sc = len(riders), len(in_specs), len(out_specs), len(scratch)

    def at_step(which):
        hit = None
        for d, n in enumerate(grid):
            here = pl.program_id(d) == (0 if which == "first" else n - 1)
            hit = here if hit is None else hit & here
        return hit

    def wrapped(*refs):
        ins, rin = refs[:n_in], refs[n_in:n_in + nr]
        outs, rout = refs[n_in + nr:n_in + nr + n_out], refs[n_in + nr + n_out:n_in + 2 * nr + n_out]
        own, sems = refs[n_in + 2 * nr + n_out:n_in + 2 * nr + n_out + n_sc], refs[n_in + 2 * nr + n_out + n_sc:]
        if nr:
            @pl.when(at_step("first"))
            def _():
                _exchange_start(rin, rout, sems, gather)

        body(*ins, *outs, *own)
        if nr:
            @pl.when(at_step("last"))
            def _():
                _exchange_wait(rin, rout, sems, gather)

    res = pl.pallas_call(
        wrapped, name=name, grid=grid,
        in_specs=list(in_specs) + [_ANY] * nr, out_specs=list(out_specs) + [_ANY] * nr,
        out_shape=list(out_shape) + _exchange_out_shape(riders),
        scratch_shapes=list(scratch) + (_exchange_sems(nr) if nr else []),
        compiler_params=_params(limit),
    )(*args, *riders)
    return res[:n_out], res[n_out:]


def _sb_fwd(qkv, riders=()):
    s = qkv.shape[0]
    nq = s // TQ
    assert nq <= LANE

    def body(q_ref, k_ref, v_ref, o_ref, rs_ref):
        i = pl.program_id(1)
        lane, row, col = _tile_iotas()
        u2 = _tri(True)
        diag = col < row
        q = q_ref[...]
        qms = [jnp.where(hm, q, jnp.zeros_like(q)) for hm in (lane < HEAD_DIM, lane >= HEAD_DIM)]

        def step(kb, carry, masked):
            k0 = pl.multiple_of(kb * TQ, TQ)
            k = k_ref[pl.ds(k0, TQ), :]
            v = v_ref[pl.ds(k0, TQ), :]
            new = []
            for qm, (run, acc, rt) in zip(qms, carry):
                z = _dot_nt(qm, k)
                lneg, _ = _log_sigmoid_parts(z)
                lpos = z + lneg
                if masked:
                    lneg = jnp.where(diag, lneg, 0.0)
                rt = jnp.where(lane == kb, run, rt)
                a = []
                for sl in reversed(_sub_blocks()):
                    st = _dot(_split(lneg[:, sl]), u2)
                    a.append(jnp.exp(lpos[:, sl] + st[:, :LANE] + run))
                    run = run + st[:, LANE:]
                a = jnp.concatenate(a[::-1], axis=1)
                if masked:
                    a = jnp.where(diag, a, 0.0)
                new.append((run, acc + _dot(a.astype(MXU), v), rt))
            return tuple(new)

        zero = jnp.zeros((TQ, LANE), F32)
        carry = step(i, ((zero, zero, zero),) * 2, True)
        carry = lax.fori_loop(0, i, lambda j, cr: step(i - 1 - j, cr, False), carry)
        rs_ref[0] = carry[0][2]
        rs_ref[1] = carry[1][2]
        o_ref[...] = jnp.where(lane < HEAD_DIM, carry[0][1], carry[1][1]).astype(o_ref.dtype)

    return _call_with_riders(
        body, "sb_fwd", (4, nq), _attn_specs(s, 0),
        [pl.BlockSpec((TQ, LANE), lambda hp, i: (i, hp)), pl.BlockSpec((2, TQ, LANE), lambda hp, i: (hp, i, 0))],
        [jax.ShapeDtypeStruct((s, WIDTH), MXU), jax.ShapeDtypeStruct((8, s, LANE), F32)], [], (qkv, qkv, qkv), riders, True)


def _sb_bwd(qkv, do, rs, riders=()):
    s = qkv.shape[0]
    nq = s // TQ

    def body(q_ref, k_ref, v_ref, do_ref, rs_ref, dq_ref, dk_ref, dv_ref, dk_acc, dv_acc):
        i = pl.program_id(1)

        @pl.when(i == 0)
        def _():
            dk_acc[...] = jnp.zeros_like(dk_acc)
            dv_acc[...] = jnp.zeros_like(dv_acc)

        lane, row, col = _tile_iotas()
        u2 = _tri(True)
        l2 = _tri(False)
        diag = col < row
        q = q_ref[...]
        do = do_ref[...]
        heads = [(jnp.where(hm, q, jnp.zeros_like(q)), jnp.where(hm, do, jnp.zeros_like(do)), rs_ref[hh])
                 for hh, hm in enumerate((lane < HEAD_DIM, lane >= HEAD_DIM))]

        def step(kb, carry, masked):
            k0 = pl.multiple_of(kb * TQ, TQ)
            k = k_ref[pl.ds(k0, TQ), :]
            v = v_ref[pl.ds(k0, TQ), :]
            new, dk, dv = [], 0.0, 0.0
            for (qm, dom, rblk), (gpre, dq) in zip(heads, carry):
                z = _dot_nt(qm, k)
                lneg, e = _log_sigmoid_parts(z)
                lpos = z + lneg
                if masked:
                    lneg = jnp.where(diag, lneg, 0.0)
                run = jnp.sum(jnp.where(lane == kb, rblk, 0.0), axis=1, keepdims=True) + jnp.zeros((TQ, LANE), F32)
                a = []
                for sl in reversed(_sub_blocks()):
                    st = _dot(_split(lneg[:, sl]), u2)
                    a.append(jnp.exp(lpos[:, sl] + st[:, :LANE] + run))
                    run = run + st[:, LANE:]
                a = jnp.concatenate(a[::-1], axis=1)
                if masked:
                    a = jnp.where(diag, a, 0.0)
                g = a * _dot_nt(dom, v)
                pre = []
                for sl in _sub_blocks():
                    pt = _dot(_split(g[:, sl]), l2)
                    pre.append(gpre + pt[:, :LANE])
                    gpre = gpre + pt[:, LANE:]
                sig = jnp.where(z >= 0.0, 1.0, e) / (1.0 + e)
                dz = g - (g + jnp.concatenate(pre, axis=1)) * sig
                if masked:
                    dz = jnp.where(diag, dz, 0.0)
                dzb = dz.astype(MXU)
                dk = dk + _dot_tn(dzb, qm)
                dv = dv + _dot_tn(a.astype(MXU), dom)
                new.append((gpre, dq + _dot(dzb, k)))
            dk_acc[pl.ds(k0, TQ), :] += dk
            dv_acc[pl.ds(k0, TQ), :] += dv
            return tuple(new)

        zero = jnp.zeros((TQ, LANE), F32)
        carry = step(i, lax.fori_loop(0, i, lambda kb, cr: step(kb, cr, False), ((zero, zero),) * 2), True)
        dq_ref[...] = (jnp.where(lane < HEAD_DIM, carry[0][1], carry[1][1]) * QK_SCALE).astype(dq_ref.dtype)

        @pl.when(i == nq - 1)
        def _():
            dk_ref[...] = dk_acc[...].astype(dk_ref.dtype)
            dv_ref[...] = dv_acc[...].astype(dv_ref.dtype)

    blk = pl.BlockSpec((TQ, LANE), lambda hp, i: (i, hp))
    whole = pl.BlockSpec((s, LANE), lambda hp, i: (0, hp))
    return _call_with_riders(
        body, "sb_bwd", (4, nq), _attn_specs(s, 0) + [blk, pl.BlockSpec((2, TQ, LANE), lambda hp, i: (hp, i, 0))],
        [blk, whole, whole], [jax.ShapeDtypeStruct((s, WIDTH), MXU)] * 3,
        [pltpu.VMEM((s, LANE), F32), pltpu.VMEM((s, LANE), F32)], (qkv, qkv, qkv, do, rs), riders, False)


def _key_bias(fkt_ref, kb, h):
    n_sub = TQ // LANE
    return jnp.concatenate([fkt_ref[kb * n_sub + j, pl.ds(h, 1), :] for j in range(n_sub)], axis=1)


def _fox_fwd(qkv, fc, fkt):
    s = qkv.shape[0]
    nq = s // TQ
    nb = fkt.shape[0]

    def body(q_ref, k_ref, v_ref, fq_ref, fkt_ref, o_ref, lse_ref):
        hp = pl.program_id(0)
        i = pl.program_id(1)
        lane, row, col = _tile_iotas()
        diag = col <= row
        q = q_ref[...]
        fqb = fq_ref[...]
        heads = []
        for hh in range(2):
            h = 2 * hp + hh
            hm = (lane >= HEAD_DIM) if hh else (lane < HEAD_DIM)
            heads.append((h, jnp.where(hm, q, jnp.zeros_like(q)), jnp.sum(jnp.where(lane == h, fqb, 0.0), axis=1, keepdims=True)))

        def step(kb, carry, masked):
            k0 = pl.multiple_of(kb * TQ, TQ)
            k = k_ref[pl.ds(k0, TQ), :]
            v = v_ref[pl.ds(k0, TQ), :]
            new = []
            for (h, qm, fq), (m, l, acc) in zip(heads, carry):
                z = _dot_nt(qm, k) + fq - _key_bias(fkt_ref, kb, h)
                if masked:
                    z = jnp.where(diag, z, NEG)
                mn = jnp.maximum(m, jnp.max(z, axis=1, keepdims=True))
                p = jnp.exp(z - mn)
                alpha = jnp.exp(m - mn)
                new.append((mn, alpha * l + jnp.sum(p, axis=1, keepdims=True), alpha * acc + _dot(p.astype(MXU), v)))
            return tuple(new)

        init = ((jnp.full((TQ, 1), NEG, F32), jnp.zeros((TQ, 1), F32), jnp.zeros((TQ, LANE), F32)),) * 2
        carry = step(i, lax.fori_loop(0, i, lambda kb, cr: step(kb, cr, False), init), True)
        outs = []
        for hh, (m, l, acc) in enumerate(carry):
            outs.append(acc / l)
            lse_ref[hh] = jnp.broadcast_to(m + jnp.log(l), (TQ, LANE))
        o_ref[...] = jnp.where(lane < HEAD_DIM, outs[0], outs[1]).astype(o_ref.dtype)

    return pl.pallas_call(
        body, name="fox_fwd", grid=(4, nq),
        in_specs=_attn_specs(s, 12) + [pl.BlockSpec((TQ, LANE), lambda hp, i: (i, 0)),
                                       pl.BlockSpec((nb, N_FGATE, LANE), lambda hp, i: (0, 0, 0))],
        out_specs=[pl.BlockSpec((TQ, LANE), lambda hp, i: (i, hp)), pl.BlockSpec((2, TQ, LANE), lambda hp, i: (hp, i, 0))],
        out_shape=[jax.ShapeDtypeStruct((s, WIDTH), MXU), jax.ShapeDtypeStruct((8, s, LANE), F32)],
    )(qkv, qkv, qkv, fc, fkt)


def _fox_bwd(qkv, fc, fkt, do, o, lse):
    s = qkv.shape[0]
    nq = s // TQ
    nb = fkt.shape[0]

    def body(q_ref, k_ref, v_ref, fq_ref, fkt_ref, do_ref, o_ref, lse_ref, dq_ref, dk_ref, dv_ref, dfk_ref, dfq_ref, dk_acc, dv_acc):
        hp = pl.program_id(0)
        i = pl.program_id(1)

        @pl.when(i == 0)
        def _():
            dk_acc[...] = jnp.zeros_like(dk_acc)
            dv_acc[...] = jnp.zeros_like(dv_acc)

        @pl.when((i == 0) & (hp == 0))
        def _():
            dfk_ref[...] = jnp.zeros_like(dfk_ref)

        lane, row, col = _tile_iotas()
        diag = col <= row
        q = q_ref[...]
        do = do_ref[...]
        dof = do.astype(F32) * o_ref[...].astype(F32)
        fqb = fq_ref[...]
        heads = []
        for hh in range(2):
            h = 2 * hp + hh
            hm = (lane >= HEAD_DIM) if hh else (lane < HEAD_DIM)
            heads.append((h, jnp.where(hm, q, jnp.zeros_like(q)), jnp.where(hm, do, jnp.zeros_like(do)),
                          jnp.sum(jnp.where(hm, dof, 0.0), axis=1, keepdims=True),
                          jnp.sum(jnp.where(lane == h, fqb, 0.0), axis=1, keepdims=True), lse_ref[hh][:, :1]))

        def step(kb, carry, masked):
            k0 = pl.multiple_of(kb * TQ, TQ)
            k = k_ref[pl.ds(k0, TQ), :]
            v = v_ref[pl.ds(k0, TQ), :]
            new, dk, dv = [], 0.0, 0.0
            for (h, qm, dom, delta, fq, lse_t), (dq, rsum) in zip(heads, carry):
                z = _dot_nt(qm, k) + fq - _key_bias(fkt_ref, kb, h)
                if masked:
                    z = jnp.where(diag, z, NEG)
                p = jnp.exp(z - lse_t)
                ds = p * (_dot_nt(dom, v) - delta)
                dsb = ds.astype(MXU)
                dk = dk + _dot_tn(dsb, qm)
                dv = dv + _dot_tn(p.astype(MXU), dom)
                csum = _colsum(ds)
                for j, sl in enumerate(_sub_blocks()):
                    dfk_ref[kb * len(_sub_blocks()) + j, pl.ds(h, 1), :] += -csum[:, sl]
                new.append((dq + _dot(dsb, k), rsum + jnp.sum(ds, axis=1, keepdims=True)))
            dk_acc[pl.ds(k0, TQ), :] += dk
            dv_acc[pl.ds(k0, TQ), :] += dv
            return tuple(new)

        init = ((jnp.zeros((TQ, LANE), F32), jnp.zeros((TQ, 1), F32)),) * 2
        carry = step(i, lax.fori_loop(0, i, lambda kb, cr: step(kb, cr, False), init), True)
        dq_ref[...] = (jnp.where(lane < HEAD_DIM, carry[0][0], carry[1][0]) * QK_SCALE).astype(dq_ref.dtype)
        dfq_ref[0] = jnp.where(lane == heads[0][0], carry[0][1], jnp.where(lane == heads[1][0], carry[1][1], 0.0))

        @pl.when(i == nq - 1)
        def _():
            dk_ref[...] = dk_acc[...].astype(dk_ref.dtype)
            dv_ref[...] = dv_acc[...].astype(dv_ref.dtype)

    blk = pl.BlockSpec((TQ, LANE), lambda hp, i: (i, hp))
    whole = pl.BlockSpec((s, LANE), lambda hp, i: (0, hp))
    pair = pl.BlockSpec((2, TQ, LANE), lambda hp, i: (hp, i, 0))
    fkt_spec = pl.BlockSpec((nb, N_FGATE, LANE), lambda hp, i: (0, 0, 0))
    return pl.pallas_call(
        body, name="fox_bwd", grid=(4, nq),
        in_specs=_attn_specs(s, 12) + [pl.BlockSpec((TQ, LANE), lambda hp, i: (i, 0)), fkt_spec, blk, blk, pair],
        out_specs=[blk, whole, whole, fkt_spec, pl.BlockSpec((1, TQ, LANE), lambda hp, i: (hp, i, 0))],
        out_shape=[jax.ShapeDtypeStruct((s, WIDTH), MXU)] * 3
        + [jax.ShapeDtypeStruct((nb, N_FGATE, LANE), F32), jax.ShapeDtypeStruct((4, s, LANE), F32)],
        scratch_shapes=[pltpu.VMEM((s, LANE), F32), pltpu.VMEM((s, LANE), F32)],
    )(qkv, qkv, qkv, fc, fkt, do, o, lse)


def _fcum_bwd(dfkt, dfq, fl, bf):
    s = fl.shape[0]
    nb = s // LANE

    def body(dfkt_ref, dfq_ref, fl_ref, bf_ref, df_ref, dbf_ref, tail_ref):
        @pl.when(pl.program_id(0) == 0)
        def _():
            tail_ref[...] = jnp.zeros_like(tail_ref)
            dbf_ref[...] = jnp.zeros_like(dbf_ref)

        r = lax.broadcasted_iota(jnp.int32, (LANE, LANE), 0)
        c = lax.broadcasted_iota(jnp.int32, (LANE, LANE), 1)
        tri = (c >= r).astype(F32)
        dfc = jnp.concatenate([dfkt_ref[0], jnp.zeros((LANE - N_FGATE, LANE), F32)], axis=0).T
        dfc = dfc + ((dfq_ref[0] + dfq_ref[1]) + (dfq_ref[2] + dfq_ref[3]))
        dls = jnp.dot(tri, dfc, precision=lax.Precision.HIGHEST, preferred_element_type=F32) + tail_ref[...]
        xb = fl_ref[...] + bf_ref[...]
        e = jnp.exp(-jnp.abs(xb))
        dfl = dls * (jnp.where(xb >= 0.0, e, 1.0) / (1.0 + e))
        df_ref[...] = dfl.astype(df_ref.dtype)
        tail_ref[...] = dls[0:1, :]
        dbf_ref[...] += _colsum(dfl)

    return pl.pallas_call(
        body, name="fcum_bwd", grid=(nb,),
        in_specs=[pl.BlockSpec((1, N_FGATE, LANE), lambda j: (nb - 1 - j, 0, 0)), pl.BlockSpec((4, LANE, LANE), lambda j: (0, nb - 1 - j, 0)),
                  pl.BlockSpec((LANE, LANE), lambda j: (nb - 1 - j, 0)), _fixed(1, LANE)],
        out_specs=[pl.BlockSpec((LANE, LANE), lambda j: (nb - 1 - j, 0)), _fixed(1, LANE)],
        out_shape=[jax.ShapeDtypeStruct((s, LANE), MXU), jax.ShapeDtypeStruct((1, LANE), F32)],
        scratch_shapes=[pltpu.VMEM((1, LANE), F32)],
    )(dfkt, dfq, fl, bf)


def _mix_fwd(x, o_sb, o_fx, gl, w_sb, w_fx, w_o, g1, ln1_g, ln1_b, sh2, sc2):
    s = x.shape[0]
    tm = 256

    def body(x_ref, osb_ref, ofx_ref, gl_ref, wsb_ref, wfx_ref, wo_ref, g1_ref, lg_ref, lb_ref, sh_ref, sc_ref, r1_ref, u2_ref):
        mixin = (_sigmoid(gl_ref[:, :D]) * _dot(osb_ref[...], wsb_ref[...])
                 + _sigmoid(gl_ref[:, D:]) * _dot(ofx_ref[...], wfx_ref[...]))
        r1 = ALPHA * x_ref[...] + g1_ref[...] * _dot(mixin.astype(MXU), wo_ref[...])
        r1_ref[...] = r1
        x1 = _ln(r1)[0] * lg_ref[...] + lb_ref[...]
        u2_ref[...] = (_ln(x1)[0] * (1.0 + sc_ref[...]) + sh_ref[...]).astype(MXU)

    vec = _fixed(1, D)
    return pl.pallas_call(
        body, name="mix_fwd", grid=(s // tm,),
        in_specs=[_rows(tm, D), _rows(tm, WIDTH), _rows(tm, WIDTH), _rows(tm, 2 * D), _res(w_sb), _res(w_fx), _res(w_o),
                  vec, vec, vec, vec, vec],
        out_specs=[_rows(tm, D), _rows(tm, D)],
        out_shape=[jax.ShapeDtypeStruct((s, D), F32), jax.ShapeDtypeStruct((s, D), MXU)],
        compiler_params=_params(VMEM_BIG),
    )(x, o_sb, o_fx, gl, w_sb, w_fx, w_o, g1, ln1_g, ln1_b, sh2, sc2)


def _ffn_fwd(r1, u2, tgt, w_g, w_u, w_d, g2, ln1_g, ln1_b, ln2_g, ln2_b):
    s = r1.shape[0]
    tm = 256

    def body(r1_ref, u2_ref, t_ref, wg_ref, wu_ref, wd_ref, g2_ref, l1g_ref, l1b_ref, l2g_ref, l2b_ref,
             hg_ref, hu_ref, dxa_ref, dh_ref, acc_ref):
        @pl.when(pl.program_id(0) == 0)
        def _():
            acc_ref[...] = jnp.zeros_like(acc_ref)

        u2 = u2_ref[...]
        hg = _dot_nt(u2, wg_ref[...])
        hu = _dot_nt(u2, wu_ref[...])
        hg_ref[...] = hg
        hu_ref[...] = hu
        h = _dot((hg * _sigmoid(hg) * hu).astype(MXU), wd_ref[...])
        x1 = _ln(r1_ref[...])[0] * l1g_ref[...] + l1b_ref[...]
        xh2, rstd2 = _ln(ALPHA * x1 + g2_ref[...] * h)
        err = xh2 * l2g_ref[...] + l2b_ref[...] - t_ref[...]
        dy = err * (1.0 / D)
        dr2 = _ln_bwd(dy * l2g_ref[...], xh2, rstd2)
        dxa_ref[...] = ALPHA * dr2
        dh_ref[...] = (g2_ref[...] * dr2).astype(MXU)
        acc_ref[0:1, :] += _colsum(dr2 * h)
        acc_ref[1:2, :] += _colsum(dy * xh2)
        acc_ref[2:3, :] += _colsum(dy)
        acc_ref[3:4, :] += _colsum(err * err) * (0.5 / D)

    vec = _fixed(1, D)
    return pl.pallas_call(
        body, name="ffn_fwd", grid=(s // tm,),
        in_specs=[_rows(tm, D), _rows(tm, D), _rows(tm, D), _res(w_g), _res(w_u), _res(w_d), vec, vec, vec, vec, vec],
        out_specs=[_rows(tm, D_FF), _rows(tm, D_FF), _rows(tm, D), _rows(tm, D), _fixed(8, D)],
        out_shape=[jax.ShapeDtypeStruct((s, D_FF), F32), jax.ShapeDtypeStruct((s, D_FF), F32),
                   jax.ShapeDtypeStruct((s, D), F32), jax.ShapeDtypeStruct((s, D), MXU), jax.ShapeDtypeStruct((8, D), F32)],
        compiler_params=_params(VMEM_BIG),
    )(r1, u2, tgt, w_g, w_u, w_d, g2, ln1_g, ln1_b, ln2_g, ln2_b)


def _ffn_bwd(dh, hg, hu, w_g, w_u, w_d):
    s = dh.shape[0]
    tm = 256
    half = D_FF // 2

    def body(dh_ref, hg_ref, hu_ref, wg_ref, wu_ref, wd_ref, act_ref, dhg_ref, dhu_ref, du2_ref):
        dh = dh_ref[...]
        du2 = jnp.zeros((tm, D), F32)
        for c0 in (0, half):
            cols = slice(c0, c0 + half)
            dact = _dot_nt(dh, wd_ref[cols, :])
            hg = hg_ref[:, cols]
            hu = hu_ref[:, cols]
            sg = _sigmoid(hg)
            sl = hg * sg
            act_ref[:, cols] = (sl * hu).astype(MXU)
            dhg = (dact * hu * (sg * (1.0 + hg * (1.0 - sg)))).astype(MXU)
            dhu = (dact * sl).astype(MXU)
            dhg_ref[:, cols] = dhg
            dhu_ref[:, cols] = dhu
            du2 = du2 + _dot(dhg, wg_ref[cols, :]) + _dot(dhu, wu_ref[cols, :])
        du2_ref[...] = du2

    return pl.pallas_call(
        body, name="ffn_bwd", grid=(s // tm,),
        in_specs=[_rows(tm, D), _rows(tm, D_FF), _rows(tm, D_FF), _res(w_g), _res(w_u), _res(w_d)],
        out_specs=[_rows(tm, D_FF), _rows(tm, D_FF), _rows(tm, D_FF), _rows(tm, D)],
        out_shape=[jax.ShapeDtypeStruct((s, D_FF), MXU)] * 3 + [jax.ShapeDtypeStruct((s, D), F32)],
        compiler_params=_params(VMEM_BIG),
    )(dh, hg, hu, w_g, w_u, w_d)


def _mix_bwd(du2, dxa, r1, o_sb, o_fx, gl, w_sb, w_fx, w_o, g1, ln1_g, ln1_b, sc2):
    s = r1.shape[0]
    tm = 256

    def body(du2_ref, dxa_ref, r1_ref, osb_ref, ofx_ref, gl_ref, wsb_ref, wfx_ref, wo_ref, g1_ref, lg_ref, lb_ref, sc_ref,
             dx_ref, mixin_ref, dmix_ref, dysb_ref, dyfx_ref, dosb_ref, dofx_ref, dgl_ref, dbg_ref, acc_ref):
        @pl.when(pl.program_id(0) == 0)
        def _():
            acc_ref[...] = jnp.zeros_like(acc_ref)
            dbg_ref[...] = jnp.zeros_like(dbg_ref)

        du2 = du2_ref[...]
        xh1, rstd1 = _ln(r1_ref[...])
        x1 = xh1 * lg_ref[...] + lb_ref[...]
        n1, rstdn = _ln(x1)
        dx1 = dxa_ref[...] + _ln_bwd(du2 * (1.0 + sc_ref[...]), n1, rstdn)
        dr1 = _ln_bwd(dx1 * lg_ref[...], xh1, rstd1)
        dx_ref[...] = ALPHA * dr1
        ysb = _dot(osb_ref[...], wsb_ref[...])
        yfx = _dot(ofx_ref[...], wfx_ref[...])
        gs = _sigmoid(gl_ref[:, :D])
        gf = _sigmoid(gl_ref[:, D:])
        mixin = (gs * ysb + gf * yfx).astype(MXU)
        mixin_ref[...] = mixin
        mix = _dot(mixin, wo_ref[...])
        dmix = (g1_ref[...] * dr1).astype(MXU)
        dmix_ref[...] = dmix
        dmixin = _dot_nt(dmix, wo_ref[...])
        dysb = (dmixin * gs).astype(MXU)
        dyfx = (dmixin * gf).astype(MXU)
        dysb_ref[...] = dysb
        dyfx_ref[...] = dyfx
        dosb_ref[...] = _dot_nt(dysb, wsb_ref[...]).astype(MXU)
        dofx_ref[...] = _dot_nt(dyfx, wfx_ref[...]).astype(MXU)
        dgs = dmixin * ysb * gs * (1.0 - gs)
        dgf = dmixin * yfx * gf * (1.0 - gf)
        dgl_ref[:, :D] = dgs.astype(MXU)
        dgl_ref[:, D:] = dgf.astype(MXU)
        dbg_ref[:, :D] += _colsum(dgs)
        dbg_ref[:, D:] += _colsum(dgf)
        acc_ref[0:1, :] += _colsum(du2)
        acc_ref[1:2, :] += _colsum(du2 * n1)
        acc_ref[2:3, :] += _colsum(dx1 * xh1)
        acc_ref[3:4, :] += _colsum(dx1)
        acc_ref[4:5, :] += _colsum(dr1 * mix)

    vec = _fixed(1, D)
    return pl.pallas_call(
        body, name="mix_bwd", grid=(s // tm,),
        in_specs=[_rows(tm, D), _rows(tm, D), _rows(tm, D), _rows(tm, WIDTH), _rows(tm, WIDTH), _rows(tm, 2 * D),
                  _res(w_sb), _res(w_fx), _res(w_o), vec, vec, vec, vec],
        out_specs=[_rows(tm, D), _rows(tm, D), _rows(tm, D), _rows(tm, D), _rows(tm, D), _rows(tm, WIDTH), _rows(tm, WIDTH),
                   _rows(tm, 2 * D), _fixed(1, 2 * D), _fixed(8, D)],
        out_shape=[jax.ShapeDtypeStruct((s, D), F32)] + [jax.ShapeDtypeStruct((s, D), MXU)] * 4
        + [jax.ShapeDtypeStruct((s, WIDTH), MXU)] * 2
        + [jax.ShapeDtypeStruct((s, 2 * D), MXU), jax.ShapeDtypeStruct((1, 2 * D), F32), jax.ShapeDtypeStruct((8, D), F32)],
        compiler_params=_params(VMEM_BIG),
    )(du2, dxa, r1, o_sb, o_fx, gl, w_sb, w_fx, w_o, g1, ln1_g, ln1_b, sc2)


def _in_bwd(pieces, x, dxa, w_all, sc1, riders=()):
    s = x.shape[0]
    tm = 256
    n_p = len(pieces)

    def body(*refs):
        p_refs = refs[:n_p]
        x_ref, dxa_ref, w_ref, sc_ref, gx_ref, acc_ref = refs[n_p:]

        @pl.when(pl.program_id(0) == 0)
        def _():
            acc_ref[...] = jnp.zeros_like(acc_ref)

        du1 = jnp.zeros((tm, D), F32)
        for p_ref, (arr, c0) in zip(p_refs, pieces):
            du1 = du1 + _dot(p_ref[...], w_ref[c0:c0 + arr.shape[1], :])
        n0, rstd0 = _ln(x_ref[...])
        gx_ref[...] = dxa_ref[...] + _ln_bwd(du1 * (1.0 + sc_ref[...]), n0, rstd0)
        acc_ref[0:1, :] += _colsum(du1)
        acc_ref[1:2, :] += _colsum(du1 * n0)

    return _call_with_riders(
        body, "in_bwd", (s // tm,),
        [_rows(tm, a.shape[1]) for a, _ in pieces] + [_rows(tm, D), _rows(tm, D), _res(w_all), _fixed(1, D)],
        [_rows(tm, D), _fixed(8, D)], [jax.ShapeDtypeStruct((s, D), F32), jax.ShapeDtypeStruct((8, D), F32)], [],
        (*[a for a, _ in pieces], x, dxa, w_all, sc1), riders, False, VMEM_BIG)


def _matmul_tn(a, b, name):
    s, m = a.shape
    n = b.shape[1]
    tm = 512 if m % 512 == 0 else (m if m < 512 else m // 2)
    tn = n // 2 if n > 2048 else n
    ts = 512
    assert m % tm == 0 and tm % LANE == 0 and n % tn == 0 and tn % LANE == 0 and s % ts == 0

    def body(a_ref, b_ref, o_ref):
        @pl.when(pl.program_id(2) == 0)
        def _():
            o_ref[...] = jnp.zeros_like(o_ref)

        o_ref[...] += _dot_tn(a_ref[...], b_ref[...])

    return pl.pallas_call(
        body, name=name, grid=(m // tm, n // tn, s // ts),
        in_specs=[pl.BlockSpec((ts, tm), lambda i, j, k: (k, i)), pl.BlockSpec((ts, tn), lambda i, j, k: (k, j))],
        out_specs=pl.BlockSpec((tm, tn), lambda i, j, k: (i, j)),
        out_shape=jax.ShapeDtypeStruct((m, n), F32),
        compiler_params=_params(VMEM_BIG),
    )(a, b)


def _local_step(x, tgt, ada, w_all, b_gate, bf_pad, late_weights, early_grads, w_in_grads, ln1_g, ln1_b, ln2_g, ln2_b):
    sh1, sc1, g1, sh2, sc2, g2 = ada
    u1, qkv, fl, gl = _in_proj(x, sh1, sc1, w_all, b_gate)
    fc, fkt = _fcum_fwd(fl, bf_pad)
    late_riders, late_full = late_weights
    (o_sb, rs), late_gathered = _sb_fwd(qkv, late_riders)
    w_sb, w_fx, w_o, w_g, w_u, w_d = late_full(late_gathered)
    o_fx, lse = _fox_fwd(qkv, fc, fkt)
    r1, u2 = _mix_fwd(x, o_sb, o_fx, gl, w_sb, w_fx, w_o, g1, ln1_g, ln1_b, sh2, sc2)
    hg, hu, dxa2, dh, acc_f = _ffn_fwd(r1, u2, tgt, w_g, w_u, w_d, g2, ln1_g, ln1_b, ln2_g, ln2_b)
    act, dhg, dhu, du2 = _ffn_bwd(dh, hg, hu, w_g, w_u, w_d)
    dxa1, mixin, dmix, dysb, dyfx, dosb, dofx, dgl, dbg, acc_m = _mix_bwd(
        du2, dxa2, r1, o_sb, o_fx, gl, w_sb, w_fx, w_o, g1, ln1_g, ln1_b, sc2)
    early = dict(w_sb_out=_matmul_tn(o_sb, dysb, "dw_sb_out"), w_fox_out=_matmul_tn(o_fx, dyfx, "dw_fox_out"),
                 w_o=_matmul_tn(mixin, dmix, "dw_o"), w_ffn_gate=_matmul_tn(dhg, u2, "dw_ffn_gate"),
                 w_ffn_up=_matmul_tn(dhu, u2, "dw_ffn_up"), w_ffn_down=_matmul_tn(act, dh, "dw_ffn_down"))
    (dq_sb, dk_sb, dv_sb), early_received = _sb_bwd(qkv, dosb, rs, early_grads(early))
    dq_fx, dk_fx, dv_fx, dfkt, dfq = _fox_bwd(qkv, fc, fkt, dofx, o_fx, lse)
    df, dbf = _fcum_bwd(dfkt, dfq, fl, bf_pad)
    pieces = [(dq_sb, 0), (dk_sb, WIDTH), (dv_sb, 2 * WIDTH), (dq_fx, 3 * WIDTH), (dk_fx, 4 * WIDTH), (dv_fx, 5 * WIDTH),
              (df, OFF_FGATE), (dgl, OFF_FGATE + LANE)]
    dw_in = [_matmul_tn(p, u1, f"dw_in_{j}") for j, (p, _) in enumerate(pieces)]
    (grad_x, acc_i), w_in_received = _in_bwd(pieces, x, dxa1, w_all, sc1, w_in_grads(dw_in))
    return dict(
        loss_lanes=acc_f[3:4], grad_x=grad_x, dw_in=dw_in, early=early, early_received=early_received,
        w_in_received=w_in_received,
        d_ada=[acc_i[0:1], acc_i[1:2], acc_m[4:5], acc_m[0:1], acc_m[1:2], acc_f[0:1]],
        dln1_g=acc_m[2:3], dln1_b=acc_m[3:4], dln2_g=acc_f[1:2], dln2_b=acc_f[2:3], db_gate=dbg, db_forget=dbf)


_MESH_ID = pl.DeviceIdType.MESH
_ANY = pl.BlockSpec(memory_space=pl.ANY)
_VMEM = pl.BlockSpec(memory_space=pltpu.VMEM)


def _mesh_pos():
    return lax.axis_index("x"), lax.axis_index("y"), lax.axis_index("c")


def _other_chips(x, y):
    return [(1 - x, y), (x, 1 - y), (1 - x, 1 - y)]


def _allgather_rows(v, name):
    n = v.shape[1]

    def body(v_ref, out_ref, send_sems, recv_sems, local_sem):
        x, y, c = _mesh_pos()
        me = 4 * x + 2 * y + c
        mine = pltpu.make_async_copy(v_ref, out_ref.at[me], local_sem)
        mine.start()
        copies = []
        for d in range(1, 8):
            fx, fy, fc = (d >> 2) & 1, (d >> 1) & 1, d & 1
            to = (1 - x if fx else x, 1 - y if fy else y, 1 - c if fc else c)
            cp = pltpu.make_async_remote_copy(src_ref=v_ref, dst_ref=out_ref.at[me], send_sem=send_sems.at[d - 1],
                                              recv_sem=recv_sems.at[d - 1], device_id=to, device_id_type=_MESH_ID)
            cp.start()
            copies.append(cp)
        for cp in copies:
            cp.wait_recv()
        for cp in copies:
            cp.wait_send()
        mine.wait()

    return pl.pallas_call(
        body, name=name, in_specs=[_VMEM], out_specs=_VMEM,
        out_shape=jax.ShapeDtypeStruct((8, 1, n), v.dtype),
        scratch_shapes=[pltpu.SemaphoreType.DMA((7,)), pltpu.SemaphoreType.DMA((7,)), pltpu.SemaphoreType.DMA(())],
    )(v)


def _chip_exchange(arrays, name, gather):
    nt = len(arrays)

    def body(*refs):
        ins, outs = refs[:nt], refs[nt:2 * nt]
        _exchange_start(ins, outs, refs[2 * nt:], gather)
        _exchange_wait(ins, outs, refs[2 * nt:], gather)

    return pl.pallas_call(
        body, name=name, in_specs=[_ANY] * nt, out_specs=[_ANY] * nt, out_shape=_exchange_out_shape(arrays),
        scratch_shapes=_exchange_sems(nt),
    )(*arrays)


def _exchange_out_shape(arrays):
    return [jax.ShapeDtypeStruct((4,) + a.shape[-2:], a.dtype) for a in arrays]


def _exchange_sems(nt):
    return [pltpu.SemaphoreType.DMA((3 * nt,)), pltpu.SemaphoreType.DMA((3 * nt,)), pltpu.SemaphoreType.DMA((nt,))]


def _exchange_copies(ins, outs, sems, gather):
    send_sems, recv_sems, local_sems = sems
    x, y, c = _mesh_pos()
    me = 2 * x + y
    local, remote = [], []
    for t in range(len(ins)):
        local.append(pltpu.make_async_copy(ins[t] if gather else ins[t].at[me], outs[t].at[me], local_sems.at[t]))
        for j, (px, py) in enumerate(_other_chips(x, y)):
            remote.append(pltpu.make_async_remote_copy(
                src_ref=ins[t] if gather else ins[t].at[2 * px + py], dst_ref=outs[t].at[me], send_sem=send_sems.at[3 * t + j],
                recv_sem=recv_sems.at[3 * t + j], device_id=(px, py, c), device_id_type=_MESH_ID))
    return local, remote


def _exchange_start(ins, outs, sems, gather):
    local, remote = _exchange_copies(ins, outs, sems, gather)
    for cp in local + remote:
        cp.start()


def _exchange_wait(ins, outs, sems, gather):
    local, remote = _exchange_copies(ins, outs, sems, gather)
    for cp in remote:
        cp.wait_recv()
    for cp in remote:
        cp.wait_send()
    for cp in local:
        cp.wait()


def _sibling_exchange(arrays, name):
    nt = len(arrays)

    def body(*refs):
        ins, outs = refs[:nt], refs[nt:2 * nt]
        send_sems, recv_sems = refs[2 * nt:]
        x, y, c = _mesh_pos()
        copies = []
        for t in range(nt):
            cp = pltpu.make_async_remote_copy(src_ref=ins[t], dst_ref=outs[t], send_sem=send_sems.at[t], recv_sem=recv_sems.at[t],
                                              device_id=(x, y, 1 - c), device_id_type=_MESH_ID)
            cp.start()
            copies.append(cp)
        for cp in copies:
            cp.wait_recv()
        for cp in copies:
            cp.wait_send()

    return pl.pallas_call(
        body, name=name, in_specs=[_ANY] * nt, out_specs=[_ANY] * nt,
        out_shape=[jax.ShapeDtypeStruct(a.shape, a.dtype) for a in arrays],
        scratch_shapes=[pltpu.SemaphoreType.DMA((nt,)), pltpu.SemaphoreType.DMA((nt,))],
    )(*arrays)


def _tiles(r, n):
    for tr in (256, 352, 128):
        if r % tr == 0:
            return tr, n, r // tr, lambda i: (i, 0)
    assert n % 256 == 0
    return r, 256, n // 256, lambda i: (0, i)


def _reduce_chips(own, recv, name):
    r, n = own.shape
    tr, tn, steps, at = _tiles(r, n)

    def body(own_ref, recv_ref, out_ref):
        x, y, _ = _mesh_pos()
        me = 2 * x + y
        total = jnp.zeros((tr, tn), F32)
        for k in range(4):
            total = total + jnp.where(me == k, own_ref[...], recv_ref[k].astype(F32))
        out_ref[...] = total

    blk = pl.BlockSpec((tr, tn), at)
    return pl.pallas_call(
        body, name=name, grid=(steps,),
        in_specs=[blk, pl.BlockSpec((4, tr, tn), lambda i: (0,) + at(i))], out_specs=blk,
        out_shape=jax.ShapeDtypeStruct((r, n), F32),
    )(own, recv)


def _adamw_math(w, g, m, v):
    m = ADAM_B1 * m + (1.0 - ADAM_B1) * g
    v = ADAM_B2 * v + (1.0 - ADAM_B2) * (g * g)
    m_hat = m / (1.0 - ADAM_B1 ** ADAM_STEP)
    v_hat = v / (1.0 - ADAM_B2 ** ADAM_STEP)
    return -ADAM_LR * (m_hat / (jnp.sqrt(v_hat) + ADAM_EPS) + ADAM_WD * w), m, v


def _adamw(w, m, v, g_parts, name):
    r, n = w.shape
    tr, tn, steps, at = _tiles(r, n)
    blk = pl.BlockSpec((tr, tn), at)
    ng = len(g_parts)

    def body(*refs):
        w_ref, m_ref, v_ref = refs[:3]
        g_refs = refs[3:3 + ng]
        g_out, d_out, m_out, v_out = refs[3 + ng:]
        g = g_refs[0][...]
        for gr in g_refs[1:]:
            g = g + gr[...]
        g_out[...] = g
        d_out[...], m_out[...], v_out[...] = _adamw_math(w_ref[...], g, m_ref[...], v_ref[...])

    return pl.pallas_call(
        body, name=name, grid=(steps,),
        in_specs=[blk] * (3 + ng), out_specs=[blk] * 4,
        out_shape=[jax.ShapeDtypeStruct((r, n), F32)] * 4,
    )(w, m, v, *g_parts)


def _ada_fwd(c_all, w_shard, b_shard):
    n = w_shard.shape[1]
    tn = 512

    def body(c_ref, w_ref, b_ref, o_ref):
        cv = c_ref[...]
        ca = (cv * _sigmoid(cv)).astype(MXU)
        o_ref[...] = _dot(ca, w_ref[...].astype(MXU)) + b_ref[...]

    return pl.pallas_call(
        body, name="ada_fwd", grid=(n // tn,),
        in_specs=[_fixed(8, D), pl.BlockSpec((D, tn), lambda j: (0, j)), pl.BlockSpec((1, tn), lambda j: (0, j))],
        out_specs=pl.BlockSpec((8, tn), lambda j: (0, j)),
        out_shape=jax.ShapeDtypeStruct((8, n), F32),
    )(c_all, w_shard, b_shard)


def _ada_bwd(c_all, dada_shard):
    n = dada_shard.shape[1]
    tn = 512

    def body(c_ref, d_ref, o_ref):
        cv = c_ref[...]
        ca = (cv * _sigmoid(cv)).astype(MXU)
        o_ref[...] = _dot_tn(ca, d_ref[...].astype(MXU))

    return pl.pallas_call(
        body, name="ada_bwd", grid=(n // tn,),
        in_specs=[_fixed(8, D), pl.BlockSpec((8, tn), lambda j: (0, j))],
        out_specs=pl.BlockSpec((D, tn), lambda j: (0, j)),
        out_shape=jax.ShapeDtypeStruct((D, n), F32),
    )(c_all, dada_shard)


_SMALL = [("d_ada", N_COND * D), ("ln1_g", D), ("ln1_b", D), ("ln2_g", D), ("ln2_b", D), ("b_gate", 2 * D), ("b_forget", LANE),
          ("loss", D)]
_SMALL_OFF = {}
_o = 0
for _n, _w in _SMALL:
    _SMALL_OFF[_n] = (_o, _w)
    _o += _w
_SMALL_LEN = _o
_SMALL_PARAMS = [("b_ada", "d_ada", N_COND * D), ("b_gate", "b_gate", 2 * D), ("b_forget", "b_forget", N_FGATE),
                 ("ln1_g", "ln1_g", D), ("ln1_b", "ln1_b", D), ("ln2_g", "ln2_g", D), ("ln2_b", "ln2_b", D)]


def _small_update(rows, params):
    npar = len(_SMALL_PARAMS)

    def body(*refs):
        rows_ref = refs[0]
        p_refs = refs[1:1 + 3 * npar]
        loss_ref = refs[1 + 3 * npar]
        o_refs = refs[2 + 3 * npar:]
        total = rows_ref[0]
        for d in range(1, 8):
            total = total + rows_ref[d]
        lo, lw = _SMALL_OFF["loss"]
        loss_ref[...] = jnp.sum(total[:, lo:lo + lw], axis=1, keepdims=True)
        for j, (_, key, n) in enumerate(_SMALL_PARAMS):
            off = _SMALL_OFF[key][0]
            g = total[:, off:off + n]
            w_ref, m_ref, v_ref = p_refs[3 * j:3 * j + 3]
            o_refs[4 * j][...] = g
            o_refs[4 * j + 1][...], o_refs[4 * j + 2][...], o_refs[4 * j + 3][...] = _adamw_math(w_ref[...], g, m_ref[...], v_ref[...])

    flat = [a for p in params for a in p]
    out_shape = [jax.ShapeDtypeStruct((1, 1), F32)] + [jax.ShapeDtypeStruct((1, n), F32) for _, _, n in _SMALL_PARAMS for _ in range(4)]
    return pl.pallas_call(body, name="small_update", out_shape=out_shape)(rows, *flat)


_BIG = [("w_in", "cols_t"), ("w_sb_out", "cols"), ("w_fox_out", "cols"), ("w_o", "rows"),
        ("w_ffn_gate", "cols_t"), ("w_ffn_up", "cols_t"), ("w_ffn_down", "rows")]


def _shard2d(a, how):
    return a[0].T if how == "cols_t" else a[0]


def _unshard(g, how):
    if how == "cols":
        return g.transpose(1, 0, 2).reshape(g.shape[1], 4 * g.shape[2])
    return g.reshape(4 * g.shape[1], g.shape[2])


def _reshard(w, how):
    if how == "cols":
        return w.reshape(w.shape[0], 4, w.shape[1] // 4).transpose(1, 0, 2)
    return w.reshape(4, w.shape[0] // 4, w.shape[1])


def kernel(x, c, w_ada, b_ada, w_in, b_gate, b_forget, w_sb_out, w_fox_out, w_o, ln1_g, ln1_b, w_ffn_gate, w_ffn_up, w_ffn_down, ln2_g, ln2_b, loss_target, m_w_ada, m_b_ada, m_w_in, m_b_gate, m_b_forget, m_w_sb_out, m_w_fox_out, m_w_o, m_ln1_g, m_ln1_b, m_w_ffn_gate, m_w_ffn_up, m_w_ffn_down, m_ln2_g, m_ln2_b, v_w_ada, v_b_ada, v_w_in, v_b_gate, v_b_forget, v_w_sb_out, v_w_fox_out, v_w_o, v_ln1_g, v_ln1_b, v_w_ffn_gate, v_w_ffn_up, v_w_ffn_down, v_ln2_g, v_ln2_b):
    given = dict(locals())
    mx, my, mc = _mesh_pos()
    chip = 2 * mx + my
    seq = 4 * mx + 2 * my + mc

    c_all = _allgather_rows(c, "gather_c").reshape(8, D)
    n_ada = w_ada.shape[2]
    b_ada_shard = lax.dynamic_slice(b_ada, (0, chip * n_ada), (1, n_ada))
    ada_part = _ada_fwd(c_all, w_ada[0], b_ada_shard)
    ada_all = _allgather_rows(ada_part.reshape(1, 8 * n_ada), "gather_ada").reshape(4, 2, 8, n_ada)
    ada_row = lax.dynamic_slice(ada_all, (0, mc, seq, 0), (4, 1, 1, n_ada)).reshape(1, N_COND * D)
    ada = [ada_row[:, j * D:(j + 1) * D] for j in range(N_COND)]

    (w_in_g,) = _chip_exchange([_shard2d(w_in, "cols_t").astype(MXU)], "gather_w_in", gather=True)
    wi = _unshard(w_in_g, "cols_t")
    w_all = jnp.concatenate([wi[:OFF_FGATE + N_FGATE], jnp.zeros((LANE - N_FGATE, D), MXU), wi[OFF_FGATE + N_FGATE:]], axis=0)
    bf_pad = jnp.concatenate([b_forget, jnp.zeros((1, LANE - N_FGATE), F32)], axis=1)
    late = _BIG[1:]
    late_riders = [_shard2d(given[n], how).astype(MXU) for n, how in late]
    pieces = {}

    def late_full(gathered):
        return [_unshard(g, how) for (_, how), g in zip(late, gathered)]

    def early_grads(dw):
        for n, how in late:
            pieces[n] = _reshard(dw[n], how)
        return [pieces[n].astype(MXU) for n, _ in late]

    def w_in_grads(dwi):
        pieces["w_in"] = _reshard(jnp.concatenate(dwi[:6] + [dwi[6][:N_FGATE], dwi[7]], axis=0), "cols_t")
        return [pieces["w_in"].astype(MXU)]

    out = _local_step(x[0], loss_target[0], ada, w_all, b_gate, bf_pad, (late_riders, late_full), early_grads, w_in_grads,
                      ln1_g, ln1_b, ln2_g, ln2_b)

    row = jnp.concatenate(out["d_ada"] + [out["dln1_g"], out["dln1_b"], out["dln2_g"], out["dln2_b"], out["db_gate"],
                                          out["db_forget"], out["loss_lanes"]], axis=1)
    rows = _allgather_rows(row, "gather_small")
    small = _small_update(rows, [(given[p], given["m_" + p], given["v_" + p]) for p, _, _ in _SMALL_PARAMS])
    loss = small[0].reshape(())
    res = {}
    for j, (p, _, _) in enumerate(_SMALL_PARAMS):
        res[p] = small[1 + 4 * j:5 + 4 * j]

    dada_all = rows.reshape(8, _SMALL_LEN)[:, :N_COND * D]
    dada_shard = lax.dynamic_slice(dada_all, (0, chip * n_ada), (8, n_ada))
    g_ada = _ada_bwd(c_all, dada_shard)
    res["w_ada"] = [a[None] for a in _adamw(w_ada[0], m_w_ada[0], v_w_ada[0], [g_ada], "adamw_w_ada")]

    received = dict(zip([n for n, _ in late], out["early_received"]))
    (received["w_in"],) = out["w_in_received"]
    partial = [_reduce_chips(lax.dynamic_index_in_dim(pieces[n], chip, 0, keepdims=False), received[n], "reduce_" + n)
               for n, _ in _BIG]
    theirs = _sibling_exchange(partial, "swap_cores")
    for (n, how), mine, other in zip(_BIG, partial, theirs):
        upd = _adamw(_shard2d(given[n], how), _shard2d(given["m_" + n], how), _shard2d(given["v_" + n], how), [mine, other], "adamw_" + n)
        res[n] = [(a.T if how == "cols_t" else a)[None] for a in upd]

    order = ["w_ada", "b_ada", "w_in", "b_gate", "b_forget", "w_sb_out", "w_fox_out", "w_o", "ln1_g", "ln1_b",
             "w_ffn_gate", "w_ffn_up", "w_ffn_down", "ln2_g", "ln2_b"]
    return (loss, out["grad_x"][None], *[res[n][0] for n in order], *[res[n][1] for n in order],
            *[res[n][2] for n in order], *[res[n][3] for n in order])
```

```python
import functools

import jax
import jax.numpy as jnp
from jax import lax
from jax.experimental import pallas as pl
from jax.experimental.pallas import tpu as pltpu

F32 = jnp.float32
MXU = jnp.bfloat16

D = 1024
HEAD_DIM = 64
WIDTH = 512
D_FF = 2816
N_COND = 6
LN_EPS = 1e-5
ALPHA = 2.0 ** 0.25
QK_SCALE = HEAD_DIM ** -0.5
OFF_FGATE = 6 * WIDTH
N_FGATE = 8
IN_COLS = OFF_FGATE + N_FGATE + 2 * D
LANE = 128
W_ALL_COLS = OFF_FGATE + LANE + 2 * D
TQ = 512
ADAM_LR, ADAM_B1, ADAM_B2, ADAM_EPS, ADAM_WD, ADAM_STEP = 0.001, 0.9, 0.999, 1e-08, 0.01, 10
NEG = -1e30
MESH_AXES = ("x", "y", "c")
VMEM_BIG = 56 * 1024 * 1024


def _dot(a, b):
    return jnp.dot(a, b, preferred_element_type=F32)


def _dot_nt(a, b):
    return lax.dot_general(a, b, (((1,), (1,)), ((), ())), preferred_element_type=F32)


def _dot_tn(a, b):
    return lax.dot_general(a, b, (((0,), (0,)), ((), ())), preferred_element_type=F32)


def _ln(x):
    mu = jnp.mean(x, axis=-1, keepdims=True)
    xc = x - mu
    var = jnp.mean(xc * xc, axis=-1, keepdims=True)
    rstd = lax.rsqrt(var + LN_EPS)
    return xc * rstd, rstd


def _ln_bwd(dxhat, xhat, rstd):
    return rstd * (dxhat - jnp.mean(dxhat, axis=-1, keepdims=True) - xhat * jnp.mean(dxhat * xhat, axis=-1, keepdims=True))


def _sigmoid(x):
    return 1.0 / (1.0 + jnp.exp(-x))


def _colsum(x):
    return jnp.sum(x, axis=0, keepdims=True)


def _split(x):
    hi = x.astype(MXU)
    lo = (x - hi.astype(F32)).astype(MXU)
    return jnp.concatenate([hi, lo], axis=1)


def _rows(tm, n):
    return pl.BlockSpec((tm, n), lambda i: (i, 0))


def _fixed(r, n):
    return pl.BlockSpec((r, n), lambda i: (0, 0))


def _res(a):
    return pl.BlockSpec(a.shape, lambda i: (0, 0), pipeline_mode=pl.Buffered(1))


def _params(limit=None, sem=None):
    return pltpu.CompilerParams(vmem_limit_bytes=limit, dimension_semantics=sem)


def _in_proj(x, sh1, sc1, w_all, b_gate):
    s = x.shape[0]
    tm = 256

    def body(x_ref, sh_ref, sc_ref, w_ref, bg_ref, u_ref, qkv_ref, fl_ref, gl_ref):
        xhat, _ = _ln(x_ref[...])
        u = (xhat * (1.0 + sc_ref[...]) + sh_ref[...]).astype(MXU)
        u_ref[...] = u
        for c0 in range(0, OFF_FGATE, WIDTH):
            p = _dot_nt(u, w_ref[c0:c0 + WIDTH, :])
            if c0 in (0, 3 * WIDTH):
                p = p * QK_SCALE
            qkv_ref[:, c0:c0 + WIDTH] = p.astype(MXU)
        fl_ref[...] = _dot_nt(u, w_ref[OFF_FGATE:OFF_FGATE + LANE, :])
        for c0 in range(0, 2 * D, D):
            gl_ref[:, c0:c0 + D] = _dot_nt(u, w_ref[OFF_FGATE + LANE + c0:OFF_FGATE + LANE + c0 + D, :]) + bg_ref[:, c0:c0 + D]

    return pl.pallas_call(
        body, name="in_proj", grid=(s // tm,),
        in_specs=[_rows(tm, D), _fixed(1, D), _fixed(1, D), _res(w_all), _fixed(1, 2 * D)],
        out_specs=[_rows(tm, D), _rows(tm, OFF_FGATE), _rows(tm, LANE), _rows(tm, 2 * D)],
        out_shape=[jax.ShapeDtypeStruct((s, D), MXU), jax.ShapeDtypeStruct((s, OFF_FGATE), MXU),
                   jax.ShapeDtypeStruct((s, LANE), F32), jax.ShapeDtypeStruct((s, 2 * D), F32)],
        compiler_params=_params(VMEM_BIG),
    )(x, sh1, sc1, w_all, b_gate)


def _log_sigmoid_parts(z):
    e = jnp.exp(-jnp.abs(z))
    return -(jnp.maximum(z, 0.0) + jnp.log(1.0 + e)), e


def _fcum_fwd(fl, bf):
    s = fl.shape[0]
    nb = s // LANE

    def body(fl_ref, bf_ref, fc_ref, fkt_ref):
        r = lax.broadcasted_iota(jnp.int32, (LANE, LANE), 0)
        c = lax.broadcasted_iota(jnp.int32, (LANE, LANE), 1)
        tri = (c <= r).astype(F32)

        def step(b, carry):
            r0 = pl.multiple_of(b * LANE, LANE)
            xb = fl_ref[pl.ds(r0, LANE), :] + bf_ref[...]
            ls = _log_sigmoid_parts(-xb)[0]
            cs = jnp.dot(tri, ls, precision=lax.Precision.HIGHEST, preferred_element_type=F32) + carry
            fc_ref[pl.ds(r0, LANE), :] = cs
            fkt_ref[b] = cs.T[:N_FGATE, :]
            return cs[LANE - 1:LANE, :]

        lax.fori_loop(0, nb, step, jnp.zeros((1, LANE), F32))

    return pl.pallas_call(
        body, name="fcum_fwd",
        out_shape=[jax.ShapeDtypeStruct((s, LANE), F32), jax.ShapeDtypeStruct((nb, N_FGATE, LANE), F32)],
    )(fl, bf)


def _attn_specs(s, col0):
    return [pl.BlockSpec((TQ, LANE), lambda hp, i: (i, col0 + hp)),
            pl.BlockSpec((s, LANE), lambda hp, i: (0, col0 + 4 + hp)),
            pl.BlockSpec((s, LANE), lambda hp, i: (0, col0 + 8 + hp))]


def _tile_iotas():
    lane = lax.broadcasted_iota(jnp.int32, (TQ, LANE), 1)
    row = lax.broadcasted_iota(jnp.int32, (TQ, TQ), 0)
    col = lax.broadcasted_iota(jnp.int32, (TQ, TQ), 1)
    return lane, row, col


def _sub_blocks():
    return [slice(j * LANE, (j + 1) * LANE) for j in range(TQ // LANE)]


def _tri(below):
    r = lax.broadcasted_iota(jnp.int32, (LANE, LANE), 0)
    c = lax.broadcasted_iota(jnp.int32, (LANE, LANE), 1)
    t = jnp.concatenate([((r > c) if below else (r < c)).astype(MXU), jnp.ones((LANE, LANE), MXU)], axis=1)
    return jnp.concatenate([t, t], axis=0)


def _call_with_riders(body, name, grid, in_specs, out_specs, out_shape, scratch, args, riders, gather, limit=None):
    nr, n_in, n_out, n_sc = len(riders), len(in_specs), len(out_specs), len(scratch)

    def at_step(which):
        hit = None
        for d, n in enumerate(grid):
            here = pl.program_id(d) == (0 if which == "first" else n - 1)
            hit = here if hit is None else hit & here
        return hit

    def wrapped(*refs):
        ins, rin = refs[:n_in], refs[n_in:n_in + nr]
        outs, rout = refs[n_in + nr:n_in + nr + n_out], refs[n_in + nr + n_out:n_in + 2 * nr + n_out]
        own, sems = refs[n_in + 2 * nr + n_out:n_in + 2 * nr + n_out + n_sc], refs[n_in + 2 * nr + n_out + n_sc:]
        if nr:
            @pl.when(at_step("first"))
            def _():
                _exchange_start(rin, rout, sems, gather)

        body(*ins, *outs, *own)
        if nr:
            @pl.when(at_step("last"))
            def _():
                _exchange_wait(rin, rout, sems, gather)

    res = pl.pallas_call(
        wrapped, name=name, grid=grid,
        in_specs=list(in_specs) + [_ANY] * nr, out_specs=list(out_specs) + [_ANY] * nr,
        out_shape=list(out_shape) + _exchange_out_shape(riders),
        scratch_shapes=list(scratch) + (_exchange_sems(nr) if nr else []),
        compiler_params=_params(limit),
    )(*args, *riders)
    return res[:n_out], res[n_out:]


def _sb_fwd(qkv, riders=()):
    s = qkv.shape[0]
    nq = s // TQ
    assert nq <= LANE

    def body(q_ref, k_ref, v_ref, o_ref, rs_ref):
        i = pl.program_id(1)
        lane, row, col = _tile_iotas()
        u2 = _tri(True)
        diag = col < row
        q = q_ref[...]
        qms = [jnp.where(hm, q, jnp.zeros_like(q)) for hm in (lane < HEAD_DIM, lane >= HEAD_DIM)]

        def step(kb, carry, masked):
            k0 = pl.multiple_of(kb * TQ, TQ)
            k = k_ref[pl.ds(k0, TQ), :]
            v = v_ref[pl.ds(k0, TQ), :]
            new = []
            for qm, (run, acc, rt) in zip(qms, carry):
                z = _dot_nt(qm, k)
                lneg, _ = _log_sigmoid_parts(z)
                lpos = z + lneg
                if masked:
                    lneg = jnp.where(diag, lneg, 0.0)
                rt = jnp.where(lane == kb, run, rt)
                a = []
                for sl in reversed(_sub_blocks()):
                    st = _dot(_split(lneg[:, sl]), u2)
                    a.append(jnp.exp(lpos[:, sl] + st[:, :LANE] + run))
                    run = run + st[:, LANE:]
                a = jnp.concatenate(a[::-1], axis=1)
                if masked:
                    a = jnp.where(diag, a, 0.0)
                new.append((run, acc + _dot(a.astype(MXU), v), rt))
            return tuple(new)

        zero = jnp.zeros((TQ, LANE), F32)
        carry = step(i, ((zero, zero, zero),) * 2, True)
        carry = lax.fori_loop(0, i, lambda j, cr: step(i - 1 - j, cr, False), carry)
        rs_ref[0] = carry[0][2]
        rs_ref[1] = carry[1][2]
        o_ref[...] = jnp.where(lane < HEAD_DIM, carry[0][1], carry[1][1]).astype(o_ref.dtype)

    return _call_with_riders(
        body, "sb_fwd", (4, nq), _attn_specs(s, 0),
        [pl.BlockSpec((TQ, LANE), lambda hp, i: (i, hp)), pl.BlockSpec((2, TQ, LANE), lambda hp, i: (hp, i, 0))],
        [jax.ShapeDtypeStruct((s, WIDTH), MXU), jax.ShapeDtypeStruct((8, s, LANE), F32)], [], (qkv, qkv, qkv), riders, True)


def _sb_bwd(qkv, do, rs, riders=()):
    s = qkv.shape[0]
    nq = s // TQ

    def body(q_ref, k_ref, v_ref, do_ref, rs_ref, dq_ref, dk_ref, dv_ref, dk_acc, dv_acc):
        i = pl.program_id(1)

        @pl.when(i == 0)
        def _():
            dk_acc[...] = jnp.zeros_like(dk_acc)
            dv_acc[...] = jnp.zeros_like(dv_acc)

        lane, row, col = _tile_iotas()
        u2 = _tri(True)
        l2 = _tri(False)
        diag = col < row
        q = q_ref[...]
        do = do_ref[...]
        heads = [(jnp.where(hm, q, jnp.zeros_like(q)), jnp.where(hm, do, jnp.zeros_like(do)), rs_ref[hh])
                 for hh, hm in enumerate((lane < HEAD_DIM, lane >= HEAD_DIM))]

        def step(kb, carry, masked):
            k0 = pl.multiple_of(kb * TQ, TQ)
            k = k_ref[pl.ds(k0, TQ), :]
            v = v_ref[pl.ds(k0, TQ), :]
            new, dk, dv = [], 0.0, 0.0
            for (qm, dom, rblk), (gpre, dq) in zip(heads, carry):
                z = _dot_nt(qm, k)
                lneg, e = _log_sigmoid_parts(z)
                lpos = z + lneg
                if masked:
                    lneg = jnp.where(diag, lneg, 0.0)
                run = jnp.sum(jnp.where(lane == kb, rblk, 0.0), axis=1, keepdims=True) + jnp.zeros((TQ, LANE), F32)
                a = []
                for sl in reversed(_sub_blocks()):
                    st = _dot(_split(lneg[:, sl]), u2)
                    a.append(jnp.exp(lpos[:, sl] + st[:, :LANE] + run))
                    run = run + st[:, LANE:]
                a = jnp.concatenate(a[::-1], axis=1)
                if masked:
                    a = jnp.where(diag, a, 0.0)
                g = a * _dot_nt(dom, v)
                pre = []
                for sl in _sub_blocks():
                    pt = _dot(_split(g[:, sl]), l2)
                    pre.append(gpre + pt[:, :LANE])
                    gpre = gpre + pt[:, LANE:]
                sig = jnp.where(z >= 0.0, 1.0, e) / (1.0 + e)
                dz = g - (g + jnp.concatenate(pre, axis=1)) * sig
                if masked:
                    dz = jnp.where(diag, dz, 0.0)
                dzb = dz.astype(MXU)
                dk = dk + _dot_tn(dzb, qm)
                dv = dv + _dot_tn(a.astype(MXU), dom)
                new.append((gpre, dq + _dot(dzb, k)))
            dk_acc[pl.ds(k0, TQ), :] += dk
            dv_acc[pl.ds(k0, TQ), :] += dv
            return tuple(new)

        zero = jnp.zeros((TQ, LANE), F32)
        carry = step(i, lax.fori_loop(0, i, lambda kb, cr: step(kb, cr, False), ((zero, zero),) * 2), True)
        dq_ref[...] = (jnp.where(lane < HEAD_DIM, carry[0][1], carry[1][1]) * QK_SCALE).astype(dq_ref.dtype)

        @pl.when(i == nq - 1)
        def _():
            dk_ref[...] = dk_acc[...].astype(dk_ref.dtype)
            dv_ref[...] = dv_acc[...].astype(dv_ref.dtype)

    blk = pl.BlockSpec((TQ, LANE), lambda hp, i: (i, hp))
    whole = pl.BlockSpec((s, LANE), lambda hp, i: (0, hp))
    return _call_with_riders(
        body, "sb_bwd", (4, nq), _attn_specs(s, 0) + [blk, pl.BlockSpec((2, TQ, LANE), lambda hp, i: (hp, i, 0))],
        [blk, whole, whole], [jax.ShapeDtypeStruct((s, WIDTH), MXU)] * 3,
        [pltpu.VMEM((s, LANE), F32), pltpu.VMEM((s, LANE), F32)], (qkv, qkv, qkv, do, rs), riders, False)


def _key_bias(fkt_ref, kb, h):
    n_sub = TQ // LANE
    return jnp.concatenate([fkt_ref[kb * n_sub + j, pl.ds(h, 1), :] for j in range(n_sub)], axis=1)


def _fox_fwd(qkv, fc, fkt):
    s = qkv.shape[0]
    nq = s // TQ
    nb = fkt.shape[0]

    def body(q_ref, k_ref, v_ref, fq_ref, fkt_ref, o_ref, lse_ref):
        hp = pl.program_id(0)
        i = pl.program_id(1)
        lane, row, col = _tile_iotas()
        diag = col <= row
        q = q_ref[...]
        fqb = fq_ref[...]
        heads = []
        for hh in range(2):
            h = 2 * hp + hh
            hm = (lane >= HEAD_DIM) if hh else (lane < HEAD_DIM)
            heads.append((h, jnp.where(hm, q, jnp.zeros_like(q)), jnp.sum(jnp.where(lane == h, fqb, 0.0), axis=1, keepdims=True)))

        def step(kb, carry, masked):
            k0 = pl.multiple_of(kb * TQ, TQ)
            k = k_ref[pl.ds(k0, TQ), :]
            v = v_ref[pl.ds(k0, TQ), :]
            new = []
            for (h, qm, fq), (m, l, acc) in zip(heads, carry):
                z = _dot_nt(qm, k) + fq - _key_bias(fkt_ref, kb, h)
                if masked:
                    z = jnp.where(diag, z, NEG)
                mn = jnp.maximum(m, jnp.max(z, axis=1, keepdims=True))
                p = jnp.exp(z - mn)
                alpha = jnp.exp(m - mn)
                new.append((mn, alpha * l + jnp.sum(p, axis=1, keepdims=True), alpha * acc + _dot(p.astype(MXU), v)))
            return tuple(new)

        init = ((jnp.full((TQ, 1), NEG, F32), jnp.zeros((TQ, 1), F32), jnp.zeros((TQ, LANE), F32)),) * 2
        carry = step(i, lax.fori_loop(0, i, lambda kb, cr: step(kb, cr, False), init), True)
        outs = []
        for hh, (m, l, acc) in enumerate(carry):
            outs.append(acc / l)
            lse_ref[hh] = jnp.broadcast_to(m + jnp.log(l), (TQ, LANE))
        o_ref[...] = jnp.where(lane < HEAD_DIM, outs[0], outs[1]).astype(o_ref.dtype)

    return pl.pallas_call(
        body, name="fox_fwd", grid=(4, nq),
        in_specs=_attn_specs(s, 12) + [pl.BlockSpec((TQ, LANE), lambda hp, i: (i, 0)),
                                       pl.BlockSpec((nb, N_FGATE, LANE), lambda hp, i: (0, 0, 0))],
        out_specs=[pl.BlockSpec((TQ, LANE), lambda hp, i: (i, hp)), pl.BlockSpec((2, TQ, LANE), lambda hp, i: (hp, i, 0))],
        out_shape=[jax.ShapeDtypeStruct((s, WIDTH), MXU), jax.ShapeDtypeStruct((8, s, LANE), F32)],
    )(qkv, qkv, qkv, fc, fkt)


def _fox_bwd(qkv, fc, fkt, do, o, lse):
    s = qkv.shape[0]
    nq = s // TQ
    nb = fkt.shape[0]

    def body(q_ref, k_ref, v_ref, fq_ref, fkt_ref, do_ref, o_ref, lse_ref, dq_ref, dk_ref, dv_ref, dfk_ref, dfq_ref, dk_acc, dv_acc):
        hp = pl.program_id(0)
        i = pl.program_id(1)

        @pl.when(i == 0)
        def _():
            dk_acc[...] = jnp.zeros_like(dk_acc)
            dv_acc[...] = jnp.zeros_like(dv_acc)

        @pl.when((i == 0) & (hp == 0))
        def _():
            dfk_ref[...] = jnp.zeros_like(dfk_ref)

        lane, row, col = _tile_iotas()
        diag = col <= row
        q = q_ref[...]
        do = do_ref[...]
        dof = do.astype(F32) * o_ref[...].astype(F32)
        fqb = fq_ref[...]
        heads = []
        for hh in range(2):
            h = 2 * hp + hh
            hm = (lane >= HEAD_DIM) if hh else (lane < HEAD_DIM)
            heads.append((h, jnp.where(hm, q, jnp.zeros_like(q)), jnp.where(hm, do, jnp.zeros_like(do)),
                          jnp.sum(jnp.where(hm, dof, 0.0), axis=1, keepdims=True),
                          jnp.sum(jnp.where(lane == h, fqb, 0.0), axis=1, keepdims=True), lse_ref[hh][:, :1]))

        def step(kb, carry, masked):
            k0 = pl.multiple_of(kb * TQ, TQ)
            k = k_ref[pl.ds(k0, TQ), :]
            v = v_ref[pl.ds(k0, TQ), :]
            new, dk, dv = [], 0.0, 0.0
            for (h, qm, dom, delta, fq, lse_t), (dq, rsum) in zip(heads, carry):
                z = _dot_nt(qm, k) + fq - _key_bias(fkt_ref, kb, h)
                if masked:
                    z = jnp.where(diag, z, NEG)
                p = jnp.exp(z - lse_t)
                ds = p * (_dot_nt(dom, v) - delta)
                dsb = ds.astype(MXU)
                dk = dk + _dot_tn(dsb, qm)
                dv = dv + _dot_tn(p.astype(MXU), dom)
                csum = _colsum(ds)
                for j, sl in enumerate(_sub_blocks()):
                    dfk_ref[kb * len(_sub_blocks()) + j, pl.ds(h, 1), :] += -csum[:, sl]
                new.append((dq + _dot(dsb, k), rsum + jnp.sum(ds, axis=1, keepdims=True)))
            dk_acc[pl.ds(k0, TQ), :] += dk
            dv_acc[pl.ds(k0, TQ), :] += dv
            return tuple(new)

        init = ((jnp.zeros((TQ, LANE), F32), jnp.zeros((TQ, 1), F32)),) * 2
        carry = step(i, lax.fori_loop(0, i, lambda kb, cr: step(kb, cr, False), init), True)
        dq_ref[...] = (jnp.where(lane < HEAD_DIM, carry[0][0], carry[1][0]) * QK_SCALE).astype(dq_ref.dtype)
        dfq_ref[0] = jnp.where(lane == heads[0][0], carry[0][1], jnp.where(lane == heads[1][0], carry[1][1], 0.0))

        @pl.when(i == nq - 1)
        def _():
            dk_ref[...] = dk_acc[...].astype(dk_ref.dtype)
            dv_ref[...] = dv_acc[...].astype(dv_ref.dtype)

    blk = pl.BlockSpec((TQ, LANE), lambda hp, i: (i, hp))
    whole = pl.BlockSpec((s, LANE), lambda hp, i: (0, hp))
    pair = pl.BlockSpec((2, TQ, LANE), lambda hp, i: (hp, i, 0))
    fkt_spec = pl.BlockSpec((nb, N_FGATE, LANE), lambda hp, i: (0, 0, 0))
    return pl.pallas_call(
        body, name="fox_bwd", grid=(4, nq),
        in_specs=_attn_specs(s, 12) + [pl.BlockSpec((TQ, LANE), lambda hp, i: (i, 0)), fkt_spec, blk, blk, pair],
        out_specs=[blk, whole, whole, fkt_spec, pl.BlockSpec((1, TQ, LANE), lambda hp, i: (hp, i, 0))],
        out_shape=[jax.ShapeDtypeStruct((s, WIDTH), MXU)] * 3
        + [jax.ShapeDtypeStruct((nb, N_FGATE, LANE), F32), jax.ShapeDtypeStruct((4, s, LANE), F32)],
        scratch_shapes=[pltpu.VMEM((s, LANE), F32), pltpu.VMEM((s, LANE), F32)],
    )(qkv, qkv, qkv, fc, fkt, do, o, lse)


def _fcum_bwd(dfkt, dfq, fl, bf):
    s = fl.shape[0]
    nb = s // LANE

    def body(dfkt_ref, dfq_ref, fl_ref, bf_ref, df_ref, dbf_ref, tail_ref):
        @pl.when(pl.program_id(0) == 0)
        def _():
            tail_ref[...] = jnp.zeros_like(tail_ref)
            dbf_ref[...] = jnp.zeros_like(dbf_ref)

        r = lax.broadcasted_iota(jnp.int32, (LANE, LANE), 0)
        c = lax.broadcasted_iota(jnp.int32, (LANE, LANE), 1)
        tri = (c >= r).astype(F32)
        dfc = jnp.concatenate([dfkt_ref[0], jnp.zeros((LANE - N_FGATE, LANE), F32)], axis=0).T
        dfc = dfc + ((dfq_ref[0] + dfq_ref[1]) + (dfq_ref[2] + dfq_ref[3]))
        dls = jnp.dot(tri, dfc, precision=lax.Precision.HIGHEST, preferred_element_type=F32) + tail_ref[...]
        xb = fl_ref[...] + bf_ref[...]
        e = jnp.exp(-jnp.abs(xb))
        dfl = dls * (jnp.where(xb >= 0.0, e, 1.0) / (1.0 + e))
        df_ref[...] = dfl.astype(df_ref.dtype)
        tail_ref[...] = dls[0:1, :]
        dbf_ref[...] += _colsum(dfl)

    return pl.pallas_call(
        body, name="fcum_bwd", grid=(nb,),
        in_specs=[pl.BlockSpec((1, N_FGATE, LANE), lambda j: (nb - 1 - j, 0, 0)), pl.BlockSpec((4, LANE, LANE), lambda j: (0, nb - 1 - j, 0)),
                  pl.BlockSpec((LANE, LANE), lambda j: (nb - 1 - j, 0)), _fixed(1, LANE)],
        out_specs=[pl.BlockSpec((LANE, LANE), lambda j: (nb - 1 - j, 0)), _fixed(1, LANE)],
        out_shape=[jax.ShapeDtypeStruct((s, LANE), MXU), jax.ShapeDtypeStruct((1, LANE), F32)],
        scratch_shapes=[pltpu.VMEM((1, LANE), F32)],
    )(dfkt, dfq, fl, bf)


def _mix_fwd(x, o_sb, o_fx, gl, w_sb, w_fx, w_o, g1, ln1_g, ln1_b, sh2, sc2):
    s = x.shape[0]
    tm = 256

    def body(x_ref, osb_ref, ofx_ref, gl_ref, wsb_ref, wfx_ref, wo_ref, g1_ref, lg_ref, lb_ref, sh_ref, sc_ref, r1_ref, u2_ref):
        mixin = (_sigmoid(gl_ref[:, :D]) * _dot(osb_ref[...], wsb_ref[...])
                 + _sigmoid(gl_ref[:, D:]) * _dot(ofx_ref[...], wfx_ref[...]))
        r1 = ALPHA * x_ref[...] + g1_ref[...] * _dot(mixin.astype(MXU), wo_ref[...])
        r1_ref[...] = r1
        x1 = _ln(r1)[0] * lg_ref[...] + lb_ref[...]
        u2_ref[...] = (_ln(x1)[0] * (1.0 + sc_ref[...]) + sh_ref[...]).astype(MXU)

    vec = _fixed(1, D)
    return pl.pallas_call(
        body, name="mix_fwd", grid=(s // tm,),
        in_specs=[_rows(tm, D), _rows(tm, WIDTH), _rows(tm, WIDTH), _rows(tm, 2 * D), _res(w_sb), _res(w_fx), _res(w_o),
                  vec, vec, vec, vec, vec],
        out_specs=[_rows(tm, D), _rows(tm, D)],
        out_shape=[jax.ShapeDtypeStruct((s, D), F32), jax.ShapeDtypeStruct((s, D), MXU)],
        compiler_params=_params(VMEM_BIG),
    )(x, o_sb, o_fx, gl, w_sb, w_fx, w_o, g1, ln1_g, ln1_b, sh2, sc2)


def _ffn_fwd(r1, u2, tgt, w_g, w_u, w_d, g2, ln1_g, ln1_b, ln2_g, ln2_b):
    s = r1.shape[0]
    tm = 256

    def body(r1_ref, u2_ref, t_ref, wg_ref, wu_ref, wd_ref, g2_ref, l1g_ref, l1b_ref, l2g_ref, l2b_ref,
             hg_ref, hu_ref, dxa_ref, dh_ref, acc_ref):
        @pl.when(pl.program_id(0) == 0)
        def _():
            acc_ref[...] = jnp.zeros_like(acc_ref)

        u2 = u2_ref[...]
        hg = _dot_nt(u2, wg_ref[...])
        hu = _dot_nt(u2, wu_ref[...])
        hg_ref[...] = hg
        hu_ref[...] = hu
        h = _dot((hg * _sigmoid(hg) * hu).astype(MXU), wd_ref[...])
        x1 = _ln(r1_ref[...])[0] * l1g_ref[...] + l1b_ref[...]
        xh2, rstd2 = _ln(ALPHA * x1 + g2_ref[...] * h)
        err = xh2 * l2g_ref[...] + l2b_ref[...] - t_ref[...]
        dy = err * (1.0 / D)
        dr2 = _ln_bwd(dy * l2g_ref[...], xh2, rstd2)
        dxa_ref[...] = ALPHA * dr2
        dh_ref[...] = (g2_ref[...] * dr2).astype(MXU)
        acc_ref[0:1, :] += _colsum(dr2 * h)
        acc_ref[1:2, :] += _colsum(dy * xh2)
        acc_ref[2:3, :] += _colsum(dy)
        acc_ref[3:4, :] += _colsum(err * err) * (0.5 / D)

    vec = _fixed(1, D)
    return pl.pallas_call(
        body, name="ffn_fwd", grid=(s // tm,),
        in_specs=[_rows(tm, D), _rows(tm, D), _rows(tm, D), _res(w_g), _res(w_u), _res(w_d), vec, vec, vec, vec, vec],
        out_specs=[_rows(tm, D_FF), _rows(tm, D_FF), _rows(tm, D), _rows(tm, D), _fixed(8, D)],
        out_shape=[jax.ShapeDtypeStruct((s, D_FF), F32), jax.ShapeDtypeStruct((s, D_FF), F32),
                   jax.ShapeDtypeStruct((s, D), F32), jax.ShapeDtypeStruct((s, D), MXU), jax.ShapeDtypeStruct((8, D), F32)],
        compiler_params=_params(VMEM_BIG),
    )(r1, u2, tgt, w_g, w_u, w_d, g2, ln1_g, ln1_b, ln2_g, ln2_b)


def _ffn_bwd(dh, hg, hu, w_g, w_u, w_d):
    s = dh.shape[0]
    tm = 256
    half = D_FF // 2

    def body(dh_ref, hg_ref, hu_ref, wg_ref, wu_ref, wd_ref, act_ref, dhg_ref, dhu_ref, du2_ref):
        dh = dh_ref[...]
        du2 = jnp.zeros((tm, D), F32)
        for c0 in (0, half):
            cols = slice(c0, c0 + half)
            dact = _dot_nt(dh, wd_ref[cols, :])
            hg = hg_ref[:, cols]
            hu = hu_ref[:, cols]
            sg = _sigmoid(hg)
            sl = hg * sg
            act_ref[:, cols] = (sl * hu).astype(MXU)
            dhg = (dact * hu * (sg * (1.0 + hg * (1.0 - sg)))).astype(MXU)
            dhu = (dact * sl).astype(MXU)
            dhg_ref[:, cols] = dhg
            dhu_ref[:, cols] = dhu
            du2 = du2 + _dot(dhg, wg_ref[cols, :]) + _dot(dhu, wu_ref[cols, :])
        du2_ref[...] = du2

    return pl.pallas_call(
        body, name="ffn_bwd", grid=(s // tm,),
        in_specs=[_rows(tm, D), _rows(tm, D_FF), _rows(tm, D_FF), _res(w_g), _res(w_u), _res(w_d)],
        out_specs=[_rows(tm, D_FF), _rows(tm, D_FF), _rows(tm, D_FF), _rows(tm, D)],
        out_shape=[jax.ShapeDtypeStruct((s, D_FF), MXU)] * 3 + [jax.ShapeDtypeStruct((s, D), F32)],
        compiler_params=_params(VMEM_BIG),
    )(dh, hg, hu, w_g, w_u, w_d)


def _mix_bwd(du2, dxa, r1, o_sb, o_fx, gl, w_sb, w_fx, w_o, g1, ln1_g, ln1_b, sc2):
    s = r1.shape[0]
    tm = 256

    def body(du2_ref, dxa_ref, r1_ref, osb_ref, ofx_ref, gl_ref, wsb_ref, wfx_ref, wo_ref, g1_ref, lg_ref, lb_ref, sc_ref,
             dx_ref, mixin_ref, dmix_ref, dysb_ref, dyfx_ref, dosb_ref, dofx_ref, dgl_ref, dbg_ref, acc_ref):
        @pl.when(pl.program_id(0) == 0)
        def _():
            acc_ref[...] = jnp.zeros_like(acc_ref)
            dbg_ref[...] = jnp.zeros_like(dbg_ref)

        du2 = du2_ref[...]
        xh1, rstd1 = _ln(r1_ref[...])
        x1 = xh1 * lg_ref[...] + lb_ref[...]
        n1, rstdn = _ln(x1)
        dx1 = dxa_ref[...] + _ln_bwd(du2 * (1.0 + sc_ref[...]), n1, rstdn)
        dr1 = _ln_bwd(dx1 * lg_ref[...], xh1, rstd1)
        dx_ref[...] = ALPHA * dr1
        ysb = _dot(osb_ref[...], wsb_ref[...])
        yfx = _dot(ofx_ref[...], wfx_ref[...])
        gs = _sigmoid(gl_ref[:, :D])
        gf = _sigmoid(gl_ref[:, D:])
        mixin = (gs * ysb + gf * yfx).astype(MXU)
        mixin_ref[...] = mixin
        mix = _dot(mixin, wo_ref[...])
        dmix = (g1_ref[...] * dr1).astype(MXU)
        dmix_ref[...] = dmix
        dmixin = _dot_nt(dmix, wo_ref[...])
        dysb = (dmixin * gs).astype(MXU)
        dyfx = (dmixin * gf).astype(MXU)
        dysb_ref[...] = dysb
        dyfx_ref[...] = dyfx
        dosb_ref[...] = _dot_nt(dysb, wsb_ref[...]).astype(MXU)
        dofx_ref[...] = _dot_nt(dyfx, wfx_ref[...]).astype(MXU)
        dgs = dmixin * ysb * gs * (1.0 - gs)
        dgf = dmixin * yfx * gf * (1.0 - gf)
        dgl_ref[:, :D] = dgs.astype(MXU)
        dgl_ref[:, D:] = dgf.astype(MXU)
        dbg_ref[:, :D] += _colsum(dgs)
        dbg_ref[:, D:] += _colsum(dgf)
        acc_ref[0:1, :] += _colsum(du2)
        acc_ref[1:2, :] += _colsum(du2 * n1)
        acc_ref[2:3, :] += _colsum(dx1 * xh1)
        acc_ref[3:4, :] += _colsum(dx1)
        acc_ref[4:5, :] += _colsum(dr1 * mix)

    vec = _fixed(1, D)
    return pl.pallas_call(
        body, name="mix_bwd", grid=(s // tm,),
        in_specs=[_rows(tm, D), _rows(tm, D), _rows(tm, D), _rows(tm, WIDTH), _rows(tm, WIDTH), _rows(tm, 2 * D),
                  _res(w_sb), _res(w_fx), _res(w_o), vec, vec, vec, vec],
        out_specs=[_rows(tm, D), _rows(tm, D), _rows(tm, D), _rows(tm, D), _rows(tm, D), _rows(tm, WIDTH), _rows(tm, WIDTH),
                   _rows(tm, 2 * D), _fixed(1, 2 * D), _fixed(8, D)],
        out_shape=[jax.ShapeDtypeStruct((s, D), F32)] + [jax.ShapeDtypeStruct((s, D), MXU)] * 4
        + [jax.ShapeDtypeStruct((s, WIDTH), MXU)] * 2
        + [jax.ShapeDtypeStruct((s, 2 * D), MXU), jax.ShapeDtypeStruct((1, 2 * D), F32), jax.ShapeDtypeStruct((8, D), F32)],
        compiler_params=_params(VMEM_BIG),
    )(du2, dxa, r1, o_sb, o_fx, gl, w_sb, w_fx, w_o, g1, ln1_g, ln1_b, sc2)


def _in_bwd(pieces, x, dxa, w_all, sc1, riders=()):
    s = x.shape[0]
    tm = 256
    n_p = len(pieces)

    def body(*refs):
        p_refs = refs[:n_p]
        x_ref, dxa_ref, w_ref, sc_ref, gx_ref, acc_ref = refs[n_p:]

        @pl.when(pl.program_id(0) == 0)
        def _():
            acc_ref[...] = jnp.zeros_like(acc_ref)

        du1 = jnp.zeros((tm, D), F32)
        for p_ref, (arr, c0) in zip(p_refs, pieces):
            du1 = du1 + _dot(p_ref[...], w_ref[c0:c0 + arr.shape[1], :])
        n0, rstd0 = _ln(x_ref[...])
        gx_ref[...] = dxa_ref[...] + _ln_bwd(du1 * (1.0 + sc_ref[...]), n0, rstd0)
        acc_ref[0:1, :] += _colsum(du1)
        acc_ref[1:2, :] += _colsum(du1 * n0)

    return _call_with_riders(
        body, "in_bwd", (s // tm,),
        [_rows(tm, a.shape[1]) for a, _ in pieces] + [_rows(tm, D), _rows(tm, D), _res(w_all), _fixed(1, D)],
        [_rows(tm, D), _fixed(8, D)], [jax.ShapeDtypeStruct((s, D), F32), jax.ShapeDtypeStruct((8, D), F32)], [],
        (*[a for a, _ in pieces], x, dxa, w_all, sc1), riders, False, VMEM_BIG)


def _matmul_tn(a, b, name):
    s, m = a.shape
    n = b.shape[1]
    tm = 512 if m % 512 == 0 else (m if m < 512 else m // 2)
    tn = n // 2 if n > 2048 else n
    ts = 512
    assert m % tm == 0 and tm % LANE == 0 and n % tn == 0 and tn % LANE == 0 and s % ts == 0

    def body(a_ref, b_ref, o_ref):
        @pl.when(pl.program_id(2) == 0)
        def _():
            o_ref[...] = jnp.zeros_like(o_ref)

        o_ref[...] += _dot_tn(a_ref[...], b_ref[...])

    return pl.pallas_call(
        body, name=name, grid=(m // tm, n // tn, s // ts),
        in_specs=[pl.BlockSpec((ts, tm), lambda i, j, k: (k, i)), pl.BlockSpec((ts, tn), lambda i, j, k: (k, j))],
        out_specs=pl.BlockSpec((tm, tn), lambda i, j, k: (i, j)),
        out_shape=jax.ShapeDtypeStruct((m, n), F32),
        compiler_params=_params(VMEM_BIG),
    )(a, b)


def _local_step(x, tgt, ada, w_all, b_gate, bf_pad, late_weights, early_grads, w_in_grads, ln1_g, ln1_b, ln2_g, ln2_b):
    sh1, sc1, g1, sh2, sc2, g2 = ada
    u1, qkv, fl, gl = _in_proj(x, sh1, sc1, w_all, b_gate)
    fc, fkt = _fcum_fwd(fl, bf_pad)
    late_riders, late_full = late_weights
    (o_sb, rs), late_gathered = _sb_fwd(qkv, late_riders)
    w_sb, w_fx, w_o, w_g, w_u, w_d = late_full(late_gathered)
    o_fx, lse = _fox_fwd(qkv, fc, fkt)
    r1, u2 = _mix_fwd(x, o_sb, o_fx, gl, w_sb, w_fx, w_o, g1, ln1_g, ln1_b, sh2, sc2)
    hg, hu, dxa2, dh, acc_f = _ffn_fwd(r1, u2, tgt, w_g, w_u, w_d, g2, ln1_g, ln1_b, ln2_g, ln2_b)
    act, dhg, dhu, du2 = _ffn_bwd(dh, hg, hu, w_g, w_u, w_d)
    dxa1, mixin, dmix, dysb, dyfx, dosb, dofx, dgl, dbg, acc_m = _mix_bwd(
        du2, dxa2, r1, o_sb, o_fx, gl, w_sb, w_fx, w_o, g1, ln1_g, ln1_b, sc2)
    early = dict(w_sb_out=_matmul_tn(o_sb, dysb, "dw_sb_out"), w_fox_out=_matmul_tn(o_fx, dyfx, "dw_fox_out"),
                 w_o=_matmul_tn(mixin, dmix, "dw_o"), w_ffn_gate=_matmul_tn(dhg, u2, "dw_ffn_gate"),
                 w_ffn_up=_matmul_tn(dhu, u2, "dw_ffn_up"), w_ffn_down=_matmul_tn(act, dh, "dw_ffn_down"))
    (dq_sb, dk_sb, dv_sb), early_received = _sb_bwd(qkv, dosb, rs, early_grads(early))
    dq_fx, dk_fx, dv_fx, dfkt, dfq = _fox_bwd(qkv, fc, fkt, dofx, o_fx, lse)
    df, dbf = _fcum_bwd(dfkt, dfq, fl, bf_pad)
    pieces = [(dq_sb, 0), (dk_sb, WIDTH), (dv_sb, 2 * WIDTH), (dq_fx, 3 * WIDTH), (dk_fx, 4 * WIDTH), (dv_fx, 5 * WIDTH),
              (df, OFF_FGATE), (dgl, OFF_FGATE + LANE)]
    dw_in = [_matmul_tn(p, u1, f"dw_in_{j}") for j, (p, _) in enumerate(pieces)]
    (grad_x, acc_i), w_in_received = _in_bwd(pieces, x, dxa1, w_all, sc1, w_in_grads(dw_in))
    return dict(
        loss_lanes=acc_f[3:4], grad_x=grad_x, dw_in=dw_in, early=early, early_received=early_received,
        w_in_received=w_in_received,
        d_ada=[acc_i[0:1], acc_i[1:2], acc_m[4:5], acc_m[0:1], acc_m[1:2], acc_f[0:1]],
        dln1_g=acc_m[2:3], dln1_b=acc_m[3:4], dln2_g=acc_f[1:2], dln2_b=acc_f[2:3], db_gate=dbg, db_forget=dbf)


_MESH_ID = pl.DeviceIdType.MESH
_ANY = pl.BlockSpec(memory_space=pl.ANY)
_VMEM = pl.BlockSpec(memory_space=pltpu.VMEM)


def _mesh_pos():
    return lax.axis_index("x"), lax.axis_index("y"), lax.axis_index("c")


def _other_chips(x, y):
    return [(1 - x, y), (x, 1 - y), (1 - x, 1 - y)]


def _allgather_rows(v, name):
    n = v.shape[1]

    def body(v_ref, out_ref, send_sems, recv_sems, local_sem):
        x, y, c = _mesh_pos()
        me = 4 * x + 2 * y + c
        mine = pltpu.make_async_copy(v_ref, out_ref.at[me], local_sem)
        mine.start()
        copies = []
        for d in range(1, 8):
            fx, fy, fc = (d >> 2) & 1, (d >> 1) & 1, d & 1
            to = (1 - x if fx else x, 1 - y if fy else y, 1 - c if fc else c)
            cp = pltpu.make_async_remote_copy(src_ref=v_ref, dst_ref=out_ref.at[me], send_sem=send_sems.at[d - 1],
                                              recv_sem=recv_sems.at[d - 1], device_id=to, device_id_type=_MESH_ID)
            cp.start()
            copies.append(cp)
        for cp in copies:
            cp.wait_recv()
        for cp in copies:
            cp.wait_send()
        mine.wait()

    return pl.pallas_call(
        body, name=name, in_specs=[_VMEM], out_specs=_VMEM,
        out_shape=jax.ShapeDtypeStruct((8, 1, n), v.dtype),
        scratch_shapes=[pltpu.SemaphoreType.DMA((7,)), pltpu.SemaphoreType.DMA((7,)), pltpu.SemaphoreType.DMA(())],
    )(v)


def _chip_exchange(arrays, name, gather):
    nt = len(arrays)

    def body(*refs):
        ins, outs = refs[:nt], refs[nt:2 * nt]
        _exchange_start(ins, outs, refs[2 * nt:], gather)
        _exchange_wait(ins, outs, refs[2 * nt:], gather)

    return pl.pallas_call(
        body, name=name, in_specs=[_ANY] * nt, out_specs=[_ANY] * nt, out_shape=_exchange_out_shape(arrays),
        scratch_shapes=_exchange_sems(nt),
    )(*arrays)


def _exchange_out_shape(arrays):
    return [jax.ShapeDtypeStruct((4,) + a.shape[-2:], a.dtype) for a in arrays]


def _exchange_sems(nt):
    return [pltpu.SemaphoreType.DMA((3 * nt,)), pltpu.SemaphoreType.DMA((3 * nt,)), pltpu.SemaphoreType.DMA((nt,))]


def _exchange_copies(ins, outs, sems, gather):
    send_sems, recv_sems, local_sems = sems
    x, y, c = _mesh_pos()
    me = 2 * x + y
    local, remote = [], []
    for t in range(len(ins)):
        local.append(pltpu.make_async_copy(ins[t] if gather else ins[t].at[me], outs[t].at[me], local_sems.at[t]))
        for j, (px, py) in enumerate(_other_chips(x, y)):
            remote.append(pltpu.make_async_remote_copy(
                src_ref=ins[t] if gather else ins[t].at[2 * px + py], dst_ref=outs[t].at[me], send_sem=send_sems.at[3 * t + j],
                recv_sem=recv_sems.at[3 * t + j], device_id=(px, py, c), device_id_type=_MESH_ID))
    return local, remote


def _exchange_start(ins, outs, sems, gather):
    local, remote = _exchange_copies(ins, outs, sems, gather)
    for cp in local + remote:
        cp.start()


def _exchange_wait(ins, outs, sems, gather):
    local, remote = _exchange_copies(ins, outs, sems, gather)
    for cp in remote:
        cp.wait_recv()
    for cp in remote:
        cp.wait_send()
    for cp in local:
        cp.wait()


def _gather_two_level(shard, name):
    r, n = shard.shape
    half = n // 2
    assert half % LANE == 0

    def body(in_ref, out_ref, ici_send, ici_recv, d2d_send, d2d_recv, local_sem):
        x, y, c = _mesh_pos()
        me = 2 * x + y
        mine = pl.ds(pl.multiple_of(c * half, LANE), half)
        theirs = pl.ds(pl.multiple_of((1 - c) * half, LANE), half)
        local = pltpu.make_async_copy(in_ref, out_ref.at[me], local_sem)
        local.start()
        chips = _other_chips(x, y)
        over_ici = [pltpu.make_async_remote_copy(
            src_ref=in_ref.at[:, mine], dst_ref=out_ref.at[me, :, mine], send_sem=ici_send.at[j], recv_sem=ici_recv.at[j],
            device_id=(px, py, c), device_id_type=_MESH_ID) for j, (px, py) in enumerate(chips)]
        for cp in over_ici:
            cp.start()
        passed_on = [pltpu.make_async_remote_copy(
            src_ref=out_ref.at[2 * px + py, :, mine], dst_ref=out_ref.at[2 * px + py, :, mine], send_sem=d2d_send.at[j],
            recv_sem=d2d_recv.at[j], device_id=(x, y, 1 - c), device_id_type=_MESH_ID) for j, (px, py) in enumerate(chips)]
        for j, (px, py) in enumerate(chips):
            pltpu.make_async_remote_copy(
                src_ref=in_ref.at[:, mine], dst_ref=out_ref.at[2 * px + py, :, mine], send_sem=ici_send.at[j],
                recv_sem=ici_recv.at[j], device_id=(px, py, c), device_id_type=_MESH_ID).wait_recv()
            passed_on[j].start()
        for j, (px, py) in enumerate(chips):
            pltpu.make_async_remote_copy(
                src_ref=out_ref.at[2 * px + py, :, theirs], dst_ref=out_ref.at[2 * px + py, :, theirs], send_sem=d2d_send.at[j],
                recv_sem=d2d_recv.at[j], device_id=(x, y, 1 - c), device_id_type=_MESH_ID).wait_recv()
        for cp in over_ici + passed_on:
            cp.wait_send()
        local.wait()

    sems = pltpu.SemaphoreType.DMA((3,))
    return pl.pallas_call(
        body, name=name, in_specs=[_ANY], out_specs=_ANY, out_shape=jax.ShapeDtypeStruct((4, r, n), shard.dtype),
        scratch_shapes=[sems, sems, sems, sems, pltpu.SemaphoreType.DMA(())],
    )(shard)


def _sibling_exchange(arrays, name):
    nt = len(arrays)

    def body(*refs):
        ins, outs = refs[:nt], refs[nt:2 * nt]
        send_sems, recv_sems = refs[2 * nt:]
        x, y, c = _mesh_pos()
        copies = []
        for t in range(nt):
            cp = pltpu.make_async_remote_copy(src_ref=ins[t], dst_ref=outs[t], send_sem=send_sems.at[t], recv_sem=recv_sems.at[t],
                                              device_id=(x, y, 1 - c), device_id_type=_MESH_ID)
            cp.start()
            copies.append(cp)
        for cp in copies:
            cp.wait_recv()
        for cp in copies:
            cp.wait_send()

    return pl.pallas_call(
        body, name=name, in_specs=[_ANY] * nt, out_specs=[_ANY] * nt,
        out_shape=[jax.ShapeDtypeStruct(a.shape, a.dtype) for a in arrays],
        scratch_shapes=[pltpu.SemaphoreType.DMA((nt,)), pltpu.SemaphoreType.DMA((nt,))],
    )(*arrays)


def _tiles(r, n):
    for tr in (256, 352, 128):
        if r % tr == 0:
            return tr, n, r // tr, lambda i: (i, 0)
    assert n % 256 == 0
    return r, 256, n // 256, lambda i: (0, i)


def _reduce_chips(own, recv, name):
    r, n = own.shape
    tr, tn, steps, at = _tiles(r, n)

    def body(own_ref, recv_ref, out_ref):
        x, y, _ = _mesh_pos()
        me = 2 * x + y
        total = jnp.zeros((tr, tn), F32)
        for k in range(4):
            total = total + jnp.where(me == k, own_ref[...], recv_ref[k].astype(F32))
        out_ref[...] = total

    blk = pl.BlockSpec((tr, tn), at)
    return pl.pallas_call(
        body, name=name, grid=(steps,),
        in_specs=[blk, pl.BlockSpec((4, tr, tn), lambda i: (0,) + at(i))], out_specs=blk,
        out_shape=jax.ShapeDtypeStruct((r, n), F32),
    )(own, recv)


def _adamw_math(w, g, m, v):
    m = ADAM_B1 * m + (1.0 - ADAM_B1) * g
    v = ADAM_B2 * v + (1.0 - ADAM_B2) * (g * g)
    m_hat = m / (1.0 - ADAM_B1 ** ADAM_STEP)
    v_hat = v / (1.0 - ADAM_B2 ** ADAM_STEP)
    return -ADAM_LR * (m_hat / (jnp.sqrt(v_hat) + ADAM_EPS) + ADAM_WD * w), m, v


def _adamw(w, m, v, g_parts, name):
    r, n = w.shape
    tr, tn, steps, at = _tiles(r, n)
    blk = pl.BlockSpec((tr, tn), at)
    ng = len(g_parts)

    def body(*refs):
        w_ref, m_ref, v_ref = refs[:3]
        g_refs = refs[3:3 + ng]
        g_out, d_out, m_out, v_out = refs[3 + ng:]
        g = g_refs[0][...]
        for gr in g_refs[1:]:
            g = g + gr[...]
        g_out[...] = g
        d_out[...], m_out[...], v_out[...] = _adamw_math(w_ref[...], g, m_ref[...], v_ref[...])

    return pl.pallas_call(
        body, name=name, grid=(steps,),
        in_specs=[blk] * (3 + ng), out_specs=[blk] * 4,
        out_shape=[jax.ShapeDtypeStruct((r, n), F32)] * 4,
    )(w, m, v, *g_parts)


def _ada_fwd(c_all, w_shard, b_shard):
    n = w_shard.shape[1]
    tn = 512

    def body(c_ref, w_ref, b_ref, o_ref):
        cv = c_ref[...]
        ca = (cv * _sigmoid(cv)).astype(MXU)
        o_ref[...] = _dot(ca, w_ref[...].astype(MXU)) + b_ref[...]

    return pl.pallas_call(
        body, name="ada_fwd", grid=(n // tn,),
        in_specs=[_fixed(8, D), pl.BlockSpec((D, tn), lambda j: (0, j)), pl.BlockSpec((1, tn), lambda j: (0, j))],
        out_specs=pl.BlockSpec((8, tn), lambda j: (0, j)),
        out_shape=jax.ShapeDtypeStruct((8, n), F32),
    )(c_all, w_shard, b_shard)


def _ada_bwd(c_all, dada_shard):
    n = dada_shard.shape[1]
    tn = 512

    def body(c_ref, d_ref, o_ref):
        cv = c_ref[...]
        ca = (cv * _sigmoid(cv)).astype(MXU)
        o_ref[...] = _dot_tn(ca, d_ref[...].astype(MXU))

    return pl.pallas_call(
        body, name="ada_bwd", grid=(n // tn,),
        in_specs=[_fixed(8, D), pl.BlockSpec((8, tn), lambda j: (0, j))],
        out_specs=pl.BlockSpec((D, tn), lambda j: (0, j)),
        out_shape=jax.ShapeDtypeStruct((D, n), F32),
    )(c_all, dada_shard)


_SMALL = [("d_ada", N_COND * D), ("ln1_g", D), ("ln1_b", D), ("ln2_g", D), ("ln2_b", D), ("b_gate", 2 * D), ("b_forget", LANE),
          ("loss", D)]
_SMALL_OFF = {}
_o = 0
for _n, _w in _SMALL:
    _SMALL_OFF[_n] = (_o, _w)
    _o += _w
_SMALL_LEN = _o
_SMALL_PARAMS = [("b_ada", "d_ada", N_COND * D), ("b_gate", "b_gate", 2 * D), ("b_forget", "b_forget", N_FGATE),
                 ("ln1_g", "ln1_g", D), ("ln1_b", "ln1_b", D), ("ln2_g", "ln2_g", D), ("ln2_b", "ln2_b", D)]


def _small_update(rows, params):
    npar = len(_SMALL_PARAMS)

    def body(*refs):
        rows_ref = refs[0]
        p_refs = refs[1:1 + 3 * npar]
        loss_ref = refs[1 + 3 * npar]
        o_refs = refs[2 + 3 * npar:]
        total = rows_ref[0]
        for d in range(1, 8):
            total = total + rows_ref[d]
        lo, lw = _SMALL_OFF["loss"]
        loss_ref[...] = jnp.sum(total[:, lo:lo + lw], axis=1, keepdims=True)
        for j, (_, key, n) in enumerate(_SMALL_PARAMS):
            off = _SMALL_OFF[key][0]
            g = total[:, off:off + n]
            w_ref, m_ref, v_ref = p_refs[3 * j:3 * j + 3]
            o_refs[4 * j][...] = g
            o_refs[4 * j + 1][...], o_refs[4 * j + 2][...], o_refs[4 * j + 3][...] = _adamw_math(w_ref[...], g, m_ref[...], v_ref[...])

    flat = [a for p in params for a in p]
    out_shape = [jax.ShapeDtypeStruct((1, 1), F32)] + [jax.ShapeDtypeStruct((1, n), F32) for _, _, n in _SMALL_PARAMS for _ in range(4)]
    return pl.pallas_call(body, name="small_update", out_shape=out_shape)(rows, *flat)


_BIG = [("w_in", "cols_t"), ("w_sb_out", "cols"), ("w_fox_out", "cols"), ("w_o", "rows"),
        ("w_ffn_gate", "cols_t"), ("w_ffn_up", "cols_t"), ("w_ffn_down", "rows")]


def _shard2d(a, how):
    return a[0].T if how == "cols_t" else a[0]


def _unshard(g, how):
    if how == "cols":
        return g.transpose(1, 0, 2).reshape(g.shape[1], 4 * g.shape[2])
    return g.reshape(4 * g.shape[1], g.shape[2])


def _reshard(w, how):
    if how == "cols":
        return w.reshape(w.shape[0], 4, w.shape[1] // 4).transpose(1, 0, 2)
    return w.reshape(4, w.shape[0] // 4, w.shape[1])


def kernel(x, c, w_ada, b_ada, w_in, b_gate, b_forget, w_sb_out, w_fox_out, w_o, ln1_g, ln1_b, w_ffn_gate, w_ffn_up, w_ffn_down, ln2_g, ln2_b, loss_target, m_w_ada, m_b_ada, m_w_in, m_b_gate, m_b_forget, m_w_sb_out, m_w_fox_out, m_w_o, m_ln1_g, m_ln1_b, m_w_ffn_gate, m_w_ffn_up, m_w_ffn_down, m_ln2_g, m_ln2_b, v_w_ada, v_b_ada, v_w_in, v_b_gate, v_b_forget, v_w_sb_out, v_w_fox_out, v_w_o, v_ln1_g, v_ln1_b, v_w_ffn_gate, v_w_ffn_up, v_w_ffn_down, v_ln2_g, v_ln2_b):
    given = dict(locals())
    mx, my, mc = _mesh_pos()
    chip = 2 * mx + my
    seq = 4 * mx + 2 * my + mc

    c_all = _allgather_rows(c, "gather_c").reshape(8, D)
    n_ada = w_ada.shape[2]
    b_ada_shard = lax.dynamic_slice(b_ada, (0, chip * n_ada), (1, n_ada))
    ada_part = _ada_fwd(c_all, w_ada[0], b_ada_shard)
    ada_all = _allgather_rows(ada_part.reshape(1, 8 * n_ada), "gather_ada").reshape(4, 2, 8, n_ada)
    ada_row = lax.dynamic_slice(ada_all, (0, mc, seq, 0), (4, 1, 1, n_ada)).reshape(1, N_COND * D)
    ada = [ada_row[:, j * D:(j + 1) * D] for j in range(N_COND)]

    w_in_g = _gather_two_level(_shard2d(w_in, "cols_t").astype(MXU), "gather_w_in")
    wi = _unshard(w_in_g, "cols_t")
    w_all = jnp.concatenate([wi[:OFF_FGATE + N_FGATE], jnp.zeros((LANE - N_FGATE, D), MXU), wi[OFF_FGATE + N_FGATE:]], axis=0)
    bf_pad = jnp.concatenate([b_forget, jnp.zeros((1, LANE - N_FGATE), F32)], axis=1)
    late = _BIG[1:]
    late_riders = [_shard2d(given[n], how).astype(MXU) for n, how in late]
    pieces = {}

    def late_full(gathered):
        return [_unshard(g, how) for (_, how), g in zip(late, gathered)]

    def early_grads(dw):
        for n, how in late:
            pieces[n] = _reshard(dw[n], how)
        return [pieces[n].astype(MXU) for n, _ in late]

    def w_in_grads(dwi):
        pieces["w_in"] = _reshard(jnp.concatenate(dwi[:6] + [dwi[6][:N_FGATE], dwi[7]], axis=0), "cols_t")
        return [pieces["w_in"].astype(MXU)]

    out = _local_step(x[0], loss_target[0], ada, w_all, b_gate, bf_pad, (late_riders, late_full), early_grads, w_in_grads,
                      ln1_g, ln1_b, ln2_g, ln2_b)

    row = jnp.concatenate(out["d_ada"] + [out["dln1_g"], out["dln1_b"], out["dln2_g"], out["dln2_b"], out["db_gate"],
                                          out["db_forget"], out["loss_lanes"]], axis=1)
    rows = _allgather_rows(row, "gather_small")
    small = _small_update(rows, [(given[p], given["m_" + p], given["v_" + p]) for p, _, _ in _SMALL_PARAMS])
    loss = small[0].reshape(())
    res = {}
    for j, (p, _, _) in enumerate(_SMALL_PARAMS):
        res[p] = small[1 + 4 * j:5 + 4 * j]

    dada_all = rows.reshape(8, _SMALL_LEN)[:, :N_COND * D]
    dada_shard = lax.dynamic_slice(dada_all, (0, chip * n_ada), (8, n_ada))
    g_ada = _ada_bwd(c_all, dada_shard)
    res["w_ada"] = [a[None] for a in _adamw(w_ada[0], m_w_ada[0], v_w_ada[0], [g_ada], "adamw_w_ada")]

    received = dict(zip([n for n, _ in late], out["early_received"]))
    (received["w_in"],) = out["w_in_received"]
    partial = [_reduce_chips(lax.dynamic_index_in_dim(pieces[n], chip, 0, keepdims=False), received[n], "reduce_" + n)
               for n, _ in _BIG]
    theirs = _sibling_exchange(partial, "swap_cores")
    for (n, how), mine, other in zip(_BIG, partial, theirs):
        upd = _adamw(_shard2d(given[n], how), _shard2d(given["m_" + n], how), _shard2d(given["v_" + n], how), [mine, other], "adamw_" + n)
        res[n] = [(a.T if how == "cols_t" else a)[None] for a in upd]

    order = ["w_ada", "b_ada", "w_in", "b_gate", "b_forget", "w_sb_out", "w_fox_out", "w_o", "ln1_g", "ln1_b",
             "w_ffn_gate", "w_ffn_up", "w_ffn_down", "ln2_g", "ln2_b"]
    return (loss, out["grad_x"][None], *[res[n][0] for n in order], *[res[n][1] for n in order],
            *[res[n][2] for n in order], *[res[n][3] for n in order])
```

```python
import functools

import jax
import jax.numpy as jnp
from jax import lax
from jax.experimental import pallas as pl
from jax.experimental.pallas import tpu as pltpu

F32 = jnp.float32
MXU = jnp.bfloat16

D = 1024
HEAD_DIM = 64
WIDTH = 512
D_FF = 2816
N_COND = 6
LN_EPS = 1e-5
ALPHA = 2.0 ** 0.25
QK_SCALE = HEAD_DIM ** -0.5
OFF_FGATE = 6 * WIDTH
N_FGATE = 8
IN_COLS = OFF_FGATE + N_FGATE + 2 * D
LANE = 128
W_ALL_COLS = OFF_FGATE + LANE + 2 * D
TQ = 512
ADAM_LR, ADAM_B1, ADAM_B2, ADAM_EPS, ADAM_WD, ADAM_STEP = 0.001, 0.9, 0.999, 1e-08, 0.01, 10
NEG = -1e30
MESH_AXES = ("x", "y", "c")
VMEM_BIG = 56 * 1024 * 1024


def _dot(a, b):
    return jnp.dot(a, b, preferred_element_type=F32)


def _dot_nt(a, b):
    return lax.dot_general(a, b, (((1,), (1,)), ((), ())), preferred_element_type=F32)


def _dot_tn(a, b):
    return lax.dot_general(a, b, (((0,), (0,)), ((), ())), preferred_element_type=F32)


def _ln(x):
    mu = jnp.mean(x, axis=-1, keepdims=True)
    xc = x - mu
    var = jnp.mean(xc * xc, axis=-1, keepdims=True)
    rstd = lax.rsqrt(var + LN_EPS)
    return xc * rstd, rstd


def _ln_bwd(dxhat, xhat, rstd):
    return rstd * (dxhat - jnp.mean(dxhat, axis=-1, keepdims=True) - xhat * jnp.mean(dxhat * xhat, axis=-1, keepdims=True))


def _sigmoid(x):
    return 1.0 / (1.0 + jnp.exp(-x))


def _colsum(x):
    return jnp.sum(x, axis=0, keepdims=True)


def _split(x):
    hi = x.astype(MXU)
    lo = (x - hi.astype(F32)).astype(MXU)
    return jnp.concatenate([hi, lo], axis=1)


def _rows(tm, n):
    return pl.BlockSpec((tm, n), lambda i: (i, 0))


def _fixed(r, n):
    return pl.BlockSpec((r, n), lambda i: (0, 0))


def _res(a):
    return pl.BlockSpec(a.shape, lambda i: (0, 0), pipeline_mode=pl.Buffered(1))


def _params(limit=None, sem=None):
    return pltpu.CompilerParams(vmem_limit_bytes=limit, dimension_semantics=sem)


def _in_proj(x, sh1, sc1, w_all, b_gate):
    s = x.shape[0]
    tm = 256

    def body(x_ref, sh_ref, sc_ref, w_ref, bg_ref, u_ref, qkv_ref, fl_ref, gl_ref):
        xhat, _ = _ln(x_ref[...])
        u = (xhat * (1.0 + sc_ref[...]) + sh_ref[...]).astype(MXU)
        u_ref[...] = u
        for c0 in range(0, OFF_FGATE, WIDTH):
            p = _dot_nt(u, w_ref[c0:c0 + WIDTH, :])
            if c0 in (0, 3 * WIDTH):
                p = p * QK_SCALE
            qkv_ref[:, c0:c0 + WIDTH] = p.astype(MXU)
        fl_ref[...] = _dot_nt(u, w_ref[OFF_FGATE:OFF_FGATE + LANE, :])
        for c0 in range(0, 2 * D, D):
            gl_ref[:, c0:c0 + D] = _dot_nt(u, w_ref[OFF_FGATE + LANE + c0:OFF_FGATE + LANE + c0 + D, :]) + bg_ref[:, c0:c0 + D]

    return pl.pallas_call(
        body, name="in_proj", grid=(s // tm,),
        in_specs=[_rows(tm, D), _fixed(1, D), _fixed(1, D), _res(w_all), _fixed(1, 2 * D)],
        out_specs=[_rows(tm, D), _rows(tm, OFF_FGATE), _rows(tm, LANE), _rows(tm, 2 * D)],
        out_shape=[jax.ShapeDtypeStruct((s, D), MXU), jax.ShapeDtypeStruct((s, OFF_FGATE), MXU),
                   jax.ShapeDtypeStruct((s, LANE), F32), jax.ShapeDtypeStruct((s, 2 * D), F32)],
        compiler_params=_params(VMEM_BIG),
    )(x, sh1, sc1, w_all, b_gate)


def _log_sigmoid_parts(z):
    e = jnp.exp(-jnp.abs(z))
    return -(jnp.maximum(z, 0.0) + jnp.log(1.0 + e)), e


def _fcum_fwd(fl, bf):
    s = fl.shape[0]
    nb = s // LANE

    def body(fl_ref, bf_ref, fc_ref, fkt_ref):
        r = lax.broadcasted_iota(jnp.int32, (LANE, LANE), 0)
        c = lax.broadcasted_iota(jnp.int32, (LANE, LANE), 1)
        tri = (c <= r).astype(F32)

        def step(b, carry):
            r0 = pl.multiple_of(b * LANE, LANE)
            xb = fl_ref[pl.ds(r0, LANE), :] + bf_ref[...]
            ls = _log_sigmoid_parts(-xb)[0]
            cs = jnp.dot(tri, ls, precision=lax.Precision.HIGHEST, preferred_element_type=F32) + carry
            fc_ref[pl.ds(r0, LANE), :] = cs
            fkt_ref[b] = cs.T[:N_FGATE, :]
            return cs[LANE - 1:LANE, :]

        lax.fori_loop(0, nb, step, jnp.zeros((1, LANE), F32))

    return pl.pallas_call(
        body, name="fcum_fwd",
        out_shape=[jax.ShapeDtypeStruct((s, LANE), F32), jax.ShapeDtypeStruct((nb, N_FGATE, LANE), F32)],
    )(fl, bf)


def _attn_specs(s, col0):
    return [pl.BlockSpec((TQ, LANE), lambda hp, i: (i, col0 + hp)),
            pl.BlockSpec((s, LANE), lambda hp, i: (0, col0 + 4 + hp)),
            pl.BlockSpec((s, LANE), lambda hp, i: (0, col0 + 8 + hp))]


def _tile_iotas():
    lane = lax.broadcasted_iota(jnp.int32, (TQ, LANE), 1)
    row = lax.broadcasted_iota(jnp.int32, (TQ, TQ), 0)
    col = lax.broadcasted_iota(jnp.int32, (TQ, TQ), 1)
    return lane, row, col


def _sub_blocks(nk=None):
    return [slice(j * LANE, (j + 1) * LANE) for j in range((TQ if nk is None else nk) // LANE)]


def _over_strips(tile, diagonal):
    del diagonal
    return tile(slice(0, TQ), TQ)


def _tri(below):
    r = lax.broadcasted_iota(jnp.int32, (LANE, LANE), 0)
    c = lax.broadcasted_iota(jnp.int32, (LANE, LANE), 1)
    t = jnp.concatenate([((r > c) if below else (r < c)).astype(MXU), jnp.ones((LANE, LANE), MXU)], axis=1)
    return jnp.concatenate([t, t], axis=0)


def _call_with_riders(body, name, grid, in_specs, out_specs, out_shape, scratch, args, riders, gather, limit=None):
    nr, n_in, n_out, n_sc = len(riders), len(in_specs), len(out_specs), len(scratch)

    def at_step(which):
        hit = None
        for d, n in enumerate(grid):
            here = pl.program_id(d) == (0 if which == "first" else n - 1)
            hit = here if hit is None else hit & here
        return hit

    def wrapped(*refs):
        ins, rin = refs[:n_in], refs[n_in:n_in + nr]
        outs, rout = refs[n_in + nr:n_in + nr + n_out], refs[n_in + nr + n_out:n_in + 2 * nr + n_out]
        own, sems = refs[n_in + 2 * nr + n_out:n_in + 2 * nr + n_out + n_sc], refs[n_in + 2 * nr + n_out + n_sc:]
        if nr:
            @pl.when(at_step("first"))
            def _():
                _exchange_start(rin, rout, sems, gather)

        body(*ins, *outs, *own)
        if nr:
            @pl.when(at_step("last"))
            def _():
                _exchange_wait(rin, rout, sems, gather)

    res = pl.pallas_call(
        wrapped, name=name, grid=grid,
        in_specs=list(in_specs) + [_ANY] * nr, out_specs=list(out_specs) + [_ANY] * nr,
        out_shape=list(out_shape) + _exchange_out_shape(riders),
        scratch_shapes=list(scratch) + (_exchange_sems(nr) if nr else []),
        compiler_params=_params(limit),
    )(*args, *riders)
    return res[:n_out], res[n_out:]


def _sb_fwd(qkv, riders=()):
    s = qkv.shape[0]
    nq = s // TQ
    assert nq <= LANE

    def body(q_ref, k_ref, v_ref, o_ref, rs_ref):
        i = pl.program_id(1)
        lane, row, col = _tile_iotas()
        u2 = _tri(True)
        diag = col < row
        q = q_ref[...]
        qms = [jnp.where(hm, q, jnp.zeros_like(q)) for hm in (lane < HEAD_DIM, lane >= HEAD_DIM)]

        def step(kb, carry, masked):
            k0 = pl.multiple_of(kb * TQ, TQ)
            k = k_ref[pl.ds(k0, TQ), :]
            v = v_ref[pl.ds(k0, TQ), :]
            def tile(rows, nk, qm, state):
                run, acc, rt = (t[rows] for t in state)
                z = _dot_nt(qm[rows], k[:nk])
                lneg, _ = _log_sigmoid_parts(z)
                lpos = z + lneg
                if masked:
                    lneg = jnp.where(diag[rows, :nk], lneg, 0.0)
                rt = jnp.where(lane[rows] == kb, run, rt)
                a = []
                for sl in reversed(_sub_blocks(nk)):
                    st = _dot(_split(lneg[:, sl]), u2)
                    a.append(jnp.exp(lpos[:, sl] + st[:, :LANE] + run))
                    run = run + st[:, LANE:]
                a = jnp.concatenate(a[::-1], axis=1)
                if masked:
                    a = jnp.where(diag[rows, :nk], a, 0.0)
                return run, acc + _dot(a.astype(MXU), v[:nk]), rt

            return tuple(_over_strips(functools.partial(tile, qm=qm, state=state), masked) for qm, state in zip(qms, carry))

        zero = jnp.zeros((TQ, LANE), F32)
        carry = step(i, ((zero, zero, zero),) * 2, True)
        carry = lax.fori_loop(0, i, lambda j, cr: step(i - 1 - j, cr, False), carry)
        rs_ref[0] = carry[0][2]
        rs_ref[1] = carry[1][2]
        o_ref[...] = jnp.where(lane < HEAD_DIM, carry[0][1], carry[1][1]).astype(o_ref.dtype)

    return _call_with_riders(
        body, "sb_fwd", (4, nq), _attn_specs(s, 0),
        [pl.BlockSpec((TQ, LANE), lambda hp, i: (i, hp)), pl.BlockSpec((2, TQ, LANE), lambda hp, i: (hp, i, 0))],
        [jax.ShapeDtypeStruct((s, WIDTH), MXU), jax.ShapeDtypeStruct((8, s, LANE), F32)], [], (qkv, qkv, qkv), riders, True)


def _sb_bwd(qkv, do, rs, riders=()):
    s = qkv.shape[0]
    nq = s // TQ

    def body(q_ref, k_ref, v_ref, do_ref, rs_ref, dq_ref, dk_ref, dv_ref, dk_acc, dv_acc):
        i = pl.program_id(1)

        @pl.when(i == 0)
        def _():
            dk_acc[...] = jnp.zeros_like(dk_acc)
            dv_acc[...] = jnp.zeros_like(dv_acc)

        lane, row, col = _tile_iotas()
        u2 = _tri(True)
        l2 = _tri(False)
        diag = col < row
        q = q_ref[...]
        do = do_ref[...]
        heads = [(jnp.where(hm, q, jnp.zeros_like(q)), jnp.where(hm, do, jnp.zeros_like(do)), rs_ref[hh])
                 for hh, hm in enumerate((lane < HEAD_DIM, lane >= HEAD_DIM))]

        def step(kb, carry, masked):
            k0 = pl.multiple_of(kb * TQ, TQ)
            k = k_ref[pl.ds(k0, TQ), :]
            v = v_ref[pl.ds(k0, TQ), :]
            to_keys = {}

            def tile(rows, nk, qm, dom, rblk, state):
                gpre, dq = (t[rows] for t in state)
                z = _dot_nt(qm[rows], k[:nk])
                lneg, e = _log_sigmoid_parts(z)
                lpos = z + lneg
                if masked:
                    lneg = jnp.where(diag[rows, :nk], lneg, 0.0)
                run = jnp.sum(jnp.where(lane[rows] == kb, rblk[rows], 0.0), axis=1, keepdims=True) + jnp.zeros_like(gpre)
                a = []
                for sl in reversed(_sub_blocks(nk)):
                    st = _dot(_split(lneg[:, sl]), u2)
                    a.append(jnp.exp(lpos[:, sl] + st[:, :LANE] + run))
                    run = run + st[:, LANE:]
                a = jnp.concatenate(a[::-1], axis=1)
                if masked:
                    a = jnp.where(diag[rows, :nk], a, 0.0)
                g = a * _dot_nt(dom[rows], v[:nk])
                pre = []
                for sl in _sub_blocks(nk):
                    pt = _dot(_split(g[:, sl]), l2)
                    pre.append(gpre + pt[:, :LANE])
                    gpre = gpre + pt[:, LANE:]
                sig = jnp.where(z >= 0.0, 1.0, e) / (1.0 + e)
                dz = g - (g + jnp.concatenate(pre, axis=1)) * sig
                if masked:
                    dz = jnp.where(diag[rows, :nk], dz, 0.0)
                dzb = dz.astype(MXU)
                both = to_keys.setdefault(nk, [0.0, 0.0])
                both[0] = both[0] + _dot_tn(dzb, qm[rows])
                both[1] = both[1] + _dot_tn(a.astype(MXU), dom[rows])
                return gpre, dq + _dot(dzb, k[:nk])

            new = tuple(_over_strips(functools.partial(tile, qm=qm, dom=dom, rblk=rblk, state=state), masked)
                        for (qm, dom, rblk), state in zip(heads, carry))
            for nk, (dk, dv) in to_keys.items():
                dk_acc[pl.ds(k0, nk), :] += dk
                dv_acc[pl.ds(k0, nk), :] += dv
            return new

        zero = jnp.zeros((TQ, LANE), F32)
        carry = step(i, lax.fori_loop(0, i, lambda kb, cr: step(kb, cr, False), ((zero, zero),) * 2), True)
        dq_ref[...] = (jnp.where(lane < HEAD_DIM, carry[0][1], carry[1][1]) * QK_SCALE).astype(dq_ref.dtype)

        @pl.when(i == nq - 1)
        def _():
            dk_ref[...] = dk_acc[...].astype(dk_ref.dtype)
            dv_ref[...] = dv_acc[...].astype(dv_ref.dtype)

    blk = pl.BlockSpec((TQ, LANE), lambda hp, i: (i, hp))
    whole = pl.BlockSpec((s, LANE), lambda hp, i: (0, hp))
    return _call_with_riders(
        body, "sb_bwd", (4, nq), _attn_specs(s, 0) + [blk, pl.BlockSpec((2, TQ, LANE), lambda hp, i: (hp, i, 0))],
        [blk, whole, whole], [jax.ShapeDtypeStruct((s, WIDTH), MXU)] * 3,
        [pltpu.VMEM((s, LANE), F32), pltpu.VMEM((s, LANE), F32)], (qkv, qkv, qkv, do, rs), riders, False)


def _key_bias(fkt_ref, kb, h):
    n_sub = TQ // LANE
    return jnp.concatenate([fkt_ref[kb * n_sub + j, pl.ds(h, 1), :] for j in range(n_sub)], axis=1)


def _fox_fwd(qkv, fc, fkt):
    s = qkv.shape[0]
    nq = s // TQ
    nb = fkt.shape[0]

    def body(q_ref, k_ref, v_ref, fq_ref, fkt_ref, o_ref, lse_ref):
        hp = pl.program_id(0)
        i = pl.program_id(1)
        lane, row, col = _tile_iotas()
        diag = col <= row
        q = q_ref[...]
        fqb = fq_ref[...]
        heads = []
        for hh in range(2):
            h = 2 * hp + hh
            hm = (lane >= HEAD_DIM) if hh else (lane < HEAD_DIM)
            heads.append((h, jnp.where(hm, q, jnp.zeros_like(q)), jnp.sum(jnp.where(lane == h, fqb, 0.0), axis=1, keepdims=True)))

        def step(kb, carry, masked):
            k0 = pl.multiple_of(kb * TQ, TQ)
            k = k_ref[pl.ds(k0, TQ), :]
            v = v_ref[pl.ds(k0, TQ), :]
            def tile(rows, nk, h, qm, fq, state):
                m, l, acc = (t[rows] for t in state)
                z = _dot_nt(qm[rows], k[:nk]) + fq[rows] - _key_bias(fkt_ref, kb, h)[:, :nk]
                if masked:
                    z = jnp.where(diag[rows, :nk], z, NEG)
                mn = jnp.maximum(m, jnp.max(z, axis=1, keepdims=True))
                p = jnp.exp(z - mn)
                alpha = jnp.exp(m - mn)
                return mn, alpha * l + jnp.sum(p, axis=1, keepdims=True), alpha * acc + _dot(p.astype(MXU), v[:nk])

            return tuple(_over_strips(functools.partial(tile, h=h, qm=qm, fq=fq, state=state), masked)
                         for (h, qm, fq), state in zip(heads, carry))

        init = ((jnp.full((TQ, 1), NEG, F32), jnp.zeros((TQ, 1), F32), jnp.zeros((TQ, LANE), F32)),) * 2
        carry = step(i, lax.fori_loop(0, i, lambda kb, cr: step(kb, cr, False), init), True)
        outs = []
        for hh, (m, l, acc) in enumerate(carry):
            outs.append(acc / l)
            lse_ref[hh] = jnp.broadcast_to(m + jnp.log(l), (TQ, LANE))
        o_ref[...] = jnp.where(lane < HEAD_DIM, outs[0], outs[1]).astype(o_ref.dtype)

    return pl.pallas_call(
        body, name="fox_fwd", grid=(4, nq),
        in_specs=_attn_specs(s, 12) + [pl.BlockSpec((TQ, LANE), lambda hp, i: (i, 0)),
                                       pl.BlockSpec((nb, N_FGATE, LANE), lambda hp, i: (0, 0, 0))],
        out_specs=[pl.BlockSpec((TQ, LANE), lambda hp, i: (i, hp)), pl.BlockSpec((2, TQ, LANE), lambda hp, i: (hp, i, 0))],
        out_shape=[jax.ShapeDtypeStruct((s, WIDTH), MXU), jax.ShapeDtypeStruct((8, s, LANE), F32)],
    )(qkv, qkv, qkv, fc, fkt)


def _fox_bwd(qkv, fc, fkt, do, o, lse):
    s = qkv.shape[0]
    nq = s // TQ
    nb = fkt.shape[0]

    def body(q_ref, k_ref, v_ref, fq_ref, fkt_ref, do_ref, o_ref, lse_ref, dq_ref, dk_ref, dv_ref, dfk_ref, dfq_ref, dk_acc, dv_acc):
        hp = pl.program_id(0)
        i = pl.program_id(1)

        @pl.when(i == 0)
        def _():
            dk_acc[...] = jnp.zeros_like(dk_acc)
            dv_acc[...] = jnp.zeros_like(dv_acc)

        @pl.when((i == 0) & (hp == 0))
        def _():
            dfk_ref[...] = jnp.zeros_like(dfk_ref)

        lane, row, col = _tile_iotas()
        diag = col <= row
        q = q_ref[...]
        do = do_ref[...]
        dof = do.astype(F32) * o_ref[...].astype(F32)
        fqb = fq_ref[...]
        heads = []
        for hh in range(2):
            h = 2 * hp + hh
            hm = (lane >= HEAD_DIM) if hh else (lane < HEAD_DIM)
            heads.append((h, jnp.where(hm, q, jnp.zeros_like(q)), jnp.where(hm, do, jnp.zeros_like(do)),
                          jnp.sum(jnp.where(hm, dof, 0.0), axis=1, keepdims=True),
                          jnp.sum(jnp.where(lane == h, fqb, 0.0), axis=1, keepdims=True), lse_ref[hh][:, :1]))

        def step(kb, carry, masked):
            k0 = pl.multiple_of(kb * TQ, TQ)
            k = k_ref[pl.ds(k0, TQ), :]
            v = v_ref[pl.ds(k0, TQ), :]
            to_keys = {}

            def tile(rows, nk, h, qm, dom, delta, fq, lse_t, state):
                dq, rsum = (t[rows] for t in state)
                z = _dot_nt(qm[rows], k[:nk]) + fq[rows] - _key_bias(fkt_ref, kb, h)[:, :nk]
                if masked:
                    z = jnp.where(diag[rows, :nk], z, NEG)
                p = jnp.exp(z - lse_t[rows])
                ds = p * (_dot_nt(dom[rows], v[:nk]) - delta[rows])
                dsb = ds.astype(MXU)
                both = to_keys.setdefault(nk, [0.0, 0.0])
                both[0] = both[0] + _dot_tn(dsb, qm[rows])
                both[1] = both[1] + _dot_tn(p.astype(MXU), dom[rows])
                csum = _colsum(ds)
                for j, sl in enumerate(_sub_blocks(nk)):
                    dfk_ref[kb * len(_sub_blocks()) + j, pl.ds(h, 1), :] += -csum[:, sl]
                return dq + _dot(dsb, k[:nk]), rsum + jnp.sum(ds, axis=1, keepdims=True)

            new = tuple(_over_strips(functools.partial(tile, h=h, qm=qm, dom=dom, delta=delta, fq=fq, lse_t=lse_t, state=state), masked)
                        for (h, qm, dom, delta, fq, lse_t), state in zip(heads, carry))
            for nk, (dk, dv) in to_keys.items():
                dk_acc[pl.ds(k0, nk), :] += dk
                dv_acc[pl.ds(k0, nk), :] += dv
            return new

        init = ((jnp.zeros((TQ, LANE), F32), jnp.zeros((TQ, 1), F32)),) * 2
        carry = step(i, lax.fori_loop(0, i, lambda kb, cr: step(kb, cr, False), init), True)
        dq_ref[...] = (jnp.where(lane < HEAD_DIM, carry[0][0], carry[1][0]) * QK_SCALE).astype(dq_ref.dtype)
        dfq_ref[0] = jnp.where(lane == heads[0][0], carry[0][1], jnp.where(lane == heads[1][0], carry[1][1], 0.0))

        @pl.when(i == nq - 1)
        def _():
            dk_ref[...] = dk_acc[...].astype(dk_ref.dtype)
            dv_ref[...] = dv_acc[...].astype(dv_ref.dtype)

    blk = pl.BlockSpec((TQ, LANE), lambda hp, i: (i, hp))
    whole = pl.BlockSpec((s, LANE), lambda hp, i: (0, hp))
    pair = pl.BlockSpec((2, TQ, LANE), lambda hp, i: (hp, i, 0))
    fkt_spec = pl.BlockSpec((nb, N_FGATE, LANE), lambda hp, i: (0, 0, 0))
    return pl.pallas_call(
        body, name="fox_bwd", grid=(4, nq),
        in_specs=_attn_specs(s, 12) + [pl.BlockSpec((TQ, LANE), lambda hp, i: (i, 0)), fkt_spec, blk, blk, pair],
        out_specs=[blk, whole, whole, fkt_spec, pl.BlockSpec((1, TQ, LANE), lambda hp, i: (hp, i, 0))],
        out_shape=[jax.ShapeDtypeStruct((s, WIDTH), MXU)] * 3
        + [jax.ShapeDtypeStruct((nb, N_FGATE, LANE), F32), jax.ShapeDtypeStruct((4, s, LANE), F32)],
        scratch_shapes=[pltpu.VMEM((s, LANE), F32), pltpu.VMEM((s, LANE), F32)],
    )(qkv, qkv, qkv, fc, fkt, do, o, lse)


def _fcum_bwd(dfkt, dfq, fl, bf):
    s = fl.shape[0]
    nb = s // LANE

    def body(dfkt_ref, dfq_ref, fl_ref, bf_ref, df_ref, dbf_ref, tail_ref):
        @pl.when(pl.program_id(0) == 0)
        def _():
            tail_ref[...] = jnp.zeros_like(tail_ref)
            dbf_ref[...] = jnp.zeros_like(dbf_ref)

        r = lax.broadcasted_iota(jnp.int32, (LANE, LANE), 0)
        c = lax.broadcasted_iota(jnp.int32, (LANE, LANE), 1)
        tri = (c >= r).astype(F32)
        dfc = jnp.concatenate([dfkt_ref[0], jnp.zeros((LANE - N_FGATE, LANE), F32)], axis=0).T
        dfc = dfc + ((dfq_ref[0] + dfq_ref[1]) + (dfq_ref[2] + dfq_ref[3]))
        dls = jnp.dot(tri, dfc, precision=lax.Precision.HIGHEST, preferred_element_type=F32) + tail_ref[...]
        xb = fl_ref[...] + bf_ref[...]
        e = jnp.exp(-jnp.abs(xb))
        dfl = dls * (jnp.where(xb >= 0.0, e, 1.0) / (1.0 + e))
        df_ref[...] = dfl.astype(df_ref.dtype)
        tail_ref[...] = dls[0:1, :]
        dbf_ref[...] += _colsum(dfl)

    return pl.pallas_call(
        body, name="fcum_bwd", grid=(nb,),
        in_specs=[pl.BlockSpec((1, N_FGATE, LANE), lambda j: (nb - 1 - j, 0, 0)), pl.BlockSpec((4, LANE, LANE), lambda j: (0, nb - 1 - j, 0)),
                  pl.BlockSpec((LANE, LANE), lambda j: (nb - 1 - j, 0)), _fixed(1, LANE)],
        out_specs=[pl.BlockSpec((LANE, LANE), lambda j: (nb - 1 - j, 0)), _fixed(1, LANE)],
        out_shape=[jax.ShapeDtypeStruct((s, LANE), MXU), jax.ShapeDtypeStruct((1, LANE), F32)],
        scratch_shapes=[pltpu.VMEM((1, LANE), F32)],
    )(dfkt, dfq, fl, bf)


def _mix_fwd(x, o_sb, o_fx, gl, w_sb, w_fx, w_o, g1, ln1_g, ln1_b, sh2, sc2):
    s = x.shape[0]
    tm = 256

    def body(x_ref, osb_ref, ofx_ref, gl_ref, wsb_ref, wfx_ref, wo_ref, g1_ref, lg_ref, lb_ref, sh_ref, sc_ref, r1_ref, u2_ref):
        mixin = (_sigmoid(gl_ref[:, :D]) * _dot(osb_ref[...], wsb_ref[...])
                 + _sigmoid(gl_ref[:, D:]) * _dot(ofx_ref[...], wfx_ref[...]))
        r1 = ALPHA * x_ref[...] + g1_ref[...] * _dot(mixin.astype(MXU), wo_ref[...])
        r1_ref[...] = r1
        x1 = _ln(r1)[0] * lg_ref[...] + lb_ref[...]
        u2_ref[...] = (_ln(x1)[0] * (1.0 + sc_ref[...]) + sh_ref[...]).astype(MXU)

    vec = _fixed(1, D)
    return pl.pallas_call(
        body, name="mix_fwd", grid=(s // tm,),
        in_specs=[_rows(tm, D), _rows(tm, WIDTH), _rows(tm, WIDTH), _rows(tm, 2 * D), _res(w_sb), _res(w_fx), _res(w_o),
                  vec, vec, vec, vec, vec],
        out_specs=[_rows(tm, D), _rows(tm, D)],
        out_shape=[jax.ShapeDtypeStruct((s, D), F32), jax.ShapeDtypeStruct((s, D), MXU)],
        compiler_params=_params(VMEM_BIG),
    )(x, o_sb, o_fx, gl, w_sb, w_fx, w_o, g1, ln1_g, ln1_b, sh2, sc2)


def _ffn_fwd(r1, u2, tgt, w_g, w_u, w_d, g2, ln1_g, ln1_b, ln2_g, ln2_b):
    s = r1.shape[0]
    tm = 256

    def body(r1_ref, u2_ref, t_ref, wg_ref, wu_ref, wd_ref, g2_ref, l1g_ref, l1b_ref, l2g_ref, l2b_ref,
             hg_ref, hu_ref, dxa_ref, dh_ref, acc_ref):
        @pl.when(pl.program_id(0) == 0)
        def _():
            acc_ref[...] = jnp.zeros_like(acc_ref)

        u2 = u2_ref[...]
        hg = _dot_nt(u2, wg_ref[...])
        hu = _dot_nt(u2, wu_ref[...])
        hg_ref[...] = hg
        hu_ref[...] = hu
        h = _dot((hg * _sigmoid(hg) * hu).astype(MXU), wd_ref[...])
        x1 = _ln(r1_ref[...])[0] * l1g_ref[...] + l1b_ref[...]
        xh2, rstd2 = _ln(ALPHA * x1 + g2_ref[...] * h)
        err = xh2 * l2g_ref[...] + l2b_ref[...] - t_ref[...]
        dy = err * (1.0 / D)
        dr2 = _ln_bwd(dy * l2g_ref[...], xh2, rstd2)
        dxa_ref[...] = ALPHA * dr2
        dh_ref[...] = (g2_ref[...] * dr2).astype(MXU)
        acc_ref[0:1, :] += _colsum(dr2 * h)
        acc_ref[1:2, :] += _colsum(dy * xh2)
        acc_ref[2:3, :] += _colsum(dy)
        acc_ref[3:4, :] += _colsum(err * err) * (0.5 / D)

    vec = _fixed(1, D)
    return pl.pallas_call(
        body, name="ffn_fwd", grid=(s // tm,),
        in_specs=[_rows(tm, D), _rows(tm, D), _rows(tm, D), _res(w_g), _res(w_u), _res(w_d), vec, vec, vec, vec, vec],
        out_specs=[_rows(tm, D_FF), _rows(tm, D_FF), _rows(tm, D), _rows(tm, D), _fixed(8, D)],
        out_shape=[jax.ShapeDtypeStruct((s, D_FF), F32), jax.ShapeDtypeStruct((s, D_FF), F32),
                   jax.ShapeDtypeStruct((s, D), F32), jax.ShapeDtypeStruct((s, D), MXU), jax.ShapeDtypeStruct((8, D), F32)],
        compiler_params=_params(VMEM_BIG),
    )(r1, u2, tgt, w_g, w_u, w_d, g2, ln1_g, ln1_b, ln2_g, ln2_b)


def _ffn_bwd(dh, hg, hu, w_g, w_u, w_d):
    s = dh.shape[0]
    tm = 256
    half = D_FF // 2

    def body(dh_ref, hg_ref, hu_ref, wg_ref, wu_ref, wd_ref, act_ref, dhg_ref, dhu_ref, du2_ref):
        dh = dh_ref[...]
        du2 = jnp.zeros((tm, D), F32)
        for c0 in (0, half):
            cols = slice(c0, c0 + half)
            dact = _dot_nt(dh, wd_ref[cols, :])
            hg = hg_ref[:, cols]
            hu = hu_ref[:, cols]
            sg = _sigmoid(hg)
            sl = hg * sg
            act_ref[:, cols] = (sl * hu).astype(MXU)
            dhg = (dact * hu * (sg * (1.0 + hg * (1.0 - sg)))).astype(MXU)
            dhu = (dact * sl).astype(MXU)
            dhg_ref[:, cols] = dhg
            dhu_ref[:, cols] = dhu
            du2 = du2 + _dot(dhg, wg_ref[cols, :]) + _dot(dhu, wu_ref[cols, :])
        du2_ref[...] = du2

    return pl.pallas_call(
        body, name="ffn_bwd", grid=(s // tm,),
        in_specs=[_rows(tm, D), _rows(tm, D_FF), _rows(tm, D_FF), _res(w_g), _res(w_u), _res(w_d)],
        out_specs=[_rows(tm, D_FF), _rows(tm, D_FF), _rows(tm, D_FF), _rows(tm, D)],
        out_shape=[jax.ShapeDtypeStruct((s, D_FF), MXU)] * 3 + [jax.ShapeDtypeStruct((s, D), F32)],
        compiler_params=_params(VMEM_BIG),
    )(dh, hg, hu, w_g, w_u, w_d)


def _mix_bwd(du2, dxa, r1, o_sb, o_fx, gl, w_sb, w_fx, w_o, g1, ln1_g, ln1_b, sc2):
    s = r1.shape[0]
    tm = 256

    def body(du2_ref, dxa_ref, r1_ref, osb_ref, ofx_ref, gl_ref, wsb_ref, wfx_ref, wo_ref, g1_ref, lg_ref, lb_ref, sc_ref,
             dx_ref, mixin_ref, dmix_ref, dysb_ref, dyfx_ref, dosb_ref, dofx_ref, dgl_ref, dbg_ref, acc_ref):
        @pl.when(pl.program_id(0) == 0)
        def _():
            acc_ref[...] = jnp.zeros_like(acc_ref)
            dbg_ref[...] = jnp.zeros_like(dbg_ref)

        du2 = du2_ref[...]
        xh1, rstd1 = _ln(r1_ref[...])
        x1 = xh1 * lg_ref[...] + lb_ref[...]
        n1, rstdn = _ln(x1)
        dx1 = dxa_ref[...] + _ln_bwd(du2 * (1.0 + sc_ref[...]), n1, rstdn)
        dr1 = _ln_bwd(dx1 * lg_ref[...], xh1, rstd1)
        dx_ref[...] = ALPHA * dr1
        ysb = _dot(osb_ref[...], wsb_ref[...])
        yfx = _dot(ofx_ref[...], wfx_ref[...])
        gs = _sigmoid(gl_ref[:, :D])
        gf = _sigmoid(gl_ref[:, D:])
        mixin = (gs * ysb + gf * yfx).astype(MXU)
        mixin_ref[...] = mixin
        mix = _dot(mixin, wo_ref[...])
        dmix = (g1_ref[...] * dr1).astype(MXU)
        dmix_ref[...] = dmix
        dmixin = _dot_nt(dmix, wo_ref[...])
        dysb = (dmixin * gs).astype(MXU)
        dyfx = (dmixin * gf).astype(MXU)
        dysb_ref[...] = dysb
        dyfx_ref[...] = dyfx
        dosb_ref[...] = _dot_nt(dysb, wsb_ref[...]).astype(MXU)
        dofx_ref[...] = _dot_nt(dyfx, wfx_ref[...]).astype(MXU)
        dgs = dmixin * ysb * gs * (1.0 - gs)
        dgf = dmixin * yfx * gf * (1.0 - gf)
        dgl_ref[:, :D] = dgs.astype(MXU)
        dgl_ref[:, D:] = dgf.astype(MXU)
        dbg_ref[:, :D] += _colsum(dgs)
        dbg_ref[:, D:] += _colsum(dgf)
        acc_ref[0:1, :] += _colsum(du2)
        acc_ref[1:2, :] += _colsum(du2 * n1)
        acc_ref[2:3, :] += _colsum(dx1 * xh1)
        acc_ref[3:4, :] += _colsum(dx1)
        acc_ref[4:5, :] += _colsum(dr1 * mix)

    vec = _fixed(1, D)
    return pl.pallas_call(
        body, name="mix_bwd", grid=(s // tm,),
        in_specs=[_rows(tm, D), _rows(tm, D), _rows(tm, D), _rows(tm, WIDTH), _rows(tm, WIDTH), _rows(tm, 2 * D),
                  _res(w_sb), _res(w_fx), _res(w_o), vec, vec, vec, vec],
        out_specs=[_rows(tm, D), _rows(tm, D), _rows(tm, D), _rows(tm, D), _rows(tm, D), _rows(tm, WIDTH), _rows(tm, WIDTH),
                   _rows(tm, 2 * D), _fixed(1, 2 * D), _fixed(8, D)],
        out_shape=[jax.ShapeDtypeStruct((s, D), F32)] + [jax.ShapeDtypeStruct((s, D), MXU)] * 4
        + [jax.ShapeDtypeStruct((s, WIDTH), MXU)] * 2
        + [jax.ShapeDtypeStruct((s, 2 * D), MXU), jax.ShapeDtypeStruct((1, 2 * D), F32), jax.ShapeDtypeStruct((8, D), F32)],
        compiler_params=_params(VMEM_BIG),
    )(du2, dxa, r1, o_sb, o_fx, gl, w_sb, w_fx, w_o, g1, ln1_g, ln1_b, sc2)


def _in_bwd(pieces, x, dxa, w_all, sc1, riders=()):
    s = x.shape[0]
    tm = 256
    n_p = len(pieces)

    def body(*refs):
        p_refs = refs[:n_p]
        x_ref, dxa_ref, w_ref, sc_ref, gx_ref, acc_ref = refs[n_p:]

        @pl.when(pl.program_id(0) == 0)
        def _():
            acc_ref[...] = jnp.zeros_like(acc_ref)

        du1 = jnp.zeros((tm, D), F32)
        for p_ref, (arr, c0) in zip(p_refs, pieces):
            du1 = du1 + _dot(p_ref[...], w_ref[c0:c0 + arr.shape[1], :])
        n0, rstd0 = _ln(x_ref[...])
        gx_ref[...] = dxa_ref[...] + _ln_bwd(du1 * (1.0 + sc_ref[...]), n0, rstd0)
        acc_ref[0:1, :] += _colsum(du1)
        acc_ref[1:2, :] += _colsum(du1 * n0)

    return _call_with_riders(
        body, "in_bwd", (s // tm,),
        [_rows(tm, a.shape[1]) for a, _ in pieces] + [_rows(tm, D), _rows(tm, D), _res(w_all), _fixed(1, D)],
        [_rows(tm, D), _fixed(8, D)], [jax.ShapeDtypeStruct((s, D), F32), jax.ShapeDtypeStruct((8, D), F32)], [],
        (*[a for a, _ in pieces], x, dxa, w_all, sc1), riders, False, VMEM_BIG)


def _matmul_tn(a, b, name, narrow=False):
    s, m = a.shape
    n = b.shape[1]
    tm = 512 if m % 512 == 0 else (m if m < 512 else m // 2)
    tn = n // 2 if n > 2048 else n
    ts = 512
    assert m % tm == 0 and tm % LANE == 0 and n % tn == 0 and tn % LANE == 0 and s % ts == 0

    def body(a_ref, b_ref, o_ref, *narrow_ref):
        @pl.when(pl.program_id(2) == 0)
        def _():
            o_ref[...] = jnp.zeros_like(o_ref)

        o_ref[...] += _dot_tn(a_ref[...], b_ref[...])
        if narrow:
            @pl.when(pl.program_id(2) == s // ts - 1)
            def _():
                narrow_ref[0][...] = o_ref[...].astype(MXU)

    out_blk = pl.BlockSpec((tm, tn), lambda i, j, k: (i, j))
    res = pl.pallas_call(
        body, name=name, grid=(m // tm, n // tn, s // ts),
        in_specs=[pl.BlockSpec((ts, tm), lambda i, j, k: (k, i)), pl.BlockSpec((ts, tn), lambda i, j, k: (k, j))],
        out_specs=[out_blk] * (2 if narrow else 1),
        out_shape=[jax.ShapeDtypeStruct((m, n), F32)] + ([jax.ShapeDtypeStruct((m, n), MXU)] if narrow else []),
        compiler_params=_params(VMEM_BIG),
    )(a, b)
    return tuple(res) if narrow else res[0]


def _local_step(x, tgt, ada, w_all, b_gate, bf_pad, late_weights, early_grads, w_in_grads, ln1_g, ln1_b, ln2_g, ln2_b):
    sh1, sc1, g1, sh2, sc2, g2 = ada
    u1, qkv, fl, gl = _in_proj(x, sh1, sc1, w_all, b_gate)
    fc, fkt = _fcum_fwd(fl, bf_pad)
    late_riders, late_full = late_weights
    (o_sb, rs), late_gathered = _sb_fwd(qkv, late_riders)
    w_sb, w_fx, w_o, w_g, w_u, w_d = late_full(late_gathered)
    o_fx, lse = _fox_fwd(qkv, fc, fkt)
    r1, u2 = _mix_fwd(x, o_sb, o_fx, gl, w_sb, w_fx, w_o, g1, ln1_g, ln1_b, sh2, sc2)
    hg, hu, dxa2, dh, acc_f = _ffn_fwd(r1, u2, tgt, w_g, w_u, w_d, g2, ln1_g, ln1_b, ln2_g, ln2_b)
    act, dhg, dhu, du2 = _ffn_bwd(dh, hg, hu, w_g, w_u, w_d)
    dxa1, mixin, dmix, dysb, dyfx, dosb, dofx, dgl, dbg, acc_m = _mix_bwd(
        du2, dxa2, r1, o_sb, o_fx, gl, w_sb, w_fx, w_o, g1, ln1_g, ln1_b, sc2)
    early = dict(w_sb_out=_matmul_tn(o_sb, dysb, "dw_sb_out", True), w_fox_out=_matmul_tn(o_fx, dyfx, "dw_fox_out", True),
                 w_o=_matmul_tn(mixin, dmix, "dw_o", True), w_ffn_gate=_matmul_tn(dhg, u2, "dw_ffn_gate", True),
                 w_ffn_up=_matmul_tn(dhu, u2, "dw_ffn_up", True), w_ffn_down=_matmul_tn(act, dh, "dw_ffn_down", True))
    (dq_sb, dk_sb, dv_sb), early_received = _sb_bwd(qkv, dosb, rs, early_grads(early))
    dq_fx, dk_fx, dv_fx, dfkt, dfq = _fox_bwd(qkv, fc, fkt, dofx, o_fx, lse)
    df, dbf = _fcum_bwd(dfkt, dfq, fl, bf_pad)
    pieces = [(dq_sb, 0), (dk_sb, WIDTH), (dv_sb, 2 * WIDTH), (dq_fx, 3 * WIDTH), (dk_fx, 4 * WIDTH), (dv_fx, 5 * WIDTH),
              (df, OFF_FGATE), (dgl, OFF_FGATE + LANE)]
    dw_in = [_matmul_tn(p, u1, f"dw_in_{j}") for j, (p, _) in enumerate(pieces)]
    (grad_x, acc_i), w_in_received = _in_bwd(pieces, x, dxa1, w_all, sc1, w_in_grads(dw_in))
    return dict(
        loss_lanes=acc_f[3:4], grad_x=grad_x, dw_in=dw_in, early=early, early_received=early_received,
        w_in_received=w_in_received,
        d_ada=[acc_i[0:1], acc_i[1:2], acc_m[4:5], acc_m[0:1], acc_m[1:2], acc_f[0:1]],
        dln1_g=acc_m[2:3], dln1_b=acc_m[3:4], dln2_g=acc_f[1:2], dln2_b=acc_f[2:3], db_gate=dbg, db_forget=dbf)


_MESH_ID = pl.DeviceIdType.MESH
_ANY = pl.BlockSpec(memory_space=pl.ANY)
_VMEM = pl.BlockSpec(memory_space=pltpu.VMEM)


def _mesh_pos():
    return lax.axis_index("x"), lax.axis_index("y"), lax.axis_index("c")


def _other_chips(x, y):
    return [(1 - x, y), (x, 1 - y), (1 - x, 1 - y)]


def _allgather_rows(v, name):
    n = v.shape[1]

    def body(v_ref, out_ref, send_sems, recv_sems, local_sem):
        x, y, c = _mesh_pos()
        me = 4 * x + 2 * y + c
        mine = pltpu.make_async_copy(v_ref, out_ref.at[me], local_sem)
        mine.start()
        copies = []
        for d in range(1, 8):
            fx, fy, fc = (d >> 2) & 1, (d >> 1) & 1, d & 1
            to = (1 - x if fx else x, 1 - y if fy else y, 1 - c if fc else c)
            cp = pltpu.make_async_remote_copy(src_ref=v_ref, dst_ref=out_ref.at[me], send_sem=send_sems.at[d - 1],
                                              recv_sem=recv_sems.at[d - 1], device_id=to, device_id_type=_MESH_ID)
            cp.start()
            copies.append(cp)
        for cp in copies:
            cp.wait_recv()
        for cp in copies:
            cp.wait_send()
        mine.wait()

    return pl.pallas_call(
        body, name=name, in_specs=[_VMEM], out_specs=_VMEM,
        out_shape=jax.ShapeDtypeStruct((8, 1, n), v.dtype),
        scratch_shapes=[pltpu.SemaphoreType.DMA((7,)), pltpu.SemaphoreType.DMA((7,)), pltpu.SemaphoreType.DMA(())],
    )(v)


def _chip_exchange(arrays, name, gather):
    nt = len(arrays)

    def body(*refs):
        ins, outs = refs[:nt], refs[nt:2 * nt]
        _exchange_start(ins, outs, refs[2 * nt:], gather)
        _exchange_wait(ins, outs, refs[2 * nt:], gather)

    return pl.pallas_call(
        body, name=name, in_specs=[_ANY] * nt, out_specs=[_ANY] * nt, out_shape=_exchange_out_shape(arrays),
        scratch_shapes=_exchange_sems(nt),
    )(*arrays)


def _exchange_out_shape(arrays):
    return [jax.ShapeDtypeStruct((4,) + a.shape[-2:], a.dtype) for a in arrays]


def _exchange_sems(nt):
    return [pltpu.SemaphoreType.DMA((3 * nt,)), pltpu.SemaphoreType.DMA((3 * nt,)), pltpu.SemaphoreType.DMA((nt,))]


def _exchange_copies(ins, outs, sems, gather):
    send_sems, recv_sems, local_sems = sems
    x, y, c = _mesh_pos()
    me = 2 * x + y
    local, remote = [], []
    for t in range(len(ins)):
        local.append(pltpu.make_async_copy(ins[t] if gather else ins[t].at[me], outs[t].at[me], local_sems.at[t]))
        for j, (px, py) in enumerate(_other_chips(x, y)):
            remote.append(pltpu.make_async_remote_copy(
                src_ref=ins[t] if gather else ins[t].at[2 * px + py], dst_ref=outs[t].at[me], send_sem=send_sems.at[3 * t + j],
                recv_sem=recv_sems.at[3 * t + j], device_id=(px, py, c), device_id_type=_MESH_ID))
    return local, remote


def _exchange_start(ins, outs, sems, gather):
    local, remote = _exchange_copies(ins, outs, sems, gather)
    for cp in local + remote:
        cp.start()


def _exchange_wait(ins, outs, sems, gather):
    local, remote = _exchange_copies(ins, outs, sems, gather)
    for cp in remote:
        cp.wait_recv()
    for cp in remote:
        cp.wait_send()
    for cp in local:
        cp.wait()


def _gather_two_level(shard, name):
    r, n = shard.shape
    half = n // 2
    assert half % LANE == 0

    def body(in_ref, out_ref, ici_send, ici_recv, d2d_send, d2d_recv, local_sem):
        x, y, c = _mesh_pos()
        me = 2 * x + y
        mine = pl.ds(pl.multiple_of(c * half, LANE), half)
        theirs = pl.ds(pl.multiple_of((1 - c) * half, LANE), half)
        local = pltpu.make_async_copy(in_ref, out_ref.at[me], local_sem)
        local.start()
        chips = _other_chips(x, y)
        over_ici = [pltpu.make_async_remote_copy(
            src_ref=in_ref.at[:, mine], dst_ref=out_ref.at[me, :, mine], send_sem=ici_send.at[j], recv_sem=ici_recv.at[j],
            device_id=(px, py, c), device_id_type=_MESH_ID) for j, (px, py) in enumerate(chips)]
        for cp in over_ici:
            cp.start()
        passed_on = [pltpu.make_async_remote_copy(
            src_ref=out_ref.at[2 * px + py, :, mine], dst_ref=out_ref.at[2 * px + py, :, mine], send_sem=d2d_send.at[j],
            recv_sem=d2d_recv.at[j], device_id=(x, y, 1 - c), device_id_type=_MESH_ID) for j, (px, py) in enumerate(chips)]
        for j, (px, py) in enumerate(chips):
            pltpu.make_async_remote_copy(
                src_ref=in_ref.at[:, mine], dst_ref=out_ref.at[2 * px + py, :, mine], send_sem=ici_send.at[j],
                recv_sem=ici_recv.at[j], device_id=(px, py, c), device_id_type=_MESH_ID).wait_recv()
            passed_on[j].start()
        for j, (px, py) in enumerate(chips):
            pltpu.make_async_remote_copy(
                src_ref=out_ref.at[2 * px + py, :, theirs], dst_ref=out_ref.at[2 * px + py, :, theirs], send_sem=d2d_send.at[j],
                recv_sem=d2d_recv.at[j], device_id=(x, y, 1 - c), device_id_type=_MESH_ID).wait_recv()
        for cp in over_ici + passed_on:
            cp.wait_send()
        local.wait()

    sems = pltpu.SemaphoreType.DMA((3,))
    return pl.pallas_call(
        body, name=name, in_specs=[_ANY], out_specs=_ANY, out_shape=jax.ShapeDtypeStruct((4, r, n), shard.dtype),
        scratch_shapes=[sems, sems, sems, sems, pltpu.SemaphoreType.DMA(())],
    )(shard)


def _sibling_exchange(arrays, name):
    nt = len(arrays)

    def body(*refs):
        ins, outs = refs[:nt], refs[nt:2 * nt]
        send_sems, recv_sems = refs[2 * nt:]
        x, y, c = _mesh_pos()
        copies = []
        for t in range(nt):
            cp = pltpu.make_async_remote_copy(src_ref=ins[t], dst_ref=outs[t], send_sem=send_sems.at[t], recv_sem=recv_sems.at[t],
                                              device_id=(x, y, 1 - c), device_id_type=_MESH_ID)
            cp.start()
            copies.append(cp)
        for cp in copies:
            cp.wait_recv()
        for cp in copies:
            cp.wait_send()

    return pl.pallas_call(
        body, name=name, in_specs=[_ANY] * nt, out_specs=[_ANY] * nt,
        out_shape=[jax.ShapeDtypeStruct(a.shape, a.dtype) for a in arrays],
        scratch_shapes=[pltpu.SemaphoreType.DMA((nt,)), pltpu.SemaphoreType.DMA((nt,))],
    )(*arrays)


def _tiles(r, n):
    for tr in (256, 352, 128):
        if r % tr == 0:
            return tr, n, r // tr, lambda i: (i, 0)
    assert n % 256 == 0
    return r, 256, n // 256, lambda i: (0, i)


def _reduce_chips(chip, pieces, recv, name):
    _, r, n = pieces.shape
    tr, tn, steps, at = _tiles(r, n)

    def body(chip_ref, own_ref, recv_ref, out_ref):
        me = chip_ref[0]
        total = jnp.zeros((tr, tn), F32)
        for k in range(4):
            total = total + jnp.where(me == k, own_ref[0], recv_ref[k].astype(F32))
        out_ref[...] = total

    return pl.pallas_call(
        body, name=name,
        grid_spec=pltpu.PrefetchScalarGridSpec(
            num_scalar_prefetch=1, grid=(steps,),
            in_specs=[pl.BlockSpec((1, tr, tn), lambda i, chip_ref: (chip_ref[0],) + at(i)),
                      pl.BlockSpec((4, tr, tn), lambda i, chip_ref: (0,) + at(i))],
            out_specs=pl.BlockSpec((tr, tn), lambda i, chip_ref: at(i))),
        out_shape=jax.ShapeDtypeStruct((r, n), F32),
    )(chip, pieces, recv)


def _adamw_math(w, g, m, v):
    m = ADAM_B1 * m + (1.0 - ADAM_B1) * g
    v = ADAM_B2 * v + (1.0 - ADAM_B2) * (g * g)
    m_hat = m / (1.0 - ADAM_B1 ** ADAM_STEP)
    v_hat = v / (1.0 - ADAM_B2 ** ADAM_STEP)
    return -ADAM_LR * (m_hat / (jnp.sqrt(v_hat) + ADAM_EPS) + ADAM_WD * w), m, v


def _adamw(w, m, v, g_parts, name):
    r, n = w.shape
    tr, tn, steps, at = _tiles(r, n)
    blk = pl.BlockSpec((tr, tn), at)
    ng = len(g_parts)

    def body(*refs):
        w_ref, m_ref, v_ref = refs[:3]
        g_refs = refs[3:3 + ng]
        g_out, d_out, m_out, v_out = refs[3 + ng:]
        g = g_refs[0][...]
        for gr in g_refs[1:]:
            g = g + gr[...]
        g_out[...] = g
        d_out[...], m_out[...], v_out[...] = _adamw_math(w_ref[...], g, m_ref[...], v_ref[...])

    return pl.pallas_call(
        body, name=name, grid=(steps,),
        in_specs=[blk] * (3 + ng), out_specs=[blk] * 4,
        out_shape=[jax.ShapeDtypeStruct((r, n), F32)] * 4,
    )(w, m, v, *g_parts)


def _ada_fwd(c_all, w_shard, b_shard):
    n = w_shard.shape[1]
    tn = 512

    def body(c_ref, w_ref, b_ref, o_ref):
        cv = c_ref[...]
        ca = (cv * _sigmoid(cv)).astype(MXU)
        o_ref[...] = _dot(ca, w_ref[...].astype(MXU)) + b_ref[...]

    return pl.pallas_call(
        body, name="ada_fwd", grid=(n // tn,),
        in_specs=[_fixed(8, D), pl.BlockSpec((D, tn), lambda j: (0, j)), pl.BlockSpec((1, tn), lambda j: (0, j))],
        out_specs=pl.BlockSpec((8, tn), lambda j: (0, j)),
        out_shape=jax.ShapeDtypeStruct((8, n), F32),
    )(c_all, w_shard, b_shard)


def _ada_bwd(c_all, dada_shard):
    n = dada_shard.shape[1]
    tn = 512

    def body(c_ref, d_ref, o_ref):
        cv = c_ref[...]
        ca = (cv * _sigmoid(cv)).astype(MXU)
        o_ref[...] = _dot_tn(ca, d_ref[...].astype(MXU))

    return pl.pallas_call(
        body, name="ada_bwd", grid=(n // tn,),
        in_specs=[_fixed(8, D), pl.BlockSpec((8, tn), lambda j: (0, j))],
        out_specs=pl.BlockSpec((D, tn), lambda j: (0, j)),
        out_shape=jax.ShapeDtypeStruct((D, n), F32),
    )(c_all, dada_shard)


_SMALL = [("d_ada", N_COND * D), ("ln1_g", D), ("ln1_b", D), ("ln2_g", D), ("ln2_b", D), ("b_gate", 2 * D), ("b_forget", LANE),
          ("loss", D)]
_SMALL_OFF = {}
_o = 0
for _n, _w in _SMALL:
    _SMALL_OFF[_n] = (_o, _w)
    _o += _w
_SMALL_LEN = _o
_SMALL_PARAMS = [("b_ada", "d_ada", N_COND * D), ("b_gate", "b_gate", 2 * D), ("b_forget", "b_forget", N_FGATE),
                 ("ln1_g", "ln1_g", D), ("ln1_b", "ln1_b", D), ("ln2_g", "ln2_g", D), ("ln2_b", "ln2_b", D)]


def _small_update(rows, params):
    npar = len(_SMALL_PARAMS)

    def body(*refs):
        rows_ref = refs[0]
        p_refs = refs[1:1 + 3 * npar]
        loss_ref = refs[1 + 3 * npar]
        o_refs = refs[2 + 3 * npar:]
        total = rows_ref[0]
        for d in range(1, 8):
            total = total + rows_ref[d]
        lo, lw = _SMALL_OFF["loss"]
        loss_ref[...] = jnp.sum(total[:, lo:lo + lw], axis=1, keepdims=True)
        for j, (_, key, n) in enumerate(_SMALL_PARAMS):
            off = _SMALL_OFF[key][0]
            g = total[:, off:off + n]
            w_ref, m_ref, v_ref = p_refs[3 * j:3 * j + 3]
            o_refs[4 * j][...] = g
            o_refs[4 * j + 1][...], o_refs[4 * j + 2][...], o_refs[4 * j + 3][...] = _adamw_math(w_ref[...], g, m_ref[...], v_ref[...])

    flat = [a for p in params for a in p]
    out_shape = [jax.ShapeDtypeStruct((1, 1), F32)] + [jax.ShapeDtypeStruct((1, n), F32) for _, _, n in _SMALL_PARAMS for _ in range(4)]
    return pl.pallas_call(body, name="small_update", out_shape=out_shape)(rows, *flat)


_BIG = [("w_in", "cols_t"), ("w_sb_out", "cols"), ("w_fox_out", "cols"), ("w_o", "rows"),
        ("w_ffn_gate", "cols_t"), ("w_ffn_up", "cols_t"), ("w_ffn_down", "rows")]


def _shard2d(a, how):
    return a[0].T if how == "cols_t" else a[0]


def _unshard(g, how):
    if how == "cols":
        return g.transpose(1, 0, 2).reshape(g.shape[1], 4 * g.shape[2])
    return g.reshape(4 * g.shape[1], g.shape[2])


def _reshard(w, how):
    if how == "cols":
        return w.reshape(w.shape[0], 4, w.shape[1] // 4).transpose(1, 0, 2)
    return w.reshape(4, w.shape[0] // 4, w.shape[1])


def kernel(x, c, w_ada, b_ada, w_in, b_gate, b_forget, w_sb_out, w_fox_out, w_o, ln1_g, ln1_b, w_ffn_gate, w_ffn_up, w_ffn_down, ln2_g, ln2_b, loss_target, m_w_ada, m_b_ada, m_w_in, m_b_gate, m_b_forget, m_w_sb_out, m_w_fox_out, m_w_o, m_ln1_g, m_ln1_b, m_w_ffn_gate, m_w_ffn_up, m_w_ffn_down, m_ln2_g, m_ln2_b, v_w_ada, v_b_ada, v_w_in, v_b_gate, v_b_forget, v_w_sb_out, v_w_fox_out, v_w_o, v_ln1_g, v_ln1_b, v_w_ffn_gate, v_w_ffn_up, v_w_ffn_down, v_ln2_g, v_ln2_b):
    given = dict(locals())
    mx, my, mc = _mesh_pos()
    chip = 2 * mx + my
    seq = 4 * mx + 2 * my + mc

    c_all = _allgather_rows(c, "gather_c").reshape(8, D)
    n_ada = w_ada.shape[2]
    b_ada_shard = lax.dynamic_slice(b_ada, (0, chip * n_ada), (1, n_ada))
    ada_part = _ada_fwd(c_all, w_ada[0], b_ada_shard)
    ada_all = _allgather_rows(ada_part.reshape(1, 8 * n_ada), "gather_ada").reshape(4, 2, 8, n_ada)
    ada_row = lax.dynamic_slice(ada_all, (0, mc, seq, 0), (4, 1, 1, n_ada)).reshape(1, N_COND * D)
    ada = [ada_row[:, j * D:(j + 1) * D] for j in range(N_COND)]

    w_in_g = _gather_two_level(_shard2d(w_in, "cols_t").astype(MXU), "gather_w_in")
    wi = _unshard(w_in_g, "cols_t")
    w_all = jnp.concatenate([wi[:OFF_FGATE + N_FGATE], jnp.zeros((LANE - N_FGATE, D), MXU), wi[OFF_FGATE + N_FGATE:]], axis=0)
    bf_pad = jnp.concatenate([b_forget, jnp.zeros((1, LANE - N_FGATE), F32)], axis=1)
    late = _BIG[1:]
    late_riders = [_shard2d(given[n], how).astype(MXU) for n, how in late]
    pieces = {}

    def late_full(gathered):
        return [_unshard(g, how) for (_, how), g in zip(late, gathered)]

    def early_grads(dw):
        for n, how in late:
            pieces[n] = _reshard(dw[n][0], how)
        return [_reshard(dw[n][1], how) for n, how in late]

    def w_in_grads(dwi):
        pieces["w_in"] = _reshard(jnp.concatenate(dwi[:6] + [dwi[6][:N_FGATE], dwi[7]], axis=0), "cols_t")
        return [pieces["w_in"].astype(MXU)]

    out = _local_step(x[0], loss_target[0], ada, w_all, b_gate, bf_pad, (late_riders, late_full), early_grads, w_in_grads,
                      ln1_g, ln1_b, ln2_g, ln2_b)

    row = jnp.concatenate(out["d_ada"] + [out["dln1_g"], out["dln1_b"], out["dln2_g"], out["dln2_b"], out["db_gate"],
                                          out["db_forget"], out["loss_lanes"]], axis=1)
    rows = _allgather_rows(row, "gather_small")
    small = _small_update(rows, [(given[p], given["m_" + p], given["v_" + p]) for p, _, _ in _SMALL_PARAMS])
    loss = small[0].reshape(())
    res = {}
    for j, (p, _, _) in enumerate(_SMALL_PARAMS):
        res[p] = small[1 + 4 * j:5 + 4 * j]

    dada_all = rows.reshape(8, _SMALL_LEN)[:, :N_COND * D]
    dada_shard = lax.dynamic_slice(dada_all, (0, chip * n_ada), (8, n_ada))
    g_ada = _ada_bwd(c_all, dada_shard)
    res["w_ada"] = [a[None] for a in _adamw(w_ada[0], m_w_ada[0], v_w_ada[0], [g_ada], "adamw_w_ada")]

    received = dict(zip([n for n, _ in late], out["early_received"]))
    (received["w_in"],) = out["w_in_received"]
    chip_arr = jnp.reshape(chip, (1,)).astype(jnp.int32)
    partial = [_reduce_chips(chip_arr, pieces[n], received[n], "reduce_" + n) for n, _ in _BIG]
    theirs = _sibling_exchange(partial, "swap_cores")
    for (n, how), mine, other in zip(_BIG, partial, theirs):
        upd = _adamw(_shard2d(given[n], how), _shard2d(given["m_" + n], how), _shard2d(given["v_" + n], how), [mine, other], "adamw_" + n)
        res[n] = [(a.T if how == "cols_t" else a)[None] for a in upd]

    order = ["w_ada", "b_ada", "w_in", "b_gate", "b_forget", "w_sb_out", "w_fox_out", "w_o", "ln1_g", "ln1_b",
             "w_ffn_gate", "w_ffn_up", "w_ffn_down", "ln2_g", "ln2_b"]
    return (loss, out["grad_x"][None], *[res[n][0] for n in order], *[res[n][1] for n in order],
            *[res[n][2] for n in order], *[res[n][3] for n in order])
```

```python
import functools

import jax
import jax.numpy as jnp
from jax import lax
from jax.experimental import pallas as pl
from jax.experimental.pallas import tpu as pltpu

F32 = jnp.float32
MXU = jnp.bfloat16

D = 1024
HEAD_DIM = 64
WIDTH = 512
D_FF = 2816
N_COND = 6
LN_EPS = 1e-5
ALPHA = 2.0 ** 0.25
QK_SCALE = HEAD_DIM ** -0.5
OFF_FGATE = 6 * WIDTH
N_FGATE = 8
IN_COLS = OFF_FGATE + N_FGATE + 2 * D
LANE = 128
W_ALL_COLS = OFF_FGATE + LANE + 2 * D
TQ = 512
ADAM_LR, ADAM_B1, ADAM_B2, ADAM_EPS, ADAM_WD, ADAM_STEP = 0.001, 0.9, 0.999, 1e-08, 0.01, 10
NEG = -1e30
DEAD_LOG = -120.0
RS_COUNT_LANE = LANE - 1
MESH_AXES = ("x", "y", "c")
VMEM_BIG = 56 * 1024 * 1024


def _dot(a, b):
    return jnp.dot(a, b, preferred_element_type=F32)


def _dot_nt(a, b):
    return lax.dot_general(a, b, (((1,), (1,)), ((), ())), preferred_element_type=F32)


def _dot_tn(a, b):
    return lax.dot_general(a, b, (((0,), (0,)), ((), ())), preferred_element_type=F32)


def _ln(x):
    mu = jnp.mean(x, axis=-1, keepdims=True)
    xc = x - mu
    var = jnp.mean(xc * xc, axis=-1, keepdims=True)
    rstd = lax.rsqrt(var + LN_EPS)
    return xc * rstd, rstd


def _ln_bwd(dxhat, xhat, rstd):
    return rstd * (dxhat - jnp.mean(dxhat, axis=-1, keepdims=True) - xhat * jnp.mean(dxhat * xhat, axis=-1, keepdims=True))


def _sigmoid(x):
    return 1.0 / (1.0 + jnp.exp(-x))


def _colsum(x):
    return jnp.sum(x, axis=0, keepdims=True)


def _split(x):
    hi = x.astype(MXU)
    lo = (x - hi.astype(F32)).astype(MXU)
    return jnp.concatenate([hi, lo], axis=1)


def _rows(tm, n):
    return pl.BlockSpec((tm, n), lambda i: (i, 0))


def _fixed(r, n):
    return pl.BlockSpec((r, n), lambda i: (0, 0))


def _res(a):
    return pl.BlockSpec(a.shape, lambda i: (0, 0), pipeline_mode=pl.Buffered(1))


def _params(limit=None, sem=None):
    return pltpu.CompilerParams(vmem_limit_bytes=limit, dimension_semantics=sem)


def _in_proj(x, sh1, sc1, w_all, b_gate):
    s = x.shape[0]
    tm = 256

    def body(x_ref, sh_ref, sc_ref, w_ref, bg_ref, u_ref, qkv_ref, fl_ref, gl_ref):
        xhat, _ = _ln(x_ref[...])
        u = (xhat * (1.0 + sc_ref[...]) + sh_ref[...]).astype(MXU)
        u_ref[...] = u
        for c0 in range(0, OFF_FGATE, WIDTH):
            p = _dot_nt(u, w_ref[c0:c0 + WIDTH, :])
            if c0 in (0, 3 * WIDTH):
                p = p * QK_SCALE
            qkv_ref[:, c0:c0 + WIDTH] = p.astype(MXU)
        fl_ref[...] = _dot_nt(u, w_ref[OFF_FGATE:OFF_FGATE + LANE, :])
        for c0 in range(0, 2 * D, D):
            gl_ref[:, c0:c0 + D] = _dot_nt(u, w_ref[OFF_FGATE + LANE + c0:OFF_FGATE + LANE + c0 + D, :]) + bg_ref[:, c0:c0 + D]

    return pl.pallas_call(
        body, name="in_proj", grid=(s // tm,),
        in_specs=[_rows(tm, D), _fixed(1, D), _fixed(1, D), _res(w_all), _fixed(1, 2 * D)],
        out_specs=[_rows(tm, D), _rows(tm, OFF_FGATE), _rows(tm, LANE), _rows(tm, 2 * D)],
        out_shape=[jax.ShapeDtypeStruct((s, D), MXU), jax.ShapeDtypeStruct((s, OFF_FGATE), MXU),
                   jax.ShapeDtypeStruct((s, LANE), F32), jax.ShapeDtypeStruct((s, 2 * D), F32)],
        compiler_params=_params(VMEM_BIG),
    )(x, sh1, sc1, w_all, b_gate)


def _log_sigmoid_parts(z):
    e = jnp.exp(-jnp.abs(z))
    return -(jnp.maximum(z, 0.0) + jnp.log(1.0 + e)), e


def _fcum_fwd(fl, bf):
    s = fl.shape[0]
    nb = s // LANE

    def body(fl_ref, bf_ref, fc_ref, fkt_ref):
        r = lax.broadcasted_iota(jnp.int32, (LANE, LANE), 0)
        c = lax.broadcasted_iota(jnp.int32, (LANE, LANE), 1)
        tri = (c <= r).astype(F32)

        def step(b, carry):
            r0 = pl.multiple_of(b * LANE, LANE)
            xb = fl_ref[pl.ds(r0, LANE), :] + bf_ref[...]
            ls = _log_sigmoid_parts(-xb)[0]
            cs = jnp.dot(tri, ls, precision=lax.Precision.HIGHEST, preferred_element_type=F32) + carry
            fc_ref[pl.ds(r0, LANE), :] = cs
            fkt_ref[b] = cs.T[:N_FGATE, :]
            return cs[LANE - 1:LANE, :]

        lax.fori_loop(0, nb, step, jnp.zeros((1, LANE), F32))

    return pl.pallas_call(
        body, name="fcum_fwd",
        out_shape=[jax.ShapeDtypeStruct((s, LANE), F32), jax.ShapeDtypeStruct((nb, N_FGATE, LANE), F32)],
    )(fl, bf)


def _attn_specs(s, col0):
    return [pl.BlockSpec((TQ, LANE), lambda hp, i: (i, col0 + hp)),
            pl.BlockSpec((s, LANE), lambda hp, i: (0, col0 + 4 + hp)),
            pl.BlockSpec((s, LANE), lambda hp, i: (0, col0 + 8 + hp))]


def _tile_iotas():
    lane = lax.broadcasted_iota(jnp.int32, (TQ, LANE), 1)
    row = lax.broadcasted_iota(jnp.int32, (TQ, TQ), 0)
    col = lax.broadcasted_iota(jnp.int32, (TQ, TQ), 1)
    return lane, row, col


def _sub_blocks(nk=None):
    return [slice(j * LANE, (j + 1) * LANE) for j in range((TQ if nk is None else nk) // LANE)]


def _over_strips(tile, diagonal):
    del diagonal
    return tile(slice(0, TQ), TQ)


def _tri(below):
    r = lax.broadcasted_iota(jnp.int32, (LANE, LANE), 0)
    c = lax.broadcasted_iota(jnp.int32, (LANE, LANE), 1)
    t = jnp.concatenate([((r > c) if below else (r < c)).astype(MXU), jnp.ones((LANE, LANE), MXU)], axis=1)
    return jnp.concatenate([t, t], axis=0)


def _call_with_riders(body, name, grid, in_specs, out_specs, out_shape, scratch, args, riders, gather, limit=None):
    nr, n_in, n_out, n_sc = len(riders), len(in_specs), len(out_specs), len(scratch)

    def at_step(which):
        hit = None
        for d, n in enumerate(grid):
            here = pl.program_id(d) == (0 if which == "first" else n - 1)
            hit = here if hit is None else hit & here
        return hit

    def wrapped(*refs):
        ins, rin = refs[:n_in], refs[n_in:n_in + nr]
        outs, rout = refs[n_in + nr:n_in + nr + n_out], refs[n_in + nr + n_out:n_in + 2 * nr + n_out]
        own, sems = refs[n_in + 2 * nr + n_out:n_in + 2 * nr + n_out + n_sc], refs[n_in + 2 * nr + n_out + n_sc:]
        if nr:
            @pl.when(at_step("first"))
            def _():
                _exchange_start(rin, rout, sems, gather)

        body(*ins, *outs, *own)
        if nr:
            @pl.when(at_step("last"))
            def _():
                _exchange_wait(rin, rout, sems, gather)

    res = pl.pallas_call(
        wrapped, name=name, grid=grid,
        in_specs=list(in_specs) + [_ANY] * nr, out_specs=list(out_specs) + [_ANY] * nr,
        out_shape=list(out_shape) + _exchange_out_shape(riders),
        scratch_shapes=list(scratch) + (_exchange_sems(nr) if nr else []),
        compiler_params=_params(limit),
    )(*args, *riders)
    return res[:n_out], res[n_out:]


def _sb_fwd(qkv, riders=()):
    s = qkv.shape[0]
    nq = s // TQ
    assert nq <= RS_COUNT_LANE

    def body(q_ref, k_ref, v_ref, o_ref, rs_ref):
        i = pl.program_id(1)
        lane, row, col = _tile_iotas()
        u2 = _tri(True)
        diag = col < row
        q = q_ref[...]
        qms = [jnp.where(hm, q, jnp.zeros_like(q)) for hm in (lane < HEAD_DIM, lane >= HEAD_DIM)]

        def step(kb, carry, masked):
            k0 = pl.multiple_of(kb * TQ, TQ)
            k = k_ref[pl.ds(k0, TQ), :]
            v = v_ref[pl.ds(k0, TQ), :]
            def tile(rows, nk, qm, state):
                run, acc, rt = (t[rows] for t in state)
                z = _dot_nt(qm[rows], k[:nk])
                lneg, _ = _log_sigmoid_parts(z)
                lpos = z + lneg
                if masked:
                    lneg = jnp.where(diag[rows, :nk], lneg, 0.0)
                rt = jnp.where(lane[rows] == kb, run, rt)
                a = []
                for sl in reversed(_sub_blocks(nk)):
                    st = _dot(_split(lneg[:, sl]), u2)
                    a.append(jnp.exp(lpos[:, sl] + st[:, :LANE] + run))
                    run = run + st[:, LANE:]
                a = jnp.concatenate(a[::-1], axis=1)
                if masked:
                    a = jnp.where(diag[rows, :nk], a, 0.0)
                return run, acc + _dot(a.astype(MXU), v[:nk]), rt

            return tuple(_over_strips(functools.partial(tile, qm=qm, state=state), masked) for qm, state in zip(qms, carry))

        zero = jnp.zeros((TQ, LANE), F32)
        carry = step(i, ((zero, zero, zero),) * 2, True)

        def alive(cr):
            return jnp.maximum(jnp.max(cr[0][0]), jnp.max(cr[1][0])) > DEAD_LOG

        def walk(state):
            j, _, cr = state
            cr = step(i - 1 - j, cr, False)
            return j + 1, alive(cr), cr

        walked, _, carry = lax.while_loop(lambda state: (state[0] < i) & state[1], walk, (jnp.int32(0), alive(carry), carry))
        count = walked.astype(F32)
        rs_ref[0] = jnp.where(lane == RS_COUNT_LANE, count, carry[0][2])
        rs_ref[1] = jnp.where(lane == RS_COUNT_LANE, count, carry[1][2])
        o_ref[...] = jnp.where(lane < HEAD_DIM, carry[0][1], carry[1][1]).astype(o_ref.dtype)

    return _call_with_riders(
        body, "sb_fwd", (4, nq), _attn_specs(s, 0),
        [pl.BlockSpec((TQ, LANE), lambda hp, i: (i, hp)), pl.BlockSpec((2, TQ, LANE), lambda hp, i: (hp, i, 0))],
        [jax.ShapeDtypeStruct((s, WIDTH), MXU), jax.ShapeDtypeStruct((8, s, LANE), F32)], [], (qkv, qkv, qkv), riders, True)


def _sb_bwd(qkv, do, rs, riders=()):
    s = qkv.shape[0]
    nq = s // TQ

    def body(q_ref, k_ref, v_ref, do_ref, rs_ref, dq_ref, dk_ref, dv_ref, dk_acc, dv_acc):
        i = pl.program_id(1)

        @pl.when(i == 0)
        def _():
            dk_acc[...] = jnp.zeros_like(dk_acc)
            dv_acc[...] = jnp.zeros_like(dv_acc)

        lane, row, col = _tile_iotas()
        u2 = _tri(True)
        l2 = _tri(False)
        diag = col < row
        q = q_ref[...]
        do = do_ref[...]
        heads = [(jnp.where(hm, q, jnp.zeros_like(q)), jnp.where(hm, do, jnp.zeros_like(do)), rs_ref[hh])
                 for hh, hm in enumerate((lane < HEAD_DIM, lane >= HEAD_DIM))]

        def step(kb, carry, masked):
            k0 = pl.multiple_of(kb * TQ, TQ)
            k = k_ref[pl.ds(k0, TQ), :]
            v = v_ref[pl.ds(k0, TQ), :]
            to_keys = {}

            def tile(rows, nk, qm, dom, rblk, state):
                gpre, dq = (t[rows] for t in state)
                z = _dot_nt(qm[rows], k[:nk])
                lneg, e = _log_sigmoid_parts(z)
                lpos = z + lneg
                if masked:
                    lneg = jnp.where(diag[rows, :nk], lneg, 0.0)
                run = jnp.sum(jnp.where(lane[rows] == kb, rblk[rows], 0.0), axis=1, keepdims=True) + jnp.zeros_like(gpre)
                a = []
                for sl in reversed(_sub_blocks(nk)):
                    st = _dot(_split(lneg[:, sl]), u2)
                    a.append(jnp.exp(lpos[:, sl] + st[:, :LANE] + run))
                    run = run + st[:, LANE:]
                a = jnp.concatenate(a[::-1], axis=1)
                if masked:
                    a = jnp.where(diag[rows, :nk], a, 0.0)
                g = a * _dot_nt(dom[rows], v[:nk])
                pre = []
                for sl in _sub_blocks(nk):
                    pt = _dot(_split(g[:, sl]), l2)
                    pre.append(gpre + pt[:, :LANE])
                    gpre = gpre + pt[:, LANE:]
                sig = jnp.where(z >= 0.0, 1.0, e) / (1.0 + e)
                dz = g - (g + jnp.concatenate(pre, axis=1)) * sig
                if masked:
                    dz = jnp.where(diag[rows, :nk], dz, 0.0)
                dzb = dz.astype(MXU)
                both = to_keys.setdefault(nk, [0.0, 0.0])
                both[0] = both[0] + _dot_tn(dzb, qm[rows])
                both[1] = both[1] + _dot_tn(a.astype(MXU), dom[rows])
                return gpre, dq + _dot(dzb, k[:nk])

            new = tuple(_over_strips(functools.partial(tile, qm=qm, dom=dom, rblk=rblk, state=state), masked)
                        for (qm, dom, rblk), state in zip(heads, carry))
            for nk, (dk, dv) in to_keys.items():
                dk_acc[pl.ds(k0, nk), :] += dk
                dv_acc[pl.ds(k0, nk), :] += dv
            return new

        walked = jnp.max(jnp.where(lane[:8] == RS_COUNT_LANE, rs_ref[0, 0:8, :], 0.0))
        first = i - jnp.clip(walked.astype(jnp.int32), 0, i)
        zero = jnp.zeros((TQ, LANE), F32)
        carry = step(i, lax.fori_loop(first, i, lambda kb, cr: step(kb, cr, False), ((zero, zero),) * 2), True)
        dq_ref[...] = (jnp.where(lane < HEAD_DIM, carry[0][1], carry[1][1]) * QK_SCALE).astype(dq_ref.dtype)

        @pl.when(i == nq - 1)
        def _():
            dk_ref[...] = dk_acc[...].astype(dk_ref.dtype)
            dv_ref[...] = dv_acc[...].astype(dv_ref.dtype)

    blk = pl.BlockSpec((TQ, LANE), lambda hp, i: (i, hp))
    whole = pl.BlockSpec((s, LANE), lambda hp, i: (0, hp))
    return _call_with_riders(
        body, "sb_bwd", (4, nq), _attn_specs(s, 0) + [blk, pl.BlockSpec((2, TQ, LANE), lambda hp, i: (hp, i, 0))],
        [blk, whole, whole], [jax.ShapeDtypeStruct((s, WIDTH), MXU)] * 3,
        [pltpu.VMEM((s, LANE), F32), pltpu.VMEM((s, LANE), F32)], (qkv, qkv, qkv, do, rs), riders, False)


def _key_bias(fkt_ref, kb, h):
    n_sub = TQ // LANE
    return jnp.concatenate([fkt_ref[kb * n_sub + j, pl.ds(h, 1), :] for j in range(n_sub)], axis=1)


def _fox_fwd(qkv, fc, fkt):
    s = qkv.shape[0]
    nq = s // TQ
    nb = fkt.shape[0]

    def body(q_ref, k_ref, v_ref, fq_ref, fkt_ref, o_ref, lse_ref):
        hp = pl.program_id(0)
        i = pl.program_id(1)
        lane, row, col = _tile_iotas()
        diag = col <= row
        q = q_ref[...]
        fqb = fq_ref[...]
        heads = []
        for hh in range(2):
            h = 2 * hp + hh
            hm = (lane >= HEAD_DIM) if hh else (lane < HEAD_DIM)
            heads.append((h, jnp.where(hm, q, jnp.zeros_like(q)), jnp.sum(jnp.where(lane == h, fqb, 0.0), axis=1, keepdims=True)))

        def step(kb, carry, masked):
            k0 = pl.multiple_of(kb * TQ, TQ)
            k = k_ref[pl.ds(k0, TQ), :]
            v = v_ref[pl.ds(k0, TQ), :]
            def tile(rows, nk, h, qm, fq, state):
                m, l, acc = (t[rows] for t in state)
                z = _dot_nt(qm[rows], k[:nk]) + fq[rows] - _key_bias(fkt_ref, kb, h)[:, :nk]
                if masked:
                    z = jnp.where(diag[rows, :nk], z, NEG)
                mn = jnp.maximum(m, jnp.max(z, axis=1, keepdims=True))
                p = jnp.exp(z - mn)
                alpha = jnp.exp(m - mn)
                return mn, alpha * l + jnp.sum(p, axis=1, keepdims=True), alpha * acc + _dot(p.astype(MXU), v[:nk])

            return tuple(_over_strips(functools.partial(tile, h=h, qm=qm, fq=fq, state=state), masked)
                         for (h, qm, fq), state in zip(heads, carry))

        init = ((jnp.full((TQ, 1), NEG, F32), jnp.zeros((TQ, 1), F32), jnp.zeros((TQ, LANE), F32)),) * 2
        carry = step(i, lax.fori_loop(0, i, lambda kb, cr: step(kb, cr, False), init), True)
        outs = []
        for hh, (m, l, acc) in enumerate(carry):
            outs.append(acc / l)
            lse_ref[hh] = jnp.broadcast_to(m + jnp.log(l), (TQ, LANE))
        o_ref[...] = jnp.where(lane < HEAD_DIM, outs[0], outs[1]).astype(o_ref.dtype)

    return pl.pallas_call(
        body, name="fox_fwd", grid=(4, nq),
        in_specs=_attn_specs(s, 12) + [pl.BlockSpec((TQ, LANE), lambda hp, i: (i, 0)),
                                       pl.BlockSpec((nb, N_FGATE, LANE), lambda hp, i: (0, 0, 0))],
        out_specs=[pl.BlockSpec((TQ, LANE), lambda hp, i: (i, hp)), pl.BlockSpec((2, TQ, LANE), lambda hp, i: (hp, i, 0))],
        out_shape=[jax.ShapeDtypeStruct((s, WIDTH), MXU), jax.ShapeDtypeStruct((8, s, LANE), F32)],
    )(qkv, qkv, qkv, fc, fkt)


def _fox_bwd(qkv, fc, fkt, do, o, lse):
    s = qkv.shape[0]
    nq = s // TQ
    nb = fkt.shape[0]

    def body(q_ref, k_ref, v_ref, fq_ref, fkt_ref, do_ref, o_ref, lse_ref, dq_ref, dk_ref, dv_ref, dfk_ref, dfq_ref, dk_acc, dv_acc):
        hp = pl.program_id(0)
        i = pl.program_id(1)

        @pl.when(i == 0)
        def _():
            dk_acc[...] = jnp.zeros_like(dk_acc)
            dv_acc[...] = jnp.zeros_like(dv_acc)

        @pl.when((i == 0) & (hp == 0))
        def _():
            dfk_ref[...] = jnp.zeros_like(dfk_ref)

        lane, row, col = _tile_iotas()
        diag = col <= row
        q = q_ref[...]
        do = do_ref[...]
        dof = do.astype(F32) * o_ref[...].astype(F32)
        fqb = fq_ref[...]
        heads = []
        for hh in range(2):
            h = 2 * hp + hh
            hm = (lane >= HEAD_DIM) if hh else (lane < HEAD_DIM)
            heads.append((h, jnp.where(hm, q, jnp.zeros_like(q)), jnp.where(hm, do, jnp.zeros_like(do)),
                          jnp.sum(jnp.where(hm, dof, 0.0), axis=1, keepdims=True),
                          jnp.sum(jnp.where(lane == h, fqb, 0.0), axis=1, keepdims=True), lse_ref[hh][:, :1]))

        def step(kb, carry, masked):
            k0 = pl.multiple_of(kb * TQ, TQ)
            k = k_ref[pl.ds(k0, TQ), :]
            v = v_ref[pl.ds(k0, TQ), :]
            to_keys = {}

            def tile(rows, nk, h, qm, dom, delta, fq, lse_t, state):
                dq, rsum = (t[rows] for t in state)
                z = _dot_nt(qm[rows], k[:nk]) + fq[rows] - _key_bias(fkt_ref, kb, h)[:, :nk]
                if masked:
                    z = jnp.where(diag[rows, :nk], z, NEG)
                p = jnp.exp(z - lse_t[rows])
                ds = p * (_dot_nt(dom[rows], v[:nk]) - delta[rows])
                dsb = ds.astype(MXU)
                both = to_keys.setdefault(nk, [0.0, 0.0])
                both[0] = both[0] + _dot_tn(dsb, qm[rows])
                both[1] = both[1] + _dot_tn(p.astype(MXU), dom[rows])
                csum = _colsum(ds)
                for j, sl in enumerate(_sub_blocks(nk)):
                    dfk_ref[kb * len(_sub_blocks()) + j, pl.ds(h, 1), :] += -csum[:, sl]
                return dq + _dot(dsb, k[:nk]), rsum + jnp.sum(ds, axis=1, keepdims=True)

            new = tuple(_over_strips(functools.partial(tile, h=h, qm=qm, dom=dom, delta=delta, fq=fq, lse_t=lse_t, state=state), masked)
                        for (h, qm, dom, delta, fq, lse_t), state in zip(heads, carry))
            for nk, (dk, dv) in to_keys.items():
                dk_acc[pl.ds(k0, nk), :] += dk
                dv_acc[pl.ds(k0, nk), :] += dv
            return new

        init = ((jnp.zeros((TQ, LANE), F32), jnp.zeros((TQ, 1), F32)),) * 2
        carry = step(i, lax.fori_loop(0, i, lambda kb, cr: step(kb, cr, False), init), True)
        dq_ref[...] = (jnp.where(lane < HEAD_DIM, carry[0][0], carry[1][0]) * QK_SCALE).astype(dq_ref.dtype)
        dfq_ref[0] = jnp.where(lane == heads[0][0], carry[0][1], jnp.where(lane == heads[1][0], carry[1][1], 0.0))

        @pl.when(i == nq - 1)
        def _():
            dk_ref[...] = dk_acc[...].astype(dk_ref.dtype)
            dv_ref[...] = dv_acc[...].astype(dv_ref.dtype)

    blk = pl.BlockSpec((TQ, LANE), lambda hp, i: (i, hp))
    whole = pl.BlockSpec((s, LANE), lambda hp, i: (0, hp))
    pair = pl.BlockSpec((2, TQ, LANE), lambda hp, i: (hp, i, 0))
    fkt_spec = pl.BlockSpec((nb, N_FGATE, LANE), lambda hp, i: (0, 0, 0))
    return pl.pallas_call(
        body, name="fox_bwd", grid=(4, nq),
        in_specs=_attn_specs(s, 12) + [pl.BlockSpec((TQ, LANE), lambda hp, i: (i, 0)), fkt_spec, blk, blk, pair],
        out_specs=[blk, whole, whole, fkt_spec, pl.BlockSpec((1, TQ, LANE), lambda hp, i: (hp, i, 0))],
        out_shape=[jax.ShapeDtypeStruct((s, WIDTH), MXU)] * 3
        + [jax.ShapeDtypeStruct((nb, N_FGATE, LANE), F32), jax.ShapeDtypeStruct((4, s, LANE), F32)],
        scratch_shapes=[pltpu.VMEM((s, LANE), F32), pltpu.VMEM((s, LANE), F32)],
    )(qkv, qkv, qkv, fc, fkt, do, o, lse)


def _fcum_bwd(dfkt, dfq, fl, bf):
    s = fl.shape[0]
    nb = s // LANE

    def body(dfkt_ref, dfq_ref, fl_ref, bf_ref, df_ref, dbf_ref, tail_ref):
        @pl.when(pl.program_id(0) == 0)
        def _():
            tail_ref[...] = jnp.zeros_like(tail_ref)
            dbf_ref[...] = jnp.zeros_like(dbf_ref)

        r = lax.broadcasted_iota(jnp.int32, (LANE, LANE), 0)
        c = lax.broadcasted_iota(jnp.int32, (LANE, LANE), 1)
        tri = (c >= r).astype(F32)
        dfc = jnp.concatenate([dfkt_ref[0], jnp.zeros((LANE - N_FGATE, LANE), F32)], axis=0).T
        dfc = dfc + ((dfq_ref[0] + dfq_ref[1]) + (dfq_ref[2] + dfq_ref[3]))
        dls = jnp.dot(tri, dfc, precision=lax.Precision.HIGHEST, preferred_element_type=F32) + tail_ref[...]
        xb = fl_ref[...] + bf_ref[...]
        e = jnp.exp(-jnp.abs(xb))
        dfl = dls * (jnp.where(xb >= 0.0, e, 1.0) / (1.0 + e))
        df_ref[...] = dfl.astype(df_ref.dtype)
        tail_ref[...] = dls[0:1, :]
        dbf_ref[...] += _colsum(dfl)

    return pl.pallas_call(
        body, name="fcum_bwd", grid=(nb,),
        in_specs=[pl.BlockSpec((1, N_FGATE, LANE), lambda j: (nb - 1 - j, 0, 0)), pl.BlockSpec((4, LANE, LANE), lambda j: (0, nb - 1 - j, 0)),
                  pl.BlockSpec((LANE, LANE), lambda j: (nb - 1 - j, 0)), _fixed(1, LANE)],
        out_specs=[pl.BlockSpec((LANE, LANE), lambda j: (nb - 1 - j, 0)), _fixed(1, LANE)],
        out_shape=[jax.ShapeDtypeStruct((s, LANE), MXU), jax.ShapeDtypeStruct((1, LANE), F32)],
        scratch_shapes=[pltpu.VMEM((1, LANE), F32)],
    )(dfkt, dfq, fl, bf)


def _mix_fwd(x, o_sb, o_fx, gl, w_sb, w_fx, w_o, g1, ln1_g, ln1_b, sh2, sc2):
    s = x.shape[0]
    tm = 256

    def body(x_ref, osb_ref, ofx_ref, gl_ref, wsb_ref, wfx_ref, wo_ref, g1_ref, lg_ref, lb_ref, sh_ref, sc_ref, r1_ref, u2_ref):
        mixin = (_sigmoid(gl_ref[:, :D]) * _dot(osb_ref[...], wsb_ref[...])
                 + _sigmoid(gl_ref[:, D:]) * _dot(ofx_ref[...], wfx_ref[...]))
        r1 = ALPHA * x_ref[...] + g1_ref[...] * _dot(mixin.astype(MXU), wo_ref[...])
        r1_ref[...] = r1
        x1 = _ln(r1)[0] * lg_ref[...] + lb_ref[...]
        u2_ref[...] = (_ln(x1)[0] * (1.0 + sc_ref[...]) + sh_ref[...]).astype(MXU)

    vec = _fixed(1, D)
    return pl.pallas_call(
        body, name="mix_fwd", grid=(s // tm,),
        in_specs=[_rows(tm, D), _rows(tm, WIDTH), _rows(tm, WIDTH), _rows(tm, 2 * D), _res(w_sb), _res(w_fx), _res(w_o),
                  vec, vec, vec, vec, vec],
        out_specs=[_rows(tm, D), _rows(tm, D)],
        out_shape=[jax.ShapeDtypeStruct((s, D), F32), jax.ShapeDtypeStruct((s, D), MXU)],
        compiler_params=_params(VMEM_BIG),
    )(x, o_sb, o_fx, gl, w_sb, w_fx, w_o, g1, ln1_g, ln1_b, sh2, sc2)


def _ffn_fwd(r1, u2, tgt, w_g, w_u, w_d, g2, ln1_g, ln1_b, ln2_g, ln2_b):
    s = r1.shape[0]
    tm = 256

    def body(r1_ref, u2_ref, t_ref, wg_ref, wu_ref, wd_ref, g2_ref, l1g_ref, l1b_ref, l2g_ref, l2b_ref,
             hg_ref, hu_ref, dxa_ref, dh_ref, acc_ref):
        @pl.when(pl.program_id(0) == 0)
        def _():
            acc_ref[...] = jnp.zeros_like(acc_ref)

        u2 = u2_ref[...]
        hg = _dot_nt(u2, wg_ref[...])
        hu = _dot_nt(u2, wu_ref[...])
        hg_ref[...] = hg
        hu_ref[...] = hu
        h = _dot((hg * _sigmoid(hg) * hu).astype(MXU), wd_ref[...])
        x1 = _ln(r1_ref[...])[0] * l1g_ref[...] + l1b_ref[...]
        xh2, rstd2 = _ln(ALPHA * x1 + g2_ref[...] * h)
        err = xh2 * l2g_ref[...] + l2b_ref[...] - t_ref[...]
        dy = err * (1.0 / D)
        dr2 = _ln_bwd(dy * l2g_ref[...], xh2, rstd2)
        dxa_ref[...] = ALPHA * dr2
        dh_ref[...] = (g2_ref[...] * dr2).astype(MXU)
        acc_ref[0:1, :] += _colsum(dr2 * h)
        acc_ref[1:2, :] += _colsum(dy * xh2)
        acc_ref[2:3, :] += _colsum(dy)
        acc_ref[3:4, :] += _colsum(err * err) * (0.5 / D)

    vec = _fixed(1, D)
    return pl.pallas_call(
        body, name="ffn_fwd", grid=(s // tm,),
        in_specs=[_rows(tm, D), _rows(tm, D), _rows(tm, D), _res(w_g), _res(w_u), _res(w_d), vec, vec, vec, vec, vec],
        out_specs=[_rows(tm, D_FF), _rows(tm, D_FF), _rows(tm, D), _rows(tm, D), _fixed(8, D)],
        out_shape=[jax.ShapeDtypeStruct((s, D_FF), F32), jax.ShapeDtypeStruct((s, D_FF), F32),
                   jax.ShapeDtypeStruct((s, D), F32), jax.ShapeDtypeStruct((s, D), MXU), jax.ShapeDtypeStruct((8, D), F32)],
        compiler_params=_params(VMEM_BIG),
    )(r1, u2, tgt, w_g, w_u, w_d, g2, ln1_g, ln1_b, ln2_g, ln2_b)


def _ffn_bwd(dh, hg, hu, w_g, w_u, w_d):
    s = dh.shape[0]
    tm = 256
    half = D_FF // 2

    def body(dh_ref, hg_ref, hu_ref, wg_ref, wu_ref, wd_ref, act_ref, dhg_ref, dhu_ref, du2_ref):
        dh = dh_ref[...]
        du2 = jnp.zeros((tm, D), F32)
        for c0 in (0, half):
            cols = slice(c0, c0 + half)
            dact = _dot_nt(dh, wd_ref[cols, :])
            hg = hg_ref[:, cols]
            hu = hu_ref[:, cols]
            sg = _sigmoid(hg)
            sl = hg * sg
            act_ref[:, cols] = (sl * hu).astype(MXU)
            dhg = (dact * hu * (sg * (1.0 + hg * (1.0 - sg)))).astype(MXU)
            dhu = (dact * sl).astype(MXU)
            dhg_ref[:, cols] = dhg
            dhu_ref[:, cols] = dhu
            du2 = du2 + _dot(dhg, wg_ref[cols, :]) + _dot(dhu, wu_ref[cols, :])
        du2_ref[...] = du2

    return pl.pallas_call(
        body, name="ffn_bwd", grid=(s // tm,),
        in_specs=[_rows(tm, D), _rows(tm, D_FF), _rows(tm, D_FF), _res(w_g), _res(w_u), _res(w_d)],
        out_specs=[_rows(tm, D_FF), _rows(tm, D_FF), _rows(tm, D_FF), _rows(tm, D)],
        out_shape=[jax.ShapeDtypeStruct((s, D_FF), MXU)] * 3 + [jax.ShapeDtypeStruct((s, D), F32)],
        compiler_params=_params(VMEM_BIG),
    )(dh, hg, hu, w_g, w_u, w_d)


def _mix_bwd(du2, dxa, r1, o_sb, o_fx, gl, w_sb, w_fx, w_o, g1, ln1_g, ln1_b, sc2):
    s = r1.shape[0]
    tm = 256

    def body(du2_ref, dxa_ref, r1_ref, osb_ref, ofx_ref, gl_ref, wsb_ref, wfx_ref, wo_ref, g1_ref, lg_ref, lb_ref, sc_ref,
             dx_ref, mixin_ref, dmix_ref, dysb_ref, dyfx_ref, dosb_ref, dofx_ref, dgl_ref, dbg_ref, acc_ref):
        @pl.when(pl.program_id(0) == 0)
        def _():
            acc_ref[...] = jnp.zeros_like(acc_ref)
            dbg_ref[...] = jnp.zeros_like(dbg_ref)

        du2 = du2_ref[...]
        xh1, rstd1 = _ln(r1_ref[...])
        x1 = xh1 * lg_ref[...] + lb_ref[...]
        n1, rstdn = _ln(x1)
        dx1 = dxa_ref[...] + _ln_bwd(du2 * (1.0 + sc_ref[...]), n1, rstdn)
        dr1 = _ln_bwd(dx1 * lg_ref[...], xh1, rstd1)
        dx_ref[...] = ALPHA * dr1
        ysb = _dot(osb_ref[...], wsb_ref[...])
        yfx = _dot(ofx_ref[...], wfx_ref[...])
        gs = _sigmoid(gl_ref[:, :D])
        gf = _sigmoid(gl_ref[:, D:])
        mixin = (gs * ysb + gf * yfx).astype(MXU)
        mixin_ref[...] = mixin
        mix = _dot(mixin, wo_ref[...])
        dmix = (g1_ref[...] * dr1).astype(MXU)
        dmix_ref[...] = dmix
        dmixin = _dot_nt(dmix, wo_ref[...])
        dysb = (dmixin * gs).astype(MXU)
        dyfx = (dmixin * gf).astype(MXU)
        dysb_ref[...] = dysb
        dyfx_ref[...] = dyfx
        dosb_ref[...] = _dot_nt(dysb, wsb_ref[...]).astype(MXU)
        dofx_ref[...] = _dot_nt(dyfx, wfx_ref[...]).astype(MXU)
        dgs = dmixin * ysb * gs * (1.0 - gs)
        dgf = dmixin * yfx * gf * (1.0 - gf)
        dgl_ref[:, :D] = dgs.astype(MXU)
        dgl_ref[:, D:] = dgf.astype(MXU)
        dbg_ref[:, :D] += _colsum(dgs)
        dbg_ref[:, D:] += _colsum(dgf)
        acc_ref[0:1, :] += _colsum(du2)
        acc_ref[1:2, :] += _colsum(du2 * n1)
        acc_ref[2:3, :] += _colsum(dx1 * xh1)
        acc_ref[3:4, :] += _colsum(dx1)
        acc_ref[4:5, :] += _colsum(dr1 * mix)

    vec = _fixed(1, D)
    return pl.pallas_call(
        body, name="mix_bwd", grid=(s // tm,),
        in_specs=[_rows(tm, D), _rows(tm, D), _rows(tm, D), _rows(tm, WIDTH), _rows(tm, WIDTH), _rows(tm, 2 * D),
                  _res(w_sb), _res(w_fx), _res(w_o), vec, vec, vec, vec],
        out_specs=[_rows(tm, D), _rows(tm, D), _rows(tm, D), _rows(tm, D), _rows(tm, D), _rows(tm, WIDTH), _rows(tm, WIDTH),
                   _rows(tm, 2 * D), _fixed(1, 2 * D), _fixed(8, D)],
        out_shape=[jax.ShapeDtypeStruct((s, D), F32)] + [jax.ShapeDtypeStruct((s, D), MXU)] * 4
        + [jax.ShapeDtypeStruct((s, WIDTH), MXU)] * 2
        + [jax.ShapeDtypeStruct((s, 2 * D), MXU), jax.ShapeDtypeStruct((1, 2 * D), F32), jax.ShapeDtypeStruct((8, D), F32)],
        compiler_params=_params(VMEM_BIG),
    )(du2, dxa, r1, o_sb, o_fx, gl, w_sb, w_fx, w_o, g1, ln1_g, ln1_b, sc2)


def _in_bwd(pieces, x, dxa, w_all, sc1, riders=()):
    s = x.shape[0]
    tm = 256
    n_p = len(pieces)

    def body(*refs):
        p_refs = refs[:n_p]
        x_ref, dxa_ref, w_ref, sc_ref, gx_ref, acc_ref = refs[n_p:]

        @pl.when(pl.program_id(0) == 0)
        def _():
            acc_ref[...] = jnp.zeros_like(acc_ref)

        du1 = jnp.zeros((tm, D), F32)
        for p_ref, (arr, c0) in zip(p_refs, pieces):
            du1 = du1 + _dot(p_ref[...], w_ref[c0:c0 + arr.shape[1], :])
        n0, rstd0 = _ln(x_ref[...])
        gx_ref[...] = dxa_ref[...] + _ln_bwd(du1 * (1.0 + sc_ref[...]), n0, rstd0)
        acc_ref[0:1, :] += _colsum(du1)
        acc_ref[1:2, :] += _colsum(du1 * n0)

    return _call_with_riders(
        body, "in_bwd", (s // tm,),
        [_rows(tm, a.shape[1]) for a, _ in pieces] + [_rows(tm, D), _rows(tm, D), _res(w_all), _fixed(1, D)],
        [_rows(tm, D), _fixed(8, D)], [jax.ShapeDtypeStruct((s, D), F32), jax.ShapeDtypeStruct((8, D), F32)], [],
        (*[a for a, _ in pieces], x, dxa, w_all, sc1), riders, False, VMEM_BIG)


def _matmul_tn(a, b, name, narrow=False):
    s, m = a.shape
    n = b.shape[1]
    tm = 512 if m % 512 == 0 else (m if m < 512 else m // 2)
    tn = n // 2 if n > 2048 else n
    ts = 512
    assert m % tm == 0 and tm % LANE == 0 and n % tn == 0 and tn % LANE == 0 and s % ts == 0

    def body(a_ref, b_ref, o_ref, *narrow_ref):
        @pl.when(pl.program_id(2) == 0)
        def _():
            o_ref[...] = jnp.zeros_like(o_ref)

        o_ref[...] += _dot_tn(a_ref[...], b_ref[...])
        if narrow:
            @pl.when(pl.program_id(2) == s // ts - 1)
            def _():
                narrow_ref[0][...] = o_ref[...].astype(MXU)

    out_blk = pl.BlockSpec((tm, tn), lambda i, j, k: (i, j))
    res = pl.pallas_call(
        body, name=name, grid=(m // tm, n // tn, s // ts),
        in_specs=[pl.BlockSpec((ts, tm), lambda i, j, k: (k, i)), pl.BlockSpec((ts, tn), lambda i, j, k: (k, j))],
        out_specs=[out_blk] * (2 if narrow else 1),
        out_shape=[jax.ShapeDtypeStruct((m, n), F32)] + ([jax.ShapeDtypeStruct((m, n), MXU)] if narrow else []),
        compiler_params=_params(VMEM_BIG),
    )(a, b)
    return tuple(res) if narrow else res[0]


def _local_step(x, tgt, ada, w_all, b_gate, bf_pad, late_weights, early_grads, w_in_grads, ln1_g, ln1_b, ln2_g, ln2_b):
    sh1, sc1, g1, sh2, sc2, g2 = ada
    u1, qkv, fl, gl = _in_proj(x, sh1, sc1, w_all, b_gate)
    fc, fkt = _fcum_fwd(fl, bf_pad)
    late_riders, late_full = late_weights
    (o_sb, rs), late_gathered = _sb_fwd(qkv, late_riders)
    w_sb, w_fx, w_o, w_g, w_u, w_d = late_full(late_gathered)
    o_fx, lse = _fox_fwd(qkv, fc, fkt)
    r1, u2 = _mix_fwd(x, o_sb, o_fx, gl, w_sb, w_fx, w_o, g1, ln1_g, ln1_b, sh2, sc2)
    hg, hu, dxa2, dh, acc_f = _ffn_fwd(r1, u2, tgt, w_g, w_u, w_d, g2, ln1_g, ln1_b, ln2_g, ln2_b)
    act, dhg, dhu, du2 = _ffn_bwd(dh, hg, hu, w_g, w_u, w_d)
    dxa1, mixin, dmix, dysb, dyfx, dosb, dofx, dgl, dbg, acc_m = _mix_bwd(
        du2, dxa2, r1, o_sb, o_fx, gl, w_sb, w_fx, w_o, g1, ln1_g, ln1_b, sc2)
    early = dict(w_sb_out=_matmul_tn(o_sb, dysb, "dw_sb_out", True), w_fox_out=_matmul_tn(o_fx, dyfx, "dw_fox_out", True),
                 w_o=_matmul_tn(mixin, dmix, "dw_o", True), w_ffn_gate=_matmul_tn(dhg, u2, "dw_ffn_gate", True),
                 w_ffn_up=_matmul_tn(dhu, u2, "dw_ffn_up", True), w_ffn_down=_matmul_tn(act, dh, "dw_ffn_down", True))
    (dq_sb, dk_sb, dv_sb), early_received = _sb_bwd(qkv, dosb, rs, early_grads(early))
    dq_fx, dk_fx, dv_fx, dfkt, dfq = _fox_bwd(qkv, fc, fkt, dofx, o_fx, lse)
    df, dbf = _fcum_bwd(dfkt, dfq, fl, bf_pad)
    pieces = [(dq_sb, 0), (dk_sb, WIDTH), (dv_sb, 2 * WIDTH), (dq_fx, 3 * WIDTH), (dk_fx, 4 * WIDTH), (dv_fx, 5 * WIDTH),
              (df, OFF_FGATE), (dgl, OFF_FGATE + LANE)]
    dw_in = [_matmul_tn(p, u1, f"dw_in_{j}") for j, (p, _) in enumerate(pieces)]
    (grad_x, acc_i), w_in_received = _in_bwd(pieces, x, dxa1, w_all, sc1, w_in_grads(dw_in))
    return dict(
        loss_lanes=acc_f[3:4], grad_x=grad_x, dw_in=dw_in, early=early, early_received=early_received,
        w_in_received=w_in_received,
        d_ada=[acc_i[0:1], acc_i[1:2], acc_m[4:5], acc_m[0:1], acc_m[1:2], acc_f[0:1]],
        dln1_g=acc_m[2:3], dln1_b=acc_m[3:4], dln2_g=acc_f[1:2], dln2_b=acc_f[2:3], db_gate=dbg, db_forget=dbf)


_MESH_ID = pl.DeviceIdType.MESH
_ANY = pl.BlockSpec(memory_space=pl.ANY)
_VMEM = pl.BlockSpec(memory_space=pltpu.VMEM)


def _mesh_pos():
    return lax.axis_index("x"), lax.axis_index("y"), lax.axis_index("c")


def _other_chips(x, y):
    return [(1 - x, y), (x, 1 - y), (1 - x, 1 - y)]


def _allgather_rows(v, name):
    n = v.shape[1]

    def body(v_ref, out_ref, send_sems, recv_sems, local_sem):
        x, y, c = _mesh_pos()
        me = 4 * x + 2 * y + c
        mine = pltpu.make_async_copy(v_ref, out_ref.at[me], local_sem)
        mine.start()
        copies = []
        for d in range(1, 8):
            fx, fy, fc = (d >> 2) & 1, (d >> 1) & 1, d & 1
            to = (1 - x if fx else x, 1 - y if fy else y, 1 - c if fc else c)
            cp = pltpu.make_async_remote_copy(src_ref=v_ref, dst_ref=out_ref.at[me], send_sem=send_sems.at[d - 1],
                                              recv_sem=recv_sems.at[d - 1], device_id=to, device_id_type=_MESH_ID)
            cp.start()
            copies.append(cp)
        for cp in copies:
            cp.wait_recv()
        for cp in copies:
            cp.wait_send()
        mine.wait()

    return pl.pallas_call(
        body, name=name, in_specs=[_VMEM], out_specs=_VMEM,
        out_shape=jax.ShapeDtypeStruct((8, 1, n), v.dtype),
        scratch_shapes=[pltpu.SemaphoreType.DMA((7,)), pltpu.SemaphoreType.DMA((7,)), pltpu.SemaphoreType.DMA(())],
    )(v)


def _chip_exchange(arrays, name, gather):
    nt = len(arrays)

    def body(*refs):
        ins, outs = refs[:nt], refs[nt:2 * nt]
        _exchange_start(ins, outs, refs[2 * nt:], gather)
        _exchange_wait(ins, outs, refs[2 * nt:], gather)

    return pl.pallas_call(
        body, name=name, in_specs=[_ANY] * nt, out_specs=[_ANY] * nt, out_shape=_exchange_out_shape(arrays),
        scratch_shapes=_exchange_sems(nt),
    )(*arrays)


def _exchange_out_shape(arrays):
    return [jax.ShapeDtypeStruct((4,) + a.shape[-2:], a.dtype) for a in arrays]


def _exchange_sems(nt):
    return [pltpu.SemaphoreType.DMA((3 * nt,)), pltpu.SemaphoreType.DMA((3 * nt,)), pltpu.SemaphoreType.DMA((nt,))]


def _exchange_copies(ins, outs, sems, gather):
    send_sems, recv_sems, local_sems = sems
    x, y, c = _mesh_pos()
    me = 2 * x + y
    local, remote = [], []
    for t in range(len(ins)):
        local.append(pltpu.make_async_copy(ins[t] if gather else ins[t].at[me], outs[t].at[me], local_sems.at[t]))
        for j, (px, py) in enumerate(_other_chips(x, y)):
            remote.append(pltpu.make_async_remote_copy(
                src_ref=ins[t] if gather else ins[t].at[2 * px + py], dst_ref=outs[t].at[me], send_sem=send_sems.at[3 * t + j],
                recv_sem=recv_sems.at[3 * t + j], device_id=(px, py, c), device_id_type=_MESH_ID))
    return local, remote


def _exchange_start(ins, outs, sems, gather):
    local, remote = _exchange_copies(ins, outs, sems, gather)
    for cp in local + remote:
        cp.start()


def _exchange_wait(ins, outs, sems, gather):
    local, remote = _exchange_copies(ins, outs, sems, gather)
    for cp in remote:
        cp.wait_recv()
    for cp in remote:
        cp.wait_send()
    for cp in local:
        cp.wait()


def _gather_two_level(shard, name):
    r, n = shard.shape
    half = n // 2
    assert half % LANE == 0

    def body(in_ref, out_ref, ici_send, ici_recv, d2d_send, d2d_recv, local_sem):
        x, y, c = _mesh_pos()
        me = 2 * x + y
        mine = pl.ds(pl.multiple_of(c * half, LANE), half)
        theirs = pl.ds(pl.multiple_of((1 - c) * half, LANE), half)
        local = pltpu.make_async_copy(in_ref, out_ref.at[me], local_sem)
        local.start()
        chips = _other_chips(x, y)
        over_ici = [pltpu.make_async_remote_copy(
            src_ref=in_ref.at[:, mine], dst_ref=out_ref.at[me, :, mine], send_sem=ici_send.at[j], recv_sem=ici_recv.at[j],
            device_id=(px, py, c), device_id_type=_MESH_ID) for j, (px, py) in enumerate(chips)]
        for cp in over_ici:
            cp.start()
        passed_on = [pltpu.make_async_remote_copy(
            src_ref=out_ref.at[2 * px + py, :, mine], dst_ref=out_ref.at[2 * px + py, :, mine], send_sem=d2d_send.at[j],
            recv_sem=d2d_recv.at[j], device_id=(x, y, 1 - c), device_id_type=_MESH_ID) for j, (px, py) in enumerate(chips)]
        for j, (px, py) in enumerate(chips):
            pltpu.make_async_remote_copy(
                src_ref=in_ref.at[:, mine], dst_ref=out_ref.at[2 * px + py, :, mine], send_sem=ici_send.at[j],
                recv_sem=ici_recv.at[j], device_id=(px, py, c), device_id_type=_MESH_ID).wait_recv()
            passed_on[j].start()
        for j, (px, py) in enumerate(chips):
            pltpu.make_async_remote_copy(
                src_ref=out_ref.at[2 * px + py, :, theirs], dst_ref=out_ref.at[2 * px + py, :, theirs], send_sem=d2d_send.at[j],
                recv_sem=d2d_recv.at[j], device_id=(x, y, 1 - c), device_id_type=_MESH_ID).wait_recv()
        for cp in over_ici + passed_on:
            cp.wait_send()
        local.wait()

    sems = pltpu.SemaphoreType.DMA((3,))
    return pl.pallas_call(
        body, name=name, in_specs=[_ANY], out_specs=_ANY, out_shape=jax.ShapeDtypeStruct((4, r, n), shard.dtype),
        scratch_shapes=[sems, sems, sems, sems, pltpu.SemaphoreType.DMA(())],
    )(shard)


def _sibling_exchange(arrays, name):
    nt = len(arrays)

    def body(*refs):
        ins, outs = refs[:nt], refs[nt:2 * nt]
        send_sems, recv_sems = refs[2 * nt:]
        x, y, c = _mesh_pos()
        copies = []
        for t in range(nt):
            cp = pltpu.make_async_remote_copy(src_ref=ins[t], dst_ref=outs[t], send_sem=send_sems.at[t], recv_sem=recv_sems.at[t],
                                              device_id=(x, y, 1 - c), device_id_type=_MESH_ID)
            cp.start()
            copies.append(cp)
        for cp in copies:
            cp.wait_recv()
        for cp in copies:
            cp.wait_send()

    return pl.pallas_call(
        body, name=name, in_specs=[_ANY] * nt, out_specs=[_ANY] * nt,
        out_shape=[jax.ShapeDtypeStruct(a.shape, a.dtype) for a in arrays],
        scratch_shapes=[pltpu.SemaphoreType.DMA((nt,)), pltpu.SemaphoreType.DMA((nt,))],
    )(*arrays)


def _tiles(r, n):
    for tr in (256, 352, 128):
        if r % tr == 0:
            return tr, n, r // tr, lambda i: (i, 0)
    assert n % 256 == 0
    return r, 256, n // 256, lambda i: (0, i)


def _reduce_chips(chip, pieces, recv, name):
    _, r, n = pieces.shape
    tr, tn, steps, at = _tiles(r, n)

    def body(chip_ref, own_ref, recv_ref, out_ref):
        me = chip_ref[0]
        total = jnp.zeros((tr, tn), F32)
        for k in range(4):
            total = total + jnp.where(me == k, own_ref[0], recv_ref[k].astype(F32))
        out_ref[...] = total

    return pl.pallas_call(
        body, name=name,
        grid_spec=pltpu.PrefetchScalarGridSpec(
            num_scalar_prefetch=1, grid=(steps,),
            in_specs=[pl.BlockSpec((1, tr, tn), lambda i, chip_ref: (chip_ref[0],) + at(i)),
                      pl.BlockSpec((4, tr, tn), lambda i, chip_ref: (0,) + at(i))],
            out_specs=pl.BlockSpec((tr, tn), lambda i, chip_ref: at(i))),
        out_shape=jax.ShapeDtypeStruct((r, n), F32),
    )(chip, pieces, recv)


def _adamw_math(w, g, m, v):
    m = ADAM_B1 * m + (1.0 - ADAM_B1) * g
    v = ADAM_B2 * v + (1.0 - ADAM_B2) * (g * g)
    m_hat = m / (1.0 - ADAM_B1 ** ADAM_STEP)
    v_hat = v / (1.0 - ADAM_B2 ** ADAM_STEP)
    return -ADAM_LR * (m_hat / (jnp.sqrt(v_hat) + ADAM_EPS) + ADAM_WD * w), m, v


def _adamw(w, m, v, g_parts, name):
    r, n = w.shape
    tr, tn, steps, at = _tiles(r, n)
    blk = pl.BlockSpec((tr, tn), at)
    ng = len(g_parts)

    def body(*refs):
        w_ref, m_ref, v_ref = refs[:3]
        g_refs = refs[3:3 + ng]
        g_out, d_out, m_out, v_out = refs[3 + ng:]
        g = g_refs[0][...]
        for gr in g_refs[1:]:
            g = g + gr[...]
        g_out[...] = g
        d_out[...], m_out[...], v_out[...] = _adamw_math(w_ref[...], g, m_ref[...], v_ref[...])

    return pl.pallas_call(
        body, name=name, grid=(steps,),
        in_specs=[blk] * (3 + ng), out_specs=[blk] * 4,
        out_shape=[jax.ShapeDtypeStruct((r, n), F32)] * 4,
    )(w, m, v, *g_parts)


def _ada_fwd(c_all, w_shard, b_shard):
    n = w_shard.shape[1]
    tn = 512

    def body(c_ref, w_ref, b_ref, o_ref):
        cv = c_ref[...]
        ca = (cv * _sigmoid(cv)).astype(MXU)
        o_ref[...] = _dot(ca, w_ref[...].astype(MXU)) + b_ref[...]

    return pl.pallas_call(
        body, name="ada_fwd", grid=(n // tn,),
        in_specs=[_fixed(8, D), pl.BlockSpec((D, tn), lambda j: (0, j)), pl.BlockSpec((1, tn), lambda j: (0, j))],
        out_specs=pl.BlockSpec((8, tn), lambda j: (0, j)),
        out_shape=jax.ShapeDtypeStruct((8, n), F32),
    )(c_all, w_shard, b_shard)


def _ada_bwd(c_all, dada_shard):
    n = dada_shard.shape[1]
    tn = 512

    def body(c_ref, d_ref, o_ref):
        cv = c_ref[...]
        ca = (cv * _sigmoid(cv)).astype(MXU)
        o_ref[...] = _dot_tn(ca, d_ref[...].astype(MXU))

    return pl.pallas_call(
        body, name="ada_bwd", grid=(n // tn,),
        in_specs=[_fixed(8, D), pl.BlockSpec((8, tn), lambda j: (0, j))],
        out_specs=pl.BlockSpec((D, tn), lambda j: (0, j)),
        out_shape=jax.ShapeDtypeStruct((D, n), F32),
    )(c_all, dada_shard)


_SMALL = [("d_ada", N_COND * D), ("ln1_g", D), ("ln1_b", D), ("ln2_g", D), ("ln2_b", D), ("b_gate", 2 * D), ("b_forget", LANE),
          ("loss", D)]
_SMALL_OFF = {}
_o = 0
for _n, _w in _SMALL:
    _SMALL_OFF[_n] = (_o, _w)
    _o += _w
_SMALL_LEN = _o
_SMALL_PARAMS = [("b_ada", "d_ada", N_COND * D), ("b_gate", "b_gate", 2 * D), ("b_forget", "b_forget", N_FGATE),
                 ("ln1_g", "ln1_g", D), ("ln1_b", "ln1_b", D), ("ln2_g", "ln2_g", D), ("ln2_b", "ln2_b", D)]


def _small_update(rows, params):
    npar = len(_SMALL_PARAMS)

    def body(*refs):
        rows_ref = refs[0]
        p_refs = refs[1:1 + 3 * npar]
        loss_ref = refs[1 + 3 * npar]
        o_refs = refs[2 + 3 * npar:]
        total = rows_ref[0]
        for d in range(1, 8):
            total = total + rows_ref[d]
        lo, lw = _SMALL_OFF["loss"]
        loss_ref[...] = jnp.sum(total[:, lo:lo + lw], axis=1, keepdims=True)
        for j, (_, key, n) in enumerate(_SMALL_PARAMS):
            off = _SMALL_OFF[key][0]
            g = total[:, off:off + n]
            w_ref, m_ref, v_ref = p_refs[3 * j:3 * j + 3]
            o_refs[4 * j][...] = g
            o_refs[4 * j + 1][...], o_refs[4 * j + 2][...], o_refs[4 * j + 3][...] = _adamw_math(w_ref[...], g, m_ref[...], v_ref[...])

    flat = [a for p in params for a in p]
    out_shape = [jax.ShapeDtypeStruct((1, 1), F32)] + [jax.ShapeDtypeStruct((1, n), F32) for _, _, n in _SMALL_PARAMS for _ in range(4)]
    return pl.pallas_call(body, name="small_update", out_shape=out_shape)(rows, *flat)


_BIG = [("w_in", "cols_t"), ("w_sb_out", "cols"), ("w_fox_out", "cols"), ("w_o", "rows"),
        ("w_ffn_gate", "cols_t"), ("w_ffn_up", "cols_t"), ("w_ffn_down", "rows")]


def _shard2d(a, how):
    return a[0].T if how == "cols_t" else a[0]


def _unshard(g, how):
    if how == "cols":
        return g.transpose(1, 0, 2).reshape(g.shape[1], 4 * g.shape[2])
    return g.reshape(4 * g.shape[1], g.shape[2])


def _reshard(w, how):
    if how == "cols":
        return w.reshape(w.shape[0], 4, w.shape[1] // 4).transpose(1, 0, 2)
    return w.reshape(4, w.shape[0] // 4, w.shape[1])


def kernel(x, c, w_ada, b_ada, w_in, b_gate, b_forget, w_sb_out, w_fox_out, w_o, ln1_g, ln1_b, w_ffn_gate, w_ffn_up, w_ffn_down, ln2_g, ln2_b, loss_target, m_w_ada, m_b_ada, m_w_in, m_b_gate, m_b_forget, m_w_sb_out, m_w_fox_out, m_w_o, m_ln1_g, m_ln1_b, m_w_ffn_gate, m_w_ffn_up, m_w_ffn_down, m_ln2_g, m_ln2_b, v_w_ada, v_b_ada, v_w_in, v_b_gate, v_b_forget, v_w_sb_out, v_w_fox_out, v_w_o, v_ln1_g, v_ln1_b, v_w_ffn_gate, v_w_ffn_up, v_w_ffn_down, v_ln2_g, v_ln2_b):
    given = dict(locals())
    mx, my, mc = _mesh_pos()
    chip = 2 * mx + my
    seq = 4 * mx + 2 * my + mc

    c_all = _allgather_rows(c, "gather_c").reshape(8, D)
    n_ada = w_ada.shape[2]
    b_ada_shard = lax.dynamic_slice(b_ada, (0, chip * n_ada), (1, n_ada))
    ada_part = _ada_fwd(c_all, w_ada[0], b_ada_shard)
    ada_all = _allgather_rows(ada_part.reshape(1, 8 * n_ada), "gather_ada").reshape(4, 2, 8, n_ada)
    ada_row = lax.dynamic_slice(ada_all, (0, mc, seq, 0), (4, 1, 1, n_ada)).reshape(1, N_COND * D)
    ada = [ada_row[:, j * D:(j + 1) * D] for j in range(N_COND)]

    w_in_g = _gather_two_level(_shard2d(w_in, "cols_t").astype(MXU), "gather_w_in")
    wi = _unshard(w_in_g, "cols_t")
    w_all = jnp.concatenate([wi[:OFF_FGATE + N_FGATE], jnp.zeros((LANE - N_FGATE, D), MXU), wi[OFF_FGATE + N_FGATE:]], axis=0)
    bf_pad = jnp.concatenate([b_forget, jnp.zeros((1, LANE - N_FGATE), F32)], axis=1)
    late = _BIG[1:]
    late_riders = [_shard2d(given[n], how).astype(MXU) for n, how in late]
    pieces = {}

    def late_full(gathered):
        return [_unshard(g, how) for (_, how), g in zip(late, gathered)]

    def early_grads(dw):
        for n, how in late:
            pieces[n] = _reshard(dw[n][0], how)
        return [_reshard(dw[n][1], how) for n, how in late]

    def w_in_grads(dwi):
        pieces["w_in"] = _reshard(jnp.concatenate(dwi[:6] + [dwi[6][:N_FGATE], dwi[7]], axis=0), "cols_t")
        return [pieces["w_in"].astype(MXU)]

    out = _local_step(x[0], loss_target[0], ada, w_all, b_gate, bf_pad, (late_riders, late_full), early_grads, w_in_grads,
                      ln1_g, ln1_b, ln2_g, ln2_b)

    row = jnp.concatenate(out["d_ada"] + [out["dln1_g"], out["dln1_b"], out["dln2_g"], out["dln2_b"], out["db_gate"],
                                          out["db_forget"], out["loss_lanes"]], axis=1)
    rows = _allgather_rows(row, "gather_small")
    small = _small_update(rows, [(given[p], given["m_" + p], given["v_" + p]) for p, _, _ in _SMALL_PARAMS])
    loss = small[0].reshape(())
    res = {}
    for j, (p, _, _) in enumerate(_SMALL_PARAMS):
        res[p] = small[1 + 4 * j:5 + 4 * j]

    dada_all = rows.reshape(8, _SMALL_LEN)[:, :N_COND * D]
    dada_shard = lax.dynamic_slice(dada_all, (0, chip * n_ada), (8, n_ada))
    g_ada = _ada_bwd(c_all, dada_shard)
    res["w_ada"] = [a[None] for a in _adamw(w_ada[0], m_w_ada[0], v_w_ada[0], [g_ada], "adamw_w_ada")]

    received = dict(zip([n for n, _ in late], out["early_received"]))
    (received["w_in"],) = out["w_in_received"]
    chip_arr = jnp.reshape(chip, (1,)).astype(jnp.int32)
    partial = [_reduce_chips(chip_arr, pieces[n], received[n], "reduce_" + n) for n, _ in _BIG]
    theirs = _sibling_exchange(partial, "swap_cores")
    for (n, how), mine, other in zip(_BIG, partial, theirs):
        upd = _adamw(_shard2d(given[n], how), _shard2d(given["m_" + n], how), _shard2d(given["v_" + n], how), [mine, other], "adamw_" + n)
        res[n] = [(a.T if how == "cols_t" else a)[None] for a in upd]

    order = ["w_ada", "b_ada", "w_in", "b_gate", "b_forget", "w_sb_out", "w_fox_out", "w_o", "ln1_g", "ln1_b",
             "w_ffn_gate", "w_ffn_up", "w_ffn_down", "ln2_g", "ln2_b"]
    return (loss, out["grad_x"][None], *[res[n][0] for n in order], *[res[n][1] for n in order],
            *[res[n][2] for n in order], *[res[n][3] for n in order])
```

```python
import functools

import jax
import jax.numpy as jnp
from jax import lax
from jax.experimental import pallas as pl
from jax.experimental.pallas import tpu as pltpu

F32 = jnp.float32
MXU = jnp.bfloat16

D = 1024
HEAD_DIM = 64
WIDTH = 512
D_FF = 2816
N_COND = 6
LN_EPS = 1e-5
ALPHA = 2.0 ** 0.25
QK_SCALE = HEAD_DIM ** -0.5
OFF_FGATE = 6 * WIDTH
N_FGATE = 8
IN_COLS = OFF_FGATE + N_FGATE + 2 * D
LANE = 128
W_ALL_COLS = OFF_FGATE + LANE + 2 * D
TQ = 512
ADAM_LR, ADAM_B1, ADAM_B2, ADAM_EPS, ADAM_WD, ADAM_STEP = 0.001, 0.9, 0.999, 1e-08, 0.01, 10
NEG = -1e30
DEAD_LOG = -120.0
RS_COUNT_LANE = LANE - 1
MESH_AXES = ("x", "y", "c")
VMEM_BIG = 56 * 1024 * 1024


def _dot(a, b):
    return jnp.dot(a, b, preferred_element_type=F32)


def _dot_nt(a, b):
    return lax.dot_general(a, b, (((1,), (1,)), ((), ())), preferred_element_type=F32)


def _dot_tn(a, b):
    return lax.dot_general(a, b, (((0,), (0,)), ((), ())), preferred_element_type=F32)


def _ln(x):
    mu = jnp.mean(x, axis=-1, keepdims=True)
    xc = x - mu
    var = jnp.mean(xc * xc, axis=-1, keepdims=True)
    rstd = lax.rsqrt(var + LN_EPS)
    return xc * rstd, rstd


def _ln_bwd(dxhat, xhat, rstd):
    return rstd * (dxhat - jnp.mean(dxhat, axis=-1, keepdims=True) - xhat * jnp.mean(dxhat * xhat, axis=-1, keepdims=True))


def _sigmoid(x):
    return 1.0 / (1.0 + jnp.exp(-x))


def _colsum(x):
    return jnp.sum(x, axis=0, keepdims=True)


def _split(x):
    hi = x.astype(MXU)
    lo = (x - hi.astype(F32)).astype(MXU)
    return jnp.concatenate([hi, lo], axis=1)


def _rows(tm, n):
    return pl.BlockSpec((tm, n), lambda i: (i, 0))


def _fixed(r, n):
    return pl.BlockSpec((r, n), lambda i: (0, 0))


def _res(a):
    return pl.BlockSpec(a.shape, lambda i: (0, 0), pipeline_mode=pl.Buffered(1))


def _params(limit=None, sem=None):
    return pltpu.CompilerParams(vmem_limit_bytes=limit, dimension_semantics=sem)


def _in_proj(x, sh1, sc1, w_all, b_gate):
    s = x.shape[0]
    tm = 256

    def body(x_ref, sh_ref, sc_ref, w_ref, bg_ref, u_ref, qkv_ref, fl_ref, gl_ref):
        xhat, _ = _ln(x_ref[...])
        u = (xhat * (1.0 + sc_ref[...]) + sh_ref[...]).astype(MXU)
        u_ref[...] = u
        for c0 in range(0, OFF_FGATE, WIDTH):
            p = _dot_nt(u, w_ref[c0:c0 + WIDTH, :])
            if c0 in (0, 3 * WIDTH):
                p = p * QK_SCALE
            qkv_ref[:, c0:c0 + WIDTH] = p.astype(MXU)
        fl_ref[...] = _dot_nt(u, w_ref[OFF_FGATE:OFF_FGATE + LANE, :])
        for c0 in range(0, 2 * D, D):
            gl_ref[:, c0:c0 + D] = _dot_nt(u, w_ref[OFF_FGATE + LANE + c0:OFF_FGATE + LANE + c0 + D, :]) + bg_ref[:, c0:c0 + D]

    return pl.pallas_call(
        body, name="in_proj", grid=(s // tm,),
        in_specs=[_rows(tm, D), _fixed(1, D), _fixed(1, D), _res(w_all), _fixed(1, 2 * D)],
        out_specs=[_rows(tm, D), _rows(tm, OFF_FGATE), _rows(tm, LANE), _rows(tm, 2 * D)],
        out_shape=[jax.ShapeDtypeStruct((s, D), MXU), jax.ShapeDtypeStruct((s, OFF_FGATE), MXU),
                   jax.ShapeDtypeStruct((s, LANE), F32), jax.ShapeDtypeStruct((s, 2 * D), F32)],
        compiler_params=_params(VMEM_BIG),
    )(x, sh1, sc1, w_all, b_gate)


def _log_sigmoid_parts(z):
    e = jnp.exp(-jnp.abs(z))
    return -(jnp.maximum(z, 0.0) + jnp.log(1.0 + e)), e


def _fcum_fwd(fl, bf):
    s = fl.shape[0]
    nb = s // LANE

    def body(fl_ref, bf_ref, fc_ref, fkt_ref):
        r = lax.broadcasted_iota(jnp.int32, (LANE, LANE), 0)
        c = lax.broadcasted_iota(jnp.int32, (LANE, LANE), 1)
        tri = (c <= r).astype(F32)

        def step(b, carry):
            r0 = pl.multiple_of(b * LANE, LANE)
            xb = fl_ref[pl.ds(r0, LANE), :] + bf_ref[...]
            ls = _log_sigmoid_parts(-xb)[0]
            cs = jnp.dot(tri, ls, precision=lax.Precision.HIGHEST, preferred_element_type=F32) + carry
            fc_ref[pl.ds(r0, LANE), :] = cs
            fkt_ref[b] = cs.T[:N_FGATE, :]
            return cs[LANE - 1:LANE, :]

        lax.fori_loop(0, nb, step, jnp.zeros((1, LANE), F32))

    return pl.pallas_call(
        body, name="fcum_fwd",
        out_shape=[jax.ShapeDtypeStruct((s, LANE), F32), jax.ShapeDtypeStruct((nb, N_FGATE, LANE), F32)],
    )(fl, bf)


def _attn_specs(s, col0):
    return [pl.BlockSpec((TQ, LANE), lambda hp, i: (i, col0 + hp)),
            pl.BlockSpec((s, LANE), lambda hp, i: (0, col0 + 4 + hp)),
            pl.BlockSpec((s, LANE), lambda hp, i: (0, col0 + 8 + hp))]


def _tile_iotas():
    lane = lax.broadcasted_iota(jnp.int32, (TQ, LANE), 1)
    row = lax.broadcasted_iota(jnp.int32, (TQ, TQ), 0)
    col = lax.broadcasted_iota(jnp.int32, (TQ, TQ), 1)
    return lane, row, col


def _sub_blocks(nk=None):
    return [slice(j * LANE, (j + 1) * LANE) for j in range((TQ if nk is None else nk) // LANE)]


def _over_strips(tile, diagonal):
    del diagonal
    return tile(slice(0, TQ), TQ)


def _tri(below):
    r = lax.broadcasted_iota(jnp.int32, (LANE, LANE), 0)
    c = lax.broadcasted_iota(jnp.int32, (LANE, LANE), 1)
    t = jnp.concatenate([((r > c) if below else (r < c)).astype(MXU), jnp.ones((LANE, LANE), MXU)], axis=1)
    return jnp.concatenate([t, t], axis=0)


def _call_with_riders(body, name, grid, in_specs, out_specs, out_shape, scratch, args, riders, gather, limit=None):
    nr, n_in, n_out, n_sc = len(riders), len(in_specs), len(out_specs), len(scratch)

    def at_step(which):
        hit = None
        for d, n in enumerate(grid):
            here = pl.program_id(d) == (0 if which == "first" else n - 1)
            hit = here if hit is None else hit & here
        return hit

    def wrapped(*refs):
        ins, rin = refs[:n_in], refs[n_in:n_in + nr]
        outs, rout = refs[n_in + nr:n_in + nr + n_out], refs[n_in + nr + n_out:n_in + 2 * nr + n_out]
        own, sems = refs[n_in + 2 * nr + n_out:n_in + 2 * nr + n_out + n_sc], refs[n_in + 2 * nr + n_out + n_sc:]
        if nr:
            @pl.when(at_step("first"))
            def _():
                _exchange_start(rin, rout, sems, gather)

        body(*ins, *outs, *own)
        if nr:
            @pl.when(at_step("last"))
            def _():
                _exchange_wait(rin, rout, sems, gather)

    res = pl.pallas_call(
        wrapped, name=name, grid=grid,
        in_specs=list(in_specs) + [_ANY] * nr, out_specs=list(out_specs) + [_ANY] * nr,
        out_shape=list(out_shape) + _exchange_out_shape(riders),
        scratch_shapes=list(scratch) + (_exchange_sems(nr) if nr else []),
        compiler_params=_params(limit),
    )(*args, *riders)
    return res[:n_out], res[n_out:]


def _sb_fwd(qkv, riders=()):
    s = qkv.shape[0]
    nq = s // TQ
    assert nq <= RS_COUNT_LANE

    def body(q_ref, k_ref, v_ref, o_ref, rs_ref):
        i = pl.program_id(1)
        lane, row, col = _tile_iotas()
        u2 = _tri(True)
        diag = col < row
        q = q_ref[...]
        qms = [jnp.where(hm, q, jnp.zeros_like(q)) for hm in (lane < HEAD_DIM, lane >= HEAD_DIM)]

        def step(kb, carry, masked):
            k0 = pl.multiple_of(kb * TQ, TQ)
            k = k_ref[pl.ds(k0, TQ), :]
            v = v_ref[pl.ds(k0, TQ), :]
            def tile(rows, nk, qm, state):
                run, acc, rt = (t[rows] for t in state)
                z = _dot_nt(qm[rows], k[:nk])
                lneg, _ = _log_sigmoid_parts(z)
                lpos = z + lneg
                if masked:
                    lneg = jnp.where(diag[rows, :nk], lneg, 0.0)
                rt = jnp.where(lane[rows] == kb, run, rt)
                a = []
                for sl in reversed(_sub_blocks(nk)):
                    st = _dot(_split(lneg[:, sl]), u2)
                    a.append(jnp.exp(lpos[:, sl] + st[:, :LANE] + run))
                    run = run + st[:, LANE:]
                a = jnp.concatenate(a[::-1], axis=1)
                if masked:
                    a = jnp.where(diag[rows, :nk], a, 0.0)
                return run, acc + _dot(a.astype(MXU), v[:nk]), rt

            return tuple(_over_strips(functools.partial(tile, qm=qm, state=state), masked) for qm, state in zip(qms, carry))

        zero = jnp.zeros((TQ, LANE), F32)
        carry = step(i, ((zero, zero, zero),) * 2, True)

        def alive(cr):
            return jnp.maximum(jnp.max(cr[0][0]), jnp.max(cr[1][0])) > DEAD_LOG

        def walk(state):
            j, _, cr = state
            cr = step(i - 1 - j, cr, False)
            return j + 1, alive(cr), cr

        walked, _, carry = lax.while_loop(lambda state: (state[0] < i) & state[1], walk, (jnp.int32(0), alive(carry), carry))
        count = walked.astype(F32)
        rs_ref[0] = jnp.where(lane == RS_COUNT_LANE, count, carry[0][2])
        rs_ref[1] = jnp.where(lane == RS_COUNT_LANE, count, carry[1][2])
        o_ref[...] = jnp.where(lane < HEAD_DIM, carry[0][1], carry[1][1]).astype(o_ref.dtype)

    return _call_with_riders(
        body, "sb_fwd", (4, nq), _attn_specs(s, 0),
        [pl.BlockSpec((TQ, LANE), lambda hp, i: (i, hp)), pl.BlockSpec((2, TQ, LANE), lambda hp, i: (hp, i, 0))],
        [jax.ShapeDtypeStruct((s, WIDTH), MXU), jax.ShapeDtypeStruct((8, s, LANE), F32)], [], (qkv, qkv, qkv), riders, True)


def _sb_bwd(qkv, do, rs, riders=()):
    s = qkv.shape[0]
    nq = s // TQ

    def body(q_ref, k_ref, v_ref, do_ref, rs_ref, dq_ref, dk_ref, dv_ref, dk_acc, dv_acc):
        i = pl.program_id(1)

        @pl.when(i == 0)
        def _():
            dk_acc[...] = jnp.zeros_like(dk_acc)
            dv_acc[...] = jnp.zeros_like(dv_acc)

        lane, row, col = _tile_iotas()
        u2 = _tri(True)
        l2 = _tri(False)
        diag = col < row
        q = q_ref[...]
        do = do_ref[...]
        heads = [(jnp.where(hm, q, jnp.zeros_like(q)), jnp.where(hm, do, jnp.zeros_like(do)), rs_ref[hh])
                 for hh, hm in enumerate((lane < HEAD_DIM, lane >= HEAD_DIM))]

        def step(kb, carry, masked):
            k0 = pl.multiple_of(kb * TQ, TQ)
            k = k_ref[pl.ds(k0, TQ), :]
            v = v_ref[pl.ds(k0, TQ), :]
            to_keys = {}

            def tile(rows, nk, qm, dom, rblk, state):
                gpre, dq = (t[rows] for t in state)
                z = _dot_nt(qm[rows], k[:nk])
                lneg, e = _log_sigmoid_parts(z)
                lpos = z + lneg
                if masked:
                    lneg = jnp.where(diag[rows, :nk], lneg, 0.0)
                run = jnp.sum(jnp.where(lane[rows] == kb, rblk[rows], 0.0), axis=1, keepdims=True) + jnp.zeros_like(gpre)
                a = []
                for sl in reversed(_sub_blocks(nk)):
                    st = _dot(_split(lneg[:, sl]), u2)
                    a.append(jnp.exp(lpos[:, sl] + st[:, :LANE] + run))
                    run = run + st[:, LANE:]
                a = jnp.concatenate(a[::-1], axis=1)
                if masked:
                    a = jnp.where(diag[rows, :nk], a, 0.0)
                g = a * _dot_nt(dom[rows], v[:nk])
                pre = []
                for sl in _sub_blocks(nk):
                    pt = _dot(_split(g[:, sl]), l2)
                    pre.append(gpre + pt[:, :LANE])
                    gpre = gpre + pt[:, LANE:]
                sig = jnp.where(z >= 0.0, 1.0, e) / (1.0 + e)
                dz = g - (g + jnp.concatenate(pre, axis=1)) * sig
                if masked:
                    dz = jnp.where(diag[rows, :nk], dz, 0.0)
                dzb = dz.astype(MXU)
                both = to_keys.setdefault(nk, [0.0, 0.0])
                both[0] = both[0] + _dot_tn(dzb, qm[rows])
                both[1] = both[1] + _dot_tn(a.astype(MXU), dom[rows])
                return gpre, dq + _dot(dzb, k[:nk])

            new = tuple(_over_strips(functools.partial(tile, qm=qm, dom=dom, rblk=rblk, state=state), masked)
                        for (qm, dom, rblk), state in zip(heads, carry))
            for nk, (dk, dv) in to_keys.items():
                dk_acc[pl.ds(k0, nk), :] += dk
                dv_acc[pl.ds(k0, nk), :] += dv
            return new

        walked = jnp.max(jnp.where(lane[:8] == RS_COUNT_LANE, rs_ref[0, 0:8, :], 0.0))
        first = i - jnp.clip(walked.astype(jnp.int32), 0, i)
        zero = jnp.zeros((TQ, LANE), F32)
        carry = step(i, lax.fori_loop(first, i, lambda kb, cr: step(kb, cr, False), ((zero, zero),) * 2), True)
        dq_ref[...] = (jnp.where(lane < HEAD_DIM, carry[0][1], carry[1][1]) * QK_SCALE).astype(dq_ref.dtype)

        @pl.when(i == nq - 1)
        def _():
            dk_ref[...] = dk_acc[...].astype(dk_ref.dtype)
            dv_ref[...] = dv_acc[...].astype(dv_ref.dtype)

    blk = pl.BlockSpec((TQ, LANE), lambda hp, i: (i, hp))
    whole = pl.BlockSpec((s, LANE), lambda hp, i: (0, hp))
    return _call_with_riders(
        body, "sb_bwd", (4, nq), _attn_specs(s, 0) + [blk, pl.BlockSpec((2, TQ, LANE), lambda hp, i: (hp, i, 0))],
        [blk, whole, whole], [jax.ShapeDtypeStruct((s, WIDTH), MXU)] * 3,
        [pltpu.VMEM((s, LANE), F32), pltpu.VMEM((s, LANE), F32)], (qkv, qkv, qkv, do, rs), riders, False)


def _key_bias(fkt_ref, kb, h):
    n_sub = TQ // LANE
    return jnp.concatenate([fkt_ref[kb * n_sub + j, pl.ds(h, 1), :] for j in range(n_sub)], axis=1)


def _fox_fwd(qkv, fc, fkt, riders=()):
    s = qkv.shape[0]
    nq = s // TQ
    nb = fkt.shape[0]

    def body(q_ref, k_ref, v_ref, fq_ref, fkt_ref, o_ref, lse_ref):
        hp = pl.program_id(0)
        i = pl.program_id(1)
        lane, row, col = _tile_iotas()
        diag = col <= row
        q = q_ref[...]
        fqb = fq_ref[...]
        heads = []
        for hh in range(2):
            h = 2 * hp + hh
            hm = (lane >= HEAD_DIM) if hh else (lane < HEAD_DIM)
            heads.append((h, jnp.where(hm, q, jnp.zeros_like(q)), jnp.sum(jnp.where(lane == h, fqb, 0.0), axis=1, keepdims=True)))

        def step(kb, carry, masked):
            k0 = pl.multiple_of(kb * TQ, TQ)
            k = k_ref[pl.ds(k0, TQ), :]
            v = v_ref[pl.ds(k0, TQ), :]
            def tile(rows, nk, h, qm, fq, state):
                m, l, acc = (t[rows] for t in state)
                z = _dot_nt(qm[rows], k[:nk]) + fq[rows] - _key_bias(fkt_ref, kb, h)[:, :nk]
                if masked:
                    z = jnp.where(diag[rows, :nk], z, NEG)
                mn = jnp.maximum(m, jnp.max(z, axis=1, keepdims=True))
                p = jnp.exp(z - mn)
                alpha = jnp.exp(m - mn)
                return mn, alpha * l + jnp.sum(p, axis=1, keepdims=True), alpha * acc + _dot(p.astype(MXU), v[:nk])

            return tuple(_over_strips(functools.partial(tile, h=h, qm=qm, fq=fq, state=state), masked)
                         for (h, qm, fq), state in zip(heads, carry))

        init = ((jnp.full((TQ, 1), NEG, F32), jnp.zeros((TQ, 1), F32), jnp.zeros((TQ, LANE), F32)),) * 2
        carry = step(i, lax.fori_loop(0, i, lambda kb, cr: step(kb, cr, False), init), True)
        outs = []
        for hh, (m, l, acc) in enumerate(carry):
            outs.append(acc / l)
            lse_ref[hh] = jnp.broadcast_to(m + jnp.log(l), (TQ, LANE))
        o_ref[...] = jnp.where(lane < HEAD_DIM, outs[0], outs[1]).astype(o_ref.dtype)

    return _call_with_riders(
        body, "fox_fwd", (4, nq),
        _attn_specs(s, 12) + [pl.BlockSpec((TQ, LANE), lambda hp, i: (i, 0)), pl.BlockSpec((nb, N_FGATE, LANE), lambda hp, i: (0, 0, 0))],
        [pl.BlockSpec((TQ, LANE), lambda hp, i: (i, hp)), pl.BlockSpec((2, TQ, LANE), lambda hp, i: (hp, i, 0))],
        [jax.ShapeDtypeStruct((s, WIDTH), MXU), jax.ShapeDtypeStruct((8, s, LANE), F32)], [], (qkv, qkv, qkv, fc, fkt), riders, True)


def _fox_bwd(qkv, fc, fkt, do, o, lse, riders=()):
    s = qkv.shape[0]
    nq = s // TQ
    nb = fkt.shape[0]

    def body(q_ref, k_ref, v_ref, fq_ref, fkt_ref, do_ref, o_ref, lse_ref, dq_ref, dk_ref, dv_ref, dfk_ref, dfq_ref, dk_acc, dv_acc):
        hp = pl.program_id(0)
        i = pl.program_id(1)

        @pl.when(i == 0)
        def _():
            dk_acc[...] = jnp.zeros_like(dk_acc)
            dv_acc[...] = jnp.zeros_like(dv_acc)

        @pl.when((i == 0) & (hp == 0))
        def _():
            dfk_ref[...] = jnp.zeros_like(dfk_ref)

        lane, row, col = _tile_iotas()
        diag = col <= row
        q = q_ref[...]
        do = do_ref[...]
        dof = do.astype(F32) * o_ref[...].astype(F32)
        fqb = fq_ref[...]
        heads = []
        for hh in range(2):
            h = 2 * hp + hh
            hm = (lane >= HEAD_DIM) if hh else (lane < HEAD_DIM)
            heads.append((h, jnp.where(hm, q, jnp.zeros_like(q)), jnp.where(hm, do, jnp.zeros_like(do)),
                          jnp.sum(jnp.where(hm, dof, 0.0), axis=1, keepdims=True),
                          jnp.sum(jnp.where(lane == h, fqb, 0.0), axis=1, keepdims=True), lse_ref[hh][:, :1]))

        def step(kb, carry, masked):
            k0 = pl.multiple_of(kb * TQ, TQ)
            k = k_ref[pl.ds(k0, TQ), :]
            v = v_ref[pl.ds(k0, TQ), :]
            to_keys = {}

            def tile(rows, nk, h, qm, dom, delta, fq, lse_t, state):
                dq, rsum = (t[rows] for t in state)
                z = _dot_nt(qm[rows], k[:nk]) + fq[rows] - _key_bias(fkt_ref, kb, h)[:, :nk]
                if masked:
                    z = jnp.where(diag[rows, :nk], z, NEG)
                p = jnp.exp(z - lse_t[rows])
                ds = p * (_dot_nt(dom[rows], v[:nk]) - delta[rows])
                dsb = ds.astype(MXU)
                both = to_keys.setdefault(nk, [0.0, 0.0])
                both[0] = both[0] + _dot_tn(dsb, qm[rows])
                both[1] = both[1] + _dot_tn(p.astype(MXU), dom[rows])
                csum = _colsum(ds)
                for j, sl in enumerate(_sub_blocks(nk)):
                    dfk_ref[kb * len(_sub_blocks()) + j, pl.ds(h, 1), :] += -csum[:, sl]
                return dq + _dot(dsb, k[:nk]), rsum + jnp.sum(ds, axis=1, keepdims=True)

            new = tuple(_over_strips(functools.partial(tile, h=h, qm=qm, dom=dom, delta=delta, fq=fq, lse_t=lse_t, state=state), masked)
                        for (h, qm, dom, delta, fq, lse_t), state in zip(heads, carry))
            for nk, (dk, dv) in to_keys.items():
                dk_acc[pl.ds(k0, nk), :] += dk
                dv_acc[pl.ds(k0, nk), :] += dv
            return new

        init = ((jnp.zeros((TQ, LANE), F32), jnp.zeros((TQ, 1), F32)),) * 2
        carry = step(i, lax.fori_loop(0, i, lambda kb, cr: step(kb, cr, False), init), True)
        dq_ref[...] = (jnp.where(lane < HEAD_DIM, carry[0][0], carry[1][0]) * QK_SCALE).astype(dq_ref.dtype)
        dfq_ref[0] = jnp.where(lane == heads[0][0], carry[0][1], jnp.where(lane == heads[1][0], carry[1][1], 0.0))

        @pl.when(i == nq - 1)
        def _():
            dk_ref[...] = dk_acc[...].astype(dk_ref.dtype)
            dv_ref[...] = dv_acc[...].astype(dv_ref.dtype)

    blk = pl.BlockSpec((TQ, LANE), lambda hp, i: (i, hp))
    whole = pl.BlockSpec((s, LANE), lambda hp, i: (0, hp))
    pair = pl.BlockSpec((2, TQ, LANE), lambda hp, i: (hp, i, 0))
    fkt_spec = pl.BlockSpec((nb, N_FGATE, LANE), lambda hp, i: (0, 0, 0))
    return _call_with_riders(
        body, "fox_bwd", (4, nq),
        _attn_specs(s, 12) + [pl.BlockSpec((TQ, LANE), lambda hp, i: (i, 0)), fkt_spec, blk, blk, pair],
        [blk, whole, whole, fkt_spec, pl.BlockSpec((1, TQ, LANE), lambda hp, i: (hp, i, 0))],
        [jax.ShapeDtypeStruct((s, WIDTH), MXU)] * 3
        + [jax.ShapeDtypeStruct((nb, N_FGATE, LANE), F32), jax.ShapeDtypeStruct((4, s, LANE), F32)],
        [pltpu.VMEM((s, LANE), F32), pltpu.VMEM((s, LANE), F32)], (qkv, qkv, qkv, fc, fkt, do, o, lse), riders, False)


def _fcum_bwd(dfkt, dfq, fl, bf):
    s = fl.shape[0]
    nb = s // LANE

    def body(dfkt_ref, dfq_ref, fl_ref, bf_ref, df_ref, dbf_ref, tail_ref):
        @pl.when(pl.program_id(0) == 0)
        def _():
            tail_ref[...] = jnp.zeros_like(tail_ref)
            dbf_ref[...] = jnp.zeros_like(dbf_ref)

        r = lax.broadcasted_iota(jnp.int32, (LANE, LANE), 0)
        c = lax.broadcasted_iota(jnp.int32, (LANE, LANE), 1)
        tri = (c >= r).astype(F32)
        dfc = jnp.concatenate([dfkt_ref[0], jnp.zeros((LANE - N_FGATE, LANE), F32)], axis=0).T
        dfc = dfc + ((dfq_ref[0] + dfq_ref[1]) + (dfq_ref[2] + dfq_ref[3]))
        dls = jnp.dot(tri, dfc, precision=lax.Precision.HIGHEST, preferred_element_type=F32) + tail_ref[...]
        xb = fl_ref[...] + bf_ref[...]
        e = jnp.exp(-jnp.abs(xb))
        dfl = dls * (jnp.where(xb >= 0.0, e, 1.0) / (1.0 + e))
        df_ref[...] = dfl.astype(df_ref.dtype)
        tail_ref[...] = dls[0:1, :]
        dbf_ref[...] += _colsum(dfl)

    return pl.pallas_call(
        body, name="fcum_bwd", grid=(nb,),
        in_specs=[pl.BlockSpec((1, N_FGATE, LANE), lambda j: (nb - 1 - j, 0, 0)), pl.BlockSpec((4, LANE, LANE), lambda j: (0, nb - 1 - j, 0)),
                  pl.BlockSpec((LANE, LANE), lambda j: (nb - 1 - j, 0)), _fixed(1, LANE)],
        out_specs=[pl.BlockSpec((LANE, LANE), lambda j: (nb - 1 - j, 0)), _fixed(1, LANE)],
        out_shape=[jax.ShapeDtypeStruct((s, LANE), MXU), jax.ShapeDtypeStruct((1, LANE), F32)],
        scratch_shapes=[pltpu.VMEM((1, LANE), F32)],
    )(dfkt, dfq, fl, bf)


def _mix_fwd(x, o_sb, o_fx, gl, w_sb, w_fx, w_o, g1, ln1_g, ln1_b, sh2, sc2):
    s = x.shape[0]
    tm = 256

    def body(x_ref, osb_ref, ofx_ref, gl_ref, wsb_ref, wfx_ref, wo_ref, g1_ref, lg_ref, lb_ref, sh_ref, sc_ref, r1_ref, u2_ref):
        mixin = (_sigmoid(gl_ref[:, :D]) * _dot(osb_ref[...], wsb_ref[...])
                 + _sigmoid(gl_ref[:, D:]) * _dot(ofx_ref[...], wfx_ref[...]))
        r1 = ALPHA * x_ref[...] + g1_ref[...] * _dot(mixin.astype(MXU), wo_ref[...])
        r1_ref[...] = r1
        x1 = _ln(r1)[0] * lg_ref[...] + lb_ref[...]
        u2_ref[...] = (_ln(x1)[0] * (1.0 + sc_ref[...]) + sh_ref[...]).astype(MXU)

    vec = _fixed(1, D)
    return pl.pallas_call(
        body, name="mix_fwd", grid=(s // tm,),
        in_specs=[_rows(tm, D), _rows(tm, WIDTH), _rows(tm, WIDTH), _rows(tm, 2 * D), _res(w_sb), _res(w_fx), _res(w_o),
                  vec, vec, vec, vec, vec],
        out_specs=[_rows(tm, D), _rows(tm, D)],
        out_shape=[jax.ShapeDtypeStruct((s, D), F32), jax.ShapeDtypeStruct((s, D), MXU)],
        compiler_params=_params(VMEM_BIG),
    )(x, o_sb, o_fx, gl, w_sb, w_fx, w_o, g1, ln1_g, ln1_b, sh2, sc2)


def _ffn_fwd(r1, u2, tgt, w_g, w_u, w_d, g2, ln1_g, ln1_b, ln2_g, ln2_b):
    s = r1.shape[0]
    tm = 256

    def body(r1_ref, u2_ref, t_ref, wg_ref, wu_ref, wd_ref, g2_ref, l1g_ref, l1b_ref, l2g_ref, l2b_ref,
             hg_ref, hu_ref, dxa_ref, dh_ref, acc_ref):
        @pl.when(pl.program_id(0) == 0)
        def _():
            acc_ref[...] = jnp.zeros_like(acc_ref)

        u2 = u2_ref[...]
        hg = _dot_nt(u2, wg_ref[...])
        hu = _dot_nt(u2, wu_ref[...])
        hg_ref[...] = hg
        hu_ref[...] = hu
        h = _dot((hg * _sigmoid(hg) * hu).astype(MXU), wd_ref[...])
        x1 = _ln(r1_ref[...])[0] * l1g_ref[...] + l1b_ref[...]
        xh2, rstd2 = _ln(ALPHA * x1 + g2_ref[...] * h)
        err = xh2 * l2g_ref[...] + l2b_ref[...] - t_ref[...]
        dy = err * (1.0 / D)
        dr2 = _ln_bwd(dy * l2g_ref[...], xh2, rstd2)
        dxa_ref[...] = ALPHA * dr2
        dh_ref[...] = (g2_ref[...] * dr2).astype(MXU)
        acc_ref[0:1, :] += _colsum(dr2 * h)
        acc_ref[1:2, :] += _colsum(dy * xh2)
        acc_ref[2:3, :] += _colsum(dy)
        acc_ref[3:4, :] += _colsum(err * err) * (0.5 / D)

    vec = _fixed(1, D)
    return pl.pallas_call(
        body, name="ffn_fwd", grid=(s // tm,),
        in_specs=[_rows(tm, D), _rows(tm, D), _rows(tm, D), _res(w_g), _res(w_u), _res(w_d), vec, vec, vec, vec, vec],
        out_specs=[_rows(tm, D_FF), _rows(tm, D_FF), _rows(tm, D), _rows(tm, D), _fixed(8, D)],
        out_shape=[jax.ShapeDtypeStruct((s, D_FF), F32), jax.ShapeDtypeStruct((s, D_FF), F32),
                   jax.ShapeDtypeStruct((s, D), F32), jax.ShapeDtypeStruct((s, D), MXU), jax.ShapeDtypeStruct((8, D), F32)],
        compiler_params=_params(VMEM_BIG),
    )(r1, u2, tgt, w_g, w_u, w_d, g2, ln1_g, ln1_b, ln2_g, ln2_b)


def _ffn_bwd(dh, hg, hu, w_g, w_u, w_d):
    s = dh.shape[0]
    tm = 256
    half = D_FF // 2

    def body(dh_ref, hg_ref, hu_ref, wg_ref, wu_ref, wd_ref, act_ref, dhg_ref, dhu_ref, du2_ref):
        dh = dh_ref[...]
        du2 = jnp.zeros((tm, D), F32)
        for c0 in (0, half):
            cols = slice(c0, c0 + half)
            dact = _dot_nt(dh, wd_ref[cols, :])
            hg = hg_ref[:, cols]
            hu = hu_ref[:, cols]
            sg = _sigmoid(hg)
            sl = hg * sg
            act_ref[:, cols] = (sl * hu).astype(MXU)
            dhg = (dact * hu * (sg * (1.0 + hg * (1.0 - sg)))).astype(MXU)
            dhu = (dact * sl).astype(MXU)
            dhg_ref[:, cols] = dhg
            dhu_ref[:, cols] = dhu
            du2 = du2 + _dot(dhg, wg_ref[cols, :]) + _dot(dhu, wu_ref[cols, :])
        du2_ref[...] = du2

    return pl.pallas_call(
        body, name="ffn_bwd", grid=(s // tm,),
        in_specs=[_rows(tm, D), _rows(tm, D_FF), _rows(tm, D_FF), _res(w_g), _res(w_u), _res(w_d)],
        out_specs=[_rows(tm, D_FF), _rows(tm, D_FF), _rows(tm, D_FF), _rows(tm, D)],
        out_shape=[jax.ShapeDtypeStruct((s, D_FF), MXU)] * 3 + [jax.ShapeDtypeStruct((s, D), F32)],
        compiler_params=_params(VMEM_BIG),
    )(dh, hg, hu, w_g, w_u, w_d)


def _mix_bwd(du2, dxa, r1, o_sb, o_fx, gl, w_sb, w_fx, w_o, g1, ln1_g, ln1_b, sc2):
    s = r1.shape[0]
    tm = 256

    def body(du2_ref, dxa_ref, r1_ref, osb_ref, ofx_ref, gl_ref, wsb_ref, wfx_ref, wo_ref, g1_ref, lg_ref, lb_ref, sc_ref,
             dx_ref, mixin_ref, dmix_ref, dysb_ref, dyfx_ref, dosb_ref, dofx_ref, dgl_ref, dbg_ref, acc_ref):
        @pl.when(pl.program_id(0) == 0)
        def _():
            acc_ref[...] = jnp.zeros_like(acc_ref)
            dbg_ref[...] = jnp.zeros_like(dbg_ref)

        du2 = du2_ref[...]
        xh1, rstd1 = _ln(r1_ref[...])
        x1 = xh1 * lg_ref[...] + lb_ref[...]
        n1, rstdn = _ln(x1)
        dx1 = dxa_ref[...] + _ln_bwd(du2 * (1.0 + sc_ref[...]), n1, rstdn)
        dr1 = _ln_bwd(dx1 * lg_ref[...], xh1, rstd1)
        dx_ref[...] = ALPHA * dr1
        ysb = _dot(osb_ref[...], wsb_ref[...])
        yfx = _dot(ofx_ref[...], wfx_ref[...])
        gs = _sigmoid(gl_ref[:, :D])
        gf = _sigmoid(gl_ref[:, D:])
        mixin = (gs * ysb + gf * yfx).astype(MXU)
        mixin_ref[...] = mixin
        mix = _dot(mixin, wo_ref[...])
        dmix = (g1_ref[...] * dr1).astype(MXU)
        dmix_ref[...] = dmix
        dmixin = _dot_nt(dmix, wo_ref[...])
        dysb = (dmixin * gs).astype(MXU)
        dyfx = (dmixin * gf).astype(MXU)
        dysb_ref[...] = dysb
        dyfx_ref[...] = dyfx
        dosb_ref[...] = _dot_nt(dysb, wsb_ref[...]).astype(MXU)
        dofx_ref[...] = _dot_nt(dyfx, wfx_ref[...]).astype(MXU)
        dgs = dmixin * ysb * gs * (1.0 - gs)
        dgf = dmixin * yfx * gf * (1.0 - gf)
        dgl_ref[:, :D] = dgs.astype(MXU)
        dgl_ref[:, D:] = dgf.astype(MXU)
        dbg_ref[:, :D] += _colsum(dgs)
        dbg_ref[:, D:] += _colsum(dgf)
        acc_ref[0:1, :] += _colsum(du2)
        acc_ref[1:2, :] += _colsum(du2 * n1)
        acc_ref[2:3, :] += _colsum(dx1 * xh1)
        acc_ref[3:4, :] += _colsum(dx1)
        acc_ref[4:5, :] += _colsum(dr1 * mix)

    vec = _fixed(1, D)
    return pl.pallas_call(
        body, name="mix_bwd", grid=(s // tm,),
        in_specs=[_rows(tm, D), _rows(tm, D), _rows(tm, D), _rows(tm, WIDTH), _rows(tm, WIDTH), _rows(tm, 2 * D),
                  _res(w_sb), _res(w_fx), _res(w_o), vec, vec, vec, vec],
        out_specs=[_rows(tm, D), _rows(tm, D), _rows(tm, D), _rows(tm, D), _rows(tm, D), _rows(tm, WIDTH), _rows(tm, WIDTH),
                   _rows(tm, 2 * D), _fixed(1, 2 * D), _fixed(8, D)],
        out_shape=[jax.ShapeDtypeStruct((s, D), F32)] + [jax.ShapeDtypeStruct((s, D), MXU)] * 4
        + [jax.ShapeDtypeStruct((s, WIDTH), MXU)] * 2
        + [jax.ShapeDtypeStruct((s, 2 * D), MXU), jax.ShapeDtypeStruct((1, 2 * D), F32), jax.ShapeDtypeStruct((8, D), F32)],
        compiler_params=_params(VMEM_BIG),
    )(du2, dxa, r1, o_sb, o_fx, gl, w_sb, w_fx, w_o, g1, ln1_g, ln1_b, sc2)


def _in_bwd(pieces, x, dxa, w_all, sc1, riders=()):
    s = x.shape[0]
    tm = 256
    n_p = len(pieces)

    def body(*refs):
        p_refs = refs[:n_p]
        x_ref, dxa_ref, w_ref, sc_ref, gx_ref, acc_ref = refs[n_p:]

        @pl.when(pl.program_id(0) == 0)
        def _():
            acc_ref[...] = jnp.zeros_like(acc_ref)

        du1 = jnp.zeros((tm, D), F32)
        for p_ref, (arr, c0) in zip(p_refs, pieces):
            du1 = du1 + _dot(p_ref[...], w_ref[c0:c0 + arr.shape[1], :])
        n0, rstd0 = _ln(x_ref[...])
        gx_ref[...] = dxa_ref[...] + _ln_bwd(du1 * (1.0 + sc_ref[...]), n0, rstd0)
        acc_ref[0:1, :] += _colsum(du1)
        acc_ref[1:2, :] += _colsum(du1 * n0)

    return _call_with_riders(
        body, "in_bwd", (s // tm,),
        [_rows(tm, a.shape[1]) for a, _ in pieces] + [_rows(tm, D), _rows(tm, D), _res(w_all), _fixed(1, D)],
        [_rows(tm, D), _fixed(8, D)], [jax.ShapeDtypeStruct((s, D), F32), jax.ShapeDtypeStruct((8, D), F32)], [],
        (*[a for a, _ in pieces], x, dxa, w_all, sc1), riders, False, VMEM_BIG)


def _matmul_tn(a, b, name, narrow=False):
    s, m = a.shape
    n = b.shape[1]
    tm = 512 if m % 512 == 0 else (m if m < 512 else m // 2)
    tn = n // 2 if n > 2048 else n
    ts = 512
    assert m % tm == 0 and tm % LANE == 0 and n % tn == 0 and tn % LANE == 0 and s % ts == 0

    def body(a_ref, b_ref, o_ref, *narrow_ref):
        @pl.when(pl.program_id(2) == 0)
        def _():
            o_ref[...] = jnp.zeros_like(o_ref)

        o_ref[...] += _dot_tn(a_ref[...], b_ref[...])
        if narrow:
            @pl.when(pl.program_id(2) == s // ts - 1)
            def _():
                narrow_ref[0][...] = o_ref[...].astype(MXU)

    out_blk = pl.BlockSpec((tm, tn), lambda i, j, k: (i, j))
    res = pl.pallas_call(
        body, name=name, grid=(m // tm, n // tn, s // ts),
        in_specs=[pl.BlockSpec((ts, tm), lambda i, j, k: (k, i)), pl.BlockSpec((ts, tn), lambda i, j, k: (k, j))],
        out_specs=[out_blk] * (2 if narrow else 1),
        out_shape=[jax.ShapeDtypeStruct((m, n), F32)] + ([jax.ShapeDtypeStruct((m, n), MXU)] if narrow else []),
        compiler_params=_params(VMEM_BIG),
    )(a, b)
    return tuple(res) if narrow else res[0]


def _local_step(x, tgt, ada, w_all, b_gate, bf_pad, late_weights, early_grads, w_in_grads, ln1_g, ln1_b, ln2_g, ln2_b):
    sh1, sc1, g1, sh2, sc2, g2 = ada
    u1, qkv, fl, gl = _in_proj(x, sh1, sc1, w_all, b_gate)
    fc, fkt = _fcum_fwd(fl, bf_pad)
    late_riders, late_full = late_weights
    n_sb = 3
    (o_sb, rs), gathered_a = _sb_fwd(qkv, late_riders[:n_sb])
    (o_fx, lse), gathered_b = _fox_fwd(qkv, fc, fkt, late_riders[n_sb:])
    w_sb, w_fx, w_o, w_g, w_u, w_d = late_full(list(gathered_a) + list(gathered_b))
    r1, u2 = _mix_fwd(x, o_sb, o_fx, gl, w_sb, w_fx, w_o, g1, ln1_g, ln1_b, sh2, sc2)
    hg, hu, dxa2, dh, acc_f = _ffn_fwd(r1, u2, tgt, w_g, w_u, w_d, g2, ln1_g, ln1_b, ln2_g, ln2_b)
    act, dhg, dhu, du2 = _ffn_bwd(dh, hg, hu, w_g, w_u, w_d)
    dxa1, mixin, dmix, dysb, dyfx, dosb, dofx, dgl, dbg, acc_m = _mix_bwd(
        du2, dxa2, r1, o_sb, o_fx, gl, w_sb, w_fx, w_o, g1, ln1_g, ln1_b, sc2)
    early = dict(w_sb_out=_matmul_tn(o_sb, dysb, "dw_sb_out", True), w_fox_out=_matmul_tn(o_fx, dyfx, "dw_fox_out", True),
                 w_o=_matmul_tn(mixin, dmix, "dw_o", True), w_ffn_gate=_matmul_tn(dhg, u2, "dw_ffn_gate", True),
                 w_ffn_up=_matmul_tn(dhu, u2, "dw_ffn_up", True), w_ffn_down=_matmul_tn(act, dh, "dw_ffn_down", True))
    early_riders = early_grads(early)
    (dq_sb, dk_sb, dv_sb), received_a = _sb_bwd(qkv, dosb, rs, early_riders[:n_sb])
    (dq_fx, dk_fx, dv_fx, dfkt, dfq), received_b = _fox_bwd(qkv, fc, fkt, dofx, o_fx, lse, early_riders[n_sb:])
    early_received = list(received_a) + list(received_b)
    df, dbf = _fcum_bwd(dfkt, dfq, fl, bf_pad)
    pieces = [(dq_sb, 0), (dk_sb, WIDTH), (dv_sb, 2 * WIDTH), (dq_fx, 3 * WIDTH), (dk_fx, 4 * WIDTH), (dv_fx, 5 * WIDTH),
              (df, OFF_FGATE), (dgl, OFF_FGATE + LANE)]
    dw_in = [_matmul_tn(p, u1, f"dw_in_{j}") for j, (p, _) in enumerate(pieces)]
    (grad_x, acc_i), w_in_received = _in_bwd(pieces, x, dxa1, w_all, sc1, w_in_grads(dw_in))
    return dict(
        loss_lanes=acc_f[3:4], grad_x=grad_x, dw_in=dw_in, early=early, early_received=early_received,
        w_in_received=w_in_received,
        d_ada=[acc_i[0:1], acc_i[1:2], acc_m[4:5], acc_m[0:1], acc_m[1:2], acc_f[0:1]],
        dln1_g=acc_m[2:3], dln1_b=acc_m[3:4], dln2_g=acc_f[1:2], dln2_b=acc_f[2:3], db_gate=dbg, db_forget=dbf)


_MESH_ID = pl.DeviceIdType.MESH
_ANY = pl.BlockSpec(memory_space=pl.ANY)
_VMEM = pl.BlockSpec(memory_space=pltpu.VMEM)


def _mesh_pos():
    return lax.axis_index("x"), lax.axis_index("y"), lax.axis_index("c")


def _other_chips(x, y):
    return [(1 - x, y), (x, 1 - y), (1 - x, 1 - y)]


def _allgather_rows(v, name):
    n = v.shape[1]

    def body(v_ref, out_ref, send_sems, recv_sems, local_sem):
        x, y, c = _mesh_pos()
        me = 4 * x + 2 * y + c
        mine = pltpu.make_async_copy(v_ref, out_ref.at[me], local_sem)
        mine.start()
        copies = []
        for d in range(1, 8):
            fx, fy, fc = (d >> 2) & 1, (d >> 1) & 1, d & 1
            to = (1 - x if fx else x, 1 - y if fy else y, 1 - c if fc else c)
            cp = pltpu.make_async_remote_copy(src_ref=v_ref, dst_ref=out_ref.at[me], send_sem=send_sems.at[d - 1],
                                              recv_sem=recv_sems.at[d - 1], device_id=to, device_id_type=_MESH_ID)
            cp.start()
            copies.append(cp)
        for cp in copies:
            cp.wait_recv()
        for cp in copies:
            cp.wait_send()
        mine.wait()

    return pl.pallas_call(
        body, name=name, in_specs=[_VMEM], out_specs=_VMEM,
        out_shape=jax.ShapeDtypeStruct((8, 1, n), v.dtype),
        scratch_shapes=[pltpu.SemaphoreType.DMA((7,)), pltpu.SemaphoreType.DMA((7,)), pltpu.SemaphoreType.DMA(())],
    )(v)


def _chip_exchange(arrays, name, gather):
    nt = len(arrays)

    def body(*refs):
        ins, outs = refs[:nt], refs[nt:2 * nt]
        _exchange_start(ins, outs, refs[2 * nt:], gather)
        _exchange_wait(ins, outs, refs[2 * nt:], gather)

    return pl.pallas_call(
        body, name=name, in_specs=[_ANY] * nt, out_specs=[_ANY] * nt, out_shape=_exchange_out_shape(arrays),
        scratch_shapes=_exchange_sems(nt),
    )(*arrays)


def _exchange_out_shape(arrays):
    return [jax.ShapeDtypeStruct((4,) + a.shape[-2:], a.dtype) for a in arrays]


def _exchange_sems(nt):
    return [pltpu.SemaphoreType.DMA((3 * nt,)), pltpu.SemaphoreType.DMA((3 * nt,)), pltpu.SemaphoreType.DMA((nt,))]


def _exchange_copies(ins, outs, sems, gather):
    send_sems, recv_sems, local_sems = sems
    x, y, c = _mesh_pos()
    me = 2 * x + y
    local, remote = [], []
    for t in range(len(ins)):
        local.append(pltpu.make_async_copy(ins[t] if gather else ins[t].at[me], outs[t].at[me], local_sems.at[t]))
        for j, (px, py) in enumerate(_other_chips(x, y)):
            remote.append(pltpu.make_async_remote_copy(
                src_ref=ins[t] if gather else ins[t].at[2 * px + py], dst_ref=outs[t].at[me], send_sem=send_sems.at[3 * t + j],
                recv_sem=recv_sems.at[3 * t + j], device_id=(px, py, c), device_id_type=_MESH_ID))
    return local, remote


def _exchange_start(ins, outs, sems, gather):
    local, remote = _exchange_copies(ins, outs, sems, gather)
    for cp in local + remote:
        cp.start()


def _exchange_wait(ins, outs, sems, gather):
    local, remote = _exchange_copies(ins, outs, sems, gather)
    for cp in remote:
        cp.wait_recv()
    for cp in remote:
        cp.wait_send()
    for cp in local:
        cp.wait()


def _gather_two_level(shard, name):
    r, n = shard.shape
    half = n // 2
    assert half % LANE == 0

    def body(in_ref, out_ref, ici_send, ici_recv, d2d_send, d2d_recv, local_sem):
        x, y, c = _mesh_pos()
        me = 2 * x + y
        mine = pl.ds(pl.multiple_of(c * half, LANE), half)
        theirs = pl.ds(pl.multiple_of((1 - c) * half, LANE), half)
        local = pltpu.make_async_copy(in_ref, out_ref.at[me], local_sem)
        local.start()
        chips = _other_chips(x, y)
        over_ici = [pltpu.make_async_remote_copy(
            src_ref=in_ref.at[:, mine], dst_ref=out_ref.at[me, :, mine], send_sem=ici_send.at[j], recv_sem=ici_recv.at[j],
            device_id=(px, py, c), device_id_type=_MESH_ID) for j, (px, py) in enumerate(chips)]
        for cp in over_ici:
            cp.start()
        passed_on = [pltpu.make_async_remote_copy(
            src_ref=out_ref.at[2 * px + py, :, mine], dst_ref=out_ref.at[2 * px + py, :, mine], send_sem=d2d_send.at[j],
            recv_sem=d2d_recv.at[j], device_id=(x, y, 1 - c), device_id_type=_MESH_ID) for j, (px, py) in enumerate(chips)]
        for j, (px, py) in enumerate(chips):
            pltpu.make_async_remote_copy(
                src_ref=in_ref.at[:, mine], dst_ref=out_ref.at[2 * px + py, :, mine], send_sem=ici_send.at[j],
                recv_sem=ici_recv.at[j], device_id=(px, py, c), device_id_type=_MESH_ID).wait_recv()
            passed_on[j].start()
        for j, (px, py) in enumerate(chips):
            pltpu.make_async_remote_copy(
                src_ref=out_ref.at[2 * px + py, :, theirs], dst_ref=out_ref.at[2 * px + py, :, theirs], send_sem=d2d_send.at[j],
                recv_sem=d2d_recv.at[j], device_id=(x, y, 1 - c), device_id_type=_MESH_ID).wait_recv()
        for cp in over_ici + passed_on:
            cp.wait_send()
        local.wait()

    sems = pltpu.SemaphoreType.DMA((3,))
    return pl.pallas_call(
        body, name=name, in_specs=[_ANY], out_specs=_ANY, out_shape=jax.ShapeDtypeStruct((4, r, n), shard.dtype),
        scratch_shapes=[sems, sems, sems, sems, pltpu.SemaphoreType.DMA(())],
    )(shard)


def _sibling_exchange(arrays, name):
    nt = len(arrays)

    def body(*refs):
        ins, outs = refs[:nt], refs[nt:2 * nt]
        send_sems, recv_sems = refs[2 * nt:]
        x, y, c = _mesh_pos()
        copies = []
        for t in range(nt):
            cp = pltpu.make_async_remote_copy(src_ref=ins[t], dst_ref=outs[t], send_sem=send_sems.at[t], recv_sem=recv_sems.at[t],
                                              device_id=(x, y, 1 - c), device_id_type=_MESH_ID)
            cp.start()
            copies.append(cp)
        for cp in copies:
            cp.wait_recv()
        for cp in copies:
            cp.wait_send()

    return pl.pallas_call(
        body, name=name, in_specs=[_ANY] * nt, out_specs=[_ANY] * nt,
        out_shape=[jax.ShapeDtypeStruct(a.shape, a.dtype) for a in arrays],
        scratch_shapes=[pltpu.SemaphoreType.DMA((nt,)), pltpu.SemaphoreType.DMA((nt,))],
    )(*arrays)


def _tiles(r, n):
    for tr in (256, 352, 128):
        if r % tr == 0:
            return tr, n, r // tr, lambda i: (i, 0)
    assert n % 256 == 0
    return r, 256, n // 256, lambda i: (0, i)


def _reduce_chips(chip, pieces, recv, name):
    _, r, n = pieces.shape
    tr, tn, steps, at = _tiles(r, n)

    def body(chip_ref, own_ref, recv_ref, out_ref):
        me = chip_ref[0]
        total = jnp.zeros((tr, tn), F32)
        for k in range(4):
            total = total + jnp.where(me == k, own_ref[0], recv_ref[k].astype(F32))
        out_ref[...] = total

    return pl.pallas_call(
        body, name=name,
        grid_spec=pltpu.PrefetchScalarGridSpec(
            num_scalar_prefetch=1, grid=(steps,),
            in_specs=[pl.BlockSpec((1, tr, tn), lambda i, chip_ref: (chip_ref[0],) + at(i)),
                      pl.BlockSpec((4, tr, tn), lambda i, chip_ref: (0,) + at(i))],
            out_specs=pl.BlockSpec((tr, tn), lambda i, chip_ref: at(i))),
        out_shape=jax.ShapeDtypeStruct((r, n), F32),
    )(chip, pieces, recv)


def _adamw_math(w, g, m, v):
    m = ADAM_B1 * m + (1.0 - ADAM_B1) * g
    v = ADAM_B2 * v + (1.0 - ADAM_B2) * (g * g)
    m_hat = m / (1.0 - ADAM_B1 ** ADAM_STEP)
    v_hat = v / (1.0 - ADAM_B2 ** ADAM_STEP)
    return -ADAM_LR * (m_hat / (jnp.sqrt(v_hat) + ADAM_EPS) + ADAM_WD * w), m, v


def _adamw(w, m, v, g_parts, name):
    r, n = w.shape
    tr, tn, steps, at = _tiles(r, n)
    blk = pl.BlockSpec((tr, tn), at)
    ng = len(g_parts)

    def body(*refs):
        w_ref, m_ref, v_ref = refs[:3]
        g_refs = refs[3:3 + ng]
        g_out, d_out, m_out, v_out = refs[3 + ng:]
        g = g_refs[0][...]
        for gr in g_refs[1:]:
            g = g + gr[...]
        g_out[...] = g
        d_out[...], m_out[...], v_out[...] = _adamw_math(w_ref[...], g, m_ref[...], v_ref[...])

    return pl.pallas_call(
        body, name=name, grid=(steps,),
        in_specs=[blk] * (3 + ng), out_specs=[blk] * 4,
        out_shape=[jax.ShapeDtypeStruct((r, n), F32)] * 4,
    )(w, m, v, *g_parts)


def _ada_fwd(c_all, w_shard, b_shard):
    n = w_shard.shape[1]
    tn = 512

    def body(c_ref, w_ref, b_ref, o_ref):
        cv = c_ref[...]
        ca = (cv * _sigmoid(cv)).astype(MXU)
        o_ref[...] = _dot(ca, w_ref[...].astype(MXU)) + b_ref[...]

    return pl.pallas_call(
        body, name="ada_fwd", grid=(n // tn,),
        in_specs=[_fixed(8, D), pl.BlockSpec((D, tn), lambda j: (0, j)), pl.BlockSpec((1, tn), lambda j: (0, j))],
        out_specs=pl.BlockSpec((8, tn), lambda j: (0, j)),
        out_shape=jax.ShapeDtypeStruct((8, n), F32),
    )(c_all, w_shard, b_shard)


def _ada_bwd(c_all, dada_shard):
    n = dada_shard.shape[1]
    tn = 512

    def body(c_ref, d_ref, o_ref):
        cv = c_ref[...]
        ca = (cv * _sigmoid(cv)).astype(MXU)
        o_ref[...] = _dot_tn(ca, d_ref[...].astype(MXU))

    return pl.pallas_call(
        body, name="ada_bwd", grid=(n // tn,),
        in_specs=[_fixed(8, D), pl.BlockSpec((8, tn), lambda j: (0, j))],
        out_specs=pl.BlockSpec((D, tn), lambda j: (0, j)),
        out_shape=jax.ShapeDtypeStruct((D, n), F32),
    )(c_all, dada_shard)


_SMALL = [("d_ada", N_COND * D), ("ln1_g", D), ("ln1_b", D), ("ln2_g", D), ("ln2_b", D), ("b_gate", 2 * D), ("b_forget", LANE),
          ("loss", D)]
_SMALL_OFF = {}
_o = 0
for _n, _w in _SMALL:
    _SMALL_OFF[_n] = (_o, _w)
    _o += _w
_SMALL_LEN = _o
_SMALL_PARAMS = [("b_ada", "d_ada", N_COND * D), ("b_gate", "b_gate", 2 * D), ("b_forget", "b_forget", N_FGATE),
                 ("ln1_g", "ln1_g", D), ("ln1_b", "ln1_b", D), ("ln2_g", "ln2_g", D), ("ln2_b", "ln2_b", D)]


def _small_update(rows, params):
    npar = len(_SMALL_PARAMS)

    def body(*refs):
        rows_ref = refs[0]
        p_refs = refs[1:1 + 3 * npar]
        loss_ref = refs[1 + 3 * npar]
        o_refs = refs[2 + 3 * npar:]
        total = rows_ref[0]
        for d in range(1, 8):
            total = total + rows_ref[d]
        lo, lw = _SMALL_OFF["loss"]
        loss_ref[...] = jnp.sum(total[:, lo:lo + lw], axis=1, keepdims=True)
        for j, (_, key, n) in enumerate(_SMALL_PARAMS):
            off = _SMALL_OFF[key][0]
            g = total[:, off:off + n]
            w_ref, m_ref, v_ref = p_refs[3 * j:3 * j + 3]
            o_refs[4 * j][...] = g
            o_refs[4 * j + 1][...], o_refs[4 * j + 2][...], o_refs[4 * j + 3][...] = _adamw_math(w_ref[...], g, m_ref[...], v_ref[...])

    flat = [a for p in params for a in p]
    out_shape = [jax.ShapeDtypeStruct((1, 1), F32)] + [jax.ShapeDtypeStruct((1, n), F32) for _, _, n in _SMALL_PARAMS for _ in range(4)]
    return pl.pallas_call(body, name="small_update", out_shape=out_shape)(rows, *flat)


_BIG = [("w_in", "cols_t"), ("w_sb_out", "cols"), ("w_fox_out", "cols"), ("w_o", "rows"),
        ("w_ffn_gate", "cols_t"), ("w_ffn_up", "cols_t"), ("w_ffn_down", "rows")]


def _shard2d(a, how):
    return a[0].T if how == "cols_t" else a[0]


def _unshard(g, how):
    if how == "cols":
        return g.transpose(1, 0, 2).reshape(g.shape[1], 4 * g.shape[2])
    return g.reshape(4 * g.shape[1], g.shape[2])


def _reshard(w, how):
    if how == "cols":
        return w.reshape(w.shape[0], 4, w.shape[1] // 4).transpose(1, 0, 2)
    return w.reshape(4, w.shape[0] // 4, w.shape[1])


def kernel(x, c, w_ada, b_ada, w_in, b_gate, b_forget, w_sb_out, w_fox_out, w_o, ln1_g, ln1_b, w_ffn_gate, w_ffn_up, w_ffn_down, ln2_g, ln2_b, loss_target, m_w_ada, m_b_ada, m_w_in, m_b_gate, m_b_forget, m_w_sb_out, m_w_fox_out, m_w_o, m_ln1_g, m_ln1_b, m_w_ffn_gate, m_w_ffn_up, m_w_ffn_down, m_ln2_g, m_ln2_b, v_w_ada, v_b_ada, v_w_in, v_b_gate, v_b_forget, v_w_sb_out, v_w_fox_out, v_w_o, v_ln1_g, v_ln1_b, v_w_ffn_gate, v_w_ffn_up, v_w_ffn_down, v_ln2_g, v_ln2_b):
    given = dict(locals())
    mx, my, mc = _mesh_pos()
    chip = 2 * mx + my
    seq = 4 * mx + 2 * my + mc

    c_all = _allgather_rows(c, "gather_c").reshape(8, D)
    n_ada = w_ada.shape[2]
    b_ada_shard = lax.dynamic_slice(b_ada, (0, chip * n_ada), (1, n_ada))
    ada_part = _ada_fwd(c_all, w_ada[0], b_ada_shard)
    ada_all = _allgather_rows(ada_part.reshape(1, 8 * n_ada), "gather_ada").reshape(4, 2, 8, n_ada)
    ada_row = lax.dynamic_slice(ada_all, (0, mc, seq, 0), (4, 1, 1, n_ada)).reshape(1, N_COND * D)
    ada = [ada_row[:, j * D:(j + 1) * D] for j in range(N_COND)]

    w_in_g = _gather_two_level(_shard2d(w_in, "cols_t").astype(MXU), "gather_w_in")
    wi = _unshard(w_in_g, "cols_t")
    w_all = jnp.concatenate([wi[:OFF_FGATE + N_FGATE], jnp.zeros((LANE - N_FGATE, D), MXU), wi[OFF_FGATE + N_FGATE:]], axis=0)
    bf_pad = jnp.concatenate([b_forget, jnp.zeros((1, LANE - N_FGATE), F32)], axis=1)
    late = _BIG[1:]
    late_riders = [_shard2d(given[n], how).astype(MXU) for n, how in late]
    pieces = {}

    def late_full(gathered):
        return [_unshard(g, how) for (_, how), g in zip(late, gathered)]

    def early_grads(dw):
        for n, how in late:
            pieces[n] = _reshard(dw[n][0], how)
        return [_reshard(dw[n][1], how) for n, how in late]

    def w_in_grads(dwi):
        pieces["w_in"] = _reshard(jnp.concatenate(dwi[:6] + [dwi[6][:N_FGATE], dwi[7]], axis=0), "cols_t")
        return [pieces["w_in"].astype(MXU)]

    out = _local_step(x[0], loss_target[0], ada, w_all, b_gate, bf_pad, (late_riders, late_full), early_grads, w_in_grads,
                      ln1_g, ln1_b, ln2_g, ln2_b)

    row = jnp.concatenate(out["d_ada"] + [out["dln1_g"], out["dln1_b"], out["dln2_g"], out["dln2_b"], out["db_gate"],
                                          out["db_forget"], out["loss_lanes"]], axis=1)
    rows = _allgather_rows(row, "gather_small")
    small = _small_update(rows, [(given[p], given["m_" + p], given["v_" + p]) for p, _, _ in _SMALL_PARAMS])
    loss = small[0].reshape(())
    res = {}
    for j, (p, _, _) in enumerate(_SMALL_PARAMS):
        res[p] = small[1 + 4 * j:5 + 4 * j]

    dada_all = rows.reshape(8, _SMALL_LEN)[:, :N_COND * D]
    dada_shard = lax.dynamic_slice(dada_all, (0, chip * n_ada), (8, n_ada))
    g_ada = _ada_bwd(c_all, dada_shard)
    res["w_ada"] = [a[None] for a in _adamw(w_ada[0], m_w_ada[0], v_w_ada[0], [g_ada], "adamw_w_ada")]

    received = dict(zip([n for n, _ in late], out["early_received"]))
    (received["w_in"],) = out["w_in_received"]
    chip_arr = jnp.reshape(chip, (1,)).astype(jnp.int32)
    partial = [_reduce_chips(chip_arr, pieces[n], received[n], "reduce_" + n) for n, _ in _BIG]
    theirs = _sibling_exchange(partial, "swap_cores")
    for (n, how), mine, other in zip(_BIG, partial, theirs):
        upd = _adamw(_shard2d(given[n], how), _shard2d(given["m_" + n], how), _shard2d(given["v_" + n], how), [mine, other], "adamw_" + n)
        res[n] = [(a.T if how == "cols_t" else a)[None] for a in upd]

    order = ["w_ada", "b_ada", "w_in", "b_gate", "b_forget", "w_sb_out", "w_fox_out", "w_o", "ln1_g", "ln1_b",
             "w_ffn_gate", "w_ffn_up", "w_ffn_down", "ln2_g", "ln2_b"]
    return (loss, out["grad_x"][None], *[res[n][0] for n in order], *[res[n][1] for n in order],
            *[res[n][2] for n in order], *[res[n][3] for n in order])
```

```python
import functools

import jax
import jax.numpy as jnp
from jax import lax
from jax.experimental import pallas as pl
from jax.experimental.pallas import tpu as pltpu

F32 = jnp.float32
MXU = jnp.bfloat16

D = 1024
HEAD_DIM = 64
WIDTH = 512
D_FF = 2816
N_COND = 6
LN_EPS = 1e-5
ALPHA = 2.0 ** 0.25
QK_SCALE = HEAD_DIM ** -0.5
OFF_FGATE = 6 * WIDTH
N_FGATE = 8
IN_COLS = OFF_FGATE + N_FGATE + 2 * D
LANE = 128
W_ALL_COLS = OFF_FGATE + LANE + 2 * D
TQ = 512
SB_TQ = 256
ADAM_LR, ADAM_B1, ADAM_B2, ADAM_EPS, ADAM_WD, ADAM_STEP = 0.001, 0.9, 0.999, 1e-08, 0.01, 10
NEG = -1e30
DEAD_LOG = -120.0
RS_COUNT_LANE = LANE - 1
MESH_AXES = ("x", "y", "c")
VMEM_BIG = 56 * 1024 * 1024


def _dot(a, b):
    return jnp.dot(a, b, preferred_element_type=F32)


def _dot_nt(a, b):
    return lax.dot_general(a, b, (((1,), (1,)), ((), ())), preferred_element_type=F32)


def _dot_tn(a, b):
    return lax.dot_general(a, b, (((0,), (0,)), ((), ())), preferred_element_type=F32)


def _ln(x):
    mu = jnp.mean(x, axis=-1, keepdims=True)
    xc = x - mu
    var = jnp.mean(xc * xc, axis=-1, keepdims=True)
    rstd = lax.rsqrt(var + LN_EPS)
    return xc * rstd, rstd


def _ln_bwd(dxhat, xhat, rstd):
    return rstd * (dxhat - jnp.mean(dxhat, axis=-1, keepdims=True) - xhat * jnp.mean(dxhat * xhat, axis=-1, keepdims=True))


def _sigmoid(x):
    return 1.0 / (1.0 + jnp.exp(-x))


def _colsum(x):
    return jnp.sum(x, axis=0, keepdims=True)


def _split(x):
    hi = x.astype(MXU)
    lo = (x - hi.astype(F32)).astype(MXU)
    return jnp.concatenate([hi, lo], axis=1)


def _rows(tm, n):
    return pl.BlockSpec((tm, n), lambda i: (i, 0))


def _fixed(r, n):
    return pl.BlockSpec((r, n), lambda i: (0, 0))


def _res(a):
    return pl.BlockSpec(a.shape, lambda i: (0, 0), pipeline_mode=pl.Buffered(1))


def _params(limit=None, sem=None):
    return pltpu.CompilerParams(vmem_limit_bytes=limit, dimension_semantics=sem)


def _in_proj(x, sh1, sc1, w_all, b_gate):
    s = x.shape[0]
    tm = 256

    def body(x_ref, sh_ref, sc_ref, w_ref, bg_ref, u_ref, qkv_ref, fl_ref, gl_ref):
        xhat, _ = _ln(x_ref[...])
        u = (xhat * (1.0 + sc_ref[...]) + sh_ref[...]).astype(MXU)
        u_ref[...] = u
        for c0 in range(0, OFF_FGATE, WIDTH):
            p = _dot_nt(u, w_ref[c0:c0 + WIDTH, :])
            if c0 in (0, 3 * WIDTH):
                p = p * QK_SCALE
            qkv_ref[:, c0:c0 + WIDTH] = p.astype(MXU)
        fl_ref[...] = _dot_nt(u, w_ref[OFF_FGATE:OFF_FGATE + LANE, :])
        for c0 in range(0, 2 * D, D):
            gl_ref[:, c0:c0 + D] = _dot_nt(u, w_ref[OFF_FGATE + LANE + c0:OFF_FGATE + LANE + c0 + D, :]) + bg_ref[:, c0:c0 + D]

    return pl.pallas_call(
        body, name="in_proj", grid=(s // tm,),
        in_specs=[_rows(tm, D), _fixed(1, D), _fixed(1, D), _res(w_all), _fixed(1, 2 * D)],
        out_specs=[_rows(tm, D), _rows(tm, OFF_FGATE), _rows(tm, LANE), _rows(tm, 2 * D)],
        out_shape=[jax.ShapeDtypeStruct((s, D), MXU), jax.ShapeDtypeStruct((s, OFF_FGATE), MXU),
                   jax.ShapeDtypeStruct((s, LANE), F32), jax.ShapeDtypeStruct((s, 2 * D), F32)],
        compiler_params=_params(VMEM_BIG),
    )(x, sh1, sc1, w_all, b_gate)


def _log_sigmoid_parts(z):
    e = jnp.exp(-jnp.abs(z))
    return -(jnp.maximum(z, 0.0) + jnp.log(1.0 + e)), e


def _fcum_fwd(fl, bf):
    s = fl.shape[0]
    nb = s // LANE

    def body(fl_ref, bf_ref, fc_ref, fkt_ref):
        r = lax.broadcasted_iota(jnp.int32, (LANE, LANE), 0)
        c = lax.broadcasted_iota(jnp.int32, (LANE, LANE), 1)
        tri = (c <= r).astype(F32)

        def step(b, carry):
            r0 = pl.multiple_of(b * LANE, LANE)
            xb = fl_ref[pl.ds(r0, LANE), :] + bf_ref[...]
            ls = _log_sigmoid_parts(-xb)[0]
            cs = jnp.dot(tri, ls, precision=lax.Precision.HIGHEST, preferred_element_type=F32) + carry
            fc_ref[pl.ds(r0, LANE), :] = cs
            fkt_ref[b] = cs.T[:N_FGATE, :]
            return cs[LANE - 1:LANE, :]

        lax.fori_loop(0, nb, step, jnp.zeros((1, LANE), F32))

    return pl.pallas_call(
        body, name="fcum_fwd",
        out_shape=[jax.ShapeDtypeStruct((s, LANE), F32), jax.ShapeDtypeStruct((nb, N_FGATE, LANE), F32)],
    )(fl, bf)


def _attn_specs(s, col0, tq):
    return [pl.BlockSpec((tq, LANE), lambda hp, i: (i, col0 + hp)),
            pl.BlockSpec((s, LANE), lambda hp, i: (0, col0 + 4 + hp)),
            pl.BlockSpec((s, LANE), lambda hp, i: (0, col0 + 8 + hp))]


def _tile_iotas(tq):
    lane = lax.broadcasted_iota(jnp.int32, (tq, LANE), 1)
    row = lax.broadcasted_iota(jnp.int32, (tq, tq), 0)
    col = lax.broadcasted_iota(jnp.int32, (tq, tq), 1)
    return lane, row, col


def _sub_blocks(nk):
    return [slice(j * LANE, (j + 1) * LANE) for j in range(nk // LANE)]


def _over_strips(tile, tq):
    return tile(slice(0, tq), tq)


def _tri(below):
    r = lax.broadcasted_iota(jnp.int32, (LANE, LANE), 0)
    c = lax.broadcasted_iota(jnp.int32, (LANE, LANE), 1)
    t = jnp.concatenate([((r > c) if below else (r < c)).astype(MXU), jnp.ones((LANE, LANE), MXU)], axis=1)
    return jnp.concatenate([t, t], axis=0)


def _call_with_riders(body, name, grid, in_specs, out_specs, out_shape, scratch, args, riders, gather, limit=None):
    nr, n_in, n_out, n_sc = len(riders), len(in_specs), len(out_specs), len(scratch)

    def at_step(which):
        hit = None
        for d, n in enumerate(grid):
            here = pl.program_id(d) == (0 if which == "first" else n - 1)
            hit = here if hit is None else hit & here
        return hit

    def wrapped(*refs):
        ins, rin = refs[:n_in], refs[n_in:n_in + nr]
        outs, rout = refs[n_in + nr:n_in + nr + n_out], refs[n_in + nr + n_out:n_in + 2 * nr + n_out]
        own, sems = refs[n_in + 2 * nr + n_out:n_in + 2 * nr + n_out + n_sc], refs[n_in + 2 * nr + n_out + n_sc:]
        if nr:
            @pl.when(at_step("first"))
            def _():
                _exchange_start(rin, rout, sems, gather)

        body(*ins, *outs, *own)
        if nr:
            @pl.when(at_step("last"))
            def _():
                _exchange_wait(rin, rout, sems, gather)

    res = pl.pallas_call(
        wrapped, name=name, grid=grid,
        in_specs=list(in_specs) + [_ANY] * nr, out_specs=list(out_specs) + [_ANY] * nr,
        out_shape=list(out_shape) + _exchange_out_shape(riders),
        scratch_shapes=list(scratch) + (_exchange_sems(nr) if nr else []),
        compiler_params=_params(limit),
    )(*args, *riders)
    return res[:n_out], res[n_out:]


def _sb_fwd(qkv, riders=()):
    s = qkv.shape[0]
    tq = SB_TQ
    nq = s // tq
    assert nq <= RS_COUNT_LANE

    def body(q_ref, k_ref, v_ref, o_ref, rs_ref):
        i = pl.program_id(1)
        lane, row, col = _tile_iotas(tq)
        u2 = _tri(True)
        diag = col < row
        q = q_ref[...]
        qms = [jnp.where(hm, q, jnp.zeros_like(q)) for hm in (lane < HEAD_DIM, lane >= HEAD_DIM)]

        def step(kb, carry, masked):
            k0 = pl.multiple_of(kb * tq, tq)
            k = k_ref[pl.ds(k0, tq), :]
            v = v_ref[pl.ds(k0, tq), :]
            def tile(rows, nk, qm, state):
                run, acc, rt = (t[rows] for t in state)
                z = _dot_nt(qm[rows], k[:nk])
                lneg, _ = _log_sigmoid_parts(z)
                lpos = z + lneg
                if masked:
                    lneg = jnp.where(diag[rows, :nk], lneg, 0.0)
                rt = jnp.where(lane[rows] == kb, run, rt)
                a = []
                for sl in reversed(_sub_blocks(nk)):
                    st = _dot(_split(lneg[:, sl]), u2)
                    a.append(jnp.exp(lpos[:, sl] + st[:, :LANE] + run))
                    run = run + st[:, LANE:]
                a = jnp.concatenate(a[::-1], axis=1)
                if masked:
                    a = jnp.where(diag[rows, :nk], a, 0.0)
                return run, acc + _dot(a.astype(MXU), v[:nk]), rt

            return tuple(_over_strips(functools.partial(tile, qm=qm, state=state), tq) for qm, state in zip(qms, carry))

        zero = jnp.zeros((tq, LANE), F32)
        carry = step(i, ((zero, zero, zero),) * 2, True)

        def alive(cr):
            return jnp.maximum(jnp.max(cr[0][0]), jnp.max(cr[1][0])) > DEAD_LOG

        def walk(state):
            j, _, cr = state
            cr = step(i - 1 - j, cr, False)
            return j + 1, alive(cr), cr

        walked, _, carry = lax.while_loop(lambda state: (state[0] < i) & state[1], walk, (jnp.int32(0), alive(carry), carry))
        count = walked.astype(F32)
        rs_ref[0] = jnp.where(lane == RS_COUNT_LANE, count, carry[0][2])
        rs_ref[1] = jnp.where(lane == RS_COUNT_LANE, count, carry[1][2])
        o_ref[...] = jnp.where(lane < HEAD_DIM, carry[0][1], carry[1][1]).astype(o_ref.dtype)

    return _call_with_riders(
        body, "sb_fwd", (4, nq), _attn_specs(s, 0, tq),
        [pl.BlockSpec((tq, LANE), lambda hp, i: (i, hp)), pl.BlockSpec((2, tq, LANE), lambda hp, i: (hp, i, 0))],
        [jax.ShapeDtypeStruct((s, WIDTH), MXU), jax.ShapeDtypeStruct((8, s, LANE), F32)], [], (qkv, qkv, qkv), riders, True)


def _sb_bwd(qkv, do, rs, riders=()):
    s = qkv.shape[0]
    tq = SB_TQ
    nq = s // tq

    def body(q_ref, k_ref, v_ref, do_ref, rs_ref, dq_ref, dk_ref, dv_ref, dk_acc, dv_acc):
        i = pl.program_id(1)

        @pl.when(i == 0)
        def _():
            dk_acc[...] = jnp.zeros_like(dk_acc)
            dv_acc[...] = jnp.zeros_like(dv_acc)

        lane, row, col = _tile_iotas(tq)
        u2 = _tri(True)
        l2 = _tri(False)
        diag = col < row
        q = q_ref[...]
        do = do_ref[...]
        heads = [(jnp.where(hm, q, jnp.zeros_like(q)), jnp.where(hm, do, jnp.zeros_like(do)), rs_ref[hh])
                 for hh, hm in enumerate((lane < HEAD_DIM, lane >= HEAD_DIM))]

        def step(kb, carry, masked):
            k0 = pl.multiple_of(kb * tq, tq)
            k = k_ref[pl.ds(k0, tq), :]
            v = v_ref[pl.ds(k0, tq), :]
            to_keys = {}

            def tile(rows, nk, qm, dom, rblk, state):
                gpre, dq = (t[rows] for t in state)
                z = _dot_nt(qm[rows], k[:nk])
                lneg, e = _log_sigmoid_parts(z)
                lpos = z + lneg
                if masked:
                    lneg = jnp.where(diag[rows, :nk], lneg, 0.0)
                run = jnp.sum(jnp.where(lane[rows] == kb, rblk[rows], 0.0), axis=1, keepdims=True) + jnp.zeros_like(gpre)
                a = []
                for sl in reversed(_sub_blocks(nk)):
                    st = _dot(_split(lneg[:, sl]), u2)
                    a.append(jnp.exp(lpos[:, sl] + st[:, :LANE] + run))
                    run = run + st[:, LANE:]
                a = jnp.concatenate(a[::-1], axis=1)
                if masked:
                    a = jnp.where(diag[rows, :nk], a, 0.0)
                g = a * _dot_nt(dom[rows], v[:nk])
                pre = []
                for sl in _sub_blocks(nk):
                    pt = _dot(_split(g[:, sl]), l2)
                    pre.append(gpre + pt[:, :LANE])
                    gpre = gpre + pt[:, LANE:]
                sig = jnp.where(z >= 0.0, 1.0, e) / (1.0 + e)
                dz = g - (g + jnp.concatenate(pre, axis=1)) * sig
                if masked:
                    dz = jnp.where(diag[rows, :nk], dz, 0.0)
                dzb = dz.astype(MXU)
                both = to_keys.setdefault(nk, [0.0, 0.0])
                both[0] = both[0] + _dot_tn(dzb, qm[rows])
                both[1] = both[1] + _dot_tn(a.astype(MXU), dom[rows])
                return gpre, dq + _dot(dzb, k[:nk])

            new = tuple(_over_strips(functools.partial(tile, qm=qm, dom=dom, rblk=rblk, state=state), tq)
                        for (qm, dom, rblk), state in zip(heads, carry))
            for nk, (dk, dv) in to_keys.items():
                dk_acc[pl.ds(k0, nk), :] += dk
                dv_acc[pl.ds(k0, nk), :] += dv
            return new

        walked = jnp.max(jnp.where(lane[:8] == RS_COUNT_LANE, rs_ref[0, 0:8, :], 0.0))
        first = i - jnp.clip(walked.astype(jnp.int32), 0, i)
        zero = jnp.zeros((tq, LANE), F32)
        carry = step(i, lax.fori_loop(first, i, lambda kb, cr: step(kb, cr, False), ((zero, zero),) * 2), True)
        dq_ref[...] = (jnp.where(lane < HEAD_DIM, carry[0][1], carry[1][1]) * QK_SCALE).astype(dq_ref.dtype)

        @pl.when(i == nq - 1)
        def _():
            dk_ref[...] = dk_acc[...].astype(dk_ref.dtype)
            dv_ref[...] = dv_acc[...].astype(dv_ref.dtype)

    blk = pl.BlockSpec((tq, LANE), lambda hp, i: (i, hp))
    whole = pl.BlockSpec((s, LANE), lambda hp, i: (0, hp))
    return _call_with_riders(
        body, "sb_bwd", (4, nq), _attn_specs(s, 0, tq) + [blk, pl.BlockSpec((2, tq, LANE), lambda hp, i: (hp, i, 0))],
        [blk, whole, whole], [jax.ShapeDtypeStruct((s, WIDTH), MXU)] * 3,
        [pltpu.VMEM((s, LANE), F32), pltpu.VMEM((s, LANE), F32)], (qkv, qkv, qkv, do, rs), riders, False)


def _key_bias(fkt_ref, kb, h, tq):
    n_sub = tq // LANE
    return jnp.concatenate([fkt_ref[kb * n_sub + j, pl.ds(h, 1), :] for j in range(n_sub)], axis=1)


def _fox_fwd(qkv, fc, fkt, riders=()):
    s = qkv.shape[0]
    tq = TQ
    nq = s // tq
    nb = fkt.shape[0]

    def body(q_ref, k_ref, v_ref, fq_ref, fkt_ref, o_ref, lse_ref):
        hp = pl.program_id(0)
        i = pl.program_id(1)
        lane, row, col = _tile_iotas(tq)
        diag = col <= row
        q = q_ref[...]
        fqb = fq_ref[...]
        heads = []
        for hh in range(2):
            h = 2 * hp + hh
            hm = (lane >= HEAD_DIM) if hh else (lane < HEAD_DIM)
            heads.append((h, jnp.where(hm, q, jnp.zeros_like(q)), jnp.sum(jnp.where(lane == h, fqb, 0.0), axis=1, keepdims=True)))

        def step(kb, carry, masked):
            k0 = pl.multiple_of(kb * tq, tq)
            k = k_ref[pl.ds(k0, tq), :]
            v = v_ref[pl.ds(k0, tq), :]
            def tile(rows, nk, h, qm, fq, state):
                m, l, acc = (t[rows] for t in state)
                z = _dot_nt(qm[rows], k[:nk]) + fq[rows] - _key_bias(fkt_ref, kb, h, tq)[:, :nk]
                if masked:
                    z = jnp.where(diag[rows, :nk], z, NEG)
                mn = jnp.maximum(m, jnp.max(z, axis=1, keepdims=True))
                p = jnp.exp(z - mn)
                alpha = jnp.exp(m - mn)
                return mn, alpha * l + jnp.sum(p, axis=1, keepdims=True), alpha * acc + _dot(p.astype(MXU), v[:nk])

            return tuple(_over_strips(functools.partial(tile, h=h, qm=qm, fq=fq, state=state), tq)
                         for (h, qm, fq), state in zip(heads, carry))

        init = ((jnp.full((tq, 1), NEG, F32), jnp.zeros((tq, 1), F32), jnp.zeros((tq, LANE), F32)),) * 2
        carry = step(i, lax.fori_loop(0, i, lambda kb, cr: step(kb, cr, False), init), True)
        outs = []
        for hh, (m, l, acc) in enumerate(carry):
            outs.append(acc / l)
            lse_ref[hh] = jnp.broadcast_to(m + jnp.log(l), (tq, LANE))
        o_ref[...] = jnp.where(lane < HEAD_DIM, outs[0], outs[1]).astype(o_ref.dtype)

    return _call_with_riders(
        body, "fox_fwd", (4, nq),
        _attn_specs(s, 12, tq) + [pl.BlockSpec((tq, LANE), lambda hp, i: (i, 0)), pl.BlockSpec((nb, N_FGATE, LANE), lambda hp, i: (0, 0, 0))],
        [pl.BlockSpec((tq, LANE), lambda hp, i: (i, hp)), pl.BlockSpec((2, tq, LANE), lambda hp, i: (hp, i, 0))],
        [jax.ShapeDtypeStruct((s, WIDTH), MXU), jax.ShapeDtypeStruct((8, s, LANE), F32)], [], (qkv, qkv, qkv, fc, fkt), riders, True)


def _fox_bwd(qkv, fc, fkt, do, o, lse, riders=()):
    s = qkv.shape[0]
    tq = TQ
    nq = s // tq
    nb = fkt.shape[0]

    def body(q_ref, k_ref, v_ref, fq_ref, fkt_ref, do_ref, o_ref, lse_ref, dq_ref, dk_ref, dv_ref, dfk_ref, dfq_ref, dk_acc, dv_acc):
        hp = pl.program_id(0)
        i = pl.program_id(1)

        @pl.when(i == 0)
        def _():
            dk_acc[...] = jnp.zeros_like(dk_acc)
            dv_acc[...] = jnp.zeros_like(dv_acc)

        @pl.when((i == 0) & (hp == 0))
        def _():
            dfk_ref[...] = jnp.zeros_like(dfk_ref)

        lane, row, col = _tile_iotas(tq)
        diag = col <= row
        q = q_ref[...]
        do = do_ref[...]
        dof = do.astype(F32) * o_ref[...].astype(F32)
        fqb = fq_ref[...]
        heads = []
        for hh in range(2):
            h = 2 * hp + hh
            hm = (lane >= HEAD_DIM) if hh else (lane < HEAD_DIM)
            heads.append((h, jnp.where(hm, q, jnp.zeros_like(q)), jnp.where(hm, do, jnp.zeros_like(do)),
                          jnp.sum(jnp.where(hm, dof, 0.0), axis=1, keepdims=True),
                          jnp.sum(jnp.where(lane == h, fqb, 0.0), axis=1, keepdims=True), lse_ref[hh][:, :1]))

        def step(kb, carry, masked):
            k0 = pl.multiple_of(kb * tq, tq)
            k = k_ref[pl.ds(k0, tq), :]
            v = v_ref[pl.ds(k0, tq), :]
            to_keys = {}

            def tile(rows, nk, h, qm, dom, delta, fq, lse_t, state):
                dq, rsum = (t[rows] for t in state)
                z = _dot_nt(qm[rows], k[:nk]) + fq[rows] - _key_bias(fkt_ref, kb, h, tq)[:, :nk]
                if masked:
                    z = jnp.where(diag[rows, :nk], z, NEG)
                p = jnp.exp(z - lse_t[rows])
                ds = p * (_dot_nt(dom[rows], v[:nk]) - delta[rows])
                dsb = ds.astype(MXU)
                both = to_keys.setdefault(nk, [0.0, 0.0])
                both[0] = both[0] + _dot_tn(dsb, qm[rows])
                both[1] = both[1] + _dot_tn(p.astype(MXU), dom[rows])
                csum = _colsum(ds)
                for j, sl in enumerate(_sub_blocks(nk)):
                    dfk_ref[kb * (tq // LANE) + j, pl.ds(h, 1), :] += -csum[:, sl]
                return dq + _dot(dsb, k[:nk]), rsum + jnp.sum(ds, axis=1, keepdims=True)

            new = tuple(_over_strips(functools.partial(tile, h=h, qm=qm, dom=dom, delta=delta, fq=fq, lse_t=lse_t, state=state), tq)
                        for (h, qm, dom, delta, fq, lse_t), state in zip(heads, carry))
            for nk, (dk, dv) in to_keys.items():
                dk_acc[pl.ds(k0, nk), :] += dk
                dv_acc[pl.ds(k0, nk), :] += dv
            return new

        init = ((jnp.zeros((tq, LANE), F32), jnp.zeros((tq, 1), F32)),) * 2
        carry = step(i, lax.fori_loop(0, i, lambda kb, cr: step(kb, cr, False), init), True)
        dq_ref[...] = (jnp.where(lane < HEAD_DIM, carry[0][0], carry[1][0]) * QK_SCALE).astype(dq_ref.dtype)
        dfq_ref[0] = jnp.where(lane == heads[0][0], carry[0][1], jnp.where(lane == heads[1][0], carry[1][1], 0.0))

        @pl.when(i == nq - 1)
        def _():
            dk_ref[...] = dk_acc[...].astype(dk_ref.dtype)
            dv_ref[...] = dv_acc[...].astype(dv_ref.dtype)

    blk = pl.BlockSpec((tq, LANE), lambda hp, i: (i, hp))
    whole = pl.BlockSpec((s, LANE), lambda hp, i: (0, hp))
    pair = pl.BlockSpec((2, tq, LANE), lambda hp, i: (hp, i, 0))
    fkt_spec = pl.BlockSpec((nb, N_FGATE, LANE), lambda hp, i: (0, 0, 0))
    return _call_with_riders(
        body, "fox_bwd", (4, nq),
        _attn_specs(s, 12, tq) + [pl.BlockSpec((tq, LANE), lambda hp, i: (i, 0)), fkt_spec, blk, blk, pair],
        [blk, whole, whole, fkt_spec, pl.BlockSpec((1, tq, LANE), lambda hp, i: (hp, i, 0))],
        [jax.ShapeDtypeStruct((s, WIDTH), MXU)] * 3
        + [jax.ShapeDtypeStruct((nb, N_FGATE, LANE), F32), jax.ShapeDtypeStruct((4, s, LANE), F32)],
        [pltpu.VMEM((s, LANE), F32), pltpu.VMEM((s, LANE), F32)], (qkv, qkv, qkv, fc, fkt, do, o, lse), riders, False)


def _fcum_bwd(dfkt, dfq, fl, bf):
    s = fl.shape[0]
    nb = s // LANE

    def body(dfkt_ref, dfq_ref, fl_ref, bf_ref, df_ref, dbf_ref, tail_ref):
        @pl.when(pl.program_id(0) == 0)
        def _():
            tail_ref[...] = jnp.zeros_like(tail_ref)
            dbf_ref[...] = jnp.zeros_like(dbf_ref)

        r = lax.broadcasted_iota(jnp.int32, (LANE, LANE), 0)
        c = lax.broadcasted_iota(jnp.int32, (LANE, LANE), 1)
        tri = (c >= r).astype(F32)
        dfc = jnp.concatenate([dfkt_ref[0], jnp.zeros((LANE - N_FGATE, LANE), F32)], axis=0).T
        dfc = dfc + ((dfq_ref[0] + dfq_ref[1]) + (dfq_ref[2] + dfq_ref[3]))
        dls = jnp.dot(tri, dfc, precision=lax.Precision.HIGHEST, preferred_element_type=F32) + tail_ref[...]
        xb = fl_ref[...] + bf_ref[...]
        e = jnp.exp(-jnp.abs(xb))
        dfl = dls * (jnp.where(xb >= 0.0, e, 1.0) / (1.0 + e))
        df_ref[...] = dfl.astype(df_ref.dtype)
        tail_ref[...] = dls[0:1, :]
        dbf_ref[...] += _colsum(dfl)

    return pl.pallas_call(
        body, name="fcum_bwd", grid=(nb,),
        in_specs=[pl.BlockSpec((1, N_FGATE, LANE), lambda j: (nb - 1 - j, 0, 0)), pl.BlockSpec((4, LANE, LANE), lambda j: (0, nb - 1 - j, 0)),
                  pl.BlockSpec((LANE, LANE), lambda j: (nb - 1 - j, 0)), _fixed(1, LANE)],
        out_specs=[pl.BlockSpec((LANE, LANE), lambda j: (nb - 1 - j, 0)), _fixed(1, LANE)],
        out_shape=[jax.ShapeDtypeStruct((s, LANE), MXU), jax.ShapeDtypeStruct((1, LANE), F32)],
        scratch_shapes=[pltpu.VMEM((1, LANE), F32)],
    )(dfkt, dfq, fl, bf)


def _mix_fwd(x, o_sb, o_fx, gl, w_sb, w_fx, w_o, g1, ln1_g, ln1_b, sh2, sc2):
    s = x.shape[0]
    tm = 256

    def body(x_ref, osb_ref, ofx_ref, gl_ref, wsb_ref, wfx_ref, wo_ref, g1_ref, lg_ref, lb_ref, sh_ref, sc_ref, r1_ref, u2_ref):
        mixin = (_sigmoid(gl_ref[:, :D]) * _dot(osb_ref[...], wsb_ref[...])
                 + _sigmoid(gl_ref[:, D:]) * _dot(ofx_ref[...], wfx_ref[...]))
        r1 = ALPHA * x_ref[...] + g1_ref[...] * _dot(mixin.astype(MXU), wo_ref[...])
        r1_ref[...] = r1
        x1 = _ln(r1)[0] * lg_ref[...] + lb_ref[...]
        u2_ref[...] = (_ln(x1)[0] * (1.0 + sc_ref[...]) + sh_ref[...]).astype(MXU)

    vec = _fixed(1, D)
    return pl.pallas_call(
        body, name="mix_fwd", grid=(s // tm,),
        in_specs=[_rows(tm, D), _rows(tm, WIDTH), _rows(tm, WIDTH), _rows(tm, 2 * D), _res(w_sb), _res(w_fx), _res(w_o),
                  vec, vec, vec, vec, vec],
        out_specs=[_rows(tm, D), _rows(tm, D)],
        out_shape=[jax.ShapeDtypeStruct((s, D), F32), jax.ShapeDtypeStruct((s, D), MXU)],
        compiler_params=_params(VMEM_BIG),
    )(x, o_sb, o_fx, gl, w_sb, w_fx, w_o, g1, ln1_g, ln1_b, sh2, sc2)


def _ffn_fwd(r1, u2, tgt, w_g, w_u, w_d, g2, ln1_g, ln1_b, ln2_g, ln2_b):
    s = r1.shape[0]
    tm = 256

    def body(r1_ref, u2_ref, t_ref, wg_ref, wu_ref, wd_ref, g2_ref, l1g_ref, l1b_ref, l2g_ref, l2b_ref,
             hg_ref, hu_ref, dxa_ref, dh_ref, acc_ref):
        @pl.when(pl.program_id(0) == 0)
        def _():
            acc_ref[...] = jnp.zeros_like(acc_ref)

        u2 = u2_ref[...]
        hg = _dot_nt(u2, wg_ref[...])
        hu = _dot_nt(u2, wu_ref[...])
        hg_ref[...] = hg
        hu_ref[...] = hu
        h = _dot((hg * _sigmoid(hg) * hu).astype(MXU), wd_ref[...])
        x1 = _ln(r1_ref[...])[0] * l1g_ref[...] + l1b_ref[...]
        xh2, rstd2 = _ln(ALPHA * x1 + g2_ref[...] * h)
        err = xh2 * l2g_ref[...] + l2b_ref[...] - t_ref[...]
        dy = err * (1.0 / D)
        dr2 = _ln_bwd(dy * l2g_ref[...], xh2, rstd2)
        dxa_ref[...] = ALPHA * dr2
        dh_ref[...] = (g2_ref[...] * dr2).astype(MXU)
        acc_ref[0:1, :] += _colsum(dr2 * h)
        acc_ref[1:2, :] += _colsum(dy * xh2)
        acc_ref[2:3, :] += _colsum(dy)
        acc_ref[3:4, :] += _colsum(err * err) * (0.5 / D)

    vec = _fixed(1, D)
    return pl.pallas_call(
        body, name="ffn_fwd", grid=(s // tm,),
        in_specs=[_rows(tm, D), _rows(tm, D), _rows(tm, D), _res(w_g), _res(w_u), _res(w_d), vec, vec, vec, vec, vec],
        out_specs=[_rows(tm, D_FF), _rows(tm, D_FF), _rows(tm, D), _rows(tm, D), _fixed(8, D)],
        out_shape=[jax.ShapeDtypeStruct((s, D_FF), F32), jax.ShapeDtypeStruct((s, D_FF), F32),
                   jax.ShapeDtypeStruct((s, D), F32), jax.ShapeDtypeStruct((s, D), MXU), jax.ShapeDtypeStruct((8, D), F32)],
        compiler_params=_params(VMEM_BIG),
    )(r1, u2, tgt, w_g, w_u, w_d, g2, ln1_g, ln1_b, ln2_g, ln2_b)


def _ffn_bwd(dh, hg, hu, w_g, w_u, w_d):
    s = dh.shape[0]
    tm = 256
    half = D_FF // 2

    def body(dh_ref, hg_ref, hu_ref, wg_ref, wu_ref, wd_ref, act_ref, dhg_ref, dhu_ref, du2_ref):
        dh = dh_ref[...]
        du2 = jnp.zeros((tm, D), F32)
        for c0 in (0, half):
            cols = slice(c0, c0 + half)
            dact = _dot_nt(dh, wd_ref[cols, :])
            hg = hg_ref[:, cols]
            hu = hu_ref[:, cols]
            sg = _sigmoid(hg)
            sl = hg * sg
            act_ref[:, cols] = (sl * hu).astype(MXU)
            dhg = (dact * hu * (sg * (1.0 + hg * (1.0 - sg)))).astype(MXU)
            dhu = (dact * sl).astype(MXU)
            dhg_ref[:, cols] = dhg
            dhu_ref[:, cols] = dhu
            du2 = du2 + _dot(dhg, wg_ref[cols, :]) + _dot(dhu, wu_ref[cols, :])
        du2_ref[...] = du2

    return pl.pallas_call(
        body, name="ffn_bwd", grid=(s // tm,),
        in_specs=[_rows(tm, D), _rows(tm, D_FF), _rows(tm, D_FF), _res(w_g), _res(w_u), _res(w_d)],
        out_specs=[_rows(tm, D_FF), _rows(tm, D_FF), _rows(tm, D_FF), _rows(tm, D)],
        out_shape=[jax.ShapeDtypeStruct((s, D_FF), MXU)] * 3 + [jax.ShapeDtypeStruct((s, D), F32)],
        compiler_params=_params(VMEM_BIG),
    )(dh, hg, hu, w_g, w_u, w_d)


def _mix_bwd(du2, dxa, r1, o_sb, o_fx, gl, w_sb, w_fx, w_o, g1, ln1_g, ln1_b, sc2):
    s = r1.shape[0]
    tm = 256

    def body(du2_ref, dxa_ref, r1_ref, osb_ref, ofx_ref, gl_ref, wsb_ref, wfx_ref, wo_ref, g1_ref, lg_ref, lb_ref, sc_ref,
             dx_ref, mixin_ref, dmix_ref, dysb_ref, dyfx_ref, dosb_ref, dofx_ref, dgl_ref, dbg_ref, acc_ref):
        @pl.when(pl.program_id(0) == 0)
        def _():
            acc_ref[...] = jnp.zeros_like(acc_ref)
            dbg_ref[...] = jnp.zeros_like(dbg_ref)

        du2 = du2_ref[...]
        xh1, rstd1 = _ln(r1_ref[...])
        x1 = xh1 * lg_ref[...] + lb_ref[...]
        n1, rstdn = _ln(x1)
        dx1 = dxa_ref[...] + _ln_bwd(du2 * (1.0 + sc_ref[...]), n1, rstdn)
        dr1 = _ln_bwd(dx1 * lg_ref[...], xh1, rstd1)
        dx_ref[...] = ALPHA * dr1
        ysb = _dot(osb_ref[...], wsb_ref[...])
        yfx = _dot(ofx_ref[...], wfx_ref[...])
        gs = _sigmoid(gl_ref[:, :D])
        gf = _sigmoid(gl_ref[:, D:])
        mixin = (gs * ysb + gf * yfx).astype(MXU)
        mixin_ref[...] = mixin
        mix = _dot(mixin, wo_ref[...])
        dmix = (g1_ref[...] * dr1).astype(MXU)
        dmix_ref[...] = dmix
        dmixin = _dot_nt(dmix, wo_ref[...])
        dysb = (dmixin * gs).astype(MXU)
        dyfx = (dmixin * gf).astype(MXU)
        dysb_ref[...] = dysb
        dyfx_ref[...] = dyfx
        dosb_ref[...] = _dot_nt(dysb, wsb_ref[...]).astype(MXU)
        dofx_ref[...] = _dot_nt(dyfx, wfx_ref[...]).astype(MXU)
        dgs = dmixin * ysb * gs * (1.0 - gs)
        dgf = dmixin * yfx * gf * (1.0 - gf)
        dgl_ref[:, :D] = dgs.astype(MXU)
        dgl_ref[:, D:] = dgf.astype(MXU)
        dbg_ref[:, :D] += _colsum(dgs)
        dbg_ref[:, D:] += _colsum(dgf)
        acc_ref[0:1, :] += _colsum(du2)
        acc_ref[1:2, :] += _colsum(du2 * n1)
        acc_ref[2:3, :] += _colsum(dx1 * xh1)
        acc_ref[3:4, :] += _colsum(dx1)
        acc_ref[4:5, :] += _colsum(dr1 * mix)

    vec = _fixed(1, D)
    return pl.pallas_call(
        body, name="mix_bwd", grid=(s // tm,),
        in_specs=[_rows(tm, D), _rows(tm, D), _rows(tm, D), _rows(tm, WIDTH), _rows(tm, WIDTH), _rows(tm, 2 * D),
                  _res(w_sb), _res(w_fx), _res(w_o), vec, vec, vec, vec],
        out_specs=[_rows(tm, D), _rows(tm, D), _rows(tm, D), _rows(tm, D), _rows(tm, D), _rows(tm, WIDTH), _rows(tm, WIDTH),
                   _rows(tm, 2 * D), _fixed(1, 2 * D), _fixed(8, D)],
        out_shape=[jax.ShapeDtypeStruct((s, D), F32)] + [jax.ShapeDtypeStruct((s, D), MXU)] * 4
        + [jax.ShapeDtypeStruct((s, WIDTH), MXU)] * 2
        + [jax.ShapeDtypeStruct((s, 2 * D), MXU), jax.ShapeDtypeStruct((1, 2 * D), F32), jax.ShapeDtypeStruct((8, D), F32)],
        compiler_params=_params(VMEM_BIG),
    )(du2, dxa, r1, o_sb, o_fx, gl, w_sb, w_fx, w_o, g1, ln1_g, ln1_b, sc2)


def _in_bwd(pieces, x, dxa, w_all, sc1, riders=()):
    s = x.shape[0]
    tm = 256
    n_p = len(pieces)

    def body(*refs):
        p_refs = refs[:n_p]
        x_ref, dxa_ref, w_ref, sc_ref, gx_ref, acc_ref = refs[n_p:]

        @pl.when(pl.program_id(0) == 0)
        def _():
            acc_ref[...] = jnp.zeros_like(acc_ref)

        du1 = jnp.zeros((tm, D), F32)
        for p_ref, (arr, c0) in zip(p_refs, pieces):
            du1 = du1 + _dot(p_ref[...], w_ref[c0:c0 + arr.shape[1], :])
        n0, rstd0 = _ln(x_ref[...])
        gx_ref[...] = dxa_ref[...] + _ln_bwd(du1 * (1.0 + sc_ref[...]), n0, rstd0)
        acc_ref[0:1, :] += _colsum(du1)
        acc_ref[1:2, :] += _colsum(du1 * n0)

    return _call_with_riders(
        body, "in_bwd", (s // tm,),
        [_rows(tm, a.shape[1]) for a, _ in pieces] + [_rows(tm, D), _rows(tm, D), _res(w_all), _fixed(1, D)],
        [_rows(tm, D), _fixed(8, D)], [jax.ShapeDtypeStruct((s, D), F32), jax.ShapeDtypeStruct((8, D), F32)], [],
        (*[a for a, _ in pieces], x, dxa, w_all, sc1), riders, False, VMEM_BIG)


def _matmul_tn(a, b, name, narrow=False):
    s, m = a.shape
    n = b.shape[1]
    tm = 512 if m % 512 == 0 else (m if m < 512 else m // 2)
    tn = n // 2 if n > 2048 else n
    ts = 512
    assert m % tm == 0 and tm % LANE == 0 and n % tn == 0 and tn % LANE == 0 and s % ts == 0

    def body(a_ref, b_ref, o_ref, *narrow_ref):
        @pl.when(pl.program_id(2) == 0)
        def _():
            o_ref[...] = jnp.zeros_like(o_ref)

        o_ref[...] += _dot_tn(a_ref[...], b_ref[...])
        if narrow:
            @pl.when(pl.program_id(2) == s // ts - 1)
            def _():
                narrow_ref[0][...] = o_ref[...].astype(MXU)

    out_blk = pl.BlockSpec((tm, tn), lambda i, j, k: (i, j))
    res = pl.pallas_call(
        body, name=name, grid=(m // tm, n // tn, s // ts),
        in_specs=[pl.BlockSpec((ts, tm), lambda i, j, k: (k, i)), pl.BlockSpec((ts, tn), lambda i, j, k: (k, j))],
        out_specs=[out_blk] * (2 if narrow else 1),
        out_shape=[jax.ShapeDtypeStruct((m, n), F32)] + ([jax.ShapeDtypeStruct((m, n), MXU)] if narrow else []),
        compiler_params=_params(VMEM_BIG),
    )(a, b)
    return tuple(res) if narrow else res[0]


def _local_step(x, tgt, ada, w_all, b_gate, bf_pad, late_weights, early_grads, w_in_grads, ln1_g, ln1_b, ln2_g, ln2_b):
    sh1, sc1, g1, sh2, sc2, g2 = ada
    u1, qkv, fl, gl = _in_proj(x, sh1, sc1, w_all, b_gate)
    fc, fkt = _fcum_fwd(fl, bf_pad)
    late_riders, late_full = late_weights
    n_sb = 3
    (o_sb, rs), gathered_a = _sb_fwd(qkv, late_riders[:n_sb])
    (o_fx, lse), gathered_b = _fox_fwd(qkv, fc, fkt, late_riders[n_sb:])
    w_sb, w_fx, w_o, w_g, w_u, w_d = late_full(list(gathered_a) + list(gathered_b))
    r1, u2 = _mix_fwd(x, o_sb, o_fx, gl, w_sb, w_fx, w_o, g1, ln1_g, ln1_b, sh2, sc2)
    hg, hu, dxa2, dh, acc_f = _ffn_fwd(r1, u2, tgt, w_g, w_u, w_d, g2, ln1_g, ln1_b, ln2_g, ln2_b)
    act, dhg, dhu, du2 = _ffn_bwd(dh, hg, hu, w_g, w_u, w_d)
    dxa1, mixin, dmix, dysb, dyfx, dosb, dofx, dgl, dbg, acc_m = _mix_bwd(
        du2, dxa2, r1, o_sb, o_fx, gl, w_sb, w_fx, w_o, g1, ln1_g, ln1_b, sc2)
    early = dict(w_sb_out=_matmul_tn(o_sb, dysb, "dw_sb_out", True), w_fox_out=_matmul_tn(o_fx, dyfx, "dw_fox_out", True),
                 w_o=_matmul_tn(mixin, dmix, "dw_o", True), w_ffn_gate=_matmul_tn(dhg, u2, "dw_ffn_gate", True),
                 w_ffn_up=_matmul_tn(dhu, u2, "dw_ffn_up", True), w_ffn_down=_matmul_tn(act, dh, "dw_ffn_down", True))
    early_riders = early_grads(early)
    (dq_sb, dk_sb, dv_sb), received_a = _sb_bwd(qkv, dosb, rs, early_riders[:n_sb])
    (dq_fx, dk_fx, dv_fx, dfkt, dfq), received_b = _fox_bwd(qkv, fc, fkt, dofx, o_fx, lse, early_riders[n_sb:])
    early_received = list(received_a) + list(received_b)
    df, dbf = _fcum_bwd(dfkt, dfq, fl, bf_pad)
    pieces = [(dq_sb, 0), (dk_sb, WIDTH), (dv_sb, 2 * WIDTH), (dq_fx, 3 * WIDTH), (dk_fx, 4 * WIDTH), (dv_fx, 5 * WIDTH),
              (df, OFF_FGATE), (dgl, OFF_FGATE + LANE)]
    dw_in = [_matmul_tn(p, u1, f"dw_in_{j}") for j, (p, _) in enumerate(pieces)]
    (grad_x, acc_i), w_in_received = _in_bwd(pieces, x, dxa1, w_all, sc1, w_in_grads(dw_in))
    return dict(
        loss_lanes=acc_f[3:4], grad_x=grad_x, dw_in=dw_in, early=early, early_received=early_received,
        w_in_received=w_in_received,
        d_ada=[acc_i[0:1], acc_i[1:2], acc_m[4:5], acc_m[0:1], acc_m[1:2], acc_f[0:1]],
        dln1_g=acc_m[2:3], dln1_b=acc_m[3:4], dln2_g=acc_f[1:2], dln2_b=acc_f[2:3], db_gate=dbg, db_forget=dbf)


_MESH_ID = pl.DeviceIdType.MESH
_ANY = pl.BlockSpec(memory_space=pl.ANY)
_VMEM = pl.BlockSpec(memory_space=pltpu.VMEM)


def _mesh_pos():
    return lax.axis_index("x"), lax.axis_index("y"), lax.axis_index("c")


def _other_chips(x, y):
    return [(1 - x, y), (x, 1 - y), (1 - x, 1 - y)]


def _allgather_rows(v, name):
    n = v.shape[1]

    def body(v_ref, out_ref, send_sems, recv_sems, local_sem):
        x, y, c = _mesh_pos()
        me = 4 * x + 2 * y + c
        mine = pltpu.make_async_copy(v_ref, out_ref.at[me], local_sem)
        mine.start()
        copies = []
        for d in range(1, 8):
            fx, fy, fc = (d >> 2) & 1, (d >> 1) & 1, d & 1
            to = (1 - x if fx else x, 1 - y if fy else y, 1 - c if fc else c)
            cp = pltpu.make_async_remote_copy(src_ref=v_ref, dst_ref=out_ref.at[me], send_sem=send_sems.at[d - 1],
                                              recv_sem=recv_sems.at[d - 1], device_id=to, device_id_type=_MESH_ID)
            cp.start()
            copies.append(cp)
        for cp in copies:
            cp.wait_recv()
        for cp in copies:
            cp.wait_send()
        mine.wait()

    return pl.pallas_call(
        body, name=name, in_specs=[_VMEM], out_specs=_VMEM,
        out_shape=jax.ShapeDtypeStruct((8, 1, n), v.dtype),
        scratch_shapes=[pltpu.SemaphoreType.DMA((7,)), pltpu.SemaphoreType.DMA((7,)), pltpu.SemaphoreType.DMA(())],
    )(v)


def _chip_exchange(arrays, name, gather):
    nt = len(arrays)

    def body(*refs):
        ins, outs = refs[:nt], refs[nt:2 * nt]
        _exchange_start(ins, outs, refs[2 * nt:], gather)
        _exchange_wait(ins, outs, refs[2 * nt:], gather)

    return pl.pallas_call(
        body, name=name, in_specs=[_ANY] * nt, out_specs=[_ANY] * nt, out_shape=_exchange_out_shape(arrays),
        scratch_shapes=_exchange_sems(nt),
    )(*arrays)


def _exchange_out_shape(arrays):
    return [jax.ShapeDtypeStruct((4,) + a.shape[-2:], a.dtype) for a in arrays]


def _exchange_sems(nt):
    return [pltpu.SemaphoreType.DMA((3 * nt,)), pltpu.SemaphoreType.DMA((3 * nt,)), pltpu.SemaphoreType.DMA((nt,))]


def _exchange_copies(ins, outs, sems, gather):
    send_sems, recv_sems, local_sems = sems
    x, y, c = _mesh_pos()
    me = 2 * x + y
    local, remote = [], []
    for t in range(len(ins)):
        local.append(pltpu.make_async_copy(ins[t] if gather else ins[t].at[me], outs[t].at[me], local_sems.at[t]))
        for j, (px, py) in enumerate(_other_chips(x, y)):
            remote.append(pltpu.make_async_remote_copy(
                src_ref=ins[t] if gather else ins[t].at[2 * px + py], dst_ref=outs[t].at[me], send_sem=send_sems.at[3 * t + j],
                recv_sem=recv_sems.at[3 * t + j], device_id=(px, py, c), device_id_type=_MESH_ID))
    return local, remote


def _exchange_start(ins, outs, sems, gather):
    local, remote = _exchange_copies(ins, outs, sems, gather)
    for cp in local + remote:
        cp.start()


def _exchange_wait(ins, outs, sems, gather):
    local, remote = _exchange_copies(ins, outs, sems, gather)
    for cp in remote:
        cp.wait_recv()
    for cp in remote:
        cp.wait_send()
    for cp in local:
        cp.wait()


def _gather_two_level(shard, name):
    r, n = shard.shape
    half = n // 2
    assert half % LANE == 0

    def body(in_ref, out_ref, ici_send, ici_recv, d2d_send, d2d_recv, local_sem):
        x, y, c = _mesh_pos()
        me = 2 * x + y
        mine = pl.ds(pl.multiple_of(c * half, LANE), half)
        theirs = pl.ds(pl.multiple_of((1 - c) * half, LANE), half)
        local = pltpu.make_async_copy(in_ref, out_ref.at[me], local_sem)
        local.start()
        chips = _other_chips(x, y)
        over_ici = [pltpu.make_async_remote_copy(
            src_ref=in_ref.at[:, mine], dst_ref=out_ref.at[me, :, mine], send_sem=ici_send.at[j], recv_sem=ici_recv.at[j],
            device_id=(px, py, c), device_id_type=_MESH_ID) for j, (px, py) in enumerate(chips)]
        for cp in over_ici:
            cp.start()
        passed_on = [pltpu.make_async_remote_copy(
            src_ref=out_ref.at[2 * px + py, :, mine], dst_ref=out_ref.at[2 * px + py, :, mine], send_sem=d2d_send.at[j],
            recv_sem=d2d_recv.at[j], device_id=(x, y, 1 - c), device_id_type=_MESH_ID) for j, (px, py) in enumerate(chips)]
        for j, (px, py) in enumerate(chips):
            pltpu.make_async_remote_copy(
                src_ref=in_ref.at[:, mine], dst_ref=out_ref.at[2 * px + py, :, mine], send_sem=ici_send.at[j],
                recv_sem=ici_recv.at[j], device_id=(px, py, c), device_id_type=_MESH_ID).wait_recv()
            passed_on[j].start()
        for j, (px, py) in enumerate(chips):
            pltpu.make_async_remote_copy(
                src_ref=out_ref.at[2 * px + py, :, theirs], dst_ref=out_ref.at[2 * px + py, :, theirs], send_sem=d2d_send.at[j],
                recv_sem=d2d_recv.at[j], device_id=(x, y, 1 - c), device_id_type=_MESH_ID).wait_recv()
        for cp in over_ici + passed_on:
            cp.wait_send()
        local.wait()

    sems = pltpu.SemaphoreType.DMA((3,))
    return pl.pallas_call(
        body, name=name, in_specs=[_ANY], out_specs=_ANY, out_shape=jax.ShapeDtypeStruct((4, r, n), shard.dtype),
        scratch_shapes=[sems, sems, sems, sems, pltpu.SemaphoreType.DMA(())],
    )(shard)


def _sibling_exchange(arrays, name):
    nt = len(arrays)

    def body(*refs):
        ins, outs = refs[:nt], refs[nt:2 * nt]
        send_sems, recv_sems = refs[2 * nt:]
        x, y, c = _mesh_pos()
        copies = []
        for t in range(nt):
            cp = pltpu.make_async_remote_copy(src_ref=ins[t], dst_ref=outs[t], send_sem=send_sems.at[t], recv_sem=recv_sems.at[t],
                                              device_id=(x, y, 1 - c), device_id_type=_MESH_ID)
            cp.start()
            copies.append(cp)
        for cp in copies:
            cp.wait_recv()
        for cp in copies:
            cp.wait_send()

    return pl.pallas_call(
        body, name=name, in_specs=[_ANY] * nt, out_specs=[_ANY] * nt,
        out_shape=[jax.ShapeDtypeStruct(a.shape, a.dtype) for a in arrays],
        scratch_shapes=[pltpu.SemaphoreType.DMA((nt,)), pltpu.SemaphoreType.DMA((nt,))],
    )(*arrays)


def _tiles(r, n):
    for tr in (256, 352, 128):
        if r % tr == 0:
            return tr, n, r // tr, lambda i: (i, 0)
    assert n % 256 == 0
    return r, 256, n // 256, lambda i: (0, i)


def _reduce_chips(chip, pieces, recv, name):
    _, r, n = pieces.shape
    tr, tn, steps, at = _tiles(r, n)

    def body(chip_ref, own_ref, recv_ref, out_ref):
        me = chip_ref[0]
        total = jnp.zeros((tr, tn), F32)
        for k in range(4):
            total = total + jnp.where(me == k, own_ref[0], recv_ref[k].astype(F32))
        out_ref[...] = total

    return pl.pallas_call(
        body, name=name,
        grid_spec=pltpu.PrefetchScalarGridSpec(
            num_scalar_prefetch=1, grid=(steps,),
            in_specs=[pl.BlockSpec((1, tr, tn), lambda i, chip_ref: (chip_ref[0],) + at(i)),
                      pl.BlockSpec((4, tr, tn), lambda i, chip_ref: (0,) + at(i))],
            out_specs=pl.BlockSpec((tr, tn), lambda i, chip_ref: at(i))),
        out_shape=jax.ShapeDtypeStruct((r, n), F32),
    )(chip, pieces, recv)


def _adamw_math(w, g, m, v):
    m = ADAM_B1 * m + (1.0 - ADAM_B1) * g
    v = ADAM_B2 * v + (1.0 - ADAM_B2) * (g * g)
    m_hat = m / (1.0 - ADAM_B1 ** ADAM_STEP)
    v_hat = v / (1.0 - ADAM_B2 ** ADAM_STEP)
    return -ADAM_LR * (m_hat / (jnp.sqrt(v_hat) + ADAM_EPS) + ADAM_WD * w), m, v


def _adamw(w, m, v, g_parts, name):
    r, n = w.shape
    tr, tn, steps, at = _tiles(r, n)
    blk = pl.BlockSpec((tr, tn), at)
    ng = len(g_parts)

    def body(*refs):
        w_ref, m_ref, v_ref = refs[:3]
        g_refs = refs[3:3 + ng]
        g_out, d_out, m_out, v_out = refs[3 + ng:]
        g = g_refs[0][...]
        for gr in g_refs[1:]:
            g = g + gr[...]
        g_out[...] = g
        d_out[...], m_out[...], v_out[...] = _adamw_math(w_ref[...], g, m_ref[...], v_ref[...])

    return pl.pallas_call(
        body, name=name, grid=(steps,),
        in_specs=[blk] * (3 + ng), out_specs=[blk] * 4,
        out_shape=[jax.ShapeDtypeStruct((r, n), F32)] * 4,
    )(w, m, v, *g_parts)


def _ada_fwd(c_all, w_shard, b_shard):
    n = w_shard.shape[1]
    tn = 512

    def body(c_ref, w_ref, b_ref, o_ref):
        cv = c_ref[...]
        ca = (cv * _sigmoid(cv)).astype(MXU)
        o_ref[...] = _dot(ca, w_ref[...].astype(MXU)) + b_ref[...]

    return pl.pallas_call(
        body, name="ada_fwd", grid=(n // tn,),
        in_specs=[_fixed(8, D), pl.BlockSpec((D, tn), lambda j: (0, j)), pl.BlockSpec((1, tn), lambda j: (0, j))],
        out_specs=pl.BlockSpec((8, tn), lambda j: (0, j)),
        out_shape=jax.ShapeDtypeStruct((8, n), F32),
    )(c_all, w_shard, b_shard)


def _ada_bwd(c_all, dada_shard):
    n = dada_shard.shape[1]
    tn = 512

    def body(c_ref, d_ref, o_ref):
        cv = c_ref[...]
        ca = (cv * _sigmoid(cv)).astype(MXU)
        o_ref[...] = _dot_tn(ca, d_ref[...].astype(MXU))

    return pl.pallas_call(
        body, name="ada_bwd", grid=(n // tn,),
        in_specs=[_fixed(8, D), pl.BlockSpec((8, tn), lambda j: (0, j))],
        out_specs=pl.BlockSpec((D, tn), lambda j: (0, j)),
        out_shape=jax.ShapeDtypeStruct((D, n), F32),
    )(c_all, dada_shard)


_SMALL = [("d_ada", N_COND * D), ("ln1_g", D), ("ln1_b", D), ("ln2_g", D), ("ln2_b", D), ("b_gate", 2 * D), ("b_forget", LANE),
          ("loss", D)]
_SMALL_OFF = {}
_o = 0
for _n, _w in _SMALL:
    _SMALL_OFF[_n] = (_o, _w)
    _o += _w
_SMALL_LEN = _o
_SMALL_PARAMS = [("b_ada", "d_ada", N_COND * D), ("b_gate", "b_gate", 2 * D), ("b_forget", "b_forget", N_FGATE),
                 ("ln1_g", "ln1_g", D), ("ln1_b", "ln1_b", D), ("ln2_g", "ln2_g", D), ("ln2_b", "ln2_b", D)]


def _small_update(rows, params):
    npar = len(_SMALL_PARAMS)

    def body(*refs):
        rows_ref = refs[0]
        p_refs = refs[1:1 + 3 * npar]
        loss_ref = refs[1 + 3 * npar]
        o_refs = refs[2 + 3 * npar:]
        total = rows_ref[0]
        for d in range(1, 8):
            total = total + rows_ref[d]
        lo, lw = _SMALL_OFF["loss"]
        loss_ref[...] = jnp.sum(total[:, lo:lo + lw], axis=1, keepdims=True)
        for j, (_, key, n) in enumerate(_SMALL_PARAMS):
            off = _SMALL_OFF[key][0]
            g = total[:, off:off + n]
            w_ref, m_ref, v_ref = p_refs[3 * j:3 * j + 3]
            o_refs[4 * j][...] = g
            o_refs[4 * j + 1][...], o_refs[4 * j + 2][...], o_refs[4 * j + 3][...] = _adamw_math(w_ref[...], g, m_ref[...], v_ref[...])

    flat = [a for p in params for a in p]
    out_shape = [jax.ShapeDtypeStruct((1, 1), F32)] + [jax.ShapeDtypeStruct((1, n), F32) for _, _, n in _SMALL_PARAMS for _ in range(4)]
    return pl.pallas_call(body, name="small_update", out_shape=out_shape)(rows, *flat)


_BIG = [("w_in", "cols_t"), ("w_sb_out", "cols"), ("w_fox_out", "cols"), ("w_o", "rows"),
        ("w_ffn_gate", "cols_t"), ("w_ffn_up", "cols_t"), ("w_ffn_down", "rows")]


def _shard2d(a, how):
    return a[0].T if how == "cols_t" else a[0]


def _unshard(g, how):
    if how == "cols":
        return g.transpose(1, 0, 2).reshape(g.shape[1], 4 * g.shape[2])
    return g.reshape(4 * g.shape[1], g.shape[2])


def _reshard(w, how):
    if how == "cols":
        return w.reshape(w.shape[0], 4, w.shape[1] // 4).transpose(1, 0, 2)
    return w.reshape(4, w.shape[0] // 4, w.shape[1])


def kernel(x, c, w_ada, b_ada, w_in, b_gate, b_forget, w_sb_out, w_fox_out, w_o, ln1_g, ln1_b, w_ffn_gate, w_ffn_up, w_ffn_down, ln2_g, ln2_b, loss_target, m_w_ada, m_b_ada, m_w_in, m_b_gate, m_b_forget, m_w_sb_out, m_w_fox_out, m_w_o, m_ln1_g, m_ln1_b, m_w_ffn_gate, m_w_ffn_up, m_w_ffn_down, m_ln2_g, m_ln2_b, v_w_ada, v_b_ada, v_w_in, v_b_gate, v_b_forget, v_w_sb_out, v_w_fox_out, v_w_o, v_ln1_g, v_ln1_b, v_w_ffn_gate, v_w_ffn_up, v_w_ffn_down, v_ln2_g, v_ln2_b):
    given = dict(locals())
    mx, my, mc = _mesh_pos()
    chip = 2 * mx + my
    seq = 4 * mx + 2 * my + mc

    c_all = _allgather_rows(c, "gather_c").reshape(8, D)
    n_ada = w_ada.shape[2]
    b_ada_shard = lax.dynamic_slice(b_ada, (0, chip * n_ada), (1, n_ada))
    ada_part = _ada_fwd(c_all, w_ada[0], b_ada_shard)
    ada_all = _allgather_rows(ada_part.reshape(1, 8 * n_ada), "gather_ada").reshape(4, 2, 8, n_ada)
    ada_row = lax.dynamic_slice(ada_all, (0, mc, seq, 0), (4, 1, 1, n_ada)).reshape(1, N_COND * D)
    ada = [ada_row[:, j * D:(j + 1) * D] for j in range(N_COND)]

    w_in_g = _gather_two_level(_shard2d(w_in, "cols_t").astype(MXU), "gather_w_in")
    wi = _unshard(w_in_g, "cols_t")
    w_all = jnp.concatenate([wi[:OFF_FGATE + N_FGATE], jnp.zeros((LANE - N_FGATE, D), MXU), wi[OFF_FGATE + N_FGATE:]], axis=0)
    bf_pad = jnp.concatenate([b_forget, jnp.zeros((1, LANE - N_FGATE), F32)], axis=1)
    late = _BIG[1:]
    late_riders = [_shard2d(given[n], how).astype(MXU) for n, how in late]
    pieces = {}

    def late_full(gathered):
        return [_unshard(g, how) for (_, how), g in zip(late, gathered)]

    def early_grads(dw):
        for n, how in late:
            pieces[n] = _reshard(dw[n][0], how)
        return [_reshard(dw[n][1], how) for n, how in late]

    def w_in_grads(dwi):
        pieces["w_in"] = _reshard(jnp.concatenate(dwi[:6] + [dwi[6][:N_FGATE], dwi[7]], axis=0), "cols_t")
        return [pieces["w_in"].astype(MXU)]

    out = _local_step(x[0], loss_target[0], ada, w_all, b_gate, bf_pad, (late_riders, late_full), early_grads, w_in_grads,
                      ln1_g, ln1_b, ln2_g, ln2_b)

    row = jnp.concatenate(out["d_ada"] + [out["dln1_g"], out["dln1_b"], out["dln2_g"], out["dln2_b"], out["db_gate"],
                                          out["db_forget"], out["loss_lanes"]], axis=1)
    rows = _allgather_rows(row, "gather_small")
    small = _small_update(rows, [(given[p], given["m_" + p], given["v_" + p]) for p, _, _ in _SMALL_PARAMS])
    loss = small[0].reshape(())
    res = {}
    for j, (p, _, _) in enumerate(_SMALL_PARAMS):
        res[p] = small[1 + 4 * j:5 + 4 * j]

    dada_all = rows.reshape(8, _SMALL_LEN)[:, :N_COND * D]
    dada_shard = lax.dynamic_slice(dada_all, (0, chip * n_ada), (8, n_ada))
    g_ada = _ada_bwd(c_all, dada_shard)
    res["w_ada"] = [a[None] for a in _adamw(w_ada[0], m_w_ada[0], v_w_ada[0], [g_ada], "adamw_w_ada")]

    received = dict(zip([n for n, _ in late], out["early_received"]))
    (received["w_in"],) = out["w_in_received"]
    chip_arr = jnp.reshape(chip, (1,)).astype(jnp.int32)
    partial = [_reduce_chips(chip_arr, pieces[n], received[n], "reduce_" + n) for n, _ in _BIG]
    theirs = _sibling_exchange(partial, "swap_cores")
    for (n, how), mine, other in zip(_BIG, partial, theirs):
        upd = _adamw(_shard2d(given[n], how), _shard2d(given["m_" + n], how), _shard2d(given["v_" + n], how), [mine, other], "adamw_" + n)
        res[n] = [(a.T if how == "cols_t" else a)[None] for a in upd]

    order = ["w_ada", "b_ada", "w_in", "b_gate", "b_forget", "w_sb_out", "w_fox_out", "w_o", "ln1_g", "ln1_b",
             "w_ffn_gate", "w_ffn_up", "w_ffn_down", "ln2_g", "ln2_b"]
    return (loss, out["grad_x"][None], *[res[n][0] for n in order], *[res[n][1] for n in order],
            *[res[n][2] for n in order], *[res[n][3] for n in order])
```

```python
import functools

import jax
import jax.numpy as jnp
from jax import lax
from jax.experimental import pallas as pl
from jax.experimental.pallas import tpu as pltpu

F32 = jnp.float32
MXU = jnp.bfloat16

D = 1024
HEAD_DIM = 64
WIDTH = 512
D_FF = 2816
N_COND = 6
LN_EPS = 1e-5
ALPHA = 2.0 ** 0.25
QK_SCALE = HEAD_DIM ** -0.5
OFF_FGATE = 6 * WIDTH
N_FGATE = 8
IN_COLS = OFF_FGATE + N_FGATE + 2 * D
LANE = 128
W_ALL_COLS = OFF_FGATE + LANE + 2 * D
TQ = 512
SB_TQ = 256
ADAM_LR, ADAM_B1, ADAM_B2, ADAM_EPS, ADAM_WD, ADAM_STEP = 0.001, 0.9, 0.999, 1e-08, 0.01, 10
NEG = -1e30
DEAD_LOG = -120.0
RS_COUNT_LANE = LANE - 1
MESH_AXES = ("x", "y", "c")
VMEM_BIG = 56 * 1024 * 1024


def _dot(a, b):
    return jnp.dot(a, b, preferred_element_type=F32)


def _dot_nt(a, b):
    return lax.dot_general(a, b, (((1,), (1,)), ((), ())), preferred_element_type=F32)


def _dot_tn(a, b):
    return lax.dot_general(a, b, (((0,), (0,)), ((), ())), preferred_element_type=F32)


def _ln(x):
    mu = jnp.mean(x, axis=-1, keepdims=True)
    xc = x - mu
    var = jnp.mean(xc * xc, axis=-1, keepdims=True)
    rstd = lax.rsqrt(var + LN_EPS)
    return xc * rstd, rstd


def _ln_bwd(dxhat, xhat, rstd):
    return rstd * (dxhat - jnp.mean(dxhat, axis=-1, keepdims=True) - xhat * jnp.mean(dxhat * xhat, axis=-1, keepdims=True))


def _sigmoid(x):
    return 1.0 / (1.0 + jnp.exp(-x))


def _colsum(x):
    return jnp.sum(x, axis=0, keepdims=True)


def _split(x):
    hi = x.astype(MXU)
    lo = (x - hi.astype(F32)).astype(MXU)
    return jnp.concatenate([hi, lo], axis=1)


def _rows(tm, n):
    return pl.BlockSpec((tm, n), lambda i: (i, 0))


def _fixed(r, n):
    return pl.BlockSpec((r, n), lambda i: (0, 0))


def _res(a):
    return pl.BlockSpec(a.shape, lambda i: (0, 0), pipeline_mode=pl.Buffered(1))


def _params(limit=None, sem=None):
    return pltpu.CompilerParams(vmem_limit_bytes=limit, dimension_semantics=sem)


def _in_proj(x, sh1, sc1, w_all, b_gate):
    s = x.shape[0]
    tm = 256

    def body(x_ref, sh_ref, sc_ref, w_ref, bg_ref, u_ref, qkv_ref, fl_ref, gl_ref):
        xhat, _ = _ln(x_ref[...])
        u = (xhat * (1.0 + sc_ref[...]) + sh_ref[...]).astype(MXU)
        u_ref[...] = u
        for c0 in range(0, OFF_FGATE, WIDTH):
            p = _dot_nt(u, w_ref[c0:c0 + WIDTH, :])
            if c0 in (0, 3 * WIDTH):
                p = p * QK_SCALE
            qkv_ref[:, c0:c0 + WIDTH] = p.astype(MXU)
        fl_ref[...] = _dot_nt(u, w_ref[OFF_FGATE:OFF_FGATE + LANE, :])
        for c0 in range(0, 2 * D, D):
            gl_ref[:, c0:c0 + D] = _dot_nt(u, w_ref[OFF_FGATE + LANE + c0:OFF_FGATE + LANE + c0 + D, :]) + bg_ref[:, c0:c0 + D]

    return pl.pallas_call(
        body, name="in_proj", grid=(s // tm,),
        in_specs=[_rows(tm, D), _fixed(1, D), _fixed(1, D), _res(w_all), _fixed(1, 2 * D)],
        out_specs=[_rows(tm, D), _rows(tm, OFF_FGATE), _rows(tm, LANE), _rows(tm, 2 * D)],
        out_shape=[jax.ShapeDtypeStruct((s, D), MXU), jax.ShapeDtypeStruct((s, OFF_FGATE), MXU),
                   jax.ShapeDtypeStruct((s, LANE), F32), jax.ShapeDtypeStruct((s, 2 * D), F32)],
        compiler_params=_params(VMEM_BIG),
    )(x, sh1, sc1, w_all, b_gate)


def _log_sigmoid_parts(z):
    e = jnp.exp(-jnp.abs(z))
    return -(jnp.maximum(z, 0.0) + jnp.log(1.0 + e)), e


def _fcum_fwd(fl, bf):
    s = fl.shape[0]
    nb = s // LANE

    def body(fl_ref, bf_ref, fc_ref, fkt_ref):
        r = lax.broadcasted_iota(jnp.int32, (LANE, LANE), 0)
        c = lax.broadcasted_iota(jnp.int32, (LANE, LANE), 1)
        tri = (c <= r).astype(F32)

        def step(b, carry):
            r0 = pl.multiple_of(b * LANE, LANE)
            xb = fl_ref[pl.ds(r0, LANE), :] + bf_ref[...]
            ls = _log_sigmoid_parts(-xb)[0]
            cs = jnp.dot(tri, ls, precision=lax.Precision.HIGHEST, preferred_element_type=F32) + carry
            fc_ref[pl.ds(r0, LANE), :] = cs
            fkt_ref[b] = cs.T[:N_FGATE, :]
            return cs[LANE - 1:LANE, :]

        lax.fori_loop(0, nb, step, jnp.zeros((1, LANE), F32))

    return pl.pallas_call(
        body, name="fcum_fwd",
        out_shape=[jax.ShapeDtypeStruct((s, LANE), F32), jax.ShapeDtypeStruct((nb, N_FGATE, LANE), F32)],
    )(fl, bf)


def _attn_specs(s, col0, tq):
    return [pl.BlockSpec((tq, LANE), lambda hp, i: (i, col0 + hp)),
            pl.BlockSpec((s, LANE), lambda hp, i: (0, col0 + 4 + hp)),
            pl.BlockSpec((s, LANE), lambda hp, i: (0, col0 + 8 + hp))]


def _tile_iotas(tq):
    lane = lax.broadcasted_iota(jnp.int32, (tq, LANE), 1)
    row = lax.broadcasted_iota(jnp.int32, (tq, tq), 0)
    col = lax.broadcasted_iota(jnp.int32, (tq, tq), 1)
    return lane, row, col


def _sub_blocks(nk):
    return [slice(j * LANE, (j + 1) * LANE) for j in range(nk // LANE)]


def _over_strips(tile, tq):
    return tile(slice(0, tq), tq)


def _tri(below):
    r = lax.broadcasted_iota(jnp.int32, (LANE, LANE), 0)
    c = lax.broadcasted_iota(jnp.int32, (LANE, LANE), 1)
    t = jnp.concatenate([((r > c) if below else (r < c)).astype(MXU), jnp.ones((LANE, LANE), MXU)], axis=1)
    return jnp.concatenate([t, t], axis=0)


def _call_with_riders(body, name, grid, in_specs, out_specs, out_shape, scratch, args, riders, gather, limit=None):
    nr, n_in, n_out, n_sc = len(riders), len(in_specs), len(out_specs), len(scratch)

    def at_step(which):
        hit = None
        for d, n in enumerate(grid):
            here = pl.program_id(d) == (0 if which == "first" else n - 1)
            hit = here if hit is None else hit & here
        return hit

    def wrapped(*refs):
        ins, rin = refs[:n_in], refs[n_in:n_in + nr]
        outs, rout = refs[n_in + nr:n_in + nr + n_out], refs[n_in + nr + n_out:n_in + 2 * nr + n_out]
        own, sems = refs[n_in + 2 * nr + n_out:n_in + 2 * nr + n_out + n_sc], refs[n_in + 2 * nr + n_out + n_sc:]
        if nr:
            @pl.when(at_step("first"))
            def _():
                _exchange_start(rin, rout, sems, gather)

        body(*ins, *outs, *own)
        if nr:
            @pl.when(at_step("last"))
            def _():
                _exchange_wait(rin, rout, sems, gather)

    res = pl.pallas_call(
        wrapped, name=name, grid=grid,
        in_specs=list(in_specs) + [_ANY] * nr, out_specs=list(out_specs) + [_ANY] * nr,
        out_shape=list(out_shape) + _exchange_out_shape(riders),
        scratch_shapes=list(scratch) + (_exchange_sems(nr) if nr else []),
        compiler_params=_params(limit),
    )(*args, *riders)
    return res[:n_out], res[n_out:]


def _sb_fwd(qkv, riders=()):
    s = qkv.shape[0]
    tq = SB_TQ
    nq = s // tq
    assert nq <= RS_COUNT_LANE

    def body(q_ref, k_ref, v_ref, o_ref, rs_ref):
        i = pl.program_id(1)
        lane, row, col = _tile_iotas(tq)
        u2 = _tri(True)
        diag = col < row
        q = q_ref[...]
        qms = [jnp.where(hm, q, jnp.zeros_like(q)) for hm in (lane < HEAD_DIM, lane >= HEAD_DIM)]

        def step(kb, carry, masked):
            k0 = pl.multiple_of(kb * tq, tq)
            k = k_ref[pl.ds(k0, tq), :]
            v = v_ref[pl.ds(k0, tq), :]
            def tile(rows, nk, qm, state):
                run, acc, rt = (t[rows] for t in state)
                z = _dot_nt(qm[rows], k[:nk])
                lneg, _ = _log_sigmoid_parts(z)
                lpos = z + lneg
                if masked:
                    lneg = jnp.where(diag[rows, :nk], lneg, 0.0)
                rt = jnp.where(lane[rows] == kb, run, rt)
                a = []
                for sl in reversed(_sub_blocks(nk)):
                    st = _dot(_split(lneg[:, sl]), u2)
                    a.append(jnp.exp(lpos[:, sl] + st[:, :LANE] + run))
                    run = run + st[:, LANE:]
                a = jnp.concatenate(a[::-1], axis=1)
                if masked:
                    a = jnp.where(diag[rows, :nk], a, 0.0)
                return run, acc + _dot(a.astype(MXU), v[:nk]), rt

            return tuple(_over_strips(functools.partial(tile, qm=qm, state=state), tq) for qm, state in zip(qms, carry))

        zero = jnp.zeros((tq, LANE), F32)
        carry = step(i, ((zero, zero, zero),) * 2, True)

        def alive(cr):
            return jnp.maximum(jnp.max(cr[0][0]), jnp.max(cr[1][0])) > DEAD_LOG

        def walk(state):
            j, _, cr = state
            cr = step(i - 1 - j, cr, False)
            return j + 1, alive(cr), cr

        walked, _, carry = lax.while_loop(lambda state: (state[0] < i) & state[1], walk, (jnp.int32(0), alive(carry), carry))
        count = walked.astype(F32)
        rs_ref[0] = jnp.where(lane == RS_COUNT_LANE, count, carry[0][2])
        rs_ref[1] = jnp.where(lane == RS_COUNT_LANE, count, carry[1][2])
        o_ref[...] = jnp.where(lane < HEAD_DIM, carry[0][1], carry[1][1]).astype(o_ref.dtype)

    return _call_with_riders(
        body, "sb_fwd", (4, nq), _attn_specs(s, 0, tq),
        [pl.BlockSpec((tq, LANE), lambda hp, i: (i, hp)), pl.BlockSpec((2, tq, LANE), lambda hp, i: (hp, i, 0))],
        [jax.ShapeDtypeStruct((s, WIDTH), MXU), jax.ShapeDtypeStruct((8, s, LANE), F32)], [], (qkv, qkv, qkv), riders, True)


def _sb_bwd(qkv, do, rs, riders=()):
    s = qkv.shape[0]
    tq = SB_TQ
    nq = s // tq

    def body(q_ref, k_ref, v_ref, do_ref, rs_ref, dq_ref, dk_ref, dv_ref, dk_acc, dv_acc):
        i = pl.program_id(1)

        @pl.when(i == 0)
        def _():
            dk_acc[...] = jnp.zeros_like(dk_acc)
            dv_acc[...] = jnp.zeros_like(dv_acc)

        lane, row, col = _tile_iotas(tq)
        u2 = _tri(True)
        l2 = _tri(False)
        diag = col < row
        q = q_ref[...]
        do = do_ref[...]
        heads = [(jnp.where(hm, q, jnp.zeros_like(q)), jnp.where(hm, do, jnp.zeros_like(do)), rs_ref[hh])
                 for hh, hm in enumerate((lane < HEAD_DIM, lane >= HEAD_DIM))]

        def step(kb, carry, masked):
            k0 = pl.multiple_of(kb * tq, tq)
            k = k_ref[pl.ds(k0, tq), :]
            v = v_ref[pl.ds(k0, tq), :]
            to_keys = {}

            def tile(rows, nk, qm, dom, rblk, state):
                gpre, dq = (t[rows] for t in state)
                z = _dot_nt(qm[rows], k[:nk])
                lneg, e = _log_sigmoid_parts(z)
                lpos = z + lneg
                if masked:
                    lneg = jnp.where(diag[rows, :nk], lneg, 0.0)
                run = jnp.sum(jnp.where(lane[rows] == kb, rblk[rows], 0.0), axis=1, keepdims=True) + jnp.zeros_like(gpre)
                a = []
                for sl in reversed(_sub_blocks(nk)):
                    st = _dot(_split(lneg[:, sl]), u2)
                    a.append(jnp.exp(lpos[:, sl] + st[:, :LANE] + run))
                    run = run + st[:, LANE:]
                a = jnp.concatenate(a[::-1], axis=1)
                if masked:
                    a = jnp.where(diag[rows, :nk], a, 0.0)
                g = a * _dot_nt(dom[rows], v[:nk])
                pre = []
                for sl in _sub_blocks(nk):
                    pt = _dot(_split(g[:, sl]), l2)
                    pre.append(gpre + pt[:, :LANE])
                    gpre = gpre + pt[:, LANE:]
                sig = jnp.where(z >= 0.0, 1.0, e) / (1.0 + e)
                dz = g - (g + jnp.concatenate(pre, axis=1)) * sig
                if masked:
                    dz = jnp.where(diag[rows, :nk], dz, 0.0)
                dzb = dz.astype(MXU)
                both = to_keys.setdefault(nk, [0.0, 0.0])
                both[0] = both[0] + _dot_tn(dzb, qm[rows])
                both[1] = both[1] + _dot_tn(a.astype(MXU), dom[rows])
                return gpre, dq + _dot(dzb, k[:nk])

            new = tuple(_over_strips(functools.partial(tile, qm=qm, dom=dom, rblk=rblk, state=state), tq)
                        for (qm, dom, rblk), state in zip(heads, carry))
            for nk, (dk, dv) in to_keys.items():
                dk_acc[pl.ds(k0, nk), :] += dk
                dv_acc[pl.ds(k0, nk), :] += dv
            return new

        walked = jnp.max(jnp.where(lane[:8] == RS_COUNT_LANE, rs_ref[0, 0:8, :], 0.0))
        first = i - jnp.clip(walked.astype(jnp.int32), 0, i)
        zero = jnp.zeros((tq, LANE), F32)
        carry = step(i, lax.fori_loop(first, i, lambda kb, cr: step(kb, cr, False), ((zero, zero),) * 2), True)
        dq_ref[...] = (jnp.where(lane < HEAD_DIM, carry[0][1], carry[1][1]) * QK_SCALE).astype(dq_ref.dtype)

        @pl.when(i == nq - 1)
        def _():
            dk_ref[...] = dk_acc[...].astype(dk_ref.dtype)
            dv_ref[...] = dv_acc[...].astype(dv_ref.dtype)

    blk = pl.BlockSpec((tq, LANE), lambda hp, i: (i, hp))
    whole = pl.BlockSpec((s, LANE), lambda hp, i: (0, hp))
    return _call_with_riders(
        body, "sb_bwd", (4, nq), _attn_specs(s, 0, tq) + [blk, pl.BlockSpec((2, tq, LANE), lambda hp, i: (hp, i, 0))],
        [blk, whole, whole], [jax.ShapeDtypeStruct((s, WIDTH), MXU)] * 3,
        [pltpu.VMEM((s, LANE), F32), pltpu.VMEM((s, LANE), F32)], (qkv, qkv, qkv, do, rs), riders, False)


def _key_bias(fkt_ref, kb, h, tq):
    n_sub = tq // LANE
    return jnp.concatenate([fkt_ref[kb * n_sub + j, pl.ds(h, 1), :] for j in range(n_sub)], axis=1)


def _fox_fwd(qkv, fc, fkt, riders=()):
    s = qkv.shape[0]
    tq = TQ
    nq = s // tq
    nb = fkt.shape[0]

    def body(q_ref, k_ref, v_ref, fq_ref, fkt_ref, o_ref, lse_ref):
        hp = pl.program_id(0)
        i = pl.program_id(1)
        lane, row, col = _tile_iotas(tq)
        diag = col <= row
        q = q_ref[...]
        fqb = fq_ref[...]
        heads = []
        for hh in range(2):
            h = 2 * hp + hh
            hm = (lane >= HEAD_DIM) if hh else (lane < HEAD_DIM)
            heads.append((h, jnp.where(hm, q, jnp.zeros_like(q)), jnp.sum(jnp.where(lane == h, fqb, 0.0), axis=1, keepdims=True)))

        def step(kb, carry, masked):
            k0 = pl.multiple_of(kb * tq, tq)
            k = k_ref[pl.ds(k0, tq), :]
            v = v_ref[pl.ds(k0, tq), :]
            def tile(rows, nk, h, qm, fq, state):
                m, l, acc = (t[rows] for t in state)
                z = _dot_nt(qm[rows], k[:nk]) + fq[rows] - _key_bias(fkt_ref, kb, h, tq)[:, :nk]
                if masked:
                    z = jnp.where(diag[rows, :nk], z, NEG)
                mn = jnp.maximum(m, jnp.max(z, axis=1, keepdims=True))
                p = jnp.exp(z - mn)
                alpha = jnp.exp(m - mn)
                return mn, alpha * l + jnp.sum(p, axis=1, keepdims=True), alpha * acc + _dot(p.astype(MXU), v[:nk])

            return tuple(_over_strips(functools.partial(tile, h=h, qm=qm, fq=fq, state=state), tq)
                         for (h, qm, fq), state in zip(heads, carry))

        init = ((jnp.full((tq, 1), NEG, F32), jnp.zeros((tq, 1), F32), jnp.zeros((tq, LANE), F32)),) * 2
        carry = step(i, lax.fori_loop(0, i, lambda kb, cr: step(kb, cr, False), init), True)
        outs = []
        for hh, (m, l, acc) in enumerate(carry):
            outs.append(acc / l)
            lse_ref[hh] = jnp.broadcast_to(m + jnp.log(l), (tq, LANE))
        o_ref[...] = jnp.where(lane < HEAD_DIM, outs[0], outs[1]).astype(o_ref.dtype)

    return _call_with_riders(
        body, "fox_fwd", (4, nq),
        _attn_specs(s, 12, tq) + [pl.BlockSpec((tq, LANE), lambda hp, i: (i, 0)), pl.BlockSpec((nb, N_FGATE, LANE), lambda hp, i: (0, 0, 0))],
        [pl.BlockSpec((tq, LANE), lambda hp, i: (i, hp)), pl.BlockSpec((2, tq, LANE), lambda hp, i: (hp, i, 0))],
        [jax.ShapeDtypeStruct((s, WIDTH), MXU), jax.ShapeDtypeStruct((8, s, LANE), F32)], [], (qkv, qkv, qkv, fc, fkt), riders, True)


def _fox_bwd(qkv, fc, fkt, do, o, lse, riders=()):
    s = qkv.shape[0]
    tq = TQ
    nq = s // tq
    nb = fkt.shape[0]

    def body(q_ref, k_ref, v_ref, fq_ref, fkt_ref, do_ref, o_ref, lse_ref, dq_ref, dk_ref, dv_ref, dfk_ref, dfq_ref, dk_acc, dv_acc):
        hp = pl.program_id(0)
        i = pl.program_id(1)

        @pl.when(i == 0)
        def _():
            dk_acc[...] = jnp.zeros_like(dk_acc)
            dv_acc[...] = jnp.zeros_like(dv_acc)

        @pl.when((i == 0) & (hp == 0))
        def _():
            dfk_ref[...] = jnp.zeros_like(dfk_ref)

        lane, row, col = _tile_iotas(tq)
        diag = col <= row
        q = q_ref[...]
        do = do_ref[...]
        dof = do.astype(F32) * o_ref[...].astype(F32)
        fqb = fq_ref[...]
        heads = []
        for hh in range(2):
            h = 2 * hp + hh
            hm = (lane >= HEAD_DIM) if hh else (lane < HEAD_DIM)
            heads.append((h, jnp.where(hm, q, jnp.zeros_like(q)), jnp.where(hm, do, jnp.zeros_like(do)),
                          jnp.sum(jnp.where(hm, dof, 0.0), axis=1, keepdims=True),
                          jnp.sum(jnp.where(lane == h, fqb, 0.0), axis=1, keepdims=True), lse_ref[hh][:, :1]))

        def step(kb, carry, masked):
            k0 = pl.multiple_of(kb * tq, tq)
            k = k_ref[pl.ds(k0, tq), :]
            v = v_ref[pl.ds(k0, tq), :]
            to_keys = {}

            def tile(rows, nk, h, qm, dom, delta, fq, lse_t, state):
                dq, rsum = (t[rows] for t in state)
                z = _dot_nt(qm[rows], k[:nk]) + fq[rows] - _key_bias(fkt_ref, kb, h, tq)[:, :nk]
                if masked:
                    z = jnp.where(diag[rows, :nk], z, NEG)
                p = jnp.exp(z - lse_t[rows])
                ds = p * (_dot_nt(dom[rows], v[:nk]) - delta[rows])
                dsb = ds.astype(MXU)
                both = to_keys.setdefault(nk, [0.0, 0.0])
                both[0] = both[0] + _dot_tn(dsb, qm[rows])
                both[1] = both[1] + _dot_tn(p.astype(MXU), dom[rows])
                csum = _colsum(ds)
                for j, sl in enumerate(_sub_blocks(nk)):
                    dfk_ref[kb * (tq // LANE) + j, pl.ds(h, 1), :] += -csum[:, sl]
                return dq + _dot(dsb, k[:nk]), rsum + jnp.sum(ds, axis=1, keepdims=True)

            new = tuple(_over_strips(functools.partial(tile, h=h, qm=qm, dom=dom, delta=delta, fq=fq, lse_t=lse_t, state=state), tq)
                        for (h, qm, dom, delta, fq, lse_t), state in zip(heads, carry))
            for nk, (dk, dv) in to_keys.items():
                dk_acc[pl.ds(k0, nk), :] += dk
                dv_acc[pl.ds(k0, nk), :] += dv
            return new

        init = ((jnp.zeros((tq, LANE), F32), jnp.zeros((tq, 1), F32)),) * 2
        carry = step(i, lax.fori_loop(0, i, lambda kb, cr: step(kb, cr, False), init), True)
        dq_ref[...] = (jnp.where(lane < HEAD_DIM, carry[0][0], carry[1][0]) * QK_SCALE).astype(dq_ref.dtype)
        dfq_ref[0] = jnp.where(lane == heads[0][0], carry[0][1], jnp.where(lane == heads[1][0], carry[1][1], 0.0))

        @pl.when(i == nq - 1)
        def _():
            dk_ref[...] = dk_acc[...].astype(dk_ref.dtype)
            dv_ref[...] = dv_acc[...].astype(dv_ref.dtype)

    blk = pl.BlockSpec((tq, LANE), lambda hp, i: (i, hp))
    whole = pl.BlockSpec((s, LANE), lambda hp, i: (0, hp))
    pair = pl.BlockSpec((2, tq, LANE), lambda hp, i: (hp, i, 0))
    fkt_spec = pl.BlockSpec((nb, N_FGATE, LANE), lambda hp, i: (0, 0, 0))
    return _call_with_riders(
        body, "fox_bwd", (4, nq),
        _attn_specs(s, 12, tq) + [pl.BlockSpec((tq, LANE), lambda hp, i: (i, 0)), fkt_spec, blk, blk, pair],
        [blk, whole, whole, fkt_spec, pl.BlockSpec((1, tq, LANE), lambda hp, i: (hp, i, 0))],
        [jax.ShapeDtypeStruct((s, WIDTH), MXU)] * 3
        + [jax.ShapeDtypeStruct((nb, N_FGATE, LANE), F32), jax.ShapeDtypeStruct((4, s, LANE), F32)],
        [pltpu.VMEM((s, LANE), F32), pltpu.VMEM((s, LANE), F32)], (qkv, qkv, qkv, fc, fkt, do, o, lse), riders, False)


def _fcum_bwd(dfkt, dfq, fl, bf):
    s = fl.shape[0]
    nb = s // LANE

    def body(dfkt_ref, dfq_ref, fl_ref, bf_ref, df_ref, dbf_ref, tail_ref):
        @pl.when(pl.program_id(0) == 0)
        def _():
            tail_ref[...] = jnp.zeros_like(tail_ref)
            dbf_ref[...] = jnp.zeros_like(dbf_ref)

        r = lax.broadcasted_iota(jnp.int32, (LANE, LANE), 0)
        c = lax.broadcasted_iota(jnp.int32, (LANE, LANE), 1)
        tri = (c >= r).astype(F32)
        dfc = jnp.concatenate([dfkt_ref[0], jnp.zeros((LANE - N_FGATE, LANE), F32)], axis=0).T
        dfc = dfc + ((dfq_ref[0] + dfq_ref[1]) + (dfq_ref[2] + dfq_ref[3]))
        dls = jnp.dot(tri, dfc, precision=lax.Precision.HIGHEST, preferred_element_type=F32) + tail_ref[...]
        xb = fl_ref[...] + bf_ref[...]
        e = jnp.exp(-jnp.abs(xb))
        dfl = dls * (jnp.where(xb >= 0.0, e, 1.0) / (1.0 + e))
        df_ref[...] = dfl.astype(df_ref.dtype)
        tail_ref[...] = dls[0:1, :]
        dbf_ref[...] += _colsum(dfl)

    return pl.pallas_call(
        body, name="fcum_bwd", grid=(nb,),
        in_specs=[pl.BlockSpec((1, N_FGATE, LANE), lambda j: (nb - 1 - j, 0, 0)), pl.BlockSpec((4, LANE, LANE), lambda j: (0, nb - 1 - j, 0)),
                  pl.BlockSpec((LANE, LANE), lambda j: (nb - 1 - j, 0)), _fixed(1, LANE)],
        out_specs=[pl.BlockSpec((LANE, LANE), lambda j: (nb - 1 - j, 0)), _fixed(1, LANE)],
        out_shape=[jax.ShapeDtypeStruct((s, LANE), MXU), jax.ShapeDtypeStruct((1, LANE), F32)],
        scratch_shapes=[pltpu.VMEM((1, LANE), F32)],
    )(dfkt, dfq, fl, bf)


def _mix_fwd(x, o_sb, o_fx, gl, w_sb, w_fx, w_o, g1, ln1_g, ln1_b, sh2, sc2):
    s = x.shape[0]
    tm = 256

    def body(x_ref, osb_ref, ofx_ref, gl_ref, wsb_ref, wfx_ref, wo_ref, g1_ref, lg_ref, lb_ref, sh_ref, sc_ref, r1_ref, u2_ref):
        mixin = (_sigmoid(gl_ref[:, :D]) * _dot(osb_ref[...], wsb_ref[...])
                 + _sigmoid(gl_ref[:, D:]) * _dot(ofx_ref[...], wfx_ref[...]))
        r1 = ALPHA * x_ref[...] + g1_ref[...] * _dot(mixin.astype(MXU), wo_ref[...])
        r1_ref[...] = r1
        x1 = _ln(r1)[0] * lg_ref[...] + lb_ref[...]
        u2_ref[...] = (_ln(x1)[0] * (1.0 + sc_ref[...]) + sh_ref[...]).astype(MXU)

    vec = _fixed(1, D)
    return pl.pallas_call(
        body, name="mix_fwd", grid=(s // tm,),
        in_specs=[_rows(tm, D), _rows(tm, WIDTH), _rows(tm, WIDTH), _rows(tm, 2 * D), _res(w_sb), _res(w_fx), _res(w_o),
                  vec, vec, vec, vec, vec],
        out_specs=[_rows(tm, D), _rows(tm, D)],
        out_shape=[jax.ShapeDtypeStruct((s, D), F32), jax.ShapeDtypeStruct((s, D), MXU)],
        compiler_params=_params(VMEM_BIG),
    )(x, o_sb, o_fx, gl, w_sb, w_fx, w_o, g1, ln1_g, ln1_b, sh2, sc2)


def _ffn_fwd(r1, u2, tgt, w_g, w_u, w_d, g2, ln1_g, ln1_b, ln2_g, ln2_b):
    s = r1.shape[0]
    tm = 256

    def body(r1_ref, u2_ref, t_ref, wg_ref, wu_ref, wd_ref, g2_ref, l1g_ref, l1b_ref, l2g_ref, l2b_ref,
             hg_ref, hu_ref, dxa_ref, dh_ref, acc_ref):
        @pl.when(pl.program_id(0) == 0)
        def _():
            acc_ref[...] = jnp.zeros_like(acc_ref)

        u2 = u2_ref[...]
        hg = _dot_nt(u2, wg_ref[...])
        hu = _dot_nt(u2, wu_ref[...])
        hg_ref[...] = hg
        hu_ref[...] = hu
        h = _dot((hg * _sigmoid(hg) * hu).astype(MXU), wd_ref[...])
        x1 = _ln(r1_ref[...])[0] * l1g_ref[...] + l1b_ref[...]
        xh2, rstd2 = _ln(ALPHA * x1 + g2_ref[...] * h)
        err = xh2 * l2g_ref[...] + l2b_ref[...] - t_ref[...]
        dy = err * (1.0 / D)
        dr2 = _ln_bwd(dy * l2g_ref[...], xh2, rstd2)
        dxa_ref[...] = ALPHA * dr2
        dh_ref[...] = (g2_ref[...] * dr2).astype(MXU)
        acc_ref[0:1, :] += _colsum(dr2 * h)
        acc_ref[1:2, :] += _colsum(dy * xh2)
        acc_ref[2:3, :] += _colsum(dy)
        acc_ref[3:4, :] += _colsum(err * err) * (0.5 / D)

    vec = _fixed(1, D)
    return pl.pallas_call(
        body, name="ffn_fwd", grid=(s // tm,),
        in_specs=[_rows(tm, D), _rows(tm, D), _rows(tm, D), _res(w_g), _res(w_u), _res(w_d), vec, vec, vec, vec, vec],
        out_specs=[_rows(tm, D_FF), _rows(tm, D_FF), _rows(tm, D), _rows(tm, D), _fixed(8, D)],
        out_shape=[jax.ShapeDtypeStruct((s, D_FF), F32), jax.ShapeDtypeStruct((s, D_FF), F32),
                   jax.ShapeDtypeStruct((s, D), F32), jax.ShapeDtypeStruct((s, D), MXU), jax.ShapeDtypeStruct((8, D), F32)],
        compiler_params=_params(VMEM_BIG),
    )(r1, u2, tgt, w_g, w_u, w_d, g2, ln1_g, ln1_b, ln2_g, ln2_b)


def _ffn_bwd(dh, hg, hu, w_g, w_u, w_d):
    s = dh.shape[0]
    tm = 256
    half = D_FF // 2

    def body(dh_ref, hg_ref, hu_ref, wg_ref, wu_ref, wd_ref, act_ref, dhg_ref, dhu_ref, du2_ref):
        dh = dh_ref[...]
        du2 = jnp.zeros((tm, D), F32)
        for c0 in (0, half):
            cols = slice(c0, c0 + half)
            dact = _dot_nt(dh, wd_ref[cols, :])
            hg = hg_ref[:, cols]
            hu = hu_ref[:, cols]
            sg = _sigmoid(hg)
            sl = hg * sg
            act_ref[:, cols] = (sl * hu).astype(MXU)
            dhg = (dact * hu * (sg * (1.0 + hg * (1.0 - sg)))).astype(MXU)
            dhu = (dact * sl).astype(MXU)
            dhg_ref[:, cols] = dhg
            dhu_ref[:, cols] = dhu
            du2 = du2 + _dot(dhg, wg_ref[cols, :]) + _dot(dhu, wu_ref[cols, :])
        du2_ref[...] = du2

    return pl.pallas_call(
        body, name="ffn_bwd", grid=(s // tm,),
        in_specs=[_rows(tm, D), _rows(tm, D_FF), _rows(tm, D_FF), _res(w_g), _res(w_u), _res(w_d)],
        out_specs=[_rows(tm, D_FF), _rows(tm, D_FF), _rows(tm, D_FF), _rows(tm, D)],
        out_shape=[jax.ShapeDtypeStruct((s, D_FF), MXU)] * 3 + [jax.ShapeDtypeStruct((s, D), F32)],
        compiler_params=_params(VMEM_BIG),
    )(dh, hg, hu, w_g, w_u, w_d)


def _mix_bwd(du2, dxa, r1, o_sb, o_fx, gl, w_sb, w_fx, w_o, g1, ln1_g, ln1_b, sc2):
    s = r1.shape[0]
    tm = 256

    def body(du2_ref, dxa_ref, r1_ref, osb_ref, ofx_ref, gl_ref, wsb_ref, wfx_ref, wo_ref, g1_ref, lg_ref, lb_ref, sc_ref,
             dx_ref, mixin_ref, dmix_ref, dysb_ref, dyfx_ref, dosb_ref, dofx_ref, dgl_ref, dbg_ref, acc_ref):
        @pl.when(pl.program_id(0) == 0)
        def _():
            acc_ref[...] = jnp.zeros_like(acc_ref)
            dbg_ref[...] = jnp.zeros_like(dbg_ref)

        du2 = du2_ref[...]
        xh1, rstd1 = _ln(r1_ref[...])
        x1 = xh1 * lg_ref[...] + lb_ref[...]
        n1, rstdn = _ln(x1)
        dx1 = dxa_ref[...] + _ln_bwd(du2 * (1.0 + sc_ref[...]), n1, rstdn)
        dr1 = _ln_bwd(dx1 * lg_ref[...], xh1, rstd1)
        dx_ref[...] = ALPHA * dr1
        ysb = _dot(osb_ref[...], wsb_ref[...])
        yfx = _dot(ofx_ref[...], wfx_ref[...])
        gs = _sigmoid(gl_ref[:, :D])
        gf = _sigmoid(gl_ref[:, D:])
        mixin = (gs * ysb + gf * yfx).astype(MXU)
        mixin_ref[...] = mixin
        mix = _dot(mixin, wo_ref[...])
        dmix = (g1_ref[...] * dr1).astype(MXU)
        dmix_ref[...] = dmix
        dmixin = _dot_nt(dmix, wo_ref[...])
        dysb = (dmixin * gs).astype(MXU)
        dyfx = (dmixin * gf).astype(MXU)
        dysb_ref[...] = dysb
        dyfx_ref[...] = dyfx
        dosb_ref[...] = _dot_nt(dysb, wsb_ref[...]).astype(MXU)
        dofx_ref[...] = _dot_nt(dyfx, wfx_ref[...]).astype(MXU)
        dgs = dmixin * ysb * gs * (1.0 - gs)
        dgf = dmixin * yfx * gf * (1.0 - gf)
        dgl_ref[:, :D] = dgs.astype(MXU)
        dgl_ref[:, D:] = dgf.astype(MXU)
        dbg_ref[:, :D] += _colsum(dgs)
        dbg_ref[:, D:] += _colsum(dgf)
        acc_ref[0:1, :] += _colsum(du2)
        acc_ref[1:2, :] += _colsum(du2 * n1)
        acc_ref[2:3, :] += _colsum(dx1 * xh1)
        acc_ref[3:4, :] += _colsum(dx1)
        acc_ref[4:5, :] += _colsum(dr1 * mix)

    vec = _fixed(1, D)
    return pl.pallas_call(
        body, name="mix_bwd", grid=(s // tm,),
        in_specs=[_rows(tm, D), _rows(tm, D), _rows(tm, D), _rows(tm, WIDTH), _rows(tm, WIDTH), _rows(tm, 2 * D),
                  _res(w_sb), _res(w_fx), _res(w_o), vec, vec, vec, vec],
        out_specs=[_rows(tm, D), _rows(tm, D), _rows(tm, D), _rows(tm, D), _rows(tm, D), _rows(tm, WIDTH), _rows(tm, WIDTH),
                   _rows(tm, 2 * D), _fixed(1, 2 * D), _fixed(8, D)],
        out_shape=[jax.ShapeDtypeStruct((s, D), F32)] + [jax.ShapeDtypeStruct((s, D), MXU)] * 4
        + [jax.ShapeDtypeStruct((s, WIDTH), MXU)] * 2
        + [jax.ShapeDtypeStruct((s, 2 * D), MXU), jax.ShapeDtypeStruct((1, 2 * D), F32), jax.ShapeDtypeStruct((8, D), F32)],
        compiler_params=_params(VMEM_BIG),
    )(du2, dxa, r1, o_sb, o_fx, gl, w_sb, w_fx, w_o, g1, ln1_g, ln1_b, sc2)


def _in_bwd(pieces, x, dxa, w_all, sc1, riders=()):
    s = x.shape[0]
    tm = 256
    n_p = len(pieces)

    def body(*refs):
        p_refs = refs[:n_p]
        x_ref, dxa_ref, w_ref, sc_ref, gx_ref, acc_ref = refs[n_p:]

        @pl.when(pl.program_id(0) == 0)
        def _():
            acc_ref[...] = jnp.zeros_like(acc_ref)

        du1 = jnp.zeros((tm, D), F32)
        for p_ref, (arr, c0) in zip(p_refs, pieces):
            du1 = du1 + _dot(p_ref[...], w_ref[c0:c0 + arr.shape[1], :])
        n0, rstd0 = _ln(x_ref[...])
        gx_ref[...] = dxa_ref[...] + _ln_bwd(du1 * (1.0 + sc_ref[...]), n0, rstd0)
        acc_ref[0:1, :] += _colsum(du1)
        acc_ref[1:2, :] += _colsum(du1 * n0)

    return _call_with_riders(
        body, "in_bwd", (s // tm,),
        [_rows(tm, a.shape[1]) for a, _ in pieces] + [_rows(tm, D), _rows(tm, D), _res(w_all), _fixed(1, D)],
        [_rows(tm, D), _fixed(8, D)], [jax.ShapeDtypeStruct((s, D), F32), jax.ShapeDtypeStruct((8, D), F32)], [],
        (*[a for a, _ in pieces], x, dxa, w_all, sc1), riders, False, VMEM_BIG)


def _matmul_tn(a, b, name, narrow=False):
    s, m = a.shape
    n = b.shape[1]
    tm = 512 if m % 512 == 0 else (m if m < 512 else m // 2)
    tn = n // 2 if n > 2048 else n
    ts = 512
    assert m % tm == 0 and tm % LANE == 0 and n % tn == 0 and tn % LANE == 0 and s % ts == 0

    def body(a_ref, b_ref, o_ref, *narrow_ref):
        @pl.when(pl.program_id(2) == 0)
        def _():
            o_ref[...] = jnp.zeros_like(o_ref)

        o_ref[...] += _dot_tn(a_ref[...], b_ref[...])
        if narrow:
            @pl.when(pl.program_id(2) == s // ts - 1)
            def _():
                narrow_ref[0][...] = o_ref[...].astype(MXU)

    out_blk = pl.BlockSpec((tm, tn), lambda i, j, k: (i, j))
    res = pl.pallas_call(
        body, name=name, grid=(m // tm, n // tn, s // ts),
        in_specs=[pl.BlockSpec((ts, tm), lambda i, j, k: (k, i)), pl.BlockSpec((ts, tn), lambda i, j, k: (k, j))],
        out_specs=[out_blk] * (2 if narrow else 1),
        out_shape=[jax.ShapeDtypeStruct((m, n), F32)] + ([jax.ShapeDtypeStruct((m, n), MXU)] if narrow else []),
        compiler_params=_params(VMEM_BIG),
    )(a, b)
    return tuple(res) if narrow else res[0]


def _local_step(x, tgt, ada, w_all, b_gate, bf_pad, late_weights, early_grads, w_in_grads, ln1_g, ln1_b, ln2_g, ln2_b):
    sh1, sc1, g1, sh2, sc2, g2 = ada
    u1, qkv, fl, gl = _in_proj(x, sh1, sc1, w_all, b_gate)
    fc, fkt = _fcum_fwd(fl, bf_pad)
    late_riders, late_full = late_weights
    n_sb = 3
    (o_sb, rs), gathered_a = _sb_fwd(qkv, late_riders[:n_sb])
    (o_fx, lse), gathered_b = _fox_fwd(qkv, fc, fkt, late_riders[n_sb:])
    w_sb, w_fx, w_o, w_g, w_u, w_d = late_full(list(gathered_a) + list(gathered_b))
    r1, u2 = _mix_fwd(x, o_sb, o_fx, gl, w_sb, w_fx, w_o, g1, ln1_g, ln1_b, sh2, sc2)
    hg, hu, dxa2, dh, acc_f = _ffn_fwd(r1, u2, tgt, w_g, w_u, w_d, g2, ln1_g, ln1_b, ln2_g, ln2_b)
    act, dhg, dhu, du2 = _ffn_bwd(dh, hg, hu, w_g, w_u, w_d)
    dxa1, mixin, dmix, dysb, dyfx, dosb, dofx, dgl, dbg, acc_m = _mix_bwd(
        du2, dxa2, r1, o_sb, o_fx, gl, w_sb, w_fx, w_o, g1, ln1_g, ln1_b, sc2)
    early = dict(w_sb_out=_matmul_tn(o_sb, dysb, "dw_sb_out", True), w_fox_out=_matmul_tn(o_fx, dyfx, "dw_fox_out", True),
                 w_o=_matmul_tn(mixin, dmix, "dw_o", True), w_ffn_gate=_matmul_tn(dhg, u2, "dw_ffn_gate", True),
                 w_ffn_up=_matmul_tn(dhu, u2, "dw_ffn_up", True), w_ffn_down=_matmul_tn(act, dh, "dw_ffn_down", True))
    early_riders = early_grads(early)
    (dq_sb, dk_sb, dv_sb), received_a = _sb_bwd(qkv, dosb, rs, early_riders[:n_sb])
    (dq_fx, dk_fx, dv_fx, dfkt, dfq), received_b = _fox_bwd(qkv, fc, fkt, dofx, o_fx, lse, early_riders[n_sb:])
    early_received = list(received_a) + list(received_b)
    df, dbf = _fcum_bwd(dfkt, dfq, fl, bf_pad)
    pieces = [(dq_sb, 0), (dk_sb, WIDTH), (dv_sb, 2 * WIDTH), (dq_fx, 3 * WIDTH), (dk_fx, 4 * WIDTH), (dv_fx, 5 * WIDTH),
              (df, OFF_FGATE), (dgl, OFF_FGATE + LANE)]
    dw_in = [_matmul_tn(p, u1, f"dw_in_{j}") for j, (p, _) in enumerate(pieces)]
    (grad_x, acc_i), w_in_received = _in_bwd(pieces, x, dxa1, w_all, sc1, w_in_grads(dw_in))
    return dict(
        loss_lanes=acc_f[3:4], grad_x=grad_x, dw_in=dw_in, early=early, early_received=early_received,
        w_in_received=w_in_received,
        d_ada=[acc_i[0:1], acc_i[1:2], acc_m[4:5], acc_m[0:1], acc_m[1:2], acc_f[0:1]],
        dln1_g=acc_m[2:3], dln1_b=acc_m[3:4], dln2_g=acc_f[1:2], dln2_b=acc_f[2:3], db_gate=dbg, db_forget=dbf)


_MESH_ID = pl.DeviceIdType.MESH
_ANY = pl.BlockSpec(memory_space=pl.ANY)
_VMEM = pl.BlockSpec(memory_space=pltpu.VMEM)


def _mesh_pos():
    return lax.axis_index("x"), lax.axis_index("y"), lax.axis_index("c")


def _other_chips(x, y):
    return [(1 - x, y), (x, 1 - y), (1 - x, 1 - y)]


def _allgather_rows(v, name):
    n = v.shape[1]

    def body(v_ref, out_ref, send_sems, recv_sems, local_sem):
        x, y, c = _mesh_pos()
        me = 4 * x + 2 * y + c
        mine = pltpu.make_async_copy(v_ref, out_ref.at[me], local_sem)
        mine.start()
        copies = []
        for d in range(1, 8):
            fx, fy, fc = (d >> 2) & 1, (d >> 1) & 1, d & 1
            to = (1 - x if fx else x, 1 - y if fy else y, 1 - c if fc else c)
            cp = pltpu.make_async_remote_copy(src_ref=v_ref, dst_ref=out_ref.at[me], send_sem=send_sems.at[d - 1],
                                              recv_sem=recv_sems.at[d - 1], device_id=to, device_id_type=_MESH_ID)
            cp.start()
            copies.append(cp)
        for cp in copies:
            cp.wait_recv()
        for cp in copies:
            cp.wait_send()
        mine.wait()

    return pl.pallas_call(
        body, name=name, in_specs=[_VMEM], out_specs=_VMEM,
        out_shape=jax.ShapeDtypeStruct((8, 1, n), v.dtype),
        scratch_shapes=[pltpu.SemaphoreType.DMA((7,)), pltpu.SemaphoreType.DMA((7,)), pltpu.SemaphoreType.DMA(())],
    )(v)


def _chip_exchange(arrays, name, gather):
    nt = len(arrays)

    def body(*refs):
        ins, outs = refs[:nt], refs[nt:2 * nt]
        _exchange_start(ins, outs, refs[2 * nt:], gather)
        _exchange_wait(ins, outs, refs[2 * nt:], gather)

    return pl.pallas_call(
        body, name=name, in_specs=[_ANY] * nt, out_specs=[_ANY] * nt, out_shape=_exchange_out_shape(arrays),
        scratch_shapes=_exchange_sems(nt),
    )(*arrays)


def _exchange_out_shape(arrays):
    return [jax.ShapeDtypeStruct((4,) + a.shape[-2:], a.dtype) for a in arrays]


def _exchange_sems(nt):
    return [pltpu.SemaphoreType.DMA((3 * nt,)), pltpu.SemaphoreType.DMA((3 * nt,)), pltpu.SemaphoreType.DMA((nt,))]


def _exchange_copies(ins, outs, sems, gather):
    send_sems, recv_sems, local_sems = sems
    x, y, c = _mesh_pos()
    me = 2 * x + y
    local, remote = [], []
    for t in range(len(ins)):
        local.append(pltpu.make_async_copy(ins[t] if gather else ins[t].at[me], outs[t].at[me], local_sems.at[t]))
        for j, (px, py) in enumerate(_other_chips(x, y)):
            remote.append(pltpu.make_async_remote_copy(
                src_ref=ins[t] if gather else ins[t].at[2 * px + py], dst_ref=outs[t].at[me], send_sem=send_sems.at[3 * t + j],
                recv_sem=recv_sems.at[3 * t + j], device_id=(px, py, c), device_id_type=_MESH_ID))
    return local, remote


def _exchange_start(ins, outs, sems, gather):
    local, remote = _exchange_copies(ins, outs, sems, gather)
    for cp in local + remote:
        cp.start()


def _exchange_wait(ins, outs, sems, gather):
    local, remote = _exchange_copies(ins, outs, sems, gather)
    for cp in remote:
        cp.wait_recv()
    for cp in remote:
        cp.wait_send()
    for cp in local:
        cp.wait()


def _gather_two_level(shard, name):
    r, n = shard.shape
    half = n // 2
    assert half % LANE == 0

    def body(in_ref, out_ref, ici_send, ici_recv, d2d_send, d2d_recv, local_sem):
        x, y, c = _mesh_pos()
        me = 2 * x + y
        mine = pl.ds(pl.multiple_of(c * half, LANE), half)
        theirs = pl.ds(pl.multiple_of((1 - c) * half, LANE), half)
        local = pltpu.make_async_copy(in_ref, out_ref.at[me], local_sem)
        local.start()
        chips = _other_chips(x, y)
        over_ici = [pltpu.make_async_remote_copy(
            src_ref=in_ref.at[:, mine], dst_ref=out_ref.at[me, :, mine], send_sem=ici_send.at[j], recv_sem=ici_recv.at[j],
            device_id=(px, py, c), device_id_type=_MESH_ID) for j, (px, py) in enumerate(chips)]
        for cp in over_ici:
            cp.start()
        passed_on = [pltpu.make_async_remote_copy(
            src_ref=out_ref.at[2 * px + py, :, mine], dst_ref=out_ref.at[2 * px + py, :, mine], send_sem=d2d_send.at[j],
            recv_sem=d2d_recv.at[j], device_id=(x, y, 1 - c), device_id_type=_MESH_ID) for j, (px, py) in enumerate(chips)]
        for j, (px, py) in enumerate(chips):
            pltpu.make_async_remote_copy(
                src_ref=in_ref.at[:, mine], dst_ref=out_ref.at[2 * px + py, :, mine], send_sem=ici_send.at[j],
                recv_sem=ici_recv.at[j], device_id=(px, py, c), device_id_type=_MESH_ID).wait_recv()
            passed_on[j].start()
        for j, (px, py) in enumerate(chips):
            pltpu.make_async_remote_copy(
                src_ref=out_ref.at[2 * px + py, :, theirs], dst_ref=out_ref.at[2 * px + py, :, theirs], send_sem=d2d_send.at[j],
                recv_sem=d2d_recv.at[j], device_id=(x, y, 1 - c), device_id_type=_MESH_ID).wait_recv()
        for cp in over_ici + passed_on:
            cp.wait_send()
        local.wait()

    sems = pltpu.SemaphoreType.DMA((3,))
    return pl.pallas_call(
        body, name=name, in_specs=[_ANY], out_specs=_ANY, out_shape=jax.ShapeDtypeStruct((4, r, n), shard.dtype),
        scratch_shapes=[sems, sems, sems, sems, pltpu.SemaphoreType.DMA(())],
    )(shard)


def _swap_halves(pieces, name):
    _, r, n = pieces.shape
    half = n // 2
    assert half % LANE == 0

    def body(in_ref, out_ref, send_sem, recv_sem):
        x, y, c = _mesh_pos()
        theirs = pl.ds(pl.multiple_of((1 - c) * half, LANE), half)
        cp = pltpu.make_async_remote_copy(src_ref=in_ref.at[:, :, theirs], dst_ref=out_ref, send_sem=send_sem, recv_sem=recv_sem,
                                          device_id=(x, y, 1 - c), device_id_type=_MESH_ID)
        cp.start()
        cp.wait()

    return pl.pallas_call(
        body, name=name, in_specs=[_ANY], out_specs=_ANY, out_shape=jax.ShapeDtypeStruct((4, r, half), pieces.dtype),
        scratch_shapes=[pltpu.SemaphoreType.DMA(()), pltpu.SemaphoreType.DMA(())],
    )(pieces)


def _add_halves(core, pieces, theirs, name):
    _, r, n = pieces.shape
    half = n // 2
    tn = 256
    steps = half // tn

    def body(core_ref, own_ref, sib_ref, sum_ref, narrow_ref):
        total = own_ref[...] + sib_ref[...].astype(F32)
        sum_ref[...] = total
        narrow_ref[...] = total.astype(MXU)

    blk = pl.BlockSpec((1, r, tn), lambda k, j, core_ref: (k, 0, j))
    return pl.pallas_call(
        body, name=name,
        grid_spec=pltpu.PrefetchScalarGridSpec(
            num_scalar_prefetch=1, grid=(4, steps),
            in_specs=[pl.BlockSpec((1, r, tn), lambda k, j, core_ref: (k, 0, core_ref[0] * steps + j)), blk],
            out_specs=[blk, blk]),
        out_shape=[jax.ShapeDtypeStruct((4, r, half), F32), jax.ShapeDtypeStruct((4, r, half), MXU)],
    )(core, pieces, theirs)


def _join_halves(mine, name):
    r, half = mine.shape

    def body(in_ref, out_ref, send_sem, recv_sem, local_sem):
        x, y, c = _mesh_pos()
        cols = pl.ds(pl.multiple_of(c * half, LANE), half)
        local = pltpu.make_async_copy(in_ref, out_ref.at[:, cols], local_sem)
        local.start()
        cp = pltpu.make_async_remote_copy(src_ref=in_ref, dst_ref=out_ref.at[:, cols], send_sem=send_sem, recv_sem=recv_sem,
                                          device_id=(x, y, 1 - c), device_id_type=_MESH_ID)
        cp.start()
        cp.wait()
        local.wait()

    return pl.pallas_call(
        body, name=name, in_specs=[_ANY], out_specs=_ANY, out_shape=jax.ShapeDtypeStruct((r, 2 * half), mine.dtype),
        scratch_shapes=[pltpu.SemaphoreType.DMA(()), pltpu.SemaphoreType.DMA(()), pltpu.SemaphoreType.DMA(())],
    )(mine)


def _sibling_exchange(arrays, name):
    nt = len(arrays)

    def body(*refs):
        ins, outs = refs[:nt], refs[nt:2 * nt]
        send_sems, recv_sems = refs[2 * nt:]
        x, y, c = _mesh_pos()
        copies = []
        for t in range(nt):
            cp = pltpu.make_async_remote_copy(src_ref=ins[t], dst_ref=outs[t], send_sem=send_sems.at[t], recv_sem=recv_sems.at[t],
                                              device_id=(x, y, 1 - c), device_id_type=_MESH_ID)
            cp.start()
            copies.append(cp)
        for cp in copies:
            cp.wait_recv()
        for cp in copies:
            cp.wait_send()

    return pl.pallas_call(
        body, name=name, in_specs=[_ANY] * nt, out_specs=[_ANY] * nt,
        out_shape=[jax.ShapeDtypeStruct(a.shape, a.dtype) for a in arrays],
        scratch_shapes=[pltpu.SemaphoreType.DMA((nt,)), pltpu.SemaphoreType.DMA((nt,))],
    )(*arrays)


def _tiles(r, n):
    for tr in (256, 352, 128):
        if r % tr == 0:
            return tr, n, r // tr, lambda i: (i, 0)
    assert n % 256 == 0
    return r, 256, n // 256, lambda i: (0, i)


def _reduce_chips(chip, pieces, recv, name):
    _, r, n = pieces.shape
    tr, tn, steps, at = _tiles(r, n)

    def body(chip_ref, own_ref, recv_ref, out_ref):
        me = chip_ref[0]
        total = jnp.zeros((tr, tn), F32)
        for k in range(4):
            total = total + jnp.where(me == k, own_ref[0], recv_ref[k].astype(F32))
        out_ref[...] = total

    return pl.pallas_call(
        body, name=name,
        grid_spec=pltpu.PrefetchScalarGridSpec(
            num_scalar_prefetch=1, grid=(steps,),
            in_specs=[pl.BlockSpec((1, tr, tn), lambda i, chip_ref: (chip_ref[0],) + at(i)),
                      pl.BlockSpec((4, tr, tn), lambda i, chip_ref: (0,) + at(i))],
            out_specs=pl.BlockSpec((tr, tn), lambda i, chip_ref: at(i))),
        out_shape=jax.ShapeDtypeStruct((r, n), F32),
    )(chip, pieces, recv)


def _adamw_math(w, g, m, v):
    m = ADAM_B1 * m + (1.0 - ADAM_B1) * g
    v = ADAM_B2 * v + (1.0 - ADAM_B2) * (g * g)
    m_hat = m / (1.0 - ADAM_B1 ** ADAM_STEP)
    v_hat = v / (1.0 - ADAM_B2 ** ADAM_STEP)
    return -ADAM_LR * (m_hat / (jnp.sqrt(v_hat) + ADAM_EPS) + ADAM_WD * w), m, v


def _adamw(w, m, v, g_parts, name):
    r, n = w.shape
    tr, tn, steps, at = _tiles(r, n)
    blk = pl.BlockSpec((tr, tn), at)
    ng = len(g_parts)

    def body(*refs):
        w_ref, m_ref, v_ref = refs[:3]
        g_refs = refs[3:3 + ng]
        g_out, d_out, m_out, v_out = refs[3 + ng:]
        g = g_refs[0][...]
        for gr in g_refs[1:]:
            g = g + gr[...]
        g_out[...] = g
        d_out[...], m_out[...], v_out[...] = _adamw_math(w_ref[...], g, m_ref[...], v_ref[...])

    return pl.pallas_call(
        body, name=name, grid=(steps,),
        in_specs=[blk] * (3 + ng), out_specs=[blk] * 4,
        out_shape=[jax.ShapeDtypeStruct((r, n), F32)] * 4,
    )(w, m, v, *g_parts)


def _ada_fwd(c_all, w_shard, b_shard):
    n = w_shard.shape[1]
    tn = 512

    def body(c_ref, w_ref, b_ref, o_ref):
        cv = c_ref[...]
        ca = (cv * _sigmoid(cv)).astype(MXU)
        o_ref[...] = _dot(ca, w_ref[...].astype(MXU)) + b_ref[...]

    return pl.pallas_call(
        body, name="ada_fwd", grid=(n // tn,),
        in_specs=[_fixed(8, D), pl.BlockSpec((D, tn), lambda j: (0, j)), pl.BlockSpec((1, tn), lambda j: (0, j))],
        out_specs=pl.BlockSpec((8, tn), lambda j: (0, j)),
        out_shape=jax.ShapeDtypeStruct((8, n), F32),
    )(c_all, w_shard, b_shard)


def _ada_bwd(c_all, dada_shard):
    n = dada_shard.shape[1]
    tn = 512

    def body(c_ref, d_ref, o_ref):
        cv = c_ref[...]
        ca = (cv * _sigmoid(cv)).astype(MXU)
        o_ref[...] = _dot_tn(ca, d_ref[...].astype(MXU))

    return pl.pallas_call(
        body, name="ada_bwd", grid=(n // tn,),
        in_specs=[_fixed(8, D), pl.BlockSpec((8, tn), lambda j: (0, j))],
        out_specs=pl.BlockSpec((D, tn), lambda j: (0, j)),
        out_shape=jax.ShapeDtypeStruct((D, n), F32),
    )(c_all, dada_shard)


_SMALL = [("d_ada", N_COND * D), ("ln1_g", D), ("ln1_b", D), ("ln2_g", D), ("ln2_b", D), ("b_gate", 2 * D), ("b_forget", LANE),
          ("loss", D)]
_SMALL_OFF = {}
_o = 0
for _n, _w in _SMALL:
    _SMALL_OFF[_n] = (_o, _w)
    _o += _w
_SMALL_LEN = _o
_SMALL_PARAMS = [("b_ada", "d_ada", N_COND * D), ("b_gate", "b_gate", 2 * D), ("b_forget", "b_forget", N_FGATE),
                 ("ln1_g", "ln1_g", D), ("ln1_b", "ln1_b", D), ("ln2_g", "ln2_g", D), ("ln2_b", "ln2_b", D)]


def _small_update(rows, params):
    npar = len(_SMALL_PARAMS)

    def body(*refs):
        rows_ref = refs[0]
        p_refs = refs[1:1 + 3 * npar]
        loss_ref = refs[1 + 3 * npar]
        o_refs = refs[2 + 3 * npar:]
        total = rows_ref[0]
        for d in range(1, 8):
            total = total + rows_ref[d]
        lo, lw = _SMALL_OFF["loss"]
        loss_ref[...] = jnp.sum(total[:, lo:lo + lw], axis=1, keepdims=True)
        for j, (_, key, n) in enumerate(_SMALL_PARAMS):
            off = _SMALL_OFF[key][0]
            g = total[:, off:off + n]
            w_ref, m_ref, v_ref = p_refs[3 * j:3 * j + 3]
            o_refs[4 * j][...] = g
            o_refs[4 * j + 1][...], o_refs[4 * j + 2][...], o_refs[4 * j + 3][...] = _adamw_math(w_ref[...], g, m_ref[...], v_ref[...])

    flat = [a for p in params for a in p]
    out_shape = [jax.ShapeDtypeStruct((1, 1), F32)] + [jax.ShapeDtypeStruct((1, n), F32) for _, _, n in _SMALL_PARAMS for _ in range(4)]
    return pl.pallas_call(body, name="small_update", out_shape=out_shape)(rows, *flat)


_BIG = [("w_in", "cols_t"), ("w_sb_out", "cols"), ("w_fox_out", "cols"), ("w_o", "rows"),
        ("w_ffn_gate", "cols_t"), ("w_ffn_up", "cols_t"), ("w_ffn_down", "rows")]


def _shard2d(a, how):
    return a[0].T if how == "cols_t" else a[0]


def _unshard(g, how):
    if how == "cols":
        return g.transpose(1, 0, 2).reshape(g.shape[1], 4 * g.shape[2])
    return g.reshape(4 * g.shape[1], g.shape[2])


def _reshard(w, how):
    if how == "cols":
        return w.reshape(w.shape[0], 4, w.shape[1] // 4).transpose(1, 0, 2)
    return w.reshape(4, w.shape[0] // 4, w.shape[1])


def kernel(x, c, w_ada, b_ada, w_in, b_gate, b_forget, w_sb_out, w_fox_out, w_o, ln1_g, ln1_b, w_ffn_gate, w_ffn_up, w_ffn_down, ln2_g, ln2_b, loss_target, m_w_ada, m_b_ada, m_w_in, m_b_gate, m_b_forget, m_w_sb_out, m_w_fox_out, m_w_o, m_ln1_g, m_ln1_b, m_w_ffn_gate, m_w_ffn_up, m_w_ffn_down, m_ln2_g, m_ln2_b, v_w_ada, v_b_ada, v_w_in, v_b_gate, v_b_forget, v_w_sb_out, v_w_fox_out, v_w_o, v_ln1_g, v_ln1_b, v_w_ffn_gate, v_w_ffn_up, v_w_ffn_down, v_ln2_g, v_ln2_b):
    given = dict(locals())
    mx, my, mc = _mesh_pos()
    chip = 2 * mx + my
    seq = 4 * mx + 2 * my + mc

    c_all = _allgather_rows(c, "gather_c").reshape(8, D)
    n_ada = w_ada.shape[2]
    b_ada_shard = lax.dynamic_slice(b_ada, (0, chip * n_ada), (1, n_ada))
    ada_part = _ada_fwd(c_all, w_ada[0], b_ada_shard)
    ada_all = _allgather_rows(ada_part.reshape(1, 8 * n_ada), "gather_ada").reshape(4, 2, 8, n_ada)
    ada_row = lax.dynamic_slice(ada_all, (0, mc, seq, 0), (4, 1, 1, n_ada)).reshape(1, N_COND * D)
    ada = [ada_row[:, j * D:(j + 1) * D] for j in range(N_COND)]

    w_in_g = _gather_two_level(_shard2d(w_in, "cols_t").astype(MXU), "gather_w_in")
    wi = _unshard(w_in_g, "cols_t")
    w_all = jnp.concatenate([wi[:OFF_FGATE + N_FGATE], jnp.zeros((LANE - N_FGATE, D), MXU), wi[OFF_FGATE + N_FGATE:]], axis=0)
    bf_pad = jnp.concatenate([b_forget, jnp.zeros((1, LANE - N_FGATE), F32)], axis=1)
    late = _BIG[1:]
    late_riders = [_shard2d(given[n], how).astype(MXU) for n, how in late]
    pieces = {}

    def late_full(gathered):
        return [_unshard(g, how) for (_, how), g in zip(late, gathered)]

    def early_grads(dw):
        for n, how in late:
            pieces[n] = _reshard(dw[n][0], how)
        return [_reshard(dw[n][1], how) for n, how in late]

    def w_in_grads(dwi):
        full = _reshard(jnp.concatenate(dwi[:6] + [dwi[6][:N_FGATE], dwi[7]], axis=0), "cols_t")
        theirs = _swap_halves(full.astype(MXU), "presum_swap_w_in")
        pieces["w_in"], narrow = _add_halves(jnp.reshape(mc, (1,)).astype(jnp.int32), full, theirs, "presum_add_w_in")
        return [narrow]

    out = _local_step(x[0], loss_target[0], ada, w_all, b_gate, bf_pad, (late_riders, late_full), early_grads, w_in_grads,
                      ln1_g, ln1_b, ln2_g, ln2_b)

    row = jnp.concatenate(out["d_ada"] + [out["dln1_g"], out["dln1_b"], out["dln2_g"], out["dln2_b"], out["db_gate"],
                                          out["db_forget"], out["loss_lanes"]], axis=1)
    rows = _allgather_rows(row, "gather_small")
    small = _small_update(rows, [(given[p], given["m_" + p], given["v_" + p]) for p, _, _ in _SMALL_PARAMS])
    loss = small[0].reshape(())
    res = {}
    for j, (p, _, _) in enumerate(_SMALL_PARAMS):
        res[p] = small[1 + 4 * j:5 + 4 * j]

    dada_all = rows.reshape(8, _SMALL_LEN)[:, :N_COND * D]
    dada_shard = lax.dynamic_slice(dada_all, (0, chip * n_ada), (8, n_ada))
    g_ada = _ada_bwd(c_all, dada_shard)
    res["w_ada"] = [a[None] for a in _adamw(w_ada[0], m_w_ada[0], v_w_ada[0], [g_ada], "adamw_w_ada")]

    received = dict(zip([n for n, _ in late], out["early_received"]))
    (received["w_in"],) = out["w_in_received"]
    chip_arr = jnp.reshape(chip, (1,)).astype(jnp.int32)
    partial = [_reduce_chips(chip_arr, pieces[n], received[n], "reduce_" + n) for n, _ in _BIG]
    theirs = _sibling_exchange(partial[1:], "swap_cores")
    grads = {"w_in": [_join_halves(partial[0], "join_w_in")]}
    grads.update({n: [mine, other] for (n, _), mine, other in zip(late, partial[1:], theirs)})
    for n, how in _BIG:
        upd = _adamw(_shard2d(given[n], how), _shard2d(given["m_" + n], how), _shard2d(given["v_" + n], how), grads[n], "adamw_" + n)
        res[n] = [(a.T if how == "cols_t" else a)[None] for a in upd]

    order = ["w_ada", "b_ada", "w_in", "b_gate", "b_forget", "w_sb_out", "w_fox_out", "w_o", "ln1_g", "ln1_b",
             "w_ffn_gate", "w_ffn_up", "w_ffn_down", "ln2_g", "ln2_b"]
    return (loss, out["grad_x"][None], *[res[n][0] for n in order], *[res[n][1] for n in order],
            *[res[n][2] for n in order], *[res[n][3] for n in order])
```

```python
import functools

import jax
import jax.numpy as jnp
from jax import lax
from jax.experimental import pallas as pl
from jax.experimental.pallas import tpu as pltpu

F32 = jnp.float32
MXU = jnp.bfloat16

D = 1024
HEAD_DIM = 64
WIDTH = 512
D_FF = 2816
N_COND = 6
LN_EPS = 1e-5
ALPHA = 2.0 ** 0.25
QK_SCALE = HEAD_DIM ** -0.5
OFF_FGATE = 6 * WIDTH
N_FGATE = 8
IN_COLS = OFF_FGATE + N_FGATE + 2 * D
LANE = 128
W_ALL_COLS = OFF_FGATE + LANE + 2 * D
TQ = 512
SB_TQ = 256
ADAM_LR, ADAM_B1, ADAM_B2, ADAM_EPS, ADAM_WD, ADAM_STEP = 0.001, 0.9, 0.999, 1e-08, 0.01, 10
NEG = -1e30
DEAD_LOG = -120.0
RS_COUNT_LANE = LANE - 1
MESH_AXES = ("x", "y", "c")
VMEM_BIG = 56 * 1024 * 1024


def _dot(a, b):
    return jnp.dot(a, b, preferred_element_type=F32)


def _dot_nt(a, b):
    return lax.dot_general(a, b, (((1,), (1,)), ((), ())), preferred_element_type=F32)


def _dot_tn(a, b):
    return lax.dot_general(a, b, (((0,), (0,)), ((), ())), preferred_element_type=F32)


def _ln(x):
    mu = jnp.mean(x, axis=-1, keepdims=True)
    xc = x - mu
    var = jnp.mean(xc * xc, axis=-1, keepdims=True)
    rstd = lax.rsqrt(var + LN_EPS)
    return xc * rstd, rstd


def _ln_bwd(dxhat, xhat, rstd):
    return rstd * (dxhat - jnp.mean(dxhat, axis=-1, keepdims=True) - xhat * jnp.mean(dxhat * xhat, axis=-1, keepdims=True))


def _sigmoid(x):
    return 1.0 / (1.0 + jnp.exp(-x))


def _colsum(x):
    return jnp.sum(x, axis=0, keepdims=True)


def _split(x):
    hi = x.astype(MXU)
    lo = (x - hi.astype(F32)).astype(MXU)
    return jnp.concatenate([hi, lo], axis=1)


def _rows(tm, n):
    return pl.BlockSpec((tm, n), lambda i: (i, 0))


def _fixed(r, n):
    return pl.BlockSpec((r, n), lambda i: (0, 0))


def _res(a):
    return pl.BlockSpec(a.shape, lambda i: (0, 0), pipeline_mode=pl.Buffered(1))


def _params(limit=None, sem=None):
    return pltpu.CompilerParams(vmem_limit_bytes=limit, dimension_semantics=sem)


def _in_proj(x, sh1, sc1, w_all, b_gate):
    s = x.shape[0]
    tm = 256

    def body(x_ref, sh_ref, sc_ref, w_ref, bg_ref, u_ref, qkv_ref, fl_ref, gl_ref):
        xhat, _ = _ln(x_ref[...])
        u = (xhat * (1.0 + sc_ref[...]) + sh_ref[...]).astype(MXU)
        u_ref[...] = u
        for c0 in range(0, OFF_FGATE, WIDTH):
            p = _dot_nt(u, w_ref[c0:c0 + WIDTH, :])
            if c0 in (0, 3 * WIDTH):
                p = p * QK_SCALE
            qkv_ref[:, c0:c0 + WIDTH] = p.astype(MXU)
        fl_ref[...] = _dot_nt(u, w_ref[OFF_FGATE:OFF_FGATE + LANE, :])
        for c0 in range(0, 2 * D, D):
            gl_ref[:, c0:c0 + D] = _dot_nt(u, w_ref[OFF_FGATE + LANE + c0:OFF_FGATE + LANE + c0 + D, :]) + bg_ref[:, c0:c0 + D]

    return pl.pallas_call(
        body, name="in_proj", grid=(s // tm,),
        in_specs=[_rows(tm, D), _fixed(1, D), _fixed(1, D), _res(w_all), _fixed(1, 2 * D)],
        out_specs=[_rows(tm, D), _rows(tm, OFF_FGATE), _rows(tm, LANE), _rows(tm, 2 * D)],
        out_shape=[jax.ShapeDtypeStruct((s, D), MXU), jax.ShapeDtypeStruct((s, OFF_FGATE), MXU),
                   jax.ShapeDtypeStruct((s, LANE), F32), jax.ShapeDtypeStruct((s, 2 * D), F32)],
        compiler_params=_params(VMEM_BIG),
    )(x, sh1, sc1, w_all, b_gate)


def _log_sigmoid_parts(z):
    e = jnp.exp(-jnp.abs(z))
    return -(jnp.maximum(z, 0.0) + jnp.log(1.0 + e)), e


def _fcum_fwd(fl, bf):
    s = fl.shape[0]
    nb = s // LANE

    def body(fl_ref, bf_ref, fc_ref, fkt_ref):
        r = lax.broadcasted_iota(jnp.int32, (LANE, LANE), 0)
        c = lax.broadcasted_iota(jnp.int32, (LANE, LANE), 1)
        tri = (c <= r).astype(F32)

        def step(b, carry):
            r0 = pl.multiple_of(b * LANE, LANE)
            xb = fl_ref[pl.ds(r0, LANE), :] + bf_ref[...]
            ls = _log_sigmoid_parts(-xb)[0]
            cs = jnp.dot(tri, ls, precision=lax.Precision.HIGHEST, preferred_element_type=F32) + carry
            fc_ref[pl.ds(r0, LANE), :] = cs
            fkt_ref[b] = cs.T[:N_FGATE, :]
            return cs[LANE - 1:LANE, :]

        lax.fori_loop(0, nb, step, jnp.zeros((1, LANE), F32))

    return pl.pallas_call(
        body, name="fcum_fwd",
        out_shape=[jax.ShapeDtypeStruct((s, LANE), F32), jax.ShapeDtypeStruct((nb, N_FGATE, LANE), F32)],
    )(fl, bf)


def _attn_specs(s, col0, tq):
    return [pl.BlockSpec((tq, LANE), lambda hp, i: (i, col0 + hp)),
            pl.BlockSpec((s, LANE), lambda hp, i: (0, col0 + 4 + hp)),
            pl.BlockSpec((s, LANE), lambda hp, i: (0, col0 + 8 + hp))]


def _tile_iotas(tq):
    lane = lax.broadcasted_iota(jnp.int32, (tq, LANE), 1)
    row = lax.broadcasted_iota(jnp.int32, (tq, tq), 0)
    col = lax.broadcasted_iota(jnp.int32, (tq, tq), 1)
    return lane, row, col


def _sub_blocks(nk):
    return [slice(j * LANE, (j + 1) * LANE) for j in range(nk // LANE)]


def _over_strips(tile, tq):
    return tile(slice(0, tq), tq)


def _tri(below):
    r = lax.broadcasted_iota(jnp.int32, (LANE, LANE), 0)
    c = lax.broadcasted_iota(jnp.int32, (LANE, LANE), 1)
    t = jnp.concatenate([((r > c) if below else (r < c)).astype(MXU), jnp.ones((LANE, LANE), MXU)], axis=1)
    return jnp.concatenate([t, t], axis=0)


def _call_with_riders(body, name, grid, in_specs, out_specs, out_shape, scratch, args, riders, gather, limit=None):
    nr, n_in, n_out, n_sc = len(riders), len(in_specs), len(out_specs), len(scratch)

    def at_step(which):
        hit = None
        for d, n in enumerate(grid):
            here = pl.program_id(d) == (0 if which == "first" else n - 1)
            hit = here if hit is None else hit & here
        return hit

    def wrapped(*refs):
        ins, rin = refs[:n_in], refs[n_in:n_in + nr]
        outs, rout = refs[n_in + nr:n_in + nr + n_out], refs[n_in + nr + n_out:n_in + 2 * nr + n_out]
        own, sems = refs[n_in + 2 * nr + n_out:n_in + 2 * nr + n_out + n_sc], refs[n_in + 2 * nr + n_out + n_sc:]
        if nr:
            @pl.when(at_step("first"))
            def _():
                _exchange_start(rin, rout, sems, gather)

        body(*ins, *outs, *own)
        if nr:
            @pl.when(at_step("last"))
            def _():
                _exchange_wait(rin, rout, sems, gather)

    res = pl.pallas_call(
        wrapped, name=name, grid=grid,
        in_specs=list(in_specs) + [_ANY] * nr, out_specs=list(out_specs) + [_ANY] * nr,
        out_shape=list(out_shape) + _exchange_out_shape(riders),
        scratch_shapes=list(scratch) + (_exchange_sems(nr) if nr else []),
        compiler_params=_params(limit),
    )(*args, *riders)
    return res[:n_out], res[n_out:]


def _sb_fwd(qkv, riders=()):
    s = qkv.shape[0]
    tq = SB_TQ
    nq = s // tq
    assert nq <= RS_COUNT_LANE

    def body(q_ref, k_ref, v_ref, o_ref, rs_ref):
        i = pl.program_id(1)
        lane, row, col = _tile_iotas(tq)
        u2 = _tri(True)
        diag = col < row
        q = q_ref[...]
        qms = [jnp.where(hm, q, jnp.zeros_like(q)) for hm in (lane < HEAD_DIM, lane >= HEAD_DIM)]

        def step(kb, carry, masked):
            k0 = pl.multiple_of(kb * tq, tq)
            k = k_ref[pl.ds(k0, tq), :]
            v = v_ref[pl.ds(k0, tq), :]
            def tile(rows, nk, qm, state):
                run, acc, rt = (t[rows] for t in state)
                z = _dot_nt(qm[rows], k[:nk])
                lneg, _ = _log_sigmoid_parts(z)
                lpos = z + lneg
                if masked:
                    lneg = jnp.where(diag[rows, :nk], lneg, 0.0)
                rt = jnp.where(lane[rows] == kb, run, rt)
                a = []
                for sl in reversed(_sub_blocks(nk)):
                    st = _dot(_split(lneg[:, sl]), u2)
                    a.append(jnp.exp(lpos[:, sl] + st[:, :LANE] + run))
                    run = run + st[:, LANE:]
                a = jnp.concatenate(a[::-1], axis=1)
                if masked:
                    a = jnp.where(diag[rows, :nk], a, 0.0)
                return run, acc + _dot(a.astype(MXU), v[:nk]), rt

            return tuple(_over_strips(functools.partial(tile, qm=qm, state=state), tq) for qm, state in zip(qms, carry))

        zero = jnp.zeros((tq, LANE), F32)
        carry = step(i, ((zero, zero, zero),) * 2, True)

        def alive(cr):
            return jnp.maximum(jnp.max(cr[0][0]), jnp.max(cr[1][0])) > DEAD_LOG

        def walk(state):
            j, _, cr = state
            cr = step(i - 1 - j, cr, False)
            return j + 1, alive(cr), cr

        walked, _, carry = lax.while_loop(lambda state: (state[0] < i) & state[1], walk, (jnp.int32(0), alive(carry), carry))
        count = walked.astype(F32)
        rs_ref[0] = jnp.where(lane == RS_COUNT_LANE, count, carry[0][2])
        rs_ref[1] = jnp.where(lane == RS_COUNT_LANE, count, carry[1][2])
        o_ref[...] = jnp.where(lane < HEAD_DIM, carry[0][1], carry[1][1]).astype(o_ref.dtype)

    return _call_with_riders(
        body, "sb_fwd", (4, nq), _attn_specs(s, 0, tq),
        [pl.BlockSpec((tq, LANE), lambda hp, i: (i, hp)), pl.BlockSpec((2, tq, LANE), lambda hp, i: (hp, i, 0))],
        [jax.ShapeDtypeStruct((s, WIDTH), MXU), jax.ShapeDtypeStruct((8, s, LANE), F32)], [], (qkv, qkv, qkv), riders, True)


def _sb_bwd(qkv, do, rs, riders=()):
    s = qkv.shape[0]
    tq = SB_TQ
    nq = s // tq

    def body(q_ref, k_ref, v_ref, do_ref, rs_ref, dq_ref, dk_ref, dv_ref, dk_acc, dv_acc):
        i = pl.program_id(1)

        @pl.when(i == 0)
        def _():
            dk_acc[...] = jnp.zeros_like(dk_acc)
            dv_acc[...] = jnp.zeros_like(dv_acc)

        lane, row, col = _tile_iotas(tq)
        u2 = _tri(True)
        l2 = _tri(False)
        diag = col < row
        q = q_ref[...]
        do = do_ref[...]
        heads = [(jnp.where(hm, q, jnp.zeros_like(q)), jnp.where(hm, do, jnp.zeros_like(do)), rs_ref[hh])
                 for hh, hm in enumerate((lane < HEAD_DIM, lane >= HEAD_DIM))]

        def step(kb, carry, masked):
            k0 = pl.multiple_of(kb * tq, tq)
            k = k_ref[pl.ds(k0, tq), :]
            v = v_ref[pl.ds(k0, tq), :]
            to_keys = {}

            def tile(rows, nk, qm, dom, rblk, state):
                gpre, dq = (t[rows] for t in state)
                z = _dot_nt(qm[rows], k[:nk])
                lneg, e = _log_sigmoid_parts(z)
                lpos = z + lneg
                if masked:
                    lneg = jnp.where(diag[rows, :nk], lneg, 0.0)
                run = jnp.sum(jnp.where(lane[rows] == kb, rblk[rows], 0.0), axis=1, keepdims=True) + jnp.zeros_like(gpre)
                a = []
                for sl in reversed(_sub_blocks(nk)):
                    st = _dot(_split(lneg[:, sl]), u2)
                    a.append(jnp.exp(lpos[:, sl] + st[:, :LANE] + run))
                    run = run + st[:, LANE:]
                a = jnp.concatenate(a[::-1], axis=1)
                if masked:
                    a = jnp.where(diag[rows, :nk], a, 0.0)
                g = a * _dot_nt(dom[rows], v[:nk])
                pre = []
                for sl in _sub_blocks(nk):
                    pt = _dot(_split(g[:, sl]), l2)
                    pre.append(gpre + pt[:, :LANE])
                    gpre = gpre + pt[:, LANE:]
                sig = jnp.where(z >= 0.0, 1.0, e) / (1.0 + e)
                dz = g - (g + jnp.concatenate(pre, axis=1)) * sig
                if masked:
                    dz = jnp.where(diag[rows, :nk], dz, 0.0)
                dzb = dz.astype(MXU)
                both = to_keys.setdefault(nk, [0.0, 0.0])
                both[0] = both[0] + _dot_tn(dzb, qm[rows])
                both[1] = both[1] + _dot_tn(a.astype(MXU), dom[rows])
                return gpre, dq + _dot(dzb, k[:nk])

            new = tuple(_over_strips(functools.partial(tile, qm=qm, dom=dom, rblk=rblk, state=state), tq)
                        for (qm, dom, rblk), state in zip(heads, carry))
            for nk, (dk, dv) in to_keys.items():
                dk_acc[pl.ds(k0, nk), :] += dk
                dv_acc[pl.ds(k0, nk), :] += dv
            return new

        walked = jnp.max(jnp.where(lane[:8] == RS_COUNT_LANE, rs_ref[0, 0:8, :], 0.0))
        first = i - jnp.clip(walked.astype(jnp.int32), 0, i)
        zero = jnp.zeros((tq, LANE), F32)
        carry = step(i, lax.fori_loop(first, i, lambda kb, cr: step(kb, cr, False), ((zero, zero),) * 2), True)
        dq_ref[...] = (jnp.where(lane < HEAD_DIM, carry[0][1], carry[1][1]) * QK_SCALE).astype(dq_ref.dtype)

        @pl.when(i == nq - 1)
        def _():
            dk_ref[...] = dk_acc[...].astype(dk_ref.dtype)
            dv_ref[...] = dv_acc[...].astype(dv_ref.dtype)

    blk = pl.BlockSpec((tq, LANE), lambda hp, i: (i, hp))
    whole = pl.BlockSpec((s, LANE), lambda hp, i: (0, hp))
    return _call_with_riders(
        body, "sb_bwd", (4, nq), _attn_specs(s, 0, tq) + [blk, pl.BlockSpec((2, tq, LANE), lambda hp, i: (hp, i, 0))],
        [blk, whole, whole], [jax.ShapeDtypeStruct((s, WIDTH), MXU)] * 3,
        [pltpu.VMEM((s, LANE), F32), pltpu.VMEM((s, LANE), F32)], (qkv, qkv, qkv, do, rs), riders, False)


def _key_bias(fkt_ref, kb, h, tq):
    n_sub = tq // LANE
    return jnp.concatenate([fkt_ref[kb * n_sub + j, pl.ds(h, 1), :] for j in range(n_sub)], axis=1)


def _fox_fwd(qkv, fc, fkt, riders=()):
    s = qkv.shape[0]
    tq = TQ
    nq = s // tq
    nb = fkt.shape[0]

    def body(q_ref, k_ref, v_ref, fq_ref, fkt_ref, o_ref, lse_ref):
        hp = pl.program_id(0)
        i = pl.program_id(1)
        lane, row, col = _tile_iotas(tq)
        diag = col <= row
        q = q_ref[...]
        fqb = fq_ref[...]
        heads = []
        for hh in range(2):
            h = 2 * hp + hh
            hm = (lane >= HEAD_DIM) if hh else (lane < HEAD_DIM)
            heads.append((h, jnp.where(hm, q, jnp.zeros_like(q)), jnp.sum(jnp.where(lane == h, fqb, 0.0), axis=1, keepdims=True)))

        def step(kb, carry, masked):
            k0 = pl.multiple_of(kb * tq, tq)
            k = k_ref[pl.ds(k0, tq), :]
            v = v_ref[pl.ds(k0, tq), :]
            def tile(rows, nk, h, qm, fq, state):
                m, l, acc = (t[rows] for t in state)
                z = _dot_nt(qm[rows], k[:nk]) + fq[rows] - _key_bias(fkt_ref, kb, h, tq)[:, :nk]
                if masked:
                    z = jnp.where(diag[rows, :nk], z, NEG)
                mn = jnp.maximum(m, jnp.max(z, axis=1, keepdims=True))
                p = jnp.exp(z - mn)
                alpha = jnp.exp(m - mn)
                return mn, alpha * l + jnp.sum(p, axis=1, keepdims=True), alpha * acc + _dot(p.astype(MXU), v[:nk])

            return tuple(_over_strips(functools.partial(tile, h=h, qm=qm, fq=fq, state=state), tq)
                         for (h, qm, fq), state in zip(heads, carry))

        init = ((jnp.full((tq, 1), NEG, F32), jnp.zeros((tq, 1), F32), jnp.zeros((tq, LANE), F32)),) * 2
        carry = step(i, lax.fori_loop(0, i, lambda kb, cr: step(kb, cr, False), init), True)
        outs = []
        for hh, (m, l, acc) in enumerate(carry):
            outs.append(acc / l)
            lse_ref[hh] = jnp.broadcast_to(m + jnp.log(l), (tq, LANE))
        o_ref[...] = jnp.where(lane < HEAD_DIM, outs[0], outs[1]).astype(o_ref.dtype)

    return _call_with_riders(
        body, "fox_fwd", (4, nq),
        _attn_specs(s, 12, tq) + [pl.BlockSpec((tq, LANE), lambda hp, i: (i, 0)), pl.BlockSpec((nb, N_FGATE, LANE), lambda hp, i: (0, 0, 0))],
        [pl.BlockSpec((tq, LANE), lambda hp, i: (i, hp)), pl.BlockSpec((2, tq, LANE), lambda hp, i: (hp, i, 0))],
        [jax.ShapeDtypeStruct((s, WIDTH), MXU), jax.ShapeDtypeStruct((8, s, LANE), F32)], [], (qkv, qkv, qkv, fc, fkt), riders, True)


def _fox_bwd(qkv, fc, fkt, do, o, lse, riders=()):
    s = qkv.shape[0]
    tq = TQ
    nq = s // tq
    nb = fkt.shape[0]

    def body(q_ref, k_ref, v_ref, fq_ref, fkt_ref, do_ref, o_ref, lse_ref, dq_ref, dk_ref, dv_ref, dfk_ref, dfq_ref, dk_acc, dv_acc):
        hp = pl.program_id(0)
        i = pl.program_id(1)

        @pl.when(i == 0)
        def _():
            dk_acc[...] = jnp.zeros_like(dk_acc)
            dv_acc[...] = jnp.zeros_like(dv_acc)

        @pl.when((i == 0) & (hp == 0))
        def _():
            dfk_ref[...] = jnp.zeros_like(dfk_ref)

        lane, row, col = _tile_iotas(tq)
        diag = col <= row
        q = q_ref[...]
        do = do_ref[...]
        dof = do.astype(F32) * o_ref[...].astype(F32)
        fqb = fq_ref[...]
        heads = []
        for hh in range(2):
            h = 2 * hp + hh
            hm = (lane >= HEAD_DIM) if hh else (lane < HEAD_DIM)
            heads.append((h, jnp.where(hm, q, jnp.zeros_like(q)), jnp.where(hm, do, jnp.zeros_like(do)),
                          jnp.sum(jnp.where(hm, dof, 0.0), axis=1, keepdims=True),
                          jnp.sum(jnp.where(lane == h, fqb, 0.0), axis=1, keepdims=True), lse_ref[hh][:, :1]))

        def step(kb, carry, masked):
            k0 = pl.multiple_of(kb * tq, tq)
            k = k_ref[pl.ds(k0, tq), :]
            v = v_ref[pl.ds(k0, tq), :]
            to_keys = {}

            def tile(rows, nk, h, qm, dom, delta, fq, lse_t, state):
                dq, rsum = (t[rows] for t in state)
                z = _dot_nt(qm[rows], k[:nk]) + fq[rows] - _key_bias(fkt_ref, kb, h, tq)[:, :nk]
                if masked:
                    z = jnp.where(diag[rows, :nk], z, NEG)
                p = jnp.exp(z - lse_t[rows])
                ds = p * (_dot_nt(dom[rows], v[:nk]) - delta[rows])
                dsb = ds.astype(MXU)
                both = to_keys.setdefault(nk, [0.0, 0.0])
                both[0] = both[0] + _dot_tn(dsb, qm[rows])
                both[1] = both[1] + _dot_tn(p.astype(MXU), dom[rows])
                csum = _colsum(ds)
                for j, sl in enumerate(_sub_blocks(nk)):
                    dfk_ref[kb * (tq // LANE) + j, pl.ds(h, 1), :] += -csum[:, sl]
                return dq + _dot(dsb, k[:nk]), rsum + jnp.sum(ds, axis=1, keepdims=True)

            new = tuple(_over_strips(functools.partial(tile, h=h, qm=qm, dom=dom, delta=delta, fq=fq, lse_t=lse_t, state=state), tq)
                        for (h, qm, dom, delta, fq, lse_t), state in zip(heads, carry))
            for nk, (dk, dv) in to_keys.items():
                dk_acc[pl.ds(k0, nk), :] += dk
                dv_acc[pl.ds(k0, nk), :] += dv
            return new

        init = ((jnp.zeros((tq, LANE), F32), jnp.zeros((tq, 1), F32)),) * 2
        carry = step(i, lax.fori_loop(0, i, lambda kb, cr: step(kb, cr, False), init), True)
        dq_ref[...] = (jnp.where(lane < HEAD_DIM, carry[0][0], carry[1][0]) * QK_SCALE).astype(dq_ref.dtype)
        dfq_ref[0] = jnp.where(lane == heads[0][0], carry[0][1], jnp.where(lane == heads[1][0], carry[1][1], 0.0))

        @pl.when(i == nq - 1)
        def _():
            dk_ref[...] = dk_acc[...].astype(dk_ref.dtype)
            dv_ref[...] = dv_acc[...].astype(dv_ref.dtype)

    blk = pl.BlockSpec((tq, LANE), lambda hp, i: (i, hp))
    whole = pl.BlockSpec((s, LANE), lambda hp, i: (0, hp))
    pair = pl.BlockSpec((2, tq, LANE), lambda hp, i: (hp, i, 0))
    fkt_spec = pl.BlockSpec((nb, N_FGATE, LANE), lambda hp, i: (0, 0, 0))
    return _call_with_riders(
        body, "fox_bwd", (4, nq),
        _attn_specs(s, 12, tq) + [pl.BlockSpec((tq, LANE), lambda hp, i: (i, 0)), fkt_spec, blk, blk, pair],
        [blk, whole, whole, fkt_spec, pl.BlockSpec((1, tq, LANE), lambda hp, i: (hp, i, 0))],
        [jax.ShapeDtypeStruct((s, WIDTH), MXU)] * 3
        + [jax.ShapeDtypeStruct((nb, N_FGATE, LANE), F32), jax.ShapeDtypeStruct((4, s, LANE), F32)],
        [pltpu.VMEM((s, LANE), F32), pltpu.VMEM((s, LANE), F32)], (qkv, qkv, qkv, fc, fkt, do, o, lse), riders, False)


def _fcum_bwd(dfkt, dfq, fl, bf):
    s = fl.shape[0]
    nb = s // LANE

    def body(dfkt_ref, dfq_ref, fl_ref, bf_ref, df_ref, dbf_ref, tail_ref):
        @pl.when(pl.program_id(0) == 0)
        def _():
            tail_ref[...] = jnp.zeros_like(tail_ref)
            dbf_ref[...] = jnp.zeros_like(dbf_ref)

        r = lax.broadcasted_iota(jnp.int32, (LANE, LANE), 0)
        c = lax.broadcasted_iota(jnp.int32, (LANE, LANE), 1)
        tri = (c >= r).astype(F32)
        dfc = jnp.concatenate([dfkt_ref[0], jnp.zeros((LANE - N_FGATE, LANE), F32)], axis=0).T
        dfc = dfc + ((dfq_ref[0] + dfq_ref[1]) + (dfq_ref[2] + dfq_ref[3]))
        dls = jnp.dot(tri, dfc, precision=lax.Precision.HIGHEST, preferred_element_type=F32) + tail_ref[...]
        xb = fl_ref[...] + bf_ref[...]
        e = jnp.exp(-jnp.abs(xb))
        dfl = dls * (jnp.where(xb >= 0.0, e, 1.0) / (1.0 + e))
        df_ref[...] = dfl.astype(df_ref.dtype)
        tail_ref[...] = dls[0:1, :]
        dbf_ref[...] += _colsum(dfl)

    return pl.pallas_call(
        body, name="fcum_bwd", grid=(nb,),
        in_specs=[pl.BlockSpec((1, N_FGATE, LANE), lambda j: (nb - 1 - j, 0, 0)), pl.BlockSpec((4, LANE, LANE), lambda j: (0, nb - 1 - j, 0)),
                  pl.BlockSpec((LANE, LANE), lambda j: (nb - 1 - j, 0)), _fixed(1, LANE)],
        out_specs=[pl.BlockSpec((LANE, LANE), lambda j: (nb - 1 - j, 0)), _fixed(1, LANE)],
        out_shape=[jax.ShapeDtypeStruct((s, LANE), MXU), jax.ShapeDtypeStruct((1, LANE), F32)],
        scratch_shapes=[pltpu.VMEM((1, LANE), F32)],
    )(dfkt, dfq, fl, bf)


def _mix_fwd(x, o_sb, o_fx, gl, w_sb, w_fx, w_o, g1, ln1_g, ln1_b, sh2, sc2):
    s = x.shape[0]
    tm = 256

    def body(x_ref, osb_ref, ofx_ref, gl_ref, wsb_ref, wfx_ref, wo_ref, g1_ref, lg_ref, lb_ref, sh_ref, sc_ref, r1_ref, u2_ref):
        mixin = (_sigmoid(gl_ref[:, :D]) * _dot(osb_ref[...], wsb_ref[...])
                 + _sigmoid(gl_ref[:, D:]) * _dot(ofx_ref[...], wfx_ref[...]))
        r1 = ALPHA * x_ref[...] + g1_ref[...] * _dot(mixin.astype(MXU), wo_ref[...])
        r1_ref[...] = r1
        x1 = _ln(r1)[0] * lg_ref[...] + lb_ref[...]
        u2_ref[...] = (_ln(x1)[0] * (1.0 + sc_ref[...]) + sh_ref[...]).astype(MXU)

    vec = _fixed(1, D)
    return pl.pallas_call(
        body, name="mix_fwd", grid=(s // tm,),
        in_specs=[_rows(tm, D), _rows(tm, WIDTH), _rows(tm, WIDTH), _rows(tm, 2 * D), _res(w_sb), _res(w_fx), _res(w_o),
                  vec, vec, vec, vec, vec],
        out_specs=[_rows(tm, D), _rows(tm, D)],
        out_shape=[jax.ShapeDtypeStruct((s, D), F32), jax.ShapeDtypeStruct((s, D), MXU)],
        compiler_params=_params(VMEM_BIG),
    )(x, o_sb, o_fx, gl, w_sb, w_fx, w_o, g1, ln1_g, ln1_b, sh2, sc2)


def _ffn_fwd(r1, u2, tgt, w_g, w_u, w_d, g2, ln1_g, ln1_b, ln2_g, ln2_b):
    s = r1.shape[0]
    tm = 256

    def body(r1_ref, u2_ref, t_ref, wg_ref, wu_ref, wd_ref, g2_ref, l1g_ref, l1b_ref, l2g_ref, l2b_ref,
             hg_ref, hu_ref, dxa_ref, dh_ref, acc_ref):
        @pl.when(pl.program_id(0) == 0)
        def _():
            acc_ref[...] = jnp.zeros_like(acc_ref)

        u2 = u2_ref[...]
        hg = _dot_nt(u2, wg_ref[...])
        hu = _dot_nt(u2, wu_ref[...])
        hg_ref[...] = hg
        hu_ref[...] = hu
        h = _dot((hg * _sigmoid(hg) * hu).astype(MXU), wd_ref[...])
        x1 = _ln(r1_ref[...])[0] * l1g_ref[...] + l1b_ref[...]
        xh2, rstd2 = _ln(ALPHA * x1 + g2_ref[...] * h)
        err = xh2 * l2g_ref[...] + l2b_ref[...] - t_ref[...]
        dy = err * (1.0 / D)
        dr2 = _ln_bwd(dy * l2g_ref[...], xh2, rstd2)
        dxa_ref[...] = ALPHA * dr2
        dh_ref[...] = (g2_ref[...] * dr2).astype(MXU)
        acc_ref[0:1, :] += _colsum(dr2 * h)
        acc_ref[1:2, :] += _colsum(dy * xh2)
        acc_ref[2:3, :] += _colsum(dy)
        acc_ref[3:4, :] += _colsum(err * err) * (0.5 / D)

    vec = _fixed(1, D)
    return pl.pallas_call(
        body, name="ffn_fwd", grid=(s // tm,),
        in_specs=[_rows(tm, D), _rows(tm, D), _rows(tm, D), _res(w_g), _res(w_u), _res(w_d), vec, vec, vec, vec, vec],
        out_specs=[_rows(tm, D_FF), _rows(tm, D_FF), _rows(tm, D), _rows(tm, D), _fixed(8, D)],
        out_shape=[jax.ShapeDtypeStruct((s, D_FF), F32), jax.ShapeDtypeStruct((s, D_FF), F32),
                   jax.ShapeDtypeStruct((s, D), F32), jax.ShapeDtypeStruct((s, D), MXU), jax.ShapeDtypeStruct((8, D), F32)],
        compiler_params=_params(VMEM_BIG),
    )(r1, u2, tgt, w_g, w_u, w_d, g2, ln1_g, ln1_b, ln2_g, ln2_b)


def _ffn_bwd(dh, hg, hu, w_g, w_u, w_d):
    s = dh.shape[0]
    tm = 256
    half = D_FF // 2

    def body(dh_ref, hg_ref, hu_ref, wg_ref, wu_ref, wd_ref, act_ref, dhg_ref, dhu_ref, du2_ref):
        dh = dh_ref[...]
        du2 = jnp.zeros((tm, D), F32)
        for c0 in (0, half):
            cols = slice(c0, c0 + half)
            dact = _dot_nt(dh, wd_ref[cols, :])
            hg = hg_ref[:, cols]
            hu = hu_ref[:, cols]
            sg = _sigmoid(hg)
            sl = hg * sg
            act_ref[:, cols] = (sl * hu).astype(MXU)
            dhg = (dact * hu * (sg * (1.0 + hg * (1.0 - sg)))).astype(MXU)
            dhu = (dact * sl).astype(MXU)
            dhg_ref[:, cols] = dhg
            dhu_ref[:, cols] = dhu
            du2 = du2 + _dot(dhg, wg_ref[cols, :]) + _dot(dhu, wu_ref[cols, :])
        du2_ref[...] = du2

    return pl.pallas_call(
        body, name="ffn_bwd", grid=(s // tm,),
        in_specs=[_rows(tm, D), _rows(tm, D_FF), _rows(tm, D_FF), _res(w_g), _res(w_u), _res(w_d)],
        out_specs=[_rows(tm, D_FF), _rows(tm, D_FF), _rows(tm, D_FF), _rows(tm, D)],
        out_shape=[jax.ShapeDtypeStruct((s, D_FF), MXU)] * 3 + [jax.ShapeDtypeStruct((s, D), F32)],
        compiler_params=_params(VMEM_BIG),
    )(dh, hg, hu, w_g, w_u, w_d)


def _mix_bwd(du2, dxa, r1, o_sb, o_fx, gl, w_sb, w_fx, w_o, g1, ln1_g, ln1_b, sc2):
    s = r1.shape[0]
    tm = 256
    n_tiles = s // tm

    def body(du2_ref, dxa_ref, r1_ref, osb_ref, ofx_ref, gl_ref, wsb_ref, wfx_ref, wo_ref, g1_ref, lg_ref, lb_ref, sc_ref,
             dx_ref, dosb_ref, dofx_ref, dgl_ref, dbg_ref, acc_ref, dwsb_ref, dwfx_ref, dwo_ref, nsb_ref, nfx_ref, no_ref):
        @pl.when(pl.program_id(0) == 0)
        def _():
            for ref in (acc_ref, dbg_ref, dwsb_ref, dwfx_ref, dwo_ref):
                ref[...] = jnp.zeros_like(ref)

        du2 = du2_ref[...]
        xh1, rstd1 = _ln(r1_ref[...])
        x1 = xh1 * lg_ref[...] + lb_ref[...]
        n1, rstdn = _ln(x1)
        dx1 = dxa_ref[...] + _ln_bwd(du2 * (1.0 + sc_ref[...]), n1, rstdn)
        dr1 = _ln_bwd(dx1 * lg_ref[...], xh1, rstd1)
        dx_ref[...] = ALPHA * dr1
        ysb = _dot(osb_ref[...], wsb_ref[...])
        yfx = _dot(ofx_ref[...], wfx_ref[...])
        gs = _sigmoid(gl_ref[:, :D])
        gf = _sigmoid(gl_ref[:, D:])
        mixin = (gs * ysb + gf * yfx).astype(MXU)
        mix = _dot(mixin, wo_ref[...])
        dmix = (g1_ref[...] * dr1).astype(MXU)
        dmixin = _dot_nt(dmix, wo_ref[...])
        dysb = (dmixin * gs).astype(MXU)
        dyfx = (dmixin * gf).astype(MXU)
        dwo_ref[...] += _dot_tn(mixin, dmix)
        dwsb_ref[...] += _dot_tn(osb_ref[...], dysb)
        dwfx_ref[...] += _dot_tn(ofx_ref[...], dyfx)
        dosb_ref[...] = _dot_nt(dysb, wsb_ref[...]).astype(MXU)
        dofx_ref[...] = _dot_nt(dyfx, wfx_ref[...]).astype(MXU)
        dgs = dmixin * ysb * gs * (1.0 - gs)
        dgf = dmixin * yfx * gf * (1.0 - gf)
        dgl_ref[:, :D] = dgs.astype(MXU)
        dgl_ref[:, D:] = dgf.astype(MXU)
        dbg_ref[:, :D] += _colsum(dgs)
        dbg_ref[:, D:] += _colsum(dgf)
        acc_ref[0:1, :] += _colsum(du2)
        acc_ref[1:2, :] += _colsum(du2 * n1)
        acc_ref[2:3, :] += _colsum(dx1 * xh1)
        acc_ref[3:4, :] += _colsum(dx1)
        acc_ref[4:5, :] += _colsum(dr1 * mix)

        @pl.when(pl.program_id(0) == n_tiles - 1)
        def _():
            nsb_ref[...] = dwsb_ref[...].astype(MXU)
            nfx_ref[...] = dwfx_ref[...].astype(MXU)
            no_ref[...] = dwo_ref[...].astype(MXU)

    vec = _fixed(1, D)
    dw_specs = [_fixed(WIDTH, D), _fixed(WIDTH, D), _fixed(D, D)]
    dw_shapes = [(WIDTH, D), (WIDTH, D), (D, D)]
    return pl.pallas_call(
        body, name="mix_bwd", grid=(n_tiles,),
        in_specs=[_rows(tm, D), _rows(tm, D), _rows(tm, D), _rows(tm, WIDTH), _rows(tm, WIDTH), _rows(tm, 2 * D),
                  _res(w_sb), _res(w_fx), _res(w_o), vec, vec, vec, vec],
        out_specs=[_rows(tm, D), _rows(tm, WIDTH), _rows(tm, WIDTH), _rows(tm, 2 * D), _fixed(1, 2 * D), _fixed(8, D)] + dw_specs * 2,
        out_shape=[jax.ShapeDtypeStruct((s, D), F32)] + [jax.ShapeDtypeStruct((s, WIDTH), MXU)] * 2
        + [jax.ShapeDtypeStruct((s, 2 * D), MXU), jax.ShapeDtypeStruct((1, 2 * D), F32), jax.ShapeDtypeStruct((8, D), F32)]
        + [jax.ShapeDtypeStruct(sh, F32) for sh in dw_shapes] + [jax.ShapeDtypeStruct(sh, MXU) for sh in dw_shapes],
        compiler_params=_params(VMEM_BIG),
    )(du2, dxa, r1, o_sb, o_fx, gl, w_sb, w_fx, w_o, g1, ln1_g, ln1_b, sc2)


def _in_bwd(pieces, x, dxa, w_all, sc1, riders=()):
    s = x.shape[0]
    tm = 256
    n_p = len(pieces)

    def body(*refs):
        p_refs = refs[:n_p]
        x_ref, dxa_ref, w_ref, sc_ref, gx_ref, acc_ref = refs[n_p:]

        @pl.when(pl.program_id(0) == 0)
        def _():
            acc_ref[...] = jnp.zeros_like(acc_ref)

        du1 = jnp.zeros((tm, D), F32)
        for p_ref, (arr, c0) in zip(p_refs, pieces):
            du1 = du1 + _dot(p_ref[...], w_ref[c0:c0 + arr.shape[1], :])
        n0, rstd0 = _ln(x_ref[...])
        gx_ref[...] = dxa_ref[...] + _ln_bwd(du1 * (1.0 + sc_ref[...]), n0, rstd0)
        acc_ref[0:1, :] += _colsum(du1)
        acc_ref[1:2, :] += _colsum(du1 * n0)

    return _call_with_riders(
        body, "in_bwd", (s // tm,),
        [_rows(tm, a.shape[1]) for a, _ in pieces] + [_rows(tm, D), _rows(tm, D), _res(w_all), _fixed(1, D)],
        [_rows(tm, D), _fixed(8, D)], [jax.ShapeDtypeStruct((s, D), F32), jax.ShapeDtypeStruct((8, D), F32)], [],
        (*[a for a, _ in pieces], x, dxa, w_all, sc1), riders, False, VMEM_BIG)


def _matmul_tn(a, b, name, narrow=False):
    s, m = a.shape
    n = b.shape[1]
    tm = 512 if m % 512 == 0 else (m if m < 512 else m // 2)
    tn = n // 2 if n > 2048 else n
    ts = 512
    assert m % tm == 0 and tm % LANE == 0 and n % tn == 0 and tn % LANE == 0 and s % ts == 0

    def body(a_ref, b_ref, o_ref, *narrow_ref):
        @pl.when(pl.program_id(2) == 0)
        def _():
            o_ref[...] = jnp.zeros_like(o_ref)

        o_ref[...] += _dot_tn(a_ref[...], b_ref[...])
        if narrow:
            @pl.when(pl.program_id(2) == s // ts - 1)
            def _():
                narrow_ref[0][...] = o_ref[...].astype(MXU)

    out_blk = pl.BlockSpec((tm, tn), lambda i, j, k: (i, j))
    res = pl.pallas_call(
        body, name=name, grid=(m // tm, n // tn, s // ts),
        in_specs=[pl.BlockSpec((ts, tm), lambda i, j, k: (k, i)), pl.BlockSpec((ts, tn), lambda i, j, k: (k, j))],
        out_specs=[out_blk] * (2 if narrow else 1),
        out_shape=[jax.ShapeDtypeStruct((m, n), F32)] + ([jax.ShapeDtypeStruct((m, n), MXU)] if narrow else []),
        compiler_params=_params(VMEM_BIG),
    )(a, b)
    return tuple(res) if narrow else res[0]


def _local_step(x, tgt, ada, w_all, b_gate, bf_pad, late_weights, early_grads, w_in_grads, ln1_g, ln1_b, ln2_g, ln2_b):
    sh1, sc1, g1, sh2, sc2, g2 = ada
    u1, qkv, fl, gl = _in_proj(x, sh1, sc1, w_all, b_gate)
    fc, fkt = _fcum_fwd(fl, bf_pad)
    late_riders, late_full = late_weights
    n_sb = 3
    (o_sb, rs), gathered_a = _sb_fwd(qkv, late_riders[:n_sb])
    (o_fx, lse), gathered_b = _fox_fwd(qkv, fc, fkt, late_riders[n_sb:])
    w_sb, w_fx, w_o, w_g, w_u, w_d = late_full(list(gathered_a) + list(gathered_b))
    r1, u2 = _mix_fwd(x, o_sb, o_fx, gl, w_sb, w_fx, w_o, g1, ln1_g, ln1_b, sh2, sc2)
    hg, hu, dxa2, dh, acc_f = _ffn_fwd(r1, u2, tgt, w_g, w_u, w_d, g2, ln1_g, ln1_b, ln2_g, ln2_b)
    act, dhg, dhu, du2 = _ffn_bwd(dh, hg, hu, w_g, w_u, w_d)
    dxa1, dosb, dofx, dgl, dbg, acc_m, dw_sb, dw_fx, dw_o, n_sb_out, n_fx_out, n_o = _mix_bwd(
        du2, dxa2, r1, o_sb, o_fx, gl, w_sb, w_fx, w_o, g1, ln1_g, ln1_b, sc2)
    early = dict(w_sb_out=(dw_sb, n_sb_out), w_fox_out=(dw_fx, n_fx_out), w_o=(dw_o, n_o),
                 w_ffn_gate=_matmul_tn(dhg, u2, "dw_ffn_gate", True),
                 w_ffn_up=_matmul_tn(dhu, u2, "dw_ffn_up", True), w_ffn_down=_matmul_tn(act, dh, "dw_ffn_down", True))
    early_riders = early_grads(early)
    (dq_sb, dk_sb, dv_sb), received_a = _sb_bwd(qkv, dosb, rs, early_riders[:n_sb])
    (dq_fx, dk_fx, dv_fx, dfkt, dfq), received_b = _fox_bwd(qkv, fc, fkt, dofx, o_fx, lse, early_riders[n_sb:])
    early_received = list(received_a) + list(received_b)
    df, dbf = _fcum_bwd(dfkt, dfq, fl, bf_pad)
    pieces = [(dq_sb, 0), (dk_sb, WIDTH), (dv_sb, 2 * WIDTH), (dq_fx, 3 * WIDTH), (dk_fx, 4 * WIDTH), (dv_fx, 5 * WIDTH),
              (df, OFF_FGATE), (dgl, OFF_FGATE + LANE)]
    dw_in = [_matmul_tn(p, u1, f"dw_in_{j}") for j, (p, _) in enumerate(pieces)]
    (grad_x, acc_i), w_in_received = _in_bwd(pieces, x, dxa1, w_all, sc1, w_in_grads(dw_in))
    return dict(
        loss_lanes=acc_f[3:4], grad_x=grad_x, dw_in=dw_in, early=early, early_received=early_received,
        w_in_received=w_in_received,
        d_ada=[acc_i[0:1], acc_i[1:2], acc_m[4:5], acc_m[0:1], acc_m[1:2], acc_f[0:1]],
        dln1_g=acc_m[2:3], dln1_b=acc_m[3:4], dln2_g=acc_f[1:2], dln2_b=acc_f[2:3], db_gate=dbg, db_forget=dbf)


_MESH_ID = pl.DeviceIdType.MESH
_ANY = pl.BlockSpec(memory_space=pl.ANY)
_VMEM = pl.BlockSpec(memory_space=pltpu.VMEM)


def _mesh_pos():
    return lax.axis_index("x"), lax.axis_index("y"), lax.axis_index("c")


def _other_chips(x, y):
    return [(1 - x, y), (x, 1 - y), (1 - x, 1 - y)]


def _allgather_rows(v, name):
    n = v.shape[1]

    def body(v_ref, out_ref, send_sems, recv_sems, local_sem):
        x, y, c = _mesh_pos()
        me = 4 * x + 2 * y + c
        mine = pltpu.make_async_copy(v_ref, out_ref.at[me], local_sem)
        mine.start()
        copies = []
        for d in range(1, 8):
            fx, fy, fc = (d >> 2) & 1, (d >> 1) & 1, d & 1
            to = (1 - x if fx else x, 1 - y if fy else y, 1 - c if fc else c)
            cp = pltpu.make_async_remote_copy(src_ref=v_ref, dst_ref=out_ref.at[me], send_sem=send_sems.at[d - 1],
                                              recv_sem=recv_sems.at[d - 1], device_id=to, device_id_type=_MESH_ID)
            cp.start()
            copies.append(cp)
        for cp in copies:
            cp.wait_recv()
        for cp in copies:
            cp.wait_send()
        mine.wait()

    return pl.pallas_call(
        body, name=name, in_specs=[_VMEM], out_specs=_VMEM,
        out_shape=jax.ShapeDtypeStruct((8, 1, n), v.dtype),
        scratch_shapes=[pltpu.SemaphoreType.DMA((7,)), pltpu.SemaphoreType.DMA((7,)), pltpu.SemaphoreType.DMA(())],
    )(v)


def _chip_exchange(arrays, name, gather):
    nt = len(arrays)

    def body(*refs):
        ins, outs = refs[:nt], refs[nt:2 * nt]
        _exchange_start(ins, outs, refs[2 * nt:], gather)
        _exchange_wait(ins, outs, refs[2 * nt:], gather)

    return pl.pallas_call(
        body, name=name, in_specs=[_ANY] * nt, out_specs=[_ANY] * nt, out_shape=_exchange_out_shape(arrays),
        scratch_shapes=_exchange_sems(nt),
    )(*arrays)


def _exchange_out_shape(arrays):
    return [jax.ShapeDtypeStruct((4,) + a.shape[-2:], a.dtype) for a in arrays]


def _exchange_sems(nt):
    return [pltpu.SemaphoreType.DMA((3 * nt,)), pltpu.SemaphoreType.DMA((3 * nt,)), pltpu.SemaphoreType.DMA((nt,))]


def _exchange_copies(ins, outs, sems, gather):
    send_sems, recv_sems, local_sems = sems
    x, y, c = _mesh_pos()
    me = 2 * x + y
    local, remote = [], []
    for t in range(len(ins)):
        local.append(pltpu.make_async_copy(ins[t] if gather else ins[t].at[me], outs[t].at[me], local_sems.at[t]))
        for j, (px, py) in enumerate(_other_chips(x, y)):
            remote.append(pltpu.make_async_remote_copy(
                src_ref=ins[t] if gather else ins[t].at[2 * px + py], dst_ref=outs[t].at[me], send_sem=send_sems.at[3 * t + j],
                recv_sem=recv_sems.at[3 * t + j], device_id=(px, py, c), device_id_type=_MESH_ID))
    return local, remote


def _exchange_start(ins, outs, sems, gather):
    local, remote = _exchange_copies(ins, outs, sems, gather)
    for cp in local + remote:
        cp.start()


def _exchange_wait(ins, outs, sems, gather):
    local, remote = _exchange_copies(ins, outs, sems, gather)
    for cp in remote:
        cp.wait_recv()
    for cp in remote:
        cp.wait_send()
    for cp in local:
        cp.wait()


def _gather_two_level(shard, name):
    r, n = shard.shape
    half = n // 2
    assert half % LANE == 0

    def body(in_ref, out_ref, ici_send, ici_recv, d2d_send, d2d_recv, local_sem):
        x, y, c = _mesh_pos()
        me = 2 * x + y
        mine = pl.ds(pl.multiple_of(c * half, LANE), half)
        theirs = pl.ds(pl.multiple_of((1 - c) * half, LANE), half)
        local = pltpu.make_async_copy(in_ref, out_ref.at[me], local_sem)
        local.start()
        chips = _other_chips(x, y)
        over_ici = [pltpu.make_async_remote_copy(
            src_ref=in_ref.at[:, mine], dst_ref=out_ref.at[me, :, mine], send_sem=ici_send.at[j], recv_sem=ici_recv.at[j],
            device_id=(px, py, c), device_id_type=_MESH_ID) for j, (px, py) in enumerate(chips)]
        for cp in over_ici:
            cp.start()
        passed_on = [pltpu.make_async_remote_copy(
            src_ref=out_ref.at[2 * px + py, :, mine], dst_ref=out_ref.at[2 * px + py, :, mine], send_sem=d2d_send.at[j],
            recv_sem=d2d_recv.at[j], device_id=(x, y, 1 - c), device_id_type=_MESH_ID) for j, (px, py) in enumerate(chips)]
        for j, (px, py) in enumerate(chips):
            pltpu.make_async_remote_copy(
                src_ref=in_ref.at[:, mine], dst_ref=out_ref.at[2 * px + py, :, mine], send_sem=ici_send.at[j],
                recv_sem=ici_recv.at[j], device_id=(px, py, c), device_id_type=_MESH_ID).wait_recv()
            passed_on[j].start()
        for j, (px, py) in enumerate(chips):
            pltpu.make_async_remote_copy(
                src_ref=out_ref.at[2 * px + py, :, theirs], dst_ref=out_ref.at[2 * px + py, :, theirs], send_sem=d2d_send.at[j],
                recv_sem=d2d_recv.at[j], device_id=(x, y, 1 - c), device_id_type=_MESH_ID).wait_recv()
        for cp in over_ici + passed_on:
            cp.wait_send()
        local.wait()

    sems = pltpu.SemaphoreType.DMA((3,))
    return pl.pallas_call(
        body, name=name, in_specs=[_ANY], out_specs=_ANY, out_shape=jax.ShapeDtypeStruct((4, r, n), shard.dtype),
        scratch_shapes=[sems, sems, sems, sems, pltpu.SemaphoreType.DMA(())],
    )(shard)


def _sibling_exchange(arrays, name):
    nt = len(arrays)

    def body(*refs):
        ins, outs = refs[:nt], refs[nt:2 * nt]
        send_sems, recv_sems = refs[2 * nt:]
        x, y, c = _mesh_pos()
        copies = []
        for t in range(nt):
            cp = pltpu.make_async_remote_copy(src_ref=ins[t], dst_ref=outs[t], send_sem=send_sems.at[t], recv_sem=recv_sems.at[t],
                                              device_id=(x, y, 1 - c), device_id_type=_MESH_ID)
            cp.start()
            copies.append(cp)
        for cp in copies:
            cp.wait_recv()
        for cp in copies:
            cp.wait_send()

    return pl.pallas_call(
        body, name=name, in_specs=[_ANY] * nt, out_specs=[_ANY] * nt,
        out_shape=[jax.ShapeDtypeStruct(a.shape, a.dtype) for a in arrays],
        scratch_shapes=[pltpu.SemaphoreType.DMA((nt,)), pltpu.SemaphoreType.DMA((nt,))],
    )(*arrays)


def _tiles(r, n):
    for tr in (256, 352, 128):
        if r % tr == 0:
            return tr, n, r // tr, lambda i: (i, 0)
    assert n % 256 == 0
    return r, 256, n // 256, lambda i: (0, i)


def _reduce_chips(chip, pieces, recv, name):
    _, r, n = pieces.shape
    tr, tn, steps, at = _tiles(r, n)

    def body(chip_ref, own_ref, recv_ref, out_ref):
        me = chip_ref[0]
        total = jnp.zeros((tr, tn), F32)
        for k in range(4):
            total = total + jnp.where(me == k, own_ref[0], recv_ref[k].astype(F32))
        out_ref[...] = total

    return pl.pallas_call(
        body, name=name,
        grid_spec=pltpu.PrefetchScalarGridSpec(
            num_scalar_prefetch=1, grid=(steps,),
            in_specs=[pl.BlockSpec((1, tr, tn), lambda i, chip_ref: (chip_ref[0],) + at(i)),
                      pl.BlockSpec((4, tr, tn), lambda i, chip_ref: (0,) + at(i))],
            out_specs=pl.BlockSpec((tr, tn), lambda i, chip_ref: at(i))),
        out_shape=jax.ShapeDtypeStruct((r, n), F32),
    )(chip, pieces, recv)


def _adamw_math(w, g, m, v):
    m = ADAM_B1 * m + (1.0 - ADAM_B1) * g
    v = ADAM_B2 * v + (1.0 - ADAM_B2) * (g * g)
    m_hat = m / (1.0 - ADAM_B1 ** ADAM_STEP)
    v_hat = v / (1.0 - ADAM_B2 ** ADAM_STEP)
    return -ADAM_LR * (m_hat / (jnp.sqrt(v_hat) + ADAM_EPS) + ADAM_WD * w), m, v


def _adamw(w, m, v, g_parts, name):
    r, n = w.shape
    tr, tn, steps, at = _tiles(r, n)
    blk = pl.BlockSpec((tr, tn), at)
    ng = len(g_parts)

    def body(*refs):
        w_ref, m_ref, v_ref = refs[:3]
        g_refs = refs[3:3 + ng]
        g_out, d_out, m_out, v_out = refs[3 + ng:]
        g = g_refs[0][...]
        for gr in g_refs[1:]:
            g = g + gr[...]
        g_out[...] = g
        d_out[...], m_out[...], v_out[...] = _adamw_math(w_ref[...], g, m_ref[...], v_ref[...])

    return pl.pallas_call(
        body, name=name, grid=(steps,),
        in_specs=[blk] * (3 + ng), out_specs=[blk] * 4,
        out_shape=[jax.ShapeDtypeStruct((r, n), F32)] * 4,
    )(w, m, v, *g_parts)


def _ada_fwd(c_all, w_shard, b_shard):
    n = w_shard.shape[1]
    tn = 512

    def body(c_ref, w_ref, b_ref, o_ref):
        cv = c_ref[...]
        ca = (cv * _sigmoid(cv)).astype(MXU)
        o_ref[...] = _dot(ca, w_ref[...].astype(MXU)) + b_ref[...]

    return pl.pallas_call(
        body, name="ada_fwd", grid=(n // tn,),
        in_specs=[_fixed(8, D), pl.BlockSpec((D, tn), lambda j: (0, j)), pl.BlockSpec((1, tn), lambda j: (0, j))],
        out_specs=pl.BlockSpec((8, tn), lambda j: (0, j)),
        out_shape=jax.ShapeDtypeStruct((8, n), F32),
    )(c_all, w_shard, b_shard)


def _ada_bwd(c_all, dada_shard):
    n = dada_shard.shape[1]
    tn = 512

    def body(c_ref, d_ref, o_ref):
        cv = c_ref[...]
        ca = (cv * _sigmoid(cv)).astype(MXU)
        o_ref[...] = _dot_tn(ca, d_ref[...].astype(MXU))

    return pl.pallas_call(
        body, name="ada_bwd", grid=(n // tn,),
        in_specs=[_fixed(8, D), pl.BlockSpec((8, tn), lambda j: (0, j))],
        out_specs=pl.BlockSpec((D, tn), lambda j: (0, j)),
        out_shape=jax.ShapeDtypeStruct((D, n), F32),
    )(c_all, dada_shard)


_SMALL = [("d_ada", N_COND * D), ("ln1_g", D), ("ln1_b", D), ("ln2_g", D), ("ln2_b", D), ("b_gate", 2 * D), ("b_forget", LANE),
          ("loss", D)]
_SMALL_OFF = {}
_o = 0
for _n, _w in _SMALL:
    _SMALL_OFF[_n] = (_o, _w)
    _o += _w
_SMALL_LEN = _o
_SMALL_PARAMS = [("b_ada", "d_ada", N_COND * D), ("b_gate", "b_gate", 2 * D), ("b_forget", "b_forget", N_FGATE),
                 ("ln1_g", "ln1_g", D), ("ln1_b", "ln1_b", D), ("ln2_g", "ln2_g", D), ("ln2_b", "ln2_b", D)]


def _small_update(rows, params):
    npar = len(_SMALL_PARAMS)

    def body(*refs):
        rows_ref = refs[0]
        p_refs = refs[1:1 + 3 * npar]
        loss_ref = refs[1 + 3 * npar]
        o_refs = refs[2 + 3 * npar:]
        total = rows_ref[0]
        for d in range(1, 8):
            total = total + rows_ref[d]
        lo, lw = _SMALL_OFF["loss"]
        loss_ref[...] = jnp.sum(total[:, lo:lo + lw], axis=1, keepdims=True)
        for j, (_, key, n) in enumerate(_SMALL_PARAMS):
            off = _SMALL_OFF[key][0]
            g = total[:, off:off + n]
            w_ref, m_ref, v_ref = p_refs[3 * j:3 * j + 3]
            o_refs[4 * j][...] = g
            o_refs[4 * j + 1][...], o_refs[4 * j + 2][...], o_refs[4 * j + 3][...] = _adamw_math(w_ref[...], g, m_ref[...], v_ref[...])

    flat = [a for p in params for a in p]
    out_shape = [jax.ShapeDtypeStruct((1, 1), F32)] + [jax.ShapeDtypeStruct((1, n), F32) for _, _, n in _SMALL_PARAMS for _ in range(4)]
    return pl.pallas_call(body, name="small_update", out_shape=out_shape)(rows, *flat)


_BIG = [("w_in", "cols_t"), ("w_sb_out", "cols"), ("w_fox_out", "cols"), ("w_o", "rows"),
        ("w_ffn_gate", "cols_t"), ("w_ffn_up", "cols_t"), ("w_ffn_down", "rows")]


def _shard2d(a, how):
    return a[0].T if how == "cols_t" else a[0]


def _unshard(g, how):
    if how == "cols":
        return g.transpose(1, 0, 2).reshape(g.shape[1], 4 * g.shape[2])
    return g.reshape(4 * g.shape[1], g.shape[2])


def _reshard(w, how):
    if how == "cols":
        return w.reshape(w.shape[0], 4, w.shape[1] // 4).transpose(1, 0, 2)
    return w.reshape(4, w.shape[0] // 4, w.shape[1])


def kernel(x, c, w_ada, b_ada, w_in, b_gate, b_forget, w_sb_out, w_fox_out, w_o, ln1_g, ln1_b, w_ffn_gate, w_ffn_up, w_ffn_down, ln2_g, ln2_b, loss_target, m_w_ada, m_b_ada, m_w_in, m_b_gate, m_b_forget, m_w_sb_out, m_w_fox_out, m_w_o, m_ln1_g, m_ln1_b, m_w_ffn_gate, m_w_ffn_up, m_w_ffn_down, m_ln2_g, m_ln2_b, v_w_ada, v_b_ada, v_w_in, v_b_gate, v_b_forget, v_w_sb_out, v_w_fox_out, v_w_o, v_ln1_g, v_ln1_b, v_w_ffn_gate, v_w_ffn_up, v_w_ffn_down, v_ln2_g, v_ln2_b):
    given = dict(locals())
    mx, my, mc = _mesh_pos()
    chip = 2 * mx + my
    seq = 4 * mx + 2 * my + mc

    c_all = _allgather_rows(c, "gather_c").reshape(8, D)
    n_ada = w_ada.shape[2]
    b_ada_shard = lax.dynamic_slice(b_ada, (0, chip * n_ada), (1, n_ada))
    ada_part = _ada_fwd(c_all, w_ada[0], b_ada_shard)
    ada_all = _allgather_rows(ada_part.reshape(1, 8 * n_ada), "gather_ada").reshape(4, 2, 8, n_ada)
    ada_row = lax.dynamic_slice(ada_all, (0, mc, seq, 0), (4, 1, 1, n_ada)).reshape(1, N_COND * D)
    ada = [ada_row[:, j * D:(j + 1) * D] for j in range(N_COND)]

    w_in_g = _gather_two_level(_shard2d(w_in, "cols_t").astype(MXU), "gather_w_in")
    wi = _unshard(w_in_g, "cols_t")
    w_all = jnp.concatenate([wi[:OFF_FGATE + N_FGATE], jnp.zeros((LANE - N_FGATE, D), MXU), wi[OFF_FGATE + N_FGATE:]], axis=0)
    bf_pad = jnp.concatenate([b_forget, jnp.zeros((1, LANE - N_FGATE), F32)], axis=1)
    late = _BIG[1:]
    late_riders = [_shard2d(given[n], how).astype(MXU) for n, how in late]
    pieces = {}

    def late_full(gathered):
        return [_unshard(g, how) for (_, how), g in zip(late, gathered)]

    def early_grads(dw):
        for n, how in late:
            pieces[n] = _reshard(dw[n][0], how)
        return [_reshard(dw[n][1], how) for n, how in late]

    def w_in_grads(dwi):
        pieces["w_in"] = _reshard(jnp.concatenate(dwi[:6] + [dwi[6][:N_FGATE], dwi[7]], axis=0), "cols_t")
        return [pieces["w_in"].astype(MXU)]

    out = _local_step(x[0], loss_target[0], ada, w_all, b_gate, bf_pad, (late_riders, late_full), early_grads, w_in_grads,
                      ln1_g, ln1_b, ln2_g, ln2_b)

    row = jnp.concatenate(out["d_ada"] + [out["dln1_g"], out["dln1_b"], out["dln2_g"], out["dln2_b"], out["db_gate"],
                                          out["db_forget"], out["loss_lanes"]], axis=1)
    rows = _allgather_rows(row, "gather_small")
    small = _small_update(rows, [(given[p], given["m_" + p], given["v_" + p]) for p, _, _ in _SMALL_PARAMS])
    loss = small[0].reshape(())
    res = {}
    for j, (p, _, _) in enumerate(_SMALL_PARAMS):
        res[p] = small[1 + 4 * j:5 + 4 * j]

    dada_all = rows.reshape(8, _SMALL_LEN)[:, :N_COND * D]
    dada_shard = lax.dynamic_slice(dada_all, (0, chip * n_ada), (8, n_ada))
    g_ada = _ada_bwd(c_all, dada_shard)
    res["w_ada"] = [a[None] for a in _adamw(w_ada[0], m_w_ada[0], v_w_ada[0], [g_ada], "adamw_w_ada")]

    received = dict(zip([n for n, _ in late], out["early_received"]))
    (received["w_in"],) = out["w_in_received"]
    chip_arr = jnp.reshape(chip, (1,)).astype(jnp.int32)
    partial = [_reduce_chips(chip_arr, pieces[n], received[n], "reduce_" + n) for n, _ in _BIG]
    theirs = _sibling_exchange(partial, "swap_cores")
    for (n, how), mine, other in zip(_BIG, partial, theirs):
        upd = _adamw(_shard2d(given[n], how), _shard2d(given["m_" + n], how), _shard2d(given["v_" + n], how), [mine, other], "adamw_" + n)
        res[n] = [(a.T if how == "cols_t" else a)[None] for a in upd]

    order = ["w_ada", "b_ada", "w_in", "b_gate", "b_forget", "w_sb_out", "w_fox_out", "w_o", "ln1_g", "ln1_b",
             "w_ffn_gate", "w_ffn_up", "w_ffn_down", "ln2_g", "ln2_b"]
    return (loss, out["grad_x"][None], *[res[n][0] for n in order], *[res[n][1] for n in order],
            *[res[n][2] for n in order], *[res[n][3] for n in order])
```

```python
import functools

import jax
import jax.numpy as jnp
from jax import lax
from jax.experimental import pallas as pl
from jax.experimental.pallas import tpu as pltpu

F32 = jnp.float32
MXU = jnp.bfloat16

D = 1024
HEAD_DIM = 64
WIDTH = 512
D_FF = 2816
N_COND = 6
LN_EPS = 1e-5
ALPHA = 2.0 ** 0.25
QK_SCALE = HEAD_DIM ** -0.5
OFF_FGATE = 6 * WIDTH
N_FGATE = 8
IN_COLS = OFF_FGATE + N_FGATE + 2 * D
LANE = 128
W_ALL_COLS = OFF_FGATE + LANE + 2 * D
TQ = 512
SB_TQ = 256
ADAM_LR, ADAM_B1, ADAM_B2, ADAM_EPS, ADAM_WD, ADAM_STEP = 0.001, 0.9, 0.999, 1e-08, 0.01, 10
NEG = -1e30
DEAD_LOG = -120.0
RS_COUNT_LANE = LANE - 1
MESH_AXES = ("x", "y", "c")
VMEM_BIG = 56 * 1024 * 1024


def _dot(a, b):
    return jnp.dot(a, b, preferred_element_type=F32)


def _dot_nt(a, b):
    return lax.dot_general(a, b, (((1,), (1,)), ((), ())), preferred_element_type=F32)


def _dot_tn(a, b):
    return lax.dot_general(a, b, (((0,), (0,)), ((), ())), preferred_element_type=F32)


def _ln(x):
    mu = jnp.mean(x, axis=-1, keepdims=True)
    xc = x - mu
    var = jnp.mean(xc * xc, axis=-1, keepdims=True)
    rstd = lax.rsqrt(var + LN_EPS)
    return xc * rstd, rstd


def _ln_bwd(dxhat, xhat, rstd):
    return rstd * (dxhat - jnp.mean(dxhat, axis=-1, keepdims=True) - xhat * jnp.mean(dxhat * xhat, axis=-1, keepdims=True))


def _sigmoid(x):
    return 1.0 / (1.0 + jnp.exp(-x))


def _colsum(x):
    return jnp.sum(x, axis=0, keepdims=True)


def _split(x):
    hi = x.astype(MXU)
    lo = (x - hi.astype(F32)).astype(MXU)
    return jnp.concatenate([hi, lo], axis=1)


def _rows(tm, n):
    return pl.BlockSpec((tm, n), lambda i: (i, 0))


def _fixed(r, n):
    return pl.BlockSpec((r, n), lambda i: (0, 0))


def _res(a):
    return pl.BlockSpec(a.shape, lambda i: (0, 0), pipeline_mode=pl.Buffered(1))


def _params(limit=None, sem=None):
    return pltpu.CompilerParams(vmem_limit_bytes=limit, dimension_semantics=sem)


def _in_proj(x, sh1, sc1, w_all, b_gate):
    s = x.shape[0]
    tm = 256

    def body(x_ref, sh_ref, sc_ref, w_ref, bg_ref, u_ref, qkv_ref, fl_ref, gl_ref):
        xhat, _ = _ln(x_ref[...])
        u = (xhat * (1.0 + sc_ref[...]) + sh_ref[...]).astype(MXU)
        u_ref[...] = u
        for c0 in range(0, OFF_FGATE, WIDTH):
            p = _dot_nt(u, w_ref[c0:c0 + WIDTH, :])
            if c0 in (0, 3 * WIDTH):
                p = p * QK_SCALE
            qkv_ref[:, c0:c0 + WIDTH] = p.astype(MXU)
        fl_ref[...] = _dot_nt(u, w_ref[OFF_FGATE:OFF_FGATE + LANE, :])
        for c0 in range(0, 2 * D, D):
            gl_ref[:, c0:c0 + D] = _dot_nt(u, w_ref[OFF_FGATE + LANE + c0:OFF_FGATE + LANE + c0 + D, :]) + bg_ref[:, c0:c0 + D]

    return pl.pallas_call(
        body, name="in_proj", grid=(s // tm,),
        in_specs=[_rows(tm, D), _fixed(1, D), _fixed(1, D), _res(w_all), _fixed(1, 2 * D)],
        out_specs=[_rows(tm, D), _rows(tm, OFF_FGATE), _rows(tm, LANE), _rows(tm, 2 * D)],
        out_shape=[jax.ShapeDtypeStruct((s, D), MXU), jax.ShapeDtypeStruct((s, OFF_FGATE), MXU),
                   jax.ShapeDtypeStruct((s, LANE), F32), jax.ShapeDtypeStruct((s, 2 * D), F32)],
        compiler_params=_params(VMEM_BIG),
    )(x, sh1, sc1, w_all, b_gate)


def _log_sigmoid_parts(z):
    e = jnp.exp(-jnp.abs(z))
    return -(jnp.maximum(z, 0.0) + jnp.log(1.0 + e)), e


def _fcum_fwd(fl, bf):
    s = fl.shape[0]
    nb = s // LANE

    def body(fl_ref, bf_ref, fc_ref, fkt_ref):
        r = lax.broadcasted_iota(jnp.int32, (LANE, LANE), 0)
        c = lax.broadcasted_iota(jnp.int32, (LANE, LANE), 1)
        tri = (c <= r).astype(F32)

        def step(b, carry):
            r0 = pl.multiple_of(b * LANE, LANE)
            xb = fl_ref[pl.ds(r0, LANE), :] + bf_ref[...]
            ls = _log_sigmoid_parts(-xb)[0]
            cs = jnp.dot(tri, ls, precision=lax.Precision.HIGHEST, preferred_element_type=F32) + carry
            fc_ref[pl.ds(r0, LANE), :] = cs
            fkt_ref[b] = cs.T[:N_FGATE, :]
            return cs[LANE - 1:LANE, :]

        lax.fori_loop(0, nb, step, jnp.zeros((1, LANE), F32))

    return pl.pallas_call(
        body, name="fcum_fwd",
        out_shape=[jax.ShapeDtypeStruct((s, LANE), F32), jax.ShapeDtypeStruct((nb, N_FGATE, LANE), F32)],
    )(fl, bf)


def _attn_specs(s, col0, tq):
    return [pl.BlockSpec((tq, LANE), lambda hp, i: (i, col0 + hp)),
            pl.BlockSpec((s, LANE), lambda hp, i: (0, col0 + 4 + hp)),
            pl.BlockSpec((s, LANE), lambda hp, i: (0, col0 + 8 + hp))]


def _tile_iotas(tq):
    lane = lax.broadcasted_iota(jnp.int32, (tq, LANE), 1)
    row = lax.broadcasted_iota(jnp.int32, (tq, tq), 0)
    col = lax.broadcasted_iota(jnp.int32, (tq, tq), 1)
    return lane, row, col


def _sub_blocks(nk):
    return [slice(j * LANE, (j + 1) * LANE) for j in range(nk // LANE)]


def _over_strips(tile, tq):
    return tile(slice(0, tq), tq)


def _tri(below):
    r = lax.broadcasted_iota(jnp.int32, (LANE, LANE), 0)
    c = lax.broadcasted_iota(jnp.int32, (LANE, LANE), 1)
    t = jnp.concatenate([((r > c) if below else (r < c)).astype(MXU), jnp.ones((LANE, LANE), MXU)], axis=1)
    return jnp.concatenate([t, t], axis=0)


def _call_with_riders(body, name, grid, in_specs, out_specs, out_shape, scratch, args, riders, gather, limit=None):
    nr, n_in, n_out, n_sc = len(riders), len(in_specs), len(out_specs), len(scratch)

    def at_step(which):
        hit = None
        for d, n in enumerate(grid):
            here = pl.program_id(d) == (0 if which == "first" else n - 1)
            hit = here if hit is None else hit & here
        return hit

    def wrapped(*refs):
        ins, rin = refs[:n_in], refs[n_in:n_in + nr]
        outs, rout = refs[n_in + nr:n_in + nr + n_out], refs[n_in + nr + n_out:n_in + 2 * nr + n_out]
        own, sems = refs[n_in + 2 * nr + n_out:n_in + 2 * nr + n_out + n_sc], refs[n_in + 2 * nr + n_out + n_sc:]
        if nr:
            @pl.when(at_step("first"))
            def _():
                _exchange_start(rin, rout, sems, gather)

        body(*ins, *outs, *own)
        if nr:
            @pl.when(at_step("last"))
            def _():
                _exchange_wait(rin, rout, sems, gather)

    res = pl.pallas_call(
        wrapped, name=name, grid=grid,
        in_specs=list(in_specs) + [_ANY] * nr, out_specs=list(out_specs) + [_ANY] * nr,
        out_shape=list(out_shape) + _exchange_out_shape(riders),
        scratch_shapes=list(scratch) + (_exchange_sems(nr) if nr else []),
        compiler_params=_params(limit),
    )(*args, *riders)
    return res[:n_out], res[n_out:]


def _sb_fwd(qkv, riders=()):
    s = qkv.shape[0]
    tq = SB_TQ
    nq = s // tq
    assert nq <= RS_COUNT_LANE

    def body(q_ref, k_ref, v_ref, o_ref, rs_ref):
        i = pl.program_id(1)
        lane, row, col = _tile_iotas(tq)
        u2 = _tri(True)
        diag = col < row
        q = q_ref[...]
        qms = [jnp.where(hm, q, jnp.zeros_like(q)) for hm in (lane < HEAD_DIM, lane >= HEAD_DIM)]

        def step(kb, carry, masked):
            k0 = pl.multiple_of(kb * tq, tq)
            k = k_ref[pl.ds(k0, tq), :]
            v = v_ref[pl.ds(k0, tq), :]
            def tile(rows, nk, qm, state):
                run, acc, rt = (t[rows] for t in state)
                z = _dot_nt(qm[rows], k[:nk])
                lneg, _ = _log_sigmoid_parts(z)
                lpos = z + lneg
                if masked:
                    lneg = jnp.where(diag[rows, :nk], lneg, 0.0)
                rt = jnp.where(lane[rows] == kb, run, rt)
                a = []
                for sl in reversed(_sub_blocks(nk)):
                    st = _dot(_split(lneg[:, sl]), u2)
                    a.append(jnp.exp(lpos[:, sl] + st[:, :LANE] + run))
                    run = run + st[:, LANE:]
                a = jnp.concatenate(a[::-1], axis=1)
                if masked:
                    a = jnp.where(diag[rows, :nk], a, 0.0)
                return run, acc + _dot(a.astype(MXU), v[:nk]), rt

            return tuple(_over_strips(functools.partial(tile, qm=qm, state=state), tq) for qm, state in zip(qms, carry))

        zero = jnp.zeros((tq, LANE), F32)
        carry = step(i, ((zero, zero, zero),) * 2, True)

        def alive(cr):
            return jnp.maximum(jnp.max(cr[0][0]), jnp.max(cr[1][0])) > DEAD_LOG

        def walk(state):
            j, _, cr = state
            cr = step(i - 1 - j, cr, False)
            return j + 1, alive(cr), cr

        walked, _, carry = lax.while_loop(lambda state: (state[0] < i) & state[1], walk, (jnp.int32(0), alive(carry), carry))
        count = walked.astype(F32)
        rs_ref[0] = jnp.where(lane == RS_COUNT_LANE, count, carry[0][2])
        rs_ref[1] = jnp.where(lane == RS_COUNT_LANE, count, carry[1][2])
        o_ref[...] = jnp.where(lane < HEAD_DIM, carry[0][1], carry[1][1]).astype(o_ref.dtype)

    return _call_with_riders(
        body, "sb_fwd", (4, nq), _attn_specs(s, 0, tq),
        [pl.BlockSpec((tq, LANE), lambda hp, i: (i, hp)), pl.BlockSpec((2, tq, LANE), lambda hp, i: (hp, i, 0))],
        [jax.ShapeDtypeStruct((s, WIDTH), MXU), jax.ShapeDtypeStruct((8, s, LANE), F32)], [], (qkv, qkv, qkv), riders, True)


def _sb_bwd(qkv, do, rs, riders=()):
    s = qkv.shape[0]
    tq = SB_TQ
    nq = s // tq

    def body(q_ref, k_ref, v_ref, do_ref, rs_ref, dq_ref, dk_ref, dv_ref, dk_acc, dv_acc):
        i = pl.program_id(1)

        @pl.when(i == 0)
        def _():
            dk_acc[...] = jnp.zeros_like(dk_acc)
            dv_acc[...] = jnp.zeros_like(dv_acc)

        lane, row, col = _tile_iotas(tq)
        u2 = _tri(True)
        l2 = _tri(False)
        diag = col < row
        q = q_ref[...]
        do = do_ref[...]
        heads = [(jnp.where(hm, q, jnp.zeros_like(q)), jnp.where(hm, do, jnp.zeros_like(do)), rs_ref[hh])
                 for hh, hm in enumerate((lane < HEAD_DIM, lane >= HEAD_DIM))]

        def step(kb, carry, masked):
            k0 = pl.multiple_of(kb * tq, tq)
            k = k_ref[pl.ds(k0, tq), :]
            v = v_ref[pl.ds(k0, tq), :]
            to_keys = {}

            def tile(rows, nk, qm, dom, rblk, state):
                gpre, dq = (t[rows] for t in state)
                z = _dot_nt(qm[rows], k[:nk])
                lneg, e = _log_sigmoid_parts(z)
                lpos = z + lneg
                if masked:
                    lneg = jnp.where(diag[rows, :nk], lneg, 0.0)
                run = jnp.sum(jnp.where(lane[rows] == kb, rblk[rows], 0.0), axis=1, keepdims=True) + jnp.zeros_like(gpre)
                a = []
                for sl in reversed(_sub_blocks(nk)):
                    st = _dot(_split(lneg[:, sl]), u2)
                    a.append(jnp.exp(lpos[:, sl] + st[:, :LANE] + run))
                    run = run + st[:, LANE:]
                a = jnp.concatenate(a[::-1], axis=1)
                if masked:
                    a = jnp.where(diag[rows, :nk], a, 0.0)
                g = a * _dot_nt(dom[rows], v[:nk])
                pre = []
                for sl in _sub_blocks(nk):
                    pt = _dot(_split(g[:, sl]), l2)
                    pre.append(gpre + pt[:, :LANE])
                    gpre = gpre + pt[:, LANE:]
                sig = jnp.where(z >= 0.0, 1.0, e) / (1.0 + e)
                dz = g - (g + jnp.concatenate(pre, axis=1)) * sig
                if masked:
                    dz = jnp.where(diag[rows, :nk], dz, 0.0)
                dzb = dz.astype(MXU)
                both = to_keys.setdefault(nk, [0.0, 0.0])
                both[0] = both[0] + _dot_tn(dzb, qm[rows])
                both[1] = both[1] + _dot_tn(a.astype(MXU), dom[rows])
                return gpre, dq + _dot(dzb, k[:nk])

            new = tuple(_over_strips(functools.partial(tile, qm=qm, dom=dom, rblk=rblk, state=state), tq)
                        for (qm, dom, rblk), state in zip(heads, carry))
            for nk, (dk, dv) in to_keys.items():
                dk_acc[pl.ds(k0, nk), :] += dk
                dv_acc[pl.ds(k0, nk), :] += dv
            return new

        walked = jnp.max(jnp.where(lane[:8] == RS_COUNT_LANE, rs_ref[0, 0:8, :], 0.0))
        first = i - jnp.clip(walked.astype(jnp.int32), 0, i)
        zero = jnp.zeros((tq, LANE), F32)
        carry = step(i, lax.fori_loop(first, i, lambda kb, cr: step(kb, cr, False), ((zero, zero),) * 2), True)
        dq_ref[...] = (jnp.where(lane < HEAD_DIM, carry[0][1], carry[1][1]) * QK_SCALE).astype(dq_ref.dtype)

        @pl.when(i == nq - 1)
        def _():
            dk_ref[...] = dk_acc[...].astype(dk_ref.dtype)
            dv_ref[...] = dv_acc[...].astype(dv_ref.dtype)

    blk = pl.BlockSpec((tq, LANE), lambda hp, i: (i, hp))
    whole = pl.BlockSpec((s, LANE), lambda hp, i: (0, hp))
    return _call_with_riders(
        body, "sb_bwd", (4, nq), _attn_specs(s, 0, tq) + [blk, pl.BlockSpec((2, tq, LANE), lambda hp, i: (hp, i, 0))],
        [blk, whole, whole], [jax.ShapeDtypeStruct((s, WIDTH), MXU)] * 3,
        [pltpu.VMEM((s, LANE), F32), pltpu.VMEM((s, LANE), F32)], (qkv, qkv, qkv, do, rs), riders, False)


def _key_bias(fkt_ref, kb, h, tq):
    n_sub = tq // LANE
    return jnp.concatenate([fkt_ref[kb * n_sub + j, pl.ds(h, 1), :] for j in range(n_sub)], axis=1)


def _fox_fwd(qkv, fc, fkt, riders=()):
    s = qkv.shape[0]
    tq = TQ
    nq = s // tq
    nb = fkt.shape[0]

    def body(q_ref, k_ref, v_ref, fq_ref, fkt_ref, o_ref, lse_ref):
        hp = pl.program_id(0)
        i = pl.program_id(1)
        lane, row, col = _tile_iotas(tq)
        diag = col <= row
        q = q_ref[...]
        fqb = fq_ref[...]
        heads = []
        for hh in range(2):
            h = 2 * hp + hh
            hm = (lane >= HEAD_DIM) if hh else (lane < HEAD_DIM)
            heads.append((h, jnp.where(hm, q, jnp.zeros_like(q)), jnp.sum(jnp.where(lane == h, fqb, 0.0), axis=1, keepdims=True)))

        def step(kb, carry, masked):
            k0 = pl.multiple_of(kb * tq, tq)
            k = k_ref[pl.ds(k0, tq), :]
            v = v_ref[pl.ds(k0, tq), :]
            def tile(rows, nk, h, qm, fq, state):
                m, l, acc = (t[rows] for t in state)
                z = _dot_nt(qm[rows], k[:nk]) + fq[rows] - _key_bias(fkt_ref, kb, h, tq)[:, :nk]
                if masked:
                    z = jnp.where(diag[rows, :nk], z, NEG)
                mn = jnp.maximum(m, jnp.max(z, axis=1, keepdims=True))
                p = jnp.exp(z - mn)
                alpha = jnp.exp(m - mn)
                return mn, alpha * l + jnp.sum(p, axis=1, keepdims=True), alpha * acc + _dot(p.astype(MXU), v[:nk])

            return tuple(_over_strips(functools.partial(tile, h=h, qm=qm, fq=fq, state=state), tq)
                         for (h, qm, fq), state in zip(heads, carry))

        init = ((jnp.full((tq, 1), NEG, F32), jnp.zeros((tq, 1), F32), jnp.zeros((tq, LANE), F32)),) * 2
        carry = step(i, lax.fori_loop(0, i, lambda kb, cr: step(kb, cr, False), init), True)
        outs = []
        for hh, (m, l, acc) in enumerate(carry):
            outs.append(acc / l)
            lse_ref[hh] = jnp.broadcast_to(m + jnp.log(l), (tq, LANE))
        o_ref[...] = jnp.where(lane < HEAD_DIM, outs[0], outs[1]).astype(o_ref.dtype)

    return _call_with_riders(
        body, "fox_fwd", (4, nq),
        _attn_specs(s, 12, tq) + [pl.BlockSpec((tq, LANE), lambda hp, i: (i, 0)), pl.BlockSpec((nb, N_FGATE, LANE), lambda hp, i: (0, 0, 0))],
        [pl.BlockSpec((tq, LANE), lambda hp, i: (i, hp)), pl.BlockSpec((2, tq, LANE), lambda hp, i: (hp, i, 0))],
        [jax.ShapeDtypeStruct((s, WIDTH), MXU), jax.ShapeDtypeStruct((8, s, LANE), F32)], [], (qkv, qkv, qkv, fc, fkt), riders, True)


def _fox_bwd(qkv, fc, fkt, do, o, lse, riders=()):
    s = qkv.shape[0]
    tq = TQ
    nq = s // tq
    nb = fkt.shape[0]

    def body(q_ref, k_ref, v_ref, fq_ref, fkt_ref, do_ref, o_ref, lse_ref, dq_ref, dk_ref, dv_ref, dfk_ref, dfq_ref, dk_acc, dv_acc):
        hp = pl.program_id(0)
        i = pl.program_id(1)

        @pl.when(i == 0)
        def _():
            dk_acc[...] = jnp.zeros_like(dk_acc)
            dv_acc[...] = jnp.zeros_like(dv_acc)

        @pl.when((i == 0) & (hp == 0))
        def _():
            dfk_ref[...] = jnp.zeros_like(dfk_ref)

        lane, row, col = _tile_iotas(tq)
        diag = col <= row
        q = q_ref[...]
        do = do_ref[...]
        dof = do.astype(F32) * o_ref[...].astype(F32)
        fqb = fq_ref[...]
        heads = []
        for hh in range(2):
            h = 2 * hp + hh
            hm = (lane >= HEAD_DIM) if hh else (lane < HEAD_DIM)
            heads.append((h, jnp.where(hm, q, jnp.zeros_like(q)), jnp.where(hm, do, jnp.zeros_like(do)),
                          jnp.sum(jnp.where(hm, dof, 0.0), axis=1, keepdims=True),
                          jnp.sum(jnp.where(lane == h, fqb, 0.0), axis=1, keepdims=True), lse_ref[hh][:, :1]))

        def step(kb, carry, masked):
            k0 = pl.multiple_of(kb * tq, tq)
            k = k_ref[pl.ds(k0, tq), :]
            v = v_ref[pl.ds(k0, tq), :]
            to_keys = {}

            def tile(rows, nk, h, qm, dom, delta, fq, lse_t, state):
                dq, rsum = (t[rows] for t in state)
                z = _dot_nt(qm[rows], k[:nk]) + fq[rows] - _key_bias(fkt_ref, kb, h, tq)[:, :nk]
                if masked:
                    z = jnp.where(diag[rows, :nk], z, NEG)
                p = jnp.exp(z - lse_t[rows])
                ds = p * (_dot_nt(dom[rows], v[:nk]) - delta[rows])
                dsb = ds.astype(MXU)
                both = to_keys.setdefault(nk, [0.0, 0.0])
                both[0] = both[0] + _dot_tn(dsb, qm[rows])
                both[1] = both[1] + _dot_tn(p.astype(MXU), dom[rows])
                csum = _colsum(ds)
                for j, sl in enumerate(_sub_blocks(nk)):
                    dfk_ref[kb * (tq // LANE) + j, pl.ds(h, 1), :] += -csum[:, sl]
                return dq + _dot(dsb, k[:nk]), rsum + jnp.sum(ds, axis=1, keepdims=True)

            new = tuple(_over_strips(functools.partial(tile, h=h, qm=qm, dom=dom, delta=delta, fq=fq, lse_t=lse_t, state=state), tq)
                        for (h, qm, dom, delta, fq, lse_t), state in zip(heads, carry))
            for nk, (dk, dv) in to_keys.items():
                dk_acc[pl.ds(k0, nk), :] += dk
                dv_acc[pl.ds(k0, nk), :] += dv
            return new

        init = ((jnp.zeros((tq, LANE), F32), jnp.zeros((tq, 1), F32)),) * 2
        carry = step(i, lax.fori_loop(0, i, lambda kb, cr: step(kb, cr, False), init), True)
        dq_ref[...] = (jnp.where(lane < HEAD_DIM, carry[0][0], carry[1][0]) * QK_SCALE).astype(dq_ref.dtype)
        dfq_ref[0] = jnp.where(lane == heads[0][0], carry[0][1], jnp.where(lane == heads[1][0], carry[1][1], 0.0))

        @pl.when(i == nq - 1)
        def _():
            dk_ref[...] = dk_acc[...].astype(dk_ref.dtype)
            dv_ref[...] = dv_acc[...].astype(dv_ref.dtype)

    blk = pl.BlockSpec((tq, LANE), lambda hp, i: (i, hp))
    whole = pl.BlockSpec((s, LANE), lambda hp, i: (0, hp))
    pair = pl.BlockSpec((2, tq, LANE), lambda hp, i: (hp, i, 0))
    fkt_spec = pl.BlockSpec((nb, N_FGATE, LANE), lambda hp, i: (0, 0, 0))
    return _call_with_riders(
        body, "fox_bwd", (4, nq),
        _attn_specs(s, 12, tq) + [pl.BlockSpec((tq, LANE), lambda hp, i: (i, 0)), fkt_spec, blk, blk, pair],
        [blk, whole, whole, fkt_spec, pl.BlockSpec((1, tq, LANE), lambda hp, i: (hp, i, 0))],
        [jax.ShapeDtypeStruct((s, WIDTH), MXU)] * 3
        + [jax.ShapeDtypeStruct((nb, N_FGATE, LANE), F32), jax.ShapeDtypeStruct((4, s, LANE), F32)],
        [pltpu.VMEM((s, LANE), F32), pltpu.VMEM((s, LANE), F32)], (qkv, qkv, qkv, fc, fkt, do, o, lse), riders, False)


def _fcum_bwd(dfkt, dfq, fl, bf):
    s = fl.shape[0]
    nb = s // LANE

    def body(dfkt_ref, dfq_ref, fl_ref, bf_ref, df_ref, dbf_ref, tail_ref):
        @pl.when(pl.program_id(0) == 0)
        def _():
            tail_ref[...] = jnp.zeros_like(tail_ref)
            dbf_ref[...] = jnp.zeros_like(dbf_ref)

        r = lax.broadcasted_iota(jnp.int32, (LANE, LANE), 0)
        c = lax.broadcasted_iota(jnp.int32, (LANE, LANE), 1)
        tri = (c >= r).astype(F32)
        dfc = jnp.concatenate([dfkt_ref[0], jnp.zeros((LANE - N_FGATE, LANE), F32)], axis=0).T
        dfc = dfc + ((dfq_ref[0] + dfq_ref[1]) + (dfq_ref[2] + dfq_ref[3]))
        dls = jnp.dot(tri, dfc, precision=lax.Precision.HIGHEST, preferred_element_type=F32) + tail_ref[...]
        xb = fl_ref[...] + bf_ref[...]
        e = jnp.exp(-jnp.abs(xb))
        dfl = dls * (jnp.where(xb >= 0.0, e, 1.0) / (1.0 + e))
        df_ref[...] = dfl.astype(df_ref.dtype)
        tail_ref[...] = dls[0:1, :]
        dbf_ref[...] += _colsum(dfl)

    return pl.pallas_call(
        body, name="fcum_bwd", grid=(nb,),
        in_specs=[pl.BlockSpec((1, N_FGATE, LANE), lambda j: (nb - 1 - j, 0, 0)), pl.BlockSpec((4, LANE, LANE), lambda j: (0, nb - 1 - j, 0)),
                  pl.BlockSpec((LANE, LANE), lambda j: (nb - 1 - j, 0)), _fixed(1, LANE)],
        out_specs=[pl.BlockSpec((LANE, LANE), lambda j: (nb - 1 - j, 0)), _fixed(1, LANE)],
        out_shape=[jax.ShapeDtypeStruct((s, LANE), MXU), jax.ShapeDtypeStruct((1, LANE), F32)],
        scratch_shapes=[pltpu.VMEM((1, LANE), F32)],
    )(dfkt, dfq, fl, bf)


def _mix_fwd(x, o_sb, o_fx, gl, w_sb, w_fx, w_o, g1, ln1_g, ln1_b, sh2, sc2):
    s = x.shape[0]
    tm = 256

    def body(x_ref, osb_ref, ofx_ref, gl_ref, wsb_ref, wfx_ref, wo_ref, g1_ref, lg_ref, lb_ref, sh_ref, sc_ref, r1_ref, u2_ref):
        mixin = (_sigmoid(gl_ref[:, :D]) * _dot(osb_ref[...], wsb_ref[...])
                 + _sigmoid(gl_ref[:, D:]) * _dot(ofx_ref[...], wfx_ref[...]))
        r1 = ALPHA * x_ref[...] + g1_ref[...] * _dot(mixin.astype(MXU), wo_ref[...])
        r1_ref[...] = r1
        x1 = _ln(r1)[0] * lg_ref[...] + lb_ref[...]
        u2_ref[...] = (_ln(x1)[0] * (1.0 + sc_ref[...]) + sh_ref[...]).astype(MXU)

    vec = _fixed(1, D)
    return pl.pallas_call(
        body, name="mix_fwd", grid=(s // tm,),
        in_specs=[_rows(tm, D), _rows(tm, WIDTH), _rows(tm, WIDTH), _rows(tm, 2 * D), _res(w_sb), _res(w_fx), _res(w_o),
                  vec, vec, vec, vec, vec],
        out_specs=[_rows(tm, D), _rows(tm, D)],
        out_shape=[jax.ShapeDtypeStruct((s, D), F32), jax.ShapeDtypeStruct((s, D), MXU)],
        compiler_params=_params(VMEM_BIG),
    )(x, o_sb, o_fx, gl, w_sb, w_fx, w_o, g1, ln1_g, ln1_b, sh2, sc2)


def _ffn_fwd(r1, u2, tgt, w_g, w_u, w_d, g2, ln1_g, ln1_b, ln2_g, ln2_b):
    s = r1.shape[0]
    tm = 256

    def body(r1_ref, u2_ref, t_ref, wg_ref, wu_ref, wd_ref, g2_ref, l1g_ref, l1b_ref, l2g_ref, l2b_ref,
             hg_ref, hu_ref, dxa_ref, dh_ref, acc_ref):
        @pl.when(pl.program_id(0) == 0)
        def _():
            acc_ref[...] = jnp.zeros_like(acc_ref)

        u2 = u2_ref[...]
        hg = _dot_nt(u2, wg_ref[...])
        hu = _dot_nt(u2, wu_ref[...])
        hg_ref[...] = hg
        hu_ref[...] = hu
        h = _dot((hg * _sigmoid(hg) * hu).astype(MXU), wd_ref[...])
        x1 = _ln(r1_ref[...])[0] * l1g_ref[...] + l1b_ref[...]
        xh2, rstd2 = _ln(ALPHA * x1 + g2_ref[...] * h)
        err = xh2 * l2g_ref[...] + l2b_ref[...] - t_ref[...]
        dy = err * (1.0 / D)
        dr2 = _ln_bwd(dy * l2g_ref[...], xh2, rstd2)
        dxa_ref[...] = ALPHA * dr2
        dh_ref[...] = (g2_ref[...] * dr2).astype(MXU)
        acc_ref[0:1, :] += _colsum(dr2 * h)
        acc_ref[1:2, :] += _colsum(dy * xh2)
        acc_ref[2:3, :] += _colsum(dy)
        acc_ref[3:4, :] += _colsum(err * err) * (0.5 / D)

    vec = _fixed(1, D)
    return pl.pallas_call(
        body, name="ffn_fwd", grid=(s // tm,),
        in_specs=[_rows(tm, D), _rows(tm, D), _rows(tm, D), _res(w_g), _res(w_u), _res(w_d), vec, vec, vec, vec, vec],
        out_specs=[_rows(tm, D_FF), _rows(tm, D_FF), _rows(tm, D), _rows(tm, D), _fixed(8, D)],
        out_shape=[jax.ShapeDtypeStruct((s, D_FF), F32), jax.ShapeDtypeStruct((s, D_FF), F32),
                   jax.ShapeDtypeStruct((s, D), F32), jax.ShapeDtypeStruct((s, D), MXU), jax.ShapeDtypeStruct((8, D), F32)],
        compiler_params=_params(VMEM_BIG),
    )(r1, u2, tgt, w_g, w_u, w_d, g2, ln1_g, ln1_b, ln2_g, ln2_b)


def _ffn_bwd(dh, hg, hu, w_g, w_u, w_d):
    s = dh.shape[0]
    tm = 256
    half = D_FF // 2

    def body(dh_ref, hg_ref, hu_ref, wg_ref, wu_ref, wd_ref, act_ref, dhg_ref, dhu_ref, du2_ref):
        dh = dh_ref[...]
        du2 = jnp.zeros((tm, D), F32)
        for c0 in (0, half):
            cols = slice(c0, c0 + half)
            dact = _dot_nt(dh, wd_ref[cols, :])
            hg = hg_ref[:, cols]
            hu = hu_ref[:, cols]
            sg = _sigmoid(hg)
            sl = hg * sg
            act_ref[:, cols] = (sl * hu).astype(MXU)
            dhg = (dact * hu * (sg * (1.0 + hg * (1.0 - sg)))).astype(MXU)
            dhu = (dact * sl).astype(MXU)
            dhg_ref[:, cols] = dhg
            dhu_ref[:, cols] = dhu
            du2 = du2 + _dot(dhg, wg_ref[cols, :]) + _dot(dhu, wu_ref[cols, :])
        du2_ref[...] = du2

    return pl.pallas_call(
        body, name="ffn_bwd", grid=(s // tm,),
        in_specs=[_rows(tm, D), _rows(tm, D_FF), _rows(tm, D_FF), _res(w_g), _res(w_u), _res(w_d)],
        out_specs=[_rows(tm, D_FF), _rows(tm, D_FF), _rows(tm, D_FF), _rows(tm, D)],
        out_shape=[jax.ShapeDtypeStruct((s, D_FF), MXU)] * 3 + [jax.ShapeDtypeStruct((s, D), F32)],
        compiler_params=_params(VMEM_BIG),
    )(dh, hg, hu, w_g, w_u, w_d)


def _mix_bwd(du2, dxa, r1, o_sb, o_fx, gl, w_sb, w_fx, w_o, g1, ln1_g, ln1_b, sc2):
    s = r1.shape[0]
    tm = 256
    n_tiles = s // tm

    def body(du2_ref, dxa_ref, r1_ref, osb_ref, ofx_ref, gl_ref, wsb_ref, wfx_ref, wo_ref, g1_ref, lg_ref, lb_ref, sc_ref,
             dx_ref, dosb_ref, dofx_ref, dgl_ref, dbg_ref, acc_ref, dwsb_ref, dwfx_ref, dwo_ref, nsb_ref, nfx_ref, no_ref):
        @pl.when(pl.program_id(0) == 0)
        def _():
            for ref in (acc_ref, dbg_ref, dwsb_ref, dwfx_ref, dwo_ref):
                ref[...] = jnp.zeros_like(ref)

        du2 = du2_ref[...]
        xh1, rstd1 = _ln(r1_ref[...])
        x1 = xh1 * lg_ref[...] + lb_ref[...]
        n1, rstdn = _ln(x1)
        dx1 = dxa_ref[...] + _ln_bwd(du2 * (1.0 + sc_ref[...]), n1, rstdn)
        dr1 = _ln_bwd(dx1 * lg_ref[...], xh1, rstd1)
        dx_ref[...] = ALPHA * dr1
        ysb = _dot(osb_ref[...], wsb_ref[...])
        yfx = _dot(ofx_ref[...], wfx_ref[...])
        gs = _sigmoid(gl_ref[:, :D])
        gf = _sigmoid(gl_ref[:, D:])
        mixin = (gs * ysb + gf * yfx).astype(MXU)
        mix = _dot(mixin, wo_ref[...])
        dmix = (g1_ref[...] * dr1).astype(MXU)
        dmixin = _dot_nt(dmix, wo_ref[...])
        dysb = (dmixin * gs).astype(MXU)
        dyfx = (dmixin * gf).astype(MXU)
        dwo_ref[...] += _dot_tn(mixin, dmix)
        dwsb_ref[...] += _dot_tn(osb_ref[...], dysb)
        dwfx_ref[...] += _dot_tn(ofx_ref[...], dyfx)
        dosb_ref[...] = _dot_nt(dysb, wsb_ref[...]).astype(MXU)
        dofx_ref[...] = _dot_nt(dyfx, wfx_ref[...]).astype(MXU)
        dgs = dmixin * ysb * gs * (1.0 - gs)
        dgf = dmixin * yfx * gf * (1.0 - gf)
        dgl_ref[:, :D] = dgs.astype(MXU)
        dgl_ref[:, D:] = dgf.astype(MXU)
        dbg_ref[:, :D] += _colsum(dgs)
        dbg_ref[:, D:] += _colsum(dgf)
        acc_ref[0:1, :] += _colsum(du2)
        acc_ref[1:2, :] += _colsum(du2 * n1)
        acc_ref[2:3, :] += _colsum(dx1 * xh1)
        acc_ref[3:4, :] += _colsum(dx1)
        acc_ref[4:5, :] += _colsum(dr1 * mix)

        @pl.when(pl.program_id(0) == n_tiles - 1)
        def _():
            nsb_ref[...] = dwsb_ref[...].astype(MXU)
            nfx_ref[...] = dwfx_ref[...].astype(MXU)
            no_ref[...] = dwo_ref[...].astype(MXU)

    vec = _fixed(1, D)
    dw_specs = [_fixed(WIDTH, D), _fixed(WIDTH, D), _fixed(D, D)]
    dw_shapes = [(WIDTH, D), (WIDTH, D), (D, D)]
    return pl.pallas_call(
        body, name="mix_bwd", grid=(n_tiles,),
        in_specs=[_rows(tm, D), _rows(tm, D), _rows(tm, D), _rows(tm, WIDTH), _rows(tm, WIDTH), _rows(tm, 2 * D),
                  _res(w_sb), _res(w_fx), _res(w_o), vec, vec, vec, vec],
        out_specs=[_rows(tm, D), _rows(tm, WIDTH), _rows(tm, WIDTH), _rows(tm, 2 * D), _fixed(1, 2 * D), _fixed(8, D)] + dw_specs * 2,
        out_shape=[jax.ShapeDtypeStruct((s, D), F32)] + [jax.ShapeDtypeStruct((s, WIDTH), MXU)] * 2
        + [jax.ShapeDtypeStruct((s, 2 * D), MXU), jax.ShapeDtypeStruct((1, 2 * D), F32), jax.ShapeDtypeStruct((8, D), F32)]
        + [jax.ShapeDtypeStruct(sh, F32) for sh in dw_shapes] + [jax.ShapeDtypeStruct(sh, MXU) for sh in dw_shapes],
        compiler_params=_params(VMEM_BIG),
    )(du2, dxa, r1, o_sb, o_fx, gl, w_sb, w_fx, w_o, g1, ln1_g, ln1_b, sc2)


def _in_bwd(pieces, x, dxa, w_all, sc1, riders=()):
    s = x.shape[0]
    tm = 256
    n_p = len(pieces)

    def body(*refs):
        p_refs = refs[:n_p]
        x_ref, dxa_ref, w_ref, sc_ref, gx_ref, acc_ref = refs[n_p:]

        @pl.when(pl.program_id(0) == 0)
        def _():
            acc_ref[...] = jnp.zeros_like(acc_ref)

        du1 = jnp.zeros((tm, D), F32)
        for p_ref, (arr, c0) in zip(p_refs, pieces):
            du1 = du1 + _dot(p_ref[...], w_ref[c0:c0 + arr.shape[1], :])
        n0, rstd0 = _ln(x_ref[...])
        gx_ref[...] = dxa_ref[...] + _ln_bwd(du1 * (1.0 + sc_ref[...]), n0, rstd0)
        acc_ref[0:1, :] += _colsum(du1)
        acc_ref[1:2, :] += _colsum(du1 * n0)

    return _call_with_riders(
        body, "in_bwd", (s // tm,),
        [_rows(tm, a.shape[1]) for a, _ in pieces] + [_rows(tm, D), _rows(tm, D), _res(w_all), _fixed(1, D)],
        [_rows(tm, D), _fixed(8, D)], [jax.ShapeDtypeStruct((s, D), F32), jax.ShapeDtypeStruct((8, D), F32)], [],
        (*[a for a, _ in pieces], x, dxa, w_all, sc1), riders, False, VMEM_BIG)


def _matmul_tn(a, b, name, narrow=False):
    s, m = a.shape
    n = b.shape[1]
    tm = 512 if m % 512 == 0 else (m if m < 512 else m // 2)
    tn = n // 2 if n > 2048 else n
    ts = 1024
    assert m % tm == 0 and tm % LANE == 0 and n % tn == 0 and tn % LANE == 0 and s % ts == 0

    def body(a_ref, b_ref, o_ref, *narrow_ref):
        @pl.when(pl.program_id(2) == 0)
        def _():
            o_ref[...] = jnp.zeros_like(o_ref)

        o_ref[...] += _dot_tn(a_ref[...], b_ref[...])
        if narrow:
            @pl.when(pl.program_id(2) == s // ts - 1)
            def _():
                narrow_ref[0][...] = o_ref[...].astype(MXU)

    out_blk = pl.BlockSpec((tm, tn), lambda i, j, k: (i, j))
    res = pl.pallas_call(
        body, name=name, grid=(m // tm, n // tn, s // ts),
        in_specs=[pl.BlockSpec((ts, tm), lambda i, j, k: (k, i)), pl.BlockSpec((ts, tn), lambda i, j, k: (k, j))],
        out_specs=[out_blk] * (2 if narrow else 1),
        out_shape=[jax.ShapeDtypeStruct((m, n), F32)] + ([jax.ShapeDtypeStruct((m, n), MXU)] if narrow else []),
        compiler_params=_params(VMEM_BIG),
    )(a, b)
    return tuple(res) if narrow else res[0]


def _local_step(x, tgt, ada, w_all, b_gate, bf_pad, late_weights, early_grads, w_in_grads, ln1_g, ln1_b, ln2_g, ln2_b):
    sh1, sc1, g1, sh2, sc2, g2 = ada
    u1, qkv, fl, gl = _in_proj(x, sh1, sc1, w_all, b_gate)
    fc, fkt = _fcum_fwd(fl, bf_pad)
    late_riders, late_full = late_weights
    n_sb = 3
    (o_sb, rs), gathered_a = _sb_fwd(qkv, late_riders[:n_sb])
    (o_fx, lse), gathered_b = _fox_fwd(qkv, fc, fkt, late_riders[n_sb:])
    w_sb, w_fx, w_o, w_g, w_u, w_d = late_full(list(gathered_a) + list(gathered_b))
    r1, u2 = _mix_fwd(x, o_sb, o_fx, gl, w_sb, w_fx, w_o, g1, ln1_g, ln1_b, sh2, sc2)
    hg, hu, dxa2, dh, acc_f = _ffn_fwd(r1, u2, tgt, w_g, w_u, w_d, g2, ln1_g, ln1_b, ln2_g, ln2_b)
    act, dhg, dhu, du2 = _ffn_bwd(dh, hg, hu, w_g, w_u, w_d)
    dxa1, dosb, dofx, dgl, dbg, acc_m, dw_sb, dw_fx, dw_o, n_sb_out, n_fx_out, n_o = _mix_bwd(
        du2, dxa2, r1, o_sb, o_fx, gl, w_sb, w_fx, w_o, g1, ln1_g, ln1_b, sc2)
    early = dict(w_sb_out=(dw_sb, n_sb_out), w_fox_out=(dw_fx, n_fx_out), w_o=(dw_o, n_o),
                 w_ffn_gate=_matmul_tn(dhg, u2, "dw_ffn_gate", True),
                 w_ffn_up=_matmul_tn(dhu, u2, "dw_ffn_up", True), w_ffn_down=_matmul_tn(act, dh, "dw_ffn_down", True))
    early_riders = early_grads(early)
    (dq_sb, dk_sb, dv_sb), received_a = _sb_bwd(qkv, dosb, rs, early_riders[:n_sb])
    (dq_fx, dk_fx, dv_fx, dfkt, dfq), received_b = _fox_bwd(qkv, fc, fkt, dofx, o_fx, lse, early_riders[n_sb:])
    early_received = list(received_a) + list(received_b)
    df, dbf = _fcum_bwd(dfkt, dfq, fl, bf_pad)
    pieces = [(dq_sb, 0), (dk_sb, WIDTH), (dv_sb, 2 * WIDTH), (dq_fx, 3 * WIDTH), (dk_fx, 4 * WIDTH), (dv_fx, 5 * WIDTH),
              (df, OFF_FGATE), (dgl, OFF_FGATE + LANE)]
    dw_in = [_matmul_tn(p, u1, f"dw_in_{j}") for j, (p, _) in enumerate(pieces)]
    (grad_x, acc_i), w_in_received = _in_bwd(pieces, x, dxa1, w_all, sc1, w_in_grads(dw_in))
    return dict(
        loss_lanes=acc_f[3:4], grad_x=grad_x, dw_in=dw_in, early=early, early_received=early_received,
        w_in_received=w_in_received,
        d_ada=[acc_i[0:1], acc_i[1:2], acc_m[4:5], acc_m[0:1], acc_m[1:2], acc_f[0:1]],
        dln1_g=acc_m[2:3], dln1_b=acc_m[3:4], dln2_g=acc_f[1:2], dln2_b=acc_f[2:3], db_gate=dbg, db_forget=dbf)


_MESH_ID = pl.DeviceIdType.MESH
_ANY = pl.BlockSpec(memory_space=pl.ANY)
_VMEM = pl.BlockSpec(memory_space=pltpu.VMEM)


def _mesh_pos():
    return lax.axis_index("x"), lax.axis_index("y"), lax.axis_index("c")


def _other_chips(x, y):
    return [(1 - x, y), (x, 1 - y), (1 - x, 1 - y)]


def _allgather_rows(v, name):
    n = v.shape[1]

    def body(v_ref, out_ref, send_sems, recv_sems, local_sem):
        x, y, c = _mesh_pos()
        me = 4 * x + 2 * y + c
        mine = pltpu.make_async_copy(v_ref, out_ref.at[me], local_sem)
        mine.start()
        copies = []
        for d in range(1, 8):
            fx, fy, fc = (d >> 2) & 1, (d >> 1) & 1, d & 1
            to = (1 - x if fx else x, 1 - y if fy else y, 1 - c if fc else c)
            cp = pltpu.make_async_remote_copy(src_ref=v_ref, dst_ref=out_ref.at[me], send_sem=send_sems.at[d - 1],
                                              recv_sem=recv_sems.at[d - 1], device_id=to, device_id_type=_MESH_ID)
            cp.start()
            copies.append(cp)
        for cp in copies:
            cp.wait_recv()
        for cp in copies:
            cp.wait_send()
        mine.wait()

    return pl.pallas_call(
        body, name=name, in_specs=[_VMEM], out_specs=_VMEM,
        out_shape=jax.ShapeDtypeStruct((8, 1, n), v.dtype),
        scratch_shapes=[pltpu.SemaphoreType.DMA((7,)), pltpu.SemaphoreType.DMA((7,)), pltpu.SemaphoreType.DMA(())],
    )(v)


def _chip_exchange(arrays, name, gather):
    nt = len(arrays)

    def body(*refs):
        ins, outs = refs[:nt], refs[nt:2 * nt]
        _exchange_start(ins, outs, refs[2 * nt:], gather)
        _exchange_wait(ins, outs, refs[2 * nt:], gather)

    return pl.pallas_call(
        body, name=name, in_specs=[_ANY] * nt, out_specs=[_ANY] * nt, out_shape=_exchange_out_shape(arrays),
        scratch_shapes=_exchange_sems(nt),
    )(*arrays)


def _exchange_out_shape(arrays):
    return [jax.ShapeDtypeStruct((4,) + a.shape[-2:], a.dtype) for a in arrays]


def _exchange_sems(nt):
    return [pltpu.SemaphoreType.DMA((3 * nt,)), pltpu.SemaphoreType.DMA((3 * nt,)), pltpu.SemaphoreType.DMA((nt,))]


def _exchange_copies(ins, outs, sems, gather):
    send_sems, recv_sems, local_sems = sems
    x, y, c = _mesh_pos()
    me = 2 * x + y
    local, remote = [], []
    for t in range(len(ins)):
        local.append(pltpu.make_async_copy(ins[t] if gather else ins[t].at[me], outs[t].at[me], local_sems.at[t]))
        for j, (px, py) in enumerate(_other_chips(x, y)):
            remote.append(pltpu.make_async_remote_copy(
                src_ref=ins[t] if gather else ins[t].at[2 * px + py], dst_ref=outs[t].at[me], send_sem=send_sems.at[3 * t + j],
                recv_sem=recv_sems.at[3 * t + j], device_id=(px, py, c), device_id_type=_MESH_ID))
    return local, remote


def _exchange_start(ins, outs, sems, gather):
    local, remote = _exchange_copies(ins, outs, sems, gather)
    for cp in local + remote:
        cp.start()


def _exchange_wait(ins, outs, sems, gather):
    local, remote = _exchange_copies(ins, outs, sems, gather)
    for cp in remote:
        cp.wait_recv()
    for cp in remote:
        cp.wait_send()
    for cp in local:
        cp.wait()


def _gather_two_level(shard, name):
    r, n = shard.shape
    half = n // 2
    assert half % LANE == 0

    def body(in_ref, out_ref, ici_send, ici_recv, d2d_send, d2d_recv, local_sem):
        x, y, c = _mesh_pos()
        me = 2 * x + y
        mine = pl.ds(pl.multiple_of(c * half, LANE), half)
        theirs = pl.ds(pl.multiple_of((1 - c) * half, LANE), half)
        local = pltpu.make_async_copy(in_ref, out_ref.at[me], local_sem)
        local.start()
        chips = _other_chips(x, y)
        over_ici = [pltpu.make_async_remote_copy(
            src_ref=in_ref.at[:, mine], dst_ref=out_ref.at[me, :, mine], send_sem=ici_send.at[j], recv_sem=ici_recv.at[j],
            device_id=(px, py, c), device_id_type=_MESH_ID) for j, (px, py) in enumerate(chips)]
        for cp in over_ici:
            cp.start()
        passed_on = [pltpu.make_async_remote_copy(
            src_ref=out_ref.at[2 * px + py, :, mine], dst_ref=out_ref.at[2 * px + py, :, mine], send_sem=d2d_send.at[j],
            recv_sem=d2d_recv.at[j], device_id=(x, y, 1 - c), device_id_type=_MESH_ID) for j, (px, py) in enumerate(chips)]
        for j, (px, py) in enumerate(chips):
            pltpu.make_async_remote_copy(
                src_ref=in_ref.at[:, mine], dst_ref=out_ref.at[2 * px + py, :, mine], send_sem=ici_send.at[j],
                recv_sem=ici_recv.at[j], device_id=(px, py, c), device_id_type=_MESH_ID).wait_recv()
            passed_on[j].start()
        for j, (px, py) in enumerate(chips):
            pltpu.make_async_remote_copy(
                src_ref=out_ref.at[2 * px + py, :, theirs], dst_ref=out_ref.at[2 * px + py, :, theirs], send_sem=d2d_send.at[j],
                recv_sem=d2d_recv.at[j], device_id=(x, y, 1 - c), device_id_type=_MESH_ID).wait_recv()
        for cp in over_ici + passed_on:
            cp.wait_send()
        local.wait()

    sems = pltpu.SemaphoreType.DMA((3,))
    return pl.pallas_call(
        body, name=name, in_specs=[_ANY], out_specs=_ANY, out_shape=jax.ShapeDtypeStruct((4, r, n), shard.dtype),
        scratch_shapes=[sems, sems, sems, sems, pltpu.SemaphoreType.DMA(())],
    )(shard)


def _sibling_exchange(arrays, name):
    nt = len(arrays)

    def body(*refs):
        ins, outs = refs[:nt], refs[nt:2 * nt]
        send_sems, recv_sems = refs[2 * nt:]
        x, y, c = _mesh_pos()
        copies = []
        for t in range(nt):
            cp = pltpu.make_async_remote_copy(src_ref=ins[t], dst_ref=outs[t], send_sem=send_sems.at[t], recv_sem=recv_sems.at[t],
                                              device_id=(x, y, 1 - c), device_id_type=_MESH_ID)
            cp.start()
            copies.append(cp)
        for cp in copies:
            cp.wait_recv()
        for cp in copies:
            cp.wait_send()

    return pl.pallas_call(
        body, name=name, in_specs=[_ANY] * nt, out_specs=[_ANY] * nt,
        out_shape=[jax.ShapeDtypeStruct(a.shape, a.dtype) for a in arrays],
        scratch_shapes=[pltpu.SemaphoreType.DMA((nt,)), pltpu.SemaphoreType.DMA((nt,))],
    )(*arrays)


def _tiles(r, n):
    for tr in (256, 352, 128):
        if r % tr == 0:
            return tr, n, r // tr, lambda i: (i, 0)
    assert n % 256 == 0
    return r, 256, n // 256, lambda i: (0, i)


def _reduce_chips(chip, pieces, recv, name):
    _, r, n = pieces.shape
    tr, tn, steps, at = _tiles(r, n)

    def body(chip_ref, own_ref, recv_ref, out_ref):
        me = chip_ref[0]
        total = jnp.zeros((tr, tn), F32)
        for k in range(4):
            total = total + jnp.where(me == k, own_ref[0], recv_ref[k].astype(F32))
        out_ref[...] = total

    return pl.pallas_call(
        body, name=name,
        grid_spec=pltpu.PrefetchScalarGridSpec(
            num_scalar_prefetch=1, grid=(steps,),
            in_specs=[pl.BlockSpec((1, tr, tn), lambda i, chip_ref: (chip_ref[0],) + at(i)),
                      pl.BlockSpec((4, tr, tn), lambda i, chip_ref: (0,) + at(i))],
            out_specs=pl.BlockSpec((tr, tn), lambda i, chip_ref: at(i))),
        out_shape=jax.ShapeDtypeStruct((r, n), F32),
    )(chip, pieces, recv)


def _adamw_math(w, g, m, v):
    m = ADAM_B1 * m + (1.0 - ADAM_B1) * g
    v = ADAM_B2 * v + (1.0 - ADAM_B2) * (g * g)
    m_hat = m / (1.0 - ADAM_B1 ** ADAM_STEP)
    v_hat = v / (1.0 - ADAM_B2 ** ADAM_STEP)
    return -ADAM_LR * (m_hat / (jnp.sqrt(v_hat) + ADAM_EPS) + ADAM_WD * w), m, v


def _adamw(w, m, v, g_parts, name):
    r, n = w.shape
    tr, tn, steps, at = _tiles(r, n)
    blk = pl.BlockSpec((tr, tn), at)
    ng = len(g_parts)

    def body(*refs):
        w_ref, m_ref, v_ref = refs[:3]
        g_refs = refs[3:3 + ng]
        g_out, d_out, m_out, v_out = refs[3 + ng:]
        g = g_refs[0][...]
        for gr in g_refs[1:]:
            g = g + gr[...]
        g_out[...] = g
        d_out[...], m_out[...], v_out[...] = _adamw_math(w_ref[...], g, m_ref[...], v_ref[...])

    return pl.pallas_call(
        body, name=name, grid=(steps,),
        in_specs=[blk] * (3 + ng), out_specs=[blk] * 4,
        out_shape=[jax.ShapeDtypeStruct((r, n), F32)] * 4,
    )(w, m, v, *g_parts)


def _ada_fwd(c_all, w_shard, b_shard):
    n = w_shard.shape[1]
    tn = 512

    def body(c_ref, w_ref, b_ref, o_ref):
        cv = c_ref[...]
        ca = (cv * _sigmoid(cv)).astype(MXU)
        o_ref[...] = _dot(ca, w_ref[...].astype(MXU)) + b_ref[...]

    return pl.pallas_call(
        body, name="ada_fwd", grid=(n // tn,),
        in_specs=[_fixed(8, D), pl.BlockSpec((D, tn), lambda j: (0, j)), pl.BlockSpec((1, tn), lambda j: (0, j))],
        out_specs=pl.BlockSpec((8, tn), lambda j: (0, j)),
        out_shape=jax.ShapeDtypeStruct((8, n), F32),
    )(c_all, w_shard, b_shard)


def _ada_bwd(c_all, dada_shard):
    n = dada_shard.shape[1]
    tn = 512

    def body(c_ref, d_ref, o_ref):
        cv = c_ref[...]
        ca = (cv * _sigmoid(cv)).astype(MXU)
        o_ref[...] = _dot_tn(ca, d_ref[...].astype(MXU))

    return pl.pallas_call(
        body, name="ada_bwd", grid=(n // tn,),
        in_specs=[_fixed(8, D), pl.BlockSpec((8, tn), lambda j: (0, j))],
        out_specs=pl.BlockSpec((D, tn), lambda j: (0, j)),
        out_shape=jax.ShapeDtypeStruct((D, n), F32),
    )(c_all, dada_shard)


_SMALL = [("d_ada", N_COND * D), ("ln1_g", D), ("ln1_b", D), ("ln2_g", D), ("ln2_b", D), ("b_gate", 2 * D), ("b_forget", LANE),
          ("loss", D)]
_SMALL_OFF = {}
_o = 0
for _n, _w in _SMALL:
    _SMALL_OFF[_n] = (_o, _w)
    _o += _w
_SMALL_LEN = _o
_SMALL_PARAMS = [("b_ada", "d_ada", N_COND * D), ("b_gate", "b_gate", 2 * D), ("b_forget", "b_forget", N_FGATE),
                 ("ln1_g", "ln1_g", D), ("ln1_b", "ln1_b", D), ("ln2_g", "ln2_g", D), ("ln2_b", "ln2_b", D)]


def _small_update(rows, params):
    npar = len(_SMALL_PARAMS)

    def body(*refs):
        rows_ref = refs[0]
        p_refs = refs[1:1 + 3 * npar]
        loss_ref = refs[1 + 3 * npar]
        o_refs = refs[2 + 3 * npar:]
        total = rows_ref[0]
        for d in range(1, 8):
            total = total + rows_ref[d]
        lo, lw = _SMALL_OFF["loss"]
        loss_ref[...] = jnp.sum(total[:, lo:lo + lw], axis=1, keepdims=True)
        for j, (_, key, n) in enumerate(_SMALL_PARAMS):
            off = _SMALL_OFF[key][0]
            g = total[:, off:off + n]
            w_ref, m_ref, v_ref = p_refs[3 * j:3 * j + 3]
            o_refs[4 * j][...] = g
            o_refs[4 * j + 1][...], o_refs[4 * j + 2][...], o_refs[4 * j + 3][...] = _adamw_math(w_ref[...], g, m_ref[...], v_ref[...])

    flat = [a for p in params for a in p]
    out_shape = [jax.ShapeDtypeStruct((1, 1), F32)] + [jax.ShapeDtypeStruct((1, n), F32) for _, _, n in _SMALL_PARAMS for _ in range(4)]
    return pl.pallas_call(body, name="small_update", out_shape=out_shape)(rows, *flat)


_BIG = [("w_in", "cols_t"), ("w_sb_out", "cols"), ("w_fox_out", "cols"), ("w_o", "rows"),
        ("w_ffn_gate", "cols_t"), ("w_ffn_up", "cols_t"), ("w_ffn_down", "rows")]


def _shard2d(a, how):
    return a[0].T if how == "cols_t" else a[0]


def _unshard(g, how):
    if how == "cols":
        return g.transpose(1, 0, 2).reshape(g.shape[1], 4 * g.shape[2])
    return g.reshape(4 * g.shape[1], g.shape[2])


def _reshard(w, how):
    if how == "cols":
        return w.reshape(w.shape[0], 4, w.shape[1] // 4).transpose(1, 0, 2)
    return w.reshape(4, w.shape[0] // 4, w.shape[1])


def kernel(x, c, w_ada, b_ada, w_in, b_gate, b_forget, w_sb_out, w_fox_out, w_o, ln1_g, ln1_b, w_ffn_gate, w_ffn_up, w_ffn_down, ln2_g, ln2_b, loss_target, m_w_ada, m_b_ada, m_w_in, m_b_gate, m_b_forget, m_w_sb_out, m_w_fox_out, m_w_o, m_ln1_g, m_ln1_b, m_w_ffn_gate, m_w_ffn_up, m_w_ffn_down, m_ln2_g, m_ln2_b, v_w_ada, v_b_ada, v_w_in, v_b_gate, v_b_forget, v_w_sb_out, v_w_fox_out, v_w_o, v_ln1_g, v_ln1_b, v_w_ffn_gate, v_w_ffn_up, v_w_ffn_down, v_ln2_g, v_ln2_b):
    given = dict(locals())
    mx, my, mc = _mesh_pos()
    chip = 2 * mx + my
    seq = 4 * mx + 2 * my + mc

    c_all = _allgather_rows(c, "gather_c").reshape(8, D)
    n_ada = w_ada.shape[2]
    b_ada_shard = lax.dynamic_slice(b_ada, (0, chip * n_ada), (1, n_ada))
    ada_part = _ada_fwd(c_all, w_ada[0], b_ada_shard)
    ada_all = _allgather_rows(ada_part.reshape(1, 8 * n_ada), "gather_ada").reshape(4, 2, 8, n_ada)
    ada_row = lax.dynamic_slice(ada_all, (0, mc, seq, 0), (4, 1, 1, n_ada)).reshape(1, N_COND * D)
    ada = [ada_row[:, j * D:(j + 1) * D] for j in range(N_COND)]

    w_in_g = _gather_two_level(_shard2d(w_in, "cols_t").astype(MXU), "gather_w_in")
    wi = _unshard(w_in_g, "cols_t")
    w_all = jnp.concatenate([wi[:OFF_FGATE + N_FGATE], jnp.zeros((LANE - N_FGATE, D), MXU), wi[OFF_FGATE + N_FGATE:]], axis=0)
    bf_pad = jnp.concatenate([b_forget, jnp.zeros((1, LANE - N_FGATE), F32)], axis=1)
    late = _BIG[1:]
    late_riders = [_shard2d(given[n], how).astype(MXU) for n, how in late]
    pieces = {}

    def late_full(gathered):
        return [_unshard(g, how) for (_, how), g in zip(late, gathered)]

    def early_grads(dw):
        for n, how in late:
            pieces[n] = _reshard(dw[n][0], how)
        return [_reshard(dw[n][1], how) for n, how in late]

    def w_in_grads(dwi):
        pieces["w_in"] = _reshard(jnp.concatenate(dwi[:6] + [dwi[6][:N_FGATE], dwi[7]], axis=0), "cols_t")
        return [pieces["w_in"].astype(MXU)]

    out = _local_step(x[0], loss_target[0], ada, w_all, b_gate, bf_pad, (late_riders, late_full), early_grads, w_in_grads,
                      ln1_g, ln1_b, ln2_g, ln2_b)

    row = jnp.concatenate(out["d_ada"] + [out["dln1_g"], out["dln1_b"], out["dln2_g"], out["dln2_b"], out["db_gate"],
                                          out["db_forget"], out["loss_lanes"]], axis=1)
    rows = _allgather_rows(row, "gather_small")
    small = _small_update(rows, [(given[p], given["m_" + p], given["v_" + p]) for p, _, _ in _SMALL_PARAMS])
    loss = small[0].reshape(())
    res = {}
    for j, (p, _, _) in enumerate(_SMALL_PARAMS):
        res[p] = small[1 + 4 * j:5 + 4 * j]

    dada_all = rows.reshape(8, _SMALL_LEN)[:, :N_COND * D]
    dada_shard = lax.dynamic_slice(dada_all, (0, chip * n_ada), (8, n_ada))
    g_ada = _ada_bwd(c_all, dada_shard)
    res["w_ada"] = [a[None] for a in _adamw(w_ada[0], m_w_ada[0], v_w_ada[0], [g_ada], "adamw_w_ada")]

    received = dict(zip([n for n, _ in late], out["early_received"]))
    (received["w_in"],) = out["w_in_received"]
    chip_arr = jnp.reshape(chip, (1,)).astype(jnp.int32)
    partial = [_reduce_chips(chip_arr, pieces[n], received[n], "reduce_" + n) for n, _ in _BIG]
    theirs = _sibling_exchange(partial, "swap_cores")
    for (n, how), mine, other in zip(_BIG, partial, theirs):
        upd = _adamw(_shard2d(given[n], how), _shard2d(given["m_" + n], how), _shard2d(given["v_" + n], how), [mine, other], "adamw_" + n)
        res[n] = [(a.T if how == "cols_t" else a)[None] for a in upd]

    order = ["w_ada", "b_ada", "w_in", "b_gate", "b_forget", "w_sb_out", "w_fox_out", "w_o", "ln1_g", "ln1_b",
             "w_ffn_gate", "w_ffn_up", "w_ffn_down", "ln2_g", "ln2_b"]
    return (loss, out["grad_x"][None], *[res[n][0] for n in order], *[res[n][1] for n in order],
            *[res[n][2] for n in order], *[res[n][3] for n in order])
```

```python
import functools

import jax
import jax.numpy as jnp
from jax import lax
from jax.experimental import pallas as pl
from jax.experimental.pallas import tpu as pltpu

F32 = jnp.float32
MXU = jnp.bfloat16

D = 1024
HEAD_DIM = 64
WIDTH = 512
D_FF = 2816
N_COND = 6
LN_EPS = 1e-5
ALPHA = 2.0 ** 0.25
QK_SCALE = HEAD_DIM ** -0.5
OFF_FGATE = 6 * WIDTH
N_FGATE = 8
IN_COLS = OFF_FGATE + N_FGATE + 2 * D
LANE = 128
W_ALL_COLS = OFF_FGATE + LANE + 2 * D
TQ = 512
SB_TQ = 256
ADAM_LR, ADAM_B1, ADAM_B2, ADAM_EPS, ADAM_WD, ADAM_STEP = 0.001, 0.9, 0.999, 1e-08, 0.01, 10
NEG = -1e30
DEAD_LOG = -120.0
RS_COUNT_LANE = LANE - 1
MESH_AXES = ("x", "y", "c")
VMEM_BIG = 56 * 1024 * 1024


def _dot(a, b):
    return jnp.dot(a, b, preferred_element_type=F32)


def _dot_nt(a, b):
    return lax.dot_general(a, b, (((1,), (1,)), ((), ())), preferred_element_type=F32)


def _dot_tn(a, b):
    return lax.dot_general(a, b, (((0,), (0,)), ((), ())), preferred_element_type=F32)


def _ln(x):
    mu = jnp.mean(x, axis=-1, keepdims=True)
    xc = x - mu
    var = jnp.mean(xc * xc, axis=-1, keepdims=True)
    rstd = lax.rsqrt(var + LN_EPS)
    return xc * rstd, rstd


def _ln_bwd(dxhat, xhat, rstd):
    return rstd * (dxhat - jnp.mean(dxhat, axis=-1, keepdims=True) - xhat * jnp.mean(dxhat * xhat, axis=-1, keepdims=True))


def _sigmoid(x):
    return 1.0 / (1.0 + jnp.exp(-x))


def _colsum(x):
    return jnp.sum(x, axis=0, keepdims=True)


def _split(x):
    hi = x.astype(MXU)
    lo = (x - hi.astype(F32)).astype(MXU)
    return jnp.concatenate([hi, lo], axis=1)


def _rows(tm, n):
    return pl.BlockSpec((tm, n), lambda i: (i, 0))


def _fixed(r, n):
    return pl.BlockSpec((r, n), lambda i: (0, 0))


def _res(a):
    return pl.BlockSpec(a.shape, lambda i: (0, 0), pipeline_mode=pl.Buffered(1))


def _params(limit=None, sem=None):
    return pltpu.CompilerParams(vmem_limit_bytes=limit, dimension_semantics=sem)


def _in_proj(x, sh1, sc1, w_all, b_gate):
    s = x.shape[0]
    tm = 256

    def body(x_ref, sh_ref, sc_ref, w_ref, bg_ref, u_ref, qkv_ref, fl_ref, gl_ref):
        xhat, _ = _ln(x_ref[...])
        u = (xhat * (1.0 + sc_ref[...]) + sh_ref[...]).astype(MXU)
        u_ref[...] = u
        for c0 in range(0, OFF_FGATE, WIDTH):
            p = _dot_nt(u, w_ref[c0:c0 + WIDTH, :])
            if c0 in (0, 3 * WIDTH):
                p = p * QK_SCALE
            qkv_ref[:, c0:c0 + WIDTH] = p.astype(MXU)
        fl_ref[...] = _dot_nt(u, w_ref[OFF_FGATE:OFF_FGATE + LANE, :])
        for c0 in range(0, 2 * D, D):
            gl_ref[:, c0:c0 + D] = _dot_nt(u, w_ref[OFF_FGATE + LANE + c0:OFF_FGATE + LANE + c0 + D, :]) + bg_ref[:, c0:c0 + D]

    return pl.pallas_call(
        body, name="in_proj", grid=(s // tm,),
        in_specs=[_rows(tm, D), _fixed(1, D), _fixed(1, D), _res(w_all), _fixed(1, 2 * D)],
        out_specs=[_rows(tm, D), _rows(tm, OFF_FGATE), _rows(tm, LANE), _rows(tm, 2 * D)],
        out_shape=[jax.ShapeDtypeStruct((s, D), MXU), jax.ShapeDtypeStruct((s, OFF_FGATE), MXU),
                   jax.ShapeDtypeStruct((s, LANE), F32), jax.ShapeDtypeStruct((s, 2 * D), F32)],
        compiler_params=_params(VMEM_BIG),
    )(x, sh1, sc1, w_all, b_gate)


def _log_sigmoid_parts(z):
    e = jnp.exp(-jnp.abs(z))
    return -(jnp.maximum(z, 0.0) + jnp.log(1.0 + e)), e


def _fcum_fwd(fl, bf):
    s = fl.shape[0]
    nb = s // LANE

    def body(fl_ref, bf_ref, fc_ref, fkt_ref):
        r = lax.broadcasted_iota(jnp.int32, (LANE, LANE), 0)
        c = lax.broadcasted_iota(jnp.int32, (LANE, LANE), 1)
        tri = (c <= r).astype(F32)

        def step(b, carry):
            r0 = pl.multiple_of(b * LANE, LANE)
            xb = fl_ref[pl.ds(r0, LANE), :] + bf_ref[...]
            ls = _log_sigmoid_parts(-xb)[0]
            cs = jnp.dot(tri, ls, precision=lax.Precision.HIGHEST, preferred_element_type=F32) + carry
            fc_ref[pl.ds(r0, LANE), :] = cs
            fkt_ref[b] = cs.T[:N_FGATE, :]
            return cs[LANE - 1:LANE, :]

        lax.fori_loop(0, nb, step, jnp.zeros((1, LANE), F32))

    return pl.pallas_call(
        body, name="fcum_fwd",
        out_shape=[jax.ShapeDtypeStruct((s, LANE), F32), jax.ShapeDtypeStruct((nb, N_FGATE, LANE), F32)],
    )(fl, bf)


def _attn_specs(s, col0, tq):
    return [pl.BlockSpec((tq, LANE), lambda hp, i: (i, col0 + hp)),
            pl.BlockSpec((s, LANE), lambda hp, i: (0, col0 + 4 + hp)),
            pl.BlockSpec((s, LANE), lambda hp, i: (0, col0 + 8 + hp))]


def _tile_iotas(tq):
    lane = lax.broadcasted_iota(jnp.int32, (tq, LANE), 1)
    row = lax.broadcasted_iota(jnp.int32, (tq, tq), 0)
    col = lax.broadcasted_iota(jnp.int32, (tq, tq), 1)
    return lane, row, col


def _sub_blocks(nk):
    return [slice(j * LANE, (j + 1) * LANE) for j in range(nk // LANE)]


def _over_strips(tile, tq):
    return tile(slice(0, tq), tq)


def _tri(below):
    r = lax.broadcasted_iota(jnp.int32, (LANE, LANE), 0)
    c = lax.broadcasted_iota(jnp.int32, (LANE, LANE), 1)
    t = jnp.concatenate([((r > c) if below else (r < c)).astype(MXU), jnp.ones((LANE, LANE), MXU)], axis=1)
    return jnp.concatenate([t, t], axis=0)


def _call_with_riders(body, name, grid, in_specs, out_specs, out_shape, scratch, args, riders, gather, limit=None):
    nr, n_in, n_out, n_sc = len(riders), len(in_specs), len(out_specs), len(scratch)

    def at_step(which):
        hit = None
        for d, n in enumerate(grid):
            here = pl.program_id(d) == (0 if which == "first" else n - 1)
            hit = here if hit is None else hit & here
        return hit

    def wrapped(*refs):
        ins, rin = refs[:n_in], refs[n_in:n_in + nr]
        outs, rout = refs[n_in + nr:n_in + nr + n_out], refs[n_in + nr + n_out:n_in + 2 * nr + n_out]
        own, sems = refs[n_in + 2 * nr + n_out:n_in + 2 * nr + n_out + n_sc], refs[n_in + 2 * nr + n_out + n_sc:]
        if nr:
            @pl.when(at_step("first"))
            def _():
                _exchange_start(rin, rout, sems, gather)

        body(*ins, *outs, *own)
        if nr:
            @pl.when(at_step("last"))
            def _():
                _exchange_wait(rin, rout, sems, gather)

    res = pl.pallas_call(
        wrapped, name=name, grid=grid,
        in_specs=list(in_specs) + [_ANY] * nr, out_specs=list(out_specs) + [_ANY] * nr,
        out_shape=list(out_shape) + _exchange_out_shape(riders),
        scratch_shapes=list(scratch) + (_exchange_sems(nr) if nr else []),
        compiler_params=_params(limit),
    )(*args, *riders)
    return res[:n_out], res[n_out:]


def _sb_fwd(qkv, riders=()):
    s = qkv.shape[0]
    tq = SB_TQ
    nq = s // tq
    assert nq <= RS_COUNT_LANE

    def body(q_ref, k_ref, v_ref, o_ref, rs_ref):
        i = pl.program_id(1)
        lane, row, col = _tile_iotas(tq)
        u2 = _tri(True)
        diag = col < row
        q = q_ref[...]
        qms = [jnp.where(hm, q, jnp.zeros_like(q)) for hm in (lane < HEAD_DIM, lane >= HEAD_DIM)]

        def step(kb, carry, masked):
            k0 = pl.multiple_of(kb * tq, tq)
            k = k_ref[pl.ds(k0, tq), :]
            v = v_ref[pl.ds(k0, tq), :]
            def tile(rows, nk, qm, state):
                run, acc, rt = (t[rows] for t in state)
                z = _dot_nt(qm[rows], k[:nk])
                lneg, _ = _log_sigmoid_parts(z)
                lpos = z + lneg
                if masked:
                    lneg = jnp.where(diag[rows, :nk], lneg, 0.0)
                rt = jnp.where(lane[rows] == kb, run, rt)
                a = []
                for sl in reversed(_sub_blocks(nk)):
                    st = _dot(_split(lneg[:, sl]), u2)
                    a.append(jnp.exp(lpos[:, sl] + st[:, :LANE] + run))
                    run = run + st[:, LANE:]
                a = jnp.concatenate(a[::-1], axis=1)
                if masked:
                    a = jnp.where(diag[rows, :nk], a, 0.0)
                return run, acc + _dot(a.astype(MXU), v[:nk]), rt

            return tuple(_over_strips(functools.partial(tile, qm=qm, state=state), tq) for qm, state in zip(qms, carry))

        zero = jnp.zeros((tq, LANE), F32)
        carry = step(i, ((zero, zero, zero),) * 2, True)

        def alive(cr):
            return jnp.maximum(jnp.max(cr[0][0]), jnp.max(cr[1][0])) > DEAD_LOG

        def walk(state):
            j, _, cr = state
            cr = step(i - 1 - j, cr, False)
            return j + 1, alive(cr), cr

        walked, _, carry = lax.while_loop(lambda state: (state[0] < i) & state[1], walk, (jnp.int32(0), alive(carry), carry))
        count = walked.astype(F32)
        rs_ref[0] = jnp.where(lane == RS_COUNT_LANE, count, carry[0][2])
        rs_ref[1] = jnp.where(lane == RS_COUNT_LANE, count, carry[1][2])
        o_ref[...] = jnp.where(lane < HEAD_DIM, carry[0][1], carry[1][1]).astype(o_ref.dtype)

    return _call_with_riders(
        body, "sb_fwd", (4, nq), _attn_specs(s, 0, tq),
        [pl.BlockSpec((tq, LANE), lambda hp, i: (i, hp)), pl.BlockSpec((2, tq, LANE), lambda hp, i: (hp, i, 0))],
        [jax.ShapeDtypeStruct((s, WIDTH), MXU), jax.ShapeDtypeStruct((8, s, LANE), F32)], [], (qkv, qkv, qkv), riders, True)


def _sb_bwd(qkv, do, rs, riders=()):
    s = qkv.shape[0]
    tq = SB_TQ
    nq = s // tq

    def body(q_ref, k_ref, v_ref, do_ref, rs_ref, dq_ref, dk_ref, dv_ref, dk_acc, dv_acc):
        i = pl.program_id(1)

        @pl.when(i == 0)
        def _():
            dk_acc[...] = jnp.zeros_like(dk_acc)
            dv_acc[...] = jnp.zeros_like(dv_acc)

        lane, row, col = _tile_iotas(tq)
        u2 = _tri(True)
        l2 = _tri(False)
        diag = col < row
        q = q_ref[...]
        do = do_ref[...]
        heads = [(jnp.where(hm, q, jnp.zeros_like(q)), jnp.where(hm, do, jnp.zeros_like(do)), rs_ref[hh])
                 for hh, hm in enumerate((lane < HEAD_DIM, lane >= HEAD_DIM))]

        def step(kb, carry, masked):
            k0 = pl.multiple_of(kb * tq, tq)
            k = k_ref[pl.ds(k0, tq), :]
            v = v_ref[pl.ds(k0, tq), :]
            to_keys = {}

            def tile(rows, nk, qm, dom, rblk, state):
                gpre, dq = (t[rows] for t in state)
                z = _dot_nt(qm[rows], k[:nk])
                lneg, e = _log_sigmoid_parts(z)
                lpos = z + lneg
                if masked:
                    lneg = jnp.where(diag[rows, :nk], lneg, 0.0)
                run = jnp.sum(jnp.where(lane[rows] == kb, rblk[rows], 0.0), axis=1, keepdims=True) + jnp.zeros_like(gpre)
                a = []
                for sl in reversed(_sub_blocks(nk)):
                    st = _dot(_split(lneg[:, sl]), u2)
                    a.append(jnp.exp(lpos[:, sl] + st[:, :LANE] + run))
                    run = run + st[:, LANE:]
                a = jnp.concatenate(a[::-1], axis=1)
                if masked:
                    a = jnp.where(diag[rows, :nk], a, 0.0)
                g = a * _dot_nt(dom[rows], v[:nk])
                pre = []
                for sl in _sub_blocks(nk):
                    pt = _dot(_split(g[:, sl]), l2)
                    pre.append(gpre + pt[:, :LANE])
                    gpre = gpre + pt[:, LANE:]
                sig = jnp.where(z >= 0.0, 1.0, e) / (1.0 + e)
                dz = g - (g + jnp.concatenate(pre, axis=1)) * sig
                if masked:
                    dz = jnp.where(diag[rows, :nk], dz, 0.0)
                dzb = dz.astype(MXU)
                both = to_keys.setdefault(nk, [0.0, 0.0])
                both[0] = both[0] + _dot_tn(dzb, qm[rows])
                both[1] = both[1] + _dot_tn(a.astype(MXU), dom[rows])
                return gpre, dq + _dot(dzb, k[:nk])

            new = tuple(_over_strips(functools.partial(tile, qm=qm, dom=dom, rblk=rblk, state=state), tq)
                        for (qm, dom, rblk), state in zip(heads, carry))
            for nk, (dk, dv) in to_keys.items():
                dk_acc[pl.ds(k0, nk), :] += dk
                dv_acc[pl.ds(k0, nk), :] += dv
            return new

        walked = jnp.max(jnp.where(lane[:8] == RS_COUNT_LANE, rs_ref[0, 0:8, :], 0.0))
        first = i - jnp.clip(walked.astype(jnp.int32), 0, i)
        zero = jnp.zeros((tq, LANE), F32)
        carry = step(i, lax.fori_loop(first, i, lambda kb, cr: step(kb, cr, False), ((zero, zero),) * 2), True)
        dq_ref[...] = (jnp.where(lane < HEAD_DIM, carry[0][1], carry[1][1]) * QK_SCALE).astype(dq_ref.dtype)

        @pl.when(i == nq - 1)
        def _():
            dk_ref[...] = dk_acc[...].astype(dk_ref.dtype)
            dv_ref[...] = dv_acc[...].astype(dv_ref.dtype)

    blk = pl.BlockSpec((tq, LANE), lambda hp, i: (i, hp))
    whole = pl.BlockSpec((s, LANE), lambda hp, i: (0, hp))
    return _call_with_riders(
        body, "sb_bwd", (4, nq), _attn_specs(s, 0, tq) + [blk, pl.BlockSpec((2, tq, LANE), lambda hp, i: (hp, i, 0))],
        [blk, whole, whole], [jax.ShapeDtypeStruct((s, WIDTH), MXU)] * 3,
        [pltpu.VMEM((s, LANE), F32), pltpu.VMEM((s, LANE), F32)], (qkv, qkv, qkv, do, rs), riders, False)


def _key_bias(fkt_ref, kb, h, tq):
    n_sub = tq // LANE
    return jnp.concatenate([fkt_ref[kb * n_sub + j, pl.ds(h, 1), :] for j in range(n_sub)], axis=1)


def _fox_fwd(qkv, fc, fkt, riders=()):
    s = qkv.shape[0]
    tq = TQ
    nq = s // tq
    nb = fkt.shape[0]

    def body(q_ref, k_ref, v_ref, fq_ref, fkt_ref, o_ref, lse_ref):
        hp = pl.program_id(0)
        i = pl.program_id(1)
        lane, row, col = _tile_iotas(tq)
        diag = col <= row
        q = q_ref[...]
        fqb = fq_ref[...]
        heads = []
        for hh in range(2):
            h = 2 * hp + hh
            hm = (lane >= HEAD_DIM) if hh else (lane < HEAD_DIM)
            heads.append((h, jnp.where(hm, q, jnp.zeros_like(q)), jnp.sum(jnp.where(lane == h, fqb, 0.0), axis=1, keepdims=True)))

        def step(kb, carry, masked):
            k0 = pl.multiple_of(kb * tq, tq)
            k = k_ref[pl.ds(k0, tq), :]
            v = v_ref[pl.ds(k0, tq), :]
            def tile(rows, nk, h, qm, fq, state):
                m, l, acc = (t[rows] for t in state)
                z = _dot_nt(qm[rows], k[:nk]) + fq[rows] - _key_bias(fkt_ref, kb, h, tq)[:, :nk]
                if masked:
                    z = jnp.where(diag[rows, :nk], z, NEG)
                mn = jnp.maximum(m, jnp.max(z, axis=1, keepdims=True))
                p = jnp.exp(z - mn)
                alpha = jnp.exp(m - mn)
                return mn, alpha * l + jnp.sum(p, axis=1, keepdims=True), alpha * acc + _dot(p.astype(MXU), v[:nk])

            return tuple(_over_strips(functools.partial(tile, h=h, qm=qm, fq=fq, state=state), tq)
                         for (h, qm, fq), state in zip(heads, carry))

        init = ((jnp.full((tq, 1), NEG, F32), jnp.zeros((tq, 1), F32), jnp.zeros((tq, LANE), F32)),) * 2
        carry = step(i, lax.fori_loop(0, i, lambda kb, cr: step(kb, cr, False), init), True)
        outs = []
        for hh, (m, l, acc) in enumerate(carry):
            outs.append(acc / l)
            lse_ref[hh] = jnp.broadcast_to(m + jnp.log(l), (tq, LANE))
        o_ref[...] = jnp.where(lane < HEAD_DIM, outs[0], outs[1]).astype(o_ref.dtype)

    return _call_with_riders(
        body, "fox_fwd", (4, nq),
        _attn_specs(s, 12, tq) + [pl.BlockSpec((tq, LANE), lambda hp, i: (i, 0)), pl.BlockSpec((nb, N_FGATE, LANE), lambda hp, i: (0, 0, 0))],
        [pl.BlockSpec((tq, LANE), lambda hp, i: (i, hp)), pl.BlockSpec((2, tq, LANE), lambda hp, i: (hp, i, 0))],
        [jax.ShapeDtypeStruct((s, WIDTH), MXU), jax.ShapeDtypeStruct((8, s, LANE), F32)], [], (qkv, qkv, qkv, fc, fkt), riders, True)


def _fox_bwd(qkv, fc, fkt, do, o, lse, riders=()):
    s = qkv.shape[0]
    tq = TQ
    nq = s // tq
    nb = fkt.shape[0]

    def body(q_ref, k_ref, v_ref, fq_ref, fkt_ref, do_ref, o_ref, lse_ref, dq_ref, dk_ref, dv_ref, dfk_ref, dfq_ref, dk_acc, dv_acc):
        hp = pl.program_id(0)
        i = pl.program_id(1)

        @pl.when(i == 0)
        def _():
            dk_acc[...] = jnp.zeros_like(dk_acc)
            dv_acc[...] = jnp.zeros_like(dv_acc)

        @pl.when((i == 0) & (hp == 0))
        def _():
            dfk_ref[...] = jnp.zeros_like(dfk_ref)

        lane, row, col = _tile_iotas(tq)
        diag = col <= row
        q = q_ref[...]
        do = do_ref[...]
        dof = do.astype(F32) * o_ref[...].astype(F32)
        fqb = fq_ref[...]
        heads = []
        for hh in range(2):
            h = 2 * hp + hh
            hm = (lane >= HEAD_DIM) if hh else (lane < HEAD_DIM)
            heads.append((h, jnp.where(hm, q, jnp.zeros_like(q)), jnp.where(hm, do, jnp.zeros_like(do)),
                          jnp.sum(jnp.where(hm, dof, 0.0), axis=1, keepdims=True),
                          jnp.sum(jnp.where(lane == h, fqb, 0.0), axis=1, keepdims=True), lse_ref[hh][:, :1]))

        def step(kb, carry, masked):
            k0 = pl.multiple_of(kb * tq, tq)
            k = k_ref[pl.ds(k0, tq), :]
            v = v_ref[pl.ds(k0, tq), :]
            to_keys = {}

            def tile(rows, nk, h, qm, dom, delta, fq, lse_t, state):
                dq, rsum = (t[rows] for t in state)
                z = _dot_nt(qm[rows], k[:nk]) + fq[rows] - _key_bias(fkt_ref, kb, h, tq)[:, :nk]
                if masked:
                    z = jnp.where(diag[rows, :nk], z, NEG)
                p = jnp.exp(z - lse_t[rows])
                ds = p * (_dot_nt(dom[rows], v[:nk]) - delta[rows])
                dsb = ds.astype(MXU)
                both = to_keys.setdefault(nk, [0.0, 0.0])
                both[0] = both[0] + _dot_tn(dsb, qm[rows])
                both[1] = both[1] + _dot_tn(p.astype(MXU), dom[rows])
                csum = _colsum(ds)
                for j, sl in enumerate(_sub_blocks(nk)):
                    dfk_ref[kb * (tq // LANE) + j, pl.ds(h, 1), :] += -csum[:, sl]
                return dq + _dot(dsb, k[:nk]), rsum + jnp.sum(ds, axis=1, keepdims=True)

            new = tuple(_over_strips(functools.partial(tile, h=h, qm=qm, dom=dom, delta=delta, fq=fq, lse_t=lse_t, state=state), tq)
                        for (h, qm, dom, delta, fq, lse_t), state in zip(heads, carry))
            for nk, (dk, dv) in to_keys.items():
                dk_acc[pl.ds(k0, nk), :] += dk
                dv_acc[pl.ds(k0, nk), :] += dv
            return new

        init = ((jnp.zeros((tq, LANE), F32), jnp.zeros((tq, 1), F32)),) * 2
        carry = step(i, lax.fori_loop(0, i, lambda kb, cr: step(kb, cr, False), init), True)
        dq_ref[...] = (jnp.where(lane < HEAD_DIM, carry[0][0], carry[1][0]) * QK_SCALE).astype(dq_ref.dtype)
        dfq_ref[0] = jnp.where(lane == heads[0][0], carry[0][1], jnp.where(lane == heads[1][0], carry[1][1], 0.0))

        @pl.when(i == nq - 1)
        def _():
            dk_ref[...] = dk_acc[...].astype(dk_ref.dtype)
            dv_ref[...] = dv_acc[...].astype(dv_ref.dtype)

    blk = pl.BlockSpec((tq, LANE), lambda hp, i: (i, hp))
    whole = pl.BlockSpec((s, LANE), lambda hp, i: (0, hp))
    pair = pl.BlockSpec((2, tq, LANE), lambda hp, i: (hp, i, 0))
    fkt_spec = pl.BlockSpec((nb, N_FGATE, LANE), lambda hp, i: (0, 0, 0))
    return _call_with_riders(
        body, "fox_bwd", (4, nq),
        _attn_specs(s, 12, tq) + [pl.BlockSpec((tq, LANE), lambda hp, i: (i, 0)), fkt_spec, blk, blk, pair],
        [blk, whole, whole, fkt_spec, pl.BlockSpec((1, tq, LANE), lambda hp, i: (hp, i, 0))],
        [jax.ShapeDtypeStruct((s, WIDTH), MXU)] * 3
        + [jax.ShapeDtypeStruct((nb, N_FGATE, LANE), F32), jax.ShapeDtypeStruct((4, s, LANE), F32)],
        [pltpu.VMEM((s, LANE), F32), pltpu.VMEM((s, LANE), F32)], (qkv, qkv, qkv, fc, fkt, do, o, lse), riders, False)


def _fcum_bwd(dfkt, dfq, fl, bf):
    s = fl.shape[0]
    nb = s // LANE

    def body(dfkt_ref, dfq_ref, fl_ref, bf_ref, df_ref, dbf_ref, tail_ref):
        @pl.when(pl.program_id(0) == 0)
        def _():
            tail_ref[...] = jnp.zeros_like(tail_ref)
            dbf_ref[...] = jnp.zeros_like(dbf_ref)

        r = lax.broadcasted_iota(jnp.int32, (LANE, LANE), 0)
        c = lax.broadcasted_iota(jnp.int32, (LANE, LANE), 1)
        tri = (c >= r).astype(F32)
        dfc = jnp.concatenate([dfkt_ref[0], jnp.zeros((LANE - N_FGATE, LANE), F32)], axis=0).T
        dfc = dfc + ((dfq_ref[0] + dfq_ref[1]) + (dfq_ref[2] + dfq_ref[3]))
        dls = jnp.dot(tri, dfc, precision=lax.Precision.HIGHEST, preferred_element_type=F32) + tail_ref[...]
        xb = fl_ref[...] + bf_ref[...]
        e = jnp.exp(-jnp.abs(xb))
        dfl = dls * (jnp.where(xb >= 0.0, e, 1.0) / (1.0 + e))
        df_ref[...] = dfl.astype(df_ref.dtype)
        tail_ref[...] = dls[0:1, :]
        dbf_ref[...] += _colsum(dfl)

    return pl.pallas_call(
        body, name="fcum_bwd", grid=(nb,),
        in_specs=[pl.BlockSpec((1, N_FGATE, LANE), lambda j: (nb - 1 - j, 0, 0)), pl.BlockSpec((4, LANE, LANE), lambda j: (0, nb - 1 - j, 0)),
                  pl.BlockSpec((LANE, LANE), lambda j: (nb - 1 - j, 0)), _fixed(1, LANE)],
        out_specs=[pl.BlockSpec((LANE, LANE), lambda j: (nb - 1 - j, 0)), _fixed(1, LANE)],
        out_shape=[jax.ShapeDtypeStruct((s, LANE), MXU), jax.ShapeDtypeStruct((1, LANE), F32)],
        scratch_shapes=[pltpu.VMEM((1, LANE), F32)],
    )(dfkt, dfq, fl, bf)


def _mix_fwd(x, o_sb, o_fx, gl, w_sb, w_fx, w_o, g1, ln1_g, ln1_b, sh2, sc2):
    s = x.shape[0]
    tm = 256

    def body(x_ref, osb_ref, ofx_ref, gl_ref, wsb_ref, wfx_ref, wo_ref, g1_ref, lg_ref, lb_ref, sh_ref, sc_ref, r1_ref, u2_ref):
        mixin = (_sigmoid(gl_ref[:, :D]) * _dot(osb_ref[...], wsb_ref[...])
                 + _sigmoid(gl_ref[:, D:]) * _dot(ofx_ref[...], wfx_ref[...]))
        r1 = ALPHA * x_ref[...] + g1_ref[...] * _dot(mixin.astype(MXU), wo_ref[...])
        r1_ref[...] = r1
        x1 = _ln(r1)[0] * lg_ref[...] + lb_ref[...]
        u2_ref[...] = (_ln(x1)[0] * (1.0 + sc_ref[...]) + sh_ref[...]).astype(MXU)

    vec = _fixed(1, D)
    return pl.pallas_call(
        body, name="mix_fwd", grid=(s // tm,),
        in_specs=[_rows(tm, D), _rows(tm, WIDTH), _rows(tm, WIDTH), _rows(tm, 2 * D), _res(w_sb), _res(w_fx), _res(w_o),
                  vec, vec, vec, vec, vec],
        out_specs=[_rows(tm, D), _rows(tm, D)],
        out_shape=[jax.ShapeDtypeStruct((s, D), F32), jax.ShapeDtypeStruct((s, D), MXU)],
        compiler_params=_params(VMEM_BIG),
    )(x, o_sb, o_fx, gl, w_sb, w_fx, w_o, g1, ln1_g, ln1_b, sh2, sc2)


def _ffn_fwd(r1, u2, tgt, w_g, w_u, w_d, g2, ln1_g, ln1_b, ln2_g, ln2_b):
    s = r1.shape[0]
    tm = 256

    def body(r1_ref, u2_ref, t_ref, wg_ref, wu_ref, wd_ref, g2_ref, l1g_ref, l1b_ref, l2g_ref, l2b_ref,
             hg_ref, hu_ref, dxa_ref, dh_ref, acc_ref):
        @pl.when(pl.program_id(0) == 0)
        def _():
            acc_ref[...] = jnp.zeros_like(acc_ref)

        u2 = u2_ref[...]
        hg = _dot_nt(u2, wg_ref[...])
        hu = _dot_nt(u2, wu_ref[...])
        hg_ref[...] = hg
        hu_ref[...] = hu
        h = _dot((hg * _sigmoid(hg) * hu).astype(MXU), wd_ref[...])
        x1 = _ln(r1_ref[...])[0] * l1g_ref[...] + l1b_ref[...]
        xh2, rstd2 = _ln(ALPHA * x1 + g2_ref[...] * h)
        err = xh2 * l2g_ref[...] + l2b_ref[...] - t_ref[...]
        dy = err * (1.0 / D)
        dr2 = _ln_bwd(dy * l2g_ref[...], xh2, rstd2)
        dxa_ref[...] = ALPHA * dr2
        dh_ref[...] = (g2_ref[...] * dr2).astype(MXU)
        acc_ref[0:1, :] += _colsum(dr2 * h)
        acc_ref[1:2, :] += _colsum(dy * xh2)
        acc_ref[2:3, :] += _colsum(dy)
        acc_ref[3:4, :] += _colsum(err * err) * (0.5 / D)

    vec = _fixed(1, D)
    return pl.pallas_call(
        body, name="ffn_fwd", grid=(s // tm,),
        in_specs=[_rows(tm, D), _rows(tm, D), _rows(tm, D), _res(w_g), _res(w_u), _res(w_d), vec, vec, vec, vec, vec],
        out_specs=[_rows(tm, D_FF), _rows(tm, D_FF), _rows(tm, D), _rows(tm, D), _fixed(8, D)],
        out_shape=[jax.ShapeDtypeStruct((s, D_FF), F32), jax.ShapeDtypeStruct((s, D_FF), F32),
                   jax.ShapeDtypeStruct((s, D), F32), jax.ShapeDtypeStruct((s, D), MXU), jax.ShapeDtypeStruct((8, D), F32)],
        compiler_params=_params(VMEM_BIG),
    )(r1, u2, tgt, w_g, w_u, w_d, g2, ln1_g, ln1_b, ln2_g, ln2_b)


def _ffn_bwd(dh, hg, hu, w_g, w_u, w_d):
    s = dh.shape[0]
    tm = 256
    half = D_FF // 2

    def body(dh_ref, hg_ref, hu_ref, wg_ref, wu_ref, wd_ref, act_ref, dhg_ref, dhu_ref, du2_ref):
        dh = dh_ref[...]
        du2 = jnp.zeros((tm, D), F32)
        for c0 in (0, half):
            cols = slice(c0, c0 + half)
            dact = _dot_nt(dh, wd_ref[cols, :])
            hg = hg_ref[:, cols]
            hu = hu_ref[:, cols]
            sg = _sigmoid(hg)
            sl = hg * sg
            act_ref[:, cols] = (sl * hu).astype(MXU)
            dhg = (dact * hu * (sg * (1.0 + hg * (1.0 - sg)))).astype(MXU)
            dhu = (dact * sl).astype(MXU)
            dhg_ref[:, cols] = dhg
            dhu_ref[:, cols] = dhu
            du2 = du2 + _dot(dhg, wg_ref[cols, :]) + _dot(dhu, wu_ref[cols, :])
        du2_ref[...] = du2

    return pl.pallas_call(
        body, name="ffn_bwd", grid=(s // tm,),
        in_specs=[_rows(tm, D), _rows(tm, D_FF), _rows(tm, D_FF), _res(w_g), _res(w_u), _res(w_d)],
        out_specs=[_rows(tm, D_FF), _rows(tm, D_FF), _rows(tm, D_FF), _rows(tm, D)],
        out_shape=[jax.ShapeDtypeStruct((s, D_FF), MXU)] * 3 + [jax.ShapeDtypeStruct((s, D), F32)],
        compiler_params=_params(VMEM_BIG),
    )(dh, hg, hu, w_g, w_u, w_d)


def _mix_bwd(du2, dxa, r1, o_sb, o_fx, gl, w_sb, w_fx, w_o, g1, ln1_g, ln1_b, sc2):
    s = r1.shape[0]
    tm = 256
    n_tiles = s // tm

    def body(du2_ref, dxa_ref, r1_ref, osb_ref, ofx_ref, gl_ref, wsb_ref, wfx_ref, wo_ref, g1_ref, lg_ref, lb_ref, sc_ref,
             dx_ref, dosb_ref, dofx_ref, dgl_ref, dbg_ref, acc_ref, dwsb_ref, dwfx_ref, dwo_ref, nsb_ref, nfx_ref, no_ref):
        @pl.when(pl.program_id(0) == 0)
        def _():
            for ref in (acc_ref, dbg_ref, dwsb_ref, dwfx_ref, dwo_ref):
                ref[...] = jnp.zeros_like(ref)

        du2 = du2_ref[...]
        xh1, rstd1 = _ln(r1_ref[...])
        x1 = xh1 * lg_ref[...] + lb_ref[...]
        n1, rstdn = _ln(x1)
        dx1 = dxa_ref[...] + _ln_bwd(du2 * (1.0 + sc_ref[...]), n1, rstdn)
        dr1 = _ln_bwd(dx1 * lg_ref[...], xh1, rstd1)
        dx_ref[...] = ALPHA * dr1
        ysb = _dot(osb_ref[...], wsb_ref[...])
        yfx = _dot(ofx_ref[...], wfx_ref[...])
        gs = _sigmoid(gl_ref[:, :D])
        gf = _sigmoid(gl_ref[:, D:])
        mixin = (gs * ysb + gf * yfx).astype(MXU)
        mix = _dot(mixin, wo_ref[...])
        dmix = (g1_ref[...] * dr1).astype(MXU)
        dmixin = _dot_nt(dmix, wo_ref[...])
        dysb = (dmixin * gs).astype(MXU)
        dyfx = (dmixin * gf).astype(MXU)
        dwo_ref[...] += _dot_tn(mixin, dmix)
        dwsb_ref[...] += _dot_tn(osb_ref[...], dysb)
        dwfx_ref[...] += _dot_tn(ofx_ref[...], dyfx)
        dosb_ref[...] = _dot_nt(dysb, wsb_ref[...]).astype(MXU)
        dofx_ref[...] = _dot_nt(dyfx, wfx_ref[...]).astype(MXU)
        dgs = dmixin * ysb * gs * (1.0 - gs)
        dgf = dmixin * yfx * gf * (1.0 - gf)
        dgl_ref[:, :D] = dgs.astype(MXU)
        dgl_ref[:, D:] = dgf.astype(MXU)
        dbg_ref[:, :D] += _colsum(dgs)
        dbg_ref[:, D:] += _colsum(dgf)
        acc_ref[0:1, :] += _colsum(du2)
        acc_ref[1:2, :] += _colsum(du2 * n1)
        acc_ref[2:3, :] += _colsum(dx1 * xh1)
        acc_ref[3:4, :] += _colsum(dx1)
        acc_ref[4:5, :] += _colsum(dr1 * mix)

        @pl.when(pl.program_id(0) == n_tiles - 1)
        def _():
            nsb_ref[...] = dwsb_ref[...].astype(MXU)
            nfx_ref[...] = dwfx_ref[...].astype(MXU)
            no_ref[...] = dwo_ref[...].astype(MXU)

    vec = _fixed(1, D)
    dw_specs = [_fixed(WIDTH, D), _fixed(WIDTH, D), _fixed(D, D)]
    dw_shapes = [(WIDTH, D), (WIDTH, D), (D, D)]
    return pl.pallas_call(
        body, name="mix_bwd", grid=(n_tiles,),
        in_specs=[_rows(tm, D), _rows(tm, D), _rows(tm, D), _rows(tm, WIDTH), _rows(tm, WIDTH), _rows(tm, 2 * D),
                  _res(w_sb), _res(w_fx), _res(w_o), vec, vec, vec, vec],
        out_specs=[_rows(tm, D), _rows(tm, WIDTH), _rows(tm, WIDTH), _rows(tm, 2 * D), _fixed(1, 2 * D), _fixed(8, D)] + dw_specs * 2,
        out_shape=[jax.ShapeDtypeStruct((s, D), F32)] + [jax.ShapeDtypeStruct((s, WIDTH), MXU)] * 2
        + [jax.ShapeDtypeStruct((s, 2 * D), MXU), jax.ShapeDtypeStruct((1, 2 * D), F32), jax.ShapeDtypeStruct((8, D), F32)]
        + [jax.ShapeDtypeStruct(sh, F32) for sh in dw_shapes] + [jax.ShapeDtypeStruct(sh, MXU) for sh in dw_shapes],
        compiler_params=_params(VMEM_BIG),
    )(du2, dxa, r1, o_sb, o_fx, gl, w_sb, w_fx, w_o, g1, ln1_g, ln1_b, sc2)


def _in_bwd(pieces, x, dxa, w_all, sc1, riders=()):
    s = x.shape[0]
    tm = 256
    n_p = len(pieces)

    def body(*refs):
        p_refs = refs[:n_p]
        x_ref, dxa_ref, w_ref, sc_ref, gx_ref, acc_ref = refs[n_p:]

        @pl.when(pl.program_id(0) == 0)
        def _():
            acc_ref[...] = jnp.zeros_like(acc_ref)

        du1 = jnp.zeros((tm, D), F32)
        for p_ref, (arr, c0) in zip(p_refs, pieces):
            du1 = du1 + _dot(p_ref[...], w_ref[c0:c0 + arr.shape[1], :])
        n0, rstd0 = _ln(x_ref[...])
        gx_ref[...] = dxa_ref[...] + _ln_bwd(du1 * (1.0 + sc_ref[...]), n0, rstd0)
        acc_ref[0:1, :] += _colsum(du1)
        acc_ref[1:2, :] += _colsum(du1 * n0)

    return _call_with_riders(
        body, "in_bwd", (s // tm,),
        [_rows(tm, a.shape[1]) for a, _ in pieces] + [_rows(tm, D), _rows(tm, D), _res(w_all), _fixed(1, D)],
        [_rows(tm, D), _fixed(8, D)], [jax.ShapeDtypeStruct((s, D), F32), jax.ShapeDtypeStruct((8, D), F32)], [],
        (*[a for a, _ in pieces], x, dxa, w_all, sc1), riders, False, VMEM_BIG)


def _matmul_tn(a, b, name, narrow=False):
    s, m = a.shape
    n = b.shape[1]
    tm = 512 if m % 512 == 0 else (m if m < 512 else m // 2)
    tn = n // 2 if n > 2048 else n
    ts = 2048
    assert m % tm == 0 and tm % LANE == 0 and n % tn == 0 and tn % LANE == 0 and s % ts == 0

    def body(a_ref, b_ref, o_ref, *narrow_ref):
        @pl.when(pl.program_id(2) == 0)
        def _():
            o_ref[...] = jnp.zeros_like(o_ref)

        o_ref[...] += _dot_tn(a_ref[...], b_ref[...])
        if narrow:
            @pl.when(pl.program_id(2) == s // ts - 1)
            def _():
                narrow_ref[0][...] = o_ref[...].astype(MXU)

    out_blk = pl.BlockSpec((tm, tn), lambda i, j, k: (i, j))
    res = pl.pallas_call(
        body, name=name, grid=(m // tm, n // tn, s // ts),
        in_specs=[pl.BlockSpec((ts, tm), lambda i, j, k: (k, i)), pl.BlockSpec((ts, tn), lambda i, j, k: (k, j))],
        out_specs=[out_blk] * (2 if narrow else 1),
        out_shape=[jax.ShapeDtypeStruct((m, n), F32)] + ([jax.ShapeDtypeStruct((m, n), MXU)] if narrow else []),
        compiler_params=_params(VMEM_BIG),
    )(a, b)
    return tuple(res) if narrow else res[0]


def _local_step(x, tgt, ada, w_all, b_gate, bf_pad, late_weights, early_grads, w_in_grads, ln1_g, ln1_b, ln2_g, ln2_b):
    sh1, sc1, g1, sh2, sc2, g2 = ada
    u1, qkv, fl, gl = _in_proj(x, sh1, sc1, w_all, b_gate)
    fc, fkt = _fcum_fwd(fl, bf_pad)
    late_riders, late_full = late_weights
    n_sb = 3
    (o_sb, rs), gathered_a = _sb_fwd(qkv, late_riders[:n_sb])
    (o_fx, lse), gathered_b = _fox_fwd(qkv, fc, fkt, late_riders[n_sb:])
    w_sb, w_fx, w_o, w_g, w_u, w_d = late_full(list(gathered_a) + list(gathered_b))
    r1, u2 = _mix_fwd(x, o_sb, o_fx, gl, w_sb, w_fx, w_o, g1, ln1_g, ln1_b, sh2, sc2)
    hg, hu, dxa2, dh, acc_f = _ffn_fwd(r1, u2, tgt, w_g, w_u, w_d, g2, ln1_g, ln1_b, ln2_g, ln2_b)
    act, dhg, dhu, du2 = _ffn_bwd(dh, hg, hu, w_g, w_u, w_d)
    dxa1, dosb, dofx, dgl, dbg, acc_m, dw_sb, dw_fx, dw_o, n_sb_out, n_fx_out, n_o = _mix_bwd(
        du2, dxa2, r1, o_sb, o_fx, gl, w_sb, w_fx, w_o, g1, ln1_g, ln1_b, sc2)
    early = dict(w_sb_out=(dw_sb, n_sb_out), w_fox_out=(dw_fx, n_fx_out), w_o=(dw_o, n_o),
                 w_ffn_gate=_matmul_tn(dhg, u2, "dw_ffn_gate", True),
                 w_ffn_up=_matmul_tn(dhu, u2, "dw_ffn_up", True), w_ffn_down=_matmul_tn(act, dh, "dw_ffn_down", True))
    early_riders = early_grads(early)
    (dq_sb, dk_sb, dv_sb), received_a = _sb_bwd(qkv, dosb, rs, early_riders[:n_sb])
    (dq_fx, dk_fx, dv_fx, dfkt, dfq), received_b = _fox_bwd(qkv, fc, fkt, dofx, o_fx, lse, early_riders[n_sb:])
    early_received = list(received_a) + list(received_b)
    df, dbf = _fcum_bwd(dfkt, dfq, fl, bf_pad)
    pieces = [(dq_sb, 0), (dk_sb, WIDTH), (dv_sb, 2 * WIDTH), (dq_fx, 3 * WIDTH), (dk_fx, 4 * WIDTH), (dv_fx, 5 * WIDTH),
              (df, OFF_FGATE), (dgl, OFF_FGATE + LANE)]
    dw_in = [_matmul_tn(p, u1, f"dw_in_{j}") for j, (p, _) in enumerate(pieces)]
    (grad_x, acc_i), w_in_received = _in_bwd(pieces, x, dxa1, w_all, sc1, w_in_grads(dw_in))
    return dict(
        loss_lanes=acc_f[3:4], grad_x=grad_x, dw_in=dw_in, early=early, early_received=early_received,
        w_in_received=w_in_received,
        d_ada=[acc_i[0:1], acc_i[1:2], acc_m[4:5], acc_m[0:1], acc_m[1:2], acc_f[0:1]],
        dln1_g=acc_m[2:3], dln1_b=acc_m[3:4], dln2_g=acc_f[1:2], dln2_b=acc_f[2:3], db_gate=dbg, db_forget=dbf)


_MESH_ID = pl.DeviceIdType.MESH
_ANY = pl.BlockSpec(memory_space=pl.ANY)
_VMEM = pl.BlockSpec(memory_space=pltpu.VMEM)


def _mesh_pos():
    return lax.axis_index("x"), lax.axis_index("y"), lax.axis_index("c")


def _other_chips(x, y):
    return [(1 - x, y), (x, 1 - y), (1 - x, 1 - y)]


def _allgather_rows(v, name):
    n = v.shape[1]

    def body(v_ref, out_ref, send_sems, recv_sems, local_sem):
        x, y, c = _mesh_pos()
        me = 4 * x + 2 * y + c
        mine = pltpu.make_async_copy(v_ref, out_ref.at[me], local_sem)
        mine.start()
        copies = []
        for d in range(1, 8):
            fx, fy, fc = (d >> 2) & 1, (d >> 1) & 1, d & 1
            to = (1 - x if fx else x, 1 - y if fy else y, 1 - c if fc else c)
            cp = pltpu.make_async_remote_copy(src_ref=v_ref, dst_ref=out_ref.at[me], send_sem=send_sems.at[d - 1],
                                              recv_sem=recv_sems.at[d - 1], device_id=to, device_id_type=_MESH_ID)
            cp.start()
            copies.append(cp)
        for cp in copies:
            cp.wait_recv()
        for cp in copies:
            cp.wait_send()
        mine.wait()

    return pl.pallas_call(
        body, name=name, in_specs=[_VMEM], out_specs=_VMEM,
        out_shape=jax.ShapeDtypeStruct((8, 1, n), v.dtype),
        scratch_shapes=[pltpu.SemaphoreType.DMA((7,)), pltpu.SemaphoreType.DMA((7,)), pltpu.SemaphoreType.DMA(())],
    )(v)


def _chip_exchange(arrays, name, gather):
    nt = len(arrays)

    def body(*refs):
        ins, outs = refs[:nt], refs[nt:2 * nt]
        _exchange_start(ins, outs, refs[2 * nt:], gather)
        _exchange_wait(ins, outs, refs[2 * nt:], gather)

    return pl.pallas_call(
        body, name=name, in_specs=[_ANY] * nt, out_specs=[_ANY] * nt, out_shape=_exchange_out_shape(arrays),
        scratch_shapes=_exchange_sems(nt),
    )(*arrays)


def _exchange_out_shape(arrays):
    return [jax.ShapeDtypeStruct((4,) + a.shape[-2:], a.dtype) for a in arrays]


def _exchange_sems(nt):
    return [pltpu.SemaphoreType.DMA((3 * nt,)), pltpu.SemaphoreType.DMA((3 * nt,)), pltpu.SemaphoreType.DMA((nt,))]


def _exchange_copies(ins, outs, sems, gather):
    send_sems, recv_sems, local_sems = sems
    x, y, c = _mesh_pos()
    me = 2 * x + y
    local, remote = [], []
    for t in range(len(ins)):
        local.append(pltpu.make_async_copy(ins[t] if gather else ins[t].at[me], outs[t].at[me], local_sems.at[t]))
        for j, (px, py) in enumerate(_other_chips(x, y)):
            remote.append(pltpu.make_async_remote_copy(
                src_ref=ins[t] if gather else ins[t].at[2 * px + py], dst_ref=outs[t].at[me], send_sem=send_sems.at[3 * t + j],
                recv_sem=recv_sems.at[3 * t + j], device_id=(px, py, c), device_id_type=_MESH_ID))
    return local, remote


def _exchange_start(ins, outs, sems, gather):
    local, remote = _exchange_copies(ins, outs, sems, gather)
    for cp in local + remote:
        cp.start()


def _exchange_wait(ins, outs, sems, gather):
    local, remote = _exchange_copies(ins, outs, sems, gather)
    for cp in remote:
        cp.wait_recv()
    for cp in remote:
        cp.wait_send()
    for cp in local:
        cp.wait()


def _gather_two_level(shard, name):
    r, n = shard.shape
    half = n // 2
    assert half % LANE == 0

    def body(in_ref, out_ref, ici_send, ici_recv, d2d_send, d2d_recv, local_sem):
        x, y, c = _mesh_pos()
        me = 2 * x + y
        mine = pl.ds(pl.multiple_of(c * half, LANE), half)
        theirs = pl.ds(pl.multiple_of((1 - c) * half, LANE), half)
        local = pltpu.make_async_copy(in_ref, out_ref.at[me], local_sem)
        local.start()
        chips = _other_chips(x, y)
        over_ici = [pltpu.make_async_remote_copy(
            src_ref=in_ref.at[:, mine], dst_ref=out_ref.at[me, :, mine], send_sem=ici_send.at[j], recv_sem=ici_recv.at[j],
            device_id=(px, py, c), device_id_type=_MESH_ID) for j, (px, py) in enumerate(chips)]
        for cp in over_ici:
            cp.start()
        passed_on = [pltpu.make_async_remote_copy(
            src_ref=out_ref.at[2 * px + py, :, mine], dst_ref=out_ref.at[2 * px + py, :, mine], send_sem=d2d_send.at[j],
            recv_sem=d2d_recv.at[j], device_id=(x, y, 1 - c), device_id_type=_MESH_ID) for j, (px, py) in enumerate(chips)]
        for j, (px, py) in enumerate(chips):
            pltpu.make_async_remote_copy(
                src_ref=in_ref.at[:, mine], dst_ref=out_ref.at[2 * px + py, :, mine], send_sem=ici_send.at[j],
                recv_sem=ici_recv.at[j], device_id=(px, py, c), device_id_type=_MESH_ID).wait_recv()
            passed_on[j].start()
        for j, (px, py) in enumerate(chips):
            pltpu.make_async_remote_copy(
                src_ref=out_ref.at[2 * px + py, :, theirs], dst_ref=out_ref.at[2 * px + py, :, theirs], send_sem=d2d_send.at[j],
                recv_sem=d2d_recv.at[j], device_id=(x, y, 1 - c), device_id_type=_MESH_ID).wait_recv()
        for cp in over_ici + passed_on:
            cp.wait_send()
        local.wait()

    sems = pltpu.SemaphoreType.DMA((3,))
    return pl.pallas_call(
        body, name=name, in_specs=[_ANY], out_specs=_ANY, out_shape=jax.ShapeDtypeStruct((4, r, n), shard.dtype),
        scratch_shapes=[sems, sems, sems, sems, pltpu.SemaphoreType.DMA(())],
    )(shard)


def _sibling_exchange(arrays, name):
    nt = len(arrays)

    def body(*refs):
        ins, outs = refs[:nt], refs[nt:2 * nt]
        send_sems, recv_sems = refs[2 * nt:]
        x, y, c = _mesh_pos()
        copies = []
        for t in range(nt):
            cp = pltpu.make_async_remote_copy(src_ref=ins[t], dst_ref=outs[t], send_sem=send_sems.at[t], recv_sem=recv_sems.at[t],
                                              device_id=(x, y, 1 - c), device_id_type=_MESH_ID)
            cp.start()
            copies.append(cp)
        for cp in copies:
            cp.wait_recv()
        for cp in copies:
            cp.wait_send()

    return pl.pallas_call(
        body, name=name, in_specs=[_ANY] * nt, out_specs=[_ANY] * nt,
        out_shape=[jax.ShapeDtypeStruct(a.shape, a.dtype) for a in arrays],
        scratch_shapes=[pltpu.SemaphoreType.DMA((nt,)), pltpu.SemaphoreType.DMA((nt,))],
    )(*arrays)


def _tiles(r, n):
    for tr in (256, 352, 128):
        if r % tr == 0:
            return tr, n, r // tr, lambda i: (i, 0)
    assert n % 256 == 0
    return r, 256, n // 256, lambda i: (0, i)


def _reduce_chips(chip, pieces, recv, name):
    _, r, n = pieces.shape
    tr, tn, steps, at = _tiles(r, n)

    def body(chip_ref, own_ref, recv_ref, out_ref):
        me = chip_ref[0]
        total = jnp.zeros((tr, tn), F32)
        for k in range(4):
            total = total + jnp.where(me == k, own_ref[0], recv_ref[k].astype(F32))
        out_ref[...] = total

    return pl.pallas_call(
        body, name=name,
        grid_spec=pltpu.PrefetchScalarGridSpec(
            num_scalar_prefetch=1, grid=(steps,),
            in_specs=[pl.BlockSpec((1, tr, tn), lambda i, chip_ref: (chip_ref[0],) + at(i)),
                      pl.BlockSpec((4, tr, tn), lambda i, chip_ref: (0,) + at(i))],
            out_specs=pl.BlockSpec((tr, tn), lambda i, chip_ref: at(i))),
        out_shape=jax.ShapeDtypeStruct((r, n), F32),
    )(chip, pieces, recv)


def _adamw_math(w, g, m, v):
    m = ADAM_B1 * m + (1.0 - ADAM_B1) * g
    v = ADAM_B2 * v + (1.0 - ADAM_B2) * (g * g)
    m_hat = m / (1.0 - ADAM_B1 ** ADAM_STEP)
    v_hat = v / (1.0 - ADAM_B2 ** ADAM_STEP)
    return -ADAM_LR * (m_hat / (jnp.sqrt(v_hat) + ADAM_EPS) + ADAM_WD * w), m, v


def _adamw(w, m, v, g_parts, name):
    r, n = w.shape
    tr, tn, steps, at = _tiles(r, n)
    blk = pl.BlockSpec((tr, tn), at)
    ng = len(g_parts)

    def body(*refs):
        w_ref, m_ref, v_ref = refs[:3]
        g_refs = refs[3:3 + ng]
        g_out, d_out, m_out, v_out = refs[3 + ng:]
        g = g_refs[0][...]
        for gr in g_refs[1:]:
            g = g + gr[...]
        g_out[...] = g
        d_out[...], m_out[...], v_out[...] = _adamw_math(w_ref[...], g, m_ref[...], v_ref[...])

    return pl.pallas_call(
        body, name=name, grid=(steps,),
        in_specs=[blk] * (3 + ng), out_specs=[blk] * 4,
        out_shape=[jax.ShapeDtypeStruct((r, n), F32)] * 4,
    )(w, m, v, *g_parts)


def _ada_fwd(c_all, w_shard, b_shard):
    n = w_shard.shape[1]
    tn = 512

    def body(c_ref, w_ref, b_ref, o_ref):
        cv = c_ref[...]
        ca = (cv * _sigmoid(cv)).astype(MXU)
        o_ref[...] = _dot(ca, w_ref[...].astype(MXU)) + b_ref[...]

    return pl.pallas_call(
        body, name="ada_fwd", grid=(n // tn,),
        in_specs=[_fixed(8, D), pl.BlockSpec((D, tn), lambda j: (0, j)), pl.BlockSpec((1, tn), lambda j: (0, j))],
        out_specs=pl.BlockSpec((8, tn), lambda j: (0, j)),
        out_shape=jax.ShapeDtypeStruct((8, n), F32),
    )(c_all, w_shard, b_shard)


def _ada_bwd(c_all, dada_shard):
    n = dada_shard.shape[1]
    tn = 512

    def body(c_ref, d_ref, o_ref):
        cv = c_ref[...]
        ca = (cv * _sigmoid(cv)).astype(MXU)
        o_ref[...] = _dot_tn(ca, d_ref[...].astype(MXU))

    return pl.pallas_call(
        body, name="ada_bwd", grid=(n // tn,),
        in_specs=[_fixed(8, D), pl.BlockSpec((8, tn), lambda j: (0, j))],
        out_specs=pl.BlockSpec((D, tn), lambda j: (0, j)),
        out_shape=jax.ShapeDtypeStruct((D, n), F32),
    )(c_all, dada_shard)


_SMALL = [("d_ada", N_COND * D), ("ln1_g", D), ("ln1_b", D), ("ln2_g", D), ("ln2_b", D), ("b_gate", 2 * D), ("b_forget", LANE),
          ("loss", D)]
_SMALL_OFF = {}
_o = 0
for _n, _w in _SMALL:
    _SMALL_OFF[_n] = (_o, _w)
    _o += _w
_SMALL_LEN = _o
_SMALL_PARAMS = [("b_ada", "d_ada", N_COND * D), ("b_gate", "b_gate", 2 * D), ("b_forget", "b_forget", N_FGATE),
                 ("ln1_g", "ln1_g", D), ("ln1_b", "ln1_b", D), ("ln2_g", "ln2_g", D), ("ln2_b", "ln2_b", D)]


def _small_update(rows, params):
    npar = len(_SMALL_PARAMS)

    def body(*refs):
        rows_ref = refs[0]
        p_refs = refs[1:1 + 3 * npar]
        loss_ref = refs[1 + 3 * npar]
        o_refs = refs[2 + 3 * npar:]
        total = rows_ref[0]
        for d in range(1, 8):
            total = total + rows_ref[d]
        lo, lw = _SMALL_OFF["loss"]
        loss_ref[...] = jnp.sum(total[:, lo:lo + lw], axis=1, keepdims=True)
        for j, (_, key, n) in enumerate(_SMALL_PARAMS):
            off = _SMALL_OFF[key][0]
            g = total[:, off:off + n]
            w_ref, m_ref, v_ref = p_refs[3 * j:3 * j + 3]
            o_refs[4 * j][...] = g
            o_refs[4 * j + 1][...], o_refs[4 * j + 2][...], o_refs[4 * j + 3][...] = _adamw_math(w_ref[...], g, m_ref[...], v_ref[...])

    flat = [a for p in params for a in p]
    out_shape = [jax.ShapeDtypeStruct((1, 1), F32)] + [jax.ShapeDtypeStruct((1, n), F32) for _, _, n in _SMALL_PARAMS for _ in range(4)]
    return pl.pallas_call(body, name="small_update", out_shape=out_shape)(rows, *flat)


_BIG = [("w_in", "cols_t"), ("w_sb_out", "cols"), ("w_fox_out", "cols"), ("w_o", "rows"),
        ("w_ffn_gate", "cols_t"), ("w_ffn_up", "cols_t"), ("w_ffn_down", "rows")]


def _shard2d(a, how):
    return a[0].T if how == "cols_t" else a[0]


def _unshard(g, how):
    if how == "cols":
        return g.transpose(1, 0, 2).reshape(g.shape[1], 4 * g.shape[2])
    return g.reshape(4 * g.shape[1], g.shape[2])


def _reshard(w, how):
    if how == "cols":
        return w.reshape(w.shape[0], 4, w.shape[1] // 4).transpose(1, 0, 2)
    return w.reshape(4, w.shape[0] // 4, w.shape[1])


def kernel(x, c, w_ada, b_ada, w_in, b_gate, b_forget, w_sb_out, w_fox_out, w_o, ln1_g, ln1_b, w_ffn_gate, w_ffn_up, w_ffn_down, ln2_g, ln2_b, loss_target, m_w_ada, m_b_ada, m_w_in, m_b_gate, m_b_forget, m_w_sb_out, m_w_fox_out, m_w_o, m_ln1_g, m_ln1_b, m_w_ffn_gate, m_w_ffn_up, m_w_ffn_down, m_ln2_g, m_ln2_b, v_w_ada, v_b_ada, v_w_in, v_b_gate, v_b_forget, v_w_sb_out, v_w_fox_out, v_w_o, v_ln1_g, v_ln1_b, v_w_ffn_gate, v_w_ffn_up, v_w_ffn_down, v_ln2_g, v_ln2_b):
    given = dict(locals())
    mx, my, mc = _mesh_pos()
    chip = 2 * mx + my
    seq = 4 * mx + 2 * my + mc

    c_all = _allgather_rows(c, "gather_c").reshape(8, D)
    n_ada = w_ada.shape[2]
    b_ada_shard = lax.dynamic_slice(b_ada, (0, chip * n_ada), (1, n_ada))
    ada_part = _ada_fwd(c_all, w_ada[0], b_ada_shard)
    ada_all = _allgather_rows(ada_part.reshape(1, 8 * n_ada), "gather_ada").reshape(4, 2, 8, n_ada)
    ada_row = lax.dynamic_slice(ada_all, (0, mc, seq, 0), (4, 1, 1, n_ada)).reshape(1, N_COND * D)
    ada = [ada_row[:, j * D:(j + 1) * D] for j in range(N_COND)]

    w_in_g = _gather_two_level(_shard2d(w_in, "cols_t").astype(MXU), "gather_w_in")
    wi = _unshard(w_in_g, "cols_t")
    w_all = jnp.concatenate([wi[:OFF_FGATE + N_FGATE], jnp.zeros((LANE - N_FGATE, D), MXU), wi[OFF_FGATE + N_FGATE:]], axis=0)
    bf_pad = jnp.concatenate([b_forget, jnp.zeros((1, LANE - N_FGATE), F32)], axis=1)
    late = _BIG[1:]
    late_riders = [_shard2d(given[n], how).astype(MXU) for n, how in late]
    pieces = {}

    def late_full(gathered):
        return [_unshard(g, how) for (_, how), g in zip(late, gathered)]

    def early_grads(dw):
        for n, how in late:
            pieces[n] = _reshard(dw[n][0], how)
        return [_reshard(dw[n][1], how) for n, how in late]

    def w_in_grads(dwi):
        pieces["w_in"] = _reshard(jnp.concatenate(dwi[:6] + [dwi[6][:N_FGATE], dwi[7]], axis=0), "cols_t")
        return [pieces["w_in"].astype(MXU)]

    out = _local_step(x[0], loss_target[0], ada, w_all, b_gate, bf_pad, (late_riders, late_full), early_grads, w_in_grads,
                      ln1_g, ln1_b, ln2_g, ln2_b)

    row = jnp.concatenate(out["d_ada"] + [out["dln1_g"], out["dln1_b"], out["dln2_g"], out["dln2_b"], out["db_gate"],
                                          out["db_forget"], out["loss_lanes"]], axis=1)
    rows = _allgather_rows(row, "gather_small")
    small = _small_update(rows, [(given[p], given["m_" + p], given["v_" + p]) for p, _, _ in _SMALL_PARAMS])
    loss = small[0].reshape(())
    res = {}
    for j, (p, _, _) in enumerate(_SMALL_PARAMS):
        res[p] = small[1 + 4 * j:5 + 4 * j]

    dada_all = rows.reshape(8, _SMALL_LEN)[:, :N_COND * D]
    dada_shard = lax.dynamic_slice(dada_all, (0, chip * n_ada), (8, n_ada))
    g_ada = _ada_bwd(c_all, dada_shard)
    res["w_ada"] = [a[None] for a in _adamw(w_ada[0], m_w_ada[0], v_w_ada[0], [g_ada], "adamw_w_ada")]

    received = dict(zip([n for n, _ in late], out["early_received"]))
    (received["w_in"],) = out["w_in_received"]
    chip_arr = jnp.reshape(chip, (1,)).astype(jnp.int32)
    partial = [_reduce_chips(chip_arr, pieces[n], received[n], "reduce_" + n) for n, _ in _BIG]
    theirs = _sibling_exchange(partial, "swap_cores")
    for (n, how), mine, other in zip(_BIG, partial, theirs):
        upd = _adamw(_shard2d(given[n], how), _shard2d(given["m_" + n], how), _shard2d(given["v_" + n], how), [mine, other], "adamw_" + n)
        res[n] = [(a.T if how == "cols_t" else a)[None] for a in upd]

    order = ["w_ada", "b_ada", "w_in", "b_gate", "b_forget", "w_sb_out", "w_fox_out", "w_o", "ln1_g", "ln1_b",
             "w_ffn_gate", "w_ffn_up", "w_ffn_down", "ln2_g", "ln2_b"]
    return (loss, out["grad_x"][None], *[res[n][0] for n in order], *[res[n][1] for n in order],
            *[res[n][2] for n in order], *[res[n][3] for n in order])
```

```python
import functools

import jax
import jax.numpy as jnp
from jax import lax
from jax.experimental import pallas as pl
from jax.experimental.pallas import tpu as pltpu

F32 = jnp.float32
MXU = jnp.bfloat16

D = 1024
HEAD_DIM = 64
WIDTH = 512
D_FF = 2816
N_COND = 6
LN_EPS = 1e-5
ALPHA = 2.0 ** 0.25
QK_SCALE = HEAD_DIM ** -0.5
OFF_FGATE = 6 * WIDTH
N_FGATE = 8
IN_COLS = OFF_FGATE + N_FGATE + 2 * D
LANE = 128
W_ALL_COLS = OFF_FGATE + LANE + 2 * D
TQ = 512
SB_TQ = 256
ADAM_LR, ADAM_B1, ADAM_B2, ADAM_EPS, ADAM_WD, ADAM_STEP = 0.001, 0.9, 0.999, 1e-08, 0.01, 10
NEG = -1e30
DEAD_LOG = -120.0
RS_COUNT_LANE = LANE - 1
MESH_AXES = ("x", "y", "c")
VMEM_BIG = 56 * 1024 * 1024


def _dot(a, b):
    return jnp.dot(a, b, preferred_element_type=F32)


def _dot_nt(a, b):
    return lax.dot_general(a, b, (((1,), (1,)), ((), ())), preferred_element_type=F32)


def _dot_tn(a, b):
    return lax.dot_general(a, b, (((0,), (0,)), ((), ())), preferred_element_type=F32)


def _ln(x):
    mu = jnp.mean(x, axis=-1, keepdims=True)
    xc = x - mu
    var = jnp.mean(xc * xc, axis=-1, keepdims=True)
    rstd = lax.rsqrt(var + LN_EPS)
    return xc * rstd, rstd


def _ln_bwd(dxhat, xhat, rstd):
    return rstd * (dxhat - jnp.mean(dxhat, axis=-1, keepdims=True) - xhat * jnp.mean(dxhat * xhat, axis=-1, keepdims=True))


def _sigmoid(x):
    return 1.0 / (1.0 + jnp.exp(-x))


def _colsum(x):
    return jnp.sum(x, axis=0, keepdims=True)


def _split(x):
    hi = x.astype(MXU)
    lo = (x - hi.astype(F32)).astype(MXU)
    return jnp.concatenate([hi, lo], axis=1)


def _rows(tm, n):
    return pl.BlockSpec((tm, n), lambda i: (i, 0))


def _fixed(r, n):
    return pl.BlockSpec((r, n), lambda i: (0, 0))


def _res(a):
    return pl.BlockSpec(a.shape, lambda i: (0, 0), pipeline_mode=pl.Buffered(1))


def _params(limit=None, sem=None):
    return pltpu.CompilerParams(vmem_limit_bytes=limit, dimension_semantics=sem)


def _in_proj(x, sh1, sc1, w_all, b_gate):
    s = x.shape[0]
    tm = 256

    def body(x_ref, sh_ref, sc_ref, w_ref, bg_ref, u_ref, qkv_ref, fl_ref, gl_ref):
        xhat, _ = _ln(x_ref[...])
        u = (xhat * (1.0 + sc_ref[...]) + sh_ref[...]).astype(MXU)
        u_ref[...] = u
        for c0 in range(0, OFF_FGATE, WIDTH):
            p = _dot_nt(u, w_ref[c0:c0 + WIDTH, :])
            if c0 in (0, 3 * WIDTH):
                p = p * QK_SCALE
            qkv_ref[:, c0:c0 + WIDTH] = p.astype(MXU)
        fl_ref[...] = _dot_nt(u, w_ref[OFF_FGATE:OFF_FGATE + LANE, :])
        for c0 in range(0, 2 * D, D):
            gl_ref[:, c0:c0 + D] = _dot_nt(u, w_ref[OFF_FGATE + LANE + c0:OFF_FGATE + LANE + c0 + D, :]) + bg_ref[:, c0:c0 + D]

    return pl.pallas_call(
        body, name="in_proj", grid=(s // tm,),
        in_specs=[_rows(tm, D), _fixed(1, D), _fixed(1, D), _res(w_all), _fixed(1, 2 * D)],
        out_specs=[_rows(tm, D), _rows(tm, OFF_FGATE), _rows(tm, LANE), _rows(tm, 2 * D)],
        out_shape=[jax.ShapeDtypeStruct((s, D), MXU), jax.ShapeDtypeStruct((s, OFF_FGATE), MXU),
                   jax.ShapeDtypeStruct((s, LANE), F32), jax.ShapeDtypeStruct((s, 2 * D), F32)],
        compiler_params=_params(VMEM_BIG),
    )(x, sh1, sc1, w_all, b_gate)


def _log_sigmoid_parts(z):
    e = jnp.exp(-jnp.abs(z))
    return -(jnp.maximum(z, 0.0) + jnp.log(1.0 + e)), e


def _fcum_fwd(fl, bf):
    s = fl.shape[0]
    nb = s // LANE

    def body(fl_ref, bf_ref, fc_ref, fkt_ref):
        r = lax.broadcasted_iota(jnp.int32, (LANE, LANE), 0)
        c = lax.broadcasted_iota(jnp.int32, (LANE, LANE), 1)
        tri = (c <= r).astype(F32)

        def step(b, carry):
            r0 = pl.multiple_of(b * LANE, LANE)
            xb = fl_ref[pl.ds(r0, LANE), :] + bf_ref[...]
            ls = _log_sigmoid_parts(-xb)[0]
            cs = jnp.dot(tri, ls, precision=lax.Precision.HIGHEST, preferred_element_type=F32) + carry
            fc_ref[pl.ds(r0, LANE), :] = cs
            fkt_ref[b] = cs.T[:N_FGATE, :]
            return cs[LANE - 1:LANE, :]

        lax.fori_loop(0, nb, step, jnp.zeros((1, LANE), F32))

    return pl.pallas_call(
        body, name="fcum_fwd",
        out_shape=[jax.ShapeDtypeStruct((s, LANE), F32), jax.ShapeDtypeStruct((nb, N_FGATE, LANE), F32)],
    )(fl, bf)


def _attn_specs(s, col0, tq):
    return [pl.BlockSpec((tq, LANE), lambda hp, i: (i, col0 + hp)),
            pl.BlockSpec((s, LANE), lambda hp, i: (0, col0 + 4 + hp)),
            pl.BlockSpec((s, LANE), lambda hp, i: (0, col0 + 8 + hp))]


def _tile_iotas(tq):
    lane = lax.broadcasted_iota(jnp.int32, (tq, LANE), 1)
    row = lax.broadcasted_iota(jnp.int32, (tq, tq), 0)
    col = lax.broadcasted_iota(jnp.int32, (tq, tq), 1)
    return lane, row, col


def _sub_blocks(nk):
    return [slice(j * LANE, (j + 1) * LANE) for j in range(nk // LANE)]


def _over_strips(tile, tq):
    return tile(slice(0, tq), tq)


def _tri(below):
    r = lax.broadcasted_iota(jnp.int32, (LANE, LANE), 0)
    c = lax.broadcasted_iota(jnp.int32, (LANE, LANE), 1)
    t = jnp.concatenate([((r > c) if below else (r < c)).astype(MXU), jnp.ones((LANE, LANE), MXU)], axis=1)
    return jnp.concatenate([t, t], axis=0)


def _call_with_riders(body, name, grid, in_specs, out_specs, out_shape, scratch, args, riders, gather, limit=None, swaps=()):
    if swaps:
        return _call_with_riders_and_swaps(body, name, grid, in_specs, out_specs, out_shape, scratch, args, riders, gather, limit, swaps)
    nr, n_in, n_out, n_sc = len(riders), len(in_specs), len(out_specs), len(scratch)

    def at_step(which):
        hit = None
        for d, n in enumerate(grid):
            here = pl.program_id(d) == (0 if which == "first" else n - 1)
            hit = here if hit is None else hit & here
        return hit

    def wrapped(*refs):
        ins, rin = refs[:n_in], refs[n_in:n_in + nr]
        outs, rout = refs[n_in + nr:n_in + nr + n_out], refs[n_in + nr + n_out:n_in + 2 * nr + n_out]
        own, sems = refs[n_in + 2 * nr + n_out:n_in + 2 * nr + n_out + n_sc], refs[n_in + 2 * nr + n_out + n_sc:]
        if nr:
            @pl.when(at_step("first"))
            def _():
                _exchange_start(rin, rout, sems, gather)

        body(*ins, *outs, *own)
        if nr:
            @pl.when(at_step("last"))
            def _():
                _exchange_wait(rin, rout, sems, gather)

    res = pl.pallas_call(
        wrapped, name=name, grid=grid,
        in_specs=list(in_specs) + [_ANY] * nr, out_specs=list(out_specs) + [_ANY] * nr,
        out_shape=list(out_shape) + _exchange_out_shape(riders),
        scratch_shapes=list(scratch) + (_exchange_sems(nr) if nr else []),
        compiler_params=_params(limit),
    )(*args, *riders)
    return res[:n_out], res[n_out:]


def _call_with_riders_and_swaps(body, name, grid, in_specs, out_specs, out_shape, scratch, args, riders, gather, limit, swaps):
    nr, ns, n_in, n_out, n_sc = len(riders), len(swaps), len(in_specs), len(out_specs), len(scratch)
    last = tuple(n - 1 for n in grid)

    def at_step(step):
        hit = None
        for d, v in enumerate(step):
            here = pl.program_id(d) == v
            hit = here if hit is None else hit & here
        return hit

    def sibling_copies(sin, sout, send_sems, recv_sems):
        x, y, c = _mesh_pos()
        return [pltpu.make_async_remote_copy(src_ref=sin[t], dst_ref=sout[t], send_sem=send_sems.at[t], recv_sem=recv_sems.at[t],
                                             device_id=(x, y, 1 - c), device_id_type=_MESH_ID) for t in range(ns)]

    def wrapped(*refs):
        pos = 0
        groups = []
        for n in (n_in, nr, ns, n_out, nr, ns, n_sc, 3, 2):
            groups.append(refs[pos:pos + n])
            pos += n
        ins, rin, sin, outs, rout, sout, own, sems, swap_sems = groups

        @pl.when(at_step((0,) * len(grid)))
        def _():
            _exchange_start(rin, rout, sems, gather)
            for cp in sibling_copies(sin, sout, *swap_sems):
                cp.start()

        body(*ins, *outs, *own)

        @pl.when(at_step(last))
        def _():
            _exchange_wait(rin, rout, sems, gather)
            copies = sibling_copies(sin, sout, *swap_sems)
            for cp in copies:
                cp.wait_recv()
            for cp in copies:
                cp.wait_send()

    res = pl.pallas_call(
        wrapped, name=name, grid=grid,
        in_specs=list(in_specs) + [_ANY] * (nr + ns), out_specs=list(out_specs) + [_ANY] * (nr + ns),
        out_shape=list(out_shape) + _exchange_out_shape(riders) + [jax.ShapeDtypeStruct(a.shape, a.dtype) for a in swaps],
        scratch_shapes=list(scratch) + _exchange_sems(nr) + [pltpu.SemaphoreType.DMA((ns,)), pltpu.SemaphoreType.DMA((ns,))],
        compiler_params=_params(limit),
    )(*args, *riders, *swaps)
    return res[:n_out], res[n_out:]


def _sb_fwd(qkv, riders=()):
    s = qkv.shape[0]
    tq = SB_TQ
    nq = s // tq
    assert nq <= RS_COUNT_LANE

    def body(q_ref, k_ref, v_ref, o_ref, rs_ref):
        i = pl.program_id(1)
        lane, row, col = _tile_iotas(tq)
        u2 = _tri(True)
        diag = col < row
        q = q_ref[...]
        qms = [jnp.where(hm, q, jnp.zeros_like(q)) for hm in (lane < HEAD_DIM, lane >= HEAD_DIM)]

        def step(kb, carry, masked):
            k0 = pl.multiple_of(kb * tq, tq)
            k = k_ref[pl.ds(k0, tq), :]
            v = v_ref[pl.ds(k0, tq), :]
            def tile(rows, nk, qm, state):
                run, acc, rt = (t[rows] for t in state)
                z = _dot_nt(qm[rows], k[:nk])
                lneg, _ = _log_sigmoid_parts(z)
                lpos = z + lneg
                if masked:
                    lneg = jnp.where(diag[rows, :nk], lneg, 0.0)
                rt = jnp.where(lane[rows] == kb, run, rt)
                a = []
                for sl in reversed(_sub_blocks(nk)):
                    st = _dot(_split(lneg[:, sl]), u2)
                    a.append(jnp.exp(lpos[:, sl] + st[:, :LANE] + run))
                    run = run + st[:, LANE:]
                a = jnp.concatenate(a[::-1], axis=1)
                if masked:
                    a = jnp.where(diag[rows, :nk], a, 0.0)
                return run, acc + _dot(a.astype(MXU), v[:nk]), rt

            return tuple(_over_strips(functools.partial(tile, qm=qm, state=state), tq) for qm, state in zip(qms, carry))

        zero = jnp.zeros((tq, LANE), F32)
        carry = step(i, ((zero, zero, zero),) * 2, True)

        def alive(cr):
            return jnp.maximum(jnp.max(cr[0][0]), jnp.max(cr[1][0])) > DEAD_LOG

        def walk(state):
            j, _, cr = state
            cr = step(i - 1 - j, cr, False)
            return j + 1, alive(cr), cr

        walked, _, carry = lax.while_loop(lambda state: (state[0] < i) & state[1], walk, (jnp.int32(0), alive(carry), carry))
        count = walked.astype(F32)
        rs_ref[0] = jnp.where(lane == RS_COUNT_LANE, count, carry[0][2])
        rs_ref[1] = jnp.where(lane == RS_COUNT_LANE, count, carry[1][2])
        o_ref[...] = jnp.where(lane < HEAD_DIM, carry[0][1], carry[1][1]).astype(o_ref.dtype)

    return _call_with_riders(
        body, "sb_fwd", (4, nq), _attn_specs(s, 0, tq),
        [pl.BlockSpec((tq, LANE), lambda hp, i: (i, hp)), pl.BlockSpec((2, tq, LANE), lambda hp, i: (hp, i, 0))],
        [jax.ShapeDtypeStruct((s, WIDTH), MXU), jax.ShapeDtypeStruct((8, s, LANE), F32)], [], (qkv, qkv, qkv), riders, True)


def _sb_bwd(qkv, do, rs, riders=()):
    s = qkv.shape[0]
    tq = SB_TQ
    nq = s // tq

    def body(q_ref, k_ref, v_ref, do_ref, rs_ref, dq_ref, dk_ref, dv_ref, dk_acc, dv_acc):
        i = pl.program_id(1)

        @pl.when(i == 0)
        def _():
            dk_acc[...] = jnp.zeros_like(dk_acc)
            dv_acc[...] = jnp.zeros_like(dv_acc)

        lane, row, col = _tile_iotas(tq)
        u2 = _tri(True)
        l2 = _tri(False)
        diag = col < row
        q = q_ref[...]
        do = do_ref[...]
        heads = [(jnp.where(hm, q, jnp.zeros_like(q)), jnp.where(hm, do, jnp.zeros_like(do)), rs_ref[hh])
                 for hh, hm in enumerate((lane < HEAD_DIM, lane >= HEAD_DIM))]

        def step(kb, carry, masked):
            k0 = pl.multiple_of(kb * tq, tq)
            k = k_ref[pl.ds(k0, tq), :]
            v = v_ref[pl.ds(k0, tq), :]
            to_keys = {}

            def tile(rows, nk, qm, dom, rblk, state):
                gpre, dq = (t[rows] for t in state)
                z = _dot_nt(qm[rows], k[:nk])
                lneg, e = _log_sigmoid_parts(z)
                lpos = z + lneg
                if masked:
                    lneg = jnp.where(diag[rows, :nk], lneg, 0.0)
                run = jnp.sum(jnp.where(lane[rows] == kb, rblk[rows], 0.0), axis=1, keepdims=True) + jnp.zeros_like(gpre)
                a = []
                for sl in reversed(_sub_blocks(nk)):
                    st = _dot(_split(lneg[:, sl]), u2)
                    a.append(jnp.exp(lpos[:, sl] + st[:, :LANE] + run))
                    run = run + st[:, LANE:]
                a = jnp.concatenate(a[::-1], axis=1)
                if masked:
                    a = jnp.where(diag[rows, :nk], a, 0.0)
                g = a * _dot_nt(dom[rows], v[:nk])
                pre = []
                for sl in _sub_blocks(nk):
                    pt = _dot(_split(g[:, sl]), l2)
                    pre.append(gpre + pt[:, :LANE])
                    gpre = gpre + pt[:, LANE:]
                sig = jnp.where(z >= 0.0, 1.0, e) / (1.0 + e)
                dz = g - (g + jnp.concatenate(pre, axis=1)) * sig
                if masked:
                    dz = jnp.where(diag[rows, :nk], dz, 0.0)
                dzb = dz.astype(MXU)
                both = to_keys.setdefault(nk, [0.0, 0.0])
                both[0] = both[0] + _dot_tn(dzb, qm[rows])
                both[1] = both[1] + _dot_tn(a.astype(MXU), dom[rows])
                return gpre, dq + _dot(dzb, k[:nk])

            new = tuple(_over_strips(functools.partial(tile, qm=qm, dom=dom, rblk=rblk, state=state), tq)
                        for (qm, dom, rblk), state in zip(heads, carry))
            for nk, (dk, dv) in to_keys.items():
                dk_acc[pl.ds(k0, nk), :] += dk
                dv_acc[pl.ds(k0, nk), :] += dv
            return new

        walked = jnp.max(jnp.where(lane[:8] == RS_COUNT_LANE, rs_ref[0, 0:8, :], 0.0))
        first = i - jnp.clip(walked.astype(jnp.int32), 0, i)
        zero = jnp.zeros((tq, LANE), F32)
        carry = step(i, lax.fori_loop(first, i, lambda kb, cr: step(kb, cr, False), ((zero, zero),) * 2), True)
        dq_ref[...] = (jnp.where(lane < HEAD_DIM, carry[0][1], carry[1][1]) * QK_SCALE).astype(dq_ref.dtype)

        @pl.when(i == nq - 1)
        def _():
            dk_ref[...] = dk_acc[...].astype(dk_ref.dtype)
            dv_ref[...] = dv_acc[...].astype(dv_ref.dtype)

    blk = pl.BlockSpec((tq, LANE), lambda hp, i: (i, hp))
    whole = pl.BlockSpec((s, LANE), lambda hp, i: (0, hp))
    return _call_with_riders(
        body, "sb_bwd", (4, nq), _attn_specs(s, 0, tq) + [blk, pl.BlockSpec((2, tq, LANE), lambda hp, i: (hp, i, 0))],
        [blk, whole, whole], [jax.ShapeDtypeStruct((s, WIDTH), MXU)] * 3,
        [pltpu.VMEM((s, LANE), F32), pltpu.VMEM((s, LANE), F32)], (qkv, qkv, qkv, do, rs), riders, False)


def _key_bias(fkt_ref, kb, h, tq):
    n_sub = tq // LANE
    return jnp.concatenate([fkt_ref[kb * n_sub + j, pl.ds(h, 1), :] for j in range(n_sub)], axis=1)


def _fox_fwd(qkv, fc, fkt, riders=()):
    s = qkv.shape[0]
    tq = TQ
    nq = s // tq
    nb = fkt.shape[0]

    def body(q_ref, k_ref, v_ref, fq_ref, fkt_ref, o_ref, lse_ref):
        hp = pl.program_id(0)
        i = pl.program_id(1)
        lane, row, col = _tile_iotas(tq)
        diag = col <= row
        q = q_ref[...]
        fqb = fq_ref[...]
        heads = []
        for hh in range(2):
            h = 2 * hp + hh
            hm = (lane >= HEAD_DIM) if hh else (lane < HEAD_DIM)
            heads.append((h, jnp.where(hm, q, jnp.zeros_like(q)), jnp.sum(jnp.where(lane == h, fqb, 0.0), axis=1, keepdims=True)))

        def step(kb, carry, masked):
            k0 = pl.multiple_of(kb * tq, tq)
            k = k_ref[pl.ds(k0, tq), :]
            v = v_ref[pl.ds(k0, tq), :]
            def tile(rows, nk, h, qm, fq, state):
                m, l, acc = (t[rows] for t in state)
                z = _dot_nt(qm[rows], k[:nk]) + fq[rows] - _key_bias(fkt_ref, kb, h, tq)[:, :nk]
                if masked:
                    z = jnp.where(diag[rows, :nk], z, NEG)
                mn = jnp.maximum(m, jnp.max(z, axis=1, keepdims=True))
                p = jnp.exp(z - mn)
                alpha = jnp.exp(m - mn)
                return mn, alpha * l + jnp.sum(p, axis=1, keepdims=True), alpha * acc + _dot(p.astype(MXU), v[:nk])

            return tuple(_over_strips(functools.partial(tile, h=h, qm=qm, fq=fq, state=state), tq)
                         for (h, qm, fq), state in zip(heads, carry))

        init = ((jnp.full((tq, 1), NEG, F32), jnp.zeros((tq, 1), F32), jnp.zeros((tq, LANE), F32)),) * 2
        carry = step(i, lax.fori_loop(0, i, lambda kb, cr: step(kb, cr, False), init), True)
        outs = []
        for hh, (m, l, acc) in enumerate(carry):
            outs.append(acc / l)
            lse_ref[hh] = jnp.broadcast_to(m + jnp.log(l), (tq, LANE))
        o_ref[...] = jnp.where(lane < HEAD_DIM, outs[0], outs[1]).astype(o_ref.dtype)

    return _call_with_riders(
        body, "fox_fwd", (4, nq),
        _attn_specs(s, 12, tq) + [pl.BlockSpec((tq, LANE), lambda hp, i: (i, 0)), pl.BlockSpec((nb, N_FGATE, LANE), lambda hp, i: (0, 0, 0))],
        [pl.BlockSpec((tq, LANE), lambda hp, i: (i, hp)), pl.BlockSpec((2, tq, LANE), lambda hp, i: (hp, i, 0))],
        [jax.ShapeDtypeStruct((s, WIDTH), MXU), jax.ShapeDtypeStruct((8, s, LANE), F32)], [], (qkv, qkv, qkv, fc, fkt), riders, True)


def _fox_bwd(qkv, fc, fkt, do, o, lse, riders=()):
    s = qkv.shape[0]
    tq = TQ
    nq = s // tq
    nb = fkt.shape[0]

    def body(q_ref, k_ref, v_ref, fq_ref, fkt_ref, do_ref, o_ref, lse_ref, dq_ref, dk_ref, dv_ref, dfk_ref, dfq_ref, dk_acc, dv_acc):
        hp = pl.program_id(0)
        i = pl.program_id(1)

        @pl.when(i == 0)
        def _():
            dk_acc[...] = jnp.zeros_like(dk_acc)
            dv_acc[...] = jnp.zeros_like(dv_acc)

        @pl.when((i == 0) & (hp == 0))
        def _():
            dfk_ref[...] = jnp.zeros_like(dfk_ref)

        lane, row, col = _tile_iotas(tq)
        diag = col <= row
        q = q_ref[...]
        do = do_ref[...]
        dof = do.astype(F32) * o_ref[...].astype(F32)
        fqb = fq_ref[...]
        heads = []
        for hh in range(2):
            h = 2 * hp + hh
            hm = (lane >= HEAD_DIM) if hh else (lane < HEAD_DIM)
            heads.append((h, jnp.where(hm, q, jnp.zeros_like(q)), jnp.where(hm, do, jnp.zeros_like(do)),
                          jnp.sum(jnp.where(hm, dof, 0.0), axis=1, keepdims=True),
                          jnp.sum(jnp.where(lane == h, fqb, 0.0), axis=1, keepdims=True), lse_ref[hh][:, :1]))

        def step(kb, carry, masked):
            k0 = pl.multiple_of(kb * tq, tq)
            k = k_ref[pl.ds(k0, tq), :]
            v = v_ref[pl.ds(k0, tq), :]
            to_keys = {}

            def tile(rows, nk, h, qm, dom, delta, fq, lse_t, state):
                dq, rsum = (t[rows] for t in state)
                z = _dot_nt(qm[rows], k[:nk]) + fq[rows] - _key_bias(fkt_ref, kb, h, tq)[:, :nk]
                if masked:
                    z = jnp.where(diag[rows, :nk], z, NEG)
                p = jnp.exp(z - lse_t[rows])
                ds = p * (_dot_nt(dom[rows], v[:nk]) - delta[rows])
                dsb = ds.astype(MXU)
                both = to_keys.setdefault(nk, [0.0, 0.0])
                both[0] = both[0] + _dot_tn(dsb, qm[rows])
                both[1] = both[1] + _dot_tn(p.astype(MXU), dom[rows])
                csum = _colsum(ds)
                for j, sl in enumerate(_sub_blocks(nk)):
                    dfk_ref[kb * (tq // LANE) + j, pl.ds(h, 1), :] += -csum[:, sl]
                return dq + _dot(dsb, k[:nk]), rsum + jnp.sum(ds, axis=1, keepdims=True)

            new = tuple(_over_strips(functools.partial(tile, h=h, qm=qm, dom=dom, delta=delta, fq=fq, lse_t=lse_t, state=state), tq)
                        for (h, qm, dom, delta, fq, lse_t), state in zip(heads, carry))
            for nk, (dk, dv) in to_keys.items():
                dk_acc[pl.ds(k0, nk), :] += dk
                dv_acc[pl.ds(k0, nk), :] += dv
            return new

        init = ((jnp.zeros((tq, LANE), F32), jnp.zeros((tq, 1), F32)),) * 2
        carry = step(i, lax.fori_loop(0, i, lambda kb, cr: step(kb, cr, False), init), True)
        dq_ref[...] = (jnp.where(lane < HEAD_DIM, carry[0][0], carry[1][0]) * QK_SCALE).astype(dq_ref.dtype)
        dfq_ref[0] = jnp.where(lane == heads[0][0], carry[0][1], jnp.where(lane == heads[1][0], carry[1][1], 0.0))

        @pl.when(i == nq - 1)
        def _():
            dk_ref[...] = dk_acc[...].astype(dk_ref.dtype)
            dv_ref[...] = dv_acc[...].astype(dv_ref.dtype)

    blk = pl.BlockSpec((tq, LANE), lambda hp, i: (i, hp))
    whole = pl.BlockSpec((s, LANE), lambda hp, i: (0, hp))
    pair = pl.BlockSpec((2, tq, LANE), lambda hp, i: (hp, i, 0))
    fkt_spec = pl.BlockSpec((nb, N_FGATE, LANE), lambda hp, i: (0, 0, 0))
    return _call_with_riders(
        body, "fox_bwd", (4, nq),
        _attn_specs(s, 12, tq) + [pl.BlockSpec((tq, LANE), lambda hp, i: (i, 0)), fkt_spec, blk, blk, pair],
        [blk, whole, whole, fkt_spec, pl.BlockSpec((1, tq, LANE), lambda hp, i: (hp, i, 0))],
        [jax.ShapeDtypeStruct((s, WIDTH), MXU)] * 3
        + [jax.ShapeDtypeStruct((nb, N_FGATE, LANE), F32), jax.ShapeDtypeStruct((4, s, LANE), F32)],
        [pltpu.VMEM((s, LANE), F32), pltpu.VMEM((s, LANE), F32)], (qkv, qkv, qkv, fc, fkt, do, o, lse), riders, False)


def _fcum_bwd(dfkt, dfq, fl, bf):
    s = fl.shape[0]
    nb = s // LANE

    def body(dfkt_ref, dfq_ref, fl_ref, bf_ref, df_ref, dbf_ref, tail_ref):
        @pl.when(pl.program_id(0) == 0)
        def _():
            tail_ref[...] = jnp.zeros_like(tail_ref)
            dbf_ref[...] = jnp.zeros_like(dbf_ref)

        r = lax.broadcasted_iota(jnp.int32, (LANE, LANE), 0)
        c = lax.broadcasted_iota(jnp.int32, (LANE, LANE), 1)
        tri = (c >= r).astype(F32)
        dfc = jnp.concatenate([dfkt_ref[0], jnp.zeros((LANE - N_FGATE, LANE), F32)], axis=0).T
        dfc = dfc + ((dfq_ref[0] + dfq_ref[1]) + (dfq_ref[2] + dfq_ref[3]))
        dls = jnp.dot(tri, dfc, precision=lax.Precision.HIGHEST, preferred_element_type=F32) + tail_ref[...]
        xb = fl_ref[...] + bf_ref[...]
        e = jnp.exp(-jnp.abs(xb))
        dfl = dls * (jnp.where(xb >= 0.0, e, 1.0) / (1.0 + e))
        df_ref[...] = dfl.astype(df_ref.dtype)
        tail_ref[...] = dls[0:1, :]
        dbf_ref[...] += _colsum(dfl)

    return pl.pallas_call(
        body, name="fcum_bwd", grid=(nb,),
        in_specs=[pl.BlockSpec((1, N_FGATE, LANE), lambda j: (nb - 1 - j, 0, 0)), pl.BlockSpec((4, LANE, LANE), lambda j: (0, nb - 1 - j, 0)),
                  pl.BlockSpec((LANE, LANE), lambda j: (nb - 1 - j, 0)), _fixed(1, LANE)],
        out_specs=[pl.BlockSpec((LANE, LANE), lambda j: (nb - 1 - j, 0)), _fixed(1, LANE)],
        out_shape=[jax.ShapeDtypeStruct((s, LANE), MXU), jax.ShapeDtypeStruct((1, LANE), F32)],
        scratch_shapes=[pltpu.VMEM((1, LANE), F32)],
    )(dfkt, dfq, fl, bf)


def _mix_fwd(x, o_sb, o_fx, gl, w_sb, w_fx, w_o, g1, ln1_g, ln1_b, sh2, sc2):
    s = x.shape[0]
    tm = 256

    def body(x_ref, osb_ref, ofx_ref, gl_ref, wsb_ref, wfx_ref, wo_ref, g1_ref, lg_ref, lb_ref, sh_ref, sc_ref, r1_ref, u2_ref):
        mixin = (_sigmoid(gl_ref[:, :D]) * _dot(osb_ref[...], wsb_ref[...])
                 + _sigmoid(gl_ref[:, D:]) * _dot(ofx_ref[...], wfx_ref[...]))
        r1 = ALPHA * x_ref[...] + g1_ref[...] * _dot(mixin.astype(MXU), wo_ref[...])
        r1_ref[...] = r1
        x1 = _ln(r1)[0] * lg_ref[...] + lb_ref[...]
        u2_ref[...] = (_ln(x1)[0] * (1.0 + sc_ref[...]) + sh_ref[...]).astype(MXU)

    vec = _fixed(1, D)
    return pl.pallas_call(
        body, name="mix_fwd", grid=(s // tm,),
        in_specs=[_rows(tm, D), _rows(tm, WIDTH), _rows(tm, WIDTH), _rows(tm, 2 * D), _res(w_sb), _res(w_fx), _res(w_o),
                  vec, vec, vec, vec, vec],
        out_specs=[_rows(tm, D), _rows(tm, D)],
        out_shape=[jax.ShapeDtypeStruct((s, D), F32), jax.ShapeDtypeStruct((s, D), MXU)],
        compiler_params=_params(VMEM_BIG),
    )(x, o_sb, o_fx, gl, w_sb, w_fx, w_o, g1, ln1_g, ln1_b, sh2, sc2)


def _ffn_fwd(r1, u2, tgt, w_g, w_u, w_d, g2, ln1_g, ln1_b, ln2_g, ln2_b):
    s = r1.shape[0]
    tm = 256

    def body(r1_ref, u2_ref, t_ref, wg_ref, wu_ref, wd_ref, g2_ref, l1g_ref, l1b_ref, l2g_ref, l2b_ref,
             hg_ref, hu_ref, dxa_ref, dh_ref, acc_ref):
        @pl.when(pl.program_id(0) == 0)
        def _():
            acc_ref[...] = jnp.zeros_like(acc_ref)

        u2 = u2_ref[...]
        hg = _dot_nt(u2, wg_ref[...])
        hu = _dot_nt(u2, wu_ref[...])
        hg_ref[...] = hg
        hu_ref[...] = hu
        h = _dot((hg * _sigmoid(hg) * hu).astype(MXU), wd_ref[...])
        x1 = _ln(r1_ref[...])[0] * l1g_ref[...] + l1b_ref[...]
        xh2, rstd2 = _ln(ALPHA * x1 + g2_ref[...] * h)
        err = xh2 * l2g_ref[...] + l2b_ref[...] - t_ref[...]
        dy = err * (1.0 / D)
        dr2 = _ln_bwd(dy * l2g_ref[...], xh2, rstd2)
        dxa_ref[...] = ALPHA * dr2
        dh_ref[...] = (g2_ref[...] * dr2).astype(MXU)
        acc_ref[0:1, :] += _colsum(dr2 * h)
        acc_ref[1:2, :] += _colsum(dy * xh2)
        acc_ref[2:3, :] += _colsum(dy)
        acc_ref[3:4, :] += _colsum(err * err) * (0.5 / D)

    vec = _fixed(1, D)
    return pl.pallas_call(
        body, name="ffn_fwd", grid=(s // tm,),
        in_specs=[_rows(tm, D), _rows(tm, D), _rows(tm, D), _res(w_g), _res(w_u), _res(w_d), vec, vec, vec, vec, vec],
        out_specs=[_rows(tm, D_FF), _rows(tm, D_FF), _rows(tm, D), _rows(tm, D), _fixed(8, D)],
        out_shape=[jax.ShapeDtypeStruct((s, D_FF), F32), jax.ShapeDtypeStruct((s, D_FF), F32),
                   jax.ShapeDtypeStruct((s, D), F32), jax.ShapeDtypeStruct((s, D), MXU), jax.ShapeDtypeStruct((8, D), F32)],
        compiler_params=_params(VMEM_BIG),
    )(r1, u2, tgt, w_g, w_u, w_d, g2, ln1_g, ln1_b, ln2_g, ln2_b)


def _ffn_bwd(dh, hg, hu, w_g, w_u, w_d):
    s = dh.shape[0]
    tm = 256
    half = D_FF // 2

    def body(dh_ref, hg_ref, hu_ref, wg_ref, wu_ref, wd_ref, act_ref, dhg_ref, dhu_ref, du2_ref):
        dh = dh_ref[...]
        du2 = jnp.zeros((tm, D), F32)
        for c0 in (0, half):
            cols = slice(c0, c0 + half)
            dact = _dot_nt(dh, wd_ref[cols, :])
            hg = hg_ref[:, cols]
            hu = hu_ref[:, cols]
            sg = _sigmoid(hg)
            sl = hg * sg
            act_ref[:, cols] = (sl * hu).astype(MXU)
            dhg = (dact * hu * (sg * (1.0 + hg * (1.0 - sg)))).astype(MXU)
            dhu = (dact * sl).astype(MXU)
            dhg_ref[:, cols] = dhg
            dhu_ref[:, cols] = dhu
            du2 = du2 + _dot(dhg, wg_ref[cols, :]) + _dot(dhu, wu_ref[cols, :])
        du2_ref[...] = du2

    return pl.pallas_call(
        body, name="ffn_bwd", grid=(s // tm,),
        in_specs=[_rows(tm, D), _rows(tm, D_FF), _rows(tm, D_FF), _res(w_g), _res(w_u), _res(w_d)],
        out_specs=[_rows(tm, D_FF), _rows(tm, D_FF), _rows(tm, D_FF), _rows(tm, D)],
        out_shape=[jax.ShapeDtypeStruct((s, D_FF), MXU)] * 3 + [jax.ShapeDtypeStruct((s, D), F32)],
        compiler_params=_params(VMEM_BIG),
    )(dh, hg, hu, w_g, w_u, w_d)


def _mix_bwd(du2, dxa, r1, o_sb, o_fx, gl, w_sb, w_fx, w_o, g1, ln1_g, ln1_b, sc2):
    s = r1.shape[0]
    tm = 256
    n_tiles = s // tm

    def body(du2_ref, dxa_ref, r1_ref, osb_ref, ofx_ref, gl_ref, wsb_ref, wfx_ref, wo_ref, g1_ref, lg_ref, lb_ref, sc_ref,
             dx_ref, dosb_ref, dofx_ref, dgl_ref, dbg_ref, acc_ref, dwsb_ref, dwfx_ref, dwo_ref, nsb_ref, nfx_ref, no_ref):
        @pl.when(pl.program_id(0) == 0)
        def _():
            for ref in (acc_ref, dbg_ref, dwsb_ref, dwfx_ref, dwo_ref):
                ref[...] = jnp.zeros_like(ref)

        du2 = du2_ref[...]
        xh1, rstd1 = _ln(r1_ref[...])
        x1 = xh1 * lg_ref[...] + lb_ref[...]
        n1, rstdn = _ln(x1)
        dx1 = dxa_ref[...] + _ln_bwd(du2 * (1.0 + sc_ref[...]), n1, rstdn)
        dr1 = _ln_bwd(dx1 * lg_ref[...], xh1, rstd1)
        dx_ref[...] = ALPHA * dr1
        ysb = _dot(osb_ref[...], wsb_ref[...])
        yfx = _dot(ofx_ref[...], wfx_ref[...])
        gs = _sigmoid(gl_ref[:, :D])
        gf = _sigmoid(gl_ref[:, D:])
        mixin = (gs * ysb + gf * yfx).astype(MXU)
        mix = _dot(mixin, wo_ref[...])
        dmix = (g1_ref[...] * dr1).astype(MXU)
        dmixin = _dot_nt(dmix, wo_ref[...])
        dysb = (dmixin * gs).astype(MXU)
        dyfx = (dmixin * gf).astype(MXU)
        dwo_ref[...] += _dot_tn(mixin, dmix)
        dwsb_ref[...] += _dot_tn(osb_ref[...], dysb)
        dwfx_ref[...] += _dot_tn(ofx_ref[...], dyfx)
        dosb_ref[...] = _dot_nt(dysb, wsb_ref[...]).astype(MXU)
        dofx_ref[...] = _dot_nt(dyfx, wfx_ref[...]).astype(MXU)
        dgs = dmixin * ysb * gs * (1.0 - gs)
        dgf = dmixin * yfx * gf * (1.0 - gf)
        dgl_ref[:, :D] = dgs.astype(MXU)
        dgl_ref[:, D:] = dgf.astype(MXU)
        dbg_ref[:, :D] += _colsum(dgs)
        dbg_ref[:, D:] += _colsum(dgf)
        acc_ref[0:1, :] += _colsum(du2)
        acc_ref[1:2, :] += _colsum(du2 * n1)
        acc_ref[2:3, :] += _colsum(dx1 * xh1)
        acc_ref[3:4, :] += _colsum(dx1)
        acc_ref[4:5, :] += _colsum(dr1 * mix)

        @pl.when(pl.program_id(0) == n_tiles - 1)
        def _():
            nsb_ref[...] = dwsb_ref[...].astype(MXU)
            nfx_ref[...] = dwfx_ref[...].astype(MXU)
            no_ref[...] = dwo_ref[...].astype(MXU)

    vec = _fixed(1, D)
    dw_specs = [_fixed(WIDTH, D), _fixed(WIDTH, D), _fixed(D, D)]
    dw_shapes = [(WIDTH, D), (WIDTH, D), (D, D)]
    return pl.pallas_call(
        body, name="mix_bwd", grid=(n_tiles,),
        in_specs=[_rows(tm, D), _rows(tm, D), _rows(tm, D), _rows(tm, WIDTH), _rows(tm, WIDTH), _rows(tm, 2 * D),
                  _res(w_sb), _res(w_fx), _res(w_o), vec, vec, vec, vec],
        out_specs=[_rows(tm, D), _rows(tm, WIDTH), _rows(tm, WIDTH), _rows(tm, 2 * D), _fixed(1, 2 * D), _fixed(8, D)] + dw_specs * 2,
        out_shape=[jax.ShapeDtypeStruct((s, D), F32)] + [jax.ShapeDtypeStruct((s, WIDTH), MXU)] * 2
        + [jax.ShapeDtypeStruct((s, 2 * D), MXU), jax.ShapeDtypeStruct((1, 2 * D), F32), jax.ShapeDtypeStruct((8, D), F32)]
        + [jax.ShapeDtypeStruct(sh, F32) for sh in dw_shapes] + [jax.ShapeDtypeStruct(sh, MXU) for sh in dw_shapes],
        compiler_params=_params(VMEM_BIG),
    )(du2, dxa, r1, o_sb, o_fx, gl, w_sb, w_fx, w_o, g1, ln1_g, ln1_b, sc2)


def _in_bwd(pieces, x, dxa, w_all, sc1, riders=(), swaps=()):
    s = x.shape[0]
    tm = 256
    n_p = len(pieces)

    def body(*refs):
        p_refs = refs[:n_p]
        x_ref, dxa_ref, w_ref, sc_ref, gx_ref, acc_ref = refs[n_p:]

        @pl.when(pl.program_id(0) == 0)
        def _():
            acc_ref[...] = jnp.zeros_like(acc_ref)

        du1 = jnp.zeros((tm, D), F32)
        for p_ref, (arr, c0) in zip(p_refs, pieces):
            du1 = du1 + _dot(p_ref[...], w_ref[c0:c0 + arr.shape[1], :])
        n0, rstd0 = _ln(x_ref[...])
        gx_ref[...] = dxa_ref[...] + _ln_bwd(du1 * (1.0 + sc_ref[...]), n0, rstd0)
        acc_ref[0:1, :] += _colsum(du1)
        acc_ref[1:2, :] += _colsum(du1 * n0)

    return _call_with_riders(
        body, "in_bwd", (s // tm,),
        [_rows(tm, a.shape[1]) for a, _ in pieces] + [_rows(tm, D), _rows(tm, D), _res(w_all), _fixed(1, D)],
        [_rows(tm, D), _fixed(8, D)], [jax.ShapeDtypeStruct((s, D), F32), jax.ShapeDtypeStruct((8, D), F32)], [],
        (*[a for a, _ in pieces], x, dxa, w_all, sc1), riders, False, VMEM_BIG, swaps)


def _matmul_tn(a, b, name, narrow=False):
    s, m = a.shape
    n = b.shape[1]
    tm = 512 if m % 512 == 0 else (m if m < 512 else m // 2)
    tn = n // 2 if n > 2048 else n
    ts = 2048
    assert m % tm == 0 and tm % LANE == 0 and n % tn == 0 and tn % LANE == 0 and s % ts == 0

    def body(a_ref, b_ref, o_ref, *narrow_ref):
        @pl.when(pl.program_id(2) == 0)
        def _():
            o_ref[...] = jnp.zeros_like(o_ref)

        o_ref[...] += _dot_tn(a_ref[...], b_ref[...])
        if narrow:
            @pl.when(pl.program_id(2) == s // ts - 1)
            def _():
                narrow_ref[0][...] = o_ref[...].astype(MXU)

    out_blk = pl.BlockSpec((tm, tn), lambda i, j, k: (i, j))
    res = pl.pallas_call(
        body, name=name, grid=(m // tm, n // tn, s // ts),
        in_specs=[pl.BlockSpec((ts, tm), lambda i, j, k: (k, i)), pl.BlockSpec((ts, tn), lambda i, j, k: (k, j))],
        out_specs=[out_blk] * (2 if narrow else 1),
        out_shape=[jax.ShapeDtypeStruct((m, n), F32)] + ([jax.ShapeDtypeStruct((m, n), MXU)] if narrow else []),
        compiler_params=_params(VMEM_BIG),
    )(a, b)
    return tuple(res) if narrow else res[0]


def _local_step(x, tgt, ada, w_all, b_gate, bf_pad, late_weights, early_grads, w_in_grads, early_partials,
                ln1_g, ln1_b, ln2_g, ln2_b):
    sh1, sc1, g1, sh2, sc2, g2 = ada
    u1, qkv, fl, gl = _in_proj(x, sh1, sc1, w_all, b_gate)
    fc, fkt = _fcum_fwd(fl, bf_pad)
    late_riders, late_full = late_weights
    n_sb = 3
    (o_sb, rs), gathered_a = _sb_fwd(qkv, late_riders[:n_sb])
    (o_fx, lse), gathered_b = _fox_fwd(qkv, fc, fkt, late_riders[n_sb:])
    w_sb, w_fx, w_o, w_g, w_u, w_d = late_full(list(gathered_a) + list(gathered_b))
    r1, u2 = _mix_fwd(x, o_sb, o_fx, gl, w_sb, w_fx, w_o, g1, ln1_g, ln1_b, sh2, sc2)
    hg, hu, dxa2, dh, acc_f = _ffn_fwd(r1, u2, tgt, w_g, w_u, w_d, g2, ln1_g, ln1_b, ln2_g, ln2_b)
    act, dhg, dhu, du2 = _ffn_bwd(dh, hg, hu, w_g, w_u, w_d)
    dxa1, dosb, dofx, dgl, dbg, acc_m, dw_sb, dw_fx, dw_o, n_sb_out, n_fx_out, n_o = _mix_bwd(
        du2, dxa2, r1, o_sb, o_fx, gl, w_sb, w_fx, w_o, g1, ln1_g, ln1_b, sc2)
    early = dict(w_sb_out=(dw_sb, n_sb_out), w_fox_out=(dw_fx, n_fx_out), w_o=(dw_o, n_o),
                 w_ffn_gate=_matmul_tn(dhg, u2, "dw_ffn_gate", True),
                 w_ffn_up=_matmul_tn(dhu, u2, "dw_ffn_up", True), w_ffn_down=_matmul_tn(act, dh, "dw_ffn_down", True))
    early_riders = early_grads(early)
    (dq_sb, dk_sb, dv_sb), received_a = _sb_bwd(qkv, dosb, rs, early_riders[:n_sb])
    (dq_fx, dk_fx, dv_fx, dfkt, dfq), received_b = _fox_bwd(qkv, fc, fkt, dofx, o_fx, lse, early_riders[n_sb:])
    early_received = list(received_a) + list(received_b)
    df, dbf = _fcum_bwd(dfkt, dfq, fl, bf_pad)
    pieces = [(dq_sb, 0), (dk_sb, WIDTH), (dv_sb, 2 * WIDTH), (dq_fx, 3 * WIDTH), (dk_fx, 4 * WIDTH), (dv_fx, 5 * WIDTH),
              (df, OFF_FGATE), (dgl, OFF_FGATE + LANE)]
    dw_in = [_matmul_tn(p, u1, f"dw_in_{j}") for j, (p, _) in enumerate(pieces)]
    w_in_riders = w_in_grads(dw_in)
    (grad_x, acc_i), exchanged = _in_bwd(pieces, x, dxa1, w_all, sc1, w_in_riders, early_partials(early_received))
    return dict(
        loss_lanes=acc_f[3:4], grad_x=grad_x, dw_in=dw_in, early=early, early_received=early_received,
        w_in_received=exchanged[:len(w_in_riders)], early_swapped=exchanged[len(w_in_riders):],
        d_ada=[acc_i[0:1], acc_i[1:2], acc_m[4:5], acc_m[0:1], acc_m[1:2], acc_f[0:1]],
        dln1_g=acc_m[2:3], dln1_b=acc_m[3:4], dln2_g=acc_f[1:2], dln2_b=acc_f[2:3], db_gate=dbg, db_forget=dbf)


_MESH_ID = pl.DeviceIdType.MESH
_ANY = pl.BlockSpec(memory_space=pl.ANY)
_VMEM = pl.BlockSpec(memory_space=pltpu.VMEM)


def _mesh_pos():
    return lax.axis_index("x"), lax.axis_index("y"), lax.axis_index("c")


def _other_chips(x, y):
    return [(1 - x, y), (x, 1 - y), (1 - x, 1 - y)]


def _allgather_rows(v, name):
    n = v.shape[1]

    def body(v_ref, out_ref, send_sems, recv_sems, local_sem):
        x, y, c = _mesh_pos()
        me = 4 * x + 2 * y + c
        mine = pltpu.make_async_copy(v_ref, out_ref.at[me], local_sem)
        mine.start()
        copies = []
        for d in range(1, 8):
            fx, fy, fc = (d >> 2) & 1, (d >> 1) & 1, d & 1
            to = (1 - x if fx else x, 1 - y if fy else y, 1 - c if fc else c)
            cp = pltpu.make_async_remote_copy(src_ref=v_ref, dst_ref=out_ref.at[me], send_sem=send_sems.at[d - 1],
                                              recv_sem=recv_sems.at[d - 1], device_id=to, device_id_type=_MESH_ID)
            cp.start()
            copies.append(cp)
        for cp in copies:
            cp.wait_recv()
        for cp in copies:
            cp.wait_send()
        mine.wait()

    return pl.pallas_call(
        body, name=name, in_specs=[_VMEM], out_specs=_VMEM,
        out_shape=jax.ShapeDtypeStruct((8, 1, n), v.dtype),
        scratch_shapes=[pltpu.SemaphoreType.DMA((7,)), pltpu.SemaphoreType.DMA((7,)), pltpu.SemaphoreType.DMA(())],
    )(v)


def _chip_exchange(arrays, name, gather):
    nt = len(arrays)

    def body(*refs):
        ins, outs = refs[:nt], refs[nt:2 * nt]
        _exchange_start(ins, outs, refs[2 * nt:], gather)
        _exchange_wait(ins, outs, refs[2 * nt:], gather)

    return pl.pallas_call(
        body, name=name, in_specs=[_ANY] * nt, out_specs=[_ANY] * nt, out_shape=_exchange_out_shape(arrays),
        scratch_shapes=_exchange_sems(nt),
    )(*arrays)


def _exchange_out_shape(arrays):
    return [jax.ShapeDtypeStruct((4,) + a.shape[-2:], a.dtype) for a in arrays]


def _exchange_sems(nt):
    return [pltpu.SemaphoreType.DMA((3 * nt,)), pltpu.SemaphoreType.DMA((3 * nt,)), pltpu.SemaphoreType.DMA((nt,))]


def _exchange_copies(ins, outs, sems, gather):
    send_sems, recv_sems, local_sems = sems
    x, y, c = _mesh_pos()
    me = 2 * x + y
    local, remote = [], []
    for t in range(len(ins)):
        local.append(pltpu.make_async_copy(ins[t] if gather else ins[t].at[me], outs[t].at[me], local_sems.at[t]))
        for j, (px, py) in enumerate(_other_chips(x, y)):
            remote.append(pltpu.make_async_remote_copy(
                src_ref=ins[t] if gather else ins[t].at[2 * px + py], dst_ref=outs[t].at[me], send_sem=send_sems.at[3 * t + j],
                recv_sem=recv_sems.at[3 * t + j], device_id=(px, py, c), device_id_type=_MESH_ID))
    return local, remote


def _exchange_start(ins, outs, sems, gather):
    local, remote = _exchange_copies(ins, outs, sems, gather)
    for cp in local + remote:
        cp.start()


def _exchange_wait(ins, outs, sems, gather):
    local, remote = _exchange_copies(ins, outs, sems, gather)
    for cp in remote:
        cp.wait_recv()
    for cp in remote:
        cp.wait_send()
    for cp in local:
        cp.wait()


def _gather_two_level(shard, name):
    r, n = shard.shape
    half = n // 2
    assert half % LANE == 0

    def body(in_ref, out_ref, ici_send, ici_recv, d2d_send, d2d_recv, local_sem):
        x, y, c = _mesh_pos()
        me = 2 * x + y
        mine = pl.ds(pl.multiple_of(c * half, LANE), half)
        theirs = pl.ds(pl.multiple_of((1 - c) * half, LANE), half)
        local = pltpu.make_async_copy(in_ref, out_ref.at[me], local_sem)
        local.start()
        chips = _other_chips(x, y)
        over_ici = [pltpu.make_async_remote_copy(
            src_ref=in_ref.at[:, mine], dst_ref=out_ref.at[me, :, mine], send_sem=ici_send.at[j], recv_sem=ici_recv.at[j],
            device_id=(px, py, c), device_id_type=_MESH_ID) for j, (px, py) in enumerate(chips)]
        for cp in over_ici:
            cp.start()
        passed_on = [pltpu.make_async_remote_copy(
            src_ref=out_ref.at[2 * px + py, :, mine], dst_ref=out_ref.at[2 * px + py, :, mine], send_sem=d2d_send.at[j],
            recv_sem=d2d_recv.at[j], device_id=(x, y, 1 - c), device_id_type=_MESH_ID) for j, (px, py) in enumerate(chips)]
        for j, (px, py) in enumerate(chips):
            pltpu.make_async_remote_copy(
                src_ref=in_ref.at[:, mine], dst_ref=out_ref.at[2 * px + py, :, mine], send_sem=ici_send.at[j],
                recv_sem=ici_recv.at[j], device_id=(px, py, c), device_id_type=_MESH_ID).wait_recv()
            passed_on[j].start()
        for j, (px, py) in enumerate(chips):
            pltpu.make_async_remote_copy(
                src_ref=out_ref.at[2 * px + py, :, theirs], dst_ref=out_ref.at[2 * px + py, :, theirs], send_sem=d2d_send.at[j],
                recv_sem=d2d_recv.at[j], device_id=(x, y, 1 - c), device_id_type=_MESH_ID).wait_recv()
        for cp in over_ici + passed_on:
            cp.wait_send()
        local.wait()

    sems = pltpu.SemaphoreType.DMA((3,))
    return pl.pallas_call(
        body, name=name, in_specs=[_ANY], out_specs=_ANY, out_shape=jax.ShapeDtypeStruct((4, r, n), shard.dtype),
        scratch_shapes=[sems, sems, sems, sems, pltpu.SemaphoreType.DMA(())],
    )(shard)


def _sibling_exchange(arrays, name):
    nt = len(arrays)

    def body(*refs):
        ins, outs = refs[:nt], refs[nt:2 * nt]
        send_sems, recv_sems = refs[2 * nt:]
        x, y, c = _mesh_pos()
        copies = []
        for t in range(nt):
            cp = pltpu.make_async_remote_copy(src_ref=ins[t], dst_ref=outs[t], send_sem=send_sems.at[t], recv_sem=recv_sems.at[t],
                                              device_id=(x, y, 1 - c), device_id_type=_MESH_ID)
            cp.start()
            copies.append(cp)
        for cp in copies:
            cp.wait_recv()
        for cp in copies:
            cp.wait_send()

    return pl.pallas_call(
        body, name=name, in_specs=[_ANY] * nt, out_specs=[_ANY] * nt,
        out_shape=[jax.ShapeDtypeStruct(a.shape, a.dtype) for a in arrays],
        scratch_shapes=[pltpu.SemaphoreType.DMA((nt,)), pltpu.SemaphoreType.DMA((nt,))],
    )(*arrays)


def _tiles(r, n):
    for tr in (256, 352, 128):
        if r % tr == 0:
            return tr, n, r // tr, lambda i: (i, 0)
    assert n % 256 == 0
    return r, 256, n // 256, lambda i: (0, i)


def _reduce_chips(chip, pieces, recv, name):
    _, r, n = pieces.shape
    tr, tn, steps, at = _tiles(r, n)

    def body(chip_ref, own_ref, recv_ref, out_ref):
        me = chip_ref[0]
        total = jnp.zeros((tr, tn), F32)
        for k in range(4):
            total = total + jnp.where(me == k, own_ref[0], recv_ref[k].astype(F32))
        out_ref[...] = total

    return pl.pallas_call(
        body, name=name,
        grid_spec=pltpu.PrefetchScalarGridSpec(
            num_scalar_prefetch=1, grid=(steps,),
            in_specs=[pl.BlockSpec((1, tr, tn), lambda i, chip_ref: (chip_ref[0],) + at(i)),
                      pl.BlockSpec((4, tr, tn), lambda i, chip_ref: (0,) + at(i))],
            out_specs=pl.BlockSpec((tr, tn), lambda i, chip_ref: at(i))),
        out_shape=jax.ShapeDtypeStruct((r, n), F32),
    )(chip, pieces, recv)


def _adamw_math(w, g, m, v):
    m = ADAM_B1 * m + (1.0 - ADAM_B1) * g
    v = ADAM_B2 * v + (1.0 - ADAM_B2) * (g * g)
    m_hat = m / (1.0 - ADAM_B1 ** ADAM_STEP)
    v_hat = v / (1.0 - ADAM_B2 ** ADAM_STEP)
    return -ADAM_LR * (m_hat / (jnp.sqrt(v_hat) + ADAM_EPS) + ADAM_WD * w), m, v


def _adamw(w, m, v, g_parts, name):
    r, n = w.shape
    tr, tn, steps, at = _tiles(r, n)
    blk = pl.BlockSpec((tr, tn), at)
    ng = len(g_parts)

    def body(*refs):
        w_ref, m_ref, v_ref = refs[:3]
        g_refs = refs[3:3 + ng]
        g_out, d_out, m_out, v_out = refs[3 + ng:]
        g = g_refs[0][...]
        for gr in g_refs[1:]:
            g = g + gr[...]
        g_out[...] = g
        d_out[...], m_out[...], v_out[...] = _adamw_math(w_ref[...], g, m_ref[...], v_ref[...])

    return pl.pallas_call(
        body, name=name, grid=(steps,),
        in_specs=[blk] * (3 + ng), out_specs=[blk] * 4,
        out_shape=[jax.ShapeDtypeStruct((r, n), F32)] * 4,
    )(w, m, v, *g_parts)


def _ada_fwd(c_all, w_shard, b_shard):
    n = w_shard.shape[1]
    tn = 512

    def body(c_ref, w_ref, b_ref, o_ref):
        cv = c_ref[...]
        ca = (cv * _sigmoid(cv)).astype(MXU)
        o_ref[...] = _dot(ca, w_ref[...].astype(MXU)) + b_ref[...]

    return pl.pallas_call(
        body, name="ada_fwd", grid=(n // tn,),
        in_specs=[_fixed(8, D), pl.BlockSpec((D, tn), lambda j: (0, j)), pl.BlockSpec((1, tn), lambda j: (0, j))],
        out_specs=pl.BlockSpec((8, tn), lambda j: (0, j)),
        out_shape=jax.ShapeDtypeStruct((8, n), F32),
    )(c_all, w_shard, b_shard)


def _ada_bwd(c_all, dada_shard):
    n = dada_shard.shape[1]
    tn = 512

    def body(c_ref, d_ref, o_ref):
        cv = c_ref[...]
        ca = (cv * _sigmoid(cv)).astype(MXU)
        o_ref[...] = _dot_tn(ca, d_ref[...].astype(MXU))

    return pl.pallas_call(
        body, name="ada_bwd", grid=(n // tn,),
        in_specs=[_fixed(8, D), pl.BlockSpec((8, tn), lambda j: (0, j))],
        out_specs=pl.BlockSpec((D, tn), lambda j: (0, j)),
        out_shape=jax.ShapeDtypeStruct((D, n), F32),
    )(c_all, dada_shard)


_SMALL = [("d_ada", N_COND * D), ("ln1_g", D), ("ln1_b", D), ("ln2_g", D), ("ln2_b", D), ("b_gate", 2 * D), ("b_forget", LANE),
          ("loss", D)]
_SMALL_OFF = {}
_o = 0
for _n, _w in _SMALL:
    _SMALL_OFF[_n] = (_o, _w)
    _o += _w
_SMALL_LEN = _o
_SMALL_PARAMS = [("b_ada", "d_ada", N_COND * D), ("b_gate", "b_gate", 2 * D), ("b_forget", "b_forget", N_FGATE),
                 ("ln1_g", "ln1_g", D), ("ln1_b", "ln1_b", D), ("ln2_g", "ln2_g", D), ("ln2_b", "ln2_b", D)]


def _small_update(rows, params):
    npar = len(_SMALL_PARAMS)

    def body(*refs):
        rows_ref = refs[0]
        p_refs = refs[1:1 + 3 * npar]
        loss_ref = refs[1 + 3 * npar]
        o_refs = refs[2 + 3 * npar:]
        total = rows_ref[0]
        for d in range(1, 8):
            total = total + rows_ref[d]
        lo, lw = _SMALL_OFF["loss"]
        loss_ref[...] = jnp.sum(total[:, lo:lo + lw], axis=1, keepdims=True)
        for j, (_, key, n) in enumerate(_SMALL_PARAMS):
            off = _SMALL_OFF[key][0]
            g = total[:, off:off + n]
            w_ref, m_ref, v_ref = p_refs[3 * j:3 * j + 3]
            o_refs[4 * j][...] = g
            o_refs[4 * j + 1][...], o_refs[4 * j + 2][...], o_refs[4 * j + 3][...] = _adamw_math(w_ref[...], g, m_ref[...], v_ref[...])

    flat = [a for p in params for a in p]
    out_shape = [jax.ShapeDtypeStruct((1, 1), F32)] + [jax.ShapeDtypeStruct((1, n), F32) for _, _, n in _SMALL_PARAMS for _ in range(4)]
    return pl.pallas_call(body, name="small_update", out_shape=out_shape)(rows, *flat)


_BIG = [("w_in", "cols_t"), ("w_sb_out", "cols"), ("w_fox_out", "cols"), ("w_o", "rows"),
        ("w_ffn_gate", "cols_t"), ("w_ffn_up", "cols_t"), ("w_ffn_down", "rows")]


def _shard2d(a, how):
    return a[0].T if how == "cols_t" else a[0]


def _unshard(g, how):
    if how == "cols":
        return g.transpose(1, 0, 2).reshape(g.shape[1], 4 * g.shape[2])
    return g.reshape(4 * g.shape[1], g.shape[2])


def _reshard(w, how):
    if how == "cols":
        return w.reshape(w.shape[0], 4, w.shape[1] // 4).transpose(1, 0, 2)
    return w.reshape(4, w.shape[0] // 4, w.shape[1])


def kernel(x, c, w_ada, b_ada, w_in, b_gate, b_forget, w_sb_out, w_fox_out, w_o, ln1_g, ln1_b, w_ffn_gate, w_ffn_up, w_ffn_down, ln2_g, ln2_b, loss_target, m_w_ada, m_b_ada, m_w_in, m_b_gate, m_b_forget, m_w_sb_out, m_w_fox_out, m_w_o, m_ln1_g, m_ln1_b, m_w_ffn_gate, m_w_ffn_up, m_w_ffn_down, m_ln2_g, m_ln2_b, v_w_ada, v_b_ada, v_w_in, v_b_gate, v_b_forget, v_w_sb_out, v_w_fox_out, v_w_o, v_ln1_g, v_ln1_b, v_w_ffn_gate, v_w_ffn_up, v_w_ffn_down, v_ln2_g, v_ln2_b):
    given = dict(locals())
    mx, my, mc = _mesh_pos()
    chip = 2 * mx + my
    seq = 4 * mx + 2 * my + mc

    c_all = _allgather_rows(c, "gather_c").reshape(8, D)
    n_ada = w_ada.shape[2]
    b_ada_shard = lax.dynamic_slice(b_ada, (0, chip * n_ada), (1, n_ada))
    ada_part = _ada_fwd(c_all, w_ada[0], b_ada_shard)
    ada_all = _allgather_rows(ada_part.reshape(1, 8 * n_ada), "gather_ada").reshape(4, 2, 8, n_ada)
    ada_row = lax.dynamic_slice(ada_all, (0, mc, seq, 0), (4, 1, 1, n_ada)).reshape(1, N_COND * D)
    ada = [ada_row[:, j * D:(j + 1) * D] for j in range(N_COND)]

    w_in_g = _gather_two_level(_shard2d(w_in, "cols_t").astype(MXU), "gather_w_in")
    wi = _unshard(w_in_g, "cols_t")
    w_all = jnp.concatenate([wi[:OFF_FGATE + N_FGATE], jnp.zeros((LANE - N_FGATE, D), MXU), wi[OFF_FGATE + N_FGATE:]], axis=0)
    bf_pad = jnp.concatenate([b_forget, jnp.zeros((1, LANE - N_FGATE), F32)], axis=1)
    late = _BIG[1:]
    late_riders = [_shard2d(given[n], how).astype(MXU) for n, how in late]
    pieces = {}

    def late_full(gathered):
        return [_unshard(g, how) for (_, how), g in zip(late, gathered)]

    def early_grads(dw):
        for n, how in late:
            pieces[n] = _reshard(dw[n][0], how)
        return [_reshard(dw[n][1], how) for n, how in late]

    def w_in_grads(dwi):
        pieces["w_in"] = _reshard(jnp.concatenate(dwi[:6] + [dwi[6][:N_FGATE], dwi[7]], axis=0), "cols_t")
        return [pieces["w_in"].astype(MXU)]

    chip_arr = jnp.reshape(chip, (1,)).astype(jnp.int32)
    partial = {}

    def early_partials(recv):
        for (n, _), r in zip(late, recv):
            partial[n] = _reduce_chips(chip_arr, pieces[n], r, "reduce_" + n)
        return [partial[n] for n, _ in late]

    out = _local_step(x[0], loss_target[0], ada, w_all, b_gate, bf_pad, (late_riders, late_full), early_grads, w_in_grads,
                      early_partials, ln1_g, ln1_b, ln2_g, ln2_b)

    row = jnp.concatenate(out["d_ada"] + [out["dln1_g"], out["dln1_b"], out["dln2_g"], out["dln2_b"], out["db_gate"],
                                          out["db_forget"], out["loss_lanes"]], axis=1)
    rows = _allgather_rows(row, "gather_small")
    small = _small_update(rows, [(given[p], given["m_" + p], given["v_" + p]) for p, _, _ in _SMALL_PARAMS])
    loss = small[0].reshape(())
    res = {}
    for j, (p, _, _) in enumerate(_SMALL_PARAMS):
        res[p] = small[1 + 4 * j:5 + 4 * j]

    dada_all = rows.reshape(8, _SMALL_LEN)[:, :N_COND * D]
    dada_shard = lax.dynamic_slice(dada_all, (0, chip * n_ada), (8, n_ada))
    g_ada = _ada_bwd(c_all, dada_shard)
    res["w_ada"] = [a[None] for a in _adamw(w_ada[0], m_w_ada[0], v_w_ada[0], [g_ada], "adamw_w_ada")]

    theirs = dict(zip([n for n, _ in late], out["early_swapped"]))
    partial["w_in"] = _reduce_chips(chip_arr, pieces["w_in"], out["w_in_received"][0], "reduce_w_in")
    (theirs["w_in"],) = _sibling_exchange([partial["w_in"]], "swap_cores")
    for n, how in _BIG:
        upd = _adamw(_shard2d(given[n], how), _shard2d(given["m_" + n], how), _shard2d(given["v_" + n], how),
                     [partial[n], theirs[n]], "adamw_" + n)
        res[n] = [(a.T if how == "cols_t" else a)[None] for a in upd]

    order = ["w_ada", "b_ada", "w_in", "b_gate", "b_forget", "w_sb_out", "w_fox_out", "w_o", "ln1_g", "ln1_b",
             "w_ffn_gate", "w_ffn_up", "w_ffn_down", "ln2_g", "ln2_b"]
    return (loss, out["grad_x"][None], *[res[n][0] for n in order], *[res[n][1] for n in order],
            *[res[n][2] for n in order], *[res[n][3] for n in order])
```

```python
import functools

import jax
import jax.numpy as jnp
from jax import lax
from jax.experimental import pallas as pl
from jax.experimental.pallas import tpu as pltpu

F32 = jnp.float32
MXU = jnp.bfloat16

D = 1024
HEAD_DIM = 64
WIDTH = 512
D_FF = 2816
N_COND = 6
LN_EPS = 1e-5
ALPHA = 2.0 ** 0.25
QK_SCALE = HEAD_DIM ** -0.5
OFF_FGATE = 6 * WIDTH
N_FGATE = 8
IN_COLS = OFF_FGATE + N_FGATE + 2 * D
LANE = 128
W_ALL_COLS = OFF_FGATE + LANE + 2 * D
TQ = 512
SB_TQ = 256
ADAM_LR, ADAM_B1, ADAM_B2, ADAM_EPS, ADAM_WD, ADAM_STEP = 0.001, 0.9, 0.999, 1e-08, 0.01, 10
NEG = -1e30
DEAD_LOG = -120.0
RS_COUNT_LANE = LANE - 1
MESH_AXES = ("x", "y", "c")
VMEM_BIG = 56 * 1024 * 1024


def _dot(a, b):
    return jnp.dot(a, b, preferred_element_type=F32)


def _dot_nt(a, b):
    return lax.dot_general(a, b, (((1,), (1,)), ((), ())), preferred_element_type=F32)


def _dot_tn(a, b):
    return lax.dot_general(a, b, (((0,), (0,)), ((), ())), preferred_element_type=F32)


def _ln(x):
    mu = jnp.mean(x, axis=-1, keepdims=True)
    xc = x - mu
    var = jnp.mean(xc * xc, axis=-1, keepdims=True)
    rstd = lax.rsqrt(var + LN_EPS)
    return xc * rstd, rstd


def _ln_bwd(dxhat, xhat, rstd):
    return rstd * (dxhat - jnp.mean(dxhat, axis=-1, keepdims=True) - xhat * jnp.mean(dxhat * xhat, axis=-1, keepdims=True))


def _sigmoid(x):
    return 1.0 / (1.0 + jnp.exp(-x))


def _colsum(x):
    return jnp.sum(x, axis=0, keepdims=True)


def _split(x):
    hi = x.astype(MXU)
    lo = (x - hi.astype(F32)).astype(MXU)
    return jnp.concatenate([hi, lo], axis=1)


def _rows(tm, n):
    return pl.BlockSpec((tm, n), lambda i: (i, 0))


def _fixed(r, n):
    return pl.BlockSpec((r, n), lambda i: (0, 0))


def _res(a):
    return pl.BlockSpec(a.shape, lambda i: (0, 0), pipeline_mode=pl.Buffered(1))


def _params(limit=None, sem=None):
    return pltpu.CompilerParams(vmem_limit_bytes=limit, dimension_semantics=sem)


def _in_proj(x, sh1, sc1, w_all, b_gate):
    s = x.shape[0]
    tm = 256

    def body(x_ref, sh_ref, sc_ref, w_ref, bg_ref, u_ref, qkv_ref, fl_ref, gl_ref):
        xhat, _ = _ln(x_ref[...])
        u = (xhat * (1.0 + sc_ref[...]) + sh_ref[...]).astype(MXU)
        u_ref[...] = u
        for c0 in range(0, OFF_FGATE, WIDTH):
            p = _dot_nt(u, w_ref[c0:c0 + WIDTH, :])
            if c0 in (0, 3 * WIDTH):
                p = p * QK_SCALE
            qkv_ref[:, c0:c0 + WIDTH] = p.astype(MXU)
        fl_ref[...] = _dot_nt(u, w_ref[OFF_FGATE:OFF_FGATE + LANE, :])
        for c0 in range(0, 2 * D, D):
            gl_ref[:, c0:c0 + D] = _dot_nt(u, w_ref[OFF_FGATE + LANE + c0:OFF_FGATE + LANE + c0 + D, :]) + bg_ref[:, c0:c0 + D]

    return pl.pallas_call(
        body, name="in_proj", grid=(s // tm,),
        in_specs=[_rows(tm, D), _fixed(1, D), _fixed(1, D), _res(w_all), _fixed(1, 2 * D)],
        out_specs=[_rows(tm, D), _rows(tm, OFF_FGATE), _rows(tm, LANE), _rows(tm, 2 * D)],
        out_shape=[jax.ShapeDtypeStruct((s, D), MXU), jax.ShapeDtypeStruct((s, OFF_FGATE), MXU),
                   jax.ShapeDtypeStruct((s, LANE), F32), jax.ShapeDtypeStruct((s, 2 * D), F32)],
        compiler_params=_params(VMEM_BIG),
    )(x, sh1, sc1, w_all, b_gate)


def _log_sigmoid_parts(z):
    e = jnp.exp(-jnp.abs(z))
    return -(jnp.maximum(z, 0.0) + jnp.log(1.0 + e)), e


def _fcum_fwd(fl, bf):
    s = fl.shape[0]
    nb = s // LANE

    def body(fl_ref, bf_ref, fc_ref, fkt_ref):
        r = lax.broadcasted_iota(jnp.int32, (LANE, LANE), 0)
        c = lax.broadcasted_iota(jnp.int32, (LANE, LANE), 1)
        tri = (c <= r).astype(F32)

        def step(b, carry):
            r0 = pl.multiple_of(b * LANE, LANE)
            xb = fl_ref[pl.ds(r0, LANE), :] + bf_ref[...]
            ls = _log_sigmoid_parts(-xb)[0]
            cs = jnp.dot(tri, ls, precision=lax.Precision.HIGHEST, preferred_element_type=F32) + carry
            fc_ref[pl.ds(r0, LANE), :] = cs
            fkt_ref[b] = cs.T[:N_FGATE, :]
            return cs[LANE - 1:LANE, :]

        lax.fori_loop(0, nb, step, jnp.zeros((1, LANE), F32))

    return pl.pallas_call(
        body, name="fcum_fwd",
        out_shape=[jax.ShapeDtypeStruct((s, LANE), F32), jax.ShapeDtypeStruct((nb, N_FGATE, LANE), F32)],
    )(fl, bf)


def _attn_specs(s, col0, tq):
    return [pl.BlockSpec((tq, LANE), lambda hp, i: (i, col0 + hp)),
            pl.BlockSpec((s, LANE), lambda hp, i: (0, col0 + 4 + hp)),
            pl.BlockSpec((s, LANE), lambda hp, i: (0, col0 + 8 + hp))]


def _tile_iotas(tq):
    lane = lax.broadcasted_iota(jnp.int32, (tq, LANE), 1)
    row = lax.broadcasted_iota(jnp.int32, (tq, tq), 0)
    col = lax.broadcasted_iota(jnp.int32, (tq, tq), 1)
    return lane, row, col


def _sub_blocks(nk):
    return [slice(j * LANE, (j + 1) * LANE) for j in range(nk // LANE)]


def _over_strips(tile, tq):
    return tile(slice(0, tq), tq)


def _tri(below):
    r = lax.broadcasted_iota(jnp.int32, (LANE, LANE), 0)
    c = lax.broadcasted_iota(jnp.int32, (LANE, LANE), 1)
    t = jnp.concatenate([((r > c) if below else (r < c)).astype(MXU), jnp.ones((LANE, LANE), MXU)], axis=1)
    return jnp.concatenate([t, t], axis=0)


def _call_with_riders(body, name, grid, in_specs, out_specs, out_shape, scratch, args, riders, gather, limit=None, swaps=()):
    if swaps:
        return _call_with_riders_and_swaps(body, name, grid, in_specs, out_specs, out_shape, scratch, args, riders, gather, limit, swaps)
    nr, n_in, n_out, n_sc = len(riders), len(in_specs), len(out_specs), len(scratch)

    def at_step(which):
        hit = None
        for d, n in enumerate(grid):
            here = pl.program_id(d) == (0 if which == "first" else n - 1)
            hit = here if hit is None else hit & here
        return hit

    def wrapped(*refs):
        ins, rin = refs[:n_in], refs[n_in:n_in + nr]
        outs, rout = refs[n_in + nr:n_in + nr + n_out], refs[n_in + nr + n_out:n_in + 2 * nr + n_out]
        own, sems = refs[n_in + 2 * nr + n_out:n_in + 2 * nr + n_out + n_sc], refs[n_in + 2 * nr + n_out + n_sc:]
        if nr:
            @pl.when(at_step("first"))
            def _():
                _exchange_start(rin, rout, sems, gather)

        body(*ins, *outs, *own)
        if nr:
            @pl.when(at_step("last"))
            def _():
                _exchange_wait(rin, rout, sems, gather)

    res = pl.pallas_call(
        wrapped, name=name, grid=grid,
        in_specs=list(in_specs) + [_ANY] * nr, out_specs=list(out_specs) + [_ANY] * nr,
        out_shape=list(out_shape) + _exchange_out_shape(riders),
        scratch_shapes=list(scratch) + (_exchange_sems(nr) if nr else []),
        compiler_params=_params(limit),
    )(*args, *riders)
    return res[:n_out], res[n_out:]


def _call_with_riders_and_swaps(body, name, grid, in_specs, out_specs, out_shape, scratch, args, riders, gather, limit, swaps):
    nr, ns, n_in, n_out, n_sc = len(riders), len(swaps), len(in_specs), len(out_specs), len(scratch)
    last = tuple(n - 1 for n in grid)

    def at_step(step):
        hit = None
        for d, v in enumerate(step):
            here = pl.program_id(d) == v
            hit = here if hit is None else hit & here
        return hit

    def sibling_copies(sin, sout, send_sems, recv_sems):
        x, y, c = _mesh_pos()
        return [pltpu.make_async_remote_copy(src_ref=sin[t], dst_ref=sout[t], send_sem=send_sems.at[t], recv_sem=recv_sems.at[t],
                                             device_id=(x, y, 1 - c), device_id_type=_MESH_ID) for t in range(ns)]

    def wrapped(*refs):
        pos = 0
        groups = []
        for n in (n_in, nr, ns, n_out, nr, ns, n_sc, 3, 2):
            groups.append(refs[pos:pos + n])
            pos += n
        ins, rin, sin, outs, rout, sout, own, sems, swap_sems = groups

        @pl.when(at_step((0,) * len(grid)))
        def _():
            _exchange_start(rin, rout, sems, gather)
            for cp in sibling_copies(sin, sout, *swap_sems):
                cp.start()

        body(*ins, *outs, *own)

        @pl.when(at_step(last))
        def _():
            _exchange_wait(rin, rout, sems, gather)
            copies = sibling_copies(sin, sout, *swap_sems)
            for cp in copies:
                cp.wait_recv()
            for cp in copies:
                cp.wait_send()

    res = pl.pallas_call(
        wrapped, name=name, grid=grid,
        in_specs=list(in_specs) + [_ANY] * (nr + ns), out_specs=list(out_specs) + [_ANY] * (nr + ns),
        out_shape=list(out_shape) + _exchange_out_shape(riders) + [jax.ShapeDtypeStruct(a.shape, a.dtype) for a in swaps],
        scratch_shapes=list(scratch) + _exchange_sems(nr) + [pltpu.SemaphoreType.DMA((ns,)), pltpu.SemaphoreType.DMA((ns,))],
        compiler_params=_params(limit),
    )(*args, *riders, *swaps)
    return res[:n_out], res[n_out:]


def _sb_fwd(qkv, riders=()):
    s = qkv.shape[0]
    tq = SB_TQ
    nq = s // tq
    assert nq <= RS_COUNT_LANE

    def body(q_ref, k_ref, v_ref, o_ref, rs_ref):
        i = pl.program_id(1)
        lane, row, col = _tile_iotas(tq)
        u2 = _tri(True)
        diag = col < row
        q = q_ref[...]
        qms = [jnp.where(hm, q, jnp.zeros_like(q)) for hm in (lane < HEAD_DIM, lane >= HEAD_DIM)]

        def step(kb, carry, masked):
            k0 = pl.multiple_of(kb * tq, tq)
            k = k_ref[pl.ds(k0, tq), :]
            v = v_ref[pl.ds(k0, tq), :]
            def tile(rows, nk, qm, state):
                run, acc, rt = (t[rows] for t in state)
                z = _dot_nt(qm[rows], k[:nk])
                lneg, _ = _log_sigmoid_parts(z)
                lpos = z + lneg
                if masked:
                    lneg = jnp.where(diag[rows, :nk], lneg, 0.0)
                rt = jnp.where(lane[rows] == kb, run, rt)
                a = []
                for sl in reversed(_sub_blocks(nk)):
                    st = _dot(_split(lneg[:, sl]), u2)
                    a.append(jnp.exp(lpos[:, sl] + st[:, :LANE] + run))
                    run = run + st[:, LANE:]
                a = jnp.concatenate(a[::-1], axis=1)
                if masked:
                    a = jnp.where(diag[rows, :nk], a, 0.0)
                return run, acc + _dot(a.astype(MXU), v[:nk]), rt

            return tuple(_over_strips(functools.partial(tile, qm=qm, state=state), tq) for qm, state in zip(qms, carry))

        zero = jnp.zeros((tq, LANE), F32)
        carry = step(i, ((zero, zero, zero),) * 2, True)

        def alive(cr):
            return jnp.maximum(jnp.max(cr[0][0]), jnp.max(cr[1][0])) > DEAD_LOG

        def walk(state):
            j, _, cr = state
            cr = step(i - 1 - j, cr, False)
            return j + 1, alive(cr), cr

        walked, _, carry = lax.while_loop(lambda state: (state[0] < i) & state[1], walk, (jnp.int32(0), alive(carry), carry))
        count = walked.astype(F32)
        rs_ref[0] = jnp.where(lane == RS_COUNT_LANE, count, carry[0][2])
        rs_ref[1] = jnp.where(lane == RS_COUNT_LANE, count, carry[1][2])
        o_ref[...] = jnp.where(lane < HEAD_DIM, carry[0][1], carry[1][1]).astype(o_ref.dtype)

    return _call_with_riders(
        body, "sb_fwd", (4, nq), _attn_specs(s, 0, tq),
        [pl.BlockSpec((tq, LANE), lambda hp, i: (i, hp)), pl.BlockSpec((2, tq, LANE), lambda hp, i: (hp, i, 0))],
        [jax.ShapeDtypeStruct((s, WIDTH), MXU), jax.ShapeDtypeStruct((8, s, LANE), F32)], [], (qkv, qkv, qkv), riders, True)


def _sb_bwd(qkv, do, rs, riders=()):
    s = qkv.shape[0]
    tq = SB_TQ
    nq = s // tq

    def body(q_ref, k_ref, v_ref, do_ref, rs_ref, dq_ref, dk_ref, dv_ref, dk_acc, dv_acc):
        i = pl.program_id(1)

        @pl.when(i == 0)
        def _():
            dk_acc[...] = jnp.zeros_like(dk_acc)
            dv_acc[...] = jnp.zeros_like(dv_acc)

        lane, row, col = _tile_iotas(tq)
        u2 = _tri(True)
        l2 = _tri(False)
        diag = col < row
        q = q_ref[...]
        do = do_ref[...]
        heads = [(jnp.where(hm, q, jnp.zeros_like(q)), jnp.where(hm, do, jnp.zeros_like(do)), rs_ref[hh])
                 for hh, hm in enumerate((lane < HEAD_DIM, lane >= HEAD_DIM))]

        def step(kb, carry, masked):
            k0 = pl.multiple_of(kb * tq, tq)
            k = k_ref[pl.ds(k0, tq), :]
            v = v_ref[pl.ds(k0, tq), :]
            to_keys = {}

            def tile(rows, nk, qm, dom, rblk, state):
                gpre, dq = (t[rows] for t in state)
                z = _dot_nt(qm[rows], k[:nk])
                lneg, e = _log_sigmoid_parts(z)
                lpos = z + lneg
                if masked:
                    lneg = jnp.where(diag[rows, :nk], lneg, 0.0)
                run = jnp.sum(jnp.where(lane[rows] == kb, rblk[rows], 0.0), axis=1, keepdims=True) + jnp.zeros_like(gpre)
                a = []
                for sl in reversed(_sub_blocks(nk)):
                    st = _dot(_split(lneg[:, sl]), u2)
                    a.append(jnp.exp(lpos[:, sl] + st[:, :LANE] + run))
                    run = run + st[:, LANE:]
                a = jnp.concatenate(a[::-1], axis=1)
                if masked:
                    a = jnp.where(diag[rows, :nk], a, 0.0)
                g = a * _dot_nt(dom[rows], v[:nk])
                pre = []
                for sl in _sub_blocks(nk):
                    pt = _dot(_split(g[:, sl]), l2)
                    pre.append(gpre + pt[:, :LANE])
                    gpre = gpre + pt[:, LANE:]
                sig = jnp.where(z >= 0.0, 1.0, e) / (1.0 + e)
                dz = g - (g + jnp.concatenate(pre, axis=1)) * sig
                if masked:
                    dz = jnp.where(diag[rows, :nk], dz, 0.0)
                dzb = dz.astype(MXU)
                both = to_keys.setdefault(nk, [0.0, 0.0])
                both[0] = both[0] + _dot_tn(dzb, qm[rows])
                both[1] = both[1] + _dot_tn(a.astype(MXU), dom[rows])
                return gpre, dq + _dot(dzb, k[:nk])

            new = tuple(_over_strips(functools.partial(tile, qm=qm, dom=dom, rblk=rblk, state=state), tq)
                        for (qm, dom, rblk), state in zip(heads, carry))
            for nk, (dk, dv) in to_keys.items():
                dk_acc[pl.ds(k0, nk), :] += dk
                dv_acc[pl.ds(k0, nk), :] += dv
            return new

        walked = jnp.max(jnp.where(lane[:8] == RS_COUNT_LANE, rs_ref[0, 0:8, :], 0.0))
        first = i - jnp.clip(walked.astype(jnp.int32), 0, i)
        zero = jnp.zeros((tq, LANE), F32)
        carry = step(i, lax.fori_loop(first, i, lambda kb, cr: step(kb, cr, False), ((zero, zero),) * 2), True)
        dq_ref[...] = (jnp.where(lane < HEAD_DIM, carry[0][1], carry[1][1]) * QK_SCALE).astype(dq_ref.dtype)

        @pl.when(i == nq - 1)
        def _():
            dk_ref[...] = dk_acc[...].astype(dk_ref.dtype)
            dv_ref[...] = dv_acc[...].astype(dv_ref.dtype)

    blk = pl.BlockSpec((tq, LANE), lambda hp, i: (i, hp))
    whole = pl.BlockSpec((s, LANE), lambda hp, i: (0, hp))
    return _call_with_riders(
        body, "sb_bwd", (4, nq), _attn_specs(s, 0, tq) + [blk, pl.BlockSpec((2, tq, LANE), lambda hp, i: (hp, i, 0))],
        [blk, whole, whole], [jax.ShapeDtypeStruct((s, WIDTH), MXU)] * 3,
        [pltpu.VMEM((s, LANE), F32), pltpu.VMEM((s, LANE), F32)], (qkv, qkv, qkv, do, rs), riders, False)


def _key_bias(fkt_ref, kb, h, tq):
    n_sub = tq // LANE
    return jnp.concatenate([fkt_ref[kb * n_sub + j, pl.ds(h, 1), :] for j in range(n_sub)], axis=1)


def _fox_fwd(qkv, fc, fkt, riders=()):
    s = qkv.shape[0]
    tq = TQ
    nq = s // tq
    nb = fkt.shape[0]

    def body(q_ref, k_ref, v_ref, fq_ref, fkt_ref, o_ref, lse_ref):
        hp = pl.program_id(0)
        i = pl.program_id(1)
        lane, row, col = _tile_iotas(tq)
        diag = col <= row
        q = q_ref[...]
        fqb = fq_ref[...]
        heads = []
        for hh in range(2):
            h = 2 * hp + hh
            hm = (lane >= HEAD_DIM) if hh else (lane < HEAD_DIM)
            heads.append((h, jnp.where(hm, q, jnp.zeros_like(q)), jnp.sum(jnp.where(lane == h, fqb, 0.0), axis=1, keepdims=True)))

        def step(kb, carry, masked):
            k0 = pl.multiple_of(kb * tq, tq)
            k = k_ref[pl.ds(k0, tq), :]
            v = v_ref[pl.ds(k0, tq), :]
            def tile(rows, nk, h, qm, fq, state):
                m, l, acc = (t[rows] for t in state)
                z = _dot_nt(qm[rows], k[:nk]) + fq[rows] - _key_bias(fkt_ref, kb, h, tq)[:, :nk]
                if masked:
                    z = jnp.where(diag[rows, :nk], z, NEG)
                mn = jnp.maximum(m, jnp.max(z, axis=1, keepdims=True))
                p = jnp.exp(z - mn)
                alpha = jnp.exp(m - mn)
                return mn, alpha * l + jnp.sum(p, axis=1, keepdims=True), alpha * acc + _dot(p.astype(MXU), v[:nk])

            return tuple(_over_strips(functools.partial(tile, h=h, qm=qm, fq=fq, state=state), tq)
                         for (h, qm, fq), state in zip(heads, carry))

        init = ((jnp.full((tq, 1), NEG, F32), jnp.zeros((tq, 1), F32), jnp.zeros((tq, LANE), F32)),) * 2
        carry = step(i, lax.fori_loop(0, i, lambda kb, cr: step(kb, cr, False), init), True)
        outs = []
        for hh, (m, l, acc) in enumerate(carry):
            outs.append(acc / l)
            lse_ref[hh] = jnp.broadcast_to(m + jnp.log(l), (tq, LANE))
        o_ref[...] = jnp.where(lane < HEAD_DIM, outs[0], outs[1]).astype(o_ref.dtype)

    return _call_with_riders(
        body, "fox_fwd", (4, nq),
        _attn_specs(s, 12, tq) + [pl.BlockSpec((tq, LANE), lambda hp, i: (i, 0)), pl.BlockSpec((nb, N_FGATE, LANE), lambda hp, i: (0, 0, 0))],
        [pl.BlockSpec((tq, LANE), lambda hp, i: (i, hp)), pl.BlockSpec((2, tq, LANE), lambda hp, i: (hp, i, 0))],
        [jax.ShapeDtypeStruct((s, WIDTH), MXU), jax.ShapeDtypeStruct((8, s, LANE), F32)], [], (qkv, qkv, qkv, fc, fkt), riders, True)


def _fox_bwd(qkv, fc, fkt, do, o, lse, riders=()):
    s = qkv.shape[0]
    tq = TQ
    nq = s // tq
    nb = fkt.shape[0]

    def body(q_ref, k_ref, v_ref, fq_ref, fkt_ref, do_ref, o_ref, lse_ref, dq_ref, dk_ref, dv_ref, dfk_ref, dfq_ref, dk_acc, dv_acc):
        hp = pl.program_id(0)
        i = pl.program_id(1)

        @pl.when(i == 0)
        def _():
            dk_acc[...] = jnp.zeros_like(dk_acc)
            dv_acc[...] = jnp.zeros_like(dv_acc)

        @pl.when((i == 0) & (hp == 0))
        def _():
            dfk_ref[...] = jnp.zeros_like(dfk_ref)

        lane, row, col = _tile_iotas(tq)
        diag = col <= row
        q = q_ref[...]
        do = do_ref[...]
        dof = do.astype(F32) * o_ref[...].astype(F32)
        fqb = fq_ref[...]
        heads = []
        for hh in range(2):
            h = 2 * hp + hh
            hm = (lane >= HEAD_DIM) if hh else (lane < HEAD_DIM)
            heads.append((h, jnp.where(hm, q, jnp.zeros_like(q)), jnp.where(hm, do, jnp.zeros_like(do)),
                          jnp.sum(jnp.where(hm, dof, 0.0), axis=1, keepdims=True),
                          jnp.sum(jnp.where(lane == h, fqb, 0.0), axis=1, keepdims=True), lse_ref[hh][:, :1]))

        def step(kb, carry, masked):
            k0 = pl.multiple_of(kb * tq, tq)
            k = k_ref[pl.ds(k0, tq), :]
            v = v_ref[pl.ds(k0, tq), :]
            to_keys = {}

            def tile(rows, nk, h, qm, dom, delta, fq, lse_t, state):
                dq, rsum = (t[rows] for t in state)
                z = _dot_nt(qm[rows], k[:nk]) + fq[rows] - _key_bias(fkt_ref, kb, h, tq)[:, :nk]
                if masked:
                    z = jnp.where(diag[rows, :nk], z, NEG)
                p = jnp.exp(z - lse_t[rows])
                ds = p * (_dot_nt(dom[rows], v[:nk]) - delta[rows])
                dsb = ds.astype(MXU)
                both = to_keys.setdefault(nk, [0.0, 0.0])
                both[0] = both[0] + _dot_tn(dsb, qm[rows])
                both[1] = both[1] + _dot_tn(p.astype(MXU), dom[rows])
                csum = _colsum(ds)
                for j, sl in enumerate(_sub_blocks(nk)):
                    dfk_ref[kb * (tq // LANE) + j, pl.ds(h, 1), :] += -csum[:, sl]
                return dq + _dot(dsb, k[:nk]), rsum + jnp.sum(ds, axis=1, keepdims=True)

            new = tuple(_over_strips(functools.partial(tile, h=h, qm=qm, dom=dom, delta=delta, fq=fq, lse_t=lse_t, state=state), tq)
                        for (h, qm, dom, delta, fq, lse_t), state in zip(heads, carry))
            for nk, (dk, dv) in to_keys.items():
                dk_acc[pl.ds(k0, nk), :] += dk
                dv_acc[pl.ds(k0, nk), :] += dv
            return new

        init = ((jnp.zeros((tq, LANE), F32), jnp.zeros((tq, 1), F32)),) * 2
        carry = step(i, lax.fori_loop(0, i, lambda kb, cr: step(kb, cr, False), init), True)
        dq_ref[...] = (jnp.where(lane < HEAD_DIM, carry[0][0], carry[1][0]) * QK_SCALE).astype(dq_ref.dtype)
        dfq_ref[0] = jnp.where(lane == heads[0][0], carry[0][1], jnp.where(lane == heads[1][0], carry[1][1], 0.0))

        @pl.when(i == nq - 1)
        def _():
            dk_ref[...] = dk_acc[...].astype(dk_ref.dtype)
            dv_ref[...] = dv_acc[...].astype(dv_ref.dtype)

    blk = pl.BlockSpec((tq, LANE), lambda hp, i: (i, hp))
    whole = pl.BlockSpec((s, LANE), lambda hp, i: (0, hp))
    pair = pl.BlockSpec((2, tq, LANE), lambda hp, i: (hp, i, 0))
    fkt_spec = pl.BlockSpec((nb, N_FGATE, LANE), lambda hp, i: (0, 0, 0))
    return _call_with_riders(
        body, "fox_bwd", (4, nq),
        _attn_specs(s, 12, tq) + [pl.BlockSpec((tq, LANE), lambda hp, i: (i, 0)), fkt_spec, blk, blk, pair],
        [blk, whole, whole, fkt_spec, pl.BlockSpec((1, tq, LANE), lambda hp, i: (hp, i, 0))],
        [jax.ShapeDtypeStruct((s, WIDTH), MXU)] * 3
        + [jax.ShapeDtypeStruct((nb, N_FGATE, LANE), F32), jax.ShapeDtypeStruct((4, s, LANE), F32)],
        [pltpu.VMEM((s, LANE), F32), pltpu.VMEM((s, LANE), F32)], (qkv, qkv, qkv, fc, fkt, do, o, lse), riders, False)


def _fcum_bwd(dfkt, dfq, fl, bf):
    s = fl.shape[0]
    nb = s // LANE

    def body(dfkt_ref, dfq_ref, fl_ref, bf_ref, df_ref, dbf_ref, tail_ref):
        @pl.when(pl.program_id(0) == 0)
        def _():
            tail_ref[...] = jnp.zeros_like(tail_ref)
            dbf_ref[...] = jnp.zeros_like(dbf_ref)

        r = lax.broadcasted_iota(jnp.int32, (LANE, LANE), 0)
        c = lax.broadcasted_iota(jnp.int32, (LANE, LANE), 1)
        tri = (c >= r).astype(F32)
        dfc = jnp.concatenate([dfkt_ref[0], jnp.zeros((LANE - N_FGATE, LANE), F32)], axis=0).T
        dfc = dfc + ((dfq_ref[0] + dfq_ref[1]) + (dfq_ref[2] + dfq_ref[3]))
        dls = jnp.dot(tri, dfc, precision=lax.Precision.HIGHEST, preferred_element_type=F32) + tail_ref[...]
        xb = fl_ref[...] + bf_ref[...]
        e = jnp.exp(-jnp.abs(xb))
        dfl = dls * (jnp.where(xb >= 0.0, e, 1.0) / (1.0 + e))
        df_ref[...] = dfl.astype(df_ref.dtype)
        tail_ref[...] = dls[0:1, :]
        dbf_ref[...] += _colsum(dfl)

    return pl.pallas_call(
        body, name="fcum_bwd", grid=(nb,),
        in_specs=[pl.BlockSpec((1, N_FGATE, LANE), lambda j: (nb - 1 - j, 0, 0)), pl.BlockSpec((4, LANE, LANE), lambda j: (0, nb - 1 - j, 0)),
                  pl.BlockSpec((LANE, LANE), lambda j: (nb - 1 - j, 0)), _fixed(1, LANE)],
        out_specs=[pl.BlockSpec((LANE, LANE), lambda j: (nb - 1 - j, 0)), _fixed(1, LANE)],
        out_shape=[jax.ShapeDtypeStruct((s, LANE), MXU), jax.ShapeDtypeStruct((1, LANE), F32)],
        scratch_shapes=[pltpu.VMEM((1, LANE), F32)],
    )(dfkt, dfq, fl, bf)


def _mix_fwd(x, o_sb, o_fx, gl, w_sb, w_fx, w_o, g1, ln1_g, ln1_b, sh2, sc2):
    s = x.shape[0]
    tm = 256

    def body(x_ref, osb_ref, ofx_ref, gl_ref, wsb_ref, wfx_ref, wo_ref, g1_ref, lg_ref, lb_ref, sh_ref, sc_ref, r1_ref, u2_ref):
        mixin = (_sigmoid(gl_ref[:, :D]) * _dot(osb_ref[...], wsb_ref[...])
                 + _sigmoid(gl_ref[:, D:]) * _dot(ofx_ref[...], wfx_ref[...]))
        r1 = ALPHA * x_ref[...] + g1_ref[...] * _dot(mixin.astype(MXU), wo_ref[...])
        r1_ref[...] = r1
        x1 = _ln(r1)[0] * lg_ref[...] + lb_ref[...]
        u2_ref[...] = (_ln(x1)[0] * (1.0 + sc_ref[...]) + sh_ref[...]).astype(MXU)

    vec = _fixed(1, D)
    return pl.pallas_call(
        body, name="mix_fwd", grid=(s // tm,),
        in_specs=[_rows(tm, D), _rows(tm, WIDTH), _rows(tm, WIDTH), _rows(tm, 2 * D), _res(w_sb), _res(w_fx), _res(w_o),
                  vec, vec, vec, vec, vec],
        out_specs=[_rows(tm, D), _rows(tm, D)],
        out_shape=[jax.ShapeDtypeStruct((s, D), F32), jax.ShapeDtypeStruct((s, D), MXU)],
        compiler_params=_params(VMEM_BIG),
    )(x, o_sb, o_fx, gl, w_sb, w_fx, w_o, g1, ln1_g, ln1_b, sh2, sc2)


def _ffn_fwd(r1, u2, tgt, w_g, w_u, w_d, g2, ln1_g, ln1_b, ln2_g, ln2_b):
    s = r1.shape[0]
    tm = 256

    def body(r1_ref, u2_ref, t_ref, wg_ref, wu_ref, wd_ref, g2_ref, l1g_ref, l1b_ref, l2g_ref, l2b_ref,
             hg_ref, hu_ref, dxa_ref, dh_ref, acc_ref):
        @pl.when(pl.program_id(0) == 0)
        def _():
            acc_ref[...] = jnp.zeros_like(acc_ref)

        u2 = u2_ref[...]
        hg = _dot_nt(u2, wg_ref[...])
        hu = _dot_nt(u2, wu_ref[...])
        hg_ref[...] = hg
        hu_ref[...] = hu
        h = _dot((hg * _sigmoid(hg) * hu).astype(MXU), wd_ref[...])
        x1 = _ln(r1_ref[...])[0] * l1g_ref[...] + l1b_ref[...]
        xh2, rstd2 = _ln(ALPHA * x1 + g2_ref[...] * h)
        err = xh2 * l2g_ref[...] + l2b_ref[...] - t_ref[...]
        dy = err * (1.0 / D)
        dr2 = _ln_bwd(dy * l2g_ref[...], xh2, rstd2)
        dxa_ref[...] = ALPHA * dr2
        dh_ref[...] = (g2_ref[...] * dr2).astype(MXU)
        acc_ref[0:1, :] += _colsum(dr2 * h)
        acc_ref[1:2, :] += _colsum(dy * xh2)
        acc_ref[2:3, :] += _colsum(dy)
        acc_ref[3:4, :] += _colsum(err * err) * (0.5 / D)

    vec = _fixed(1, D)
    return pl.pallas_call(
        body, name="ffn_fwd", grid=(s // tm,),
        in_specs=[_rows(tm, D), _rows(tm, D), _rows(tm, D), _res(w_g), _res(w_u), _res(w_d), vec, vec, vec, vec, vec],
        out_specs=[_rows(tm, D_FF), _rows(tm, D_FF), _rows(tm, D), _rows(tm, D), _fixed(8, D)],
        out_shape=[jax.ShapeDtypeStruct((s, D_FF), F32), jax.ShapeDtypeStruct((s, D_FF), F32),
                   jax.ShapeDtypeStruct((s, D), F32), jax.ShapeDtypeStruct((s, D), MXU), jax.ShapeDtypeStruct((8, D), F32)],
        compiler_params=_params(VMEM_BIG),
    )(r1, u2, tgt, w_g, w_u, w_d, g2, ln1_g, ln1_b, ln2_g, ln2_b)


def _ffn_bwd(dh, hg, hu, w_g, w_u, w_d):
    s = dh.shape[0]
    tm = 256
    half = D_FF // 2

    def body(dh_ref, hg_ref, hu_ref, wg_ref, wu_ref, wd_ref, act_ref, dhg_ref, dhu_ref, du2_ref):
        dh = dh_ref[...]
        du2 = jnp.zeros((tm, D), F32)
        for c0 in (0, half):
            cols = slice(c0, c0 + half)
            dact = _dot_nt(dh, wd_ref[cols, :])
            hg = hg_ref[:, cols]
            hu = hu_ref[:, cols]
            sg = _sigmoid(hg)
            sl = hg * sg
            act_ref[:, cols] = (sl * hu).astype(MXU)
            dhg = (dact * hu * (sg * (1.0 + hg * (1.0 - sg)))).astype(MXU)
            dhu = (dact * sl).astype(MXU)
            dhg_ref[:, cols] = dhg
            dhu_ref[:, cols] = dhu
            du2 = du2 + _dot(dhg, wg_ref[cols, :]) + _dot(dhu, wu_ref[cols, :])
        du2_ref[...] = du2

    return pl.pallas_call(
        body, name="ffn_bwd", grid=(s // tm,),
        in_specs=[_rows(tm, D), _rows(tm, D_FF), _rows(tm, D_FF), _res(w_g), _res(w_u), _res(w_d)],
        out_specs=[_rows(tm, D_FF), _rows(tm, D_FF), _rows(tm, D_FF), _rows(tm, D)],
        out_shape=[jax.ShapeDtypeStruct((s, D_FF), MXU)] * 3 + [jax.ShapeDtypeStruct((s, D), F32)],
        compiler_params=_params(VMEM_BIG),
    )(dh, hg, hu, w_g, w_u, w_d)


def _mix_bwd(du2, dxa, r1, o_sb, o_fx, gl, w_sb, w_fx, w_o, g1, ln1_g, ln1_b, sc2):
    s = r1.shape[0]
    tm = 256
    n_tiles = s // tm

    def body(du2_ref, dxa_ref, r1_ref, osb_ref, ofx_ref, gl_ref, wsb_ref, wfx_ref, wo_ref, g1_ref, lg_ref, lb_ref, sc_ref,
             dx_ref, dosb_ref, dofx_ref, dgl_ref, dbg_ref, acc_ref, dwsb_ref, dwfx_ref, dwo_ref, nsb_ref, nfx_ref, no_ref):
        @pl.when(pl.program_id(0) == 0)
        def _():
            for ref in (acc_ref, dbg_ref, dwsb_ref, dwfx_ref, dwo_ref):
                ref[...] = jnp.zeros_like(ref)

        du2 = du2_ref[...]
        xh1, rstd1 = _ln(r1_ref[...])
        x1 = xh1 * lg_ref[...] + lb_ref[...]
        n1, rstdn = _ln(x1)
        dx1 = dxa_ref[...] + _ln_bwd(du2 * (1.0 + sc_ref[...]), n1, rstdn)
        dr1 = _ln_bwd(dx1 * lg_ref[...], xh1, rstd1)
        dx_ref[...] = ALPHA * dr1
        ysb = _dot(osb_ref[...], wsb_ref[...])
        yfx = _dot(ofx_ref[...], wfx_ref[...])
        gs = _sigmoid(gl_ref[:, :D])
        gf = _sigmoid(gl_ref[:, D:])
        mixin = (gs * ysb + gf * yfx).astype(MXU)
        mix = _dot(mixin, wo_ref[...])
        dmix = (g1_ref[...] * dr1).astype(MXU)
        dmixin = _dot_nt(dmix, wo_ref[...])
        dysb = (dmixin * gs).astype(MXU)
        dyfx = (dmixin * gf).astype(MXU)
        dwo_ref[...] += _dot_tn(mixin, dmix)
        dwsb_ref[...] += _dot_tn(osb_ref[...], dysb)
        dwfx_ref[...] += _dot_tn(ofx_ref[...], dyfx)
        dosb_ref[...] = _dot_nt(dysb, wsb_ref[...]).astype(MXU)
        dofx_ref[...] = _dot_nt(dyfx, wfx_ref[...]).astype(MXU)
        dgs = dmixin * ysb * gs * (1.0 - gs)
        dgf = dmixin * yfx * gf * (1.0 - gf)
        dgl_ref[:, :D] = dgs.astype(MXU)
        dgl_ref[:, D:] = dgf.astype(MXU)
        dbg_ref[:, :D] += _colsum(dgs)
        dbg_ref[:, D:] += _colsum(dgf)
        acc_ref[0:1, :] += _colsum(du2)
        acc_ref[1:2, :] += _colsum(du2 * n1)
        acc_ref[2:3, :] += _colsum(dx1 * xh1)
        acc_ref[3:4, :] += _colsum(dx1)
        acc_ref[4:5, :] += _colsum(dr1 * mix)

        @pl.when(pl.program_id(0) == n_tiles - 1)
        def _():
            nsb_ref[...] = dwsb_ref[...].astype(MXU)
            nfx_ref[...] = dwfx_ref[...].astype(MXU)
            no_ref[...] = dwo_ref[...].astype(MXU)

    vec = _fixed(1, D)
    dw_specs = [_fixed(WIDTH, D), _fixed(WIDTH, D), _fixed(D, D)]
    dw_shapes = [(WIDTH, D), (WIDTH, D), (D, D)]
    return pl.pallas_call(
        body, name="mix_bwd", grid=(n_tiles,),
        in_specs=[_rows(tm, D), _rows(tm, D), _rows(tm, D), _rows(tm, WIDTH), _rows(tm, WIDTH), _rows(tm, 2 * D),
                  _res(w_sb), _res(w_fx), _res(w_o), vec, vec, vec, vec],
        out_specs=[_rows(tm, D), _rows(tm, WIDTH), _rows(tm, WIDTH), _rows(tm, 2 * D), _fixed(1, 2 * D), _fixed(8, D)] + dw_specs * 2,
        out_shape=[jax.ShapeDtypeStruct((s, D), F32)] + [jax.ShapeDtypeStruct((s, WIDTH), MXU)] * 2
        + [jax.ShapeDtypeStruct((s, 2 * D), MXU), jax.ShapeDtypeStruct((1, 2 * D), F32), jax.ShapeDtypeStruct((8, D), F32)]
        + [jax.ShapeDtypeStruct(sh, F32) for sh in dw_shapes] + [jax.ShapeDtypeStruct(sh, MXU) for sh in dw_shapes],
        compiler_params=_params(VMEM_BIG),
    )(du2, dxa, r1, o_sb, o_fx, gl, w_sb, w_fx, w_o, g1, ln1_g, ln1_b, sc2)


def _in_bwd(pieces, x, dxa, w_all, sc1, riders=(), swaps=()):
    s = x.shape[0]
    tm = 256
    n_p = len(pieces)

    def body(*refs):
        p_refs = refs[:n_p]
        x_ref, dxa_ref, w_ref, sc_ref, gx_ref, acc_ref = refs[n_p:]

        @pl.when(pl.program_id(0) == 0)
        def _():
            acc_ref[...] = jnp.zeros_like(acc_ref)

        du1 = jnp.zeros((tm, D), F32)
        for p_ref, (arr, c0) in zip(p_refs, pieces):
            du1 = du1 + _dot(p_ref[...], w_ref[c0:c0 + arr.shape[1], :])
        n0, rstd0 = _ln(x_ref[...])
        gx_ref[...] = dxa_ref[...] + _ln_bwd(du1 * (1.0 + sc_ref[...]), n0, rstd0)
        acc_ref[0:1, :] += _colsum(du1)
        acc_ref[1:2, :] += _colsum(du1 * n0)

    return _call_with_riders(
        body, "in_bwd", (s // tm,),
        [_rows(tm, a.shape[1]) for a, _ in pieces] + [_rows(tm, D), _rows(tm, D), _res(w_all), _fixed(1, D)],
        [_rows(tm, D), _fixed(8, D)], [jax.ShapeDtypeStruct((s, D), F32), jax.ShapeDtypeStruct((8, D), F32)], [],
        (*[a for a, _ in pieces], x, dxa, w_all, sc1), riders, False, VMEM_BIG, swaps)


def _matmul_tn(a, b, name, narrow=False):
    s, m = a.shape
    n = b.shape[1]
    tm = 512 if m % 512 == 0 else (m if m < 512 else m // 2)
    tn = n // 2 if n > 2048 else n
    ts = 2048
    assert m % tm == 0 and tm % LANE == 0 and n % tn == 0 and tn % LANE == 0 and s % ts == 0

    def body(a_ref, b_ref, o_ref, *narrow_ref):
        @pl.when(pl.program_id(2) == 0)
        def _():
            o_ref[...] = jnp.zeros_like(o_ref)

        o_ref[...] += _dot_tn(a_ref[...], b_ref[...])
        if narrow:
            @pl.when(pl.program_id(2) == s // ts - 1)
            def _():
                narrow_ref[0][...] = o_ref[...].astype(MXU)

    out_blk = pl.BlockSpec((tm, tn), lambda i, j, k: (i, j))
    res = pl.pallas_call(
        body, name=name, grid=(m // tm, n // tn, s // ts),
        in_specs=[pl.BlockSpec((ts, tm), lambda i, j, k: (k, i)), pl.BlockSpec((ts, tn), lambda i, j, k: (k, j))],
        out_specs=[out_blk] * (2 if narrow else 1),
        out_shape=[jax.ShapeDtypeStruct((m, n), F32)] + ([jax.ShapeDtypeStruct((m, n), MXU)] if narrow else []),
        compiler_params=_params(VMEM_BIG),
    )(a, b)
    return tuple(res) if narrow else res[0]


def _local_step(x, tgt, ada, w_all, b_gate, bf_pad, late_weights, early_grads, w_in_grads, early_partials,
                ln1_g, ln1_b, ln2_g, ln2_b):
    sh1, sc1, g1, sh2, sc2, g2 = ada
    u1, qkv, fl, gl = _in_proj(x, sh1, sc1, w_all, b_gate)
    fc, fkt = _fcum_fwd(fl, bf_pad)
    late_riders, late_full = late_weights
    n_sb = 3
    (o_sb, rs), gathered_a = _sb_fwd(qkv, late_riders[:n_sb])
    (o_fx, lse), gathered_b = _fox_fwd(qkv, fc, fkt, late_riders[n_sb:])
    w_sb, w_fx, w_o, w_g, w_u, w_d = late_full(list(gathered_a) + list(gathered_b))
    r1, u2 = _mix_fwd(x, o_sb, o_fx, gl, w_sb, w_fx, w_o, g1, ln1_g, ln1_b, sh2, sc2)
    hg, hu, dxa2, dh, acc_f = _ffn_fwd(r1, u2, tgt, w_g, w_u, w_d, g2, ln1_g, ln1_b, ln2_g, ln2_b)
    act, dhg, dhu, du2 = _ffn_bwd(dh, hg, hu, w_g, w_u, w_d)
    dxa1, dosb, dofx, dgl, dbg, acc_m, dw_sb, dw_fx, dw_o, n_sb_out, n_fx_out, n_o = _mix_bwd(
        du2, dxa2, r1, o_sb, o_fx, gl, w_sb, w_fx, w_o, g1, ln1_g, ln1_b, sc2)
    early = dict(w_sb_out=(dw_sb, n_sb_out), w_fox_out=(dw_fx, n_fx_out), w_o=(dw_o, n_o),
                 w_ffn_gate=_matmul_tn(dhg, u2, "dw_ffn_gate", True),
                 w_ffn_up=_matmul_tn(dhu, u2, "dw_ffn_up", True), w_ffn_down=_matmul_tn(act, dh, "dw_ffn_down", True))
    early_riders = early_grads(early)
    (dq_sb, dk_sb, dv_sb), received_a = _sb_bwd(qkv, dosb, rs, early_riders[:n_sb])
    (dq_fx, dk_fx, dv_fx, dfkt, dfq), received_b = _fox_bwd(qkv, fc, fkt, dofx, o_fx, lse, early_riders[n_sb:])
    early_received = list(received_a) + list(received_b)
    df, dbf = _fcum_bwd(dfkt, dfq, fl, bf_pad)
    pieces = [(dq_sb, 0), (dk_sb, WIDTH), (dv_sb, 2 * WIDTH), (dq_fx, 3 * WIDTH), (dk_fx, 4 * WIDTH), (dv_fx, 5 * WIDTH),
              (df, OFF_FGATE), (dgl, OFF_FGATE + LANE)]
    dw_in = [_matmul_tn(p, u1, f"dw_in_{j}") for j, (p, _) in enumerate(pieces)]
    w_in_riders = w_in_grads(dw_in)
    (grad_x, acc_i), exchanged = _in_bwd(pieces, x, dxa1, w_all, sc1, w_in_riders, early_partials(early_received))
    return dict(
        loss_lanes=acc_f[3:4], grad_x=grad_x, dw_in=dw_in, early=early, early_received=early_received,
        w_in_received=exchanged[:len(w_in_riders)], early_swapped=exchanged[len(w_in_riders):],
        d_ada=[acc_i[0:1], acc_i[1:2], acc_m[4:5], acc_m[0:1], acc_m[1:2], acc_f[0:1]],
        dln1_g=acc_m[2:3], dln1_b=acc_m[3:4], dln2_g=acc_f[1:2], dln2_b=acc_f[2:3], db_gate=dbg, db_forget=dbf)


_MESH_ID = pl.DeviceIdType.MESH
_ANY = pl.BlockSpec(memory_space=pl.ANY)
_VMEM = pl.BlockSpec(memory_space=pltpu.VMEM)


def _mesh_pos():
    return lax.axis_index("x"), lax.axis_index("y"), lax.axis_index("c")


def _other_chips(x, y):
    return [(1 - x, y), (x, 1 - y), (1 - x, 1 - y)]


def _allgather_rows(v, name):
    n = v.shape[1]

    def body(v_ref, out_ref, send_sems, recv_sems, local_sem):
        x, y, c = _mesh_pos()
        me = 4 * x + 2 * y + c
        mine = pltpu.make_async_copy(v_ref, out_ref.at[me], local_sem)
        mine.start()
        copies = []
        for d in range(1, 8):
            fx, fy, fc = (d >> 2) & 1, (d >> 1) & 1, d & 1
            to = (1 - x if fx else x, 1 - y if fy else y, 1 - c if fc else c)
            cp = pltpu.make_async_remote_copy(src_ref=v_ref, dst_ref=out_ref.at[me], send_sem=send_sems.at[d - 1],
                                              recv_sem=recv_sems.at[d - 1], device_id=to, device_id_type=_MESH_ID)
            cp.start()
            copies.append(cp)
        for cp in copies:
            cp.wait_recv()
        for cp in copies:
            cp.wait_send()
        mine.wait()

    return pl.pallas_call(
        body, name=name, in_specs=[_VMEM], out_specs=_VMEM,
        out_shape=jax.ShapeDtypeStruct((8, 1, n), v.dtype),
        scratch_shapes=[pltpu.SemaphoreType.DMA((7,)), pltpu.SemaphoreType.DMA((7,)), pltpu.SemaphoreType.DMA(())],
    )(v)


def _allgather_rows_and_swap(v, big, name):
    n = v.shape[1]

    def body(v_ref, big_ref, out_ref, swapped_ref, send_sems, recv_sems, local_sem, swap_send, swap_recv):
        x, y, c = _mesh_pos()
        me = 4 * x + 2 * y + c
        swap = pltpu.make_async_remote_copy(src_ref=big_ref, dst_ref=swapped_ref, send_sem=swap_send, recv_sem=swap_recv,
                                            device_id=(x, y, 1 - c), device_id_type=_MESH_ID)
        swap.start()
        mine = pltpu.make_async_copy(v_ref, out_ref.at[me], local_sem)
        mine.start()
        copies = []
        for d in range(1, 8):
            fx, fy, fc = (d >> 2) & 1, (d >> 1) & 1, d & 1
            to = (1 - x if fx else x, 1 - y if fy else y, 1 - c if fc else c)
            cp = pltpu.make_async_remote_copy(src_ref=v_ref, dst_ref=out_ref.at[me], send_sem=send_sems.at[d - 1],
                                              recv_sem=recv_sems.at[d - 1], device_id=to, device_id_type=_MESH_ID)
            cp.start()
            copies.append(cp)
        for cp in copies:
            cp.wait_recv()
        for cp in copies:
            cp.wait_send()
        mine.wait()
        swap.wait()

    return pl.pallas_call(
        body, name=name, in_specs=[_VMEM, _ANY], out_specs=[_VMEM, _ANY],
        out_shape=[jax.ShapeDtypeStruct((8, 1, n), v.dtype), jax.ShapeDtypeStruct(big.shape, big.dtype)],
        scratch_shapes=[pltpu.SemaphoreType.DMA((7,)), pltpu.SemaphoreType.DMA((7,)), pltpu.SemaphoreType.DMA(()),
                        pltpu.SemaphoreType.DMA(()), pltpu.SemaphoreType.DMA(())],
    )(v, big)


def _chip_exchange(arrays, name, gather):
    nt = len(arrays)

    def body(*refs):
        ins, outs = refs[:nt], refs[nt:2 * nt]
        _exchange_start(ins, outs, refs[2 * nt:], gather)
        _exchange_wait(ins, outs, refs[2 * nt:], gather)

    return pl.pallas_call(
        body, name=name, in_specs=[_ANY] * nt, out_specs=[_ANY] * nt, out_shape=_exchange_out_shape(arrays),
        scratch_shapes=_exchange_sems(nt),
    )(*arrays)


def _exchange_out_shape(arrays):
    return [jax.ShapeDtypeStruct((4,) + a.shape[-2:], a.dtype) for a in arrays]


def _exchange_sems(nt):
    return [pltpu.SemaphoreType.DMA((3 * nt,)), pltpu.SemaphoreType.DMA((3 * nt,)), pltpu.SemaphoreType.DMA((nt,))]


def _exchange_copies(ins, outs, sems, gather):
    send_sems, recv_sems, local_sems = sems
    x, y, c = _mesh_pos()
    me = 2 * x + y
    local, remote = [], []
    for t in range(len(ins)):
        local.append(pltpu.make_async_copy(ins[t] if gather else ins[t].at[me], outs[t].at[me], local_sems.at[t]))
        for j, (px, py) in enumerate(_other_chips(x, y)):
            remote.append(pltpu.make_async_remote_copy(
                src_ref=ins[t] if gather else ins[t].at[2 * px + py], dst_ref=outs[t].at[me], send_sem=send_sems.at[3 * t + j],
                recv_sem=recv_sems.at[3 * t + j], device_id=(px, py, c), device_id_type=_MESH_ID))
    return local, remote


def _exchange_start(ins, outs, sems, gather):
    local, remote = _exchange_copies(ins, outs, sems, gather)
    for cp in local + remote:
        cp.start()


def _exchange_wait(ins, outs, sems, gather):
    local, remote = _exchange_copies(ins, outs, sems, gather)
    for cp in remote:
        cp.wait_recv()
    for cp in remote:
        cp.wait_send()
    for cp in local:
        cp.wait()


def _gather_two_level(shard, name):
    r, n = shard.shape
    half = n // 2
    assert half % LANE == 0

    def body(in_ref, out_ref, ici_send, ici_recv, d2d_send, d2d_recv, local_sem):
        x, y, c = _mesh_pos()
        me = 2 * x + y
        mine = pl.ds(pl.multiple_of(c * half, LANE), half)
        theirs = pl.ds(pl.multiple_of((1 - c) * half, LANE), half)
        local = pltpu.make_async_copy(in_ref, out_ref.at[me], local_sem)
        local.start()
        chips = _other_chips(x, y)
        over_ici = [pltpu.make_async_remote_copy(
            src_ref=in_ref.at[:, mine], dst_ref=out_ref.at[me, :, mine], send_sem=ici_send.at[j], recv_sem=ici_recv.at[j],
            device_id=(px, py, c), device_id_type=_MESH_ID) for j, (px, py) in enumerate(chips)]
        for cp in over_ici:
            cp.start()
        passed_on = [pltpu.make_async_remote_copy(
            src_ref=out_ref.at[2 * px + py, :, mine], dst_ref=out_ref.at[2 * px + py, :, mine], send_sem=d2d_send.at[j],
            recv_sem=d2d_recv.at[j], device_id=(x, y, 1 - c), device_id_type=_MESH_ID) for j, (px, py) in enumerate(chips)]
        for j, (px, py) in enumerate(chips):
            pltpu.make_async_remote_copy(
                src_ref=in_ref.at[:, mine], dst_ref=out_ref.at[2 * px + py, :, mine], send_sem=ici_send.at[j],
                recv_sem=ici_recv.at[j], device_id=(px, py, c), device_id_type=_MESH_ID).wait_recv()
            passed_on[j].start()
        for j, (px, py) in enumerate(chips):
            pltpu.make_async_remote_copy(
                src_ref=out_ref.at[2 * px + py, :, theirs], dst_ref=out_ref.at[2 * px + py, :, theirs], send_sem=d2d_send.at[j],
                recv_sem=d2d_recv.at[j], device_id=(x, y, 1 - c), device_id_type=_MESH_ID).wait_recv()
        for cp in over_ici + passed_on:
            cp.wait_send()
        local.wait()

    sems = pltpu.SemaphoreType.DMA((3,))
    return pl.pallas_call(
        body, name=name, in_specs=[_ANY], out_specs=_ANY, out_shape=jax.ShapeDtypeStruct((4, r, n), shard.dtype),
        scratch_shapes=[sems, sems, sems, sems, pltpu.SemaphoreType.DMA(())],
    )(shard)


def _sibling_exchange(arrays, name):
    nt = len(arrays)

    def body(*refs):
        ins, outs = refs[:nt], refs[nt:2 * nt]
        send_sems, recv_sems = refs[2 * nt:]
        x, y, c = _mesh_pos()
        copies = []
        for t in range(nt):
            cp = pltpu.make_async_remote_copy(src_ref=ins[t], dst_ref=outs[t], send_sem=send_sems.at[t], recv_sem=recv_sems.at[t],
                                              device_id=(x, y, 1 - c), device_id_type=_MESH_ID)
            cp.start()
            copies.append(cp)
        for cp in copies:
            cp.wait_recv()
        for cp in copies:
            cp.wait_send()

    return pl.pallas_call(
        body, name=name, in_specs=[_ANY] * nt, out_specs=[_ANY] * nt,
        out_shape=[jax.ShapeDtypeStruct(a.shape, a.dtype) for a in arrays],
        scratch_shapes=[pltpu.SemaphoreType.DMA((nt,)), pltpu.SemaphoreType.DMA((nt,))],
    )(*arrays)


def _tiles(r, n):
    for tr in (256, 352, 128):
        if r % tr == 0:
            return tr, n, r // tr, lambda i: (i, 0)
    assert n % 256 == 0
    return r, 256, n // 256, lambda i: (0, i)


def _reduce_chips(chip, pieces, recv, name):
    _, r, n = pieces.shape
    tr, tn, steps, at = _tiles(r, n)

    def body(chip_ref, own_ref, recv_ref, out_ref):
        me = chip_ref[0]
        total = jnp.zeros((tr, tn), F32)
        for k in range(4):
            total = total + jnp.where(me == k, own_ref[0], recv_ref[k].astype(F32))
        out_ref[...] = total

    return pl.pallas_call(
        body, name=name,
        grid_spec=pltpu.PrefetchScalarGridSpec(
            num_scalar_prefetch=1, grid=(steps,),
            in_specs=[pl.BlockSpec((1, tr, tn), lambda i, chip_ref: (chip_ref[0],) + at(i)),
                      pl.BlockSpec((4, tr, tn), lambda i, chip_ref: (0,) + at(i))],
            out_specs=pl.BlockSpec((tr, tn), lambda i, chip_ref: at(i))),
        out_shape=jax.ShapeDtypeStruct((r, n), F32),
    )(chip, pieces, recv)


def _adamw_math(w, g, m, v):
    m = ADAM_B1 * m + (1.0 - ADAM_B1) * g
    v = ADAM_B2 * v + (1.0 - ADAM_B2) * (g * g)
    m_hat = m / (1.0 - ADAM_B1 ** ADAM_STEP)
    v_hat = v / (1.0 - ADAM_B2 ** ADAM_STEP)
    return -ADAM_LR * (m_hat / (jnp.sqrt(v_hat) + ADAM_EPS) + ADAM_WD * w), m, v


def _adamw(w, m, v, g_parts, name):
    r, n = w.shape
    tr, tn, steps, at = _tiles(r, n)
    blk = pl.BlockSpec((tr, tn), at)
    ng = len(g_parts)

    def body(*refs):
        w_ref, m_ref, v_ref = refs[:3]
        g_refs = refs[3:3 + ng]
        g_out, d_out, m_out, v_out = refs[3 + ng:]
        g = g_refs[0][...]
        for gr in g_refs[1:]:
            g = g + gr[...]
        g_out[...] = g
        d_out[...], m_out[...], v_out[...] = _adamw_math(w_ref[...], g, m_ref[...], v_ref[...])

    return pl.pallas_call(
        body, name=name, grid=(steps,),
        in_specs=[blk] * (3 + ng), out_specs=[blk] * 4,
        out_shape=[jax.ShapeDtypeStruct((r, n), F32)] * 4,
    )(w, m, v, *g_parts)


def _ada_fwd(c_all, w_shard, b_shard):
    n = w_shard.shape[1]
    tn = 512

    def body(c_ref, w_ref, b_ref, o_ref):
        cv = c_ref[...]
        ca = (cv * _sigmoid(cv)).astype(MXU)
        o_ref[...] = _dot(ca, w_ref[...].astype(MXU)) + b_ref[...]

    return pl.pallas_call(
        body, name="ada_fwd", grid=(n // tn,),
        in_specs=[_fixed(8, D), pl.BlockSpec((D, tn), lambda j: (0, j)), pl.BlockSpec((1, tn), lambda j: (0, j))],
        out_specs=pl.BlockSpec((8, tn), lambda j: (0, j)),
        out_shape=jax.ShapeDtypeStruct((8, n), F32),
    )(c_all, w_shard, b_shard)


def _ada_bwd(c_all, dada_shard):
    n = dada_shard.shape[1]
    tn = 512

    def body(c_ref, d_ref, o_ref):
        cv = c_ref[...]
        ca = (cv * _sigmoid(cv)).astype(MXU)
        o_ref[...] = _dot_tn(ca, d_ref[...].astype(MXU))

    return pl.pallas_call(
        body, name="ada_bwd", grid=(n // tn,),
        in_specs=[_fixed(8, D), pl.BlockSpec((8, tn), lambda j: (0, j))],
        out_specs=pl.BlockSpec((D, tn), lambda j: (0, j)),
        out_shape=jax.ShapeDtypeStruct((D, n), F32),
    )(c_all, dada_shard)


_SMALL = [("d_ada", N_COND * D), ("ln1_g", D), ("ln1_b", D), ("ln2_g", D), ("ln2_b", D), ("b_gate", 2 * D), ("b_forget", LANE),
          ("loss", D)]
_SMALL_OFF = {}
_o = 0
for _n, _w in _SMALL:
    _SMALL_OFF[_n] = (_o, _w)
    _o += _w
_SMALL_LEN = _o
_SMALL_PARAMS = [("b_ada", "d_ada", N_COND * D), ("b_gate", "b_gate", 2 * D), ("b_forget", "b_forget", N_FGATE),
                 ("ln1_g", "ln1_g", D), ("ln1_b", "ln1_b", D), ("ln2_g", "ln2_g", D), ("ln2_b", "ln2_b", D)]


def _small_update(rows, params):
    npar = len(_SMALL_PARAMS)

    def body(*refs):
        rows_ref = refs[0]
        p_refs = refs[1:1 + 3 * npar]
        loss_ref = refs[1 + 3 * npar]
        o_refs = refs[2 + 3 * npar:]
        total = rows_ref[0]
        for d in range(1, 8):
            total = total + rows_ref[d]
        lo, lw = _SMALL_OFF["loss"]
        loss_ref[...] = jnp.sum(total[:, lo:lo + lw], axis=1, keepdims=True)
        for j, (_, key, n) in enumerate(_SMALL_PARAMS):
            off = _SMALL_OFF[key][0]
            g = total[:, off:off + n]
            w_ref, m_ref, v_ref = p_refs[3 * j:3 * j + 3]
            o_refs[4 * j][...] = g
            o_refs[4 * j + 1][...], o_refs[4 * j + 2][...], o_refs[4 * j + 3][...] = _adamw_math(w_ref[...], g, m_ref[...], v_ref[...])

    flat = [a for p in params for a in p]
    out_shape = [jax.ShapeDtypeStruct((1, 1), F32)] + [jax.ShapeDtypeStruct((1, n), F32) for _, _, n in _SMALL_PARAMS for _ in range(4)]
    return pl.pallas_call(body, name="small_update", out_shape=out_shape)(rows, *flat)


_BIG = [("w_in", "cols_t"), ("w_sb_out", "cols"), ("w_fox_out", "cols"), ("w_o", "rows"),
        ("w_ffn_gate", "cols_t"), ("w_ffn_up", "cols_t"), ("w_ffn_down", "rows")]


def _shard2d(a, how):
    return a[0].T if how == "cols_t" else a[0]


def _unshard(g, how):
    if how == "cols":
        return g.transpose(1, 0, 2).reshape(g.shape[1], 4 * g.shape[2])
    return g.reshape(4 * g.shape[1], g.shape[2])


def _reshard(w, how):
    if how == "cols":
        return w.reshape(w.shape[0], 4, w.shape[1] // 4).transpose(1, 0, 2)
    return w.reshape(4, w.shape[0] // 4, w.shape[1])


def kernel(x, c, w_ada, b_ada, w_in, b_gate, b_forget, w_sb_out, w_fox_out, w_o, ln1_g, ln1_b, w_ffn_gate, w_ffn_up, w_ffn_down, ln2_g, ln2_b, loss_target, m_w_ada, m_b_ada, m_w_in, m_b_gate, m_b_forget, m_w_sb_out, m_w_fox_out, m_w_o, m_ln1_g, m_ln1_b, m_w_ffn_gate, m_w_ffn_up, m_w_ffn_down, m_ln2_g, m_ln2_b, v_w_ada, v_b_ada, v_w_in, v_b_gate, v_b_forget, v_w_sb_out, v_w_fox_out, v_w_o, v_ln1_g, v_ln1_b, v_w_ffn_gate, v_w_ffn_up, v_w_ffn_down, v_ln2_g, v_ln2_b):
    given = dict(locals())
    mx, my, mc = _mesh_pos()
    chip = 2 * mx + my
    seq = 4 * mx + 2 * my + mc

    c_all = _allgather_rows(c, "gather_c").reshape(8, D)
    n_ada = w_ada.shape[2]
    b_ada_shard = lax.dynamic_slice(b_ada, (0, chip * n_ada), (1, n_ada))
    ada_part = _ada_fwd(c_all, w_ada[0], b_ada_shard)
    ada_all = _allgather_rows(ada_part.reshape(1, 8 * n_ada), "gather_ada").reshape(4, 2, 8, n_ada)
    ada_row = lax.dynamic_slice(ada_all, (0, mc, seq, 0), (4, 1, 1, n_ada)).reshape(1, N_COND * D)
    ada = [ada_row[:, j * D:(j + 1) * D] for j in range(N_COND)]

    w_in_g = _gather_two_level(_shard2d(w_in, "cols_t").astype(MXU), "gather_w_in")
    wi = _unshard(w_in_g, "cols_t")
    w_all = jnp.concatenate([wi[:OFF_FGATE + N_FGATE], jnp.zeros((LANE - N_FGATE, D), MXU), wi[OFF_FGATE + N_FGATE:]], axis=0)
    bf_pad = jnp.concatenate([b_forget, jnp.zeros((1, LANE - N_FGATE), F32)], axis=1)
    late = _BIG[1:]
    late_riders = [_shard2d(given[n], how).astype(MXU) for n, how in late]
    pieces = {}

    def late_full(gathered):
        return [_unshard(g, how) for (_, how), g in zip(late, gathered)]

    def early_grads(dw):
        for n, how in late:
            pieces[n] = _reshard(dw[n][0], how)
        return [_reshard(dw[n][1], how) for n, how in late]

    def w_in_grads(dwi):
        pieces["w_in"] = _reshard(jnp.concatenate(dwi[:6] + [dwi[6][:N_FGATE], dwi[7]], axis=0), "cols_t")
        return [pieces["w_in"].astype(MXU)]

    chip_arr = jnp.reshape(chip, (1,)).astype(jnp.int32)
    partial = {}

    def early_partials(recv):
        for (n, _), r in zip(late, recv):
            partial[n] = _reduce_chips(chip_arr, pieces[n], r, "reduce_" + n)
        return [partial[n] for n, _ in late]

    out = _local_step(x[0], loss_target[0], ada, w_all, b_gate, bf_pad, (late_riders, late_full), early_grads, w_in_grads,
                      early_partials, ln1_g, ln1_b, ln2_g, ln2_b)

    row = jnp.concatenate(out["d_ada"] + [out["dln1_g"], out["dln1_b"], out["dln2_g"], out["dln2_b"], out["db_gate"],
                                          out["db_forget"], out["loss_lanes"]], axis=1)
    partial["w_in"] = _reduce_chips(chip_arr, pieces["w_in"], out["w_in_received"][0], "reduce_w_in")
    rows, w_in_theirs = _allgather_rows_and_swap(row, partial["w_in"], "gather_small")
    small = _small_update(rows, [(given[p], given["m_" + p], given["v_" + p]) for p, _, _ in _SMALL_PARAMS])
    loss = small[0].reshape(())
    res = {}
    for j, (p, _, _) in enumerate(_SMALL_PARAMS):
        res[p] = small[1 + 4 * j:5 + 4 * j]

    dada_all = rows.reshape(8, _SMALL_LEN)[:, :N_COND * D]
    dada_shard = lax.dynamic_slice(dada_all, (0, chip * n_ada), (8, n_ada))
    g_ada = _ada_bwd(c_all, dada_shard)
    res["w_ada"] = [a[None] for a in _adamw(w_ada[0], m_w_ada[0], v_w_ada[0], [g_ada], "adamw_w_ada")]

    theirs = dict(zip([n for n, _ in late], out["early_swapped"]))
    theirs["w_in"] = w_in_theirs
    for n, how in _BIG:
        upd = _adamw(_shard2d(given[n], how), _shard2d(given["m_" + n], how), _shard2d(given["v_" + n], how),
                     [partial[n], theirs[n]], "adamw_" + n)
        res[n] = [(a.T if how == "cols_t" else a)[None] for a in upd]

    order = ["w_ada", "b_ada", "w_in", "b_gate", "b_forget", "w_sb_out", "w_fox_out", "w_o", "ln1_g", "ln1_b",
             "w_ffn_gate", "w_ffn_up", "w_ffn_down", "ln2_g", "ln2_b"]
    return (loss, out["grad_x"][None], *[res[n][0] for n in order], *[res[n][1] for n in order],
            *[res[n][2] for n in order], *[res[n][3] for n in order])
```
